```python
import math
import jax, jax.numpy as jnp
from jax import lax
import numpy as np

D_MODEL = 1024
BATCH = 8
SEQ = 8192
DEPTH = 4

GRID_W = 64
CHUNK = 128
EPS = 1e-6
MIX_WIDTH = D_MODEL
RET_HEADS = 8
RET_DH = (MIX_WIDTH // 2) // RET_HEADS
RET_WIDTH = RET_HEADS * RET_DH
ROPE_BASE = 10000.0
NA_HEADS = 8
NA_DH = (MIX_WIDTH // 2) // NA_HEADS
NA_WIDTH = NA_HEADS * NA_DH
NA_WIN_R = 8
NA_WIN_C = 16
NA_QBLK = 16
AB_IN_DIM = 4 * RET_WIDTH + 3 * NA_WIDTH
SSD_INNER = 2 * D_MODEL
SSD_HEADDIM = 64
SSD_HEADS = SSD_INNER // SSD_HEADDIM
SSD_GROUPS = 4
SSD_HPG = SSD_HEADS // SSD_GROUPS
SSD_STATE = 128
SSD_CONV = 5
SSD_XBC = SSD_INNER + 2 * SSD_GROUPS * SSD_STATE
SSD_IN_DIM = SSD_INNER + SSD_XBC + 2 * SSD_HEADS
FFN_DIM = 2816
FFN_CONV = 3
N_EVEN = (DEPTH + 1) // 2
N_ODD = DEPTH // 2

kernel_name = 'hybrid_retention_natten_ssd_encoder'


def rms_norm(x, g):
    xf = x.astype(jnp.float32)
    y = xf * lax.rsqrt(jnp.mean(xf * xf, axis=-1, keepdims=True) + EPS)
    return (y * g.astype(jnp.float32)).astype(x.dtype)


def depthwise_conv_centered(x, w, b):
    width, ch = w.shape
    pad = width // 2
    y = lax.conv_general_dilated(x, w[:, None, :].astype(x.dtype), window_strides=(1,),
                                 padding=[(pad, pad)], dimension_numbers=('NWC', 'WIO', 'NWC'),
                                 feature_group_count=ch)
    return y + b.astype(x.dtype)


def rotary(x, pos):
    half = x.shape[-1] // 2
    inv = 1.0 / (ROPE_BASE ** (jnp.arange(half, dtype=jnp.float32) / half))
    ang = pos.astype(jnp.float32)[:, None] * inv[None, :]
    cos = jnp.cos(ang)[None, :, None, :]
    sin = jnp.sin(ang)[None, :, None, :]
    xf = x.astype(jnp.float32)
    x1, x2 = xf[..., :half], xf[..., half:]
    return jnp.concatenate([x1 * cos - x2 * sin, x1 * sin + x2 * cos], axis=-1).astype(x.dtype)


def chunked_scan(q, k, v, log_a, include_diag):
    f32 = jnp.float32
    bsz, s, g, n = q.shape
    hg, p = v.shape[3], v.shape[4]
    n_chunks = s // CHUNK

    def to_chunks(t):
        return jnp.moveaxis(t.astype(f32).reshape(bsz, n_chunks, CHUNK, *t.shape[2:]), 1, 0)

    qc, kc, vc, ac = to_chunks(q), to_chunks(k), to_chunks(v), to_chunks(log_a)
    idx = jnp.arange(CHUNK)
    mask = (idx[:, None] >= idx[None, :]) if include_diag else (idx[:, None] > idx[None, :])

    def step(h, inp):
        qq, kk, vv, aa = inp
        cs = jnp.cumsum(aa, axis=1)
        seg = cs[:, :, None] - cs[:, None, :]
        decay = jnp.exp(jnp.where(mask[None, :, :, None, None], seg, -jnp.inf))
        qk = jnp.einsum('bjgn,blgn->bjlg', qq, kk)
        y_intra = jnp.einsum('bjlgh,blghp->bjghp', qk[..., None] * decay, vv)
        y_inter = jnp.einsum('bjgn,bghnp->bjghp', qq, h) * jnp.exp(cs)[..., None]
        tail = jnp.exp(cs[:, -1:] - cs)
        h_new = h * jnp.exp(cs[:, -1])[..., None, None] + jnp.einsum('blgn,blghp->bghnp', kk, vv * tail[..., None])
        return h_new, y_intra + y_inter

    h0 = jnp.zeros((bsz, g, hg, n, p), f32)
    _, ys = lax.scan(step, h0, (qc, kc, vc, ac))
    return jnp.moveaxis(ys, 0, 1).reshape(bsz, s, g, hg, p)


def bidir_scan(q, k, v_f, v_b, a_f, a_b):
    y_f = chunked_scan(q, k, v_f, a_f, True)
    flip = lambda t: jnp.flip(t, axis=1)
    y_b = flip(chunked_scan(flip(q), flip(k), flip(v_b), flip(a_b), False))
    return y_f + y_b


def neighborhood_attention(q, k, v, rpb):
    f32 = jnp.float32
    bsz, s, h, dh = q.shape
    rows = s // GRID_W
    win_r = min(NA_WIN_R, rows)
    n_cb = GRID_W // NA_QBLK
    span = NA_QBLK + NA_WIN_C
    qb_all = (q.astype(f32) * dh ** -0.5).reshape(bsz, rows, n_cb, NA_QBLK, h, dh)
    qb_all = jnp.moveaxis(qb_all, 2, 0)
    k = k.reshape(bsz, rows, GRID_W, h, dh)
    v = v.reshape(bsz, rows, GRID_W, h, dh)
    r = jnp.arange(rows)
    key_rows = jnp.clip(r - win_r // 2, 0, rows - win_r)[:, None] + jnp.arange(win_r)[None, :]
    dr = key_rows - r[:, None] + (NA_WIN_R - 1)
    c0s = jnp.arange(n_cb) * NA_QBLK

    def block(args):
        qb, c0 = args
        qcols = c0 + jnp.arange(NA_QBLK)
        kcols = jnp.clip(c0 - NA_WIN_C // 2, 0, GRID_W - span) + jnp.arange(span)
        cstart = jnp.clip(qcols - NA_WIN_C // 2, 0, GRID_W - NA_WIN_C)
        valid = (kcols[None, :] >= cstart[:, None]) & (kcols[None, :] < cstart[:, None] + NA_WIN_C)
        dc = jnp.clip(kcols[None, :] - qcols[:, None], -(NA_WIN_C - 1), NA_WIN_C - 1) + (NA_WIN_C - 1)
        kb = k[:, key_rows[:, :, None], kcols[None, None, :]].astype(f32)
        vb = v[:, key_rows[:, :, None], kcols[None, None, :]].astype(f32)
        bias = rpb[:, dr[:, None, :, None], dc[None, :, None, :]].astype(f32)
        sc = jnp.einsum('brqhd,brwchd->bhrqwc', qb, kb) + bias[None]
        sc = jnp.where(valid[:, None, :], sc, -jnp.inf)
        pr = jax.nn.softmax(sc.reshape(*sc.shape[:4], -1), axis=-1).reshape(sc.shape)
        return jnp.einsum('bhrqwc,brwchd->brqhd', pr, vb)

    out = lax.map(block, (qb_all, c0s))
    return jnp.moveaxis(out, 0, 2).reshape(bsz, s, h * dh)


def retention_na_mixer(hn, w_in, ret_decay_logit, ret_gn_g, na_rpb, w_out):
    f32 = jnp.float32
    bsz, s, _ = hn.shape
    proj = hn @ w_in
    R, N = RET_WIDTH, NA_WIDTH
    rq, rk, rv, rg, nq, nk, nv = jnp.split(proj, [R, 2 * R, 3 * R, 4 * R, 4 * R + N, 4 * R + 2 * N], axis=-1)
    pos = jnp.arange(s)
    rshape = (bsz, s, RET_HEADS, RET_DH)
    rq = rotary(rq.reshape(rshape), pos)
    rk = rotary(rk.reshape(rshape), pos) * (RET_DH ** -0.5)
    rv = rv.reshape(bsz, s, RET_HEADS, 1, RET_DH)
    log_gamma = -jax.nn.softplus(-ret_decay_logit.astype(f32))
    a_f = jnp.broadcast_to(log_gamma[0][None, None, :, None], (bsz, s, RET_HEADS, 1))
    a_b = jnp.broadcast_to(log_gamma[1][None, None, :, None], (bsz, s, RET_HEADS, 1))
    y = bidir_scan(rq, rk, rv, rv, a_f, a_b).reshape(rshape)
    mu = jnp.mean(y, axis=-1, keepdims=True)
    var = jnp.mean(jnp.square(y - mu), axis=-1, keepdims=True)
    y = ((y - mu) * lax.rsqrt(var + EPS)).reshape(bsz, s, RET_WIDTH) * ret_gn_g.astype(f32)
    ret_out = (jax.nn.silu(rg.astype(f32)) * y).astype(hn.dtype)
    nshape = (bsz, s, NA_HEADS, NA_DH)
    na_out = neighborhood_attention(nq.reshape(nshape), nk.reshape(nshape), nv.reshape(nshape), na_rpb).astype(hn.dtype)
    return jnp.concatenate([ret_out, na_out], axis=-1) @ w_out


def ssd_mixer(hn, w_in, conv_w, conv_b, dt_bias, a_log, d_skip, norm_g, w_out):
    f32 = jnp.float32
    bsz, s, _ = hn.shape
    proj = hn @ w_in
    z, xbc, dt_raw = jnp.split(proj, [SSD_INNER, SSD_INNER + SSD_XBC], axis=-1)
    xbc = jax.nn.silu(depthwise_conv_centered(xbc, conv_w, conv_b))
    xs, bm, cm = jnp.split(xbc, [SSD_INNER, SSD_INNER + SSD_GROUPS * SSD_STATE], axis=-1)
    dt = jax.nn.softplus(dt_raw.astype(f32).reshape(bsz, s, 2, SSD_HEADS) + dt_bias.astype(f32))
    A = -jnp.exp(a_log.astype(f32))
    log_a = dt * A
    grp = (bsz, s, SSD_GROUPS, SSD_HPG)
    xh = xs.astype(f32).reshape(bsz, s, SSD_GROUPS, SSD_HPG, SSD_HEADDIM)
    v_f = xh * dt[:, :, 0].reshape(grp)[..., None]
    v_b = xh * dt[:, :, 1].reshape(grp)[..., None]
    qc = cm.reshape(bsz, s, SSD_GROUPS, SSD_STATE)
    kb = bm.reshape(bsz, s, SSD_GROUPS, SSD_STATE)
    y = bidir_scan(qc, kb, v_f, v_b, log_a[:, :, 0].reshape(grp), log_a[:, :, 1].reshape(grp))
    y = y + xh * d_skip.astype(f32).reshape(SSD_GROUPS, SSD_HPG)[..., None]
    y = y.reshape(bsz, s, SSD_INNER) * jax.nn.silu(z.astype(f32))
    yg = y.reshape(bsz, s, SSD_GROUPS, SSD_INNER // SSD_GROUPS)
    yg = yg * lax.rsqrt(jnp.mean(yg * yg, axis=-1, keepdims=True) + EPS)
    y = yg.reshape(bsz, s, SSD_INNER) * norm_g.astype(f32)
    return y.astype(hn.dtype) @ w_out


def conv_geglu_ffn(hn, w_up, conv_w, conv_b, w_down):
    u = depthwise_conv_centered(hn @ w_up, conv_w, conv_b)
    gate, val = jnp.split(u, 2, axis=-1)
    return (jax.nn.gelu(gate, approximate=True) * val) @ w_down


def _fwd_setup_inputs(seed: int = 0) -> dict:
    key = jax.random.key(seed)
    ks = jax.random.split(key, 24)
    f32 = jnp.float32

    def nrm(k, shape, scale):
        return jax.random.normal(k, shape, f32) * scale

    def gain(k, shape):
        return 1.0 + 0.05 * jax.random.normal(k, shape, f32)

    x = nrm(ks[0], (BATCH, SEQ, D_MODEL), 1.0)
    norm_mix_pre = gain(ks[1], (DEPTH, D_MODEL))
    norm_mix_post = gain(ks[2], (DEPTH, D_MODEL))
    norm_ffn_pre = gain(ks[3], (DEPTH, D_MODEL))
    norm_ffn_post = gain(ks[4], (DEPTH, D_MODEL))
    ab_w_in = nrm(ks[5], (N_EVEN, D_MODEL, AB_IN_DIM), D_MODEL ** -0.5)
    gamma0 = 1.0 - 2.0 ** (-5.0 - jnp.arange(RET_HEADS, dtype=f32))
    ab_ret_decay_logit = (jnp.log(gamma0) - jnp.log1p(-gamma0))[None, None, :] + nrm(ks[6], (N_EVEN, 2, RET_HEADS), 0.05)
    ab_ret_gn_g = gain(ks[7], (N_EVEN, RET_WIDTH))
    ab_na_rpb = nrm(ks[8], (N_EVEN, NA_HEADS, 2 * NA_WIN_R - 1, 2 * NA_WIN_C - 1), 0.1)
    ab_w_out = nrm(ks[9], (N_EVEN, RET_WIDTH + NA_WIDTH, D_MODEL), (RET_WIDTH + NA_WIDTH) ** -0.5)
    c_w_in = nrm(ks[10], (N_ODD, D_MODEL, SSD_IN_DIM), D_MODEL ** -0.5)
    c_conv_w = nrm(ks[11], (N_ODD, SSD_CONV, SSD_XBC), SSD_CONV ** -0.5)
    c_conv_b = nrm(ks[12], (N_ODD, SSD_XBC), 0.02)
    dt0 = jnp.exp(jax.random.uniform(ks[13], (N_ODD, 2, SSD_HEADS), f32, math.log(1e-3), math.log(1e-1)))
    c_dt_bias = dt0 + jnp.log(-jnp.expm1(-dt0))
    c_a_log = jnp.log(jax.random.uniform(ks[14], (N_ODD, 2, SSD_HEADS), f32, 1.0, 16.0))
    c_d_skip = 1.0 + 0.1 * jax.random.normal(ks[15], (N_ODD, SSD_HEADS), f32)
    c_norm_g = gain(ks[16], (N_ODD, SSD_INNER))
    c_w_out = nrm(ks[17], (N_ODD, SSD_INNER, D_MODEL), SSD_INNER ** -0.5)
    ffn_w_up = nrm(ks[18], (DEPTH, D_MODEL, 2 * FFN_DIM), D_MODEL ** -0.5)
    ffn_conv_w = nrm(ks[19], (DEPTH, FFN_CONV, 2 * FFN_DIM), FFN_CONV ** -0.5)
    ffn_conv_b = nrm(ks[20], (DEPTH, 2 * FFN_DIM), 0.02)
    ffn_w_down = nrm(ks[21], (DEPTH, FFN_DIM, D_MODEL), FFN_DIM ** -0.5)
    return {'x': x, 'norm_mix_pre': norm_mix_pre, 'norm_mix_post': norm_mix_post,
            'norm_ffn_pre': norm_ffn_pre, 'norm_ffn_post': norm_ffn_post,
            'ab_w_in': ab_w_in, 'ab_ret_decay_logit': ab_ret_decay_logit, 'ab_ret_gn_g': ab_ret_gn_g,
            'ab_na_rpb': ab_na_rpb, 'ab_w_out': ab_w_out,
            'c_w_in': c_w_in, 'c_conv_w': c_conv_w, 'c_conv_b': c_conv_b, 'c_dt_bias': c_dt_bias,
            'c_a_log': c_a_log, 'c_d_skip': c_d_skip, 'c_norm_g': c_norm_g, 'c_w_out': c_w_out,
            'ffn_w_up': ffn_w_up, 'ffn_conv_w': ffn_conv_w, 'ffn_conv_b': ffn_conv_b, 'ffn_w_down': ffn_w_down}


def _fwd_reference(x, norm_mix_pre, norm_mix_post, norm_ffn_pre, norm_ffn_post,
              ab_w_in, ab_ret_decay_logit, ab_ret_gn_g, ab_na_rpb, ab_w_out,
              c_w_in, c_conv_w, c_conv_b, c_dt_bias, c_a_log, c_d_skip, c_norm_g, c_w_out,
              ffn_w_up, ffn_conv_w, ffn_conv_b, ffn_w_down):
    for layer in range(DEPTH):
        i = layer // 2
        hn = rms_norm(x, norm_mix_pre[layer])
        if layer % 2 == 0:
            m = retention_na_mixer(hn, ab_w_in[i], ab_ret_decay_logit[i], ab_ret_gn_g[i], ab_na_rpb[i], ab_w_out[i])
        else:
            m = ssd_mixer(hn, c_w_in[i], c_conv_w[i], c_conv_b[i], c_dt_bias[i], c_a_log[i],
                          c_d_skip[i], c_norm_g[i], c_w_out[i])
        x = x + rms_norm(m, norm_mix_post[layer])
        f = conv_geglu_ffn(rms_norm(x, norm_ffn_pre[layer]), ffn_w_up[layer], ffn_conv_w[layer],
                           ffn_conv_b[layer], ffn_w_down[layer])
        x = x + rms_norm(f, norm_ffn_post[layer])
    return x


import jax as _jax
import jax.numpy as _jnp

TWIN_FORMAT = 'train_step'
FWD_PARAMS = ['x', 'norm_mix_pre', 'norm_mix_post', 'norm_ffn_pre', 'norm_ffn_post', 'ab_w_in', 'ab_ret_decay_logit', 'ab_ret_gn_g', 'ab_na_rpb', 'ab_w_out', 'c_w_in', 'c_conv_w', 'c_conv_b', 'c_dt_bias', 'c_a_log', 'c_d_skip', 'c_norm_g', 'c_w_out', 'ffn_w_up', 'ffn_conv_w', 'ffn_conv_b', 'ffn_w_down']
TWIN_WEIGHTS = ['norm_mix_pre', 'norm_mix_post', 'norm_ffn_pre', 'norm_ffn_post', 'ab_w_in', 'ab_ret_decay_logit', 'ab_ret_gn_g', 'ab_na_rpb', 'ab_w_out', 'c_w_in', 'c_conv_w', 'c_conv_b', 'c_dt_bias', 'c_a_log', 'c_d_skip', 'c_norm_g', 'c_w_out', 'ffn_w_up', 'ffn_conv_w', 'ffn_conv_b', 'ffn_w_down']
TWIN_DIFF_INPUT = 'x'
TWIN_INPUTS = ['x', 'norm_mix_pre', 'norm_mix_post', 'norm_ffn_pre', 'norm_ffn_post', 'ab_w_in', 'ab_ret_decay_logit', 'ab_ret_gn_g', 'ab_na_rpb', 'ab_w_out', 'c_w_in', 'c_conv_w', 'c_conv_b', 'c_dt_bias', 'c_a_log', 'c_d_skip', 'c_norm_g', 'c_w_out', 'ffn_w_up', 'ffn_conv_w', 'ffn_conv_b', 'ffn_w_down', 'loss_target', 'm_norm_mix_pre', 'm_norm_mix_post', 'm_norm_ffn_pre', 'm_norm_ffn_post', 'm_ab_w_in', 'm_ab_ret_decay_logit', 'm_ab_ret_gn_g', 'm_ab_na_rpb', 'm_ab_w_out', 'm_c_w_in', 'm_c_conv_w', 'm_c_conv_b', 'm_c_dt_bias', 'm_c_a_log', 'm_c_d_skip', 'm_c_norm_g', 'm_c_w_out', 'm_ffn_w_up', 'm_ffn_conv_w', 'm_ffn_conv_b', 'm_ffn_w_down', 'v_norm_mix_pre', 'v_norm_mix_post', 'v_norm_ffn_pre', 'v_norm_ffn_post', 'v_ab_w_in', 'v_ab_ret_decay_logit', 'v_ab_ret_gn_g', 'v_ab_na_rpb', 'v_ab_w_out', 'v_c_w_in', 'v_c_conv_w', 'v_c_conv_b', 'v_c_dt_bias', 'v_c_a_log', 'v_c_d_skip', 'v_c_norm_g', 'v_c_w_out', 'v_ffn_w_up', 'v_ffn_conv_w', 'v_ffn_conv_b', 'v_ffn_w_down']
TWIN_OUTPUTS = ['loss', 'grad_x', 'grad_norm_mix_pre', 'grad_norm_mix_post', 'grad_norm_ffn_pre', 'grad_norm_ffn_post', 'grad_ab_w_in', 'grad_ab_ret_decay_logit', 'grad_ab_ret_gn_g', 'grad_ab_na_rpb', 'grad_ab_w_out', 'grad_c_w_in', 'grad_c_conv_w', 'grad_c_conv_b', 'grad_c_dt_bias', 'grad_c_a_log', 'grad_c_d_skip', 'grad_c_norm_g', 'grad_c_w_out', 'grad_ffn_w_up', 'grad_ffn_conv_w', 'grad_ffn_conv_b', 'grad_ffn_w_down', 'delta_norm_mix_pre', 'delta_norm_mix_post', 'delta_norm_ffn_pre', 'delta_norm_ffn_post', 'delta_ab_w_in', 'delta_ab_ret_decay_logit', 'delta_ab_ret_gn_g', 'delta_ab_na_rpb', 'delta_ab_w_out', 'delta_c_w_in', 'delta_c_conv_w', 'delta_c_conv_b', 'delta_c_dt_bias', 'delta_c_a_log', 'delta_c_d_skip', 'delta_c_norm_g', 'delta_c_w_out', 'delta_ffn_w_up', 'delta_ffn_conv_w', 'delta_ffn_conv_b', 'delta_ffn_w_down', 'new_m_norm_mix_pre', 'new_m_norm_mix_post', 'new_m_norm_ffn_pre', 'new_m_norm_ffn_post', 'new_m_ab_w_in', 'new_m_ab_ret_decay_logit', 'new_m_ab_ret_gn_g', 'new_m_ab_na_rpb', 'new_m_ab_w_out', 'new_m_c_w_in', 'new_m_c_conv_w', 'new_m_c_conv_b', 'new_m_c_dt_bias', 'new_m_c_a_log', 'new_m_c_d_skip', 'new_m_c_norm_g', 'new_m_c_w_out', 'new_m_ffn_w_up', 'new_m_ffn_conv_w', 'new_m_ffn_conv_b', 'new_m_ffn_w_down', 'new_v_norm_mix_pre', 'new_v_norm_mix_post', 'new_v_norm_ffn_pre', 'new_v_norm_ffn_post', 'new_v_ab_w_in', 'new_v_ab_ret_decay_logit', 'new_v_ab_ret_gn_g', 'new_v_ab_na_rpb', 'new_v_ab_w_out', 'new_v_c_w_in', 'new_v_c_conv_w', 'new_v_c_conv_b', 'new_v_c_dt_bias', 'new_v_c_a_log', 'new_v_c_d_skip', 'new_v_c_norm_g', 'new_v_c_w_out', 'new_v_ffn_w_up', 'new_v_ffn_conv_w', 'new_v_ffn_conv_b', 'new_v_ffn_w_down']
TWIN_LEAF_KINDS = {'loss': 'loss', 'grad_x': 'grad_x', 'grad_norm_mix_pre': 'grad_w', 'grad_norm_mix_post': 'grad_w', 'grad_norm_ffn_pre': 'grad_w', 'grad_norm_ffn_post': 'grad_w', 'grad_ab_w_in': 'grad_w', 'grad_ab_ret_decay_logit': 'grad_w', 'grad_ab_ret_gn_g': 'grad_w', 'grad_ab_na_rpb': 'grad_w', 'grad_ab_w_out': 'grad_w', 'grad_c_w_in': 'grad_w', 'grad_c_conv_w': 'grad_w', 'grad_c_conv_b': 'grad_w', 'grad_c_dt_bias': 'grad_w', 'grad_c_a_log': 'grad_w', 'grad_c_d_skip': 'grad_w', 'grad_c_norm_g': 'grad_w', 'grad_c_w_out': 'grad_w', 'grad_ffn_w_up': 'grad_w', 'grad_ffn_conv_w': 'grad_w', 'grad_ffn_conv_b': 'grad_w', 'grad_ffn_w_down': 'grad_w', 'delta_norm_mix_pre': 'delta_w', 'delta_norm_mix_post': 'delta_w', 'delta_norm_ffn_pre': 'delta_w', 'delta_norm_ffn_post': 'delta_w', 'delta_ab_w_in': 'delta_w', 'delta_ab_ret_decay_logit': 'delta_w', 'delta_ab_ret_gn_g': 'delta_w', 'delta_ab_na_rpb': 'delta_w', 'delta_ab_w_out': 'delta_w', 'delta_c_w_in': 'delta_w', 'delta_c_conv_w': 'delta_w', 'delta_c_conv_b': 'delta_w', 'delta_c_dt_bias': 'delta_w', 'delta_c_a_log': 'delta_w', 'delta_c_d_skip': 'delta_w', 'delta_c_norm_g': 'delta_w', 'delta_c_w_out': 'delta_w', 'delta_ffn_w_up': 'delta_w', 'delta_ffn_conv_w': 'delta_w', 'delta_ffn_conv_b': 'delta_w', 'delta_ffn_w_down': 'delta_w', 'new_m_norm_mix_pre': 'new_m', 'new_m_norm_mix_post': 'new_m', 'new_m_norm_ffn_pre': 'new_m', 'new_m_norm_ffn_post': 'new_m', 'new_m_ab_w_in': 'new_m', 'new_m_ab_ret_decay_logit': 'new_m', 'new_m_ab_ret_gn_g': 'new_m', 'new_m_ab_na_rpb': 'new_m', 'new_m_ab_w_out': 'new_m', 'new_m_c_w_in': 'new_m', 'new_m_c_conv_w': 'new_m', 'new_m_c_conv_b': 'new_m', 'new_m_c_dt_bias': 'new_m', 'new_m_c_a_log': 'new_m', 'new_m_c_d_skip': 'new_m', 'new_m_c_norm_g': 'new_m', 'new_m_c_w_out': 'new_m', 'new_m_ffn_w_up': 'new_m', 'new_m_ffn_conv_w': 'new_m', 'new_m_ffn_conv_b': 'new_m', 'new_m_ffn_w_down': 'new_m', 'new_v_norm_mix_pre': 'new_v', 'new_v_norm_mix_post': 'new_v', 'new_v_norm_ffn_pre': 'new_v', 'new_v_norm_ffn_post': 'new_v', 'new_v_ab_w_in': 'new_v', 'new_v_ab_ret_decay_logit': 'new_v', 'new_v_ab_ret_gn_g': 'new_v', 'new_v_ab_na_rpb': 'new_v', 'new_v_ab_w_out': 'new_v', 'new_v_c_w_in': 'new_v', 'new_v_c_conv_w': 'new_v', 'new_v_c_conv_b': 'new_v', 'new_v_c_dt_bias': 'new_v', 'new_v_c_a_log': 'new_v', 'new_v_c_d_skip': 'new_v', 'new_v_c_norm_g': 'new_v', 'new_v_c_w_out': 'new_v', 'new_v_ffn_w_up': 'new_v', 'new_v_ffn_conv_w': 'new_v', 'new_v_ffn_conv_b': 'new_v', 'new_v_ffn_w_down': 'new_v'}


def _forward(args):
    return _fwd_reference(*[args[k] for k in FWD_PARAMS])


def _output_shape():
    def fwd():
        inp = _fwd_setup_inputs(0)
        return _fwd_reference(*[inp[k] for k in FWD_PARAMS])
    out = _jax.eval_shape(fwd)
    return out.shape, out.dtype

N_MICROBATCH = 1
ADAM_LR = 0.001
ADAM_B1 = 0.9
ADAM_B2 = 0.999
ADAM_EPS = 1e-08
ADAM_WD = 0.01
ADAM_STEP = 10
PER_EXAMPLE_BATCH_AXIS = {'x': 0, 'loss_target': 0}
SHARED_INPUTS = []
_WEIGHT_DTYPES = {'norm_mix_pre': _jnp.float32, 'norm_mix_post': _jnp.float32, 'norm_ffn_pre': _jnp.float32, 'norm_ffn_post': _jnp.float32, 'ab_w_in': _jnp.float32, 'ab_ret_decay_logit': _jnp.float32, 'ab_ret_gn_g': _jnp.float32, 'ab_na_rpb': _jnp.float32, 'ab_w_out': _jnp.float32, 'c_w_in': _jnp.float32, 'c_conv_w': _jnp.float32, 'c_conv_b': _jnp.float32, 'c_dt_bias': _jnp.float32, 'c_a_log': _jnp.float32, 'c_d_skip': _jnp.float32, 'c_norm_g': _jnp.float32, 'c_w_out': _jnp.float32, 'ffn_w_up': _jnp.float32, 'ffn_conv_w': _jnp.float32, 'ffn_conv_b': _jnp.float32, 'ffn_w_down': _jnp.float32}
MOMENT_SCALE = {'norm_mix_pre': 4.196287e+00, 'norm_mix_post': 6.350228e+01, 'norm_ffn_pre': 2.467460e+00, 'norm_ffn_post': 6.367579e+01, 'ab_w_in': 2.901296e+00, 'ab_ret_decay_logit': 1.688530e+01, 'ab_ret_gn_g': 3.779567e+00, 'ab_na_rpb': 4.126890e-01, 'ab_w_out': 3.109718e+00, 'c_w_in': 1.210327e+00, 'c_conv_w': 1.710832e+00, 'c_conv_b': 6.603173e+00, 'c_dt_bias': 2.829489e+00, 'c_a_log': 7.688706e+00, 'c_d_skip': 7.108698e+00, 'c_norm_g': 3.138519e+00, 'c_w_out': 4.680627e+00, 'ffn_w_up': 1.057210e+00, 'ffn_conv_w': 1.109515e+00, 'ffn_conv_b': 4.751359e+00, 'ffn_w_down': 1.964145e+00}


def _to_microbatches(a, axis):
    t = _jnp.moveaxis(a, axis, 0)
    t = t.reshape((N_MICROBATCH, t.shape[0] // N_MICROBATCH) + t.shape[1:])
    return _jnp.moveaxis(t, 1, axis + 1)


def setup_inputs(seed: int = 0) -> dict:
    inp = _fwd_setup_inputs(seed)
    key = _jax.random.fold_in(_jax.random.key(seed), 7919)
    shape, _ = _output_shape()
    out = dict(inp)
    out["loss_target"] = _jax.random.normal(_jax.random.fold_in(key, 0), shape, _jnp.float32)
    for i, name in enumerate(TWIN_WEIGHTS):
        w = inp[name].astype(_jnp.float32)
        if MOMENT_SCALE is None:
            s = _jnp.sqrt(_jnp.mean(_jnp.square(w)) + 1e-30)
        else:
            s = MOMENT_SCALE[name]
        km, kv = _jax.random.split(_jax.random.fold_in(key, i + 1))
        out[name] = w
        out["m_" + name] = s * _jax.random.normal(km, w.shape, _jnp.float32)
        out["v_" + name] = (s * s) * _jax.random.uniform(kv, w.shape, _jnp.float32, 0.5, 1.5)
    if N_MICROBATCH > 1:
        for name, axis in PER_EXAMPLE_BATCH_AXIS.items():
            out[name] = _to_microbatches(out[name], axis)
    return {'x': out['x'], 'norm_mix_pre': out['norm_mix_pre'], 'norm_mix_post': out['norm_mix_post'], 'norm_ffn_pre': out['norm_ffn_pre'], 'norm_ffn_post': out['norm_ffn_post'], 'ab_w_in': out['ab_w_in'], 'ab_ret_decay_logit': out['ab_ret_decay_logit'], 'ab_ret_gn_g': out['ab_ret_gn_g'], 'ab_na_rpb': out['ab_na_rpb'], 'ab_w_out': out['ab_w_out'], 'c_w_in': out['c_w_in'], 'c_conv_w': out['c_conv_w'], 'c_conv_b': out['c_conv_b'], 'c_dt_bias': out['c_dt_bias'], 'c_a_log': out['c_a_log'], 'c_d_skip': out['c_d_skip'], 'c_norm_g': out['c_norm_g'], 'c_w_out': out['c_w_out'], 'ffn_w_up': out['ffn_w_up'], 'ffn_conv_w': out['ffn_conv_w'], 'ffn_conv_b': out['ffn_conv_b'], 'ffn_w_down': out['ffn_w_down'], 'loss_target': out['loss_target'], 'm_norm_mix_pre': out['m_norm_mix_pre'], 'm_norm_mix_post': out['m_norm_mix_post'], 'm_norm_ffn_pre': out['m_norm_ffn_pre'], 'm_norm_ffn_post': out['m_norm_ffn_post'], 'm_ab_w_in': out['m_ab_w_in'], 'm_ab_ret_decay_logit': out['m_ab_ret_decay_logit'], 'm_ab_ret_gn_g': out['m_ab_ret_gn_g'], 'm_ab_na_rpb': out['m_ab_na_rpb'], 'm_ab_w_out': out['m_ab_w_out'], 'm_c_w_in': out['m_c_w_in'], 'm_c_conv_w': out['m_c_conv_w'], 'm_c_conv_b': out['m_c_conv_b'], 'm_c_dt_bias': out['m_c_dt_bias'], 'm_c_a_log': out['m_c_a_log'], 'm_c_d_skip': out['m_c_d_skip'], 'm_c_norm_g': out['m_c_norm_g'], 'm_c_w_out': out['m_c_w_out'], 'm_ffn_w_up': out['m_ffn_w_up'], 'm_ffn_conv_w': out['m_ffn_conv_w'], 'm_ffn_conv_b': out['m_ffn_conv_b'], 'm_ffn_w_down': out['m_ffn_w_down'], 'v_norm_mix_pre': out['v_norm_mix_pre'], 'v_norm_mix_post': out['v_norm_mix_post'], 'v_norm_ffn_pre': out['v_norm_ffn_pre'], 'v_norm_ffn_post': out['v_norm_ffn_post'], 'v_ab_w_in': out['v_ab_w_in'], 'v_ab_ret_decay_logit': out['v_ab_ret_decay_logit'], 'v_ab_ret_gn_g': out['v_ab_ret_gn_g'], 'v_ab_na_rpb': out['v_ab_na_rpb'], 'v_ab_w_out': out['v_ab_w_out'], 'v_c_w_in': out['v_c_w_in'], 'v_c_conv_w': out['v_c_conv_w'], 'v_c_conv_b': out['v_c_conv_b'], 'v_c_dt_bias': out['v_c_dt_bias'], 'v_c_a_log': out['v_c_a_log'], 'v_c_d_skip': out['v_c_d_skip'], 'v_c_norm_g': out['v_c_norm_g'], 'v_c_w_out': out['v_c_w_out'], 'v_ffn_w_up': out['v_ffn_w_up'], 'v_ffn_conv_w': out['v_ffn_conv_w'], 'v_ffn_conv_b': out['v_ffn_conv_b'], 'v_ffn_w_down': out['v_ffn_w_down']}


def _loss(weights, diff, rest, loss_target):
    with _jax.named_scope("forward"):
        args = {**rest, TWIN_DIFF_INPUT: diff, **{k: w.astype(_WEIGHT_DTYPES[k]) for k, w in weights.items()}}
        y = _forward(args)
    with _jax.named_scope("loss_head"):
        err = _jnp.square(y.astype(_jnp.float32) - loss_target)
        return 0.5 * _jnp.sum(_jnp.mean(err, axis=-1)) if err.ndim else 0.5 * err


def _adamw(w, g, m, v):
    m = ADAM_B1 * m + (1.0 - ADAM_B1) * g
    v = ADAM_B2 * v + (1.0 - ADAM_B2) * _jnp.square(g)
    m_hat = m / (1.0 - ADAM_B1 ** ADAM_STEP)
    v_hat = v / (1.0 - ADAM_B2 ** ADAM_STEP)
    delta = -ADAM_LR * (m_hat / (_jnp.sqrt(v_hat) + ADAM_EPS) + ADAM_WD * w)
    return delta, m, v


def reference(x, norm_mix_pre, norm_mix_post, norm_ffn_pre, norm_ffn_post, ab_w_in, ab_ret_decay_logit, ab_ret_gn_g, ab_na_rpb, ab_w_out, c_w_in, c_conv_w, c_conv_b, c_dt_bias, c_a_log, c_d_skip, c_norm_g, c_w_out, ffn_w_up, ffn_conv_w, ffn_conv_b, ffn_w_down, loss_target, m_norm_mix_pre, m_norm_mix_post, m_norm_ffn_pre, m_norm_ffn_post, m_ab_w_in, m_ab_ret_decay_logit, m_ab_ret_gn_g, m_ab_na_rpb, m_ab_w_out, m_c_w_in, m_c_conv_w, m_c_conv_b, m_c_dt_bias, m_c_a_log, m_c_d_skip, m_c_norm_g, m_c_w_out, m_ffn_w_up, m_ffn_conv_w, m_ffn_conv_b, m_ffn_w_down, v_norm_mix_pre, v_norm_mix_post, v_norm_ffn_pre, v_norm_ffn_post, v_ab_w_in, v_ab_ret_decay_logit, v_ab_ret_gn_g, v_ab_na_rpb, v_ab_w_out, v_c_w_in, v_c_conv_w, v_c_conv_b, v_c_dt_bias, v_c_a_log, v_c_d_skip, v_c_norm_g, v_c_w_out, v_ffn_w_up, v_ffn_conv_w, v_ffn_conv_b, v_ffn_w_down):
    given = dict(x=x, norm_mix_pre=norm_mix_pre, norm_mix_post=norm_mix_post, norm_ffn_pre=norm_ffn_pre, norm_ffn_post=norm_ffn_post, ab_w_in=ab_w_in, ab_ret_decay_logit=ab_ret_decay_logit, ab_ret_gn_g=ab_ret_gn_g, ab_na_rpb=ab_na_rpb, ab_w_out=ab_w_out, c_w_in=c_w_in, c_conv_w=c_conv_w, c_conv_b=c_conv_b, c_dt_bias=c_dt_bias, c_a_log=c_a_log, c_d_skip=c_d_skip, c_norm_g=c_norm_g, c_w_out=c_w_out, ffn_w_up=ffn_w_up, ffn_conv_w=ffn_conv_w, ffn_conv_b=ffn_conv_b, ffn_w_down=ffn_w_down, loss_target=loss_target, m_norm_mix_pre=m_norm_mix_pre, m_norm_mix_post=m_norm_mix_post, m_norm_ffn_pre=m_norm_ffn_pre, m_norm_ffn_post=m_norm_ffn_post, m_ab_w_in=m_ab_w_in, m_ab_ret_decay_logit=m_ab_ret_decay_logit, m_ab_ret_gn_g=m_ab_ret_gn_g, m_ab_na_rpb=m_ab_na_rpb, m_ab_w_out=m_ab_w_out, m_c_w_in=m_c_w_in, m_c_conv_w=m_c_conv_w, m_c_conv_b=m_c_conv_b, m_c_dt_bias=m_c_dt_bias, m_c_a_log=m_c_a_log, m_c_d_skip=m_c_d_skip, m_c_norm_g=m_c_norm_g, m_c_w_out=m_c_w_out, m_ffn_w_up=m_ffn_w_up, m_ffn_conv_w=m_ffn_conv_w, m_ffn_conv_b=m_ffn_conv_b, m_ffn_w_down=m_ffn_w_down, v_norm_mix_pre=v_norm_mix_pre, v_norm_mix_post=v_norm_mix_post, v_norm_ffn_pre=v_norm_ffn_pre, v_norm_ffn_post=v_norm_ffn_post, v_ab_w_in=v_ab_w_in, v_ab_ret_decay_logit=v_ab_ret_decay_logit, v_ab_ret_gn_g=v_ab_ret_gn_g, v_ab_na_rpb=v_ab_na_rpb, v_ab_w_out=v_ab_w_out, v_c_w_in=v_c_w_in, v_c_conv_w=v_c_conv_w, v_c_conv_b=v_c_conv_b, v_c_dt_bias=v_c_dt_bias, v_c_a_log=v_c_a_log, v_c_d_skip=v_c_d_skip, v_c_norm_g=v_c_norm_g, v_c_w_out=v_c_w_out, v_ffn_w_up=v_ffn_w_up, v_ffn_conv_w=v_ffn_conv_w, v_ffn_conv_b=v_ffn_conv_b, v_ffn_w_down=v_ffn_w_down)
    weights = {n: given[n] for n in TWIN_WEIGHTS}
    shared = {n: given[n] for n in SHARED_INPUTS}
    per_example = {n: given[n] for n in ['x']}
    grad_fn = _jax.value_and_grad(_loss, argnums=(0, 1))

    def one_microbatch(ex, loss_target):
        ex = dict(ex)
        diff = ex.pop(TWIN_DIFF_INPUT)
        return grad_fn(weights, diff, {**shared, **ex}, loss_target)

    if N_MICROBATCH == 1:
        loss, (grad_w, grad_x) = one_microbatch(per_example, given["loss_target"])
    else:
        def body(carry, xs):
            loss_sum, grad_sum = carry
            l_k, (gw_k, gx_k) = one_microbatch(xs[0], xs[1])
            with _jax.named_scope("update"):
                return (loss_sum + l_k, _jax.tree.map(_jnp.add, grad_sum, gw_k)), gx_k

        init = (_jnp.zeros((), _jnp.float32), _jax.tree.map(_jnp.zeros_like, weights))
        (loss, grad_w), grad_x = _jax.lax.scan(body, init, (per_example, given["loss_target"]))
    with _jax.named_scope("update"):
        delta_w, new_m, new_v = {}, {}, {}
        for n in TWIN_WEIGHTS:
            delta_w[n], new_m[n], new_v[n] = _adamw(weights[n], grad_w[n], given["m_" + n], given["v_" + n])
    return (loss, grad_x, *[grad_w[n] for n in TWIN_WEIGHTS], *[delta_w[n] for n in TWIN_WEIGHTS],
            *[new_m[n] for n in TWIN_WEIGHTS], *[new_v[n] for n in TWIN_WEIGHTS])
```

```python
import functools
import math

import numpy as np
import jax
import jax.numpy as jnp
from jax import lax
from jax.experimental import pallas as pl
from jax.experimental.pallas import tpu as pltpu

f32 = jnp.float32
bf16 = jnp.bfloat16
MXU_DTYPE = bf16

GRID_W = 64
CHUNK = 128
EPS = 1e-6
RET_HEADS = 8
RET_DH = 64
ROPE_BASE = 10000.0
NA_HEADS = 8
NA_DH = 64
NA_WIN_R = 8
NA_WIN_C = 16
SSD_HEADDIM = 64
SSD_GROUPS = 4
SSD_STATE = 128
ADAM_LR = 0.001
ADAM_B1 = 0.9
ADAM_B2 = 0.999
ADAM_EPS = 1e-08
ADAM_WD = 0.01
ADAM_STEP = 10

LANES = 128
PACK_W = 512
PACK_ROWS = 16
VMEM_LIMIT = 56 * 1024 * 1024
N_CHIPS = 4
N_DEV = 8
NEG_INF = -1e30

_DIMS = {"nn": (((1,), (0,)), ((), ())), "nt": (((1,), (1,)), ((), ())), "tn": (((0,), (0,)), ((), ()))}


def _cparams(sem=None):
    return pltpu.CompilerParams(dimension_semantics=sem, vmem_limit_bytes=VMEM_LIMIT)


def _pick(dim, cands):
    for c in cands:
        if dim % c == 0:
            return c
    return dim


def _bdot_raw(a, b, mode):
    return lax.dot_general(a.astype(MXU_DTYPE), b.astype(MXU_DTYPE), _DIMS[mode], preferred_element_type=f32)


@functools.partial(jax.custom_vjp, nondiff_argnums=(2,))
def bdot(a, b, mode):
    return _bdot_raw(a, b, mode)


def _bdot_fwd(a, b, mode):
    return _bdot_raw(a, b, mode), (a, b)


def _bdot_bwd(mode, res, g):
    a, b = res
    if mode == "nn":
        da, db = _bdot_raw(g, b, "nt"), _bdot_raw(a, g, "tn")
    elif mode == "nt":
        da, db = _bdot_raw(g, b, "nn"), _bdot_raw(g, a, "tn")
    else:
        da, db = _bdot_raw(b, g, "nt"), _bdot_raw(a, g, "nn")
    return da.astype(a.dtype), db.astype(b.dtype)


bdot.defvjp(_bdot_fwd, _bdot_bwd)


def _mm_call(a, b, mode, out_dtype):
    if mode == "nn":
        (M, K), (K2, N) = a.shape, b.shape
    elif mode == "nt":
        (M, K), (N, K2) = a.shape, b.shape
    else:
        (K, M), (K2, N) = a.shape, b.shape
    assert K == K2, (a.shape, b.shape, mode)
    tm = _pick(M, (512, 256, 128))
    tn = _pick(N, (512, 256, 128))
    tk = _pick(K, (1024, 512, 256, 128))
    nk = K // tk
    if mode == "nn":
        a_spec = pl.BlockSpec((tm, tk), lambda i, j, k: (i, k))
        b_spec = pl.BlockSpec((tk, tn), lambda i, j, k: (k, j))
    elif mode == "nt":
        a_spec = pl.BlockSpec((tm, tk), lambda i, j, k: (i, k))
        b_spec = pl.BlockSpec((tn, tk), lambda i, j, k: (j, k))
    else:
        a_spec = pl.BlockSpec((tk, tm), lambda i, j, k: (k, i))
        b_spec = pl.BlockSpec((tk, tn), lambda i, j, k: (k, j))

    def body(a_ref, b_ref, o_ref, acc_ref):
        k = pl.program_id(2)

        @pl.when(k == 0)
        def _():
            acc_ref[...] = jnp.zeros_like(acc_ref)

        acc_ref[...] += _bdot_raw(a_ref[...], b_ref[...], mode)

        @pl.when(k == nk - 1)
        def _():
            o_ref[...] = acc_ref[...].astype(o_ref.dtype)

    return pl.pallas_call(
        body,
        out_shape=jax.ShapeDtypeStruct((M, N), out_dtype),
        grid=(M // tm, N // tn, nk),
        in_specs=[a_spec, b_spec],
        out_specs=pl.BlockSpec((tm, tn), lambda i, j, k: (i, j)),
        scratch_shapes=[pltpu.VMEM((tm, tn), f32)],
        compiler_params=_cparams(("parallel", "parallel", "arbitrary")),
        name="mm_" + mode,
    )(a, b)


def mm(a, b, mode="nn", out_dtype=f32):
    @jax.custom_vjp
    def op(a, b):
        return _mm_call(a, b, mode, out_dtype)

    def fwd(a, b):
        return _mm_call(a, b, mode, out_dtype), (a, b)

    def bwd(res, g):
        a, b = res
        if mode == "nn":
            return _mm_call(g, b, "nt", a.dtype), _mm_call(a, g, "tn", b.dtype)
        if mode == "nt":
            return _mm_call(g, b, "nn", a.dtype), _mm_call(g, a, "tn", b.dtype)
        return _mm_call(b, g, "nt", a.dtype), _mm_call(a, g, "nn", b.dtype)

    op.defvjp(fwd, bwd)
    return op(a, b)


def _row_tile(S, row_bytes):
    tm = 512
    while tm > 8 and (tm * row_bytes > (6 << 20) or S % tm):
        tm //= 2
    return tm


def rowwise(fn, name, rows, params, out_dtypes, n_diff_rows=None, n_diff_params=None, ncol=1, bwd_fn=None):
    rows, params = list(rows), list(params)
    nr, npar = len(rows), len(params)
    ndr = nr if n_diff_rows is None else n_diff_rows
    ndp = npar if n_diff_params is None else n_diff_params
    S = rows[0].shape[0]
    rw = [r.shape[1] // ncol for r in rows]
    pshape = [(p.shape[0], p.shape[1] // ncol) for p in params]

    def block_structs(tm):
        return ([jax.ShapeDtypeStruct((tm, w), f32) for w in rw] + [jax.ShapeDtypeStruct(s, f32) for s in pshape])

    outs_s = jax.eval_shape(fn, *block_structs(8))
    ow = [o.shape[1] for o in outs_s]
    nout = len(ow)
    row_bytes = 4 * (sum(rw) * 2 + sum(ow) * 2)
    tm = _row_tile(S, row_bytes)
    grid = (ncol, S // tm)

    def rspec(w):
        return pl.BlockSpec((tm, w), lambda g, i: (i, g))

    def pspec(s):
        return pl.BlockSpec(s, lambda g, i: (0, g))

    def call_fwd(*args):
        def body(*refs):
            vals = [r[...].astype(f32) for r in refs[:nr + npar]]
            res = fn(*vals)
            for o, r in zip(refs[nr + npar:], res):
                o[...] = r.astype(o.dtype)

        return pl.pallas_call(
            body,
            out_shape=[jax.ShapeDtypeStruct((S, w * ncol), dt) for w, dt in zip(ow, out_dtypes)],
            grid=grid,
            in_specs=[rspec(w) for w in rw] + [pspec(s) for s in pshape],
            out_specs=[rspec(w) for w in ow],
            compiler_params=_cparams(("parallel", "parallel")),
            name=name + "_fwd",
        )(*args)

    def call_bwd(args, douts):
        def body(*refs):
            in_refs = refs[:nr + npar]
            do_refs = refs[nr + npar:nr + npar + nout]
            dr_refs = refs[nr + npar + nout:nr + npar + nout + ndr]
            dp_refs = refs[nr + npar + nout + ndr:]
            rv = [r[...] for r in in_refs[:nr]]
            pv = [r[...] for r in in_refs[nr:]]
            dos = [d[...].astype(f32) for d in do_refs]
            if bwd_fn is not None:
                drs, dps = bwd_fn(rv, pv, dos)
            else:
                def f(*a):
                    return fn(*a[:ndr], *rv[ndr:], *a[ndr:], *pv[ndp:])

                _, vjp = jax.vjp(f, *[v.astype(f32) for v in rv[:ndr]], *pv[:ndp])
                cts = vjp(tuple(dos))
                drs, dps = cts[:ndr], cts[ndr:]
            for r, ct in zip(dr_refs, drs):
                r[...] = ct.astype(r.dtype)
            if ndp:
                @pl.when(pl.program_id(1) == 0)
                def _():
                    for r in dp_refs:
                        r[...] = jnp.zeros_like(r)

                for r, ct in zip(dp_refs, dps):
                    r[...] += ct

        return pl.pallas_call(
            body,
            out_shape=[jax.ShapeDtypeStruct(r.shape, r.dtype) for r in rows[:ndr]]
            + [jax.ShapeDtypeStruct(p.shape, f32) for p in params[:ndp]],
            grid=grid,
            in_specs=[rspec(w) for w in rw] + [pspec(s) for s in pshape] + [rspec(w) for w in ow],
            out_specs=[rspec(w) for w in rw[:ndr]] + [pspec(s) for s in pshape[:ndp]],
            compiler_params=_cparams(("parallel", "arbitrary")),
            name=name + "_bwd",
        )(*args, *douts)

    @jax.custom_vjp
    def op(*args):
        return tuple(call_fwd(*args))

    def fwd(*args):
        return tuple(call_fwd(*args)), args

    def bwd(args, douts):
        res = call_bwd(args, douts)
        drs, dps = res[:ndr], res[ndr:]
        out = list(drs) + [jnp.zeros_like(a) for a in args[ndr:nr]]
        out += [dp.astype(p.dtype) for dp, p in zip(dps, args[nr:nr + ndp])]
        out += [jnp.zeros_like(a) for a in args[nr + ndp:]]
        return tuple(out)

    op.defvjp(fwd, bwd)
    return op(*rows, *params)


def _silu(x):
    return x * (1.0 / (1.0 + jnp.exp(-x)))


def _softplus(x):
    return jnp.maximum(x, 0.0) + jnp.log(1.0 + jnp.exp(-jnp.abs(x)))


def _gelu_tanh(x):
    return 0.5 * x * (1.0 + jnp.tanh(math.sqrt(2.0 / math.pi) * (x + 0.044715 * (x * x * x))))


def _rms_fn(x, g):
    return x * lax.rsqrt(jnp.mean(x * x, axis=-1, keepdims=True) + EPS) * g


def rms(x, g, out_dtype):
    return rowwise(lambda x, g: (_rms_fn(x, g),), "rms", [x], [g.reshape(1, -1)], [out_dtype])[0]


def rms_residual(m, g, x):
    return rowwise(lambda m, x, g: (x + _rms_fn(m, g),), "rms_res", [m, x], [g.reshape(1, -1)], [f32])[0]


def silu_op(x):
    return rowwise(lambda x: (_silu(x),), "silu", [x], [], [f32])[0]


def geglu(gate, val):
    return rowwise(lambda g, v: (_gelu_tanh(g) * v,), "geglu", [gate, val], [], [MXU_DTYPE])[0]


def loss_op(y, tgt):
    S, D = y.shape
    tm = _row_tile(S, 4 * D * 4)

    def call_fwd(y, tgt):
        def body(y_ref, t_ref, o_ref):
            @pl.when(pl.program_id(0) == 0)
            def _():
                o_ref[...] = jnp.zeros_like(o_ref)

            e = y_ref[...] - t_ref[...]
            o_ref[...] += 0.5 * jnp.sum(jnp.mean(e * e, axis=-1, keepdims=True))

        out = pl.pallas_call(
            body,
            out_shape=jax.ShapeDtypeStruct((8, LANES), f32),
            grid=(S // tm,),
            in_specs=[pl.BlockSpec((tm, D), lambda i: (i, 0))] * 2,
            out_specs=pl.BlockSpec((8, LANES), lambda i: (0, 0)),
            compiler_params=_cparams(("arbitrary",)),
            name="loss_fwd",
        )(y, tgt)
        return out[0, 0]

    def call_bwd(y, tgt, g):
        def body(y_ref, t_ref, g_ref, o_ref):
            o_ref[...] = (y_ref[...] - t_ref[...]) * (g_ref[...] * (1.0 / D))

        return pl.pallas_call(
            body,
            out_shape=jax.ShapeDtypeStruct((S, D), f32),
            grid=(S // tm,),
            in_specs=[pl.BlockSpec((tm, D), lambda i: (i, 0))] * 2 + [pl.BlockSpec((1, 1), lambda i: (0, 0))],
            out_specs=pl.BlockSpec((tm, D), lambda i: (i, 0)),
            compiler_params=_cparams(("parallel",)),
            name="loss_bwd",
        )(y, tgt, g.reshape(1, 1).astype(f32))

    @jax.custom_vjp
    def op(y, tgt):
        return call_fwd(y, tgt)

    def fwd(y, tgt):
        return call_fwd(y, tgt), (y, tgt)

    def bwd(res, g):
        y, tgt = res
        return call_bwd(y, tgt, g), jnp.zeros_like(tgt)

    op.defvjp(fwd, bwd)
    return op(y, tgt)


HALO = 8


def _conv_tile(S, R):
    def ext(ref, r0):
        cur = ref[pl.ds(r0, R), :]
        prev = ref[pl.ds(pl.multiple_of(jnp.maximum(r0 - HALO, 0), HALO), HALO), :]
        nxt = ref[pl.ds(pl.multiple_of(jnp.minimum(r0 + R, S - HALO), HALO), HALO), :]
        prev = jnp.where(r0 > 0, prev, 0.0)
        nxt = jnp.where(r0 + R < S, nxt, 0.0)
        return jnp.concatenate([prev, cur, nxt], axis=0)

    return ext


def _shift_rows(e, k, R):
    n = e.shape[0]
    if k == 0:
        return e[HALO:HALO + R]
    return pltpu.roll(e, (-k) % n, 0)[HALO:HALO + R]


def dwconv(x, w, b):
    S, C = x.shape
    W = w.shape[0]
    pad = W // 2
    bw = _pick(C, (LANES,))
    R = _pick(S, (256, 128, 64, 32, 16, 8))
    nt = S // R
    ext = _conv_tile(S, R)
    b2 = b.reshape(1, C)

    def call_fwd(x, w, b2):
        def body(x_ref, w_ref, b_ref, y_ref):
            wv = [w_ref[j:j + 1, :] for j in range(W)]
            bv = b_ref[...]

            def tile(i, c):
                r0 = pl.multiple_of(i * R, R)
                e = ext(x_ref, r0)
                acc = bv + wv[pad] * e[HALO:HALO + R]
                for j in range(W):
                    if j != pad:
                        acc = acc + wv[j] * _shift_rows(e, j - pad, R)
                y_ref[pl.ds(r0, R), :] = acc
                return c

            lax.fori_loop(0, nt, tile, 0)

        return pl.pallas_call(
            body,
            out_shape=jax.ShapeDtypeStruct((S, C), f32),
            grid=(C // bw,),
            in_specs=[pl.BlockSpec((S, bw), lambda j: (0, j)), pl.BlockSpec((W, bw), lambda j: (0, j)),
                      pl.BlockSpec((1, bw), lambda j: (0, j))],
            out_specs=pl.BlockSpec((S, bw), lambda j: (0, j)),
            compiler_params=_cparams(("parallel",)),
            name="dwconv_fwd",
        )(x, w, b2)

    def call_bwd(x, w, dy):
        def body(x_ref, w_ref, dy_ref, dx_ref, dw_ref, db_ref):
            wv = [w_ref[j:j + 1, :] for j in range(W)]

            def tile(i, carry):
                dws, db = carry
                r0 = pl.multiple_of(i * R, R)
                ex = ext(x_ref, r0)
                ed = ext(dy_ref, r0)
                d0 = ed[HALO:HALO + R]
                acc = jnp.zeros((R, bw), f32)
                new = []
                for j in range(W):
                    acc = acc + wv[j] * _shift_rows(ed, pad - j, R)
                    new.append(dws[j] + jnp.sum(d0 * _shift_rows(ex, j - pad, R), axis=0, keepdims=True))
                dx_ref[pl.ds(r0, R), :] = acc
                return tuple(new), db + jnp.sum(d0, axis=0, keepdims=True)

            z = jnp.zeros((1, bw), f32)
            dws, db = lax.fori_loop(0, nt, tile, (tuple(z for _ in range(W)), z))
            dw_ref[...] = jnp.zeros_like(dw_ref)
            for j in range(W):
                dw_ref[j:j + 1, :] = dws[j]
            db_ref[...] = db

        return pl.pallas_call(
            body,
            out_shape=[jax.ShapeDtypeStruct((S, C), f32), jax.ShapeDtypeStruct((8, C), f32),
                       jax.ShapeDtypeStruct((1, C), f32)],
            grid=(C // bw,),
            in_specs=[pl.BlockSpec((S, bw), lambda j: (0, j)), pl.BlockSpec((W, bw), lambda j: (0, j)),
                      pl.BlockSpec((S, bw), lambda j: (0, j))],
            out_specs=[pl.BlockSpec((S, bw), lambda j: (0, j)), pl.BlockSpec((8, bw), lambda j: (0, j)),
                       pl.BlockSpec((1, bw), lambda j: (0, j))],
            compiler_params=_cparams(("parallel",)),
            name="dwconv_bwd",
        )(x, w, dy)

    @jax.custom_vjp
    def op(x, w, b2):
        return call_fwd(x, w, b2)

    def fwd(x, w, b2):
        return call_fwd(x, w, b2), (x, w)

    def bwd(res, dy):
        x, w = res
        dx, dw, db = call_bwd(x, w, dy)
        return dx, dw[:W], db

    op.defvjp(fwd, bwd)
    return op(x, w, b2)


def _scan_chunk(q, k, x, a_tok, dt_tok, h, *, rev, incl, nsub):
    L, N = q.shape
    Vw = x.shape[1]
    Hg = a_tok.shape[1]
    hw = Vw // Hg
    t = lax.broadcasted_iota(jnp.int32, (L, L), 0)
    l = lax.broadcasted_iota(jnp.int32, (L, L), 1)
    if rev:
        cm, cmT = l >= t, t >= l
        mask = cm if incl else l > t
    else:
        cm, cmT = l <= t, t <= l
        mask = cm if incl else l < t
    eye = t == l
    lane_a = lax.broadcasted_iota(jnp.int32, (L, Hg), 1)
    vhead = lax.broadcasted_iota(jnp.int32, (1, Vw), 1) // hw
    qhead = lax.broadcasted_iota(jnp.int32, (1, N), 1) // (N // nsub)

    decay, lam_e, tau_e, gam_e, dt_e = [], 0.0, 0.0, 0.0, 0.0
    for i in range(Hg):
        a_col = jnp.sum(jnp.where(lane_a == i, a_tok, 0.0), axis=1, keepdims=True)
        a_row = jnp.sum(jnp.where(eye, a_col, 0.0), axis=0, keepdims=True)
        cs_col = jnp.sum(jnp.where(cm, a_row, 0.0), axis=1, keepdims=True)
        cs_row = jnp.sum(jnp.where(cmT, a_col, 0.0), axis=0, keepdims=True)
        tot = jnp.sum(a_col, axis=0, keepdims=True)
        decay.append(jnp.where(mask, jnp.exp(jnp.where(mask, cs_col - cs_row, 0.0)), 0.0))
        sel = vhead == i
        lam_e = lam_e + jnp.where(sel, jnp.exp(cs_col), 0.0)
        tau_e = tau_e + jnp.where(sel, jnp.exp(tot - cs_col), 0.0)
        gam_e = gam_e + jnp.where(sel, jnp.exp(tot), 0.0)
        if dt_tok is not None:
            dt_col = jnp.sum(jnp.where(lane_a == i, dt_tok, 0.0), axis=1, keepdims=True)
            dt_e = dt_e + jnp.where(sel, dt_col, 0.0)
    v = x if dt_tok is None else x * dt_e
    s_shared = bdot(q, k, "nt") if nsub == 1 else None
    y = lam_e * bdot(q, h, "nn")
    for i in range(Hg):
        s = s_shared if nsub == 1 else bdot(jnp.where(qhead == i, q, 0.0), k, "nt")
        y = y + jnp.where(vhead == i, bdot(s * decay[i], v, "nn"), 0.0)
    hn = gam_e * h + bdot(k, tau_e * v, "tn")
    if nsub > 1:
        nhead = lax.broadcasted_iota(jnp.int32, (N, Vw), 0) // (N // nsub)
        hn = jnp.where(nhead == lax.broadcasted_iota(jnp.int32, (N, Vw), 1) // hw, hn, 0.0)
    return y, hn


def scan_op(q, k, x, a_tok, dt_tok, *, rev, incl, nsub):
    S = q.shape[0]
    G, _, Hg = a_tok.shape
    N = q.shape[1] // G
    Vw = x.shape[1] // G
    L = CHUNK
    nc = S // L
    use_dt = dt_tok is not None
    chunk = functools.partial(_scan_chunk, rev=rev, incl=incl, nsub=nsub)

    def order(c, backward):
        return (nc - 1 - c) if (rev != backward) else c

    def specs(backward):
        qs = pl.BlockSpec((L, N), lambda g, c: (order(c, backward), g))
        xs = pl.BlockSpec((L, Vw), lambda g, c: (order(c, backward), g))
        as_ = pl.BlockSpec((1, L, Hg), lambda g, c: (g, order(c, backward), 0))
        hs = pl.BlockSpec((1, 1, N, Vw), lambda g, c: (g, order(c, backward), 0, 0))
        return qs, xs, as_, hs

    def call_fwd(q, k, x, a_tok, dt_tok):
        qs, xs, as_, hs = specs(False)

        def body(*refs):
            if use_dt:
                q_ref, k_ref, x_ref, a_ref, dt_ref, y_ref, hs_ref, h_scr = refs
            else:
                q_ref, k_ref, x_ref, a_ref, y_ref, hs_ref, h_scr = refs

            @pl.when(pl.program_id(1) == 0)
            def _():
                h_scr[...] = jnp.zeros_like(h_scr)

            h = h_scr[...]
            hs_ref[0, 0] = h
            y, hn = chunk(q_ref[...], k_ref[...], x_ref[...], a_ref[0], dt_ref[0] if use_dt else None, h)
            y_ref[...] = y
            h_scr[...] = hn

        ins = [q, k, x, a_tok] + ([dt_tok] if use_dt else [])
        return pl.pallas_call(
            body,
            out_shape=[jax.ShapeDtypeStruct((S, G * Vw), f32), jax.ShapeDtypeStruct((G, nc, N, Vw), f32)],
            grid=(G, nc),
            in_specs=[qs, qs, xs, as_] + ([as_] if use_dt else []),
            out_specs=[xs, hs],
            scratch_shapes=[pltpu.VMEM((N, Vw), f32)],
            compiler_params=_cparams(("parallel", "arbitrary")),
            name="scan_fwd",
        )(*ins)

    def call_bwd(q, k, x, a_tok, dt_tok, hsave, dy):
        qs, xs, as_, hs = specs(True)

        def body(*refs):
            if use_dt:
                q_ref, k_ref, x_ref, a_ref, dt_ref, hs_ref, dy_ref, dq_ref, dk_ref, dx_ref, da_ref, ddt_ref, dh_scr = refs
            else:
                q_ref, k_ref, x_ref, a_ref, hs_ref, dy_ref, dq_ref, dk_ref, dx_ref, da_ref, dh_scr = refs

            @pl.when(pl.program_id(1) == 0)
            def _():
                dh_scr[...] = jnp.zeros_like(dh_scr)

            prim = [q_ref[...].astype(f32), k_ref[...].astype(f32), x_ref[...], a_ref[0]]
            if use_dt:
                f = lambda q, k, x, a, dt, h: chunk(q, k, x, a, dt, h)
                prim.append(dt_ref[0])
            else:
                f = lambda q, k, x, a, h: chunk(q, k, x, a, None, h)
            prim.append(hs_ref[0, 0])
            _, vjp = jax.vjp(f, *prim)
            cts = vjp((dy_ref[...], dh_scr[...]))
            dq_ref[...] = cts[0].astype(dq_ref.dtype)
            dk_ref[...] = cts[1].astype(dk_ref.dtype)
            dx_ref[...] = cts[2]
            da_ref[0] = cts[3]
            if use_dt:
                ddt_ref[0] = cts[4]
            dh_scr[...] = cts[-1]

        ins = [q, k, x, a_tok] + ([dt_tok] if use_dt else []) + [hsave, dy]
        a_shape = jax.ShapeDtypeStruct(a_tok.shape, f32)
        return pl.pallas_call(
            body,
            out_shape=[jax.ShapeDtypeStruct(q.shape, q.dtype), jax.ShapeDtypeStruct(k.shape, k.dtype),
                       jax.ShapeDtypeStruct(x.shape, f32), a_shape] + ([a_shape] if use_dt else []),
            grid=(G, nc),
            in_specs=[qs, qs, xs, as_] + ([as_] if use_dt else []) + [hs, xs],
            out_specs=[qs, qs, xs, as_] + ([as_] if use_dt else []),
            scratch_shapes=[pltpu.VMEM((N, Vw), f32)],
            compiler_params=_cparams(("parallel", "arbitrary")),
            name="scan_bwd",
        )(*ins)

    if use_dt:
        @jax.custom_vjp
        def op(q, k, x, a_tok, dt_tok):
            return call_fwd(q, k, x, a_tok, dt_tok)[0]

        def fwd(q, k, x, a_tok, dt_tok):
            y, hsave = call_fwd(q, k, x, a_tok, dt_tok)
            return y, (q, k, x, a_tok, dt_tok, hsave)

        def bwd(res, dy):
            q, k, x, a_tok, dt_tok, hsave = res
            return tuple(call_bwd(q, k, x, a_tok, dt_tok, hsave, dy))

        op.defvjp(fwd, bwd)
        return op(q, k, x, a_tok, dt_tok)

    @jax.custom_vjp
    def op(q, k, x, a_tok):
        return call_fwd(q, k, x, a_tok, None)[0]

    def fwd(q, k, x, a_tok):
        y, hsave = call_fwd(q, k, x, a_tok, None)
        return y, (q, k, x, a_tok, hsave)

    def bwd(res, dy):
        q, k, x, a_tok, hsave = res
        return tuple(call_bwd(q, k, x, a_tok, None, hsave, dy))

    op.defvjp(fwd, bwd)
    return op(q, k, x, a_tok)


def _swap_halves(x, dh):
    W = x.shape[1]
    lane = lax.broadcasted_iota(jnp.int32, (1, W), 1) % dh
    return jnp.where(lane < dh // 2, pltpu.roll(x, W - dh // 2, 1), pltpu.roll(x, dh // 2, 1))


def rotary(rq, rk, cos_t, sin_t):
    scale = RET_DH ** -0.5

    def fn(rq, rk, c, s):
        return rq * c + _swap_halves(rq, RET_DH) * s, (rk * c + _swap_halves(rk, RET_DH) * s) * scale

    def bwd_fn(rv, pv, dos):
        _, _, c, s = rv
        dq, dk = dos
        dk = dk * scale
        return (dq * c + _swap_halves(dq * s, RET_DH), dk * c + _swap_halves(dk * s, RET_DH)), ()

    return rowwise(fn, "rotary", [rq, rk, cos_t, sin_t], [], [MXU_DTYPE, MXU_DTYPE], n_diff_rows=2, bwd_fn=bwd_fn)


def _rope_tables(S, width):
    half = RET_DH // 2
    inv = 1.0 / (ROPE_BASE ** (jnp.arange(half, dtype=f32) / half))
    ang = jnp.arange(S, dtype=f32)[:, None] * inv[None, :]
    cos, sin = jnp.cos(ang), jnp.sin(ang)
    reps = width // RET_DH
    return jnp.tile(jnp.concatenate([cos, cos], axis=1), (1, reps)), jnp.tile(jnp.concatenate([-sin, sin], axis=1), (1, reps))


def _exact_dot(x, m):
    return jnp.dot(x, m, precision=lax.Precision.HIGHEST, preferred_element_type=f32)


def ret_post(y_f, y_b, rg, gn_g):
    W = y_f.shape[1]
    idx = np.arange(W) // RET_DH
    avg = jnp.asarray((idx[:, None] == idx[None, :]).astype(np.float32) / RET_DH)

    def fn(yf, yb, rg, g, avg):
        y = yf + yb
        mu = _exact_dot(y, avg)
        d = y - mu
        var = _exact_dot(d * d, avg)
        return (_silu(rg) * (d * lax.rsqrt(var + EPS) * g),)

    return rowwise(fn, "ret_post", [y_f, y_b, rg], [gn_g.reshape(1, -1), avg], [MXU_DTYPE], n_diff_params=1)[0]


def _na_bias(rpb, win_r):
    H = rpb.shape[0]
    qc = np.arange(GRID_W)[:, None]
    kc = np.arange(GRID_W)[None, :]
    cstart = np.clip(qc - NA_WIN_C // 2, 0, GRID_W - NA_WIN_C)
    valid = (kc >= cstart) & (kc < cstart + NA_WIN_C)
    dc = np.clip(kc - qc, -(NA_WIN_C - 1), NA_WIN_C - 1) + (NA_WIN_C - 1)
    onehot = (dc[None] == np.arange(2 * NA_WIN_C - 1)[:, None, None]).astype(np.float32)
    t1 = jnp.einsum("hrd,dqk->hrqk", rpb.astype(f32), jnp.asarray(onehot), precision=lax.Precision.HIGHEST)
    per_delta = [t1[:, NA_WIN_R - 1 - d:NA_WIN_R - 1 - d + win_r] for d in range(win_r)]
    b = jnp.stack(per_delta, axis=1)
    b = jnp.where(jnp.asarray(valid)[None, None, None], b, NEG_INF)
    return jnp.transpose(b, (0, 1, 3, 2, 4)).reshape(H, win_r, GRID_W, win_r * GRID_W)


def _na_row(q, kw, vw, biases):
    lane = lax.broadcasted_iota(jnp.int32, (1, q.shape[1]), 1) // NA_DH
    o = 0.0
    for i, b in enumerate(biases):
        qi = jnp.where(lane == i, q, 0.0) * (NA_DH ** -0.5)
        s = bdot(qi, kw, "nt") + b
        e = jnp.exp(s - jnp.max(s, axis=1, keepdims=True))
        p = e / jnp.sum(e, axis=1, keepdims=True)
        o = o + jnp.where(lane == i, bdot(p, vw, "nn"), 0.0)
    return o


def na_op(nq, nk, nv, bias):
    S, W = nq.shape
    rows = S // GRID_W
    win_r = bias.shape[1]
    nkeys = win_r * GRID_W
    hp = LANES // NA_DH
    npair = W // LANES
    RB = min(16, rows)
    nrb = rows // RB
    qspec = pl.BlockSpec((RB * GRID_W, LANES), lambda p, r: (r, p))
    kspec = pl.BlockSpec((S, LANES), lambda p, r: (0, p))
    bspec = pl.BlockSpec((hp, win_r, GRID_W, nkeys), lambda p, r: (p, 0, 0, 0))

    def window(r):
        r0 = jnp.clip(r - win_r // 2, 0, rows - win_r)
        return pl.multiple_of(r0 * GRID_W, GRID_W), r - r0

    def call_fwd(nq, nk, nv, bias):
        def body(q_ref, k_ref, v_ref, b_ref, o_ref):
            rb = pl.program_id(1)

            def row(i, c):
                k0, d = window(rb * RB + i)
                q0 = pl.multiple_of(i * GRID_W, GRID_W)
                o = _na_row(q_ref[pl.ds(q0, GRID_W), :].astype(f32), k_ref[pl.ds(k0, nkeys), :], v_ref[pl.ds(k0, nkeys), :],
                            [b_ref[h, pl.ds(d, 1)][0] for h in range(hp)])
                o_ref[pl.ds(q0, GRID_W), :] = o.astype(o_ref.dtype)
                return c

            lax.fori_loop(0, RB, row, 0)

        return pl.pallas_call(
            body,
            out_shape=jax.ShapeDtypeStruct((S, W), nq.dtype),
            grid=(npair, nrb),
            in_specs=[qspec, kspec, kspec, bspec],
            out_specs=qspec,
            compiler_params=_cparams(("parallel", "arbitrary")),
            name="na_fwd",
        )(nq, nk, nv, bias)

    def call_bwd(nq, nk, nv, bias, do):
        def body(q_ref, k_ref, v_ref, b_ref, do_ref, dq_ref, dk_ref, dv_ref, db_ref, dk_acc, dv_acc):
            rb = pl.program_id(1)

            @pl.when(rb == 0)
            def _():
                dk_acc[...] = jnp.zeros_like(dk_acc)
                dv_acc[...] = jnp.zeros_like(dv_acc)
                db_ref[...] = jnp.zeros_like(db_ref)

            def row(i, c):
                k0, d = window(rb * RB + i)
                q0 = pl.multiple_of(i * GRID_W, GRID_W)
                bs = [b_ref[h, pl.ds(d, 1)][0] for h in range(hp)]
                _, vjp = jax.vjp(lambda q, kw, vw, *b: _na_row(q, kw, vw, b), q_ref[pl.ds(q0, GRID_W), :].astype(f32),
                                 k_ref[pl.ds(k0, nkeys), :].astype(f32), v_ref[pl.ds(k0, nkeys), :].astype(f32), *bs)
                cts = vjp(do_ref[pl.ds(q0, GRID_W), :].astype(f32))
                dq_ref[pl.ds(q0, GRID_W), :] = cts[0].astype(dq_ref.dtype)
                dk_acc[pl.ds(k0, nkeys), :] += cts[1]
                dv_acc[pl.ds(k0, nkeys), :] += cts[2]
                for h in range(hp):
                    db_ref[h, pl.ds(d, 1)] += cts[3 + h][None]
                return c

            lax.fori_loop(0, RB, row, 0)

            @pl.when(rb == nrb - 1)
            def _():
                dk_ref[...] = dk_acc[...].astype(dk_ref.dtype)
                dv_ref[...] = dv_acc[...].astype(dv_ref.dtype)

        return pl.pallas_call(
            body,
            out_shape=[jax.ShapeDtypeStruct((S, W), nq.dtype), jax.ShapeDtypeStruct((S, W), nk.dtype),
                       jax.ShapeDtypeStruct((S, W), nv.dtype), jax.ShapeDtypeStruct(bias.shape, f32)],
            grid=(npair, nrb),
            in_specs=[qspec, kspec, kspec, bspec, qspec],
            out_specs=[qspec, kspec, kspec, bspec],
            scratch_shapes=[pltpu.VMEM((S, LANES), f32), pltpu.VMEM((S, LANES), f32)],
            compiler_params=_cparams(("parallel", "arbitrary")),
            name="na_bwd",
        )(nq, nk, nv, bias, do)

    @jax.custom_vjp
    def op(nq, nk, nv, bias):
        return call_fwd(nq, nk, nv, bias)

    def fwd(nq, nk, nv, bias):
        return call_fwd(nq, nk, nv, bias), (nq, nk, nv, bias)

    def bwd(res, do):
        return tuple(call_bwd(*res, do))

    op.defvjp(fwd, bwd)
    return op(nq, nk, nv, bias)


def ssd_dt(dt_raw, dt_bias, a_neg):
    def fn(r, b, a):
        dt = _softplus(r + b)
        return dt, dt * a

    return rowwise(fn, "ssd_dt", [dt_raw], [dt_bias, a_neg], [f32, f32])


def ssd_post(y_f, y_b, xs, z, d_skip_lanes, norm_g, groups):
    def fn(yf, yb, xs, z, dsk, g):
        y = (yf + yb + xs * dsk) * _silu(z)
        return (y * lax.rsqrt(jnp.mean(y * y, axis=-1, keepdims=True) + EPS) * g,)

    return rowwise(fn, "ssd_post", [y_f, y_b, xs, z], [d_skip_lanes.reshape(1, -1), norm_g.reshape(1, -1)], [MXU_DTYPE],
                   ncol=groups)[0]


def _heads_major(t, groups):
    S = t.shape[0]
    return jnp.transpose(t.reshape(S, groups, -1), (1, 0, 2))


def retention_na_mixer(hn, w_in, decay_logit, gn_g, rpb, w_out, tables):
    S = hn.shape[0]
    R = RET_HEADS * RET_DH
    NW = NA_HEADS * NA_DH
    cols = lambda a, b: w_in[:, a:b]
    rq, rk, rv, rg = (mm(hn, cols(j * R, (j + 1) * R)) for j in range(4))
    nq, nk, nv = (mm(hn, cols(4 * R + j * NW, 4 * R + (j + 1) * NW), out_dtype=MXU_DTYPE) for j in range(3))
    qr, kr = rotary(rq, rk, *tables)
    log_gamma = -_softplus(-decay_logit.astype(f32))
    pairs = R // LANES
    hp = LANES // RET_DH
    a_f = jnp.broadcast_to(log_gamma[0].reshape(pairs, 1, hp), (pairs, S, hp))
    a_b = jnp.broadcast_to(log_gamma[1].reshape(pairs, 1, hp), (pairs, S, hp))
    y_f = scan_op(qr, kr, rv, a_f, None, rev=False, incl=True, nsub=hp)
    y_b = scan_op(qr, kr, rv, a_b, None, rev=True, incl=False, nsub=hp)
    ret = ret_post(y_f, y_b, rg, gn_g)
    rows = S // GRID_W
    nao = na_op(nq, nk, nv, _na_bias(rpb, min(NA_WIN_R, rows)))
    return mm(ret, w_out[:R]) + mm(nao, w_out[R:])


def ssd_mixer(hn, w_in, conv_w, conv_b, dt_bias, a_log, d_skip, norm_g, w_out):
    heads = d_skip.shape[0]
    inner = heads * SSD_HEADDIM
    gs = SSD_GROUPS * SSD_STATE
    o_x, o_b, o_c, o_dt = inner, 2 * inner, 2 * inner + gs, 2 * inner + 2 * gs
    z = mm(hn, w_in[:, :inner])
    pre = [mm(hn, w_in[:, a:b]) for a, b in ((o_x, o_b), (o_b, o_c), (o_c, o_dt))]
    dt_raw = mm(hn, w_in[:, o_dt:])
    cw = [conv_w[:, a - inner:b - inner] for a, b in ((o_x, o_b), (o_b, o_c), (o_c, o_dt))]
    cb = [conv_b[a - inner:b - inner] for a, b in ((o_x, o_b), (o_b, o_c), (o_c, o_dt))]
    xs, bm, cm = (silu_op(dwconv(p, w, b)) for p, w, b in zip(pre, cw, cb))
    a_neg = -jnp.exp(a_log.astype(f32)).reshape(1, -1)
    dt, la = ssd_dt(dt_raw, dt_bias.astype(f32).reshape(1, -1), a_neg)
    dt_f, dt_b = _heads_major(dt[:, :heads], SSD_GROUPS), _heads_major(dt[:, heads:], SSD_GROUPS)
    la_f, la_b = _heads_major(la[:, :heads], SSD_GROUPS), _heads_major(la[:, heads:], SSD_GROUPS)
    y_f = scan_op(cm, bm, xs, la_f, dt_f, rev=False, incl=True, nsub=1)
    y_b = scan_op(cm, bm, xs, la_b, dt_b, rev=True, incl=False, nsub=1)
    y = ssd_post(y_f, y_b, xs, z, jnp.repeat(d_skip.astype(f32), SSD_HEADDIM), norm_g, SSD_GROUPS)
    return mm(y, w_out)


def conv_geglu_ffn(hf, w_up, conv_w, conv_b, w_down):
    F = w_down.shape[0]
    gate = dwconv(mm(hf, w_up[:, :F]), conv_w[:, :F], conv_b[:F])
    val = dwconv(mm(hf, w_up[:, F:]), conv_w[:, F:], conv_b[F:])
    return mm(geglu(gate, val), w_down)


def model_loss(x, tgt, big, small, rep):
    S = x.shape[0]
    depth = rep["norm_mix_pre"].shape[0]
    tables = _rope_tables(S, RET_HEADS * RET_DH)
    for layer in range(depth):
        i = layer // 2
        hn = rms(x, rep["norm_mix_pre"][layer], MXU_DTYPE)
        if layer % 2 == 0:
            m = retention_na_mixer(hn, big["ab_w_in"][i], rep["ab_ret_decay_logit"][i], rep["ab_ret_gn_g"][i],
                                   rep["ab_na_rpb"][i], big["ab_w_out"][i], tables)
        else:
            m = ssd_mixer(hn, big["c_w_in"][i], small["c_conv_w"][i], small["c_conv_b"][i], rep["c_dt_bias"][i],
                          rep["c_a_log"][i], rep["c_d_skip"][i], small["c_norm_g"][i], big["c_w_out"][i])
        x = rms_residual(m, rep["norm_mix_post"][layer], x)
        hf = rms(x, rep["norm_ffn_pre"][layer], MXU_DTYPE)
        f = conv_geglu_ffn(hf, big["ffn_w_up"][layer], small["ffn_conv_w"][layer], rep["ffn_conv_b"][layer],
                           big["ffn_w_down"][layer])
        x = rms_residual(f, rep["norm_ffn_post"][layer], x)
    return loss_op(x, tgt)


def _mesh_pos():
    return lax.axis_index("x"), lax.axis_index("y"), lax.axis_index("c")


def gather_chips(local):
    R, Wd = local.shape

    def body(x_ref, out_ref, send_sems, recv_sems, local_sem):
        x, y, c = _mesh_pos()
        chips = [(1 - x, y), (x, 1 - y), (1 - x, 1 - y)]
        mine = pltpu.make_async_copy(x_ref, out_ref.at[2 * x + y], local_sem)
        mine.start()

        def copy(k, slot, to):
            return pltpu.make_async_remote_copy(src_ref=x_ref, dst_ref=out_ref.at[slot], send_sem=send_sems.at[k],
                                                recv_sem=recv_sems.at[k], device_id=to, device_id_type=pl.DeviceIdType.MESH)

        sends = [copy(k, 2 * x + y, (cx, cy, c)) for k, (cx, cy) in enumerate(chips)]
        for s in sends:
            s.start()
        for k, (cx, cy) in enumerate(chips):
            copy(k, 2 * cx + cy, (cx, cy, c)).wait_recv()
        for s in sends:
            s.wait_send()
        mine.wait()

    return pl.pallas_call(
        body,
        out_shape=jax.ShapeDtypeStruct((N_CHIPS, R, Wd), local.dtype),
        in_specs=[pl.BlockSpec(memory_space=pl.ANY)],
        out_specs=pl.BlockSpec(memory_space=pl.ANY),
        scratch_shapes=[pltpu.SemaphoreType.DMA((3,)), pltpu.SemaphoreType.DMA((3,)), pltpu.SemaphoreType.DMA],
        name="gather_chips",
    )(local)


def exchange_shards(parts):
    _, R, Wd = parts.shape
    flips = [(fx, fy, fc) for fx in (0, 1) for fy in (0, 1) for fc in (0, 1)][1:]

    def body(p_ref, out_ref, send_sems, recv_sems, local_sem):
        x, y, c = _mesh_pos()
        me = 4 * x + 2 * y + c
        mine = pltpu.make_async_copy(p_ref.at[2 * x + y], out_ref.at[me], local_sem)
        mine.start()
        peers = [(x ^ fx, y ^ fy, c ^ fc) for fx, fy, fc in flips]

        def copy(k, src_chip, slot, to):
            return pltpu.make_async_remote_copy(src_ref=p_ref.at[src_chip], dst_ref=out_ref.at[slot], send_sem=send_sems.at[k],
                                                recv_sem=recv_sems.at[k], device_id=to, device_id_type=pl.DeviceIdType.MESH)

        sends = [copy(k, 2 * px + py, me, (px, py, pc)) for k, (px, py, pc) in enumerate(peers)]
        for s in sends:
            s.start()
        for k, (px, py, pc) in enumerate(peers):
            copy(k, 2 * x + y, 4 * px + 2 * py + pc, (px, py, pc)).wait_recv()
        for s in sends:
            s.wait_send()
        mine.wait()

    return pl.pallas_call(
        body,
        out_shape=jax.ShapeDtypeStruct((N_DEV, R, Wd), parts.dtype),
        in_specs=[pl.BlockSpec(memory_space=pl.ANY)],
        out_specs=pl.BlockSpec(memory_space=pl.ANY),
        scratch_shapes=[pltpu.SemaphoreType.DMA((7,)), pltpu.SemaphoreType.DMA((7,)), pltpu.SemaphoreType.DMA],
        name="exchange_shards",
    )(parts)


def sum_slots(recv):
    n, R, Wd = recv.shape
    tr = _pick(R, (512, 256, 128, 64, 32, 16, 8))

    def body(r_ref, o_ref):
        acc = r_ref[0].astype(f32)
        for j in range(1, n):
            acc = acc + r_ref[j].astype(f32)
        o_ref[...] = acc

    return pl.pallas_call(
        body,
        out_shape=jax.ShapeDtypeStruct((R, Wd), f32),
        grid=(R // tr,),
        in_specs=[pl.BlockSpec((n, tr, Wd), lambda i: (0, i, 0))],
        out_specs=pl.BlockSpec((tr, Wd), lambda i: (i, 0)),
        compiler_params=_cparams(("parallel",)),
        name="sum_slots",
    )(recv)


def adamw(w, g, m, v):
    R, C = w.shape
    tr = R
    for cand in (512, 256, 128, 64, 32, 16, 8):
        if R % cand == 0 and cand * C * 4 <= (1 << 20):
            tr = cand
            break

    def body(w_ref, g_ref, m_ref, v_ref, d_ref, mo_ref, vo_ref):
        g = g_ref[...]
        m = ADAM_B1 * m_ref[...] + (1.0 - ADAM_B1) * g
        v = ADAM_B2 * v_ref[...] + (1.0 - ADAM_B2) * (g * g)
        m_hat = m / (1.0 - ADAM_B1 ** ADAM_STEP)
        v_hat = v / (1.0 - ADAM_B2 ** ADAM_STEP)
        d_ref[...] = -ADAM_LR * (m_hat / (jnp.sqrt(v_hat) + ADAM_EPS) + ADAM_WD * w_ref[...])
        mo_ref[...] = m
        vo_ref[...] = v

    spec = pl.BlockSpec((tr, C), lambda i: (i, 0))
    return pl.pallas_call(
        body,
        out_shape=[jax.ShapeDtypeStruct((R, C), f32)] * 3,
        grid=(R // tr,),
        in_specs=[spec] * 4,
        out_specs=[spec] * 3,
        compiler_params=_cparams(("parallel",)),
        name="adamw",
    )(w, g, m, v)


def _pack(arrs, dtype):
    flat = jnp.concatenate([a.astype(dtype).reshape(-1) for a in arrs])
    n = flat.shape[0]
    unit = PACK_W * PACK_ROWS
    padded = -(-n // unit) * unit
    return jnp.pad(flat, (0, padded - n)).reshape(-1, PACK_W)


def _unpack(buf, shapes):
    flat = buf.reshape(-1)
    out, off = [], 0
    for s in shapes:
        n = int(np.prod(s))
        out.append(flat[off:off + n].reshape(s))
        off += n
    return out


BIG = (("ab_w_in", 2), ("ab_w_out", 1), ("c_w_in", 2), ("c_w_out", 1), ("ffn_w_up", 2), ("ffn_w_down", 1))
SMALL = (("c_conv_w", 2), ("c_conv_b", 1), ("c_norm_g", 1), ("ffn_conv_w", 2))
REP = ("norm_mix_pre", "norm_mix_post", "norm_ffn_pre", "norm_ffn_post", "ab_ret_decay_logit", "ab_ret_gn_g", "ab_na_rpb",
       "c_dt_bias", "c_a_log", "c_d_skip", "ffn_conv_b")
WEIGHTS = ("norm_mix_pre", "norm_mix_post", "norm_ffn_pre", "norm_ffn_post", "ab_w_in", "ab_ret_decay_logit", "ab_ret_gn_g",
           "ab_na_rpb", "ab_w_out", "c_w_in", "c_conv_w", "c_conv_b", "c_dt_bias", "c_a_log", "c_d_skip", "c_norm_g", "c_w_out",
           "ffn_w_up", "ffn_conv_w", "ffn_conv_b", "ffn_w_down")


def _gather_set(local, spec, dtype):
    shapes = [local[n].shape for n, _ in spec]
    got = gather_chips(_pack([local[n] for n, _ in spec], dtype))
    per_chip = [_unpack(got[s], shapes) for s in range(N_CHIPS)]
    return {n: jnp.concatenate([per_chip[s][j] for s in range(N_CHIPS)], axis=ax) for j, (n, ax) in enumerate(spec)}


def _scatter_parts(full, spec, extra, dtype):
    split = {n: jnp.split(full[n], N_CHIPS, axis=ax) for n, ax in spec}
    return jnp.stack([_pack([split[n][s] for n, _ in spec] + list(extra), dtype) for s in range(N_CHIPS)])


def kernel(x, norm_mix_pre, norm_mix_post, norm_ffn_pre, norm_ffn_post, ab_w_in, ab_ret_decay_logit, ab_ret_gn_g, ab_na_rpb, ab_w_out, c_w_in, c_conv_w, c_conv_b, c_dt_bias, c_a_log, c_d_skip, c_norm_g, c_w_out, ffn_w_up, ffn_conv_w, ffn_conv_b, ffn_w_down, loss_target, m_norm_mix_pre, m_norm_mix_post, m_norm_ffn_pre, m_norm_ffn_post, m_ab_w_in, m_ab_ret_decay_logit, m_ab_ret_gn_g, m_ab_na_rpb, m_ab_w_out, m_c_w_in, m_c_conv_w, m_c_conv_b, m_c_dt_bias, m_c_a_log, m_c_d_skip, m_c_norm_g, m_c_w_out, m_ffn_w_up, m_ffn_conv_w, m_ffn_conv_b, m_ffn_w_down, v_norm_mix_pre, v_norm_mix_post, v_norm_ffn_pre, v_norm_ffn_post, v_ab_w_in, v_ab_ret_decay_logit, v_ab_ret_gn_g, v_ab_na_rpb, v_ab_w_out, v_c_w_in, v_c_conv_w, v_c_conv_b, v_c_dt_bias, v_c_a_log, v_c_d_skip, v_c_norm_g, v_c_w_out, v_ffn_w_up, v_ffn_conv_w, v_ffn_conv_b, v_ffn_w_down):
    args = dict(locals())
    w = {n: args[n] for n in WEIGHTS}
    mom = {n: args["m_" + n] for n in WEIGHTS}
    var = {n: args["v_" + n] for n in WEIGHTS}

    big = _gather_set(w, BIG, MXU_DTYPE)
    small = _gather_set(w, SMALL, f32)
    rep = {n: w[n] for n in REP}

    def loss_fn(xs, big, small, rep):
        return model_loss(xs, loss_target[0], big, small, rep)

    loss, (gx, gbig, gsmall, grep) = jax.value_and_grad(loss_fn, argnums=(0, 1, 2, 3))(x[0], big, small, rep)
    loss = lax.psum(loss, ("x", "y", "c"))

    big_shapes = [w[n].shape for n, _ in BIG]
    small_shapes = [w[n].shape for n, _ in SMALL] + [w[n].shape for n in REP]
    g_big = _unpack(sum_slots(exchange_shards(_scatter_parts(gbig, BIG, (), MXU_DTYPE))), big_shapes)
    g_small_buf = sum_slots(exchange_shards(_scatter_parts(gsmall, SMALL, [grep[n] for n in REP], f32)))
    grads = dict(zip([n for n, _ in BIG], g_big))
    small_names = [n for n, _ in SMALL] + list(REP)
    grads.update(zip(small_names, _unpack(g_small_buf, small_shapes)))

    delta, new_m, new_v = {}, {}, {}
    for n, _ in BIG:
        shp = w[n].shape
        two_d = lambda a: a.reshape(-1, shp[-1])
        d, m2, v2 = adamw(two_d(w[n]), two_d(grads[n]), two_d(mom[n]), two_d(var[n]))
        delta[n], new_m[n], new_v[n] = d.reshape(shp), m2.reshape(shp), v2.reshape(shp)
    pk = lambda src: _pack([src[n] for n in small_names], f32)
    d, m2, v2 = adamw(pk(w), g_small_buf, pk(mom), pk(var))
    for dst, buf in ((delta, d), (new_m, m2), (new_v, v2)):
        dst.update(zip(small_names, _unpack(buf, small_shapes)))

    return (loss, gx[None], *[grads[n] for n in WEIGHTS], *[delta[n] for n in WEIGHTS],
            *[new_m[n] for n in WEIGHTS], *[new_v[n] for n in WEIGHTS])
```

```python
import functools
import math

import numpy as np
import jax
import jax.numpy as jnp
from jax import lax
from jax.experimental import pallas as pl
from jax.experimental.pallas import tpu as pltpu

f32 = jnp.float32
bf16 = jnp.bfloat16
MXU_DTYPE = bf16

GRID_W = 64
CHUNK = 128
EPS = 1e-6
RET_HEADS = 8
RET_DH = 64
ROPE_BASE = 10000.0
NA_HEADS = 8
NA_DH = 64
NA_WIN_R = 8
NA_WIN_C = 16
SSD_HEADDIM = 64
SSD_GROUPS = 4
SSD_STATE = 128
ADAM_LR = 0.001
ADAM_B1 = 0.9
ADAM_B2 = 0.999
ADAM_EPS = 1e-08
ADAM_WD = 0.01
ADAM_STEP = 10

LANES = 128
PACK_W = 512
PACK_ROWS = 32
VMEM_LIMIT = 56 * 1024 * 1024
MM_BLOCK_BYTES = 6 * 1024 * 1024
N_CHIPS = 4
N_DEV = 8
NEG_INF = -1e30

_DIMS = {"nn": (((1,), (0,)), ((), ())), "nt": (((1,), (1,)), ((), ())), "tn": (((0,), (0,)), ((), ()))}


def _cparams(sem=None):
    return pltpu.CompilerParams(dimension_semantics=sem, vmem_limit_bytes=VMEM_LIMIT)


def _pick(dim, cands):
    for c in cands:
        if dim % c == 0:
            return c
    return dim


def _divisor_tile(dim, fits, align):
    for d in range(1, dim + 1):
        t = dim // d
        if dim % d == 0 and t % align == 0 and fits(t):
            return t
    return dim


def _bdot_raw(a, b, mode):
    return lax.dot_general(a.astype(MXU_DTYPE), b.astype(MXU_DTYPE), _DIMS[mode], preferred_element_type=f32)


@functools.partial(jax.custom_vjp, nondiff_argnums=(2,))
def bdot(a, b, mode):
    return _bdot_raw(a, b, mode)


def _bdot_fwd(a, b, mode):
    return _bdot_raw(a, b, mode), (a, b)


def _bdot_bwd(mode, res, g):
    a, b = res
    if mode == "nn":
        da, db = _bdot_raw(g, b, "nt"), _bdot_raw(a, g, "tn")
    elif mode == "nt":
        da, db = _bdot_raw(g, b, "nn"), _bdot_raw(g, a, "tn")
    else:
        da, db = _bdot_raw(b, g, "nt"), _bdot_raw(a, g, "nn")
    return da.astype(a.dtype), db.astype(b.dtype)


bdot.defvjp(_bdot_fwd, _bdot_bwd)


def _mm_call(a, b, mode, out_dtype):
    if mode == "nn":
        (M, K), (K2, N) = a.shape, b.shape
    elif mode == "nt":
        (M, K), (N, K2) = a.shape, b.shape
    else:
        (K, M), (K2, N) = a.shape, b.shape
    assert K == K2, (a.shape, b.shape, mode)
    a_bytes, b_bytes, o_bytes = a.dtype.itemsize, b.dtype.itemsize, jnp.dtype(out_dtype).itemsize
    if mode == "tn":
        tk = _pick(K, (512, 256, 128))
        tn = _divisor_tile(N, lambda t: t <= 1536, LANES)
        tm = _divisor_tile(M, lambda t: t * tn * 4 <= MM_BLOCK_BYTES, 8)
    else:
        tk, tn = K, N
        tm = _divisor_tile(M, lambda t: t * K * a_bytes <= MM_BLOCK_BYTES and t * N * o_bytes <= MM_BLOCK_BYTES, 8)
    nk = K // tk
    if mode == "nn":
        a_spec = pl.BlockSpec((tm, tk), lambda i, j, k: (i, k))
        b_spec = pl.BlockSpec((tk, tn), lambda i, j, k: (k, j))
    elif mode == "nt":
        a_spec = pl.BlockSpec((tm, tk), lambda i, j, k: (i, k))
        b_spec = pl.BlockSpec((tn, tk), lambda i, j, k: (j, k))
    else:
        a_spec = pl.BlockSpec((tk, tm), lambda i, j, k: (k, i))
        b_spec = pl.BlockSpec((tk, tn), lambda i, j, k: (k, j))

    if nk == 1:
        def body(a_ref, b_ref, o_ref):
            o_ref[...] = _bdot_raw(a_ref[...], b_ref[...], mode).astype(o_ref.dtype)
    else:
        def body(a_ref, b_ref, o_ref, acc_ref):
            k = pl.program_id(2)

            @pl.when(k == 0)
            def _():
                acc_ref[...] = jnp.zeros_like(acc_ref)

            acc_ref[...] += _bdot_raw(a_ref[...], b_ref[...], mode)

            @pl.when(k == nk - 1)
            def _():
                o_ref[...] = acc_ref[...].astype(o_ref.dtype)

    return pl.pallas_call(
        body,
        out_shape=jax.ShapeDtypeStruct((M, N), out_dtype),
        grid=(M // tm, N // tn, nk),
        in_specs=[a_spec, b_spec],
        out_specs=pl.BlockSpec((tm, tn), lambda i, j, k: (i, j)),
        scratch_shapes=[pltpu.VMEM((tm, tn), f32)] if nk > 1 else [],
        compiler_params=_cparams(("parallel", "parallel", "arbitrary")),
        name="mm_" + mode,
    )(a, b)


def mm(a, b, mode="nn", out_dtype=f32):
    @jax.custom_vjp
    def op(a, b):
        return _mm_call(a, b, mode, out_dtype)

    def fwd(a, b):
        return _mm_call(a, b, mode, out_dtype), (a, b)

    def bwd(res, g):
        a, b = res
        if mode == "nn":
            return _mm_call(g, b, "nt", a.dtype), _mm_call(a, g, "tn", b.dtype)
        if mode == "nt":
            return _mm_call(g, b, "nn", a.dtype), _mm_call(g, a, "tn", b.dtype)
        return _mm_call(b, g, "nt", a.dtype), _mm_call(a, g, "nn", b.dtype)

    op.defvjp(fwd, bwd)
    return op(a, b)


def _row_tile(S, row_bytes):
    tm = 512
    while tm > 8 and (tm * row_bytes > (6 << 20) or S % tm):
        tm //= 2
    return tm


def rowwise(fn, name, rows, params, out_dtypes, n_diff_rows=None, n_diff_params=None, ncol=1, bwd_fn=None):
    rows, params = list(rows), list(params)
    nr, npar = len(rows), len(params)
    ndr = nr if n_diff_rows is None else n_diff_rows
    ndp = npar if n_diff_params is None else n_diff_params
    S = rows[0].shape[0]
    rw = [r.shape[1] // ncol for r in rows]
    pshape = [(p.shape[0], p.shape[1] // ncol) for p in params]

    def block_structs(tm):
        return ([jax.ShapeDtypeStruct((tm, w), f32) for w in rw] + [jax.ShapeDtypeStruct(s, f32) for s in pshape])

    outs_s = jax.eval_shape(fn, *block_structs(8))
    ow = [o.shape[1] for o in outs_s]
    nout = len(ow)
    row_bytes = 4 * (sum(rw) * 2 + sum(ow) * 2)
    tm = _row_tile(S, row_bytes)
    grid = (ncol, S // tm)

    def rspec(w):
        return pl.BlockSpec((tm, w), lambda g, i: (i, g))

    def pspec(s):
        return pl.BlockSpec(s, lambda g, i: (0, g))

    def call_fwd(*args):
        def body(*refs):
            vals = [r[...].astype(f32) for r in refs[:nr + npar]]
            res = fn(*vals)
            for o, r in zip(refs[nr + npar:], res):
                o[...] = r.astype(o.dtype)

        return pl.pallas_call(
            body,
            out_shape=[jax.ShapeDtypeStruct((S, w * ncol), dt) for w, dt in zip(ow, out_dtypes)],
            grid=grid,
            in_specs=[rspec(w) for w in rw] + [pspec(s) for s in pshape],
            out_specs=[rspec(w) for w in ow],
            compiler_params=_cparams(("parallel", "parallel")),
            name=name + "_fwd",
        )(*args)

    def call_bwd(args, douts):
        def body(*refs):
            in_refs = refs[:nr + npar]
            do_refs = refs[nr + npar:nr + npar + nout]
            dr_refs = refs[nr + npar + nout:nr + npar + nout + ndr]
            dp_refs = refs[nr + npar + nout + ndr:]
            rv = [r[...] for r in in_refs[:nr]]
            pv = [r[...] for r in in_refs[nr:]]
            dos = [d[...].astype(f32) for d in do_refs]
            if bwd_fn is not None:
                drs, dps = bwd_fn(rv, pv, dos)
            else:
                def f(*a):
                    return fn(*a[:ndr], *rv[ndr:], *a[ndr:], *pv[ndp:])

                _, vjp = jax.vjp(f, *[v.astype(f32) for v in rv[:ndr]], *pv[:ndp])
                cts = vjp(tuple(dos))
                drs, dps = cts[:ndr], cts[ndr:]
            for r, ct in zip(dr_refs, drs):
                r[...] = ct.astype(r.dtype)
            if ndp:
                @pl.when(pl.program_id(1) == 0)
                def _():
                    for r in dp_refs:
                        r[...] = jnp.zeros_like(r)

                for r, ct in zip(dp_refs, dps):
                    r[...] += ct

        return pl.pallas_call(
            body,
            out_shape=[jax.ShapeDtypeStruct(r.shape, r.dtype) for r in rows[:ndr]]
            + [jax.ShapeDtypeStruct(p.shape, f32) for p in params[:ndp]],
            grid=grid,
            in_specs=[rspec(w) for w in rw] + [pspec(s) for s in pshape] + [rspec(w) for w in ow],
            out_specs=[rspec(w) for w in rw[:ndr]] + [pspec(s) for s in pshape[:ndp]],
            compiler_params=_cparams(("parallel", "arbitrary")),
            name=name + "_bwd",
        )(*args, *douts)

    @jax.custom_vjp
    def op(*args):
        return tuple(call_fwd(*args))

    def fwd(*args):
        return tuple(call_fwd(*args)), args

    def bwd(args, douts):
        res = call_bwd(args, douts)
        drs, dps = res[:ndr], res[ndr:]
        out = list(drs) + [jnp.zeros_like(a) for a in args[ndr:nr]]
        out += [dp.astype(p.dtype) for dp, p in zip(dps, args[nr:nr + ndp])]
        out += [jnp.zeros_like(a) for a in args[nr + ndp:]]
        return tuple(out)

    op.defvjp(fwd, bwd)
    return op(*rows, *params)


def _silu(x):
    return x * (1.0 / (1.0 + jnp.exp(-x)))


def _softplus(x):
    return jnp.maximum(x, 0.0) + jnp.log(1.0 + jnp.exp(-jnp.abs(x)))


def _gelu_tanh(x):
    return 0.5 * x * (1.0 + jnp.tanh(math.sqrt(2.0 / math.pi) * (x + 0.044715 * (x * x * x))))


def _rms_fn(x, g):
    return x * lax.rsqrt(jnp.mean(x * x, axis=-1, keepdims=True) + EPS) * g


def rms(x, g, out_dtype):
    return rowwise(lambda x, g: (_rms_fn(x, g),), "rms", [x], [g.reshape(1, -1)], [out_dtype])[0]


def rms_residual(m, g, x):
    return rowwise(lambda m, x, g: (x + _rms_fn(m, g),), "rms_res", [m, x], [g.reshape(1, -1)], [f32])[0]


def silu_op(x):
    return rowwise(lambda x: (_silu(x),), "silu", [x], [], [f32])[0]


def geglu(gate, val):
    return rowwise(lambda g, v: (_gelu_tanh(g) * v,), "geglu", [gate, val], [], [MXU_DTYPE])[0]


def loss_op(y, tgt):
    S, D = y.shape
    tm = _row_tile(S, 4 * D * 4)

    def call_fwd(y, tgt):
        def body(y_ref, t_ref, o_ref):
            @pl.when(pl.program_id(0) == 0)
            def _():
                o_ref[...] = jnp.zeros_like(o_ref)

            e = y_ref[...] - t_ref[...]
            o_ref[...] += 0.5 * jnp.sum(jnp.mean(e * e, axis=-1, keepdims=True))

        out = pl.pallas_call(
            body,
            out_shape=jax.ShapeDtypeStruct((8, LANES), f32),
            grid=(S // tm,),
            in_specs=[pl.BlockSpec((tm, D), lambda i: (i, 0))] * 2,
            out_specs=pl.BlockSpec((8, LANES), lambda i: (0, 0)),
            compiler_params=_cparams(("arbitrary",)),
            name="loss_fwd",
        )(y, tgt)
        return out[0, 0]

    def call_bwd(y, tgt, g):
        def body(y_ref, t_ref, g_ref, o_ref):
            o_ref[...] = (y_ref[...] - t_ref[...]) * (g_ref[...] * (1.0 / D))

        return pl.pallas_call(
            body,
            out_shape=jax.ShapeDtypeStruct((S, D), f32),
            grid=(S // tm,),
            in_specs=[pl.BlockSpec((tm, D), lambda i: (i, 0))] * 2 + [pl.BlockSpec((1, 1), lambda i: (0, 0))],
            out_specs=pl.BlockSpec((tm, D), lambda i: (i, 0)),
            compiler_params=_cparams(("parallel",)),
            name="loss_bwd",
        )(y, tgt, g.reshape(1, 1).astype(f32))

    @jax.custom_vjp
    def op(y, tgt):
        return call_fwd(y, tgt)

    def fwd(y, tgt):
        return call_fwd(y, tgt), (y, tgt)

    def bwd(res, g):
        y, tgt = res
        return call_bwd(y, tgt, g), jnp.zeros_like(tgt)

    op.defvjp(fwd, bwd)
    return op(y, tgt)


HALO = 8


def _conv_tile(S, R):
    def ext(ref, r0):
        cur = ref[pl.ds(r0, R), :]
        prev = ref[pl.ds(pl.multiple_of(jnp.maximum(r0 - HALO, 0), HALO), HALO), :]
        nxt = ref[pl.ds(pl.multiple_of(jnp.minimum(r0 + R, S - HALO), HALO), HALO), :]
        prev = jnp.where(r0 > 0, prev, 0.0)
        nxt = jnp.where(r0 + R < S, nxt, 0.0)
        return jnp.concatenate([prev, cur, nxt], axis=0)

    return ext


def _shift_rows(e, k, R):
    n = e.shape[0]
    if k == 0:
        return e[HALO:HALO + R]
    return pltpu.roll(e, (-k) % n, 0)[HALO:HALO + R]


def dwconv(x, w, b):
    S, C = x.shape
    W = w.shape[0]
    pad = W // 2
    bw = _pick(C, (LANES,))
    R = _pick(S, (256, 128, 64, 32, 16, 8))
    nt = S // R
    ext = _conv_tile(S, R)
    b2 = b.reshape(1, C)

    def call_fwd(x, w, b2):
        def body(x_ref, w_ref, b_ref, y_ref):
            wv = [w_ref[j:j + 1, :] for j in range(W)]
            bv = b_ref[...]

            def tile(i, c):
                r0 = pl.multiple_of(i * R, R)
                e = ext(x_ref, r0)
                acc = bv + wv[pad] * e[HALO:HALO + R]
                for j in range(W):
                    if j != pad:
                        acc = acc + wv[j] * _shift_rows(e, j - pad, R)
                y_ref[pl.ds(r0, R), :] = acc
                return c

            lax.fori_loop(0, nt, tile, 0)

        return pl.pallas_call(
            body,
            out_shape=jax.ShapeDtypeStruct((S, C), f32),
            grid=(C // bw,),
            in_specs=[pl.BlockSpec((S, bw), lambda j: (0, j)), pl.BlockSpec((W, bw), lambda j: (0, j)),
                      pl.BlockSpec((1, bw), lambda j: (0, j))],
            out_specs=pl.BlockSpec((S, bw), lambda j: (0, j)),
            compiler_params=_cparams(("parallel",)),
            name="dwconv_fwd",
        )(x, w, b2)

    def call_bwd(x, w, dy):
        def body(x_ref, w_ref, dy_ref, dx_ref, dw_ref, db_ref):
            wv = [w_ref[j:j + 1, :] for j in range(W)]

            def tile(i, carry):
                dws, db = carry
                r0 = pl.multiple_of(i * R, R)
                ex = ext(x_ref, r0)
                ed = ext(dy_ref, r0)
                d0 = ed[HALO:HALO + R]
                acc = jnp.zeros((R, bw), f32)
                new = []
                for j in range(W):
                    acc = acc + wv[j] * _shift_rows(ed, pad - j, R)
                    new.append(dws[j] + jnp.sum(d0 * _shift_rows(ex, j - pad, R), axis=0, keepdims=True))
                dx_ref[pl.ds(r0, R), :] = acc
                return tuple(new), db + jnp.sum(d0, axis=0, keepdims=True)

            z = jnp.zeros((1, bw), f32)
            dws, db = lax.fori_loop(0, nt, tile, (tuple(z for _ in range(W)), z))
            dw_ref[...] = jnp.zeros_like(dw_ref)
            for j in range(W):
                dw_ref[j:j + 1, :] = dws[j]
            db_ref[...] = db

        return pl.pallas_call(
            body,
            out_shape=[jax.ShapeDtypeStruct((S, C), f32), jax.ShapeDtypeStruct((8, C), f32),
                       jax.ShapeDtypeStruct((1, C), f32)],
            grid=(C // bw,),
            in_specs=[pl.BlockSpec((S, bw), lambda j: (0, j)), pl.BlockSpec((W, bw), lambda j: (0, j)),
                      pl.BlockSpec((S, bw), lambda j: (0, j))],
            out_specs=[pl.BlockSpec((S, bw), lambda j: (0, j)), pl.BlockSpec((8, bw), lambda j: (0, j)),
                       pl.BlockSpec((1, bw), lambda j: (0, j))],
            compiler_params=_cparams(("parallel",)),
            name="dwconv_bwd",
        )(x, w, dy)

    @jax.custom_vjp
    def op(x, w, b2):
        return call_fwd(x, w, b2)

    def fwd(x, w, b2):
        return call_fwd(x, w, b2), (x, w)

    def bwd(res, dy):
        x, w = res
        dx, dw, db = call_bwd(x, w, dy)
        return dx, dw[:W], db

    op.defvjp(fwd, bwd)
    return op(x, w, b2)


def _scan_chunk(q, k, x, a_tok, dt_tok, h, *, rev, incl, nsub):
    L, N = q.shape
    Vw = x.shape[1]
    Hg = a_tok.shape[1]
    hw = Vw // Hg
    t = lax.broadcasted_iota(jnp.int32, (L, L), 0)
    l = lax.broadcasted_iota(jnp.int32, (L, L), 1)
    if rev:
        cm, cmT = l >= t, t >= l
        mask = cm if incl else l > t
    else:
        cm, cmT = l <= t, t <= l
        mask = cm if incl else l < t
    eye = t == l
    lane_a = lax.broadcasted_iota(jnp.int32, (L, Hg), 1)
    vhead = lax.broadcasted_iota(jnp.int32, (1, Vw), 1) // hw
    qhead = lax.broadcasted_iota(jnp.int32, (1, N), 1) // (N // nsub)

    decay, lam_e, tau_e, gam_e, dt_e = [], 0.0, 0.0, 0.0, 0.0
    for i in range(Hg):
        a_col = jnp.sum(jnp.where(lane_a == i, a_tok, 0.0), axis=1, keepdims=True)
        a_row = jnp.sum(jnp.where(eye, a_col, 0.0), axis=0, keepdims=True)
        cs_col = jnp.sum(jnp.where(cm, a_row, 0.0), axis=1, keepdims=True)
        cs_row = jnp.sum(jnp.where(cmT, a_col, 0.0), axis=0, keepdims=True)
        tot = jnp.sum(a_col, axis=0, keepdims=True)
        decay.append(jnp.where(mask, jnp.exp(jnp.where(mask, cs_col - cs_row, 0.0)), 0.0))
        sel = vhead == i
        lam_e = lam_e + jnp.where(sel, jnp.exp(cs_col), 0.0)
        tau_e = tau_e + jnp.where(sel, jnp.exp(tot - cs_col), 0.0)
        gam_e = gam_e + jnp.where(sel, jnp.exp(tot), 0.0)
        if dt_tok is not None:
            dt_col = jnp.sum(jnp.where(lane_a == i, dt_tok, 0.0), axis=1, keepdims=True)
            dt_e = dt_e + jnp.where(sel, dt_col, 0.0)
    v = x if dt_tok is None else x * dt_e
    s_shared = bdot(q, k, "nt") if nsub == 1 else None
    y = lam_e * bdot(q, h, "nn")
    for i in range(Hg):
        s = s_shared if nsub == 1 else bdot(jnp.where(qhead == i, q, 0.0), k, "nt")
        y = y + jnp.where(vhead == i, bdot(s * decay[i], v, "nn"), 0.0)
    hn = gam_e * h + bdot(k, tau_e * v, "tn")
    if nsub > 1:
        nhead = lax.broadcasted_iota(jnp.int32, (N, Vw), 0) // (N // nsub)
        hn = jnp.where(nhead == lax.broadcasted_iota(jnp.int32, (N, Vw), 1) // hw, hn, 0.0)
    return y, hn


def scan_op(q, k, x, a_tok, dt_tok, *, rev, incl, nsub):
    S = q.shape[0]
    G, _, Hg = a_tok.shape
    N = q.shape[1] // G
    Vw = x.shape[1] // G
    L = CHUNK
    nc = S // L
    use_dt = dt_tok is not None
    chunk = functools.partial(_scan_chunk, rev=rev, incl=incl, nsub=nsub)

    def order(c, backward):
        return (nc - 1 - c) if (rev != backward) else c

    def specs(backward):
        qs = pl.BlockSpec((L, N), lambda g, c: (order(c, backward), g))
        xs = pl.BlockSpec((L, Vw), lambda g, c: (order(c, backward), g))
        as_ = pl.BlockSpec((1, L, Hg), lambda g, c: (g, order(c, backward), 0))
        hs = pl.BlockSpec((1, 1, N, Vw), lambda g, c: (g, order(c, backward), 0, 0))
        return qs, xs, as_, hs

    def call_fwd(q, k, x, a_tok, dt_tok):
        qs, xs, as_, hs = specs(False)

        def body(*refs):
            if use_dt:
                q_ref, k_ref, x_ref, a_ref, dt_ref, y_ref, hs_ref, h_scr = refs
            else:
                q_ref, k_ref, x_ref, a_ref, y_ref, hs_ref, h_scr = refs

            @pl.when(pl.program_id(1) == 0)
            def _():
                h_scr[...] = jnp.zeros_like(h_scr)

            h = h_scr[...]
            hs_ref[0, 0] = h
            y, hn = chunk(q_ref[...], k_ref[...], x_ref[...], a_ref[0], dt_ref[0] if use_dt else None, h)
            y_ref[...] = y
            h_scr[...] = hn

        ins = [q, k, x, a_tok] + ([dt_tok] if use_dt else [])
        return pl.pallas_call(
            body,
            out_shape=[jax.ShapeDtypeStruct((S, G * Vw), f32), jax.ShapeDtypeStruct((G, nc, N, Vw), f32)],
            grid=(G, nc),
            in_specs=[qs, qs, xs, as_] + ([as_] if use_dt else []),
            out_specs=[xs, hs],
            scratch_shapes=[pltpu.VMEM((N, Vw), f32)],
            compiler_params=_cparams(("parallel", "arbitrary")),
            name="scan_fwd",
        )(*ins)

    def call_bwd(q, k, x, a_tok, dt_tok, hsave, dy):
        qs, xs, as_, hs = specs(True)

        def body(*refs):
            if use_dt:
                q_ref, k_ref, x_ref, a_ref, dt_ref, hs_ref, dy_ref, dq_ref, dk_ref, dx_ref, da_ref, ddt_ref, dh_scr = refs
            else:
                q_ref, k_ref, x_ref, a_ref, hs_ref, dy_ref, dq_ref, dk_ref, dx_ref, da_ref, dh_scr = refs

            @pl.when(pl.program_id(1) == 0)
            def _():
                dh_scr[...] = jnp.zeros_like(dh_scr)

            prim = [q_ref[...].astype(f32), k_ref[...].astype(f32), x_ref[...], a_ref[0]]
            if use_dt:
                f = lambda q, k, x, a, dt, h: chunk(q, k, x, a, dt, h)
                prim.append(dt_ref[0])
            else:
                f = lambda q, k, x, a, h: chunk(q, k, x, a, None, h)
            prim.append(hs_ref[0, 0])
            _, vjp = jax.vjp(f, *prim)
            cts = vjp((dy_ref[...], dh_scr[...]))
            dq_ref[...] = cts[0].astype(dq_ref.dtype)
            dk_ref[...] = cts[1].astype(dk_ref.dtype)
            dx_ref[...] = cts[2]
            da_ref[0] = cts[3]
            if use_dt:
                ddt_ref[0] = cts[4]
            dh_scr[...] = cts[-1]

        ins = [q, k, x, a_tok] + ([dt_tok] if use_dt else []) + [hsave, dy]
        a_shape = jax.ShapeDtypeStruct(a_tok.shape, f32)
        return pl.pallas_call(
            body,
            out_shape=[jax.ShapeDtypeStruct(q.shape, q.dtype), jax.ShapeDtypeStruct(k.shape, k.dtype),
                       jax.ShapeDtypeStruct(x.shape, f32), a_shape] + ([a_shape] if use_dt else []),
            grid=(G, nc),
            in_specs=[qs, qs, xs, as_] + ([as_] if use_dt else []) + [hs, xs],
            out_specs=[qs, qs, xs, as_] + ([as_] if use_dt else []),
            scratch_shapes=[pltpu.VMEM((N, Vw), f32)],
            compiler_params=_cparams(("parallel", "arbitrary")),
            name="scan_bwd",
        )(*ins)

    if use_dt:
        @jax.custom_vjp
        def op(q, k, x, a_tok, dt_tok):
            return call_fwd(q, k, x, a_tok, dt_tok)[0]

        def fwd(q, k, x, a_tok, dt_tok):
            y, hsave = call_fwd(q, k, x, a_tok, dt_tok)
            return y, (q, k, x, a_tok, dt_tok, hsave)

        def bwd(res, dy):
            q, k, x, a_tok, dt_tok, hsave = res
            return tuple(call_bwd(q, k, x, a_tok, dt_tok, hsave, dy))

        op.defvjp(fwd, bwd)
        return op(q, k, x, a_tok, dt_tok)

    @jax.custom_vjp
    def op(q, k, x, a_tok):
        return call_fwd(q, k, x, a_tok, None)[0]

    def fwd(q, k, x, a_tok):
        y, hsave = call_fwd(q, k, x, a_tok, None)
        return y, (q, k, x, a_tok, hsave)

    def bwd(res, dy):
        q, k, x, a_tok, hsave = res
        return tuple(call_bwd(q, k, x, a_tok, None, hsave, dy))

    op.defvjp(fwd, bwd)
    return op(q, k, x, a_tok)


def _swap_halves(x, dh):
    W = x.shape[1]
    lane = lax.broadcasted_iota(jnp.int32, (1, W), 1) % dh
    return jnp.where(lane < dh // 2, pltpu.roll(x, W - dh // 2, 1), pltpu.roll(x, dh // 2, 1))


def rotary(rq, rk, cos_t, sin_t):
    scale = RET_DH ** -0.5

    def fn(rq, rk, c, s):
        return rq * c + _swap_halves(rq, RET_DH) * s, (rk * c + _swap_halves(rk, RET_DH) * s) * scale

    def bwd_fn(rv, pv, dos):
        _, _, c, s = rv
        dq, dk = dos
        dk = dk * scale
        return (dq * c + _swap_halves(dq * s, RET_DH), dk * c + _swap_halves(dk * s, RET_DH)), ()

    return rowwise(fn, "rotary", [rq, rk, cos_t, sin_t], [], [MXU_DTYPE, MXU_DTYPE], n_diff_rows=2, bwd_fn=bwd_fn)


def _rope_tables(S, width):
    half = RET_DH // 2
    inv = 1.0 / (ROPE_BASE ** (jnp.arange(half, dtype=f32) / half))
    ang = jnp.arange(S, dtype=f32)[:, None] * inv[None, :]
    cos, sin = jnp.cos(ang), jnp.sin(ang)
    reps = width // RET_DH
    return jnp.tile(jnp.concatenate([cos, cos], axis=1), (1, reps)), jnp.tile(jnp.concatenate([-sin, sin], axis=1), (1, reps))


def _exact_dot(x, m):
    return jnp.dot(x, m, precision=lax.Precision.HIGHEST, preferred_element_type=f32)


def ret_post(y_f, y_b, rg, gn_g):
    W = y_f.shape[1]
    idx = np.arange(W) // RET_DH
    avg = jnp.asarray((idx[:, None] == idx[None, :]).astype(np.float32) / RET_DH)

    def fn(yf, yb, rg, g, avg):
        y = yf + yb
        mu = _exact_dot(y, avg)
        d = y - mu
        var = _exact_dot(d * d, avg)
        return (_silu(rg) * (d * lax.rsqrt(var + EPS) * g),)

    return rowwise(fn, "ret_post", [y_f, y_b, rg], [gn_g.reshape(1, -1), avg], [MXU_DTYPE], n_diff_params=1)[0]


def _na_bias(rpb, win_r):
    H = rpb.shape[0]
    qc = np.arange(GRID_W)[:, None]
    kc = np.arange(GRID_W)[None, :]
    cstart = np.clip(qc - NA_WIN_C // 2, 0, GRID_W - NA_WIN_C)
    valid = (kc >= cstart) & (kc < cstart + NA_WIN_C)
    dc = np.clip(kc - qc, -(NA_WIN_C - 1), NA_WIN_C - 1) + (NA_WIN_C - 1)
    onehot = (dc[None] == np.arange(2 * NA_WIN_C - 1)[:, None, None]).astype(np.float32)
    t1 = jnp.einsum("hrd,dqk->hrqk", rpb.astype(f32), jnp.asarray(onehot), precision=lax.Precision.HIGHEST)
    per_delta = [t1[:, NA_WIN_R - 1 - d:NA_WIN_R - 1 - d + win_r] for d in range(win_r)]
    b = jnp.stack(per_delta, axis=1)
    b = jnp.where(jnp.asarray(valid)[None, None, None], b, NEG_INF)
    return jnp.transpose(b, (0, 1, 3, 2, 4)).reshape(H, win_r, GRID_W, win_r * GRID_W)


def _na_row(q, kw, vw, biases):
    lane = lax.broadcasted_iota(jnp.int32, (1, q.shape[1]), 1) // NA_DH
    o = 0.0
    for i, b in enumerate(biases):
        qi = jnp.where(lane == i, q, 0.0) * (NA_DH ** -0.5)
        s = bdot(qi, kw, "nt") + b
        e = jnp.exp(s - jnp.max(s, axis=1, keepdims=True))
        p = e / jnp.sum(e, axis=1, keepdims=True)
        o = o + jnp.where(lane == i, bdot(p, vw, "nn"), 0.0)
    return o


def na_op(nq, nk, nv, bias):
    S, W = nq.shape
    rows = S // GRID_W
    win_r = bias.shape[1]
    nkeys = win_r * GRID_W
    hp = LANES // NA_DH
    npair = W // LANES
    RB = min(16, rows)
    nrb = rows // RB
    qspec = pl.BlockSpec((RB * GRID_W, LANES), lambda p, r: (r, p))
    kspec = pl.BlockSpec((S, LANES), lambda p, r: (0, p))
    bspec = pl.BlockSpec((hp, win_r, GRID_W, nkeys), lambda p, r: (p, 0, 0, 0))

    def window(r):
        r0 = jnp.clip(r - win_r // 2, 0, rows - win_r)
        return pl.multiple_of(r0 * GRID_W, GRID_W), r - r0

    def call_fwd(nq, nk, nv, bias):
        def body(q_ref, k_ref, v_ref, b_ref, o_ref):
            rb = pl.program_id(1)

            def row(i, c):
                k0, d = window(rb * RB + i)
                q0 = pl.multiple_of(i * GRID_W, GRID_W)
                o = _na_row(q_ref[pl.ds(q0, GRID_W), :].astype(f32), k_ref[pl.ds(k0, nkeys), :], v_ref[pl.ds(k0, nkeys), :],
                            [b_ref[h, pl.ds(d, 1)][0] for h in range(hp)])
                o_ref[pl.ds(q0, GRID_W), :] = o.astype(o_ref.dtype)
                return c

            lax.fori_loop(0, RB, row, 0)

        return pl.pallas_call(
            body,
            out_shape=jax.ShapeDtypeStruct((S, W), nq.dtype),
            grid=(npair, nrb),
            in_specs=[qspec, kspec, kspec, bspec],
            out_specs=qspec,
            compiler_params=_cparams(("parallel", "arbitrary")),
            name="na_fwd",
        )(nq, nk, nv, bias)

    def call_bwd(nq, nk, nv, bias, do):
        def body(q_ref, k_ref, v_ref, b_ref, do_ref, dq_ref, dk_ref, dv_ref, db_ref, dk_acc, dv_acc):
            rb = pl.program_id(1)

            @pl.when(rb == 0)
            def _():
                dk_acc[...] = jnp.zeros_like(dk_acc)
                dv_acc[...] = jnp.zeros_like(dv_acc)
                db_ref[...] = jnp.zeros_like(db_ref)

            def row(i, c):
                k0, d = window(rb * RB + i)
                q0 = pl.multiple_of(i * GRID_W, GRID_W)
                bs = [b_ref[h, pl.ds(d, 1)][0] for h in range(hp)]
                _, vjp = jax.vjp(lambda q, kw, vw, *b: _na_row(q, kw, vw, b), q_ref[pl.ds(q0, GRID_W), :].astype(f32),
                                 k_ref[pl.ds(k0, nkeys), :].astype(f32), v_ref[pl.ds(k0, nkeys), :].astype(f32), *bs)
                cts = vjp(do_ref[pl.ds(q0, GRID_W), :].astype(f32))
                dq_ref[pl.ds(q0, GRID_W), :] = cts[0].astype(dq_ref.dtype)
                dk_acc[pl.ds(k0, nkeys), :] += cts[1]
                dv_acc[pl.ds(k0, nkeys), :] += cts[2]
                for h in range(hp):
                    db_ref[h, pl.ds(d, 1)] += cts[3 + h][None]
                return c

            lax.fori_loop(0, RB, row, 0)

            @pl.when(rb == nrb - 1)
            def _():
                dk_ref[...] = dk_acc[...].astype(dk_ref.dtype)
                dv_ref[...] = dv_acc[...].astype(dv_ref.dtype)

        return pl.pallas_call(
            body,
            out_shape=[jax.ShapeDtypeStruct((S, W), nq.dtype), jax.ShapeDtypeStruct((S, W), nk.dtype),
                       jax.ShapeDtypeStruct((S, W), nv.dtype), jax.ShapeDtypeStruct(bias.shape, f32)],
            grid=(npair, nrb),
            in_specs=[qspec, kspec, kspec, bspec, qspec],
            out_specs=[qspec, kspec, kspec, bspec],
            scratch_shapes=[pltpu.VMEM((S, LANES), f32), pltpu.VMEM((S, LANES), f32)],
            compiler_params=_cparams(("parallel", "arbitrary")),
            name="na_bwd",
        )(nq, nk, nv, bias, do)

    @jax.custom_vjp
    def op(nq, nk, nv, bias):
        return call_fwd(nq, nk, nv, bias)

    def fwd(nq, nk, nv, bias):
        return call_fwd(nq, nk, nv, bias), (nq, nk, nv, bias)

    def bwd(res, do):
        return tuple(call_bwd(*res, do))

    op.defvjp(fwd, bwd)
    return op(nq, nk, nv, bias)


def ssd_dt(dt_raw, dt_bias, a_neg):
    def fn(r, b, a):
        dt = _softplus(r + b)
        return dt, dt * a

    return rowwise(fn, "ssd_dt", [dt_raw], [dt_bias, a_neg], [f32, f32])


def ssd_post(y_f, y_b, xs, z, d_skip_lanes, norm_g, groups):
    def fn(yf, yb, xs, z, dsk, g):
        y = (yf + yb + xs * dsk) * _silu(z)
        return (y * lax.rsqrt(jnp.mean(y * y, axis=-1, keepdims=True) + EPS) * g,)

    return rowwise(fn, "ssd_post", [y_f, y_b, xs, z], [d_skip_lanes.reshape(1, -1), norm_g.reshape(1, -1)], [MXU_DTYPE],
                   ncol=groups)[0]


def _heads_major(t, groups):
    S = t.shape[0]
    return jnp.transpose(t.reshape(S, groups, -1), (1, 0, 2))


def retention_na_mixer(hn, w_in, decay_logit, gn_g, rpb, w_out, tables):
    S = hn.shape[0]
    R = RET_HEADS * RET_DH
    NW = NA_HEADS * NA_DH
    cols = lambda a, b: w_in[:, a:b]
    rq, rk, rv, rg = (mm(hn, cols(j * R, (j + 1) * R)) for j in range(4))
    nq, nk, nv = (mm(hn, cols(4 * R + j * NW, 4 * R + (j + 1) * NW), out_dtype=MXU_DTYPE) for j in range(3))
    qr, kr = rotary(rq, rk, *tables)
    log_gamma = -_softplus(-decay_logit.astype(f32))
    pairs = R // LANES
    hp = LANES // RET_DH
    a_f = jnp.broadcast_to(log_gamma[0].reshape(pairs, 1, hp), (pairs, S, hp))
    a_b = jnp.broadcast_to(log_gamma[1].reshape(pairs, 1, hp), (pairs, S, hp))
    y_f = scan_op(qr, kr, rv, a_f, None, rev=False, incl=True, nsub=hp)
    y_b = scan_op(qr, kr, rv, a_b, None, rev=True, incl=False, nsub=hp)
    ret = ret_post(y_f, y_b, rg, gn_g)
    rows = S // GRID_W
    nao = na_op(nq, nk, nv, _na_bias(rpb, min(NA_WIN_R, rows)))
    return mm(ret, w_out[:R]) + mm(nao, w_out[R:])


def ssd_mixer(hn, w_in, conv_w, conv_b, dt_bias, a_log, d_skip, norm_g, w_out):
    heads = d_skip.shape[0]
    inner = heads * SSD_HEADDIM
    gs = SSD_GROUPS * SSD_STATE
    o_x, o_b, o_c, o_dt = inner, 2 * inner, 2 * inner + gs, 2 * inner + 2 * gs
    z = mm(hn, w_in[:, :inner])
    pre = [mm(hn, w_in[:, a:b]) for a, b in ((o_x, o_b), (o_b, o_c), (o_c, o_dt))]
    dt_raw = mm(hn, w_in[:, o_dt:])
    cw = [conv_w[:, a - inner:b - inner] for a, b in ((o_x, o_b), (o_b, o_c), (o_c, o_dt))]
    cb = [conv_b[a - inner:b - inner] for a, b in ((o_x, o_b), (o_b, o_c), (o_c, o_dt))]
    xs, bm, cm = (silu_op(dwconv(p, w, b)) for p, w, b in zip(pre, cw, cb))
    a_neg = -jnp.exp(a_log.astype(f32)).reshape(1, -1)
    dt, la = ssd_dt(dt_raw, dt_bias.astype(f32).reshape(1, -1), a_neg)
    dt_f, dt_b = _heads_major(dt[:, :heads], SSD_GROUPS), _heads_major(dt[:, heads:], SSD_GROUPS)
    la_f, la_b = _heads_major(la[:, :heads], SSD_GROUPS), _heads_major(la[:, heads:], SSD_GROUPS)
    y_f = scan_op(cm, bm, xs, la_f, dt_f, rev=False, incl=True, nsub=1)
    y_b = scan_op(cm, bm, xs, la_b, dt_b, rev=True, incl=False, nsub=1)
    y = ssd_post(y_f, y_b, xs, z, jnp.repeat(d_skip.astype(f32), SSD_HEADDIM), norm_g, SSD_GROUPS)
    return mm(y, w_out)


def conv_geglu_ffn(hf, w_up, conv_w, conv_b, w_down):
    F = w_down.shape[0]
    gate = dwconv(mm(hf, w_up[:, :F]), conv_w[:, :F], conv_b[:F])
    val = dwconv(mm(hf, w_up[:, F:]), conv_w[:, F:], conv_b[F:])
    return mm(geglu(gate, val), w_down)


def model_loss(x, tgt, big, small, rep):
    S = x.shape[0]
    depth = rep["norm_mix_pre"].shape[0]
    tables = _rope_tables(S, RET_HEADS * RET_DH)
    for layer in range(depth):
        i = layer // 2
        hn = rms(x, rep["norm_mix_pre"][layer], MXU_DTYPE)
        if layer % 2 == 0:
            m = retention_na_mixer(hn, big["ab_w_in"][i], rep["ab_ret_decay_logit"][i], rep["ab_ret_gn_g"][i],
                                   rep["ab_na_rpb"][i], big["ab_w_out"][i], tables)
        else:
            m = ssd_mixer(hn, big["c_w_in"][i], small["c_conv_w"][i], small["c_conv_b"][i], rep["c_dt_bias"][i],
                          rep["c_a_log"][i], rep["c_d_skip"][i], small["c_norm_g"][i], big["c_w_out"][i])
        x = rms_residual(m, rep["norm_mix_post"][layer], x)
        hf = rms(x, rep["norm_ffn_pre"][layer], MXU_DTYPE)
        f = conv_geglu_ffn(hf, big["ffn_w_up"][layer], small["ffn_conv_w"][layer], rep["ffn_conv_b"][layer],
                           big["ffn_w_down"][layer])
        x = rms_residual(f, rep["norm_ffn_post"][layer], x)
    return loss_op(x, tgt)


def _mesh_pos():
    return lax.axis_index("x"), lax.axis_index("y"), lax.axis_index("c")


def gather_chips(local):
    R, Wd = local.shape

    half = R // 2

    def body(x_ref, out_ref, send_sems, recv_sems, local_sem):
        x, y, c = _mesh_pos()
        chips = [(1 - x, y), (x, 1 - y), (1 - x, 1 - y)]
        mine = pltpu.make_async_copy(x_ref, out_ref.at[2 * x + y], local_sem)
        mine.start()

        def rows(chip, h):
            return out_ref.at[chip, pl.ds(pl.multiple_of(h * half, PACK_ROWS // 2), half), :]

        def copy(k, src, chip, h, to):
            return pltpu.make_async_remote_copy(src_ref=src, dst_ref=rows(chip, h), send_sem=send_sems.at[k],
                                                recv_sem=recv_sems.at[k], device_id=to, device_id_type=pl.DeviceIdType.MESH)

        my_half = x_ref.at[pl.ds(pl.multiple_of(c * half, PACK_ROWS // 2), half), :]
        first = [copy(k, my_half, 2 * x + y, c, (cx, cy, c)) for k, (cx, cy) in enumerate(chips)]
        for cp in first:
            cp.start()
        passed = [copy(3 + k, rows(2 * cx + cy, c), 2 * cx + cy, c, (x, y, 1 - c)) for k, (cx, cy) in enumerate(chips)]
        for k, (cx, cy) in enumerate(chips):
            copy(k, my_half, 2 * cx + cy, c, (cx, cy, c)).wait_recv()
            passed[k].start()
        for k, (cx, cy) in enumerate(chips):
            copy(3 + k, my_half, 2 * cx + cy, 1 - c, (x, y, 1 - c)).wait_recv()
        for cp in first + passed:
            cp.wait_send()
        mine.wait()

    return pl.pallas_call(
        body,
        out_shape=jax.ShapeDtypeStruct((N_CHIPS, R, Wd), local.dtype),
        in_specs=[pl.BlockSpec(memory_space=pl.ANY)],
        out_specs=pl.BlockSpec(memory_space=pl.ANY),
        scratch_shapes=[pltpu.SemaphoreType.DMA((6,)), pltpu.SemaphoreType.DMA((6,)), pltpu.SemaphoreType.DMA],
        name="gather_chips",
    )(local)


def pair_swap(parts):
    n, R, Wd = parts.shape
    half = R // 2

    def body(p_ref, own_ref, got_ref, send_sem, recv_sem, local_sem):
        x, y, c = _mesh_pos()

        def rows(h):
            return p_ref.at[:, pl.ds(pl.multiple_of(h * half, PACK_ROWS // 2), half), :]

        own = pltpu.make_async_copy(rows(c), own_ref, local_sem)
        own.start()
        cp = pltpu.make_async_remote_copy(src_ref=rows(1 - c), dst_ref=got_ref, send_sem=send_sem, recv_sem=recv_sem,
                                          device_id=(x, y, 1 - c), device_id_type=pl.DeviceIdType.MESH)
        cp.start()
        cp.wait()
        own.wait()

    shape = jax.ShapeDtypeStruct((n, half, Wd), parts.dtype)
    return pl.pallas_call(
        body,
        out_shape=[shape, shape],
        in_specs=[pl.BlockSpec(memory_space=pl.ANY)],
        out_specs=[pl.BlockSpec(memory_space=pl.ANY)] * 2,
        scratch_shapes=[pltpu.SemaphoreType.DMA, pltpu.SemaphoreType.DMA, pltpu.SemaphoreType.DMA],
        name="pair_swap",
    )(parts)


def chip_exchange(parts):
    n, R, Wd = parts.shape

    def body(p_ref, out_ref, send_sems, recv_sems, local_sem):
        x, y, c = _mesh_pos()
        my = 2 * x + y
        chips = [(1 - x, y), (x, 1 - y), (1 - x, 1 - y)]
        mine = pltpu.make_async_copy(p_ref.at[my], out_ref.at[my], local_sem)
        mine.start()

        def copy(k, src_slot, dst_slot, to):
            return pltpu.make_async_remote_copy(src_ref=p_ref.at[src_slot], dst_ref=out_ref.at[dst_slot], send_sem=send_sems.at[k],
                                                recv_sem=recv_sems.at[k], device_id=to, device_id_type=pl.DeviceIdType.MESH)

        sends = [copy(k, 2 * cx + cy, my, (cx, cy, c)) for k, (cx, cy) in enumerate(chips)]
        for cp in sends:
            cp.start()
        for k, (cx, cy) in enumerate(chips):
            copy(k, my, 2 * cx + cy, (cx, cy, c)).wait_recv()
        for cp in sends:
            cp.wait_send()
        mine.wait()

    return pl.pallas_call(
        body,
        out_shape=jax.ShapeDtypeStruct((n, R, Wd), parts.dtype),
        in_specs=[pl.BlockSpec(memory_space=pl.ANY)],
        out_specs=pl.BlockSpec(memory_space=pl.ANY),
        scratch_shapes=[pltpu.SemaphoreType.DMA((3,)), pltpu.SemaphoreType.DMA((3,)), pltpu.SemaphoreType.DMA],
        name="chip_exchange",
    )(parts)


def pair_share(mine):
    R, Wd = mine.shape

    def body(m_ref, out_ref, send_sem, recv_sem, local_sem):
        x, y, c = _mesh_pos()
        own = pltpu.make_async_copy(m_ref, out_ref.at[c], local_sem)
        own.start()
        cp = pltpu.make_async_remote_copy(src_ref=m_ref, dst_ref=out_ref.at[c], send_sem=send_sem, recv_sem=recv_sem,
                                          device_id=(x, y, 1 - c), device_id_type=pl.DeviceIdType.MESH)
        cp.start()
        pltpu.make_async_remote_copy(src_ref=m_ref, dst_ref=out_ref.at[1 - c], send_sem=send_sem, recv_sem=recv_sem,
                                     device_id=(x, y, 1 - c), device_id_type=pl.DeviceIdType.MESH).wait_recv()
        cp.wait_send()
        own.wait()

    return pl.pallas_call(
        body,
        out_shape=jax.ShapeDtypeStruct((2, R, Wd), mine.dtype),
        in_specs=[pl.BlockSpec(memory_space=pl.ANY)],
        out_specs=pl.BlockSpec(memory_space=pl.ANY),
        scratch_shapes=[pltpu.SemaphoreType.DMA, pltpu.SemaphoreType.DMA, pltpu.SemaphoreType.DMA],
        name="pair_share",
    )(mine)


def sum_slots(recv, out_dtype=f32):
    n, R, Wd = recv.shape
    tr = _pick(R, (512, 256, 128, 64, 32, 16, 8))

    def body(r_ref, o_ref):
        acc = r_ref[0].astype(f32)
        for j in range(1, n):
            acc = acc + r_ref[j].astype(f32)
        o_ref[...] = acc.astype(o_ref.dtype)

    return pl.pallas_call(
        body,
        out_shape=jax.ShapeDtypeStruct((R, Wd), out_dtype),
        grid=(R // tr,),
        in_specs=[pl.BlockSpec((n, tr, Wd), lambda i: (0, i, 0))],
        out_specs=pl.BlockSpec((tr, Wd), lambda i: (i, 0)),
        compiler_params=_cparams(("parallel",)),
        name="sum_slots",
    )(recv)


def add_pair(a, b):
    n, R, Wd = a.shape
    tr = _pick(R, (512, 256, 128, 64, 32, 16, 8))

    def body(a_ref, b_ref, o_ref):
        o_ref[...] = (a_ref[...].astype(f32) + b_ref[...].astype(f32)).astype(o_ref.dtype)

    spec = pl.BlockSpec((1, tr, Wd), lambda s, i: (s, i, 0))
    return pl.pallas_call(
        body,
        out_shape=jax.ShapeDtypeStruct(a.shape, a.dtype),
        grid=(n, R // tr),
        in_specs=[spec, spec],
        out_specs=spec,
        compiler_params=_cparams(("parallel", "parallel")),
        name="add_pair",
    )(a, b)


def reduce_scatter(parts):
    n, R, Wd = parts.shape
    own, got = pair_swap(parts)
    chip_sum = add_pair(own, got)
    mine = sum_slots(chip_exchange(chip_sum))
    return pair_share(mine).reshape(R, Wd)


def adamw(w, g, m, v):
    R, C = w.shape
    tr = R
    for cand in (512, 256, 128, 64, 32, 16, 8):
        if R % cand == 0 and cand * C * 4 <= (1 << 20):
            tr = cand
            break

    def body(w_ref, g_ref, m_ref, v_ref, d_ref, mo_ref, vo_ref):
        g = g_ref[...]
        m = ADAM_B1 * m_ref[...] + (1.0 - ADAM_B1) * g
        v = ADAM_B2 * v_ref[...] + (1.0 - ADAM_B2) * (g * g)
        m_hat = m / (1.0 - ADAM_B1 ** ADAM_STEP)
        v_hat = v / (1.0 - ADAM_B2 ** ADAM_STEP)
        d_ref[...] = -ADAM_LR * (m_hat / (jnp.sqrt(v_hat) + ADAM_EPS) + ADAM_WD * w_ref[...])
        mo_ref[...] = m
        vo_ref[...] = v

    spec = pl.BlockSpec((tr, C), lambda i: (i, 0))
    return pl.pallas_call(
        body,
        out_shape=[jax.ShapeDtypeStruct((R, C), f32)] * 3,
        grid=(R // tr,),
        in_specs=[spec] * 4,
        out_specs=[spec] * 3,
        compiler_params=_cparams(("parallel",)),
        name="adamw",
    )(w, g, m, v)


def _pack(arrs, dtype):
    flat = jnp.concatenate([a.astype(dtype).reshape(-1) for a in arrs])
    n = flat.shape[0]
    unit = PACK_W * PACK_ROWS
    padded = -(-n // unit) * unit
    return jnp.pad(flat, (0, padded - n)).reshape(-1, PACK_W)


def _unpack(buf, shapes):
    flat = buf.reshape(-1)
    out, off = [], 0
    for s in shapes:
        n = int(np.prod(s))
        out.append(flat[off:off + n].reshape(s))
        off += n
    return out


BIG = (("ab_w_in", 2), ("ab_w_out", 1), ("c_w_in", 2), ("c_w_out", 1), ("ffn_w_up", 2), ("ffn_w_down", 1))
SMALL = (("c_conv_w", 2), ("c_conv_b", 1), ("c_norm_g", 1), ("ffn_conv_w", 2))
REP = ("norm_mix_pre", "norm_mix_post", "norm_ffn_pre", "norm_ffn_post", "ab_ret_decay_logit", "ab_ret_gn_g", "ab_na_rpb",
       "c_dt_bias", "c_a_log", "c_d_skip", "ffn_conv_b")
WEIGHTS = ("norm_mix_pre", "norm_mix_post", "norm_ffn_pre", "norm_ffn_post", "ab_w_in", "ab_ret_decay_logit", "ab_ret_gn_g",
           "ab_na_rpb", "ab_w_out", "c_w_in", "c_conv_w", "c_conv_b", "c_dt_bias", "c_a_log", "c_d_skip", "c_norm_g", "c_w_out",
           "ffn_w_up", "ffn_conv_w", "ffn_conv_b", "ffn_w_down")


def _gather_set(local, spec, dtype):
    shapes = [local[n].shape for n, _ in spec]
    got = gather_chips(_pack([local[n] for n, _ in spec], dtype))
    per_chip = [_unpack(got[s], shapes) for s in range(N_CHIPS)]
    return {n: jnp.concatenate([per_chip[s][j] for s in range(N_CHIPS)], axis=ax) for j, (n, ax) in enumerate(spec)}


def _scatter_parts(full, spec, extra, dtype):
    split = {n: jnp.split(full[n], N_CHIPS, axis=ax) for n, ax in spec}
    return jnp.stack([_pack([split[n][s] for n, _ in spec] + list(extra), dtype) for s in range(N_CHIPS)])


def kernel(x, norm_mix_pre, norm_mix_post, norm_ffn_pre, norm_ffn_post, ab_w_in, ab_ret_decay_logit, ab_ret_gn_g, ab_na_rpb, ab_w_out, c_w_in, c_conv_w, c_conv_b, c_dt_bias, c_a_log, c_d_skip, c_norm_g, c_w_out, ffn_w_up, ffn_conv_w, ffn_conv_b, ffn_w_down, loss_target, m_norm_mix_pre, m_norm_mix_post, m_norm_ffn_pre, m_norm_ffn_post, m_ab_w_in, m_ab_ret_decay_logit, m_ab_ret_gn_g, m_ab_na_rpb, m_ab_w_out, m_c_w_in, m_c_conv_w, m_c_conv_b, m_c_dt_bias, m_c_a_log, m_c_d_skip, m_c_norm_g, m_c_w_out, m_ffn_w_up, m_ffn_conv_w, m_ffn_conv_b, m_ffn_w_down, v_norm_mix_pre, v_norm_mix_post, v_norm_ffn_pre, v_norm_ffn_post, v_ab_w_in, v_ab_ret_decay_logit, v_ab_ret_gn_g, v_ab_na_rpb, v_ab_w_out, v_c_w_in, v_c_conv_w, v_c_conv_b, v_c_dt_bias, v_c_a_log, v_c_d_skip, v_c_norm_g, v_c_w_out, v_ffn_w_up, v_ffn_conv_w, v_ffn_conv_b, v_ffn_w_down):
    args = dict(locals())
    w = {n: args[n] for n in WEIGHTS}
    mom = {n: args["m_" + n] for n in WEIGHTS}
    var = {n: args["v_" + n] for n in WEIGHTS}

    big = _gather_set(w, BIG, MXU_DTYPE)
    small = _gather_set(w, SMALL, f32)
    rep = {n: w[n] for n in REP}

    def loss_fn(xs, big, small, rep):
        return model_loss(xs, loss_target[0], big, small, rep)

    loss, (gx, gbig, gsmall, grep) = jax.value_and_grad(loss_fn, argnums=(0, 1, 2, 3))(x[0], big, small, rep)
    loss = lax.psum(loss, ("x", "y", "c"))

    big_shapes = [w[n].shape for n, _ in BIG]
    small_shapes = [w[n].shape for n, _ in SMALL] + [w[n].shape for n in REP]
    g_big = _unpack(reduce_scatter(_scatter_parts(gbig, BIG, (), MXU_DTYPE)), big_shapes)
    g_small_buf = reduce_scatter(_scatter_parts(gsmall, SMALL, [grep[n] for n in REP], f32))
    grads = dict(zip([n for n, _ in BIG], g_big))
    small_names = [n for n, _ in SMALL] + list(REP)
    grads.update(zip(small_names, _unpack(g_small_buf, small_shapes)))

    delta, new_m, new_v = {}, {}, {}
    for n, _ in BIG:
        shp = w[n].shape
        two_d = lambda a: a.reshape(-1, shp[-1])
        d, m2, v2 = adamw(two_d(w[n]), two_d(grads[n]), two_d(mom[n]), two_d(var[n]))
        delta[n], new_m[n], new_v[n] = d.reshape(shp), m2.reshape(shp), v2.reshape(shp)
    pk = lambda src: _pack([src[n] for n in small_names], f32)
    d, m2, v2 = adamw(pk(w), g_small_buf, pk(mom), pk(var))
    for dst, buf in ((delta, d), (new_m, m2), (new_v, v2)):
        dst.update(zip(small_names, _unpack(buf, small_shapes)))

    return (loss, gx[None], *[grads[n] for n in WEIGHTS], *[delta[n] for n in WEIGHTS],
            *[new_m[n] for n in WEIGHTS], *[new_v[n] for n in WEIGHTS])
```

```python
import functools
import math

import numpy as np
import jax
import jax.numpy as jnp
from jax import lax
from jax.experimental import pallas as pl
from jax.experimental.pallas import tpu as pltpu

f32 = jnp.float32
bf16 = jnp.bfloat16
MXU_DTYPE = bf16

GRID_W = 64
CHUNK = 128
EPS = 1e-6
RET_HEADS = 8
RET_DH = 64
ROPE_BASE = 10000.0
NA_HEADS = 8
NA_DH = 64
NA_WIN_R = 8
NA_WIN_C = 16
SSD_HEADDIM = 64
SSD_GROUPS = 4
SSD_STATE = 128
ADAM_LR = 0.001
ADAM_B1 = 0.9
ADAM_B2 = 0.999
ADAM_EPS = 1e-08
ADAM_WD = 0.01
ADAM_STEP = 10

LANES = 128
PACK_W = 512
PACK_ROWS = 1024
PACK_ALIGN = 16
COPY_CHUNKS = 4
VMEM_LIMIT = 56 * 1024 * 1024
MM_BLOCK_BYTES = 6 * 1024 * 1024
N_CHIPS = 4
N_DEV = 8
NEG_INF = -1e30

_DIMS = {"nn": (((1,), (0,)), ((), ())), "nt": (((1,), (1,)), ((), ())), "tn": (((0,), (0,)), ((), ()))}


def _cparams(sem=None):
    return pltpu.CompilerParams(dimension_semantics=sem, vmem_limit_bytes=VMEM_LIMIT)


def _pick(dim, cands):
    for c in cands:
        if dim % c == 0:
            return c
    return dim


def _divisor_tile(dim, fits, align):
    for d in range(1, dim + 1):
        t = dim // d
        if dim % d == 0 and t % align == 0 and fits(t):
            return t
    return dim


def _bdot_raw(a, b, mode):
    return lax.dot_general(a.astype(MXU_DTYPE), b.astype(MXU_DTYPE), _DIMS[mode], preferred_element_type=f32)


@functools.partial(jax.custom_vjp, nondiff_argnums=(2,))
def bdot(a, b, mode):
    return _bdot_raw(a, b, mode)


def _bdot_fwd(a, b, mode):
    return _bdot_raw(a, b, mode), (a, b)


def _bdot_bwd(mode, res, g):
    a, b = res
    if mode == "nn":
        da, db = _bdot_raw(g, b, "nt"), _bdot_raw(a, g, "tn")
    elif mode == "nt":
        da, db = _bdot_raw(g, b, "nn"), _bdot_raw(g, a, "tn")
    else:
        da, db = _bdot_raw(b, g, "nt"), _bdot_raw(a, g, "nn")
    return da.astype(a.dtype), db.astype(b.dtype)


bdot.defvjp(_bdot_fwd, _bdot_bwd)


def _mm_call(a, b, mode, out_dtype):
    if mode == "nn":
        (M, K), (K2, N) = a.shape, b.shape
    elif mode == "nt":
        (M, K), (N, K2) = a.shape, b.shape
    else:
        (K, M), (K2, N) = a.shape, b.shape
    assert K == K2, (a.shape, b.shape, mode)
    a_bytes, b_bytes, o_bytes = a.dtype.itemsize, b.dtype.itemsize, jnp.dtype(out_dtype).itemsize
    if mode == "tn":
        tk = _pick(K, (512, 256, 128))
        tn = _divisor_tile(N, lambda t: t <= 1536, LANES)
        tm = _divisor_tile(M, lambda t: t * tn * 4 <= MM_BLOCK_BYTES, 8)
    else:
        tk, tn = K, N
        tm = _divisor_tile(M, lambda t: t * K * a_bytes <= MM_BLOCK_BYTES and t * N * o_bytes <= MM_BLOCK_BYTES, 8)
    nk = K // tk
    if mode == "nn":
        a_spec = pl.BlockSpec((tm, tk), lambda i, j, k: (i, k))
        b_spec = pl.BlockSpec((tk, tn), lambda i, j, k: (k, j))
    elif mode == "nt":
        a_spec = pl.BlockSpec((tm, tk), lambda i, j, k: (i, k))
        b_spec = pl.BlockSpec((tn, tk), lambda i, j, k: (j, k))
    else:
        a_spec = pl.BlockSpec((tk, tm), lambda i, j, k: (k, i))
        b_spec = pl.BlockSpec((tk, tn), lambda i, j, k: (k, j))

    if nk == 1:
        def body(a_ref, b_ref, o_ref):
            o_ref[...] = _bdot_raw(a_ref[...], b_ref[...], mode).astype(o_ref.dtype)
    else:
        def body(a_ref, b_ref, o_ref, acc_ref):
            k = pl.program_id(2)

            @pl.when(k == 0)
            def _():
                acc_ref[...] = jnp.zeros_like(acc_ref)

            acc_ref[...] += _bdot_raw(a_ref[...], b_ref[...], mode)

            @pl.when(k == nk - 1)
            def _():
                o_ref[...] = acc_ref[...].astype(o_ref.dtype)

    return pl.pallas_call(
        body,
        out_shape=jax.ShapeDtypeStruct((M, N), out_dtype),
        grid=(M // tm, N // tn, nk),
        in_specs=[a_spec, b_spec],
        out_specs=pl.BlockSpec((tm, tn), lambda i, j, k: (i, j)),
        scratch_shapes=[pltpu.VMEM((tm, tn), f32)] if nk > 1 else [],
        compiler_params=_cparams(("parallel", "parallel", "arbitrary")),
        name="mm_" + mode,
    )(a, b)


def mm(a, b, mode="nn", out_dtype=f32):
    @jax.custom_vjp
    def op(a, b):
        return _mm_call(a, b, mode, out_dtype)

    def fwd(a, b):
        return _mm_call(a, b, mode, out_dtype), (a, b)

    def bwd(res, g):
        a, b = res
        if mode == "nn":
            return _mm_call(g, b, "nt", a.dtype), _mm_call(a, g, "tn", b.dtype)
        if mode == "nt":
            return _mm_call(g, b, "nn", a.dtype), _mm_call(g, a, "tn", b.dtype)
        return _mm_call(b, g, "nt", a.dtype), _mm_call(a, g, "nn", b.dtype)

    op.defvjp(fwd, bwd)
    return op(a, b)


def _row_tile(S, row_bytes):
    tm = 512
    while tm > 8 and (tm * row_bytes > (6 << 20) or S % tm):
        tm //= 2
    return tm


def rowwise(fn, name, rows, params, out_dtypes, n_diff_rows=None, n_diff_params=None, ncol=1, bwd_fn=None):
    rows, params = list(rows), list(params)
    nr, npar = len(rows), len(params)
    ndr = nr if n_diff_rows is None else n_diff_rows
    ndp = npar if n_diff_params is None else n_diff_params
    S = rows[0].shape[0]
    rw = [r.shape[1] // ncol for r in rows]
    pshape = [(p.shape[0], p.shape[1] // ncol) for p in params]

    def block_structs(tm):
        return ([jax.ShapeDtypeStruct((tm, w), f32) for w in rw] + [jax.ShapeDtypeStruct(s, f32) for s in pshape])

    outs_s = jax.eval_shape(fn, *block_structs(8))
    ow = [o.shape[1] for o in outs_s]
    nout = len(ow)
    row_bytes = 4 * (sum(rw) * 2 + sum(ow) * 2)
    tm = _row_tile(S, row_bytes)
    grid = (ncol, S // tm)

    def rspec(w):
        return pl.BlockSpec((tm, w), lambda g, i: (i, g))

    def pspec(s):
        return pl.BlockSpec(s, lambda g, i: (0, g))

    def call_fwd(*args):
        def body(*refs):
            vals = [r[...].astype(f32) for r in refs[:nr + npar]]
            res = fn(*vals)
            for o, r in zip(refs[nr + npar:], res):
                o[...] = r.astype(o.dtype)

        return pl.pallas_call(
            body,
            out_shape=[jax.ShapeDtypeStruct((S, w * ncol), dt) for w, dt in zip(ow, out_dtypes)],
            grid=grid,
            in_specs=[rspec(w) for w in rw] + [pspec(s) for s in pshape],
            out_specs=[rspec(w) for w in ow],
            compiler_params=_cparams(("parallel", "parallel")),
            name=name + "_fwd",
        )(*args)

    def call_bwd(args, douts):
        def body(*refs):
            in_refs = refs[:nr + npar]
            do_refs = refs[nr + npar:nr + npar + nout]
            dr_refs = refs[nr + npar + nout:nr + npar + nout + ndr]
            dp_refs = refs[nr + npar + nout + ndr:]
            rv = [r[...] for r in in_refs[:nr]]
            pv = [r[...] for r in in_refs[nr:]]
            dos = [d[...].astype(f32) for d in do_refs]
            if bwd_fn is not None:
                drs, dps = bwd_fn(rv, pv, dos)
            else:
                def f(*a):
                    return fn(*a[:ndr], *rv[ndr:], *a[ndr:], *pv[ndp:])

                _, vjp = jax.vjp(f, *[v.astype(f32) for v in rv[:ndr]], *pv[:ndp])
                cts = vjp(tuple(dos))
                drs, dps = cts[:ndr], cts[ndr:]
            for r, ct in zip(dr_refs, drs):
                r[...] = ct.astype(r.dtype)
            if ndp:
                @pl.when(pl.program_id(1) == 0)
                def _():
                    for r in dp_refs:
                        r[...] = jnp.zeros_like(r)

                for r, ct in zip(dp_refs, dps):
                    r[...] += ct

        return pl.pallas_call(
            body,
            out_shape=[jax.ShapeDtypeStruct(r.shape, r.dtype) for r in rows[:ndr]]
            + [jax.ShapeDtypeStruct(p.shape, f32) for p in params[:ndp]],
            grid=grid,
            in_specs=[rspec(w) for w in rw] + [pspec(s) for s in pshape] + [rspec(w) for w in ow],
            out_specs=[rspec(w) for w in rw[:ndr]] + [pspec(s) for s in pshape[:ndp]],
            compiler_params=_cparams(("parallel", "arbitrary")),
            name=name + "_bwd",
        )(*args, *douts)

    @jax.custom_vjp
    def op(*args):
        return tuple(call_fwd(*args))

    def fwd(*args):
        return tuple(call_fwd(*args)), args

    def bwd(args, douts):
        res = call_bwd(args, douts)
        drs, dps = res[:ndr], res[ndr:]
        out = list(drs) + [jnp.zeros_like(a) for a in args[ndr:nr]]
        out += [dp.astype(p.dtype) for dp, p in zip(dps, args[nr:nr + ndp])]
        out += [jnp.zeros_like(a) for a in args[nr + ndp:]]
        return tuple(out)

    op.defvjp(fwd, bwd)
    return op(*rows, *params)


def _silu(x):
    return x * (1.0 / (1.0 + jnp.exp(-x)))


def _softplus(x):
    return jnp.maximum(x, 0.0) + jnp.log(1.0 + jnp.exp(-jnp.abs(x)))


def _gelu_tanh(x):
    return 0.5 * x * (1.0 + jnp.tanh(math.sqrt(2.0 / math.pi) * (x + 0.044715 * (x * x * x))))


def _rms_fn(x, g):
    return x * lax.rsqrt(jnp.mean(x * x, axis=-1, keepdims=True) + EPS) * g


def rms(x, g, out_dtype):
    return rowwise(lambda x, g: (_rms_fn(x, g),), "rms", [x], [g.reshape(1, -1)], [out_dtype])[0]


def rms_residual(m, g, x):
    return rowwise(lambda m, x, g: (x + _rms_fn(m, g),), "rms_res", [m, x], [g.reshape(1, -1)], [f32])[0]


def silu_op(x):
    return rowwise(lambda x: (_silu(x),), "silu", [x], [], [f32])[0]


def geglu(gate, val):
    return rowwise(lambda g, v: (_gelu_tanh(g) * v,), "geglu", [gate, val], [], [MXU_DTYPE])[0]


def loss_op(y, tgt):
    S, D = y.shape
    tm = _row_tile(S, 4 * D * 4)

    def call_fwd(y, tgt):
        def body(y_ref, t_ref, o_ref):
            @pl.when(pl.program_id(0) == 0)
            def _():
                o_ref[...] = jnp.zeros_like(o_ref)

            e = y_ref[...] - t_ref[...]
            o_ref[...] += 0.5 * jnp.sum(jnp.mean(e * e, axis=-1, keepdims=True))

        out = pl.pallas_call(
            body,
            out_shape=jax.ShapeDtypeStruct((8, LANES), f32),
            grid=(S // tm,),
            in_specs=[pl.BlockSpec((tm, D), lambda i: (i, 0))] * 2,
            out_specs=pl.BlockSpec((8, LANES), lambda i: (0, 0)),
            compiler_params=_cparams(("arbitrary",)),
            name="loss_fwd",
        )(y, tgt)
        return out[0, 0]

    def call_bwd(y, tgt, g):
        def body(y_ref, t_ref, g_ref, o_ref):
            o_ref[...] = (y_ref[...] - t_ref[...]) * (g_ref[...] * (1.0 / D))

        return pl.pallas_call(
            body,
            out_shape=jax.ShapeDtypeStruct((S, D), f32),
            grid=(S // tm,),
            in_specs=[pl.BlockSpec((tm, D), lambda i: (i, 0))] * 2 + [pl.BlockSpec((1, 1), lambda i: (0, 0))],
            out_specs=pl.BlockSpec((tm, D), lambda i: (i, 0)),
            compiler_params=_cparams(("parallel",)),
            name="loss_bwd",
        )(y, tgt, g.reshape(1, 1).astype(f32))

    @jax.custom_vjp
    def op(y, tgt):
        return call_fwd(y, tgt)

    def fwd(y, tgt):
        return call_fwd(y, tgt), (y, tgt)

    def bwd(res, g):
        y, tgt = res
        return call_bwd(y, tgt, g), jnp.zeros_like(tgt)

    op.defvjp(fwd, bwd)
    return op(y, tgt)


HALO = 8


def _conv_tile(S, R):
    def ext(ref, r0):
        cur = ref[pl.ds(r0, R), :]
        prev = ref[pl.ds(pl.multiple_of(jnp.maximum(r0 - HALO, 0), HALO), HALO), :]
        nxt = ref[pl.ds(pl.multiple_of(jnp.minimum(r0 + R, S - HALO), HALO), HALO), :]
        prev = jnp.where(r0 > 0, prev, 0.0)
        nxt = jnp.where(r0 + R < S, nxt, 0.0)
        return jnp.concatenate([prev, cur, nxt], axis=0)

    return ext


def _shift_rows(e, k, R):
    n = e.shape[0]
    if k == 0:
        return e[HALO:HALO + R]
    return pltpu.roll(e, (-k) % n, 0)[HALO:HALO + R]


def dwconv(x, w, b):
    S, C = x.shape
    W = w.shape[0]
    pad = W // 2
    bw = _pick(C, (LANES,))
    R = _pick(S, (256, 128, 64, 32, 16, 8))
    nt = S // R
    ext = _conv_tile(S, R)
    b2 = b.reshape(1, C)

    def call_fwd(x, w, b2):
        def body(x_ref, w_ref, b_ref, y_ref):
            wv = [w_ref[j:j + 1, :] for j in range(W)]
            bv = b_ref[...]

            def tile(i, c):
                r0 = pl.multiple_of(i * R, R)
                e = ext(x_ref, r0)
                acc = bv + wv[pad] * e[HALO:HALO + R]
                for j in range(W):
                    if j != pad:
                        acc = acc + wv[j] * _shift_rows(e, j - pad, R)
                y_ref[pl.ds(r0, R), :] = acc
                return c

            lax.fori_loop(0, nt, tile, 0)

        return pl.pallas_call(
            body,
            out_shape=jax.ShapeDtypeStruct((S, C), f32),
            grid=(C // bw,),
            in_specs=[pl.BlockSpec((S, bw), lambda j: (0, j)), pl.BlockSpec((W, bw), lambda j: (0, j)),
                      pl.BlockSpec((1, bw), lambda j: (0, j))],
            out_specs=pl.BlockSpec((S, bw), lambda j: (0, j)),
            compiler_params=_cparams(("parallel",)),
            name="dwconv_fwd",
        )(x, w, b2)

    def call_bwd(x, w, dy):
        def body(x_ref, w_ref, dy_ref, dx_ref, dw_ref, db_ref):
            wv = [w_ref[j:j + 1, :] for j in range(W)]

            def tile(i, carry):
                dws, db = carry
                r0 = pl.multiple_of(i * R, R)
                ex = ext(x_ref, r0)
                ed = ext(dy_ref, r0)
                d0 = ed[HALO:HALO + R]
                acc = jnp.zeros((R, bw), f32)
                new = []
                for j in range(W):
                    acc = acc + wv[j] * _shift_rows(ed, pad - j, R)
                    new.append(dws[j] + jnp.sum(d0 * _shift_rows(ex, j - pad, R), axis=0, keepdims=True))
                dx_ref[pl.ds(r0, R), :] = acc
                return tuple(new), db + jnp.sum(d0, axis=0, keepdims=True)

            z = jnp.zeros((1, bw), f32)
            dws, db = lax.fori_loop(0, nt, tile, (tuple(z for _ in range(W)), z))
            dw_ref[...] = jnp.zeros_like(dw_ref)
            for j in range(W):
                dw_ref[j:j + 1, :] = dws[j]
            db_ref[...] = db

        return pl.pallas_call(
            body,
            out_shape=[jax.ShapeDtypeStruct((S, C), f32), jax.ShapeDtypeStruct((8, C), f32),
                       jax.ShapeDtypeStruct((1, C), f32)],
            grid=(C // bw,),
            in_specs=[pl.BlockSpec((S, bw), lambda j: (0, j)), pl.BlockSpec((W, bw), lambda j: (0, j)),
                      pl.BlockSpec((S, bw), lambda j: (0, j))],
            out_specs=[pl.BlockSpec((S, bw), lambda j: (0, j)), pl.BlockSpec((8, bw), lambda j: (0, j)),
                       pl.BlockSpec((1, bw), lambda j: (0, j))],
            compiler_params=_cparams(("parallel",)),
            name="dwconv_bwd",
        )(x, w, dy)

    @jax.custom_vjp
    def op(x, w, b2):
        return call_fwd(x, w, b2)

    def fwd(x, w, b2):
        return call_fwd(x, w, b2), (x, w)

    def bwd(res, dy):
        x, w = res
        dx, dw, db = call_bwd(x, w, dy)
        return dx, dw[:W], db

    op.defvjp(fwd, bwd)
    return op(x, w, b2)


def _scan_chunk(q, k, x, a_tok, dt_tok, h, *, rev, incl, nsub):
    L, N = q.shape
    Vw = x.shape[1]
    Hg = a_tok.shape[1]
    hw = Vw // Hg
    t = lax.broadcasted_iota(jnp.int32, (L, L), 0)
    l = lax.broadcasted_iota(jnp.int32, (L, L), 1)
    if rev:
        cm, cmT = l >= t, t >= l
        mask = cm if incl else l > t
    else:
        cm, cmT = l <= t, t <= l
        mask = cm if incl else l < t
    eye = t == l
    lane_a = lax.broadcasted_iota(jnp.int32, (L, Hg), 1)
    vhead = lax.broadcasted_iota(jnp.int32, (1, Vw), 1) // hw
    qhead = lax.broadcasted_iota(jnp.int32, (1, N), 1) // (N // nsub)

    decay, lam_e, tau_e, gam_e, dt_e = [], 0.0, 0.0, 0.0, 0.0
    if Hg % 8 == 0:
        cs_cols = _exact_dot(cm.astype(f32), a_tok)
        cs_rows = lax.dot_general(a_tok, cmT.astype(f32), _DIMS["tn"], precision=lax.Precision.HIGHEST,
                                  preferred_element_type=f32)
        row_a = lax.broadcasted_iota(jnp.int32, (Hg, L), 0)
        for i in range(Hg):
            cs_col = jnp.sum(jnp.where(lane_a == i, cs_cols, 0.0), axis=1, keepdims=True)
            cs_row = jnp.sum(jnp.where(row_a == i, cs_rows, 0.0), axis=0, keepdims=True)
            decay.append(jnp.where(mask, jnp.exp(jnp.where(mask, cs_col - cs_row, 0.0)), 0.0))
        expand = (lax.broadcasted_iota(jnp.int32, (Hg, Vw), 0) == lax.broadcasted_iota(jnp.int32, (Hg, Vw), 1) // hw).astype(f32)
        tot = jnp.sum(a_tok, axis=0, keepdims=True)
        lam_e = _exact_dot(jnp.exp(cs_cols), expand)
        tau_e = _exact_dot(jnp.exp(tot - cs_cols), expand)
        gam_e = _exact_dot(jnp.broadcast_to(jnp.exp(tot), (8, Hg)), expand)[0:1]
        if dt_tok is not None:
            dt_e = _exact_dot(dt_tok, expand)
    else:
        for i in range(Hg):
            a_col = jnp.sum(jnp.where(lane_a == i, a_tok, 0.0), axis=1, keepdims=True)
            a_row = jnp.sum(jnp.where(eye, a_col, 0.0), axis=0, keepdims=True)
            cs_col = jnp.sum(jnp.where(cm, a_row, 0.0), axis=1, keepdims=True)
            cs_row = jnp.sum(jnp.where(cmT, a_col, 0.0), axis=0, keepdims=True)
            tot = jnp.sum(a_col, axis=0, keepdims=True)
            decay.append(jnp.where(mask, jnp.exp(jnp.where(mask, cs_col - cs_row, 0.0)), 0.0))
            sel = vhead == i
            lam_e = lam_e + jnp.where(sel, jnp.exp(cs_col), 0.0)
            tau_e = tau_e + jnp.where(sel, jnp.exp(tot - cs_col), 0.0)
            gam_e = gam_e + jnp.where(sel, jnp.exp(tot), 0.0)
            if dt_tok is not None:
                dt_col = jnp.sum(jnp.where(lane_a == i, dt_tok, 0.0), axis=1, keepdims=True)
                dt_e = dt_e + jnp.where(sel, dt_col, 0.0)
    v = x if dt_tok is None else x * dt_e
    s_shared = bdot(q, k, "nt") if nsub == 1 else None
    y = lam_e * bdot(q, h, "nn")
    for i in range(Hg):
        s = s_shared if nsub == 1 else bdot(jnp.where(qhead == i, q, 0.0), k, "nt")
        y = y + jnp.where(vhead == i, bdot(s * decay[i], v, "nn"), 0.0)
    hn = gam_e * h + bdot(k, tau_e * v, "tn")
    if nsub > 1:
        nhead = lax.broadcasted_iota(jnp.int32, (N, Vw), 0) // (N // nsub)
        hn = jnp.where(nhead == lax.broadcasted_iota(jnp.int32, (N, Vw), 1) // hw, hn, 0.0)
    return y, hn


def scan_op(q, k, x, a_tok, dt_tok, *, rev, incl, nsub):
    S = q.shape[0]
    G, _, Hg = a_tok.shape
    N = q.shape[1] // G
    Vw = x.shape[1] // G
    L = CHUNK
    nc = S // L
    use_dt = dt_tok is not None
    chunk = functools.partial(_scan_chunk, rev=rev, incl=incl, nsub=nsub)

    def order(c, backward):
        return (nc - 1 - c) if (rev != backward) else c

    def specs(backward):
        qs = pl.BlockSpec((L, N), lambda g, c: (order(c, backward), g))
        xs = pl.BlockSpec((L, Vw), lambda g, c: (order(c, backward), g))
        as_ = pl.BlockSpec((1, L, Hg), lambda g, c: (g, order(c, backward), 0))
        hs = pl.BlockSpec((1, 1, N, Vw), lambda g, c: (g, order(c, backward), 0, 0))
        return qs, xs, as_, hs

    def call_fwd(q, k, x, a_tok, dt_tok):
        qs, xs, as_, hs = specs(False)

        def body(*refs):
            if use_dt:
                q_ref, k_ref, x_ref, a_ref, dt_ref, y_ref, hs_ref, h_scr = refs
            else:
                q_ref, k_ref, x_ref, a_ref, y_ref, hs_ref, h_scr = refs

            @pl.when(pl.program_id(1) == 0)
            def _():
                h_scr[...] = jnp.zeros_like(h_scr)

            h = h_scr[...]
            hs_ref[0, 0] = h
            y, hn = chunk(q_ref[...], k_ref[...], x_ref[...], a_ref[0], dt_ref[0] if use_dt else None, h)
            y_ref[...] = y
            h_scr[...] = hn

        ins = [q, k, x, a_tok] + ([dt_tok] if use_dt else [])
        return pl.pallas_call(
            body,
            out_shape=[jax.ShapeDtypeStruct((S, G * Vw), f32), jax.ShapeDtypeStruct((G, nc, N, Vw), f32)],
            grid=(G, nc),
            in_specs=[qs, qs, xs, as_] + ([as_] if use_dt else []),
            out_specs=[xs, hs],
            scratch_shapes=[pltpu.VMEM((N, Vw), f32)],
            compiler_params=_cparams(("parallel", "arbitrary")),
            name="scan_fwd",
        )(*ins)

    def call_bwd(q, k, x, a_tok, dt_tok, hsave, dy):
        qs, xs, as_, hs = specs(True)

        def body(*refs):
            if use_dt:
                q_ref, k_ref, x_ref, a_ref, dt_ref, hs_ref, dy_ref, dq_ref, dk_ref, dx_ref, da_ref, ddt_ref, dh_scr = refs
            else:
                q_ref, k_ref, x_ref, a_ref, hs_ref, dy_ref, dq_ref, dk_ref, dx_ref, da_ref, dh_scr = refs

            @pl.when(pl.program_id(1) == 0)
            def _():
                dh_scr[...] = jnp.zeros_like(dh_scr)

            prim = [q_ref[...].astype(f32), k_ref[...].astype(f32), x_ref[...], a_ref[0]]
            if use_dt:
                f = lambda q, k, x, a, dt, h: chunk(q, k, x, a, dt, h)
                prim.append(dt_ref[0])
            else:
                f = lambda q, k, x, a, h: chunk(q, k, x, a, None, h)
            prim.append(hs_ref[0, 0])
            _, vjp = jax.vjp(f, *prim)
            cts = vjp((dy_ref[...], dh_scr[...]))
            dq_ref[...] = cts[0].astype(dq_ref.dtype)
            dk_ref[...] = cts[1].astype(dk_ref.dtype)
            dx_ref[...] = cts[2]
            da_ref[0] = cts[3]
            if use_dt:
                ddt_ref[0] = cts[4]
            dh_scr[...] = cts[-1]

        ins = [q, k, x, a_tok] + ([dt_tok] if use_dt else []) + [hsave, dy]
        a_shape = jax.ShapeDtypeStruct(a_tok.shape, f32)
        return pl.pallas_call(
            body,
            out_shape=[jax.ShapeDtypeStruct(q.shape, q.dtype), jax.ShapeDtypeStruct(k.shape, k.dtype),
                       jax.ShapeDtypeStruct(x.shape, f32), a_shape] + ([a_shape] if use_dt else []),
            grid=(G, nc),
            in_specs=[qs, qs, xs, as_] + ([as_] if use_dt else []) + [hs, xs],
            out_specs=[qs, qs, xs, as_] + ([as_] if use_dt else []),
            scratch_shapes=[pltpu.VMEM((N, Vw), f32)],
            compiler_params=_cparams(("parallel", "arbitrary")),
            name="scan_bwd",
        )(*ins)

    if use_dt:
        @jax.custom_vjp
        def op(q, k, x, a_tok, dt_tok):
            return call_fwd(q, k, x, a_tok, dt_tok)[0]

        def fwd(q, k, x, a_tok, dt_tok):
            y, hsave = call_fwd(q, k, x, a_tok, dt_tok)
            return y, (q, k, x, a_tok, dt_tok, hsave)

        def bwd(res, dy):
            q, k, x, a_tok, dt_tok, hsave = res
            return tuple(call_bwd(q, k, x, a_tok, dt_tok, hsave, dy))

        op.defvjp(fwd, bwd)
        return op(q, k, x, a_tok, dt_tok)

    @jax.custom_vjp
    def op(q, k, x, a_tok):
        return call_fwd(q, k, x, a_tok, None)[0]

    def fwd(q, k, x, a_tok):
        y, hsave = call_fwd(q, k, x, a_tok, None)
        return y, (q, k, x, a_tok, hsave)

    def bwd(res, dy):
        q, k, x, a_tok, hsave = res
        return tuple(call_bwd(q, k, x, a_tok, None, hsave, dy))

    op.defvjp(fwd, bwd)
    return op(q, k, x, a_tok)


def _swap_halves(x, dh):
    W = x.shape[1]
    lane = lax.broadcasted_iota(jnp.int32, (1, W), 1) % dh
    return jnp.where(lane < dh // 2, pltpu.roll(x, W - dh // 2, 1), pltpu.roll(x, dh // 2, 1))


def rotary(rq, rk, cos_t, sin_t):
    scale = RET_DH ** -0.5

    def fn(rq, rk, c, s):
        return rq * c + _swap_halves(rq, RET_DH) * s, (rk * c + _swap_halves(rk, RET_DH) * s) * scale

    def bwd_fn(rv, pv, dos):
        _, _, c, s = rv
        dq, dk = dos
        dk = dk * scale
        return (dq * c + _swap_halves(dq * s, RET_DH), dk * c + _swap_halves(dk * s, RET_DH)), ()

    return rowwise(fn, "rotary", [rq, rk, cos_t, sin_t], [], [MXU_DTYPE, MXU_DTYPE], n_diff_rows=2, bwd_fn=bwd_fn)


def _rope_tables(S, width):
    half = RET_DH // 2
    inv = 1.0 / (ROPE_BASE ** (jnp.arange(half, dtype=f32) / half))
    ang = jnp.arange(S, dtype=f32)[:, None] * inv[None, :]
    cos, sin = jnp.cos(ang), jnp.sin(ang)
    reps = width // RET_DH
    return jnp.tile(jnp.concatenate([cos, cos], axis=1), (1, reps)), jnp.tile(jnp.concatenate([-sin, sin], axis=1), (1, reps))


def _exact_dot(x, m):
    return jnp.dot(x, m, precision=lax.Precision.HIGHEST, preferred_element_type=f32)


def ret_post(y_f, y_b, rg, gn_g):
    W = y_f.shape[1]
    idx = np.arange(W) // RET_DH
    avg = jnp.asarray((idx[:, None] == idx[None, :]).astype(np.float32) / RET_DH)

    def fn(yf, yb, rg, g, avg):
        y = yf + yb
        mu = _exact_dot(y, avg)
        d = y - mu
        var = _exact_dot(d * d, avg)
        return (_silu(rg) * (d * lax.rsqrt(var + EPS) * g),)

    return rowwise(fn, "ret_post", [y_f, y_b, rg], [gn_g.reshape(1, -1), avg], [MXU_DTYPE], n_diff_params=1)[0]


def _na_bias(rpb, win_r):
    H = rpb.shape[0]
    qc = np.arange(GRID_W)[:, None]
    kc = np.arange(GRID_W)[None, :]
    cstart = np.clip(qc - NA_WIN_C // 2, 0, GRID_W - NA_WIN_C)
    valid = (kc >= cstart) & (kc < cstart + NA_WIN_C)
    dc = np.clip(kc - qc, -(NA_WIN_C - 1), NA_WIN_C - 1) + (NA_WIN_C - 1)
    onehot = (dc[None] == np.arange(2 * NA_WIN_C - 1)[:, None, None]).astype(np.float32)
    t1 = jnp.einsum("hrd,dqk->hrqk", rpb.astype(f32), jnp.asarray(onehot), precision=lax.Precision.HIGHEST)
    per_delta = [t1[:, NA_WIN_R - 1 - d:NA_WIN_R - 1 - d + win_r] for d in range(win_r)]
    b = jnp.stack(per_delta, axis=1)
    b = jnp.where(jnp.asarray(valid)[None, None, None], b, NEG_INF)
    return jnp.transpose(b, (0, 1, 3, 2, 4)).reshape(H, win_r, GRID_W, win_r * GRID_W)


def _na_row(q, kw, vw, biases):
    lane = lax.broadcasted_iota(jnp.int32, (1, q.shape[1]), 1) // NA_DH
    o = 0.0
    for i, b in enumerate(biases):
        qi = jnp.where(lane == i, q, 0.0) * (NA_DH ** -0.5)
        s = bdot(qi, kw, "nt") + b
        e = jnp.exp(s - jnp.max(s, axis=1, keepdims=True))
        p = e / jnp.sum(e, axis=1, keepdims=True)
        o = o + jnp.where(lane == i, bdot(p, vw, "nn"), 0.0)
    return o


def na_op(nq, nk, nv, bias):
    S, W = nq.shape
    rows = S // GRID_W
    win_r = bias.shape[1]
    nkeys = win_r * GRID_W
    hp = LANES // NA_DH
    npair = W // LANES
    RB = min(16, rows)
    nrb = rows // RB
    qspec = pl.BlockSpec((RB * GRID_W, LANES), lambda p, r: (r, p))
    kspec = pl.BlockSpec((S, LANES), lambda p, r: (0, p))
    bspec = pl.BlockSpec((hp, win_r, GRID_W, nkeys), lambda p, r: (p, 0, 0, 0))

    def window(r):
        r0 = jnp.clip(r - win_r // 2, 0, rows - win_r)
        return pl.multiple_of(r0 * GRID_W, GRID_W), r - r0

    def call_fwd(nq, nk, nv, bias):
        def body(q_ref, k_ref, v_ref, b_ref, o_ref):
            rb = pl.program_id(1)

            def row(i, c):
                k0, d = window(rb * RB + i)
                q0 = pl.multiple_of(i * GRID_W, GRID_W)
                o = _na_row(q_ref[pl.ds(q0, GRID_W), :].astype(f32), k_ref[pl.ds(k0, nkeys), :], v_ref[pl.ds(k0, nkeys), :],
                            [b_ref[h, pl.ds(d, 1)][0] for h in range(hp)])
                o_ref[pl.ds(q0, GRID_W), :] = o.astype(o_ref.dtype)
                return c

            lax.fori_loop(0, RB, row, 0)

        return pl.pallas_call(
            body,
            out_shape=jax.ShapeDtypeStruct((S, W), nq.dtype),
            grid=(npair, nrb),
            in_specs=[qspec, kspec, kspec, bspec],
            out_specs=qspec,
            compiler_params=_cparams(("parallel", "arbitrary")),
            name="na_fwd",
        )(nq, nk, nv, bias)

    def call_bwd(nq, nk, nv, bias, do):
        def body(q_ref, k_ref, v_ref, b_ref, do_ref, dq_ref, dk_ref, dv_ref, db_ref, dk_acc, dv_acc):
            rb = pl.program_id(1)

            @pl.when(rb == 0)
            def _():
                dk_acc[...] = jnp.zeros_like(dk_acc)
                dv_acc[...] = jnp.zeros_like(dv_acc)
                db_ref[...] = jnp.zeros_like(db_ref)

            def row(i, c):
                k0, d = window(rb * RB + i)
                q0 = pl.multiple_of(i * GRID_W, GRID_W)
                bs = [b_ref[h, pl.ds(d, 1)][0] for h in range(hp)]
                _, vjp = jax.vjp(lambda q, kw, vw, *b: _na_row(q, kw, vw, b), q_ref[pl.ds(q0, GRID_W), :].astype(f32),
                                 k_ref[pl.ds(k0, nkeys), :].astype(f32), v_ref[pl.ds(k0, nkeys), :].astype(f32), *bs)
                cts = vjp(do_ref[pl.ds(q0, GRID_W), :].astype(f32))
                dq_ref[pl.ds(q0, GRID_W), :] = cts[0].astype(dq_ref.dtype)
                dk_acc[pl.ds(k0, nkeys), :] += cts[1]
                dv_acc[pl.ds(k0, nkeys), :] += cts[2]
                for h in range(hp):
                    db_ref[h, pl.ds(d, 1)] += cts[3 + h][None]
                return c

            lax.fori_loop(0, RB, row, 0)

            @pl.when(rb == nrb - 1)
            def _():
                dk_ref[...] = dk_acc[...].astype(dk_ref.dtype)
                dv_ref[...] = dv_acc[...].astype(dv_ref.dtype)

        return pl.pallas_call(
            body,
            out_shape=[jax.ShapeDtypeStruct((S, W), nq.dtype), jax.ShapeDtypeStruct((S, W), nk.dtype),
                       jax.ShapeDtypeStruct((S, W), nv.dtype), jax.ShapeDtypeStruct(bias.shape, f32)],
            grid=(npair, nrb),
            in_specs=[qspec, kspec, kspec, bspec, qspec],
            out_specs=[qspec, kspec, kspec, bspec],
            scratch_shapes=[pltpu.VMEM((S, LANES), f32), pltpu.VMEM((S, LANES), f32)],
            compiler_params=_cparams(("parallel", "arbitrary")),
            name="na_bwd",
        )(nq, nk, nv, bias, do)

    @jax.custom_vjp
    def op(nq, nk, nv, bias):
        return call_fwd(nq, nk, nv, bias)

    def fwd(nq, nk, nv, bias):
        return call_fwd(nq, nk, nv, bias), (nq, nk, nv, bias)

    def bwd(res, do):
        return tuple(call_bwd(*res, do))

    op.defvjp(fwd, bwd)
    return op(nq, nk, nv, bias)


def ssd_dt(dt_raw, dt_bias, a_neg):
    def fn(r, b, a):
        dt = _softplus(r + b)
        return dt, dt * a

    return rowwise(fn, "ssd_dt", [dt_raw], [dt_bias, a_neg], [f32, f32])


def ssd_post(y_f, y_b, xs, z, d_skip_lanes, norm_g, groups):
    def fn(yf, yb, xs, z, dsk, g):
        y = (yf + yb + xs * dsk) * _silu(z)
        return (y * lax.rsqrt(jnp.mean(y * y, axis=-1, keepdims=True) + EPS) * g,)

    return rowwise(fn, "ssd_post", [y_f, y_b, xs, z], [d_skip_lanes.reshape(1, -1), norm_g.reshape(1, -1)], [MXU_DTYPE],
                   ncol=groups)[0]


def _heads_major(t, groups):
    S = t.shape[0]
    return jnp.transpose(t.reshape(S, groups, -1), (1, 0, 2))


def retention_na_mixer(hn, w_in, decay_logit, gn_g, rpb, w_out, tables):
    S = hn.shape[0]
    R = RET_HEADS * RET_DH
    NW = NA_HEADS * NA_DH
    cols = lambda a, b: w_in[:, a:b]
    rq, rk, rv, rg = (mm(hn, cols(j * R, (j + 1) * R)) for j in range(4))
    nq, nk, nv = (mm(hn, cols(4 * R + j * NW, 4 * R + (j + 1) * NW), out_dtype=MXU_DTYPE) for j in range(3))
    qr, kr = rotary(rq, rk, *tables)
    log_gamma = -_softplus(-decay_logit.astype(f32))
    pairs = R // LANES
    hp = LANES // RET_DH
    a_f = jnp.broadcast_to(log_gamma[0].reshape(pairs, 1, hp), (pairs, S, hp))
    a_b = jnp.broadcast_to(log_gamma[1].reshape(pairs, 1, hp), (pairs, S, hp))
    y_f = scan_op(qr, kr, rv, a_f, None, rev=False, incl=True, nsub=hp)
    y_b = scan_op(qr, kr, rv, a_b, None, rev=True, incl=False, nsub=hp)
    ret = ret_post(y_f, y_b, rg, gn_g)
    rows = S // GRID_W
    nao = na_op(nq, nk, nv, _na_bias(rpb, min(NA_WIN_R, rows)))
    return mm(ret, w_out[:R]) + mm(nao, w_out[R:])


def ssd_mixer(hn, w_in, conv_w, conv_b, dt_bias, a_log, d_skip, norm_g, w_out):
    heads = d_skip.shape[0]
    inner = heads * SSD_HEADDIM
    gs = SSD_GROUPS * SSD_STATE
    o_x, o_b, o_c, o_dt = inner, 2 * inner, 2 * inner + gs, 2 * inner + 2 * gs
    z = mm(hn, w_in[:, :inner])
    pre = [mm(hn, w_in[:, a:b]) for a, b in ((o_x, o_b), (o_b, o_c), (o_c, o_dt))]
    dt_raw = mm(hn, w_in[:, o_dt:])
    cw = [conv_w[:, a - inner:b - inner] for a, b in ((o_x, o_b), (o_b, o_c), (o_c, o_dt))]
    cb = [conv_b[a - inner:b - inner] for a, b in ((o_x, o_b), (o_b, o_c), (o_c, o_dt))]
    xs, bm, cm = (silu_op(dwconv(p, w, b)) for p, w, b in zip(pre, cw, cb))
    a_neg = -jnp.exp(a_log.astype(f32)).reshape(1, -1)
    dt, la = ssd_dt(dt_raw, dt_bias.astype(f32).reshape(1, -1), a_neg)
    dt_f, dt_b = _heads_major(dt[:, :heads], SSD_GROUPS), _heads_major(dt[:, heads:], SSD_GROUPS)
    la_f, la_b = _heads_major(la[:, :heads], SSD_GROUPS), _heads_major(la[:, heads:], SSD_GROUPS)
    y_f = scan_op(cm, bm, xs, la_f, dt_f, rev=False, incl=True, nsub=1)
    y_b = scan_op(cm, bm, xs, la_b, dt_b, rev=True, incl=False, nsub=1)
    y = ssd_post(y_f, y_b, xs, z, jnp.repeat(d_skip.astype(f32), SSD_HEADDIM), norm_g, SSD_GROUPS)
    return mm(y, w_out)


def conv_geglu_ffn(hf, w_up, conv_w, conv_b, w_down):
    F = w_down.shape[0]
    gate = dwconv(mm(hf, w_up[:, :F]), conv_w[:, :F], conv_b[:F])
    val = dwconv(mm(hf, w_up[:, F:]), conv_w[:, F:], conv_b[F:])
    return mm(geglu(gate, val), w_down)


def model_loss(x, tgt, big, small, rep):
    S = x.shape[0]
    depth = rep["norm_mix_pre"].shape[0]
    tables = _rope_tables(S, RET_HEADS * RET_DH)
    for layer in range(depth):
        i = layer // 2
        hn = rms(x, rep["norm_mix_pre"][layer], MXU_DTYPE)
        if layer % 2 == 0:
            m = retention_na_mixer(hn, big["ab_w_in"][i], rep["ab_ret_decay_logit"][i], rep["ab_ret_gn_g"][i],
                                   rep["ab_na_rpb"][i], big["ab_w_out"][i], tables)
        else:
            m = ssd_mixer(hn, big["c_w_in"][i], small["c_conv_w"][i], small["c_conv_b"][i], rep["c_dt_bias"][i],
                          rep["c_a_log"][i], rep["c_d_skip"][i], small["c_norm_g"][i], big["c_w_out"][i])
        x = rms_residual(m, rep["norm_mix_post"][layer], x)
        hf = rms(x, rep["norm_ffn_pre"][layer], MXU_DTYPE)
        f = conv_geglu_ffn(hf, big["ffn_w_up"][layer], small["ffn_conv_w"][layer], rep["ffn_conv_b"][layer],
                           big["ffn_w_down"][layer])
        x = rms_residual(f, rep["norm_ffn_post"][layer], x)
    return loss_op(x, tgt)


def _mesh_pos():
    return lax.axis_index("x"), lax.axis_index("y"), lax.axis_index("c")


def gather_chips(local):
    R, Wd = local.shape

    half = R // 2
    CH = COPY_CHUNKS
    q = half // CH

    def body(x_ref, out_ref, send_sems, recv_sems, local_sems):
        x, y, c = _mesh_pos()
        my = 2 * x + y
        chips = [(1 - x, y), (x, 1 - y), (1 - x, 1 - y)]

        def piece(ref, h, j):
            return ref.at[pl.ds(pl.multiple_of(h * half + j * q, PACK_ALIGN), q), :]

        mine = [pltpu.make_async_copy(piece(x_ref, h, j), piece(out_ref.at[my], h, j), local_sems.at[h * CH + j])
                for h in range(2) for j in range(CH)]
        for cp in mine:
            cp.start()

        def copy(k, src, chip, h, j, to):
            return pltpu.make_async_remote_copy(src_ref=src, dst_ref=piece(out_ref.at[chip], h, j), send_sem=send_sems.at[k],
                                                recv_sem=recv_sems.at[k], device_id=to, device_id_type=pl.DeviceIdType.MESH)

        first = [[copy(k * CH + j, piece(x_ref, c, j), my, c, j, (cx, cy, c)) for j in range(CH)]
                 for k, (cx, cy) in enumerate(chips)]
        for j in range(CH):
            for k in range(3):
                first[k][j].start()
        passed = [[copy((3 + k) * CH + j, piece(out_ref.at[2 * cx + cy], c, j), 2 * cx + cy, c, j, (x, y, 1 - c))
                   for j in range(CH)] for k, (cx, cy) in enumerate(chips)]
        for j in range(CH):
            for k, (cx, cy) in enumerate(chips):
                copy(k * CH + j, piece(x_ref, c, j), 2 * cx + cy, c, j, (cx, cy, c)).wait_recv()
                passed[k][j].start()
        for j in range(CH):
            for k, (cx, cy) in enumerate(chips):
                copy((3 + k) * CH + j, piece(x_ref, c, j), 2 * cx + cy, 1 - c, j, (x, y, 1 - c)).wait_recv()
        for k in range(3):
            for cp in first[k] + passed[k]:
                cp.wait_send()
        for cp in mine:
            cp.wait()

    return pl.pallas_call(
        body,
        out_shape=jax.ShapeDtypeStruct((N_CHIPS, R, Wd), local.dtype),
        in_specs=[pl.BlockSpec(memory_space=pl.ANY)],
        out_specs=pl.BlockSpec(memory_space=pl.ANY),
        scratch_shapes=[pltpu.SemaphoreType.DMA((6 * CH,)), pltpu.SemaphoreType.DMA((6 * CH,)),
                        pltpu.SemaphoreType.DMA((2 * CH,))],
        name="gather_chips",
    )(local)


def pair_swap(parts):
    n, R, Wd = parts.shape
    half = R // 2

    CH = COPY_CHUNKS
    q = half // CH

    def body(p_ref, own_ref, got_ref, send_sems, recv_sems, local_sems):
        x, y, c = _mesh_pos()

        def src(s, h, j):
            return p_ref.at[s, pl.ds(pl.multiple_of(h * half + j * q, PACK_ALIGN), q), :]

        def dst(ref, s, j):
            return ref.at[s, pl.ds(j * q, q), :]

        own = [pltpu.make_async_copy(src(s, c, j), dst(own_ref, s, j), local_sems.at[s * CH + j])
               for s in range(n) for j in range(CH)]
        swap = [pltpu.make_async_remote_copy(src_ref=src(s, 1 - c, j), dst_ref=dst(got_ref, s, j), send_sem=send_sems.at[s * CH + j],
                                             recv_sem=recv_sems.at[s * CH + j], device_id=(x, y, 1 - c),
                                             device_id_type=pl.DeviceIdType.MESH) for s in range(n) for j in range(CH)]
        for cp in swap + own:
            cp.start()
        for cp in swap + own:
            cp.wait()

    shape = jax.ShapeDtypeStruct((n, half, Wd), parts.dtype)
    return pl.pallas_call(
        body,
        out_shape=[shape, shape],
        in_specs=[pl.BlockSpec(memory_space=pl.ANY)],
        out_specs=[pl.BlockSpec(memory_space=pl.ANY)] * 2,
        scratch_shapes=[pltpu.SemaphoreType.DMA((n * CH,)), pltpu.SemaphoreType.DMA((n * CH,)), pltpu.SemaphoreType.DMA((n * CH,))],
        name="pair_swap",
    )(parts)


def chip_exchange(parts):
    n, R, Wd = parts.shape

    CH = COPY_CHUNKS
    q = R // CH

    def body(p_ref, out_ref, send_sems, recv_sems, local_sems):
        x, y, c = _mesh_pos()
        my = 2 * x + y
        chips = [(1 - x, y), (x, 1 - y), (1 - x, 1 - y)]
        mine = [pltpu.make_async_copy(p_ref.at[my, pl.ds(j * q, q), :], out_ref.at[my, pl.ds(j * q, q), :], local_sems.at[j])
                for j in range(CH)]
        for cp in mine:
            cp.start()

        def copy(k, src_slot, dst_slot, to):
            return pltpu.make_async_remote_copy(src_ref=p_ref.at[src_slot], dst_ref=out_ref.at[dst_slot], send_sem=send_sems.at[k],
                                                recv_sem=recv_sems.at[k], device_id=to, device_id_type=pl.DeviceIdType.MESH)

        sends = [copy(k, 2 * cx + cy, my, (cx, cy, c)) for k, (cx, cy) in enumerate(chips)]
        for cp in sends:
            cp.start()
        for k, (cx, cy) in enumerate(chips):
            copy(k, my, 2 * cx + cy, (cx, cy, c)).wait_recv()
        for cp in sends:
            cp.wait_send()
        for cp in mine:
            cp.wait()

    return pl.pallas_call(
        body,
        out_shape=jax.ShapeDtypeStruct((n, R, Wd), parts.dtype),
        in_specs=[pl.BlockSpec(memory_space=pl.ANY)],
        out_specs=pl.BlockSpec(memory_space=pl.ANY),
        scratch_shapes=[pltpu.SemaphoreType.DMA((3,)), pltpu.SemaphoreType.DMA((3,)), pltpu.SemaphoreType.DMA((CH,))],
        name="chip_exchange",
    )(parts)


def pair_share(mine):
    R, Wd = mine.shape

    CH = 2 * COPY_CHUNKS
    q = R // CH

    def body(m_ref, out_ref, send_sems, recv_sems, local_sems):
        x, y, c = _mesh_pos()

        def copy(j, slot):
            return pltpu.make_async_remote_copy(src_ref=m_ref.at[pl.ds(j * q, q), :], dst_ref=out_ref.at[slot, pl.ds(j * q, q), :],
                                                send_sem=send_sems.at[j], recv_sem=recv_sems.at[j], device_id=(x, y, 1 - c),
                                                device_id_type=pl.DeviceIdType.MESH)

        own = [pltpu.make_async_copy(m_ref.at[pl.ds(j * q, q), :], out_ref.at[c, pl.ds(j * q, q), :], local_sems.at[j])
               for j in range(CH)]
        sends = [copy(j, c) for j in range(CH)]
        for cp in sends + own:
            cp.start()
        for j in range(CH):
            copy(j, 1 - c).wait_recv()
        for cp in sends:
            cp.wait_send()
        for cp in own:
            cp.wait()

    return pl.pallas_call(
        body,
        out_shape=jax.ShapeDtypeStruct((2, R, Wd), mine.dtype),
        in_specs=[pl.BlockSpec(memory_space=pl.ANY)],
        out_specs=pl.BlockSpec(memory_space=pl.ANY),
        scratch_shapes=[pltpu.SemaphoreType.DMA((CH,)), pltpu.SemaphoreType.DMA((CH,)), pltpu.SemaphoreType.DMA((CH,))],
        name="pair_share",
    )(mine)


def sum_slots(recv, out_dtype=f32):
    n, R, Wd = recv.shape
    tr = _pick(R, (512, 256, 128, 64, 32, 16, 8))

    def body(r_ref, o_ref):
        acc = r_ref[0].astype(f32)
        for j in range(1, n):
            acc = acc + r_ref[j].astype(f32)
        o_ref[...] = acc.astype(o_ref.dtype)

    return pl.pallas_call(
        body,
        out_shape=jax.ShapeDtypeStruct((R, Wd), out_dtype),
        grid=(R // tr,),
        in_specs=[pl.BlockSpec((n, tr, Wd), lambda i: (0, i, 0))],
        out_specs=pl.BlockSpec((tr, Wd), lambda i: (i, 0)),
        compiler_params=_cparams(("parallel",)),
        name="sum_slots",
    )(recv)


def add_pair(a, b):
    n, R, Wd = a.shape
    tr = _pick(R, (512, 256, 128, 64, 32, 16, 8))

    def body(a_ref, b_ref, o_ref):
        o_ref[...] = (a_ref[...].astype(f32) + b_ref[...].astype(f32)).astype(o_ref.dtype)

    spec = pl.BlockSpec((1, tr, Wd), lambda s, i: (s, i, 0))
    return pl.pallas_call(
        body,
        out_shape=jax.ShapeDtypeStruct(a.shape, a.dtype),
        grid=(n, R // tr),
        in_specs=[spec, spec],
        out_specs=spec,
        compiler_params=_cparams(("parallel", "parallel")),
        name="add_pair",
    )(a, b)


def reduce_scatter(parts):
    n, R, Wd = parts.shape
    own, got = pair_swap(parts)
    chip_sum = add_pair(own, got)
    mine = sum_slots(chip_exchange(chip_sum))
    return pair_share(mine).reshape(R, Wd)


def adamw(w, g, m, v):
    R, C = w.shape
    tr = R
    for cand in (512, 256, 128, 64, 32, 16, 8):
        if R % cand == 0 and cand * C * 4 <= (1 << 20):
            tr = cand
            break

    def body(w_ref, g_ref, m_ref, v_ref, d_ref, mo_ref, vo_ref):
        g = g_ref[...]
        m = ADAM_B1 * m_ref[...] + (1.0 - ADAM_B1) * g
        v = ADAM_B2 * v_ref[...] + (1.0 - ADAM_B2) * (g * g)
        m_hat = m / (1.0 - ADAM_B1 ** ADAM_STEP)
        v_hat = v / (1.0 - ADAM_B2 ** ADAM_STEP)
        d_ref[...] = -ADAM_LR * (m_hat / (jnp.sqrt(v_hat) + ADAM_EPS) + ADAM_WD * w_ref[...])
        mo_ref[...] = m
        vo_ref[...] = v

    spec = pl.BlockSpec((tr, C), lambda i: (i, 0))
    return pl.pallas_call(
        body,
        out_shape=[jax.ShapeDtypeStruct((R, C), f32)] * 3,
        grid=(R // tr,),
        in_specs=[spec] * 4,
        out_specs=[spec] * 3,
        compiler_params=_cparams(("parallel",)),
        name="adamw",
    )(w, g, m, v)


def _pack(arrs, dtype):
    flat = jnp.concatenate([a.astype(dtype).reshape(-1) for a in arrs])
    n = flat.shape[0]
    unit = PACK_W * PACK_ROWS
    padded = -(-n // unit) * unit
    return jnp.pad(flat, (0, padded - n)).reshape(-1, PACK_W)


def _unpack(buf, shapes):
    flat = buf.reshape(-1)
    out, off = [], 0
    for s in shapes:
        n = int(np.prod(s))
        out.append(flat[off:off + n].reshape(s))
        off += n
    return out


BIG = (("ab_w_in", 2), ("ab_w_out", 1), ("c_w_in", 2), ("c_w_out", 1), ("ffn_w_up", 2), ("ffn_w_down", 1))
SMALL = (("c_conv_w", 2), ("c_conv_b", 1), ("c_norm_g", 1), ("ffn_conv_w", 2))
REP = ("norm_mix_pre", "norm_mix_post", "norm_ffn_pre", "norm_ffn_post", "ab_ret_decay_logit", "ab_ret_gn_g", "ab_na_rpb",
       "c_dt_bias", "c_a_log", "c_d_skip", "ffn_conv_b")
WEIGHTS = ("norm_mix_pre", "norm_mix_post", "norm_ffn_pre", "norm_ffn_post", "ab_w_in", "ab_ret_decay_logit", "ab_ret_gn_g",
           "ab_na_rpb", "ab_w_out", "c_w_in", "c_conv_w", "c_conv_b", "c_dt_bias", "c_a_log", "c_d_skip", "c_norm_g", "c_w_out",
           "ffn_w_up", "ffn_conv_w", "ffn_conv_b", "ffn_w_down")


def _gather_set(local, spec, dtype):
    shapes = [local[n].shape for n, _ in spec]
    got = gather_chips(_pack([local[n] for n, _ in spec], dtype))
    per_chip = [_unpack(got[s], shapes) for s in range(N_CHIPS)]
    return {n: jnp.concatenate([per_chip[s][j] for s in range(N_CHIPS)], axis=ax) for j, (n, ax) in enumerate(spec)}


def _scatter_parts(full, spec, extra, dtype):
    split = {n: jnp.split(full[n], N_CHIPS, axis=ax) for n, ax in spec}
    return jnp.stack([_pack([split[n][s] for n, _ in spec] + list(extra), dtype) for s in range(N_CHIPS)])


def kernel(x, norm_mix_pre, norm_mix_post, norm_ffn_pre, norm_ffn_post, ab_w_in, ab_ret_decay_logit, ab_ret_gn_g, ab_na_rpb, ab_w_out, c_w_in, c_conv_w, c_conv_b, c_dt_bias, c_a_log, c_d_skip, c_norm_g, c_w_out, ffn_w_up, ffn_conv_w, ffn_conv_b, ffn_w_down, loss_target, m_norm_mix_pre, m_norm_mix_post, m_norm_ffn_pre, m_norm_ffn_post, m_ab_w_in, m_ab_ret_decay_logit, m_ab_ret_gn_g, m_ab_na_rpb, m_ab_w_out, m_c_w_in, m_c_conv_w, m_c_conv_b, m_c_dt_bias, m_c_a_log, m_c_d_skip, m_c_norm_g, m_c_w_out, m_ffn_w_up, m_ffn_conv_w, m_ffn_conv_b, m_ffn_w_down, v_norm_mix_pre, v_norm_mix_post, v_norm_ffn_pre, v_norm_ffn_post, v_ab_w_in, v_ab_ret_decay_logit, v_ab_ret_gn_g, v_ab_na_rpb, v_ab_w_out, v_c_w_in, v_c_conv_w, v_c_conv_b, v_c_dt_bias, v_c_a_log, v_c_d_skip, v_c_norm_g, v_c_w_out, v_ffn_w_up, v_ffn_conv_w, v_ffn_conv_b, v_ffn_w_down):
    args = dict(locals())
    w = {n: args[n] for n in WEIGHTS}
    mom = {n: args["m_" + n] for n in WEIGHTS}
    var = {n: args["v_" + n] for n in WEIGHTS}

    big = _gather_set(w, BIG, MXU_DTYPE)
    small = _gather_set(w, SMALL, f32)
    rep = {n: w[n] for n in REP}

    def loss_fn(xs, big, small, rep):
        return model_loss(xs, loss_target[0], big, small, rep)

    loss, (gx, gbig, gsmall, grep) = jax.value_and_grad(loss_fn, argnums=(0, 1, 2, 3))(x[0], big, small, rep)
    loss = lax.psum(loss, ("x", "y", "c"))

    big_shapes = [w[n].shape for n, _ in BIG]
    small_shapes = [w[n].shape for n, _ in SMALL] + [w[n].shape for n in REP]
    g_big = _unpack(reduce_scatter(_scatter_parts(gbig, BIG, (), MXU_DTYPE)), big_shapes)
    g_small_buf = reduce_scatter(_scatter_parts(gsmall, SMALL, [grep[n] for n in REP], f32))
    grads = dict(zip([n for n, _ in BIG], g_big))
    small_names = [n for n, _ in SMALL] + list(REP)
    grads.update(zip(small_names, _unpack(g_small_buf, small_shapes)))

    delta, new_m, new_v = {}, {}, {}
    for n, _ in BIG:
        shp = w[n].shape
        two_d = lambda a: a.reshape(-1, shp[-1])
        d, m2, v2 = adamw(two_d(w[n]), two_d(grads[n]), two_d(mom[n]), two_d(var[n]))
        delta[n], new_m[n], new_v[n] = d.reshape(shp), m2.reshape(shp), v2.reshape(shp)
    pk = lambda src: _pack([src[n] for n in small_names], f32)
    d, m2, v2 = adamw(pk(w), g_small_buf, pk(mom), pk(var))
    for dst, buf in ((delta, d), (new_m, m2), (new_v, v2)):
        dst.update(zip(small_names, _unpack(buf, small_shapes)))

    return (loss, gx[None], *[grads[n] for n in WEIGHTS], *[delta[n] for n in WEIGHTS],
            *[new_m[n] for n in WEIGHTS], *[new_v[n] for n in WEIGHTS])
```

```python
import functools
import math

import numpy as np
import jax
import jax.numpy as jnp
from jax import lax
from jax.experimental import pallas as pl
from jax.experimental.pallas import tpu as pltpu

f32 = jnp.float32
bf16 = jnp.bfloat16
MXU_DTYPE = bf16

GRID_W = 64
CHUNK = 128
EPS = 1e-6
RET_HEADS = 8
RET_DH = 64
ROPE_BASE = 10000.0
NA_HEADS = 8
NA_DH = 64
NA_WIN_R = 8
NA_WIN_C = 16
SSD_HEADDIM = 64
SSD_GROUPS = 4
SSD_STATE = 128
ADAM_LR = 0.001
ADAM_B1 = 0.9
ADAM_B2 = 0.999
ADAM_EPS = 1e-08
ADAM_WD = 0.01
ADAM_STEP = 10

LANES = 128
HEAD_W = 64
PACK_W = 512
PACK_ROWS = 1024
PACK_ALIGN = 16
COPY_CHUNKS = 4
VMEM_LIMIT = 56 * 1024 * 1024
MM_BLOCK_BYTES = 6 * 1024 * 1024
N_CHIPS = 4
N_DEV = 8
NEG_INF = -1e30

_DIMS = {"nn": (((1,), (0,)), ((), ())), "nt": (((1,), (1,)), ((), ())), "tn": (((0,), (0,)), ((), ()))}


def _cparams(sem=None):
    return pltpu.CompilerParams(dimension_semantics=sem, vmem_limit_bytes=VMEM_LIMIT)


def _pick(dim, cands):
    for c in cands:
        if dim % c == 0:
            return c
    return dim


def _divisor_tile(dim, fits, align):
    for d in range(1, dim + 1):
        t = dim // d
        if dim % d == 0 and t % align == 0 and fits(t):
            return t
    return dim


def _bdot_raw(a, b, mode):
    return lax.dot_general(a.astype(MXU_DTYPE), b.astype(MXU_DTYPE), _DIMS[mode], preferred_element_type=f32)


@functools.partial(jax.custom_vjp, nondiff_argnums=(2,))
def bdot(a, b, mode):
    return _bdot_raw(a, b, mode)


def _bdot_fwd(a, b, mode):
    return _bdot_raw(a, b, mode), (a, b)


def _bdot_bwd(mode, res, g):
    a, b = res
    if mode == "nn":
        da, db = _bdot_raw(g, b, "nt"), _bdot_raw(a, g, "tn")
    elif mode == "nt":
        da, db = _bdot_raw(g, b, "nn"), _bdot_raw(g, a, "tn")
    else:
        da, db = _bdot_raw(b, g, "nt"), _bdot_raw(a, g, "nn")
    return da.astype(a.dtype), db.astype(b.dtype)


bdot.defvjp(_bdot_fwd, _bdot_bwd)


def _mm_call(a, b, mode, out_dtype):
    if mode == "nn":
        (M, K), (K2, N) = a.shape, b.shape
    elif mode == "nt":
        (M, K), (N, K2) = a.shape, b.shape
    else:
        (K, M), (K2, N) = a.shape, b.shape
    assert K == K2, (a.shape, b.shape, mode)
    a_bytes, b_bytes, o_bytes = a.dtype.itemsize, b.dtype.itemsize, jnp.dtype(out_dtype).itemsize
    if mode == "tn":
        tk = _pick(K, (512, 256, 128))
        tn = _divisor_tile(N, lambda t: t <= 1536, LANES)
        tm = _divisor_tile(M, lambda t: t * tn * 4 <= MM_BLOCK_BYTES, 8)
    else:
        tk, tn = K, N
        tm = _divisor_tile(M, lambda t: t * K * a_bytes <= MM_BLOCK_BYTES and t * N * o_bytes <= MM_BLOCK_BYTES, 8)
    nk = K // tk
    if mode == "nn":
        a_spec = pl.BlockSpec((tm, tk), lambda i, j, k: (i, k))
        b_spec = pl.BlockSpec((tk, tn), lambda i, j, k: (k, j))
    elif mode == "nt":
        a_spec = pl.BlockSpec((tm, tk), lambda i, j, k: (i, k))
        b_spec = pl.BlockSpec((tn, tk), lambda i, j, k: (j, k))
    else:
        a_spec = pl.BlockSpec((tk, tm), lambda i, j, k: (k, i))
        b_spec = pl.BlockSpec((tk, tn), lambda i, j, k: (k, j))

    if nk == 1:
        def body(a_ref, b_ref, o_ref):
            o_ref[...] = _bdot_raw(a_ref[...], b_ref[...], mode).astype(o_ref.dtype)
    else:
        def body(a_ref, b_ref, o_ref, acc_ref):
            k = pl.program_id(2)

            @pl.when(k == 0)
            def _():
                acc_ref[...] = jnp.zeros_like(acc_ref)

            acc_ref[...] += _bdot_raw(a_ref[...], b_ref[...], mode)

            @pl.when(k == nk - 1)
            def _():
                o_ref[...] = acc_ref[...].astype(o_ref.dtype)

    return pl.pallas_call(
        body,
        out_shape=jax.ShapeDtypeStruct((M, N), out_dtype),
        grid=(M // tm, N // tn, nk),
        in_specs=[a_spec, b_spec],
        out_specs=pl.BlockSpec((tm, tn), lambda i, j, k: (i, j)),
        scratch_shapes=[pltpu.VMEM((tm, tn), f32)] if nk > 1 else [],
        compiler_params=_cparams(("parallel", "parallel", "arbitrary")),
        name="mm_" + mode,
    )(a, b)


def mm(a, b, mode="nn", out_dtype=f32):
    @jax.custom_vjp
    def op(a, b):
        return _mm_call(a, b, mode, out_dtype)

    def fwd(a, b):
        return _mm_call(a, b, mode, out_dtype), (a, b)

    def bwd(res, g):
        a, b = res
        if mode == "nn":
            return _mm_call(g, b, "nt", a.dtype), _mm_call(a, g, "tn", b.dtype)
        if mode == "nt":
            return _mm_call(g, b, "nn", a.dtype), _mm_call(g, a, "tn", b.dtype)
        return _mm_call(b, g, "nt", a.dtype), _mm_call(a, g, "nn", b.dtype)

    op.defvjp(fwd, bwd)
    return op(a, b)


def _row_tile(S, row_bytes):
    tm = 512
    while tm > 8 and (tm * row_bytes > (6 << 20) or S % tm):
        tm //= 2
    return tm


def rowwise(fn, name, rows, params, out_dtypes, n_diff_rows=None, n_diff_params=None, ncol=1, bwd_fn=None):
    rows, params = list(rows), list(params)
    nr, npar = len(rows), len(params)
    ndr = nr if n_diff_rows is None else n_diff_rows
    ndp = npar if n_diff_params is None else n_diff_params
    S = rows[0].shape[0]
    rw = [r.shape[1] // ncol for r in rows]
    pshape = [(p.shape[0], p.shape[1] // ncol) for p in params]

    def block_structs(tm):
        return ([jax.ShapeDtypeStruct((tm, w), f32) for w in rw] + [jax.ShapeDtypeStruct(s, f32) for s in pshape])

    outs_s = jax.eval_shape(fn, *block_structs(8))
    ow = [o.shape[1] for o in outs_s]
    nout = len(ow)
    row_bytes = 4 * (sum(rw) * 2 + sum(ow) * 2)
    tm = _row_tile(S, row_bytes)
    grid = (ncol, S // tm)

    def rspec(w):
        return pl.BlockSpec((tm, w), lambda g, i: (i, g))

    def pspec(s):
        return pl.BlockSpec(s, lambda g, i: (0, g))

    def call_fwd(*args):
        def body(*refs):
            vals = [r[...].astype(f32) for r in refs[:nr + npar]]
            res = fn(*vals)
            for o, r in zip(refs[nr + npar:], res):
                o[...] = r.astype(o.dtype)

        return pl.pallas_call(
            body,
            out_shape=[jax.ShapeDtypeStruct((S, w * ncol), dt) for w, dt in zip(ow, out_dtypes)],
            grid=grid,
            in_specs=[rspec(w) for w in rw] + [pspec(s) for s in pshape],
            out_specs=[rspec(w) for w in ow],
            compiler_params=_cparams(("parallel", "parallel")),
            name=name + "_fwd",
        )(*args)

    def call_bwd(args, douts):
        def body(*refs):
            in_refs = refs[:nr + npar]
            do_refs = refs[nr + npar:nr + npar + nout]
            dr_refs = refs[nr + npar + nout:nr + npar + nout + ndr]
            dp_refs = refs[nr + npar + nout + ndr:]
            rv = [r[...] for r in in_refs[:nr]]
            pv = [r[...] for r in in_refs[nr:]]
            dos = [d[...].astype(f32) for d in do_refs]
            if bwd_fn is not None:
                drs, dps = bwd_fn(rv, pv, dos)
            else:
                def f(*a):
                    return fn(*a[:ndr], *rv[ndr:], *a[ndr:], *pv[ndp:])

                _, vjp = jax.vjp(f, *[v.astype(f32) for v in rv[:ndr]], *pv[:ndp])
                cts = vjp(tuple(dos))
                drs, dps = cts[:ndr], cts[ndr:]
            for r, ct in zip(dr_refs, drs):
                r[...] = ct.astype(r.dtype)
            if ndp:
                @pl.when(pl.program_id(1) == 0)
                def _():
                    for r in dp_refs:
                        r[...] = jnp.zeros_like(r)

                for r, ct in zip(dp_refs, dps):
                    r[...] += ct

        return pl.pallas_call(
            body,
            out_shape=[jax.ShapeDtypeStruct(r.shape, r.dtype) for r in rows[:ndr]]
            + [jax.ShapeDtypeStruct(p.shape, f32) for p in params[:ndp]],
            grid=grid,
            in_specs=[rspec(w) for w in rw] + [pspec(s) for s in pshape] + [rspec(w) for w in ow],
            out_specs=[rspec(w) for w in rw[:ndr]] + [pspec(s) for s in pshape[:ndp]],
            compiler_params=_cparams(("parallel", "arbitrary")),
            name=name + "_bwd",
        )(*args, *douts)

    @jax.custom_vjp
    def op(*args):
        return tuple(call_fwd(*args))

    def fwd(*args):
        return tuple(call_fwd(*args)), args

    def bwd(args, douts):
        res = call_bwd(args, douts)
        drs, dps = res[:ndr], res[ndr:]
        out = list(drs) + [jnp.zeros_like(a) for a in args[ndr:nr]]
        out += [dp.astype(p.dtype) for dp, p in zip(dps, args[nr:nr + ndp])]
        out += [jnp.zeros_like(a) for a in args[nr + ndp:]]
        return tuple(out)

    op.defvjp(fwd, bwd)
    return op(*rows, *params)


def _silu(x):
    return x * (1.0 / (1.0 + jnp.exp(-x)))


def _softplus(x):
    return jnp.maximum(x, 0.0) + jnp.log(1.0 + jnp.exp(-jnp.abs(x)))


def _gelu_tanh(x):
    return 0.5 * x * (1.0 + jnp.tanh(math.sqrt(2.0 / math.pi) * (x + 0.044715 * (x * x * x))))


def _rms_fn(x, g):
    return x * lax.rsqrt(jnp.mean(x * x, axis=-1, keepdims=True) + EPS) * g


def rms(x, g, out_dtype):
    return rowwise(lambda x, g: (_rms_fn(x, g),), "rms", [x], [g.reshape(1, -1)], [out_dtype])[0]


def rms_residual(m, g, x):
    return rowwise(lambda m, x, g: (x + _rms_fn(m, g),), "rms_res", [m, x], [g.reshape(1, -1)], [f32])[0]


def silu_op(x):
    return rowwise(lambda x: (_silu(x),), "silu", [x], [], [f32])[0]


def geglu(gate, val):
    return rowwise(lambda g, v: (_gelu_tanh(g) * v,), "geglu", [gate, val], [], [MXU_DTYPE])[0]


def loss_op(y, tgt):
    S, D = y.shape
    tm = _row_tile(S, 4 * D * 4)

    def call_fwd(y, tgt):
        def body(y_ref, t_ref, o_ref):
            @pl.when(pl.program_id(0) == 0)
            def _():
                o_ref[...] = jnp.zeros_like(o_ref)

            e = y_ref[...] - t_ref[...]
            o_ref[...] += 0.5 * jnp.sum(jnp.mean(e * e, axis=-1, keepdims=True))

        out = pl.pallas_call(
            body,
            out_shape=jax.ShapeDtypeStruct((8, LANES), f32),
            grid=(S // tm,),
            in_specs=[pl.BlockSpec((tm, D), lambda i: (i, 0))] * 2,
            out_specs=pl.BlockSpec((8, LANES), lambda i: (0, 0)),
            compiler_params=_cparams(("arbitrary",)),
            name="loss_fwd",
        )(y, tgt)
        return out[0, 0]

    def call_bwd(y, tgt, g):
        def body(y_ref, t_ref, g_ref, o_ref):
            o_ref[...] = (y_ref[...] - t_ref[...]) * (g_ref[...] * (1.0 / D))

        return pl.pallas_call(
            body,
            out_shape=jax.ShapeDtypeStruct((S, D), f32),
            grid=(S // tm,),
            in_specs=[pl.BlockSpec((tm, D), lambda i: (i, 0))] * 2 + [pl.BlockSpec((1, 1), lambda i: (0, 0))],
            out_specs=pl.BlockSpec((tm, D), lambda i: (i, 0)),
            compiler_params=_cparams(("parallel",)),
            name="loss_bwd",
        )(y, tgt, g.reshape(1, 1).astype(f32))

    @jax.custom_vjp
    def op(y, tgt):
        return call_fwd(y, tgt)

    def fwd(y, tgt):
        return call_fwd(y, tgt), (y, tgt)

    def bwd(res, g):
        y, tgt = res
        return call_bwd(y, tgt, g), jnp.zeros_like(tgt)

    op.defvjp(fwd, bwd)
    return op(y, tgt)


HALO = 8


def _conv_tile(S, R):
    def ext(ref, r0):
        cur = ref[pl.ds(r0, R), :]
        prev = ref[pl.ds(pl.multiple_of(jnp.maximum(r0 - HALO, 0), HALO), HALO), :]
        nxt = ref[pl.ds(pl.multiple_of(jnp.minimum(r0 + R, S - HALO), HALO), HALO), :]
        prev = jnp.where(r0 > 0, prev, 0.0)
        nxt = jnp.where(r0 + R < S, nxt, 0.0)
        return jnp.concatenate([prev, cur, nxt], axis=0)

    return ext


def _shift_rows(e, k, R):
    n = e.shape[0]
    if k == 0:
        return e[HALO:HALO + R]
    return pltpu.roll(e, (-k) % n, 0)[HALO:HALO + R]


def dwconv(x, w, b):
    S, C = x.shape
    W = w.shape[0]
    pad = W // 2
    bw = _pick(C, (LANES,))
    R = _pick(S, (256, 128, 64, 32, 16, 8))
    nt = S // R
    ext = _conv_tile(S, R)
    b2 = b.reshape(1, C)

    def call_fwd(x, w, b2):
        def body(x_ref, w_ref, b_ref, y_ref):
            wv = [w_ref[j:j + 1, :] for j in range(W)]
            bv = b_ref[...]

            def tile(i, c):
                r0 = pl.multiple_of(i * R, R)
                e = ext(x_ref, r0)
                acc = bv + wv[pad] * e[HALO:HALO + R]
                for j in range(W):
                    if j != pad:
                        acc = acc + wv[j] * _shift_rows(e, j - pad, R)
                y_ref[pl.ds(r0, R), :] = acc
                return c

            lax.fori_loop(0, nt, tile, 0)

        return pl.pallas_call(
            body,
            out_shape=jax.ShapeDtypeStruct((S, C), f32),
            grid=(C // bw,),
            in_specs=[pl.BlockSpec((S, bw), lambda j: (0, j)), pl.BlockSpec((W, bw), lambda j: (0, j)),
                      pl.BlockSpec((1, bw), lambda j: (0, j))],
            out_specs=pl.BlockSpec((S, bw), lambda j: (0, j)),
            compiler_params=_cparams(("parallel",)),
            name="dwconv_fwd",
        )(x, w, b2)

    def call_bwd(x, w, dy):
        def body(x_ref, w_ref, dy_ref, dx_ref, dw_ref, db_ref):
            wv = [w_ref[j:j + 1, :] for j in range(W)]

            def tile(i, carry):
                dws, db = carry
                r0 = pl.multiple_of(i * R, R)
                ex = ext(x_ref, r0)
                ed = ext(dy_ref, r0)
                d0 = ed[HALO:HALO + R]
                acc = jnp.zeros((R, bw), f32)
                new = []
                for j in range(W):
                    acc = acc + wv[j] * _shift_rows(ed, pad - j, R)
                    new.append(dws[j] + jnp.sum(d0 * _shift_rows(ex, j - pad, R), axis=0, keepdims=True))
                dx_ref[pl.ds(r0, R), :] = acc
                return tuple(new), db + jnp.sum(d0, axis=0, keepdims=True)

            z = jnp.zeros((1, bw), f32)
            dws, db = lax.fori_loop(0, nt, tile, (tuple(z for _ in range(W)), z))
            dw_ref[...] = jnp.zeros_like(dw_ref)
            for j in range(W):
                dw_ref[j:j + 1, :] = dws[j]
            db_ref[...] = db

        return pl.pallas_call(
            body,
            out_shape=[jax.ShapeDtypeStruct((S, C), f32), jax.ShapeDtypeStruct((8, C), f32),
                       jax.ShapeDtypeStruct((1, C), f32)],
            grid=(C // bw,),
            in_specs=[pl.BlockSpec((S, bw), lambda j: (0, j)), pl.BlockSpec((W, bw), lambda j: (0, j)),
                      pl.BlockSpec((S, bw), lambda j: (0, j))],
            out_specs=[pl.BlockSpec((S, bw), lambda j: (0, j)), pl.BlockSpec((8, bw), lambda j: (0, j)),
                       pl.BlockSpec((1, bw), lambda j: (0, j))],
            compiler_params=_cparams(("parallel",)),
            name="dwconv_bwd",
        )(x, w, dy)

    @jax.custom_vjp
    def op(x, w, b2):
        return call_fwd(x, w, b2)

    def fwd(x, w, b2):
        return call_fwd(x, w, b2), (x, w)

    def bwd(res, dy):
        x, w = res
        dx, dw, db = call_bwd(x, w, dy)
        return dx, dw[:W], db

    op.defvjp(fwd, bwd)
    return op(x, w, b2)


def _scan_chunk(q, k, x, a_tok, dt_tok, h, *, rev, incl, nsub, head0):
    L, N = q.shape
    W = x.shape[1]
    nh = W // HEAD_W
    t = lax.broadcasted_iota(jnp.int32, (L, L), 0)
    l = lax.broadcasted_iota(jnp.int32, (L, L), 1)
    if rev:
        cm, cmT = l >= t, t >= l
        mask = cm if incl else l > t
    else:
        cm, cmT = l <= t, t <= l
        mask = cm if incl else l < t
    eye = t == l
    lane_a = lax.broadcasted_iota(jnp.int32, a_tok.shape, 1)
    vhead = lax.broadcasted_iota(jnp.int32, (1, W), 1) // HEAD_W
    qhead = lax.broadcasted_iota(jnp.int32, (1, N), 1) // (N // nsub)

    decay, lam_e, tau_e, gam_e, dt_e = [], 0.0, 0.0, 0.0, 0.0
    for i in range(nh):
        a_col = jnp.sum(jnp.where(lane_a == head0 + i, a_tok, 0.0), axis=1, keepdims=True)
        a_row = jnp.sum(jnp.where(eye, a_col, 0.0), axis=0, keepdims=True)
        cs_col = jnp.sum(jnp.where(cm, a_row, 0.0), axis=1, keepdims=True)
        cs_row = jnp.sum(jnp.where(cmT, a_col, 0.0), axis=0, keepdims=True)
        tot = jnp.sum(a_col, axis=0, keepdims=True)
        decay.append(jnp.where(mask, jnp.exp(jnp.where(mask, cs_col - cs_row, 0.0)), 0.0))
        sel = vhead == i
        lam_e = lam_e + jnp.where(sel, jnp.exp(cs_col), 0.0)
        tau_e = tau_e + jnp.where(sel, jnp.exp(tot - cs_col), 0.0)
        gam_e = gam_e + jnp.where(sel, jnp.exp(tot), 0.0)
        if dt_tok is not None:
            dt_col = jnp.sum(jnp.where(lane_a == head0 + i, dt_tok, 0.0), axis=1, keepdims=True)
            dt_e = dt_e + jnp.where(sel, dt_col, 0.0)
    v = x if dt_tok is None else x * dt_e
    s_shared = bdot(q, k, "nt") if nsub == 1 else None
    y = lam_e * bdot(q, h, "nn")
    for i in range(nh):
        s = s_shared if nsub == 1 else bdot(jnp.where(qhead == i, q, 0.0), k, "nt")
        y = y + jnp.where(vhead == i, bdot(s * decay[i], v, "nn"), 0.0)
    hn = gam_e * h + bdot(k, tau_e * v, "tn")
    if nsub > 1:
        nhead = lax.broadcasted_iota(jnp.int32, (N, W), 0) // (N // nsub)
        hn = jnp.where(nhead == lax.broadcasted_iota(jnp.int32, (N, W), 1) // HEAD_W, hn, 0.0)
    return y, hn


def scan_op(q, k, x, a_tok, dt_tok, *, rev, incl, nsub):
    S = q.shape[0]
    G, _, Hg = a_tok.shape
    N = q.shape[1] // G
    Vw = x.shape[1] // G
    L = CHUNK
    nc = S // L
    use_dt = dt_tok is not None
    chunk = functools.partial(_scan_chunk, rev=rev, incl=incl, nsub=nsub)
    PW = min(Vw, LANES)
    blocks = [(p * PW, (p * PW) // HEAD_W) for p in range(Vw // PW)]

    def order(c, backward):
        return (nc - 1 - c) if (rev != backward) else c

    def specs(backward):
        qs = pl.BlockSpec((L, N), lambda g, c: (order(c, backward), g))
        xs = pl.BlockSpec((L, Vw), lambda g, c: (order(c, backward), g))
        as_ = pl.BlockSpec((1, L, Hg), lambda g, c: (g, order(c, backward), 0))
        hs = pl.BlockSpec((1, 1, N, Vw), lambda g, c: (g, order(c, backward), 0, 0))
        return qs, xs, as_, hs

    def call_fwd(q, k, x, a_tok, dt_tok):
        qs, xs, as_, hs = specs(False)

        def body(*refs):
            if use_dt:
                q_ref, k_ref, x_ref, a_ref, dt_ref, y_ref, hs_ref, h_scr = refs
            else:
                q_ref, k_ref, x_ref, a_ref, y_ref, hs_ref, h_scr = refs

            @pl.when(pl.program_id(1) == 0)
            def _():
                h_scr[...] = jnp.zeros_like(h_scr)

            hs_ref[0, 0] = h_scr[...]
            q, k, a, dt = q_ref[...], k_ref[...], a_ref[0], dt_ref[0] if use_dt else None
            for lane0, head0 in blocks:
                cols = slice(lane0, lane0 + PW)
                y, hn = chunk(q, k, x_ref[:, cols], a, dt, h_scr[:, cols], head0=head0)
                y_ref[:, cols] = y
                h_scr[:, cols] = hn

        ins = [q, k, x, a_tok] + ([dt_tok] if use_dt else [])
        return pl.pallas_call(
            body,
            out_shape=[jax.ShapeDtypeStruct((S, G * Vw), f32), jax.ShapeDtypeStruct((G, nc, N, Vw), f32)],
            grid=(G, nc),
            in_specs=[qs, qs, xs, as_] + ([as_] if use_dt else []),
            out_specs=[xs, hs],
            scratch_shapes=[pltpu.VMEM((N, Vw), f32)],
            compiler_params=_cparams(("parallel", "arbitrary")),
            name="scan_fwd",
        )(*ins)

    def call_bwd(q, k, x, a_tok, dt_tok, hsave, dy):
        qs, xs, as_, hs = specs(True)

        def body(*refs):
            if use_dt:
                q_ref, k_ref, x_ref, a_ref, dt_ref, hs_ref, dy_ref, dq_ref, dk_ref, dx_ref, da_ref, ddt_ref, dh_scr = refs
            else:
                q_ref, k_ref, x_ref, a_ref, hs_ref, dy_ref, dq_ref, dk_ref, dx_ref, da_ref, dh_scr = refs

            @pl.when(pl.program_id(1) == 0)
            def _():
                dh_scr[...] = jnp.zeros_like(dh_scr)

            q, k, a = q_ref[...].astype(f32), k_ref[...].astype(f32), a_ref[0]
            dq, dk, da, ddt = 0.0, 0.0, 0.0, 0.0
            for lane0, head0 in blocks:
                cols = slice(lane0, lane0 + PW)
                if use_dt:
                    f = lambda q, k, x, a, dt, h: chunk(q, k, x, a, dt, h, head0=head0)
                    prim = [q, k, x_ref[:, cols], a, dt_ref[0], hs_ref[0, 0, :, cols]]
                else:
                    f = lambda q, k, x, a, h: chunk(q, k, x, a, None, h, head0=head0)
                    prim = [q, k, x_ref[:, cols], a, hs_ref[0, 0, :, cols]]
                _, vjp = jax.vjp(f, *prim)
                cts = vjp((dy_ref[:, cols], dh_scr[:, cols]))
                dq, dk, da = dq + cts[0], dk + cts[1], da + cts[3]
                if use_dt:
                    ddt = ddt + cts[4]
                dx_ref[:, cols] = cts[2]
                dh_scr[:, cols] = cts[-1]
            dq_ref[...] = dq.astype(dq_ref.dtype)
            dk_ref[...] = dk.astype(dk_ref.dtype)
            da_ref[0] = da
            if use_dt:
                ddt_ref[0] = ddt

        ins = [q, k, x, a_tok] + ([dt_tok] if use_dt else []) + [hsave, dy]
        a_shape = jax.ShapeDtypeStruct(a_tok.shape, f32)
        return pl.pallas_call(
            body,
            out_shape=[jax.ShapeDtypeStruct(q.shape, q.dtype), jax.ShapeDtypeStruct(k.shape, k.dtype),
                       jax.ShapeDtypeStruct(x.shape, f32), a_shape] + ([a_shape] if use_dt else []),
            grid=(G, nc),
            in_specs=[qs, qs, xs, as_] + ([as_] if use_dt else []) + [hs, xs],
            out_specs=[qs, qs, xs, as_] + ([as_] if use_dt else []),
            scratch_shapes=[pltpu.VMEM((N, Vw), f32)],
            compiler_params=_cparams(("parallel", "arbitrary")),
            name="scan_bwd",
        )(*ins)

    if use_dt:
        @jax.custom_vjp
        def op(q, k, x, a_tok, dt_tok):
            return call_fwd(q, k, x, a_tok, dt_tok)[0]

        def fwd(q, k, x, a_tok, dt_tok):
            y, hsave = call_fwd(q, k, x, a_tok, dt_tok)
            return y, (q, k, x, a_tok, dt_tok, hsave)

        def bwd(res, dy):
            q, k, x, a_tok, dt_tok, hsave = res
            return tuple(call_bwd(q, k, x, a_tok, dt_tok, hsave, dy))

        op.defvjp(fwd, bwd)
        return op(q, k, x, a_tok, dt_tok)

    @jax.custom_vjp
    def op(q, k, x, a_tok):
        return call_fwd(q, k, x, a_tok, None)[0]

    def fwd(q, k, x, a_tok):
        y, hsave = call_fwd(q, k, x, a_tok, None)
        return y, (q, k, x, a_tok, hsave)

    def bwd(res, dy):
        q, k, x, a_tok, hsave = res
        return tuple(call_bwd(q, k, x, a_tok, None, hsave, dy))

    op.defvjp(fwd, bwd)
    return op(q, k, x, a_tok)


def _swap_halves(x, dh):
    W = x.shape[1]
    lane = lax.broadcasted_iota(jnp.int32, (1, W), 1) % dh
    return jnp.where(lane < dh // 2, pltpu.roll(x, W - dh // 2, 1), pltpu.roll(x, dh // 2, 1))


def rotary(rq, rk, cos_t, sin_t):
    scale = RET_DH ** -0.5

    def fn(rq, rk, c, s):
        return rq * c + _swap_halves(rq, RET_DH) * s, (rk * c + _swap_halves(rk, RET_DH) * s) * scale

    def bwd_fn(rv, pv, dos):
        _, _, c, s = rv
        dq, dk = dos
        dk = dk * scale
        return (dq * c + _swap_halves(dq * s, RET_DH), dk * c + _swap_halves(dk * s, RET_DH)), ()

    return rowwise(fn, "rotary", [rq, rk, cos_t, sin_t], [], [MXU_DTYPE, MXU_DTYPE], n_diff_rows=2, bwd_fn=bwd_fn)


def _rope_tables(S, width):
    half = RET_DH // 2
    inv = 1.0 / (ROPE_BASE ** (jnp.arange(half, dtype=f32) / half))
    ang = jnp.arange(S, dtype=f32)[:, None] * inv[None, :]
    cos, sin = jnp.cos(ang), jnp.sin(ang)
    reps = width // RET_DH
    return jnp.tile(jnp.concatenate([cos, cos], axis=1), (1, reps)), jnp.tile(jnp.concatenate([-sin, sin], axis=1), (1, reps))


def _exact_dot(x, m):
    return jnp.dot(x, m, precision=lax.Precision.HIGHEST, preferred_element_type=f32)


def ret_post(y_f, y_b, rg, gn_g):
    W = y_f.shape[1]
    idx = np.arange(W) // RET_DH
    avg = jnp.asarray((idx[:, None] == idx[None, :]).astype(np.float32) / RET_DH)

    def fn(yf, yb, rg, g, avg):
        y = yf + yb
        mu = _exact_dot(y, avg)
        d = y - mu
        var = _exact_dot(d * d, avg)
        return (_silu(rg) * (d * lax.rsqrt(var + EPS) * g),)

    return rowwise(fn, "ret_post", [y_f, y_b, rg], [gn_g.reshape(1, -1), avg], [MXU_DTYPE], n_diff_params=1)[0]


def _na_bias(rpb, win_r):
    H = rpb.shape[0]
    qc = np.arange(GRID_W)[:, None]
    kc = np.arange(GRID_W)[None, :]
    cstart = np.clip(qc - NA_WIN_C // 2, 0, GRID_W - NA_WIN_C)
    valid = (kc >= cstart) & (kc < cstart + NA_WIN_C)
    dc = np.clip(kc - qc, -(NA_WIN_C - 1), NA_WIN_C - 1) + (NA_WIN_C - 1)
    onehot = (dc[None] == np.arange(2 * NA_WIN_C - 1)[:, None, None]).astype(np.float32)
    t1 = jnp.einsum("hrd,dqk->hrqk", rpb.astype(f32), jnp.asarray(onehot), precision=lax.Precision.HIGHEST)
    per_delta = [t1[:, NA_WIN_R - 1 - d:NA_WIN_R - 1 - d + win_r] for d in range(win_r)]
    b = jnp.stack(per_delta, axis=1)
    b = jnp.where(jnp.asarray(valid)[None, None, None], b, NEG_INF)
    return jnp.transpose(b, (0, 1, 3, 2, 4)).reshape(H, win_r, GRID_W, win_r * GRID_W)


def _na_row(q, kw, vw, biases):
    lane = lax.broadcasted_iota(jnp.int32, (1, q.shape[1]), 1) // NA_DH
    o = 0.0
    for i, b in enumerate(biases):
        qi = jnp.where(lane == i, q, 0.0) * (NA_DH ** -0.5)
        s = bdot(qi, kw, "nt") + b
        e = jnp.exp(s - jnp.max(s, axis=1, keepdims=True))
        p = e / jnp.sum(e, axis=1, keepdims=True)
        o = o + jnp.where(lane == i, bdot(p, vw, "nn"), 0.0)
    return o


def na_op(nq, nk, nv, bias):
    S, W = nq.shape
    rows = S // GRID_W
    win_r = bias.shape[1]
    nkeys = win_r * GRID_W
    hp = LANES // NA_DH
    npair = W // LANES
    RB = min(16, rows)
    nrb = rows // RB
    qspec = pl.BlockSpec((RB * GRID_W, LANES), lambda p, r: (r, p))
    kspec = pl.BlockSpec((S, LANES), lambda p, r: (0, p))
    bspec = pl.BlockSpec((hp, win_r, GRID_W, nkeys), lambda p, r: (p, 0, 0, 0))

    def window(r):
        r0 = jnp.clip(r - win_r // 2, 0, rows - win_r)
        return pl.multiple_of(r0 * GRID_W, GRID_W), r - r0

    def call_fwd(nq, nk, nv, bias):
        def body(q_ref, k_ref, v_ref, b_ref, o_ref):
            rb = pl.program_id(1)

            def row(i, c):
                k0, d = window(rb * RB + i)
                q0 = pl.multiple_of(i * GRID_W, GRID_W)
                o = _na_row(q_ref[pl.ds(q0, GRID_W), :].astype(f32), k_ref[pl.ds(k0, nkeys), :], v_ref[pl.ds(k0, nkeys), :],
                            [b_ref[h, pl.ds(d, 1)][0] for h in range(hp)])
                o_ref[pl.ds(q0, GRID_W), :] = o.astype(o_ref.dtype)
                return c

            lax.fori_loop(0, RB, row, 0)

        return pl.pallas_call(
            body,
            out_shape=jax.ShapeDtypeStruct((S, W), nq.dtype),
            grid=(npair, nrb),
            in_specs=[qspec, kspec, kspec, bspec],
            out_specs=qspec,
            compiler_params=_cparams(("parallel", "arbitrary")),
            name="na_fwd",
        )(nq, nk, nv, bias)

    def call_bwd(nq, nk, nv, bias, do):
        def body(q_ref, k_ref, v_ref, b_ref, do_ref, dq_ref, dk_ref, dv_ref, db_ref, dk_acc, dv_acc):
            rb = pl.program_id(1)

            @pl.when(rb == 0)
            def _():
                dk_acc[...] = jnp.zeros_like(dk_acc)
                dv_acc[...] = jnp.zeros_like(dv_acc)
                db_ref[...] = jnp.zeros_like(db_ref)

            def row(i, c):
                k0, d = window(rb * RB + i)
                q0 = pl.multiple_of(i * GRID_W, GRID_W)
                bs = [b_ref[h, pl.ds(d, 1)][0] for h in range(hp)]
                _, vjp = jax.vjp(lambda q, kw, vw, *b: _na_row(q, kw, vw, b), q_ref[pl.ds(q0, GRID_W), :].astype(f32),
                                 k_ref[pl.ds(k0, nkeys), :].astype(f32), v_ref[pl.ds(k0, nkeys), :].astype(f32), *bs)
                cts = vjp(do_ref[pl.ds(q0, GRID_W), :].astype(f32))
                dq_ref[pl.ds(q0, GRID_W), :] = cts[0].astype(dq_ref.dtype)
                dk_acc[pl.ds(k0, nkeys), :] += cts[1]
                dv_acc[pl.ds(k0, nkeys), :] += cts[2]
                for h in range(hp):
                    db_ref[h, pl.ds(d, 1)] += cts[3 + h][None]
                return c

            lax.fori_loop(0, RB, row, 0)

            @pl.when(rb == nrb - 1)
            def _():
                dk_ref[...] = dk_acc[...].astype(dk_ref.dtype)
                dv_ref[...] = dv_acc[...].astype(dv_ref.dtype)

        return pl.pallas_call(
            body,
            out_shape=[jax.ShapeDtypeStruct((S, W), nq.dtype), jax.ShapeDtypeStruct((S, W), nk.dtype),
                       jax.ShapeDtypeStruct((S, W), nv.dtype), jax.ShapeDtypeStruct(bias.shape, f32)],
            grid=(npair, nrb),
            in_specs=[qspec, kspec, kspec, bspec, qspec],
            out_specs=[qspec, kspec, kspec, bspec],
            scratch_shapes=[pltpu.VMEM((S, LANES), f32), pltpu.VMEM((S, LANES), f32)],
            compiler_params=_cparams(("parallel", "arbitrary")),
            name="na_bwd",
        )(nq, nk, nv, bias, do)

    @jax.custom_vjp
    def op(nq, nk, nv, bias):
        return call_fwd(nq, nk, nv, bias)

    def fwd(nq, nk, nv, bias):
        return call_fwd(nq, nk, nv, bias), (nq, nk, nv, bias)

    def bwd(res, do):
        return tuple(call_bwd(*res, do))

    op.defvjp(fwd, bwd)
    return op(nq, nk, nv, bias)


def ssd_dt(dt_raw, dt_bias, a_neg):
    def fn(r, b, a):
        dt = _softplus(r + b)
        return dt, dt * a

    return rowwise(fn, "ssd_dt", [dt_raw], [dt_bias, a_neg], [f32, f32])


def ssd_post(y_f, y_b, xs, z, d_skip_lanes, norm_g, groups):
    def fn(yf, yb, xs, z, dsk, g):
        y = (yf + yb + xs * dsk) * _silu(z)
        return (y * lax.rsqrt(jnp.mean(y * y, axis=-1, keepdims=True) + EPS) * g,)

    return rowwise(fn, "ssd_post", [y_f, y_b, xs, z], [d_skip_lanes.reshape(1, -1), norm_g.reshape(1, -1)], [MXU_DTYPE],
                   ncol=groups)[0]


def _heads_major(t, groups):
    S = t.shape[0]
    return jnp.transpose(t.reshape(S, groups, -1), (1, 0, 2))


def retention_na_mixer(hn, w_in, decay_logit, gn_g, rpb, w_out, tables):
    S = hn.shape[0]
    R = RET_HEADS * RET_DH
    NW = NA_HEADS * NA_DH
    cols = lambda a, b: w_in[:, a:b]
    rq, rk, rv, rg = (mm(hn, cols(j * R, (j + 1) * R)) for j in range(4))
    nq, nk, nv = (mm(hn, cols(4 * R + j * NW, 4 * R + (j + 1) * NW), out_dtype=MXU_DTYPE) for j in range(3))
    qr, kr = rotary(rq, rk, *tables)
    log_gamma = -_softplus(-decay_logit.astype(f32))
    pairs = R // LANES
    hp = LANES // RET_DH
    a_f = jnp.broadcast_to(log_gamma[0].reshape(pairs, 1, hp), (pairs, S, hp))
    a_b = jnp.broadcast_to(log_gamma[1].reshape(pairs, 1, hp), (pairs, S, hp))
    y_f = scan_op(qr, kr, rv, a_f, None, rev=False, incl=True, nsub=hp)
    y_b = scan_op(qr, kr, rv, a_b, None, rev=True, incl=False, nsub=hp)
    ret = ret_post(y_f, y_b, rg, gn_g)
    rows = S // GRID_W
    nao = na_op(nq, nk, nv, _na_bias(rpb, min(NA_WIN_R, rows)))
    return mm(ret, w_out[:R]) + mm(nao, w_out[R:])


def ssd_mixer(hn, w_in, conv_w, conv_b, dt_bias, a_log, d_skip, norm_g, w_out):
    heads = d_skip.shape[0]
    inner = heads * SSD_HEADDIM
    gs = SSD_GROUPS * SSD_STATE
    o_x, o_b, o_c, o_dt = inner, 2 * inner, 2 * inner + gs, 2 * inner + 2 * gs
    z = mm(hn, w_in[:, :inner])
    pre = [mm(hn, w_in[:, a:b]) for a, b in ((o_x, o_b), (o_b, o_c), (o_c, o_dt))]
    dt_raw = mm(hn, w_in[:, o_dt:])
    cw = [conv_w[:, a - inner:b - inner] for a, b in ((o_x, o_b), (o_b, o_c), (o_c, o_dt))]
    cb = [conv_b[a - inner:b - inner] for a, b in ((o_x, o_b), (o_b, o_c), (o_c, o_dt))]
    xs, bm, cm = (silu_op(dwconv(p, w, b)) for p, w, b in zip(pre, cw, cb))
    a_neg = -jnp.exp(a_log.astype(f32)).reshape(1, -1)
    dt, la = ssd_dt(dt_raw, dt_bias.astype(f32).reshape(1, -1), a_neg)
    dt_f, dt_b = _heads_major(dt[:, :heads], SSD_GROUPS), _heads_major(dt[:, heads:], SSD_GROUPS)
    la_f, la_b = _heads_major(la[:, :heads], SSD_GROUPS), _heads_major(la[:, heads:], SSD_GROUPS)
    y_f = scan_op(cm, bm, xs, la_f, dt_f, rev=False, incl=True, nsub=1)
    y_b = scan_op(cm, bm, xs, la_b, dt_b, rev=True, incl=False, nsub=1)
    y = ssd_post(y_f, y_b, xs, z, jnp.repeat(d_skip.astype(f32), SSD_HEADDIM), norm_g, SSD_GROUPS)
    return mm(y, w_out)


def conv_geglu_ffn(hf, w_up, conv_w, conv_b, w_down):
    F = w_down.shape[0]
    gate = dwconv(mm(hf, w_up[:, :F]), conv_w[:, :F], conv_b[:F])
    val = dwconv(mm(hf, w_up[:, F:]), conv_w[:, F:], conv_b[F:])
    return mm(geglu(gate, val), w_down)


def model_loss(x, tgt, big, small, rep):
    S = x.shape[0]
    depth = rep["norm_mix_pre"].shape[0]
    tables = _rope_tables(S, RET_HEADS * RET_DH)
    for layer in range(depth):
        i = layer // 2
        hn = rms(x, rep["norm_mix_pre"][layer], MXU_DTYPE)
        if layer % 2 == 0:
            m = retention_na_mixer(hn, big["ab_w_in"][i], rep["ab_ret_decay_logit"][i], rep["ab_ret_gn_g"][i],
                                   rep["ab_na_rpb"][i], big["ab_w_out"][i], tables)
        else:
            m = ssd_mixer(hn, big["c_w_in"][i], small["c_conv_w"][i], small["c_conv_b"][i], rep["c_dt_bias"][i],
                          rep["c_a_log"][i], rep["c_d_skip"][i], small["c_norm_g"][i], big["c_w_out"][i])
        x = rms_residual(m, rep["norm_mix_post"][layer], x)
        hf = rms(x, rep["norm_ffn_pre"][layer], MXU_DTYPE)
        f = conv_geglu_ffn(hf, big["ffn_w_up"][layer], small["ffn_conv_w"][layer], rep["ffn_conv_b"][layer],
                           big["ffn_w_down"][layer])
        x = rms_residual(f, rep["norm_ffn_post"][layer], x)
    return loss_op(x, tgt)


def _mesh_pos():
    return lax.axis_index("x"), lax.axis_index("y"), lax.axis_index("c")


def gather_chips(local):
    R, Wd = local.shape

    half = R // 2
    CH = COPY_CHUNKS
    q = half // CH

    def body(x_ref, out_ref, send_sems, recv_sems):
        x, y, c = _mesh_pos()
        my = 2 * x + y
        chips = [(1 - x, y), (x, 1 - y), (1 - x, 1 - y)]

        def piece(ref, h, j):
            return ref.at[pl.ds(pl.multiple_of(h * half + j * q, PACK_ALIGN), q), :]

        def copy(k, src, chip, h, j, to):
            return pltpu.make_async_remote_copy(src_ref=src, dst_ref=piece(out_ref.at[chip], h, j), send_sem=send_sems.at[k],
                                                recv_sem=recv_sems.at[k], device_id=to, device_id_type=pl.DeviceIdType.MESH)

        first = [[copy(k * CH + j, piece(x_ref, c, j), my, c, j, (cx, cy, c)) for j in range(CH)]
                 for k, (cx, cy) in enumerate(chips)]
        for j in range(CH):
            for k in range(3):
                first[k][j].start()
        passed = [[copy((3 + k) * CH + j, piece(out_ref.at[2 * cx + cy], c, j), 2 * cx + cy, c, j, (x, y, 1 - c))
                   for j in range(CH)] for k, (cx, cy) in enumerate(chips)]
        for j in range(CH):
            for k, (cx, cy) in enumerate(chips):
                copy(k * CH + j, piece(x_ref, c, j), 2 * cx + cy, c, j, (cx, cy, c)).wait_recv()
                passed[k][j].start()
        for j in range(CH):
            for k, (cx, cy) in enumerate(chips):
                copy((3 + k) * CH + j, piece(x_ref, c, j), 2 * cx + cy, 1 - c, j, (x, y, 1 - c)).wait_recv()
        for k in range(3):
            for cp in first[k] + passed[k]:
                cp.wait_send()

    return pl.pallas_call(
        body,
        out_shape=jax.ShapeDtypeStruct((N_CHIPS, R, Wd), local.dtype),
        in_specs=[pl.BlockSpec(memory_space=pl.ANY)],
        out_specs=pl.BlockSpec(memory_space=pl.ANY),
        scratch_shapes=[pltpu.SemaphoreType.DMA((6 * CH,)), pltpu.SemaphoreType.DMA((6 * CH,))],
        name="gather_chips",
    )(local)


def pair_swap(parts):
    n, R, Wd = parts.shape
    half = R // 2

    CH = COPY_CHUNKS
    q = half // CH

    def body(p_ref, got_ref, send_sems, recv_sems):
        x, y, c = _mesh_pos()

        def src(s, j):
            return p_ref.at[s, pl.ds(pl.multiple_of((1 - c) * half + j * q, PACK_ALIGN), q), :]

        swap = [pltpu.make_async_remote_copy(src_ref=src(s, j), dst_ref=got_ref.at[s, pl.ds(j * q, q), :],
                                             send_sem=send_sems.at[s * CH + j], recv_sem=recv_sems.at[s * CH + j],
                                             device_id=(x, y, 1 - c), device_id_type=pl.DeviceIdType.MESH)
                for s in range(n) for j in range(CH)]
        for cp in swap:
            cp.start()
        for cp in swap:
            cp.wait()

    return pl.pallas_call(
        body,
        out_shape=jax.ShapeDtypeStruct((n, half, Wd), parts.dtype),
        in_specs=[pl.BlockSpec(memory_space=pl.ANY)],
        out_specs=pl.BlockSpec(memory_space=pl.ANY),
        scratch_shapes=[pltpu.SemaphoreType.DMA((n * CH,)), pltpu.SemaphoreType.DMA((n * CH,))],
        name="pair_swap",
    )(parts)


def chip_exchange(parts):
    n, R, Wd = parts.shape

    def body(p_ref, out_ref, send_sems, recv_sems):
        x, y, c = _mesh_pos()
        my = 2 * x + y
        chips = [(1 - x, y), (x, 1 - y), (1 - x, 1 - y)]

        def copy(k, src_slot, dst_slot, to):
            return pltpu.make_async_remote_copy(src_ref=p_ref.at[src_slot], dst_ref=out_ref.at[dst_slot], send_sem=send_sems.at[k],
                                                recv_sem=recv_sems.at[k], device_id=to, device_id_type=pl.DeviceIdType.MESH)

        sends = [copy(k, 2 * cx + cy, my, (cx, cy, c)) for k, (cx, cy) in enumerate(chips)]
        for cp in sends:
            cp.start()
        for k, (cx, cy) in enumerate(chips):
            copy(k, my, 2 * cx + cy, (cx, cy, c)).wait_recv()
        for cp in sends:
            cp.wait_send()

    return pl.pallas_call(
        body,
        out_shape=jax.ShapeDtypeStruct((n, R, Wd), parts.dtype),
        in_specs=[pl.BlockSpec(memory_space=pl.ANY)],
        out_specs=pl.BlockSpec(memory_space=pl.ANY),
        scratch_shapes=[pltpu.SemaphoreType.DMA((3,)), pltpu.SemaphoreType.DMA((3,))],
        name="chip_exchange",
    )(parts)


def pair_share(mine):
    R, Wd = mine.shape

    CH = 2 * COPY_CHUNKS
    q = R // CH

    def body(m_ref, out_ref, send_sems, recv_sems):
        x, y, c = _mesh_pos()
        swap = [pltpu.make_async_remote_copy(src_ref=m_ref.at[pl.ds(j * q, q), :], dst_ref=out_ref.at[pl.ds(j * q, q), :],
                                             send_sem=send_sems.at[j], recv_sem=recv_sems.at[j], device_id=(x, y, 1 - c),
                                             device_id_type=pl.DeviceIdType.MESH) for j in range(CH)]
        for cp in swap:
            cp.start()
        for cp in swap:
            cp.wait()

    return pl.pallas_call(
        body,
        out_shape=jax.ShapeDtypeStruct((R, Wd), mine.dtype),
        in_specs=[pl.BlockSpec(memory_space=pl.ANY)],
        out_specs=pl.BlockSpec(memory_space=pl.ANY),
        scratch_shapes=[pltpu.SemaphoreType.DMA((CH,)), pltpu.SemaphoreType.DMA((CH,))],
        name="pair_share",
    )(mine)


def sum_chips(recv, own):
    n, R, Wd = recv.shape
    tr = _pick(R, (512, 256, 128, 64, 32, 16, 8))

    def body(r_ref, p_ref, o_ref):
        my = 2 * lax.axis_index("x") + lax.axis_index("y")
        acc = jnp.zeros((tr, Wd), f32)
        for s in range(n):
            acc = acc + jnp.where(my == s, p_ref[s], r_ref[s]).astype(f32)
        o_ref[...] = acc

    spec = pl.BlockSpec((n, tr, Wd), lambda i: (0, i, 0))
    return pl.pallas_call(
        body,
        out_shape=jax.ShapeDtypeStruct((R, Wd), f32),
        grid=(R // tr,),
        in_specs=[spec, spec],
        out_specs=pl.BlockSpec((tr, Wd), lambda i: (i, 0)),
        compiler_params=_cparams(("parallel",)),
        name="sum_chips",
    )(recv, own)


def add_pair(parts, got):
    n, R, Wd = parts.shape
    half = R // 2
    tr = _pick(half, (512, 256, 128, 64, 32, 16, 8))
    nb = half // tr

    def body(lo_ref, hi_ref, g_ref, o_ref):
        mine = jnp.where(lax.axis_index("c") == 0, lo_ref[...], hi_ref[...])
        o_ref[...] = (mine.astype(f32) + g_ref[...].astype(f32)).astype(o_ref.dtype)

    spec = pl.BlockSpec((1, tr, Wd), lambda s, i: (s, i, 0))
    return pl.pallas_call(
        body,
        out_shape=jax.ShapeDtypeStruct(got.shape, parts.dtype),
        grid=(n, nb),
        in_specs=[spec, pl.BlockSpec((1, tr, Wd), lambda s, i: (s, nb + i, 0)), spec],
        out_specs=spec,
        compiler_params=_cparams(("parallel", "parallel")),
        name="add_pair",
    )(parts, parts, got)


def reduce_scatter(parts):
    chip_sum = add_pair(parts, pair_swap(parts))
    mine = sum_chips(chip_exchange(chip_sum), chip_sum)
    theirs = pair_share(mine)
    first = lax.axis_index("c") == 0
    return jnp.concatenate([jnp.where(first, mine, theirs), jnp.where(first, theirs, mine)], axis=0)


def adamw(w, g, m, v):
    R, C = w.shape
    tr = R
    for cand in (512, 256, 128, 64, 32, 16, 8):
        if R % cand == 0 and cand * C * 4 <= (1 << 20):
            tr = cand
            break

    def body(w_ref, g_ref, m_ref, v_ref, d_ref, mo_ref, vo_ref):
        g = g_ref[...]
        m = ADAM_B1 * m_ref[...] + (1.0 - ADAM_B1) * g
        v = ADAM_B2 * v_ref[...] + (1.0 - ADAM_B2) * (g * g)
        m_hat = m / (1.0 - ADAM_B1 ** ADAM_STEP)
        v_hat = v / (1.0 - ADAM_B2 ** ADAM_STEP)
        d_ref[...] = -ADAM_LR * (m_hat / (jnp.sqrt(v_hat) + ADAM_EPS) + ADAM_WD * w_ref[...])
        mo_ref[...] = m
        vo_ref[...] = v

    spec = pl.BlockSpec((tr, C), lambda i: (i, 0))
    return pl.pallas_call(
        body,
        out_shape=[jax.ShapeDtypeStruct((R, C), f32)] * 3,
        grid=(R // tr,),
        in_specs=[spec] * 4,
        out_specs=[spec] * 3,
        compiler_params=_cparams(("parallel",)),
        name="adamw",
    )(w, g, m, v)


def _pack(arrs, dtype):
    flat = jnp.concatenate([a.astype(dtype).reshape(-1) for a in arrs])
    n = flat.shape[0]
    unit = PACK_W * PACK_ROWS
    padded = -(-n // unit) * unit
    return jnp.pad(flat, (0, padded - n)).reshape(-1, PACK_W)


def _unpack(buf, shapes):
    flat = buf.reshape(-1)
    out, off = [], 0
    for s in shapes:
        n = int(np.prod(s))
        out.append(flat[off:off + n].reshape(s))
        off += n
    return out


BIG = (("ab_w_in", 2), ("ab_w_out", 1), ("c_w_in", 2), ("c_w_out", 1), ("ffn_w_up", 2), ("ffn_w_down", 1))
SMALL = (("c_conv_w", 2), ("c_conv_b", 1), ("c_norm_g", 1), ("ffn_conv_w", 2))
REP = ("norm_mix_pre", "norm_mix_post", "norm_ffn_pre", "norm_ffn_post", "ab_ret_decay_logit", "ab_ret_gn_g", "ab_na_rpb",
       "c_dt_bias", "c_a_log", "c_d_skip", "ffn_conv_b")
WEIGHTS = ("norm_mix_pre", "norm_mix_post", "norm_ffn_pre", "norm_ffn_post", "ab_w_in", "ab_ret_decay_logit", "ab_ret_gn_g",
           "ab_na_rpb", "ab_w_out", "c_w_in", "c_conv_w", "c_conv_b", "c_dt_bias", "c_a_log", "c_d_skip", "c_norm_g", "c_w_out",
           "ffn_w_up", "ffn_conv_w", "ffn_conv_b", "ffn_w_down")


def _gather_set(local, spec, dtype):
    shapes = [local[n].shape for n, _ in spec]
    packed = _pack([local[n] for n, _ in spec], dtype)
    got = gather_chips(packed)
    my = 2 * lax.axis_index("x") + lax.axis_index("y")
    per_chip = [_unpack(jnp.where(my == s, packed, got[s]), shapes) for s in range(N_CHIPS)]
    return {n: jnp.concatenate([per_chip[s][j] for s in range(N_CHIPS)], axis=ax) for j, (n, ax) in enumerate(spec)}


def _scatter_parts(full, spec, extra, dtype):
    split = {n: jnp.split(full[n], N_CHIPS, axis=ax) for n, ax in spec}
    return jnp.stack([_pack([split[n][s] for n, _ in spec] + list(extra), dtype) for s in range(N_CHIPS)])


def kernel(x, norm_mix_pre, norm_mix_post, norm_ffn_pre, norm_ffn_post, ab_w_in, ab_ret_decay_logit, ab_ret_gn_g, ab_na_rpb, ab_w_out, c_w_in, c_conv_w, c_conv_b, c_dt_bias, c_a_log, c_d_skip, c_norm_g, c_w_out, ffn_w_up, ffn_conv_w, ffn_conv_b, ffn_w_down, loss_target, m_norm_mix_pre, m_norm_mix_post, m_norm_ffn_pre, m_norm_ffn_post, m_ab_w_in, m_ab_ret_decay_logit, m_ab_ret_gn_g, m_ab_na_rpb, m_ab_w_out, m_c_w_in, m_c_conv_w, m_c_conv_b, m_c_dt_bias, m_c_a_log, m_c_d_skip, m_c_norm_g, m_c_w_out, m_ffn_w_up, m_ffn_conv_w, m_ffn_conv_b, m_ffn_w_down, v_norm_mix_pre, v_norm_mix_post, v_norm_ffn_pre, v_norm_ffn_post, v_ab_w_in, v_ab_ret_decay_logit, v_ab_ret_gn_g, v_ab_na_rpb, v_ab_w_out, v_c_w_in, v_c_conv_w, v_c_conv_b, v_c_dt_bias, v_c_a_log, v_c_d_skip, v_c_norm_g, v_c_w_out, v_ffn_w_up, v_ffn_conv_w, v_ffn_conv_b, v_ffn_w_down):
    args = dict(locals())
    w = {n: args[n] for n in WEIGHTS}
    mom = {n: args["m_" + n] for n in WEIGHTS}
    var = {n: args["v_" + n] for n in WEIGHTS}

    big = _gather_set(w, BIG, MXU_DTYPE)
    small = _gather_set(w, SMALL, f32)
    rep = {n: w[n] for n in REP}

    def loss_fn(xs, big, small, rep):
        return model_loss(xs, loss_target[0], big, small, rep)

    loss, (gx, gbig, gsmall, grep) = jax.value_and_grad(loss_fn, argnums=(0, 1, 2, 3))(x[0], big, small, rep)
    loss = lax.psum(loss, ("x", "y", "c"))

    big_shapes = [w[n].shape for n, _ in BIG]
    small_shapes = [w[n].shape for n, _ in SMALL] + [w[n].shape for n in REP]
    g_big = _unpack(reduce_scatter(_scatter_parts(gbig, BIG, (), MXU_DTYPE)), big_shapes)
    g_small_buf = reduce_scatter(_scatter_parts(gsmall, SMALL, [grep[n] for n in REP], f32))
    grads = dict(zip([n for n, _ in BIG], g_big))
    small_names = [n for n, _ in SMALL] + list(REP)
    grads.update(zip(small_names, _unpack(g_small_buf, small_shapes)))

    delta, new_m, new_v = {}, {}, {}
    for n, _ in BIG:
        shp = w[n].shape
        two_d = lambda a: a.reshape(-1, shp[-1])
        d, m2, v2 = adamw(two_d(w[n]), two_d(grads[n]), two_d(mom[n]), two_d(var[n]))
        delta[n], new_m[n], new_v[n] = d.reshape(shp), m2.reshape(shp), v2.reshape(shp)
    pk = lambda src: _pack([src[n] for n in small_names], f32)
    d, m2, v2 = adamw(pk(w), g_small_buf, pk(mom), pk(var))
    for dst, buf in ((delta, d), (new_m, m2), (new_v, v2)):
        dst.update(zip(small_names, _unpack(buf, small_shapes)))

    return (loss, gx[None], *[grads[n] for n in WEIGHTS], *[delta[n] for n in WEIGHTS],
            *[new_m[n] for n in WEIGHTS], *[new_v[n] for n in WEIGHTS])
```

```python
import functools
import math

import numpy as np
import jax
import jax.numpy as jnp
from jax import lax
from jax.experimental import pallas as pl
from jax.experimental.pallas import tpu as pltpu

f32 = jnp.float32
bf16 = jnp.bfloat16
MXU_DTYPE = bf16

GRID_W = 64
CHUNK = 128
EPS = 1e-6
RET_HEADS = 8
RET_DH = 64
ROPE_BASE = 10000.0
NA_HEADS = 8
NA_DH = 64
NA_WIN_R = 8
NA_WIN_C = 16
SSD_HEADDIM = 64
SSD_GROUPS = 4
SSD_STATE = 128
ADAM_LR = 0.001
ADAM_B1 = 0.9
ADAM_B2 = 0.999
ADAM_EPS = 1e-08
ADAM_WD = 0.01
ADAM_STEP = 10

LANES = 128
HEAD_W = 64
PACK_W = 512
PACK_ROWS = 1024
PACK_ALIGN = 16
COPY_CHUNKS = 4
VMEM_LIMIT = 56 * 1024 * 1024
MM_BLOCK_BYTES = 6 * 1024 * 1024
N_CHIPS = 4
N_DEV = 8
NEG_INF = -1e30

_DIMS = {"nn": (((1,), (0,)), ((), ())), "nt": (((1,), (1,)), ((), ())), "tn": (((0,), (0,)), ((), ()))}


def _cparams(sem=None):
    return pltpu.CompilerParams(dimension_semantics=sem, vmem_limit_bytes=VMEM_LIMIT)


def _pick(dim, cands):
    for c in cands:
        if dim % c == 0:
            return c
    return dim


def _divisor_tile(dim, fits, align):
    for d in range(1, dim + 1):
        t = dim // d
        if dim % d == 0 and t % align == 0 and fits(t):
            return t
    return dim


def _bdot_raw(a, b, mode):
    return lax.dot_general(a.astype(MXU_DTYPE), b.astype(MXU_DTYPE), _DIMS[mode], preferred_element_type=f32)


@functools.partial(jax.custom_vjp, nondiff_argnums=(2,))
def bdot(a, b, mode):
    return _bdot_raw(a, b, mode)


def _bdot_fwd(a, b, mode):
    return _bdot_raw(a, b, mode), (a, b)


def _bdot_bwd(mode, res, g):
    a, b = res
    if mode == "nn":
        da, db = _bdot_raw(g, b, "nt"), _bdot_raw(a, g, "tn")
    elif mode == "nt":
        da, db = _bdot_raw(g, b, "nn"), _bdot_raw(g, a, "tn")
    else:
        da, db = _bdot_raw(b, g, "nt"), _bdot_raw(a, g, "nn")
    return da.astype(a.dtype), db.astype(b.dtype)


bdot.defvjp(_bdot_fwd, _bdot_bwd)


def _mm_call(a, b, mode, out_dtype):
    if mode == "nn":
        (M, K), (K2, N) = a.shape, b.shape
    elif mode == "nt":
        (M, K), (N, K2) = a.shape, b.shape
    else:
        (K, M), (K2, N) = a.shape, b.shape
    assert K == K2, (a.shape, b.shape, mode)
    a_bytes, b_bytes, o_bytes = a.dtype.itemsize, b.dtype.itemsize, jnp.dtype(out_dtype).itemsize
    if mode == "tn":
        tk = _pick(K, (512, 256, 128))
        tn = _divisor_tile(N, lambda t: t <= 1536, LANES)
        tm = _divisor_tile(M, lambda t: t * tn * 4 <= MM_BLOCK_BYTES, 8)
    else:
        tk, tn = K, N
        tm = _divisor_tile(M, lambda t: t * K * a_bytes <= MM_BLOCK_BYTES and t * N * o_bytes <= MM_BLOCK_BYTES, 8)
    nk = K // tk
    if mode == "nn":
        a_spec = pl.BlockSpec((tm, tk), lambda i, j, k: (i, k))
        b_spec = pl.BlockSpec((tk, tn), lambda i, j, k: (k, j))
    elif mode == "nt":
        a_spec = pl.BlockSpec((tm, tk), lambda i, j, k: (i, k))
        b_spec = pl.BlockSpec((tn, tk), lambda i, j, k: (j, k))
    else:
        a_spec = pl.BlockSpec((tk, tm), lambda i, j, k: (k, i))
        b_spec = pl.BlockSpec((tk, tn), lambda i, j, k: (k, j))

    if nk == 1:
        def body(a_ref, b_ref, o_ref):
            o_ref[...] = _bdot_raw(a_ref[...], b_ref[...], mode).astype(o_ref.dtype)
    else:
        def body(a_ref, b_ref, o_ref, acc_ref):
            k = pl.program_id(2)

            @pl.when(k == 0)
            def _():
                acc_ref[...] = jnp.zeros_like(acc_ref)

            acc_ref[...] += _bdot_raw(a_ref[...], b_ref[...], mode)

            @pl.when(k == nk - 1)
            def _():
                o_ref[...] = acc_ref[...].astype(o_ref.dtype)

    return pl.pallas_call(
        body,
        out_shape=jax.ShapeDtypeStruct((M, N), out_dtype),
        grid=(M // tm, N // tn, nk),
        in_specs=[a_spec, b_spec],
        out_specs=pl.BlockSpec((tm, tn), lambda i, j, k: (i, j)),
        scratch_shapes=[pltpu.VMEM((tm, tn), f32)] if nk > 1 else [],
        compiler_params=_cparams(("parallel", "parallel", "arbitrary")),
        name="mm_" + mode,
    )(a, b)


def mm(a, b, mode="nn", out_dtype=f32):
    @jax.custom_vjp
    def op(a, b):
        return _mm_call(a, b, mode, out_dtype)

    def fwd(a, b):
        return _mm_call(a, b, mode, out_dtype), (a, b)

    def bwd(res, g):
        a, b = res
        if mode == "nn":
            return _mm_call(g, b, "nt", a.dtype), _mm_call(a, g, "tn", b.dtype)
        if mode == "nt":
            return _mm_call(g, b, "nn", a.dtype), _mm_call(g, a, "tn", b.dtype)
        return _mm_call(b, g, "nt", a.dtype), _mm_call(a, g, "nn", b.dtype)

    op.defvjp(fwd, bwd)
    return op(a, b)


def _row_tile(S, row_bytes):
    tm = 512
    while tm > 8 and (tm * row_bytes > (6 << 20) or S % tm):
        tm //= 2
    return tm


def rowwise(fn, name, rows, params, out_dtypes, n_diff_rows=None, n_diff_params=None, ncol=1, bwd_fn=None):
    rows, params = list(rows), list(params)
    nr, npar = len(rows), len(params)
    ndr = nr if n_diff_rows is None else n_diff_rows
    ndp = npar if n_diff_params is None else n_diff_params
    S = rows[0].shape[0]
    rw = [r.shape[1] // ncol for r in rows]
    pshape = [(p.shape[0], p.shape[1] // ncol) for p in params]

    def block_structs(tm):
        return ([jax.ShapeDtypeStruct((tm, w), f32) for w in rw] + [jax.ShapeDtypeStruct(s, f32) for s in pshape])

    outs_s = jax.eval_shape(fn, *block_structs(8))
    ow = [o.shape[1] for o in outs_s]
    nout = len(ow)
    row_bytes = 4 * (sum(rw) * 2 + sum(ow) * 2)
    tm = _row_tile(S, row_bytes)
    grid = (ncol, S // tm)

    def rspec(w):
        return pl.BlockSpec((tm, w), lambda g, i: (i, g))

    def pspec(s):
        return pl.BlockSpec(s, lambda g, i: (0, g))

    def call_fwd(*args):
        def body(*refs):
            vals = [r[...].astype(f32) for r in refs[:nr + npar]]
            res = fn(*vals)
            for o, r in zip(refs[nr + npar:], res):
                o[...] = r.astype(o.dtype)

        return pl.pallas_call(
            body,
            out_shape=[jax.ShapeDtypeStruct((S, w * ncol), dt) for w, dt in zip(ow, out_dtypes)],
            grid=grid,
            in_specs=[rspec(w) for w in rw] + [pspec(s) for s in pshape],
            out_specs=[rspec(w) for w in ow],
            compiler_params=_cparams(("parallel", "parallel")),
            name=name + "_fwd",
        )(*args)

    def call_bwd(args, douts):
        def body(*refs):
            in_refs = refs[:nr + npar]
            do_refs = refs[nr + npar:nr + npar + nout]
            dr_refs = refs[nr + npar + nout:nr + npar + nout + ndr]
            dp_refs = refs[nr + npar + nout + ndr:]
            rv = [r[...] for r in in_refs[:nr]]
            pv = [r[...] for r in in_refs[nr:]]
            dos = [d[...].astype(f32) for d in do_refs]
            if bwd_fn is not None:
                drs, dps = bwd_fn(rv, pv, dos)
            else:
                def f(*a):
                    return fn(*a[:ndr], *rv[ndr:], *a[ndr:], *pv[ndp:])

                _, vjp = jax.vjp(f, *[v.astype(f32) for v in rv[:ndr]], *pv[:ndp])
                cts = vjp(tuple(dos))
                drs, dps = cts[:ndr], cts[ndr:]
            for r, ct in zip(dr_refs, drs):
                r[...] = ct.astype(r.dtype)
            if ndp:
                @pl.when(pl.program_id(1) == 0)
                def _():
                    for r in dp_refs:
                        r[...] = jnp.zeros_like(r)

                for r, ct in zip(dp_refs, dps):
                    r[...] += ct

        return pl.pallas_call(
            body,
            out_shape=[jax.ShapeDtypeStruct(r.shape, r.dtype) for r in rows[:ndr]]
            + [jax.ShapeDtypeStruct(p.shape, f32) for p in params[:ndp]],
            grid=grid,
            in_specs=[rspec(w) for w in rw] + [pspec(s) for s in pshape] + [rspec(w) for w in ow],
            out_specs=[rspec(w) for w in rw[:ndr]] + [pspec(s) for s in pshape[:ndp]],
            compiler_params=_cparams(("parallel", "arbitrary")),
            name=name + "_bwd",
        )(*args, *douts)

    @jax.custom_vjp
    def op(*args):
        return tuple(call_fwd(*args))

    def fwd(*args):
        return tuple(call_fwd(*args)), args

    def bwd(args, douts):
        res = call_bwd(args, douts)
        drs, dps = res[:ndr], res[ndr:]
        out = list(drs) + [jnp.zeros_like(a) for a in args[ndr:nr]]
        out += [dp.astype(p.dtype) for dp, p in zip(dps, args[nr:nr + ndp])]
        out += [jnp.zeros_like(a) for a in args[nr + ndp:]]
        return tuple(out)

    op.defvjp(fwd, bwd)
    return op(*rows, *params)


def _silu(x):
    return x * (1.0 / (1.0 + jnp.exp(-x)))


def _softplus(x):
    return jnp.maximum(x, 0.0) + jnp.log(1.0 + jnp.exp(-jnp.abs(x)))


def _gelu_tanh(x):
    return 0.5 * x * (1.0 + jnp.tanh(math.sqrt(2.0 / math.pi) * (x + 0.044715 * (x * x * x))))


def _rms_fn(x, g):
    return x * lax.rsqrt(jnp.mean(x * x, axis=-1, keepdims=True) + EPS) * g


def rms(x, g, out_dtype):
    return rowwise(lambda x, g: (_rms_fn(x, g),), "rms", [x], [g.reshape(1, -1)], [out_dtype])[0]


def rms_residual_norm(m, g, x, g_next):
    def fn(m, x, g, gn):
        xn = x + _rms_fn(m, g)
        return xn, _rms_fn(xn, gn)

    return rowwise(fn, "rms_res_norm", [m, x], [g.reshape(1, -1), g_next.reshape(1, -1)], [f32, MXU_DTYPE])


def rms_residual(m, g, x):
    return rowwise(lambda m, x, g: (x + _rms_fn(m, g),), "rms_res", [m, x], [g.reshape(1, -1)], [f32])[0]


def loss_op(y, tgt):
    S, D = y.shape
    tm = _row_tile(S, 4 * D * 4)

    def call_fwd(y, tgt):
        def body(y_ref, t_ref, o_ref):
            @pl.when(pl.program_id(0) == 0)
            def _():
                o_ref[...] = jnp.zeros_like(o_ref)

            e = y_ref[...] - t_ref[...]
            o_ref[...] += 0.5 * jnp.sum(jnp.mean(e * e, axis=-1, keepdims=True))

        out = pl.pallas_call(
            body,
            out_shape=jax.ShapeDtypeStruct((8, LANES), f32),
            grid=(S // tm,),
            in_specs=[pl.BlockSpec((tm, D), lambda i: (i, 0))] * 2,
            out_specs=pl.BlockSpec((8, LANES), lambda i: (0, 0)),
            compiler_params=_cparams(("arbitrary",)),
            name="loss_fwd",
        )(y, tgt)
        return out[0, 0]

    def call_bwd(y, tgt, g):
        def body(y_ref, t_ref, g_ref, o_ref):
            o_ref[...] = (y_ref[...] - t_ref[...]) * (g_ref[...] * (1.0 / D))

        return pl.pallas_call(
            body,
            out_shape=jax.ShapeDtypeStruct((S, D), f32),
            grid=(S // tm,),
            in_specs=[pl.BlockSpec((tm, D), lambda i: (i, 0))] * 2 + [pl.BlockSpec((1, 1), lambda i: (0, 0))],
            out_specs=pl.BlockSpec((tm, D), lambda i: (i, 0)),
            compiler_params=_cparams(("parallel",)),
            name="loss_bwd",
        )(y, tgt, g.reshape(1, 1).astype(f32))

    @jax.custom_vjp
    def op(y, tgt):
        return call_fwd(y, tgt)

    def fwd(y, tgt):
        return call_fwd(y, tgt), (y, tgt)

    def bwd(res, g):
        y, tgt = res
        return call_bwd(y, tgt, g), jnp.zeros_like(tgt)

    op.defvjp(fwd, bwd)
    return op(y, tgt)


HALO = 8


def _conv_tile(S, R):
    def ext(ref, r0):
        cur = ref[pl.ds(r0, R), :]
        prev = ref[pl.ds(pl.multiple_of(jnp.maximum(r0 - HALO, 0), HALO), HALO), :]
        nxt = ref[pl.ds(pl.multiple_of(jnp.minimum(r0 + R, S - HALO), HALO), HALO), :]
        prev = jnp.where(r0 > 0, prev, 0.0)
        nxt = jnp.where(r0 + R < S, nxt, 0.0)
        return jnp.concatenate([prev, cur, nxt], axis=0)

    return ext


def _shift_rows(e, k, R):
    n = e.shape[0]
    if k == 0:
        return e[HALO:HALO + R]
    return pltpu.roll(e, (-k) % n, 0)[HALO:HALO + R]


def mm_conv_act(h, ws, cws, cbs, act, out_dtype, name):
    n = len(ws)
    S = h.shape[0]
    C = ws[0].shape[1]
    W = cws[0].shape[0]
    pad = W // 2
    bw = _pick(C, (LANES,))
    R = _pick(S, (256, 128, 64, 32, 16, 8))
    nt = S // R
    ext = _conv_tile(S, R)
    col = lambda rows: pl.BlockSpec((rows, bw), lambda j: (0, j))

    def conv(e, wv, bv):
        acc = bv + wv[pad] * e[HALO:HALO + R]
        for j in range(W):
            if j != pad:
                acc = acc + wv[j] * _shift_rows(e, j - pad, R)
        return acc

    def call_fwd(xs, cws, cbs):
        def body(*refs):
            x_refs, w_refs, b_refs, y_ref = refs[:n], refs[n:2 * n], refs[2 * n:3 * n], refs[3 * n]
            wvs = [[w[j:j + 1, :] for j in range(W)] for w in w_refs]
            bvs = [b[...] for b in b_refs]

            def tile(i, c):
                r0 = pl.multiple_of(i * R, R)
                us = [conv(ext(x, r0), wv, bv) for x, wv, bv in zip(x_refs, wvs, bvs)]
                y_ref[pl.ds(r0, R), :] = act(*us).astype(y_ref.dtype)
                return c

            lax.fori_loop(0, nt, tile, 0)

        return pl.pallas_call(
            body,
            out_shape=jax.ShapeDtypeStruct((S, C), out_dtype),
            grid=(C // bw,),
            in_specs=[col(S)] * n + [col(W)] * n + [col(1)] * n,
            out_specs=col(S),
            compiler_params=_cparams(("parallel",)),
            name=name + "_fwd",
        )(*xs, *cws, *cbs)

    def call_bwd(xs, cws, cbs, dy):
        def body(*refs):
            x_refs, w_refs, b_refs, dy_ref = refs[:n], refs[n:2 * n], refs[2 * n:3 * n], refs[3 * n]
            dx_refs, dw_refs, db_refs = refs[3 * n + 1:4 * n + 1], refs[4 * n + 1:5 * n + 1], refs[5 * n + 1:6 * n + 1]
            du_scr = refs[6 * n + 1:]
            wvs = [[w[j:j + 1, :] for j in range(W)] for w in w_refs]
            bvs = [b[...] for b in b_refs]
            zero = jnp.zeros((1, bw), f32)

            def first(i, dbs):
                r0 = pl.multiple_of(i * R, R)
                us = [conv(ext(x, r0), wv, bv) for x, wv, bv in zip(x_refs, wvs, bvs)]
                _, vjp = jax.vjp(act, *us)
                dus = vjp(dy_ref[pl.ds(r0, R), :].astype(f32))
                for scr, du in zip(du_scr, dus):
                    scr[pl.ds(r0, R), :] = du
                return tuple(db + jnp.sum(du, axis=0, keepdims=True) for db, du in zip(dbs, dus))

            dbs = lax.fori_loop(0, nt, first, tuple(zero for _ in range(n)))

            def second(i, dws):
                r0 = pl.multiple_of(i * R, R)
                new = []
                for x, scr, dx, wv, dw in zip(x_refs, du_scr, dx_refs, wvs, dws):
                    ex, ed = ext(x, r0), ext(scr, r0)
                    d0 = ed[HALO:HALO + R]
                    acc = jnp.zeros((R, bw), f32)
                    row = []
                    for j in range(W):
                        acc = acc + wv[j] * _shift_rows(ed, pad - j, R)
                        row.append(dw[j] + jnp.sum(d0 * _shift_rows(ex, j - pad, R), axis=0, keepdims=True))
                    dx[pl.ds(r0, R), :] = acc.astype(dx.dtype)
                    new.append(tuple(row))
                return tuple(new)

            dws = lax.fori_loop(0, nt, second, tuple(tuple(zero for _ in range(W)) for _ in range(n)))
            for dw_ref, db_ref, dw, db in zip(dw_refs, db_refs, dws, dbs):
                dw_ref[...] = jnp.zeros_like(dw_ref)
                for j in range(W):
                    dw_ref[j:j + 1, :] = dw[j]
                db_ref[...] = db

        return pl.pallas_call(
            body,
            out_shape=[jax.ShapeDtypeStruct((S, C), MXU_DTYPE)] * n + [jax.ShapeDtypeStruct((8, C), f32)] * n
            + [jax.ShapeDtypeStruct((1, C), f32)] * n,
            grid=(C // bw,),
            in_specs=[col(S)] * n + [col(W)] * n + [col(1)] * n + [col(S)],
            out_specs=[col(S)] * n + [col(8)] * n + [col(1)] * n,
            scratch_shapes=[pltpu.VMEM((S, bw), f32)] * n,
            compiler_params=_cparams(("parallel",)),
            name=name + "_bwd",
        )(*xs, *cws, *cbs, dy)

    @jax.custom_vjp
    def op(h, ws, cws, cbs):
        return call_fwd([_mm_call(h, w, "nn", f32) for w in ws], cws, cbs)

    def fwd(h, ws, cws, cbs):
        xs = [_mm_call(h, w, "nn", f32) for w in ws]
        return call_fwd(xs, cws, cbs), (h, ws, xs, cws, cbs)

    def bwd(res, dy):
        h, ws, xs, cws, cbs = res
        out = call_bwd(xs, cws, cbs, dy)
        dxs, dcws, dcbs = out[:n], out[n:2 * n], out[2 * n:]
        dh = _mm_call(dxs[0], ws[0], "nt", h.dtype)
        for dx, w in zip(dxs[1:], ws[1:]):
            dh = dh + _mm_call(dx, w, "nt", h.dtype)
        dws = tuple(_mm_call(h, dx, "tn", w.dtype) for dx, w in zip(dxs, ws))
        return dh, dws, tuple(d[:W] for d in dcws), tuple(dcbs)

    op.defvjp(fwd, bwd)
    return op(h, tuple(ws), tuple(cws), tuple(b.reshape(1, C) for b in cbs))


def _scan_chunk(q, k, x, a_tok, dt_tok, h, *, rev, incl, nsub, head0):
    L, N = q.shape
    W = x.shape[1]
    nh = W // HEAD_W
    t = lax.broadcasted_iota(jnp.int32, (L, L), 0)
    l = lax.broadcasted_iota(jnp.int32, (L, L), 1)
    if rev:
        cm, cmT = l >= t, t >= l
        mask = cm if incl else l > t
    else:
        cm, cmT = l <= t, t <= l
        mask = cm if incl else l < t
    eye = t == l
    lane_a = lax.broadcasted_iota(jnp.int32, a_tok.shape, 1)
    vhead = lax.broadcasted_iota(jnp.int32, (1, W), 1) // HEAD_W
    qhead = lax.broadcasted_iota(jnp.int32, (1, N), 1) // (N // nsub)

    decay, lam_e, tau_e, gam_e, dt_e = [], 0.0, 0.0, 0.0, 0.0
    for i in range(nh):
        a_col = jnp.sum(jnp.where(lane_a == head0 + i, a_tok, 0.0), axis=1, keepdims=True)
        a_row = jnp.sum(jnp.where(eye, a_col, 0.0), axis=0, keepdims=True)
        cs_col = jnp.sum(jnp.where(cm, a_row, 0.0), axis=1, keepdims=True)
        cs_row = jnp.sum(jnp.where(cmT, a_col, 0.0), axis=0, keepdims=True)
        tot = jnp.sum(a_col, axis=0, keepdims=True)
        decay.append(jnp.where(mask, jnp.exp(jnp.where(mask, cs_col - cs_row, 0.0)), 0.0))
        sel = vhead == i
        lam_e = lam_e + jnp.where(sel, jnp.exp(cs_col), 0.0)
        tau_e = tau_e + jnp.where(sel, jnp.exp(tot - cs_col), 0.0)
        gam_e = gam_e + jnp.where(sel, jnp.exp(tot), 0.0)
        if dt_tok is not None:
            dt_col = jnp.sum(jnp.where(lane_a == head0 + i, dt_tok, 0.0), axis=1, keepdims=True)
            dt_e = dt_e + jnp.where(sel, dt_col, 0.0)
    v = x if dt_tok is None else x * dt_e
    s_shared = bdot(q, k, "nt") if nsub == 1 else None
    y = lam_e * bdot(q, h, "nn")
    for i in range(nh):
        s = s_shared if nsub == 1 else bdot(jnp.where(qhead == i, q, 0.0), k, "nt")
        y = y + jnp.where(vhead == i, bdot(s * decay[i], v, "nn"), 0.0)
    hn = gam_e * h + bdot(k, tau_e * v, "tn")
    if nsub > 1:
        nhead = lax.broadcasted_iota(jnp.int32, (N, W), 0) // (N // nsub)
        hn = jnp.where(nhead == lax.broadcasted_iota(jnp.int32, (N, W), 1) // HEAD_W, hn, 0.0)
    return y, hn


def scan_op(q, k, x, a_tok, dt_tok, *, rev, incl, nsub):
    S = q.shape[0]
    G, _, Hg = a_tok.shape
    N = q.shape[1] // G
    Vw = x.shape[1] // G
    L = CHUNK
    nc = S // L
    use_dt = dt_tok is not None
    chunk = functools.partial(_scan_chunk, rev=rev, incl=incl, nsub=nsub)
    PW = min(Vw, LANES)
    blocks = [(p * PW, (p * PW) // HEAD_W) for p in range(Vw // PW)]

    def order(c, backward):
        return (nc - 1 - c) if (rev != backward) else c

    def specs(backward):
        qs = pl.BlockSpec((L, N), lambda g, c: (order(c, backward), g))
        xs = pl.BlockSpec((L, Vw), lambda g, c: (order(c, backward), g))
        as_ = pl.BlockSpec((1, L, Hg), lambda g, c: (g, order(c, backward), 0))
        hs = pl.BlockSpec((1, 1, N, Vw), lambda g, c: (g, order(c, backward), 0, 0))
        return qs, xs, as_, hs

    def call_fwd(q, k, x, a_tok, dt_tok):
        qs, xs, as_, hs = specs(False)

        def body(*refs):
            if use_dt:
                q_ref, k_ref, x_ref, a_ref, dt_ref, y_ref, hs_ref, h_scr = refs
            else:
                q_ref, k_ref, x_ref, a_ref, y_ref, hs_ref, h_scr = refs

            @pl.when(pl.program_id(1) == 0)
            def _():
                h_scr[...] = jnp.zeros_like(h_scr)

            hs_ref[0, 0] = h_scr[...]
            q, k, a, dt = q_ref[...], k_ref[...], a_ref[0], dt_ref[0] if use_dt else None
            for lane0, head0 in blocks:
                cols = slice(lane0, lane0 + PW)
                y, hn = chunk(q, k, x_ref[:, cols], a, dt, h_scr[:, cols], head0=head0)
                y_ref[:, cols] = y
                h_scr[:, cols] = hn

        ins = [q, k, x, a_tok] + ([dt_tok] if use_dt else [])
        return pl.pallas_call(
            body,
            out_shape=[jax.ShapeDtypeStruct((S, G * Vw), f32), jax.ShapeDtypeStruct((G, nc, N, Vw), f32)],
            grid=(G, nc),
            in_specs=[qs, qs, xs, as_] + ([as_] if use_dt else []),
            out_specs=[xs, hs],
            scratch_shapes=[pltpu.VMEM((N, Vw), f32)],
            compiler_params=_cparams(("parallel", "arbitrary")),
            name="scan_fwd",
        )(*ins)

    def call_bwd(q, k, x, a_tok, dt_tok, hsave, dy):
        qs, xs, as_, hs = specs(True)

        def body(*refs):
            if use_dt:
                q_ref, k_ref, x_ref, a_ref, dt_ref, hs_ref, dy_ref, dq_ref, dk_ref, dx_ref, da_ref, ddt_ref, dh_scr = refs
            else:
                q_ref, k_ref, x_ref, a_ref, hs_ref, dy_ref, dq_ref, dk_ref, dx_ref, da_ref, dh_scr = refs

            @pl.when(pl.program_id(1) == 0)
            def _():
                dh_scr[...] = jnp.zeros_like(dh_scr)

            q, k, a = q_ref[...].astype(f32), k_ref[...].astype(f32), a_ref[0]
            dq, dk, da, ddt = 0.0, 0.0, 0.0, 0.0
            for lane0, head0 in blocks:
                cols = slice(lane0, lane0 + PW)
                if use_dt:
                    f = lambda q, k, x, a, dt, h: chunk(q, k, x, a, dt, h, head0=head0)
                    prim = [q, k, x_ref[:, cols], a, dt_ref[0], hs_ref[0, 0, :, cols]]
                else:
                    f = lambda q, k, x, a, h: chunk(q, k, x, a, None, h, head0=head0)
                    prim = [q, k, x_ref[:, cols], a, hs_ref[0, 0, :, cols]]
                _, vjp = jax.vjp(f, *prim)
                cts = vjp((dy_ref[:, cols], dh_scr[:, cols]))
                dq, dk, da = dq + cts[0], dk + cts[1], da + cts[3]
                if use_dt:
                    ddt = ddt + cts[4]
                dx_ref[:, cols] = cts[2]
                dh_scr[:, cols] = cts[-1]
            dq_ref[...] = dq.astype(dq_ref.dtype)
            dk_ref[...] = dk.astype(dk_ref.dtype)
            da_ref[0] = da
            if use_dt:
                ddt_ref[0] = ddt

        ins = [q, k, x, a_tok] + ([dt_tok] if use_dt else []) + [hsave, dy]
        a_shape = jax.ShapeDtypeStruct(a_tok.shape, f32)
        return pl.pallas_call(
            body,
            out_shape=[jax.ShapeDtypeStruct(q.shape, q.dtype), jax.ShapeDtypeStruct(k.shape, k.dtype),
                       jax.ShapeDtypeStruct(x.shape, f32), a_shape] + ([a_shape] if use_dt else []),
            grid=(G, nc),
            in_specs=[qs, qs, xs, as_] + ([as_] if use_dt else []) + [hs, xs],
            out_specs=[qs, qs, xs, as_] + ([as_] if use_dt else []),
            scratch_shapes=[pltpu.VMEM((N, Vw), f32)],
            compiler_params=_cparams(("parallel", "arbitrary")),
            name="scan_bwd",
        )(*ins)

    if use_dt:
        @jax.custom_vjp
        def op(q, k, x, a_tok, dt_tok):
            return call_fwd(q, k, x, a_tok, dt_tok)[0]

        def fwd(q, k, x, a_tok, dt_tok):
            y, hsave = call_fwd(q, k, x, a_tok, dt_tok)
            return y, (q, k, x, a_tok, dt_tok, hsave)

        def bwd(res, dy):
            q, k, x, a_tok, dt_tok, hsave = res
            return tuple(call_bwd(q, k, x, a_tok, dt_tok, hsave, dy))

        op.defvjp(fwd, bwd)
        return op(q, k, x, a_tok, dt_tok)

    @jax.custom_vjp
    def op(q, k, x, a_tok):
        return call_fwd(q, k, x, a_tok, None)[0]

    def fwd(q, k, x, a_tok):
        y, hsave = call_fwd(q, k, x, a_tok, None)
        return y, (q, k, x, a_tok, hsave)

    def bwd(res, dy):
        q, k, x, a_tok, hsave = res
        return tuple(call_bwd(q, k, x, a_tok, None, hsave, dy))

    op.defvjp(fwd, bwd)
    return op(q, k, x, a_tok)


def _swap_halves(x, dh):
    W = x.shape[1]
    lane = lax.broadcasted_iota(jnp.int32, (1, W), 1) % dh
    return jnp.where(lane < dh // 2, pltpu.roll(x, W - dh // 2, 1), pltpu.roll(x, dh // 2, 1))


def rotary(rq, rk, cos_t, sin_t):
    scale = RET_DH ** -0.5

    def fn(rq, rk, c, s):
        return rq * c + _swap_halves(rq, RET_DH) * s, (rk * c + _swap_halves(rk, RET_DH) * s) * scale

    def bwd_fn(rv, pv, dos):
        _, _, c, s = rv
        dq, dk = dos
        dk = dk * scale
        return (dq * c + _swap_halves(dq * s, RET_DH), dk * c + _swap_halves(dk * s, RET_DH)), ()

    return rowwise(fn, "rotary", [rq, rk, cos_t, sin_t], [], [MXU_DTYPE, MXU_DTYPE], n_diff_rows=2, bwd_fn=bwd_fn)


def _rope_tables(S, width):
    half = RET_DH // 2
    inv = 1.0 / (ROPE_BASE ** (jnp.arange(half, dtype=f32) / half))
    ang = jnp.arange(S, dtype=f32)[:, None] * inv[None, :]
    cos, sin = jnp.cos(ang), jnp.sin(ang)
    reps = width // RET_DH
    return jnp.tile(jnp.concatenate([cos, cos], axis=1), (1, reps)), jnp.tile(jnp.concatenate([-sin, sin], axis=1), (1, reps))


def _exact_dot(x, m):
    return jnp.dot(x, m, precision=lax.Precision.HIGHEST, preferred_element_type=f32)


def ret_post(y_f, y_b, rg, gn_g):
    W = y_f.shape[1]
    idx = np.arange(W) // RET_DH
    avg = jnp.asarray((idx[:, None] == idx[None, :]).astype(np.float32) / RET_DH)

    def fn(yf, yb, rg, g, avg):
        y = yf + yb
        mu = _exact_dot(y, avg)
        d = y - mu
        var = _exact_dot(d * d, avg)
        return (_silu(rg) * (d * lax.rsqrt(var + EPS) * g),)

    return rowwise(fn, "ret_post", [y_f, y_b, rg], [gn_g.reshape(1, -1), avg], [MXU_DTYPE], n_diff_params=1)[0]


def _na_bias(rpb, win_r):
    H = rpb.shape[0]
    qc = np.arange(GRID_W)[:, None]
    kc = np.arange(GRID_W)[None, :]
    cstart = np.clip(qc - NA_WIN_C // 2, 0, GRID_W - NA_WIN_C)
    valid = (kc >= cstart) & (kc < cstart + NA_WIN_C)
    dc = np.clip(kc - qc, -(NA_WIN_C - 1), NA_WIN_C - 1) + (NA_WIN_C - 1)
    onehot = (dc[None] == np.arange(2 * NA_WIN_C - 1)[:, None, None]).astype(np.float32)
    t1 = jnp.einsum("hrd,dqk->hrqk", rpb.astype(f32), jnp.asarray(onehot), precision=lax.Precision.HIGHEST)
    per_delta = [t1[:, NA_WIN_R - 1 - d:NA_WIN_R - 1 - d + win_r] for d in range(win_r)]
    b = jnp.stack(per_delta, axis=1)
    b = jnp.where(jnp.asarray(valid)[None, None, None], b, NEG_INF)
    return jnp.transpose(b, (0, 1, 3, 2, 4)).reshape(H, win_r, GRID_W, win_r * GRID_W)


def _na_row(q, kw, vw, biases):
    lane = lax.broadcasted_iota(jnp.int32, (1, q.shape[1]), 1) // NA_DH
    o = 0.0
    for i, b in enumerate(biases):
        qi = jnp.where(lane == i, q, 0.0) * (NA_DH ** -0.5)
        s = bdot(qi, kw, "nt") + b
        e = jnp.exp(s - jnp.max(s, axis=1, keepdims=True))
        p = e / jnp.sum(e, axis=1, keepdims=True)
        o = o + jnp.where(lane == i, bdot(p, vw, "nn"), 0.0)
    return o


def _na_row_bwd(q, kw, vw, biases, do):
    lane = lax.broadcasted_iota(jnp.int32, (1, q.shape[1]), 1) // NA_DH
    scale = NA_DH ** -0.5
    dq, dk, dv, dbs = 0.0, 0.0, 0.0, []
    for i, b in enumerate(biases):
        sel = lane == i
        qi = jnp.where(sel, q, 0.0) * scale
        s = _bdot_raw(qi, kw, "nt") + b
        e = jnp.exp(s - jnp.max(s, axis=1, keepdims=True))
        p = e / jnp.sum(e, axis=1, keepdims=True)
        doi = jnp.where(sel, do, 0.0)
        dp = _bdot_raw(doi, vw, "nt")
        ds = p * (dp - jnp.sum(dp * p, axis=1, keepdims=True))
        dbs.append(ds)
        dq = dq + jnp.where(sel, _bdot_raw(ds, kw, "nn"), 0.0) * scale
        dk = dk + _bdot_raw(ds, qi, "tn")
        dv = dv + _bdot_raw(p, doi, "tn")
    return dq, dk, dv, dbs


def na_op(nq, nk, nv, bias):
    S, W = nq.shape
    rows = S // GRID_W
    win_r = bias.shape[1]
    nkeys = win_r * GRID_W
    hp = LANES // NA_DH
    npair = W // LANES
    RB = min(16, rows)
    nrb = rows // RB
    qspec = pl.BlockSpec((RB * GRID_W, LANES), lambda p, r: (r, p))
    kspec = pl.BlockSpec((S, LANES), lambda p, r: (0, p))
    bspec = pl.BlockSpec((hp, win_r, GRID_W, nkeys), lambda p, r: (p, 0, 0, 0))

    def window(r):
        r0 = jnp.clip(r - win_r // 2, 0, rows - win_r)
        return pl.multiple_of(r0 * GRID_W, GRID_W), r - r0

    def call_fwd(nq, nk, nv, bias):
        def body(q_ref, k_ref, v_ref, b_ref, o_ref):
            rb = pl.program_id(1)

            def row(i, c):
                k0, d = window(rb * RB + i)
                q0 = pl.multiple_of(i * GRID_W, GRID_W)
                o = _na_row(q_ref[pl.ds(q0, GRID_W), :].astype(f32), k_ref[pl.ds(k0, nkeys), :], v_ref[pl.ds(k0, nkeys), :],
                            [b_ref[h, pl.ds(d, 1)][0] for h in range(hp)])
                o_ref[pl.ds(q0, GRID_W), :] = o.astype(o_ref.dtype)
                return c

            lax.fori_loop(0, RB, row, 0, unroll=4)

        return pl.pallas_call(
            body,
            out_shape=jax.ShapeDtypeStruct((S, W), nq.dtype),
            grid=(npair, nrb),
            in_specs=[qspec, kspec, kspec, bspec],
            out_specs=qspec,
            compiler_params=_cparams(("parallel", "arbitrary")),
            name="na_fwd",
        )(nq, nk, nv, bias)

    def call_bwd(nq, nk, nv, bias, do):
        def body(q_ref, k_ref, v_ref, b_ref, do_ref, dq_ref, dk_ref, dv_ref, db_ref, dk_acc, dv_acc):
            rb = pl.program_id(1)

            @pl.when(rb == 0)
            def _():
                dk_acc[...] = jnp.zeros_like(dk_acc)
                dv_acc[...] = jnp.zeros_like(dv_acc)
                db_ref[...] = jnp.zeros_like(db_ref)

            def row(i, c):
                k0, d = window(rb * RB + i)
                q0 = pl.multiple_of(i * GRID_W, GRID_W)
                bs = [b_ref[h, pl.ds(d, 1)][0] for h in range(hp)]
                dq, dk, dv, dbs = _na_row_bwd(q_ref[pl.ds(q0, GRID_W), :].astype(f32), k_ref[pl.ds(k0, nkeys), :],
                                              v_ref[pl.ds(k0, nkeys), :], bs, do_ref[pl.ds(q0, GRID_W), :].astype(f32))
                dq_ref[pl.ds(q0, GRID_W), :] = dq.astype(dq_ref.dtype)
                dk_acc[pl.ds(k0, nkeys), :] += dk
                dv_acc[pl.ds(k0, nkeys), :] += dv
                for h in range(hp):
                    db_ref[h, pl.ds(d, 1)] += dbs[h][None]
                return c

            lax.fori_loop(0, RB, row, 0, unroll=2)

            @pl.when(rb == nrb - 1)
            def _():
                dk_ref[...] = dk_acc[...].astype(dk_ref.dtype)
                dv_ref[...] = dv_acc[...].astype(dv_ref.dtype)

        return pl.pallas_call(
            body,
            out_shape=[jax.ShapeDtypeStruct((S, W), nq.dtype), jax.ShapeDtypeStruct((S, W), nk.dtype),
                       jax.ShapeDtypeStruct((S, W), nv.dtype), jax.ShapeDtypeStruct(bias.shape, f32)],
            grid=(npair, nrb),
            in_specs=[qspec, kspec, kspec, bspec, qspec],
            out_specs=[qspec, kspec, kspec, bspec],
            scratch_shapes=[pltpu.VMEM((S, LANES), f32), pltpu.VMEM((S, LANES), f32)],
            compiler_params=_cparams(("parallel", "arbitrary")),
            name="na_bwd",
        )(nq, nk, nv, bias, do)

    @jax.custom_vjp
    def op(nq, nk, nv, bias):
        return call_fwd(nq, nk, nv, bias)

    def fwd(nq, nk, nv, bias):
        return call_fwd(nq, nk, nv, bias), (nq, nk, nv, bias)

    def bwd(res, do):
        return tuple(call_bwd(*res, do))

    op.defvjp(fwd, bwd)
    return op(nq, nk, nv, bias)


def ssd_dt(dt_raw, dt_bias, a_neg):
    def fn(r, b, a):
        dt = _softplus(r + b)
        return dt, dt * a

    return rowwise(fn, "ssd_dt", [dt_raw], [dt_bias, a_neg], [f32, f32])


def ssd_post(y_f, y_b, xs, z, d_skip_lanes, norm_g, groups):
    def fn(yf, yb, xs, z, dsk, g):
        y = (yf + yb + xs * dsk) * _silu(z)
        return (y * lax.rsqrt(jnp.mean(y * y, axis=-1, keepdims=True) + EPS) * g,)

    return rowwise(fn, "ssd_post", [y_f, y_b, xs, z], [d_skip_lanes.reshape(1, -1), norm_g.reshape(1, -1)], [MXU_DTYPE],
                   ncol=groups)[0]


def _heads_major(t, groups):
    S = t.shape[0]
    return jnp.transpose(t.reshape(S, groups, -1), (1, 0, 2))


def retention_na_mixer(hn, w_in, decay_logit, gn_g, rpb, w_out, tables):
    S = hn.shape[0]
    R = RET_HEADS * RET_DH
    NW = NA_HEADS * NA_DH
    cols = lambda a, b: w_in[:, a:b]
    rq, rk, rv, rg = (mm(hn, cols(j * R, (j + 1) * R)) for j in range(4))
    nq, nk, nv = (mm(hn, cols(4 * R + j * NW, 4 * R + (j + 1) * NW), out_dtype=MXU_DTYPE) for j in range(3))
    qr, kr = rotary(rq, rk, *tables)
    log_gamma = -_softplus(-decay_logit.astype(f32))
    pairs = R // LANES
    hp = LANES // RET_DH
    a_f = jnp.broadcast_to(log_gamma[0].reshape(pairs, 1, hp), (pairs, S, hp))
    a_b = jnp.broadcast_to(log_gamma[1].reshape(pairs, 1, hp), (pairs, S, hp))
    y_f = scan_op(qr, kr, rv, a_f, None, rev=False, incl=True, nsub=hp)
    y_b = scan_op(qr, kr, rv, a_b, None, rev=True, incl=False, nsub=hp)
    ret = ret_post(y_f, y_b, rg, gn_g)
    rows = S // GRID_W
    nao = na_op(nq, nk, nv, _na_bias(rpb, min(NA_WIN_R, rows)))
    return mm(ret, w_out[:R]) + mm(nao, w_out[R:])


def ssd_mixer(hn, w_in, conv_w, conv_b, dt_bias, a_log, d_skip, norm_g, w_out):
    heads = d_skip.shape[0]
    inner = heads * SSD_HEADDIM
    gs = SSD_GROUPS * SSD_STATE
    o_x, o_b, o_c, o_dt = inner, 2 * inner, 2 * inner + gs, 2 * inner + 2 * gs
    z = mm(hn, w_in[:, :inner])
    dt_raw = mm(hn, w_in[:, o_dt:])
    xs, bm, cm = (mm_conv_act(hn, [w_in[:, a:b]], [conv_w[:, a - inner:b - inner]], [conv_b[a - inner:b - inner]],
                              _silu, f32, "conv_silu") for a, b in ((o_x, o_b), (o_b, o_c), (o_c, o_dt)))
    a_neg = -jnp.exp(a_log.astype(f32)).reshape(1, -1)
    dt, la = ssd_dt(dt_raw, dt_bias.astype(f32).reshape(1, -1), a_neg)
    dt_f, dt_b = _heads_major(dt[:, :heads], SSD_GROUPS), _heads_major(dt[:, heads:], SSD_GROUPS)
    la_f, la_b = _heads_major(la[:, :heads], SSD_GROUPS), _heads_major(la[:, heads:], SSD_GROUPS)
    y_f = scan_op(cm, bm, xs, la_f, dt_f, rev=False, incl=True, nsub=1)
    y_b = scan_op(cm, bm, xs, la_b, dt_b, rev=True, incl=False, nsub=1)
    y = ssd_post(y_f, y_b, xs, z, jnp.repeat(d_skip.astype(f32), SSD_HEADDIM), norm_g, SSD_GROUPS)
    return mm(y, w_out)


def conv_geglu_ffn(hf, w_up, conv_w, conv_b, w_down):
    F = w_down.shape[0]
    a = mm_conv_act(hf, [w_up[:, :F], w_up[:, F:]], [conv_w[:, :F], conv_w[:, F:]], [conv_b[:F], conv_b[F:]],
                    lambda g, v: _gelu_tanh(g) * v, MXU_DTYPE, "conv_geglu")
    return mm(a, w_down)


def model_loss(x, tgt, big, small, rep):
    S = x.shape[0]
    depth = rep["norm_mix_pre"].shape[0]
    tables = _rope_tables(S, RET_HEADS * RET_DH)
    hn = rms(x, rep["norm_mix_pre"][0], MXU_DTYPE)
    for layer in range(depth):
        i = layer // 2
        if layer % 2 == 0:
            m = retention_na_mixer(hn, big["ab_w_in"][i], rep["ab_ret_decay_logit"][i], rep["ab_ret_gn_g"][i],
                                   rep["ab_na_rpb"][i], big["ab_w_out"][i], tables)
        else:
            m = ssd_mixer(hn, big["c_w_in"][i], small["c_conv_w"][i], small["c_conv_b"][i], rep["c_dt_bias"][i],
                          rep["c_a_log"][i], rep["c_d_skip"][i], small["c_norm_g"][i], big["c_w_out"][i])
        x, hf = rms_residual_norm(m, rep["norm_mix_post"][layer], x, rep["norm_ffn_pre"][layer])
        f = conv_geglu_ffn(hf, big["ffn_w_up"][layer], small["ffn_conv_w"][layer], rep["ffn_conv_b"][layer],
                           big["ffn_w_down"][layer])
        if layer + 1 < depth:
            x, hn = rms_residual_norm(f, rep["norm_ffn_post"][layer], x, rep["norm_mix_pre"][layer + 1])
        else:
            x = rms_residual(f, rep["norm_ffn_post"][layer], x)
    return loss_op(x, tgt)


def _mesh_pos():
    return lax.axis_index("x"), lax.axis_index("y"), lax.axis_index("c")


def gather_chips(local):
    R, Wd = local.shape

    half = R // 2
    CH = COPY_CHUNKS
    q = half // CH

    def body(x_ref, out_ref, send_sems, recv_sems):
        x, y, c = _mesh_pos()
        my = 2 * x + y
        chips = [(1 - x, y), (x, 1 - y), (1 - x, 1 - y)]

        def piece(ref, h, j):
            return ref.at[pl.ds(pl.multiple_of(h * half + j * q, PACK_ALIGN), q), :]

        def copy(k, src, chip, h, j, to):
            return pltpu.make_async_remote_copy(src_ref=src, dst_ref=piece(out_ref.at[chip], h, j), send_sem=send_sems.at[k],
                                                recv_sem=recv_sems.at[k], device_id=to, device_id_type=pl.DeviceIdType.MESH)

        first = [[copy(k * CH + j, piece(x_ref, c, j), my, c, j, (cx, cy, c)) for j in range(CH)]
                 for k, (cx, cy) in enumerate(chips)]
        for j in range(CH):
            for k in range(3):
                first[k][j].start()
        passed = [[copy((3 + k) * CH + j, piece(out_ref.at[2 * cx + cy], c, j), 2 * cx + cy, c, j, (x, y, 1 - c))
                   for j in range(CH)] for k, (cx, cy) in enumerate(chips)]
        for j in range(CH):
            for k, (cx, cy) in enumerate(chips):
                copy(k * CH + j, piece(x_ref, c, j), 2 * cx + cy, c, j, (cx, cy, c)).wait_recv()
                passed[k][j].start()
        for j in range(CH):
            for k, (cx, cy) in enumerate(chips):
                copy((3 + k) * CH + j, piece(x_ref, c, j), 2 * cx + cy, 1 - c, j, (x, y, 1 - c)).wait_recv()
        for k in range(3):
            for cp in first[k] + passed[k]:
                cp.wait_send()

    return pl.pallas_call(
        body,
        out_shape=jax.ShapeDtypeStruct((N_CHIPS, R, Wd), local.dtype),
        in_specs=[pl.BlockSpec(memory_space=pl.ANY)],
        out_specs=pl.BlockSpec(memory_space=pl.ANY),
        scratch_shapes=[pltpu.SemaphoreType.DMA((6 * CH,)), pltpu.SemaphoreType.DMA((6 * CH,))],
        name="gather_chips",
    )(local)


def pair_swap(parts):
    n, R, Wd = parts.shape
    half = R // 2

    CH = COPY_CHUNKS
    q = half // CH

    def body(p_ref, got_ref, send_sems, recv_sems):
        x, y, c = _mesh_pos()

        def src(s, j):
            return p_ref.at[s, pl.ds(pl.multiple_of((1 - c) * half + j * q, PACK_ALIGN), q), :]

        swap = [pltpu.make_async_remote_copy(src_ref=src(s, j), dst_ref=got_ref.at[s, pl.ds(j * q, q), :],
                                             send_sem=send_sems.at[s * CH + j], recv_sem=recv_sems.at[s * CH + j],
                                             device_id=(x, y, 1 - c), device_id_type=pl.DeviceIdType.MESH)
                for s in range(n) for j in range(CH)]
        for cp in swap:
            cp.start()
        for cp in swap:
            cp.wait()

    return pl.pallas_call(
        body,
        out_shape=jax.ShapeDtypeStruct((n, half, Wd), parts.dtype),
        in_specs=[pl.BlockSpec(memory_space=pl.ANY)],
        out_specs=pl.BlockSpec(memory_space=pl.ANY),
        scratch_shapes=[pltpu.SemaphoreType.DMA((n * CH,)), pltpu.SemaphoreType.DMA((n * CH,))],
        name="pair_swap",
    )(parts)


def chip_exchange(parts):
    n, R, Wd = parts.shape

    def body(p_ref, out_ref, send_sems, recv_sems):
        x, y, c = _mesh_pos()
        my = 2 * x + y
        chips = [(1 - x, y), (x, 1 - y), (1 - x, 1 - y)]

        def copy(k, src_slot, dst_slot, to):
            return pltpu.make_async_remote_copy(src_ref=p_ref.at[src_slot], dst_ref=out_ref.at[dst_slot], send_sem=send_sems.at[k],
                                                recv_sem=recv_sems.at[k], device_id=to, device_id_type=pl.DeviceIdType.MESH)

        sends = [copy(k, 2 * cx + cy, my, (cx, cy, c)) for k, (cx, cy) in enumerate(chips)]
        for cp in sends:
            cp.start()
        for k, (cx, cy) in enumerate(chips):
            copy(k, my, 2 * cx + cy, (cx, cy, c)).wait_recv()
        for cp in sends:
            cp.wait_send()

    return pl.pallas_call(
        body,
        out_shape=jax.ShapeDtypeStruct((n, R, Wd), parts.dtype),
        in_specs=[pl.BlockSpec(memory_space=pl.ANY)],
        out_specs=pl.BlockSpec(memory_space=pl.ANY),
        scratch_shapes=[pltpu.SemaphoreType.DMA((3,)), pltpu.SemaphoreType.DMA((3,))],
        name="chip_exchange",
    )(parts)


def pair_share(mine):
    R, Wd = mine.shape

    CH = 2 * COPY_CHUNKS
    q = R // CH

    def body(m_ref, out_ref, send_sems, recv_sems):
        x, y, c = _mesh_pos()
        swap = [pltpu.make_async_remote_copy(src_ref=m_ref.at[pl.ds(j * q, q), :], dst_ref=out_ref.at[pl.ds(j * q, q), :],
                                             send_sem=send_sems.at[j], recv_sem=recv_sems.at[j], device_id=(x, y, 1 - c),
                                             device_id_type=pl.DeviceIdType.MESH) for j in range(CH)]
        for cp in swap:
            cp.start()
        for cp in swap:
            cp.wait()

    return pl.pallas_call(
        body,
        out_shape=jax.ShapeDtypeStruct((R, Wd), mine.dtype),
        in_specs=[pl.BlockSpec(memory_space=pl.ANY)],
        out_specs=pl.BlockSpec(memory_space=pl.ANY),
        scratch_shapes=[pltpu.SemaphoreType.DMA((CH,)), pltpu.SemaphoreType.DMA((CH,))],
        name="pair_share",
    )(mine)


def sum_chips(recv, own):
    n, R, Wd = recv.shape
    tr = _pick(R, (512, 256, 128, 64, 32, 16, 8))

    def body(r_ref, p_ref, o_ref):
        my = 2 * lax.axis_index("x") + lax.axis_index("y")
        acc = jnp.zeros((tr, Wd), f32)
        for s in range(n):
            acc = acc + jnp.where(my == s, p_ref[s], r_ref[s]).astype(f32)
        o_ref[...] = acc

    spec = pl.BlockSpec((n, tr, Wd), lambda i: (0, i, 0))
    return pl.pallas_call(
        body,
        out_shape=jax.ShapeDtypeStruct((R, Wd), f32),
        grid=(R // tr,),
        in_specs=[spec, spec],
        out_specs=pl.BlockSpec((tr, Wd), lambda i: (i, 0)),
        compiler_params=_cparams(("parallel",)),
        name="sum_chips",
    )(recv, own)


def add_pair(parts, got):
    n, R, Wd = parts.shape
    half = R // 2
    tr = _pick(half, (512, 256, 128, 64, 32, 16, 8))
    nb = half // tr

    def body(lo_ref, hi_ref, g_ref, o_ref):
        mine = jnp.where(lax.axis_index("c") == 0, lo_ref[...], hi_ref[...])
        o_ref[...] = (mine.astype(f32) + g_ref[...].astype(f32)).astype(o_ref.dtype)

    spec = pl.BlockSpec((1, tr, Wd), lambda s, i: (s, i, 0))
    return pl.pallas_call(
        body,
        out_shape=jax.ShapeDtypeStruct(got.shape, parts.dtype),
        grid=(n, nb),
        in_specs=[spec, pl.BlockSpec((1, tr, Wd), lambda s, i: (s, nb + i, 0)), spec],
        out_specs=spec,
        compiler_params=_cparams(("parallel", "parallel")),
        name="add_pair",
    )(parts, parts, got)


def reduce_scatter(parts):
    chip_sum = add_pair(parts, pair_swap(parts))
    mine = sum_chips(chip_exchange(chip_sum), chip_sum)
    theirs = pair_share(mine)
    first = lax.axis_index("c") == 0
    return jnp.concatenate([jnp.where(first, mine, theirs), jnp.where(first, theirs, mine)], axis=0)


def adamw(w, g, m, v):
    R, C = w.shape
    tr = R
    for cand in (512, 256, 128, 64, 32, 16, 8):
        if R % cand == 0 and cand * C * 4 <= (1 << 20):
            tr = cand
            break

    def body(w_ref, g_ref, m_ref, v_ref, d_ref, mo_ref, vo_ref):
        g = g_ref[...]
        m = ADAM_B1 * m_ref[...] + (1.0 - ADAM_B1) * g
        v = ADAM_B2 * v_ref[...] + (1.0 - ADAM_B2) * (g * g)
        m_hat = m / (1.0 - ADAM_B1 ** ADAM_STEP)
        v_hat = v / (1.0 - ADAM_B2 ** ADAM_STEP)
        d_ref[...] = -ADAM_LR * (m_hat / (jnp.sqrt(v_hat) + ADAM_EPS) + ADAM_WD * w_ref[...])
        mo_ref[...] = m
        vo_ref[...] = v

    spec = pl.BlockSpec((tr, C), lambda i: (i, 0))
    return pl.pallas_call(
        body,
        out_shape=[jax.ShapeDtypeStruct((R, C), f32)] * 3,
        grid=(R // tr,),
        in_specs=[spec] * 4,
        out_specs=[spec] * 3,
        compiler_params=_cparams(("parallel",)),
        name="adamw",
    )(w, g, m, v)


def _pack(arrs, dtype):
    flat = jnp.concatenate([a.astype(dtype).reshape(-1) for a in arrs])
    n = flat.shape[0]
    unit = PACK_W * PACK_ROWS
    padded = -(-n // unit) * unit
    return jnp.pad(flat, (0, padded - n)).reshape(-1, PACK_W)


def _unpack(buf, shapes):
    flat = buf.reshape(-1)
    out, off = [], 0
    for s in shapes:
        n = int(np.prod(s))
        out.append(flat[off:off + n].reshape(s))
        off += n
    return out


BIG = (("ab_w_in", 2), ("ab_w_out", 1), ("c_w_in", 2), ("c_w_out", 1), ("ffn_w_up", 2), ("ffn_w_down", 1))
SMALL = (("c_conv_w", 2), ("c_conv_b", 1), ("c_norm_g", 1), ("ffn_conv_w", 2))
REP = ("norm_mix_pre", "norm_mix_post", "norm_ffn_pre", "norm_ffn_post", "ab_ret_decay_logit", "ab_ret_gn_g", "ab_na_rpb",
       "c_dt_bias", "c_a_log", "c_d_skip", "ffn_conv_b")
WEIGHTS = ("norm_mix_pre", "norm_mix_post", "norm_ffn_pre", "norm_ffn_post", "ab_w_in", "ab_ret_decay_logit", "ab_ret_gn_g",
           "ab_na_rpb", "ab_w_out", "c_w_in", "c_conv_w", "c_conv_b", "c_dt_bias", "c_a_log", "c_d_skip", "c_norm_g", "c_w_out",
           "ffn_w_up", "ffn_conv_w", "ffn_conv_b", "ffn_w_down")


def _gather_set(local, spec, dtype):
    shapes = [local[n].shape for n, _ in spec]
    packed = _pack([local[n] for n, _ in spec], dtype)
    got = gather_chips(packed)
    my = 2 * lax.axis_index("x") + lax.axis_index("y")
    per_chip = [_unpack(jnp.where(my == s, packed, got[s]), shapes) for s in range(N_CHIPS)]
    return {n: jnp.concatenate([per_chip[s][j] for s in range(N_CHIPS)], axis=ax) for j, (n, ax) in enumerate(spec)}


def _scatter_parts(full, spec, extra, dtype):
    split = {n: jnp.split(full[n], N_CHIPS, axis=ax) for n, ax in spec}
    return jnp.stack([_pack([split[n][s] for n, _ in spec] + list(extra), dtype) for s in range(N_CHIPS)])


def kernel(x, norm_mix_pre, norm_mix_post, norm_ffn_pre, norm_ffn_post, ab_w_in, ab_ret_decay_logit, ab_ret_gn_g, ab_na_rpb, ab_w_out, c_w_in, c_conv_w, c_conv_b, c_dt_bias, c_a_log, c_d_skip, c_norm_g, c_w_out, ffn_w_up, ffn_conv_w, ffn_conv_b, ffn_w_down, loss_target, m_norm_mix_pre, m_norm_mix_post, m_norm_ffn_pre, m_norm_ffn_post, m_ab_w_in, m_ab_ret_decay_logit, m_ab_ret_gn_g, m_ab_na_rpb, m_ab_w_out, m_c_w_in, m_c_conv_w, m_c_conv_b, m_c_dt_bias, m_c_a_log, m_c_d_skip, m_c_norm_g, m_c_w_out, m_ffn_w_up, m_ffn_conv_w, m_ffn_conv_b, m_ffn_w_down, v_norm_mix_pre, v_norm_mix_post, v_norm_ffn_pre, v_norm_ffn_post, v_ab_w_in, v_ab_ret_decay_logit, v_ab_ret_gn_g, v_ab_na_rpb, v_ab_w_out, v_c_w_in, v_c_conv_w, v_c_conv_b, v_c_dt_bias, v_c_a_log, v_c_d_skip, v_c_norm_g, v_c_w_out, v_ffn_w_up, v_ffn_conv_w, v_ffn_conv_b, v_ffn_w_down):
    args = dict(locals())
    w = {n: args[n] for n in WEIGHTS}
    mom = {n: args["m_" + n] for n in WEIGHTS}
    var = {n: args["v_" + n] for n in WEIGHTS}

    big = _gather_set(w, BIG, MXU_DTYPE)
    small = _gather_set(w, SMALL, f32)
    rep = {n: w[n] for n in REP}

    def loss_fn(xs, big, small, rep):
        return model_loss(xs, loss_target[0], big, small, rep)

    loss, (gx, gbig, gsmall, grep) = jax.value_and_grad(loss_fn, argnums=(0, 1, 2, 3))(x[0], big, small, rep)
    loss = lax.psum(loss, ("x", "y", "c"))

    big_shapes = [w[n].shape for n, _ in BIG]
    small_shapes = [w[n].shape for n, _ in SMALL] + [w[n].shape for n in REP]
    g_big = _unpack(reduce_scatter(_scatter_parts(gbig, BIG, (), MXU_DTYPE)), big_shapes)
    g_small_buf = reduce_scatter(_scatter_parts(gsmall, SMALL, [grep[n] for n in REP], f32))
    grads = dict(zip([n for n, _ in BIG], g_big))
    small_names = [n for n, _ in SMALL] + list(REP)
    grads.update(zip(small_names, _unpack(g_small_buf, small_shapes)))

    delta, new_m, new_v = {}, {}, {}
    for n, _ in BIG:
        shp = w[n].shape
        two_d = lambda a: a.reshape(-1, shp[-1])
        d, m2, v2 = adamw(two_d(w[n]), two_d(grads[n]), two_d(mom[n]), two_d(var[n]))
        delta[n], new_m[n], new_v[n] = d.reshape(shp), m2.reshape(shp), v2.reshape(shp)
    pk = lambda src: _pack([src[n] for n in small_names], f32)
    d, m2, v2 = adamw(pk(w), g_small_buf, pk(mom), pk(var))
    for dst, buf in ((delta, d), (new_m, m2), (new_v, v2)):
        dst.update(zip(small_names, _unpack(buf, small_shapes)))

    return (loss, gx[None], *[grads[n] for n in WEIGHTS], *[delta[n] for n in WEIGHTS],
            *[new_m[n] for n in WEIGHTS], *[new_v[n] for n in WEIGHTS])
```

```python
import functools
import math

import numpy as np
import jax
import jax.numpy as jnp
from jax import lax
from jax.experimental import pallas as pl
from jax.experimental.pallas import tpu as pltpu

f32 = jnp.float32
bf16 = jnp.bfloat16
MXU_DTYPE = bf16

GRID_W = 64
CHUNK = 128
EPS = 1e-6
RET_HEADS = 8
RET_DH = 64
ROPE_BASE = 10000.0
NA_HEADS = 8
NA_DH = 64
NA_WIN_R = 8
NA_WIN_C = 16
SSD_HEADDIM = 64
SSD_GROUPS = 4
SSD_STATE = 128
ADAM_LR = 0.001
ADAM_B1 = 0.9
ADAM_B2 = 0.999
ADAM_EPS = 1e-08
ADAM_WD = 0.01
ADAM_STEP = 10

LANES = 128
HEAD_W = 64
PACK_W = 512
PACK_ROWS = 1024
PACK_ALIGN = 16
COPY_CHUNKS = 4
VMEM_LIMIT = 56 * 1024 * 1024
MM_BLOCK_BYTES = 6 * 1024 * 1024
N_CHIPS = 4
N_DEV = 8
NEG_INF = -1e30

_DIMS = {"nn": (((1,), (0,)), ((), ())), "nt": (((1,), (1,)), ((), ())), "tn": (((0,), (0,)), ((), ()))}


def _cparams(sem=None):
    return pltpu.CompilerParams(dimension_semantics=sem, vmem_limit_bytes=VMEM_LIMIT)


def _pick(dim, cands):
    for c in cands:
        if dim % c == 0:
            return c
    return dim


def _divisor_tile(dim, fits, align):
    for d in range(1, dim + 1):
        t = dim // d
        if dim % d == 0 and t % align == 0 and fits(t):
            return t
    return dim


def _bdot_raw(a, b, mode):
    return lax.dot_general(a.astype(MXU_DTYPE), b.astype(MXU_DTYPE), _DIMS[mode], preferred_element_type=f32)


@functools.partial(jax.custom_vjp, nondiff_argnums=(2,))
def bdot(a, b, mode):
    return _bdot_raw(a, b, mode)


def _bdot_fwd(a, b, mode):
    return _bdot_raw(a, b, mode), (a, b)


def _bdot_bwd(mode, res, g):
    a, b = res
    if mode == "nn":
        da, db = _bdot_raw(g, b, "nt"), _bdot_raw(a, g, "tn")
    elif mode == "nt":
        da, db = _bdot_raw(g, b, "nn"), _bdot_raw(g, a, "tn")
    else:
        da, db = _bdot_raw(b, g, "nt"), _bdot_raw(a, g, "nn")
    return da.astype(a.dtype), db.astype(b.dtype)


bdot.defvjp(_bdot_fwd, _bdot_bwd)


def _mm_call(a, b, mode, out_dtype):
    if mode == "nn":
        (M, K), (K2, N) = a.shape, b.shape
    elif mode == "nt":
        (M, K), (N, K2) = a.shape, b.shape
    else:
        (K, M), (K2, N) = a.shape, b.shape
    assert K == K2, (a.shape, b.shape, mode)
    a_bytes, b_bytes, o_bytes = a.dtype.itemsize, b.dtype.itemsize, jnp.dtype(out_dtype).itemsize
    if mode == "tn":
        tk = _pick(K, (512, 256, 128))
        tn = _divisor_tile(N, lambda t: t <= 1536, LANES)
        tm = _divisor_tile(M, lambda t: t * tn * 4 <= MM_BLOCK_BYTES, 8)
    else:
        tk, tn = K, N
        tm = _divisor_tile(M, lambda t: t * K * a_bytes <= MM_BLOCK_BYTES and t * N * o_bytes <= MM_BLOCK_BYTES, 8)
    nk = K // tk
    if mode == "nn":
        a_spec = pl.BlockSpec((tm, tk), lambda i, j, k: (i, k))
        b_spec = pl.BlockSpec((tk, tn), lambda i, j, k: (k, j))
    elif mode == "nt":
        a_spec = pl.BlockSpec((tm, tk), lambda i, j, k: (i, k))
        b_spec = pl.BlockSpec((tn, tk), lambda i, j, k: (j, k))
    else:
        a_spec = pl.BlockSpec((tk, tm), lambda i, j, k: (k, i))
        b_spec = pl.BlockSpec((tk, tn), lambda i, j, k: (k, j))

    if nk == 1:
        def body(a_ref, b_ref, o_ref):
            o_ref[...] = _bdot_raw(a_ref[...], b_ref[...], mode).astype(o_ref.dtype)
    else:
        def body(a_ref, b_ref, o_ref, acc_ref):
            k = pl.program_id(2)

            @pl.when(k == 0)
            def _():
                acc_ref[...] = jnp.zeros_like(acc_ref)

            acc_ref[...] += _bdot_raw(a_ref[...], b_ref[...], mode)

            @pl.when(k == nk - 1)
            def _():
                o_ref[...] = acc_ref[...].astype(o_ref.dtype)

    return pl.pallas_call(
        body,
        out_shape=jax.ShapeDtypeStruct((M, N), out_dtype),
        grid=(M // tm, N // tn, nk),
        in_specs=[a_spec, b_spec],
        out_specs=pl.BlockSpec((tm, tn), lambda i, j, k: (i, j)),
        scratch_shapes=[pltpu.VMEM((tm, tn), f32)] if nk > 1 else [],
        compiler_params=_cparams(("parallel", "parallel", "arbitrary")),
        name="mm_" + mode,
    )(a, b)


def mm(a, b, mode="nn", out_dtype=f32):
    @jax.custom_vjp
    def op(a, b):
        return _mm_call(a, b, mode, out_dtype)

    def fwd(a, b):
        return _mm_call(a, b, mode, out_dtype), (a, b)

    def bwd(res, g):
        a, b = res
        if mode == "nn":
            return _mm_call(g, b, "nt", a.dtype), _mm_call(a, g, "tn", b.dtype)
        if mode == "nt":
            return _mm_call(g, b, "nn", a.dtype), _mm_call(g, a, "tn", b.dtype)
        return _mm_call(b, g, "nt", a.dtype), _mm_call(a, g, "nn", b.dtype)

    op.defvjp(fwd, bwd)
    return op(a, b)


def _row_tile(S, row_bytes):
    tm = 512
    while tm > 8 and (tm * row_bytes > (6 << 20) or S % tm):
        tm //= 2
    return tm


def rowwise(fn, name, rows, params, out_dtypes, n_diff_rows=None, n_diff_params=None, ncol=1, bwd_fn=None):
    rows, params = list(rows), list(params)
    nr, npar = len(rows), len(params)
    ndr = nr if n_diff_rows is None else n_diff_rows
    ndp = npar if n_diff_params is None else n_diff_params
    S = rows[0].shape[0]
    rw = [r.shape[1] // ncol for r in rows]
    pshape = [(p.shape[0], p.shape[1] // ncol) for p in params]

    def block_structs(tm):
        return ([jax.ShapeDtypeStruct((tm, w), f32) for w in rw] + [jax.ShapeDtypeStruct(s, f32) for s in pshape])

    outs_s = jax.eval_shape(fn, *block_structs(8))
    ow = [o.shape[1] for o in outs_s]
    nout = len(ow)
    row_bytes = 4 * (sum(rw) * 2 + sum(ow) * 2)
    tm = _row_tile(S, row_bytes)
    grid = (ncol, S // tm)

    def rspec(w):
        return pl.BlockSpec((tm, w), lambda g, i: (i, g))

    def pspec(s):
        return pl.BlockSpec(s, lambda g, i: (0, g))

    def call_fwd(*args):
        def body(*refs):
            vals = [r[...].astype(f32) for r in refs[:nr + npar]]
            res = fn(*vals)
            for o, r in zip(refs[nr + npar:], res):
                o[...] = r.astype(o.dtype)

        return pl.pallas_call(
            body,
            out_shape=[jax.ShapeDtypeStruct((S, w * ncol), dt) for w, dt in zip(ow, out_dtypes)],
            grid=grid,
            in_specs=[rspec(w) for w in rw] + [pspec(s) for s in pshape],
            out_specs=[rspec(w) for w in ow],
            compiler_params=_cparams(("parallel", "parallel")),
            name=name + "_fwd",
        )(*args)

    def call_bwd(args, douts):
        def body(*refs):
            in_refs = refs[:nr + npar]
            do_refs = refs[nr + npar:nr + npar + nout]
            dr_refs = refs[nr + npar + nout:nr + npar + nout + ndr]
            dp_refs = refs[nr + npar + nout + ndr:]
            rv = [r[...] for r in in_refs[:nr]]
            pv = [r[...] for r in in_refs[nr:]]
            dos = [d[...].astype(f32) for d in do_refs]
            if bwd_fn is not None:
                drs, dps = bwd_fn(rv, pv, dos)
            else:
                def f(*a):
                    return fn(*a[:ndr], *rv[ndr:], *a[ndr:], *pv[ndp:])

                _, vjp = jax.vjp(f, *[v.astype(f32) for v in rv[:ndr]], *pv[:ndp])
                cts = vjp(tuple(dos))
                drs, dps = cts[:ndr], cts[ndr:]
            for r, ct in zip(dr_refs, drs):
                r[...] = ct.astype(r.dtype)
            if ndp:
                @pl.when(pl.program_id(1) == 0)
                def _():
                    for r in dp_refs:
                        r[...] = jnp.zeros_like(r)

                for r, ct in zip(dp_refs, dps):
                    r[...] += ct

        return pl.pallas_call(
            body,
            out_shape=[jax.ShapeDtypeStruct(r.shape, r.dtype) for r in rows[:ndr]]
            + [jax.ShapeDtypeStruct(p.shape, f32) for p in params[:ndp]],
            grid=grid,
            in_specs=[rspec(w) for w in rw] + [pspec(s) for s in pshape] + [rspec(w) for w in ow],
            out_specs=[rspec(w) for w in rw[:ndr]] + [pspec(s) for s in pshape[:ndp]],
            compiler_params=_cparams(("parallel", "arbitrary")),
            name=name + "_bwd",
        )(*args, *douts)

    @jax.custom_vjp
    def op(*args):
        return tuple(call_fwd(*args))

    def fwd(*args):
        return tuple(call_fwd(*args)), args

    def bwd(args, douts):
        res = call_bwd(args, douts)
        drs, dps = res[:ndr], res[ndr:]
        out = list(drs) + [jnp.zeros_like(a) for a in args[ndr:nr]]
        out += [dp.astype(p.dtype) for dp, p in zip(dps, args[nr:nr + ndp])]
        out += [jnp.zeros_like(a) for a in args[nr + ndp:]]
        return tuple(out)

    op.defvjp(fwd, bwd)
    return op(*rows, *params)


def _silu(x):
    return x * (1.0 / (1.0 + jnp.exp(-x)))


def _softplus(x):
    return jnp.maximum(x, 0.0) + jnp.log(1.0 + jnp.exp(-jnp.abs(x)))


def _gelu_tanh(x):
    return 0.5 * x * (1.0 + jnp.tanh(math.sqrt(2.0 / math.pi) * (x + 0.044715 * (x * x * x))))


def _rms_fn(x, g):
    return x * lax.rsqrt(jnp.mean(x * x, axis=-1, keepdims=True) + EPS) * g


def rms(x, g, out_dtype):
    return rowwise(lambda x, g: (_rms_fn(x, g),), "rms", [x], [g.reshape(1, -1)], [out_dtype])[0]


def rms_residual_norm(m, g, x, g_next):
    def fn(m, x, g, gn):
        xn = x + _rms_fn(m, g)
        return xn, _rms_fn(xn, gn)

    return rowwise(fn, "rms_res_norm", [m, x], [g.reshape(1, -1), g_next.reshape(1, -1)], [f32, MXU_DTYPE])


def rms_residual(m, g, x):
    return rowwise(lambda m, x, g: (x + _rms_fn(m, g),), "rms_res", [m, x], [g.reshape(1, -1)], [f32])[0]


def loss_op(y, tgt):
    S, D = y.shape
    tm = _row_tile(S, 4 * D * 4)

    def call_fwd(y, tgt):
        def body(y_ref, t_ref, o_ref):
            @pl.when(pl.program_id(0) == 0)
            def _():
                o_ref[...] = jnp.zeros_like(o_ref)

            e = y_ref[...] - t_ref[...]
            o_ref[...] += 0.5 * jnp.sum(jnp.mean(e * e, axis=-1, keepdims=True))

        out = pl.pallas_call(
            body,
            out_shape=jax.ShapeDtypeStruct((8, LANES), f32),
            grid=(S // tm,),
            in_specs=[pl.BlockSpec((tm, D), lambda i: (i, 0))] * 2,
            out_specs=pl.BlockSpec((8, LANES), lambda i: (0, 0)),
            compiler_params=_cparams(("arbitrary",)),
            name="loss_fwd",
        )(y, tgt)
        return out[0, 0]

    def call_bwd(y, tgt, g):
        def body(y_ref, t_ref, g_ref, o_ref):
            o_ref[...] = (y_ref[...] - t_ref[...]) * (g_ref[...] * (1.0 / D))

        return pl.pallas_call(
            body,
            out_shape=jax.ShapeDtypeStruct((S, D), f32),
            grid=(S // tm,),
            in_specs=[pl.BlockSpec((tm, D), lambda i: (i, 0))] * 2 + [pl.BlockSpec((1, 1), lambda i: (0, 0))],
            out_specs=pl.BlockSpec((tm, D), lambda i: (i, 0)),
            compiler_params=_cparams(("parallel",)),
            name="loss_bwd",
        )(y, tgt, g.reshape(1, 1).astype(f32))

    @jax.custom_vjp
    def op(y, tgt):
        return call_fwd(y, tgt)

    def fwd(y, tgt):
        return call_fwd(y, tgt), (y, tgt)

    def bwd(res, g):
        y, tgt = res
        return call_bwd(y, tgt, g), jnp.zeros_like(tgt)

    op.defvjp(fwd, bwd)
    return op(y, tgt)


HALO = 8


def _conv_tile(S, R):
    def ext(ref, r0):
        cur = ref[pl.ds(r0, R), :]
        prev = ref[pl.ds(pl.multiple_of(jnp.maximum(r0 - HALO, 0), HALO), HALO), :]
        nxt = ref[pl.ds(pl.multiple_of(jnp.minimum(r0 + R, S - HALO), HALO), HALO), :]
        prev = jnp.where(r0 > 0, prev, 0.0)
        nxt = jnp.where(r0 + R < S, nxt, 0.0)
        return jnp.concatenate([prev, cur, nxt], axis=0)

    return ext


def _shift_rows(e, k, R):
    n = e.shape[0]
    if k == 0:
        return e[HALO:HALO + R]
    return pltpu.roll(e, (-k) % n, 0)[HALO:HALO + R]


def mm_conv_act(h, ws, cws, cbs, act, out_dtype, name):
    n = len(ws)
    S = h.shape[0]
    C = ws[0].shape[1]
    W = cws[0].shape[0]
    pad = W // 2
    bw = _pick(C, (LANES,))
    R = _pick(S, (256, 128, 64, 32, 16, 8))
    nt = S // R
    ext = _conv_tile(S, R)
    col = lambda rows: pl.BlockSpec((rows, bw), lambda j: (0, j))

    def conv(e, wv, bv):
        acc = bv + wv[pad] * e[HALO:HALO + R]
        for j in range(W):
            if j != pad:
                acc = acc + wv[j] * _shift_rows(e, j - pad, R)
        return acc

    def call_fwd(xs, cws, cbs):
        def body(*refs):
            x_refs, w_refs, b_refs, y_ref = refs[:n], refs[n:2 * n], refs[2 * n:3 * n], refs[3 * n]
            wvs = [[w[j:j + 1, :] for j in range(W)] for w in w_refs]
            bvs = [b[...] for b in b_refs]

            def tile(i, c):
                r0 = pl.multiple_of(i * R, R)
                us = [conv(ext(x, r0), wv, bv) for x, wv, bv in zip(x_refs, wvs, bvs)]
                y_ref[pl.ds(r0, R), :] = act(*us).astype(y_ref.dtype)
                return c

            lax.fori_loop(0, nt, tile, 0)

        return pl.pallas_call(
            body,
            out_shape=jax.ShapeDtypeStruct((S, C), out_dtype),
            grid=(C // bw,),
            in_specs=[col(S)] * n + [col(W)] * n + [col(1)] * n,
            out_specs=col(S),
            compiler_params=_cparams(("parallel",)),
            name=name + "_fwd",
        )(*xs, *cws, *cbs)

    def call_bwd(xs, cws, cbs, dy):
        def body(*refs):
            x_refs, w_refs, b_refs, dy_ref = refs[:n], refs[n:2 * n], refs[2 * n:3 * n], refs[3 * n]
            dx_refs, dw_refs, db_refs = refs[3 * n + 1:4 * n + 1], refs[4 * n + 1:5 * n + 1], refs[5 * n + 1:6 * n + 1]
            du_scr = refs[6 * n + 1:]
            wvs = [[w[j:j + 1, :] for j in range(W)] for w in w_refs]
            bvs = [b[...] for b in b_refs]
            zero = jnp.zeros((1, bw), f32)

            def first(i, dbs):
                r0 = pl.multiple_of(i * R, R)
                us = [conv(ext(x, r0), wv, bv) for x, wv, bv in zip(x_refs, wvs, bvs)]
                _, vjp = jax.vjp(act, *us)
                dus = vjp(dy_ref[pl.ds(r0, R), :].astype(f32))
                for scr, du in zip(du_scr, dus):
                    scr[pl.ds(r0, R), :] = du
                return tuple(db + jnp.sum(du, axis=0, keepdims=True) for db, du in zip(dbs, dus))

            dbs = lax.fori_loop(0, nt, first, tuple(zero for _ in range(n)))

            def second(i, dws):
                r0 = pl.multiple_of(i * R, R)
                new = []
                for x, scr, dx, wv, dw in zip(x_refs, du_scr, dx_refs, wvs, dws):
                    ex, ed = ext(x, r0), ext(scr, r0)
                    d0 = ed[HALO:HALO + R]
                    acc = jnp.zeros((R, bw), f32)
                    row = []
                    for j in range(W):
                        acc = acc + wv[j] * _shift_rows(ed, pad - j, R)
                        row.append(dw[j] + jnp.sum(d0 * _shift_rows(ex, j - pad, R), axis=0, keepdims=True))
                    dx[pl.ds(r0, R), :] = acc.astype(dx.dtype)
                    new.append(tuple(row))
                return tuple(new)

            dws = lax.fori_loop(0, nt, second, tuple(tuple(zero for _ in range(W)) for _ in range(n)))
            for dw_ref, db_ref, dw, db in zip(dw_refs, db_refs, dws, dbs):
                dw_ref[...] = jnp.zeros_like(dw_ref)
                for j in range(W):
                    dw_ref[j:j + 1, :] = dw[j]
                db_ref[...] = db

        return pl.pallas_call(
            body,
            out_shape=[jax.ShapeDtypeStruct((S, C), MXU_DTYPE)] * n + [jax.ShapeDtypeStruct((8, C), f32)] * n
            + [jax.ShapeDtypeStruct((1, C), f32)] * n,
            grid=(C // bw,),
            in_specs=[col(S)] * n + [col(W)] * n + [col(1)] * n + [col(S)],
            out_specs=[col(S)] * n + [col(8)] * n + [col(1)] * n,
            scratch_shapes=[pltpu.VMEM((S, bw), f32)] * n,
            compiler_params=_cparams(("parallel",)),
            name=name + "_bwd",
        )(*xs, *cws, *cbs, dy)

    @jax.custom_vjp
    def op(h, ws, cws, cbs):
        return call_fwd([_mm_call(h, w, "nn", f32) for w in ws], cws, cbs)

    def fwd(h, ws, cws, cbs):
        xs = [_mm_call(h, w, "nn", f32) for w in ws]
        return call_fwd(xs, cws, cbs), (h, ws, xs, cws, cbs)

    def bwd(res, dy):
        h, ws, xs, cws, cbs = res
        out = call_bwd(xs, cws, cbs, dy)
        dxs, dcws, dcbs = out[:n], out[n:2 * n], out[2 * n:]
        dh = _mm_call(dxs[0], ws[0], "nt", h.dtype)
        for dx, w in zip(dxs[1:], ws[1:]):
            dh = dh + _mm_call(dx, w, "nt", h.dtype)
        dws = tuple(_mm_call(h, dx, "tn", w.dtype) for dx, w in zip(dxs, ws))
        return dh, dws, tuple(d[:W] for d in dcws), tuple(dcbs)

    op.defvjp(fwd, bwd)
    return op(h, tuple(ws), tuple(cws), tuple(b.reshape(1, C) for b in cbs))


def _scan_chunk(q, k, x, a_tok, dt_tok, h, *, rev, incl, nsub, head0):
    L, N = q.shape
    W = x.shape[1]
    nh = W // HEAD_W
    t = lax.broadcasted_iota(jnp.int32, (L, L), 0)
    l = lax.broadcasted_iota(jnp.int32, (L, L), 1)
    if rev:
        cm, cmT = l >= t, t >= l
        mask = cm if incl else l > t
    else:
        cm, cmT = l <= t, t <= l
        mask = cm if incl else l < t
    eye = t == l
    lane_a = lax.broadcasted_iota(jnp.int32, a_tok.shape, 1)
    vhead = lax.broadcasted_iota(jnp.int32, (1, W), 1) // HEAD_W
    qhead = lax.broadcasted_iota(jnp.int32, (1, N), 1) // (N // nsub)

    decay, lam_e, tau_e, gam_e, dt_e = [], 0.0, 0.0, 0.0, 0.0
    for i in range(nh):
        a_col = jnp.sum(jnp.where(lane_a == head0 + i, a_tok, 0.0), axis=1, keepdims=True)
        a_row = jnp.sum(jnp.where(eye, a_col, 0.0), axis=0, keepdims=True)
        cs_col = jnp.sum(jnp.where(cm, a_row, 0.0), axis=1, keepdims=True)
        cs_row = jnp.sum(jnp.where(cmT, a_col, 0.0), axis=0, keepdims=True)
        tot = jnp.sum(a_col, axis=0, keepdims=True)
        decay.append(jnp.where(mask, jnp.exp(jnp.where(mask, cs_col - cs_row, 0.0)), 0.0))
        sel = vhead == i
        lam_e = lam_e + jnp.where(sel, jnp.exp(cs_col), 0.0)
        tau_e = tau_e + jnp.where(sel, jnp.exp(tot - cs_col), 0.0)
        gam_e = gam_e + jnp.where(sel, jnp.exp(tot), 0.0)
        if dt_tok is not None:
            dt_col = jnp.sum(jnp.where(lane_a == head0 + i, dt_tok, 0.0), axis=1, keepdims=True)
            dt_e = dt_e + jnp.where(sel, dt_col, 0.0)
    v = x if dt_tok is None else x * dt_e
    s_shared = bdot(q, k, "nt") if nsub == 1 else None
    y = lam_e * bdot(q, h, "nn")
    for i in range(nh):
        s = s_shared if nsub == 1 else bdot(jnp.where(qhead == i, q, 0.0), k, "nt")
        y = y + jnp.where(vhead == i, bdot(s * decay[i], v, "nn"), 0.0)
    hn = gam_e * h + bdot(k, tau_e * v, "tn")
    if nsub > 1:
        nhead = lax.broadcasted_iota(jnp.int32, (N, W), 0) // (N // nsub)
        hn = jnp.where(nhead == lax.broadcasted_iota(jnp.int32, (N, W), 1) // HEAD_W, hn, 0.0)
    return y, hn


def scan_op(q, k, x, a_tok, dt_tok, *, rev, incl, nsub):
    S = q.shape[0]
    G, _, Hg = a_tok.shape
    N = q.shape[1] // G
    Vw = x.shape[1] // G
    L = CHUNK
    nc = S // L
    use_dt = dt_tok is not None
    chunk = functools.partial(_scan_chunk, rev=rev, incl=incl, nsub=nsub)
    PW = min(Vw, LANES)
    blocks = [(p * PW, (p * PW) // HEAD_W) for p in range(Vw // PW)]

    def order(c, backward):
        return (nc - 1 - c) if (rev != backward) else c

    def specs(backward):
        qs = pl.BlockSpec((L, N), lambda g, c: (order(c, backward), g))
        xs = pl.BlockSpec((L, Vw), lambda g, c: (order(c, backward), g))
        as_ = pl.BlockSpec((1, L, Hg), lambda g, c: (g, order(c, backward), 0))
        hs = pl.BlockSpec((1, 1, N, Vw), lambda g, c: (g, order(c, backward), 0, 0))
        return qs, xs, as_, hs

    def call_fwd(q, k, x, a_tok, dt_tok, y_prev=None):
        qs, xs, as_, hs = specs(False)
        n_in = 4 + use_dt + (y_prev is not None)

        def body(*refs):
            q_ref, k_ref, x_ref, a_ref = refs[:4]
            dt_ref = refs[4] if use_dt else None
            yp_ref = refs[n_in - 1] if y_prev is not None else None
            y_ref, hs_ref, h_scr = refs[n_in:]

            @pl.when(pl.program_id(1) == 0)
            def _():
                h_scr[...] = jnp.zeros_like(h_scr)

            hs_ref[0, 0] = h_scr[...]
            q, k, a, dt = q_ref[...], k_ref[...], a_ref[0], dt_ref[0] if use_dt else None
            for lane0, head0 in blocks:
                cols = slice(lane0, lane0 + PW)
                y, hn = chunk(q, k, x_ref[:, cols], a, dt, h_scr[:, cols], head0=head0)
                y_ref[:, cols] = y if yp_ref is None else y + yp_ref[:, cols]
                h_scr[:, cols] = hn

        ins = [q, k, x, a_tok] + ([dt_tok] if use_dt else []) + ([y_prev] if y_prev is not None else [])
        return pl.pallas_call(
            body,
            out_shape=[jax.ShapeDtypeStruct((S, G * Vw), f32), jax.ShapeDtypeStruct((G, nc, N, Vw), f32)],
            grid=(G, nc),
            in_specs=[qs, qs, xs, as_] + ([as_] if use_dt else []) + ([xs] if y_prev is not None else []),
            out_specs=[xs, hs],
            scratch_shapes=[pltpu.VMEM((N, Vw), f32)],
            compiler_params=_cparams(("parallel", "arbitrary")),
            name="scan_fwd",
        )(*ins)

    def call_bwd(q, k, x, a_tok, dt_tok, hsave, dy, acc=None):
        qs, xs, as_, hs = specs(True)
        n_in = 6 + use_dt + (3 if acc is not None else 0)

        def body(*refs):
            q_ref, k_ref, x_ref, a_ref = refs[:4]
            dt_ref = refs[4] if use_dt else None
            hs_ref, dy_ref = refs[4 + use_dt], refs[5 + use_dt]
            acc_refs = refs[n_in - 3:n_in] if acc is not None else None
            dq_ref, dk_ref, dx_ref, da_ref = refs[n_in:n_in + 4]
            ddt_ref = refs[n_in + 4] if use_dt else None
            dh_scr = refs[-1]

            @pl.when(pl.program_id(1) == 0)
            def _():
                dh_scr[...] = jnp.zeros_like(dh_scr)

            q, k, a = q_ref[...].astype(f32), k_ref[...].astype(f32), a_ref[0]
            dq, dk, da, ddt = 0.0, 0.0, 0.0, 0.0
            if acc is not None:
                dq, dk = acc_refs[0][...].astype(f32), acc_refs[1][...].astype(f32)
            for lane0, head0 in blocks:
                cols = slice(lane0, lane0 + PW)
                if use_dt:
                    f = lambda q, k, x, a, dt, h: chunk(q, k, x, a, dt, h, head0=head0)
                    prim = [q, k, x_ref[:, cols], a, dt_ref[0], hs_ref[0, 0, :, cols]]
                else:
                    f = lambda q, k, x, a, h: chunk(q, k, x, a, None, h, head0=head0)
                    prim = [q, k, x_ref[:, cols], a, hs_ref[0, 0, :, cols]]
                _, vjp = jax.vjp(f, *prim)
                cts = vjp((dy_ref[:, cols], dh_scr[:, cols]))
                dq, dk, da = dq + cts[0], dk + cts[1], da + cts[3]
                if use_dt:
                    ddt = ddt + cts[4]
                dx_ref[:, cols] = cts[2] if acc is None else cts[2] + acc_refs[2][:, cols]
                dh_scr[:, cols] = cts[-1]
            dq_ref[...] = dq.astype(dq_ref.dtype)
            dk_ref[...] = dk.astype(dk_ref.dtype)
            da_ref[0] = da
            if use_dt:
                ddt_ref[0] = ddt

        ins = [q, k, x, a_tok] + ([dt_tok] if use_dt else []) + [hsave, dy] + (list(acc) if acc is not None else [])
        a_shape = jax.ShapeDtypeStruct(a_tok.shape, f32)
        return pl.pallas_call(
            body,
            out_shape=[jax.ShapeDtypeStruct(q.shape, q.dtype), jax.ShapeDtypeStruct(k.shape, k.dtype),
                       jax.ShapeDtypeStruct(x.shape, f32), a_shape] + ([a_shape] if use_dt else []),
            grid=(G, nc),
            in_specs=[qs, qs, xs, as_] + ([as_] if use_dt else []) + [hs, xs] + ([qs, qs, xs] if acc is not None else []),
            out_specs=[qs, qs, xs, as_] + ([as_] if use_dt else []),
            scratch_shapes=[pltpu.VMEM((N, Vw), f32)],
            compiler_params=_cparams(("parallel", "arbitrary")),
            name="scan_bwd",
        )(*ins)

    return call_fwd, call_bwd


def bidir_scan(q, k, x, a_f, a_b, dt_f, dt_b, *, nsub):
    use_dt = dt_f is not None
    fwd_f, bwd_f = scan_op(q, k, x, a_f, dt_f, rev=False, incl=True, nsub=nsub)
    fwd_b, bwd_b = scan_op(q, k, x, a_b, dt_b, rev=True, incl=False, nsub=nsub)

    def run(q, k, x, a_f, a_b, dt_f, dt_b):
        y_f, hs_f = fwd_f(q, k, x, a_f, dt_f)
        y, hs_b = fwd_b(q, k, x, a_b, dt_b, y_prev=y_f)
        return y, (hs_f, hs_b)

    def grads(q, k, x, a_f, a_b, dt_f, dt_b, hs, dy):
        first = bwd_f(q, k, x, a_f, dt_f, hs[0], dy)
        both = bwd_b(q, k, x, a_b, dt_b, hs[1], dy, acc=first[:3])
        return both[0], both[1], both[2], first[3], both[3], (first[4] if use_dt else None), (both[4] if use_dt else None)

    if use_dt:
        @jax.custom_vjp
        def op(q, k, x, a_f, a_b, dt_f, dt_b):
            return run(q, k, x, a_f, a_b, dt_f, dt_b)[0]

        def fwd(q, k, x, a_f, a_b, dt_f, dt_b):
            y, hs = run(q, k, x, a_f, a_b, dt_f, dt_b)
            return y, (q, k, x, a_f, a_b, dt_f, dt_b, hs)

        def bwd(res, dy):
            return grads(*res, dy)

        op.defvjp(fwd, bwd)
        return op(q, k, x, a_f, a_b, dt_f, dt_b)

    @jax.custom_vjp
    def op(q, k, x, a_f, a_b):
        return run(q, k, x, a_f, a_b, None, None)[0]

    def fwd(q, k, x, a_f, a_b):
        y, hs = run(q, k, x, a_f, a_b, None, None)
        return y, (q, k, x, a_f, a_b, hs)

    def bwd(res, dy):
        q, k, x, a_f, a_b, hs = res
        return grads(q, k, x, a_f, a_b, None, None, hs, dy)[:5]

    op.defvjp(fwd, bwd)
    return op(q, k, x, a_f, a_b)


def _swap_halves(x, dh):
    W = x.shape[1]
    lane = lax.broadcasted_iota(jnp.int32, (1, W), 1) % dh
    return jnp.where(lane < dh // 2, pltpu.roll(x, W - dh // 2, 1), pltpu.roll(x, dh // 2, 1))


def rotary(rq, rk, cos_t, sin_t):
    scale = RET_DH ** -0.5

    def fn(rq, rk, c, s):
        return rq * c + _swap_halves(rq, RET_DH) * s, (rk * c + _swap_halves(rk, RET_DH) * s) * scale

    def bwd_fn(rv, pv, dos):
        _, _, c, s = rv
        dq, dk = dos
        dk = dk * scale
        return (dq * c + _swap_halves(dq * s, RET_DH), dk * c + _swap_halves(dk * s, RET_DH)), ()

    return rowwise(fn, "rotary", [rq, rk, cos_t, sin_t], [], [MXU_DTYPE, MXU_DTYPE], n_diff_rows=2, bwd_fn=bwd_fn)


def _rope_tables(S, width):
    half = RET_DH // 2
    inv = 1.0 / (ROPE_BASE ** (jnp.arange(half, dtype=f32) / half))
    ang = jnp.arange(S, dtype=f32)[:, None] * inv[None, :]
    cos, sin = jnp.cos(ang), jnp.sin(ang)
    reps = width // RET_DH
    return jnp.tile(jnp.concatenate([cos, cos], axis=1), (1, reps)), jnp.tile(jnp.concatenate([-sin, sin], axis=1), (1, reps))


def _exact_dot(x, m):
    return jnp.dot(x, m, precision=lax.Precision.HIGHEST, preferred_element_type=f32)


def ret_post(y, rg, gn_g):
    W = y.shape[1]
    idx = np.arange(W) // RET_DH
    avg = jnp.asarray((idx[:, None] == idx[None, :]).astype(np.float32) / RET_DH)

    def fn(y, rg, g, avg):
        mu = _exact_dot(y, avg)
        d = y - mu
        var = _exact_dot(d * d, avg)
        return (_silu(rg) * (d * lax.rsqrt(var + EPS) * g),)

    return rowwise(fn, "ret_post", [y, rg], [gn_g.reshape(1, -1), avg], [MXU_DTYPE], n_diff_params=1)[0]


def _na_bias(rpb, win_r):
    H = rpb.shape[0]
    qc = np.arange(GRID_W)[:, None]
    kc = np.arange(GRID_W)[None, :]
    cstart = np.clip(qc - NA_WIN_C // 2, 0, GRID_W - NA_WIN_C)
    valid = (kc >= cstart) & (kc < cstart + NA_WIN_C)
    dc = np.clip(kc - qc, -(NA_WIN_C - 1), NA_WIN_C - 1) + (NA_WIN_C - 1)
    onehot = (dc[None] == np.arange(2 * NA_WIN_C - 1)[:, None, None]).astype(np.float32)
    t1 = jnp.einsum("hrd,dqk->hrqk", rpb.astype(f32), jnp.asarray(onehot), precision=lax.Precision.HIGHEST)
    per_delta = [t1[:, NA_WIN_R - 1 - d:NA_WIN_R - 1 - d + win_r] for d in range(win_r)]
    b = jnp.stack(per_delta, axis=1)
    b = jnp.where(jnp.asarray(valid)[None, None, None], b, NEG_INF)
    return jnp.transpose(b, (0, 1, 3, 2, 4)).reshape(H, win_r, GRID_W, win_r * GRID_W)


def _na_row(q, kw, vw, biases):
    lane = lax.broadcasted_iota(jnp.int32, (1, q.shape[1]), 1) // NA_DH
    o = 0.0
    for i, b in enumerate(biases):
        qi = jnp.where(lane == i, q, 0.0) * (NA_DH ** -0.5)
        s = bdot(qi, kw, "nt") + b
        e = jnp.exp(s - jnp.max(s, axis=1, keepdims=True))
        p = e / jnp.sum(e, axis=1, keepdims=True)
        o = o + jnp.where(lane == i, bdot(p, vw, "nn"), 0.0)
    return o


def _na_row_bwd(q, kw, vw, biases, do):
    lane = lax.broadcasted_iota(jnp.int32, (1, q.shape[1]), 1) // NA_DH
    scale = NA_DH ** -0.5
    dq, dk, dv, dbs = 0.0, 0.0, 0.0, []
    for i, b in enumerate(biases):
        sel = lane == i
        qi = jnp.where(sel, q, 0.0) * scale
        s = _bdot_raw(qi, kw, "nt") + b
        e = jnp.exp(s - jnp.max(s, axis=1, keepdims=True))
        p = e / jnp.sum(e, axis=1, keepdims=True)
        doi = jnp.where(sel, do, 0.0)
        dp = _bdot_raw(doi, vw, "nt")
        ds = p * (dp - jnp.sum(dp * p, axis=1, keepdims=True))
        dbs.append(ds)
        dq = dq + jnp.where(sel, _bdot_raw(ds, kw, "nn"), 0.0) * scale
        dk = dk + _bdot_raw(ds, qi, "tn")
        dv = dv + _bdot_raw(p, doi, "tn")
    return dq, dk, dv, dbs


def na_op(nq, nk, nv, bias):
    S, W = nq.shape
    rows = S // GRID_W
    win_r = bias.shape[1]
    nkeys = win_r * GRID_W
    hp = LANES // NA_DH
    npair = W // LANES
    RB = min(16, rows)
    nrb = rows // RB
    qspec = pl.BlockSpec((RB * GRID_W, LANES), lambda p, r: (r, p))
    kspec = pl.BlockSpec((S, LANES), lambda p, r: (0, p))
    bspec = pl.BlockSpec((hp, win_r, GRID_W, nkeys), lambda p, r: (p, 0, 0, 0))

    def window(r):
        r0 = jnp.clip(r - win_r // 2, 0, rows - win_r)
        return pl.multiple_of(r0 * GRID_W, GRID_W), r - r0

    def call_fwd(nq, nk, nv, bias):
        def body(q_ref, k_ref, v_ref, b_ref, o_ref):
            rb = pl.program_id(1)

            def row(i, c):
                k0, d = window(rb * RB + i)
                q0 = pl.multiple_of(i * GRID_W, GRID_W)
                o = _na_row(q_ref[pl.ds(q0, GRID_W), :].astype(f32), k_ref[pl.ds(k0, nkeys), :], v_ref[pl.ds(k0, nkeys), :],
                            [b_ref[h, pl.ds(d, 1)][0] for h in range(hp)])
                o_ref[pl.ds(q0, GRID_W), :] = o.astype(o_ref.dtype)
                return c

            lax.fori_loop(0, RB, row, 0, unroll=4)

        return pl.pallas_call(
            body,
            out_shape=jax.ShapeDtypeStruct((S, W), nq.dtype),
            grid=(npair, nrb),
            in_specs=[qspec, kspec, kspec, bspec],
            out_specs=qspec,
            compiler_params=_cparams(("parallel", "arbitrary")),
            name="na_fwd",
        )(nq, nk, nv, bias)

    def call_bwd(nq, nk, nv, bias, do):
        def body(q_ref, k_ref, v_ref, b_ref, do_ref, dq_ref, dk_ref, dv_ref, db_ref, dk_acc, dv_acc):
            rb = pl.program_id(1)

            @pl.when(rb == 0)
            def _():
                dk_acc[...] = jnp.zeros_like(dk_acc)
                dv_acc[...] = jnp.zeros_like(dv_acc)
                db_ref[...] = jnp.zeros_like(db_ref)

            def row(i, c):
                k0, d = window(rb * RB + i)
                q0 = pl.multiple_of(i * GRID_W, GRID_W)
                bs = [b_ref[h, pl.ds(d, 1)][0] for h in range(hp)]
                dq, dk, dv, dbs = _na_row_bwd(q_ref[pl.ds(q0, GRID_W), :].astype(f32), k_ref[pl.ds(k0, nkeys), :],
                                              v_ref[pl.ds(k0, nkeys), :], bs, do_ref[pl.ds(q0, GRID_W), :].astype(f32))
                dq_ref[pl.ds(q0, GRID_W), :] = dq.astype(dq_ref.dtype)
                dk_acc[pl.ds(k0, nkeys), :] += dk
                dv_acc[pl.ds(k0, nkeys), :] += dv
                for h in range(hp):
                    db_ref[h, pl.ds(d, 1)] += dbs[h][None]
                return c

            lax.fori_loop(0, RB, row, 0, unroll=2)

            @pl.when(rb == nrb - 1)
            def _():
                dk_ref[...] = dk_acc[...].astype(dk_ref.dtype)
                dv_ref[...] = dv_acc[...].astype(dv_ref.dtype)

        return pl.pallas_call(
            body,
            out_shape=[jax.ShapeDtypeStruct((S, W), nq.dtype), jax.ShapeDtypeStruct((S, W), nk.dtype),
                       jax.ShapeDtypeStruct((S, W), nv.dtype), jax.ShapeDtypeStruct(bias.shape, f32)],
            grid=(npair, nrb),
            in_specs=[qspec, kspec, kspec, bspec, qspec],
            out_specs=[qspec, kspec, kspec, bspec],
            scratch_shapes=[pltpu.VMEM((S, LANES), f32), pltpu.VMEM((S, LANES), f32)],
            compiler_params=_cparams(("parallel", "arbitrary")),
            name="na_bwd",
        )(nq, nk, nv, bias, do)

    @jax.custom_vjp
    def op(nq, nk, nv, bias):
        return call_fwd(nq, nk, nv, bias)

    def fwd(nq, nk, nv, bias):
        return call_fwd(nq, nk, nv, bias), (nq, nk, nv, bias)

    def bwd(res, do):
        return tuple(call_bwd(*res, do))

    op.defvjp(fwd, bwd)
    return op(nq, nk, nv, bias)


def ssd_dt(dt_raw, dt_bias, a_neg):
    def fn(r, b, a):
        dt = _softplus(r + b)
        return dt, dt * a

    return rowwise(fn, "ssd_dt", [dt_raw], [dt_bias, a_neg], [f32, f32])


def ssd_post(y, xs, z, d_skip_lanes, norm_g, groups):
    def fn(y, xs, z, dsk, g):
        y = (y + xs * dsk) * _silu(z)
        return (y * lax.rsqrt(jnp.mean(y * y, axis=-1, keepdims=True) + EPS) * g,)

    return rowwise(fn, "ssd_post", [y, xs, z], [d_skip_lanes.reshape(1, -1), norm_g.reshape(1, -1)], [MXU_DTYPE],
                   ncol=groups)[0]


def _heads_major(t, groups):
    S = t.shape[0]
    return jnp.transpose(t.reshape(S, groups, -1), (1, 0, 2))


def retention_na_mixer(hn, w_in, decay_logit, gn_g, rpb, w_out, tables):
    S = hn.shape[0]
    R = RET_HEADS * RET_DH
    NW = NA_HEADS * NA_DH
    cols = lambda a, b: w_in[:, a:b]
    rq, rk, rv, rg = (mm(hn, cols(j * R, (j + 1) * R)) for j in range(4))
    nq, nk, nv = (mm(hn, cols(4 * R + j * NW, 4 * R + (j + 1) * NW), out_dtype=MXU_DTYPE) for j in range(3))
    qr, kr = rotary(rq, rk, *tables)
    log_gamma = -_softplus(-decay_logit.astype(f32))
    pairs = R // LANES
    hp = LANES // RET_DH
    a_f = jnp.broadcast_to(log_gamma[0].reshape(pairs, 1, hp), (pairs, S, hp))
    a_b = jnp.broadcast_to(log_gamma[1].reshape(pairs, 1, hp), (pairs, S, hp))
    ret = ret_post(bidir_scan(qr, kr, rv, a_f, a_b, None, None, nsub=hp), rg, gn_g)
    rows = S // GRID_W
    nao = na_op(nq, nk, nv, _na_bias(rpb, min(NA_WIN_R, rows)))
    return mm(ret, w_out[:R]) + mm(nao, w_out[R:])


def ssd_mixer(hn, w_in, conv_w, conv_b, dt_bias, a_log, d_skip, norm_g, w_out):
    heads = d_skip.shape[0]
    inner = heads * SSD_HEADDIM
    gs = SSD_GROUPS * SSD_STATE
    o_x, o_b, o_c, o_dt = inner, 2 * inner, 2 * inner + gs, 2 * inner + 2 * gs
    z = mm(hn, w_in[:, :inner])
    dt_raw = mm(hn, w_in[:, o_dt:])
    xs, bm, cm = (mm_conv_act(hn, [w_in[:, a:b]], [conv_w[:, a - inner:b - inner]], [conv_b[a - inner:b - inner]],
                              _silu, f32, "conv_silu") for a, b in ((o_x, o_b), (o_b, o_c), (o_c, o_dt)))
    a_neg = -jnp.exp(a_log.astype(f32)).reshape(1, -1)
    dt, la = ssd_dt(dt_raw, dt_bias.astype(f32).reshape(1, -1), a_neg)
    dt_f, dt_b = _heads_major(dt[:, :heads], SSD_GROUPS), _heads_major(dt[:, heads:], SSD_GROUPS)
    la_f, la_b = _heads_major(la[:, :heads], SSD_GROUPS), _heads_major(la[:, heads:], SSD_GROUPS)
    y = bidir_scan(cm, bm, xs, la_f, la_b, dt_f, dt_b, nsub=1)
    y = ssd_post(y, xs, z, jnp.repeat(d_skip.astype(f32), SSD_HEADDIM), norm_g, SSD_GROUPS)
    return mm(y, w_out)


def conv_geglu_ffn(hf, w_up, conv_w, conv_b, w_down):
    F = w_down.shape[0]
    a = mm_conv_act(hf, [w_up[:, :F], w_up[:, F:]], [conv_w[:, :F], conv_w[:, F:]], [conv_b[:F], conv_b[F:]],
                    lambda g, v: _gelu_tanh(g) * v, MXU_DTYPE, "conv_geglu")
    return mm(a, w_down)


def model_loss(x, tgt, big, small, rep):
    S = x.shape[0]
    depth = rep["norm_mix_pre"].shape[0]
    tables = _rope_tables(S, RET_HEADS * RET_DH)
    hn = rms(x, rep["norm_mix_pre"][0], MXU_DTYPE)
    for layer in range(depth):
        i = layer // 2
        if layer % 2 == 0:
            m = retention_na_mixer(hn, big["ab_w_in"][i], rep["ab_ret_decay_logit"][i], rep["ab_ret_gn_g"][i],
                                   rep["ab_na_rpb"][i], big["ab_w_out"][i], tables)
        else:
            m = ssd_mixer(hn, big["c_w_in"][i], small["c_conv_w"][i], small["c_conv_b"][i], rep["c_dt_bias"][i],
                          rep["c_a_log"][i], rep["c_d_skip"][i], small["c_norm_g"][i], big["c_w_out"][i])
        x, hf = rms_residual_norm(m, rep["norm_mix_post"][layer], x, rep["norm_ffn_pre"][layer])
        f = conv_geglu_ffn(hf, big["ffn_w_up"][layer], small["ffn_conv_w"][layer], rep["ffn_conv_b"][layer],
                           big["ffn_w_down"][layer])
        if layer + 1 < depth:
            x, hn = rms_residual_norm(f, rep["norm_ffn_post"][layer], x, rep["norm_mix_pre"][layer + 1])
        else:
            x = rms_residual(f, rep["norm_ffn_post"][layer], x)
    return loss_op(x, tgt)


def _mesh_pos():
    return lax.axis_index("x"), lax.axis_index("y"), lax.axis_index("c")


def gather_chips(local):
    R, Wd = local.shape

    half = R // 2
    CH = COPY_CHUNKS
    q = half // CH

    def body(x_ref, out_ref, send_sems, recv_sems):
        x, y, c = _mesh_pos()
        my = 2 * x + y
        chips = [(1 - x, y), (x, 1 - y), (1 - x, 1 - y)]

        def piece(ref, h, j):
            return ref.at[pl.ds(pl.multiple_of(h * half + j * q, PACK_ALIGN), q), :]

        def copy(k, src, chip, h, j, to):
            return pltpu.make_async_remote_copy(src_ref=src, dst_ref=piece(out_ref.at[chip], h, j), send_sem=send_sems.at[k],
                                                recv_sem=recv_sems.at[k], device_id=to, device_id_type=pl.DeviceIdType.MESH)

        first = [[copy(k * CH + j, piece(x_ref, c, j), my, c, j, (cx, cy, c)) for j in range(CH)]
                 for k, (cx, cy) in enumerate(chips)]
        for j in range(CH):
            for k in range(3):
                first[k][j].start()
        passed = [[copy((3 + k) * CH + j, piece(out_ref.at[2 * cx + cy], c, j), 2 * cx + cy, c, j, (x, y, 1 - c))
                   for j in range(CH)] for k, (cx, cy) in enumerate(chips)]
        for j in range(CH):
            for k, (cx, cy) in enumerate(chips):
                copy(k * CH + j, piece(x_ref, c, j), 2 * cx + cy, c, j, (cx, cy, c)).wait_recv()
                passed[k][j].start()
        for j in range(CH):
            for k, (cx, cy) in enumerate(chips):
                copy((3 + k) * CH + j, piece(x_ref, c, j), 2 * cx + cy, 1 - c, j, (x, y, 1 - c)).wait_recv()
        for k in range(3):
            for cp in first[k] + passed[k]:
                cp.wait_send()

    return pl.pallas_call(
        body,
        out_shape=jax.ShapeDtypeStruct((N_CHIPS, R, Wd), local.dtype),
        in_specs=[pl.BlockSpec(memory_space=pl.ANY)],
        out_specs=pl.BlockSpec(memory_space=pl.ANY),
        scratch_shapes=[pltpu.SemaphoreType.DMA((6 * CH,)), pltpu.SemaphoreType.DMA((6 * CH,))],
        name="gather_chips",
    )(local)


def pair_swap(parts):
    n, R, Wd = parts.shape
    half = R // 2

    CH = COPY_CHUNKS
    q = half // CH

    def body(p_ref, got_ref, send_sems, recv_sems):
        x, y, c = _mesh_pos()

        def src(s, j):
            return p_ref.at[s, pl.ds(pl.multiple_of((1 - c) * half + j * q, PACK_ALIGN), q), :]

        swap = [pltpu.make_async_remote_copy(src_ref=src(s, j), dst_ref=got_ref.at[s, pl.ds(j * q, q), :],
                                             send_sem=send_sems.at[s * CH + j], recv_sem=recv_sems.at[s * CH + j],
                                             device_id=(x, y, 1 - c), device_id_type=pl.DeviceIdType.MESH)
                for s in range(n) for j in range(CH)]
        for cp in swap:
            cp.start()
        for cp in swap:
            cp.wait()

    return pl.pallas_call(
        body,
        out_shape=jax.ShapeDtypeStruct((n, half, Wd), parts.dtype),
        in_specs=[pl.BlockSpec(memory_space=pl.ANY)],
        out_specs=pl.BlockSpec(memory_space=pl.ANY),
        scratch_shapes=[pltpu.SemaphoreType.DMA((n * CH,)), pltpu.SemaphoreType.DMA((n * CH,))],
        name="pair_swap",
    )(parts)


def chip_exchange(parts):
    n, R, Wd = parts.shape

    def body(p_ref, out_ref, send_sems, recv_sems):
        x, y, c = _mesh_pos()
        my = 2 * x + y
        chips = [(1 - x, y), (x, 1 - y), (1 - x, 1 - y)]

        def copy(k, src_slot, dst_slot, to):
            return pltpu.make_async_remote_copy(src_ref=p_ref.at[src_slot], dst_ref=out_ref.at[dst_slot], send_sem=send_sems.at[k],
                                                recv_sem=recv_sems.at[k], device_id=to, device_id_type=pl.DeviceIdType.MESH)

        sends = [copy(k, 2 * cx + cy, my, (cx, cy, c)) for k, (cx, cy) in enumerate(chips)]
        for cp in sends:
            cp.start()
        for k, (cx, cy) in enumerate(chips):
            copy(k, my, 2 * cx + cy, (cx, cy, c)).wait_recv()
        for cp in sends:
            cp.wait_send()

    return pl.pallas_call(
        body,
        out_shape=jax.ShapeDtypeStruct((n, R, Wd), parts.dtype),
        in_specs=[pl.BlockSpec(memory_space=pl.ANY)],
        out_specs=pl.BlockSpec(memory_space=pl.ANY),
        scratch_shapes=[pltpu.SemaphoreType.DMA((3,)), pltpu.SemaphoreType.DMA((3,))],
        name="chip_exchange",
    )(parts)


def pair_share(mine):
    R, Wd = mine.shape

    CH = 2 * COPY_CHUNKS
    q = R // CH

    def body(m_ref, out_ref, send_sems, recv_sems):
        x, y, c = _mesh_pos()
        swap = [pltpu.make_async_remote_copy(src_ref=m_ref.at[pl.ds(j * q, q), :], dst_ref=out_ref.at[pl.ds(j * q, q), :],
                                             send_sem=send_sems.at[j], recv_sem=recv_sems.at[j], device_id=(x, y, 1 - c),
                                             device_id_type=pl.DeviceIdType.MESH) for j in range(CH)]
        for cp in swap:
            cp.start()
        for cp in swap:
            cp.wait()

    return pl.pallas_call(
        body,
        out_shape=jax.ShapeDtypeStruct((R, Wd), mine.dtype),
        in_specs=[pl.BlockSpec(memory_space=pl.ANY)],
        out_specs=pl.BlockSpec(memory_space=pl.ANY),
        scratch_shapes=[pltpu.SemaphoreType.DMA((CH,)), pltpu.SemaphoreType.DMA((CH,))],
        name="pair_share",
    )(mine)


def sum_chips(recv, own):
    n, R, Wd = recv.shape
    tr = _pick(R, (512, 256, 128, 64, 32, 16, 8))

    def body(r_ref, p_ref, o_ref):
        my = 2 * lax.axis_index("x") + lax.axis_index("y")
        acc = jnp.zeros((tr, Wd), f32)
        for s in range(n):
            acc = acc + jnp.where(my == s, p_ref[s], r_ref[s]).astype(f32)
        o_ref[...] = acc

    spec = pl.BlockSpec((n, tr, Wd), lambda i: (0, i, 0))
    return pl.pallas_call(
        body,
        out_shape=jax.ShapeDtypeStruct((R, Wd), f32),
        grid=(R // tr,),
        in_specs=[spec, spec],
        out_specs=pl.BlockSpec((tr, Wd), lambda i: (i, 0)),
        compiler_params=_cparams(("parallel",)),
        name="sum_chips",
    )(recv, own)


def add_pair(parts, got):
    n, R, Wd = parts.shape
    half = R // 2
    tr = _pick(half, (512, 256, 128, 64, 32, 16, 8))
    nb = half // tr

    def body(lo_ref, hi_ref, g_ref, o_ref):
        mine = jnp.where(lax.axis_index("c") == 0, lo_ref[...], hi_ref[...])
        o_ref[...] = (mine.astype(f32) + g_ref[...].astype(f32)).astype(o_ref.dtype)

    spec = pl.BlockSpec((1, tr, Wd), lambda s, i: (s, i, 0))
    return pl.pallas_call(
        body,
        out_shape=jax.ShapeDtypeStruct(got.shape, parts.dtype),
        grid=(n, nb),
        in_specs=[spec, pl.BlockSpec((1, tr, Wd), lambda s, i: (s, nb + i, 0)), spec],
        out_specs=spec,
        compiler_params=_cparams(("parallel", "parallel")),
        name="add_pair",
    )(parts, parts, got)


def reduce_scatter(parts):
    chip_sum = add_pair(parts, pair_swap(parts))
    mine = sum_chips(chip_exchange(chip_sum), chip_sum)
    theirs = pair_share(mine)
    first = lax.axis_index("c") == 0
    return jnp.concatenate([jnp.where(first, mine, theirs), jnp.where(first, theirs, mine)], axis=0)


def adamw(w, g, m, v):
    R, C = w.shape
    tr = R
    for cand in (512, 256, 128, 64, 32, 16, 8):
        if R * C * 4 > (1 << 20) and R % cand == 0 and cand * C * 4 <= (1 << 20):
            tr = cand
            break

    def body(w_ref, g_ref, m_ref, v_ref, d_ref, mo_ref, vo_ref):
        g = g_ref[...]
        m = ADAM_B1 * m_ref[...] + (1.0 - ADAM_B1) * g
        v = ADAM_B2 * v_ref[...] + (1.0 - ADAM_B2) * (g * g)
        m_hat = m / (1.0 - ADAM_B1 ** ADAM_STEP)
        v_hat = v / (1.0 - ADAM_B2 ** ADAM_STEP)
        d_ref[...] = -ADAM_LR * (m_hat / (jnp.sqrt(v_hat) + ADAM_EPS) + ADAM_WD * w_ref[...])
        mo_ref[...] = m
        vo_ref[...] = v

    spec = pl.BlockSpec((tr, C), lambda i: (i, 0))
    return pl.pallas_call(
        body,
        out_shape=[jax.ShapeDtypeStruct((R, C), f32)] * 3,
        grid=(R // tr,),
        in_specs=[spec] * 4,
        out_specs=[spec] * 3,
        compiler_params=_cparams(("parallel",)),
        name="adamw",
    )(w, g, m, v)


def _pack(arrs, dtype):
    flat = jnp.concatenate([a.astype(dtype).reshape(-1) for a in arrs])
    n = flat.shape[0]
    unit = PACK_W * PACK_ROWS
    padded = -(-n // unit) * unit
    return jnp.pad(flat, (0, padded - n)).reshape(-1, PACK_W)


def _unpack(buf, shapes):
    flat = buf.reshape(-1)
    out, off = [], 0
    for s in shapes:
        n = int(np.prod(s))
        out.append(flat[off:off + n].reshape(s))
        off += n
    return out


BIG = (("ab_w_in", 2), ("ab_w_out", 1), ("c_w_in", 2), ("c_w_out", 1), ("ffn_w_up", 2), ("ffn_w_down", 1))
SMALL = (("c_conv_w", 2), ("c_conv_b", 1), ("c_norm_g", 1), ("ffn_conv_w", 2))
REP = ("norm_mix_pre", "norm_mix_post", "norm_ffn_pre", "norm_ffn_post", "ab_ret_decay_logit", "ab_ret_gn_g", "ab_na_rpb",
       "c_dt_bias", "c_a_log", "c_d_skip", "ffn_conv_b")
WEIGHTS = ("norm_mix_pre", "norm_mix_post", "norm_ffn_pre", "norm_ffn_post", "ab_w_in", "ab_ret_decay_logit", "ab_ret_gn_g",
           "ab_na_rpb", "ab_w_out", "c_w_in", "c_conv_w", "c_conv_b", "c_dt_bias", "c_a_log", "c_d_skip", "c_norm_g", "c_w_out",
           "ffn_w_up", "ffn_conv_w", "ffn_conv_b", "ffn_w_down")


def _gather_set(local, spec, dtype):
    shapes = [local[n].shape for n, _ in spec]
    got = gather_chips(_pack([local[n] for n, _ in spec], dtype))
    my = 2 * lax.axis_index("x") + lax.axis_index("y")
    per_chip = [_unpack(got[s], shapes) for s in range(N_CHIPS)]
    return {n: jnp.concatenate([jnp.where(my == s, local[n].astype(dtype), per_chip[s][j]) for s in range(N_CHIPS)], axis=ax)
            for j, (n, ax) in enumerate(spec)}


def _scatter_parts(full, spec, extra, dtype):
    split = {n: jnp.split(full[n], N_CHIPS, axis=ax) for n, ax in spec}
    return jnp.stack([_pack([split[n][s] for n, _ in spec] + list(extra), dtype) for s in range(N_CHIPS)])


def kernel(x, norm_mix_pre, norm_mix_post, norm_ffn_pre, norm_ffn_post, ab_w_in, ab_ret_decay_logit, ab_ret_gn_g, ab_na_rpb, ab_w_out, c_w_in, c_conv_w, c_conv_b, c_dt_bias, c_a_log, c_d_skip, c_norm_g, c_w_out, ffn_w_up, ffn_conv_w, ffn_conv_b, ffn_w_down, loss_target, m_norm_mix_pre, m_norm_mix_post, m_norm_ffn_pre, m_norm_ffn_post, m_ab_w_in, m_ab_ret_decay_logit, m_ab_ret_gn_g, m_ab_na_rpb, m_ab_w_out, m_c_w_in, m_c_conv_w, m_c_conv_b, m_c_dt_bias, m_c_a_log, m_c_d_skip, m_c_norm_g, m_c_w_out, m_ffn_w_up, m_ffn_conv_w, m_ffn_conv_b, m_ffn_w_down, v_norm_mix_pre, v_norm_mix_post, v_norm_ffn_pre, v_norm_ffn_post, v_ab_w_in, v_ab_ret_decay_logit, v_ab_ret_gn_g, v_ab_na_rpb, v_ab_w_out, v_c_w_in, v_c_conv_w, v_c_conv_b, v_c_dt_bias, v_c_a_log, v_c_d_skip, v_c_norm_g, v_c_w_out, v_ffn_w_up, v_ffn_conv_w, v_ffn_conv_b, v_ffn_w_down):
    args = dict(locals())
    w = {n: args[n] for n in WEIGHTS}
    mom = {n: args["m_" + n] for n in WEIGHTS}
    var = {n: args["v_" + n] for n in WEIGHTS}

    big = _gather_set(w, BIG, MXU_DTYPE)
    small = _gather_set(w, SMALL, f32)
    rep = {n: w[n] for n in REP}

    def loss_fn(xs, big, small, rep):
        return model_loss(xs, loss_target[0], big, small, rep)

    loss, (gx, gbig, gsmall, grep) = jax.value_and_grad(loss_fn, argnums=(0, 1, 2, 3))(x[0], big, small, rep)
    loss = lax.psum(loss, ("x", "y", "c"))

    big_shapes = [w[n].shape for n, _ in BIG]
    small_shapes = [w[n].shape for n, _ in SMALL] + [w[n].shape for n in REP]
    g_big = _unpack(reduce_scatter(_scatter_parts(gbig, BIG, (), MXU_DTYPE)), big_shapes)
    g_small_buf = reduce_scatter(_scatter_parts(gsmall, SMALL, [grep[n] for n in REP], f32))
    grads = dict(zip([n for n, _ in BIG], g_big))
    small_names = [n for n, _ in SMALL] + list(REP)
    grads.update(zip(small_names, _unpack(g_small_buf, small_shapes)))

    delta, new_m, new_v = {}, {}, {}
    for n in WEIGHTS:
        shp = w[n].shape
        two_d = lambda a: a.reshape(-1, shp[-1])
        d, m2, v2 = adamw(two_d(w[n]), two_d(grads[n]), two_d(mom[n]), two_d(var[n]))
        delta[n], new_m[n], new_v[n] = d.reshape(shp), m2.reshape(shp), v2.reshape(shp)

    return (loss, gx[None], *[grads[n] for n in WEIGHTS], *[delta[n] for n in WEIGHTS],
            *[new_m[n] for n in WEIGHTS], *[new_v[n] for n in WEIGHTS])
```

```python
import functools
import math

import numpy as np
import jax
import jax.numpy as jnp
from jax import lax
from jax.experimental import pallas as pl
from jax.experimental.pallas import tpu as pltpu

f32 = jnp.float32
bf16 = jnp.bfloat16
MXU_DTYPE = bf16

GRID_W = 64
CHUNK = 128
EPS = 1e-6
RET_HEADS = 8
RET_DH = 64
ROPE_BASE = 10000.0
NA_HEADS = 8
NA_DH = 64
NA_WIN_R = 8
NA_WIN_C = 16
SSD_HEADDIM = 64
SSD_GROUPS = 4
SSD_STATE = 128
ADAM_LR = 0.001
ADAM_B1 = 0.9
ADAM_B2 = 0.999
ADAM_EPS = 1e-08
ADAM_WD = 0.01
ADAM_STEP = 10

LANES = 128
HEAD_W = 64
PACK_W = 512
PACK_ROWS = 1024
PACK_ALIGN = 16
COPY_CHUNKS = 4
VMEM_LIMIT = 56 * 1024 * 1024
MM_BLOCK_BYTES = 6 * 1024 * 1024
N_CHIPS = 4
N_DEV = 8
NEG_INF = -1e30

_DIMS = {"nn": (((1,), (0,)), ((), ())), "nt": (((1,), (1,)), ((), ())), "tn": (((0,), (0,)), ((), ()))}


def _cparams(sem=None):
    return pltpu.CompilerParams(dimension_semantics=sem, vmem_limit_bytes=VMEM_LIMIT)


def _pick(dim, cands):
    for c in cands:
        if dim % c == 0:
            return c
    return dim


def _divisor_tile(dim, fits, align):
    for d in range(1, dim + 1):
        t = dim // d
        if dim % d == 0 and t % align == 0 and fits(t):
            return t
    return dim


def _bdot_raw(a, b, mode):
    return lax.dot_general(a.astype(MXU_DTYPE), b.astype(MXU_DTYPE), _DIMS[mode], preferred_element_type=f32)


@functools.partial(jax.custom_vjp, nondiff_argnums=(2,))
def bdot(a, b, mode):
    return _bdot_raw(a, b, mode)


def _bdot_fwd(a, b, mode):
    return _bdot_raw(a, b, mode), (a, b)


def _bdot_bwd(mode, res, g):
    a, b = res
    if mode == "nn":
        da, db = _bdot_raw(g, b, "nt"), _bdot_raw(a, g, "tn")
    elif mode == "nt":
        da, db = _bdot_raw(g, b, "nn"), _bdot_raw(g, a, "tn")
    else:
        da, db = _bdot_raw(b, g, "nt"), _bdot_raw(a, g, "nn")
    return da.astype(a.dtype), db.astype(b.dtype)


bdot.defvjp(_bdot_fwd, _bdot_bwd)


def _mm_call(a, b, mode, out_dtype):
    if mode == "nn":
        (M, K), (K2, N) = a.shape, b.shape
    elif mode == "nt":
        (M, K), (N, K2) = a.shape, b.shape
    else:
        (K, M), (K2, N) = a.shape, b.shape
    assert K == K2, (a.shape, b.shape, mode)
    a_bytes, b_bytes, o_bytes = a.dtype.itemsize, b.dtype.itemsize, jnp.dtype(out_dtype).itemsize
    if mode == "tn":
        tk = _pick(K, (512, 256, 128))
        tn = _divisor_tile(N, lambda t: t <= 1536, LANES)
        tm = _divisor_tile(M, lambda t: t * tn * 4 <= MM_BLOCK_BYTES, 8)
    else:
        tk, tn = K, N
        tm = _divisor_tile(M, lambda t: t * K * a_bytes <= MM_BLOCK_BYTES and t * N * o_bytes <= MM_BLOCK_BYTES, 8)
    nk = K // tk
    if mode == "nn":
        a_spec = pl.BlockSpec((tm, tk), lambda i, j, k: (i, k))
        b_spec = pl.BlockSpec((tk, tn), lambda i, j, k: (k, j))
    elif mode == "nt":
        a_spec = pl.BlockSpec((tm, tk), lambda i, j, k: (i, k))
        b_spec = pl.BlockSpec((tn, tk), lambda i, j, k: (j, k))
    else:
        a_spec = pl.BlockSpec((tk, tm), lambda i, j, k: (k, i))
        b_spec = pl.BlockSpec((tk, tn), lambda i, j, k: (k, j))

    if nk == 1:
        def body(a_ref, b_ref, o_ref):
            o_ref[...] = _bdot_raw(a_ref[...], b_ref[...], mode).astype(o_ref.dtype)
    else:
        def body(a_ref, b_ref, o_ref, acc_ref):
            k = pl.program_id(2)

            @pl.when(k == 0)
            def _():
                acc_ref[...] = jnp.zeros_like(acc_ref)

            acc_ref[...] += _bdot_raw(a_ref[...], b_ref[...], mode)

            @pl.when(k == nk - 1)
            def _():
                o_ref[...] = acc_ref[...].astype(o_ref.dtype)

    return pl.pallas_call(
        body,
        out_shape=jax.ShapeDtypeStruct((M, N), out_dtype),
        grid=(M // tm, N // tn, nk),
        in_specs=[a_spec, b_spec],
        out_specs=pl.BlockSpec((tm, tn), lambda i, j, k: (i, j)),
        scratch_shapes=[pltpu.VMEM((tm, tn), f32)] if nk > 1 else [],
        compiler_params=_cparams(("parallel", "parallel", "arbitrary")),
        name="mm_" + mode,
    )(a, b)


def mm(a, b, mode="nn", out_dtype=f32):
    @jax.custom_vjp
    def op(a, b):
        return _mm_call(a, b, mode, out_dtype)

    def fwd(a, b):
        return _mm_call(a, b, mode, out_dtype), (a, b)

    def bwd(res, g):
        a, b = res
        if mode == "nn":
            return _mm_call(g, b, "nt", a.dtype), _mm_call(a, g, "tn", b.dtype)
        if mode == "nt":
            return _mm_call(g, b, "nn", a.dtype), _mm_call(g, a, "tn", b.dtype)
        return _mm_call(b, g, "nt", a.dtype), _mm_call(a, g, "nn", b.dtype)

    op.defvjp(fwd, bwd)
    return op(a, b)


def _row_tile(S, row_bytes):
    tm = 512
    while tm > 8 and (tm * row_bytes > (6 << 20) or S % tm):
        tm //= 2
    return tm


def rowwise(fn, name, rows, params, out_dtypes, n_diff_rows=None, n_diff_params=None, ncol=1, bwd_fn=None):
    rows, params = list(rows), list(params)
    nr, npar = len(rows), len(params)
    ndr = nr if n_diff_rows is None else n_diff_rows
    ndp = npar if n_diff_params is None else n_diff_params
    S = rows[0].shape[0]
    rw = [r.shape[1] // ncol for r in rows]
    pshape = [(p.shape[0], p.shape[1] // ncol) for p in params]

    def block_structs(tm):
        return ([jax.ShapeDtypeStruct((tm, w), f32) for w in rw] + [jax.ShapeDtypeStruct(s, f32) for s in pshape])

    outs_s = jax.eval_shape(fn, *block_structs(8))
    ow = [o.shape[1] for o in outs_s]
    nout = len(ow)
    row_bytes = 4 * (sum(rw) * 2 + sum(ow) * 2)
    tm = _row_tile(S, row_bytes)
    grid = (ncol, S // tm)

    def rspec(w):
        return pl.BlockSpec((tm, w), lambda g, i: (i, g))

    def pspec(s):
        return pl.BlockSpec(s, lambda g, i: (0, g))

    def call_fwd(*args):
        def body(*refs):
            vals = [r[...].astype(f32) for r in refs[:nr + npar]]
            res = fn(*vals)
            for o, r in zip(refs[nr + npar:], res):
                o[...] = r.astype(o.dtype)

        return pl.pallas_call(
            body,
            out_shape=[jax.ShapeDtypeStruct((S, w * ncol), dt) for w, dt in zip(ow, out_dtypes)],
            grid=grid,
            in_specs=[rspec(w) for w in rw] + [pspec(s) for s in pshape],
            out_specs=[rspec(w) for w in ow],
            compiler_params=_cparams(("parallel", "parallel")),
            name=name + "_fwd",
        )(*args)

    def call_bwd(args, douts):
        def body(*refs):
            in_refs = refs[:nr + npar]
            do_refs = refs[nr + npar:nr + npar + nout]
            dr_refs = refs[nr + npar + nout:nr + npar + nout + ndr]
            dp_refs = refs[nr + npar + nout + ndr:]
            rv = [r[...] for r in in_refs[:nr]]
            pv = [r[...] for r in in_refs[nr:]]
            dos = [d[...].astype(f32) for d in do_refs]
            if bwd_fn is not None:
                drs, dps = bwd_fn(rv, pv, dos)
            else:
                def f(*a):
                    return fn(*a[:ndr], *rv[ndr:], *a[ndr:], *pv[ndp:])

                _, vjp = jax.vjp(f, *[v.astype(f32) for v in rv[:ndr]], *pv[:ndp])
                cts = vjp(tuple(dos))
                drs, dps = cts[:ndr], cts[ndr:]
            for r, ct in zip(dr_refs, drs):
                r[...] = ct.astype(r.dtype)
            if ndp:
                @pl.when(pl.program_id(1) == 0)
                def _():
                    for r in dp_refs:
                        r[...] = jnp.zeros_like(r)

                for r, ct in zip(dp_refs, dps):
                    r[...] += ct

        return pl.pallas_call(
            body,
            out_shape=[jax.ShapeDtypeStruct(r.shape, r.dtype) for r in rows[:ndr]]
            + [jax.ShapeDtypeStruct(p.shape, f32) for p in params[:ndp]],
            grid=grid,
            in_specs=[rspec(w) for w in rw] + [pspec(s) for s in pshape] + [rspec(w) for w in ow],
            out_specs=[rspec(w) for w in rw[:ndr]] + [pspec(s) for s in pshape[:ndp]],
            compiler_params=_cparams(("parallel", "arbitrary")),
            name=name + "_bwd",
        )(*args, *douts)

    @jax.custom_vjp
    def op(*args):
        return tuple(call_fwd(*args))

    def fwd(*args):
        return tuple(call_fwd(*args)), args

    def bwd(args, douts):
        res = call_bwd(args, douts)
        drs, dps = res[:ndr], res[ndr:]
        out = list(drs) + [jnp.zeros_like(a) for a in args[ndr:nr]]
        out += [dp.astype(p.dtype) for dp, p in zip(dps, args[nr:nr + ndp])]
        out += [jnp.zeros_like(a) for a in args[nr + ndp:]]
        return tuple(out)

    op.defvjp(fwd, bwd)
    return op(*rows, *params)


def _silu(x):
    return x * (1.0 / (1.0 + jnp.exp(-x)))


def _softplus(x):
    return jnp.maximum(x, 0.0) + jnp.log(1.0 + jnp.exp(-jnp.abs(x)))


def _gelu_tanh(x):
    return 0.5 * x * (1.0 + jnp.tanh(math.sqrt(2.0 / math.pi) * (x + 0.044715 * (x * x * x))))


def _rms_fn(x, g):
    return x * lax.rsqrt(jnp.mean(x * x, axis=-1, keepdims=True) + EPS) * g


def rms(x, g, out_dtype):
    return rowwise(lambda x, g: (_rms_fn(x, g),), "rms", [x], [g.reshape(1, -1)], [out_dtype])[0]


def rms_residual_norm(m, g, x, g_next):
    def fn(m, x, g, gn):
        xn = x + _rms_fn(m, g)
        return xn, _rms_fn(xn, gn)

    return rowwise(fn, "rms_res_norm", [m, x], [g.reshape(1, -1), g_next.reshape(1, -1)], [f32, MXU_DTYPE])


def rms_residual(m, g, x):
    return rowwise(lambda m, x, g: (x + _rms_fn(m, g),), "rms_res", [m, x], [g.reshape(1, -1)], [f32])[0]


def loss_op(y, tgt):
    S, D = y.shape
    tm = _row_tile(S, 4 * D * 4)

    def call_fwd(y, tgt):
        def body(y_ref, t_ref, o_ref):
            @pl.when(pl.program_id(0) == 0)
            def _():
                o_ref[...] = jnp.zeros_like(o_ref)

            e = y_ref[...] - t_ref[...]
            o_ref[...] += 0.5 * jnp.sum(jnp.mean(e * e, axis=-1, keepdims=True))

        out = pl.pallas_call(
            body,
            out_shape=jax.ShapeDtypeStruct((8, LANES), f32),
            grid=(S // tm,),
            in_specs=[pl.BlockSpec((tm, D), lambda i: (i, 0))] * 2,
            out_specs=pl.BlockSpec((8, LANES), lambda i: (0, 0)),
            compiler_params=_cparams(("arbitrary",)),
            name="loss_fwd",
        )(y, tgt)
        return out[0, 0]

    def call_bwd(y, tgt, g):
        def body(y_ref, t_ref, g_ref, o_ref):
            o_ref[...] = (y_ref[...] - t_ref[...]) * (g_ref[...] * (1.0 / D))

        return pl.pallas_call(
            body,
            out_shape=jax.ShapeDtypeStruct((S, D), f32),
            grid=(S // tm,),
            in_specs=[pl.BlockSpec((tm, D), lambda i: (i, 0))] * 2 + [pl.BlockSpec((1, 1), lambda i: (0, 0))],
            out_specs=pl.BlockSpec((tm, D), lambda i: (i, 0)),
            compiler_params=_cparams(("parallel",)),
            name="loss_bwd",
        )(y, tgt, g.reshape(1, 1).astype(f32))

    @jax.custom_vjp
    def op(y, tgt):
        return call_fwd(y, tgt)

    def fwd(y, tgt):
        return call_fwd(y, tgt), (y, tgt)

    def bwd(res, g):
        y, tgt = res
        return call_bwd(y, tgt, g), jnp.zeros_like(tgt)

    op.defvjp(fwd, bwd)
    return op(y, tgt)


HALO = 8


def _conv_tile(S, R):
    def ext(ref, r0):
        cur = ref[pl.ds(r0, R), :]
        prev = ref[pl.ds(pl.multiple_of(jnp.maximum(r0 - HALO, 0), HALO), HALO), :]
        nxt = ref[pl.ds(pl.multiple_of(jnp.minimum(r0 + R, S - HALO), HALO), HALO), :]
        prev = jnp.where(r0 > 0, prev, 0.0)
        nxt = jnp.where(r0 + R < S, nxt, 0.0)
        return jnp.concatenate([prev, cur, nxt], axis=0)

    return ext


def _shift_rows(e, k, R):
    n = e.shape[0]
    if k == 0:
        return e[HALO:HALO + R]
    return pltpu.roll(e, (-k) % n, 0)[HALO:HALO + R]


def mm_conv_act(h, ws, cws, cbs, act, out_dtype, name):
    n = len(ws)
    S = h.shape[0]
    C = ws[0].shape[1]
    W = cws[0].shape[0]
    pad = W // 2
    bw = _pick(C, (LANES,))
    R = _pick(S, (256, 128, 64, 32, 16, 8))
    nt = S // R
    ext = _conv_tile(S, R)
    col = lambda rows: pl.BlockSpec((rows, bw), lambda j: (0, j))

    def conv(e, wv, bv):
        acc = bv + wv[pad] * e[HALO:HALO + R]
        for j in range(W):
            if j != pad:
                acc = acc + wv[j] * _shift_rows(e, j - pad, R)
        return acc

    def call_fwd(xs, cws, cbs):
        def body(*refs):
            x_refs, w_refs, b_refs, y_ref = refs[:n], refs[n:2 * n], refs[2 * n:3 * n], refs[3 * n]
            wvs = [[w[j:j + 1, :] for j in range(W)] for w in w_refs]
            bvs = [b[...] for b in b_refs]

            def tile(i, c):
                r0 = pl.multiple_of(i * R, R)
                us = [conv(ext(x, r0), wv, bv) for x, wv, bv in zip(x_refs, wvs, bvs)]
                y_ref[pl.ds(r0, R), :] = act(*us).astype(y_ref.dtype)
                return c

            lax.fori_loop(0, nt, tile, 0)

        return pl.pallas_call(
            body,
            out_shape=jax.ShapeDtypeStruct((S, C), out_dtype),
            grid=(C // bw,),
            in_specs=[col(S)] * n + [col(W)] * n + [col(1)] * n,
            out_specs=col(S),
            compiler_params=_cparams(("parallel",)),
            name=name + "_fwd",
        )(*xs, *cws, *cbs)

    def call_bwd(xs, cws, cbs, dy):
        def body(*refs):
            x_refs, w_refs, b_refs, dy_ref = refs[:n], refs[n:2 * n], refs[2 * n:3 * n], refs[3 * n]
            dx_refs, dw_refs, db_refs = refs[3 * n + 1:4 * n + 1], refs[4 * n + 1:5 * n + 1], refs[5 * n + 1:6 * n + 1]
            du_scr = refs[6 * n + 1:]
            wvs = [[w[j:j + 1, :] for j in range(W)] for w in w_refs]
            bvs = [b[...] for b in b_refs]
            zero = jnp.zeros((1, bw), f32)

            def first(i, dbs):
                r0 = pl.multiple_of(i * R, R)
                us = [conv(ext(x, r0), wv, bv) for x, wv, bv in zip(x_refs, wvs, bvs)]
                _, vjp = jax.vjp(act, *us)
                dus = vjp(dy_ref[pl.ds(r0, R), :].astype(f32))
                for scr, du in zip(du_scr, dus):
                    scr[pl.ds(r0, R), :] = du
                return tuple(db + jnp.sum(du, axis=0, keepdims=True) for db, du in zip(dbs, dus))

            dbs = lax.fori_loop(0, nt, first, tuple(zero for _ in range(n)))

            def second(i, dws):
                r0 = pl.multiple_of(i * R, R)
                new = []
                for x, scr, dx, wv, dw in zip(x_refs, du_scr, dx_refs, wvs, dws):
                    ex, ed = ext(x, r0), ext(scr, r0)
                    d0 = ed[HALO:HALO + R]
                    acc = jnp.zeros((R, bw), f32)
                    row = []
                    for j in range(W):
                        acc = acc + wv[j] * _shift_rows(ed, pad - j, R)
                        row.append(dw[j] + jnp.sum(d0 * _shift_rows(ex, j - pad, R), axis=0, keepdims=True))
                    dx[pl.ds(r0, R), :] = acc.astype(dx.dtype)
                    new.append(tuple(row))
                return tuple(new)

            dws = lax.fori_loop(0, nt, second, tuple(tuple(zero for _ in range(W)) for _ in range(n)))
            for dw_ref, db_ref, dw, db in zip(dw_refs, db_refs, dws, dbs):
                dw_ref[...] = jnp.zeros_like(dw_ref)
                for j in range(W):
                    dw_ref[j:j + 1, :] = dw[j]
                db_ref[...] = db

        return pl.pallas_call(
            body,
            out_shape=[jax.ShapeDtypeStruct((S, C), MXU_DTYPE)] * n + [jax.ShapeDtypeStruct((8, C), f32)] * n
            + [jax.ShapeDtypeStruct((1, C), f32)] * n,
            grid=(C // bw,),
            in_specs=[col(S)] * n + [col(W)] * n + [col(1)] * n + [col(S)],
            out_specs=[col(S)] * n + [col(8)] * n + [col(1)] * n,
            scratch_shapes=[pltpu.VMEM((S, bw), f32)] * n,
            compiler_params=_cparams(("parallel",)),
            name=name + "_bwd",
        )(*xs, *cws, *cbs, dy)

    @jax.custom_vjp
    def op(h, ws, cws, cbs):
        return call_fwd([_mm_call(h, w, "nn", f32) for w in ws], cws, cbs)

    def fwd(h, ws, cws, cbs):
        xs = [_mm_call(h, w, "nn", f32) for w in ws]
        return call_fwd(xs, cws, cbs), (h, ws, xs, cws, cbs)

    def bwd(res, dy):
        h, ws, xs, cws, cbs = res
        out = call_bwd(xs, cws, cbs, dy)
        dxs, dcws, dcbs = out[:n], out[n:2 * n], out[2 * n:]
        dh = _mm_call(dxs[0], ws[0], "nt", h.dtype)
        for dx, w in zip(dxs[1:], ws[1:]):
            dh = dh + _mm_call(dx, w, "nt", h.dtype)
        dws = tuple(_mm_call(h, dx, "tn", w.dtype) for dx, w in zip(dxs, ws))
        return dh, dws, tuple(d[:W] for d in dcws), tuple(dcbs)

    op.defvjp(fwd, bwd)
    return op(h, tuple(ws), tuple(cws), tuple(b.reshape(1, C) for b in cbs))


@jax.custom_vjp
def _masked_decay(cs_col, cs_row, mask01):
    return jnp.where(mask01 > 0, jnp.exp(cs_col - cs_row), 0.0)


def _masked_decay_fwd(cs_col, cs_row, mask01):
    d = jnp.where(mask01 > 0, jnp.exp(cs_col - cs_row), 0.0)
    return d, (d, mask01)


def _masked_decay_bwd(res, g):
    d, mask01 = res
    t = g * d
    return jnp.sum(t, axis=1, keepdims=True), -jnp.sum(t, axis=0, keepdims=True), jnp.zeros_like(mask01)


_masked_decay.defvjp(_masked_decay_fwd, _masked_decay_bwd)


def _scan_chunk(q, k, x, cs_tok, dt_tok, h, *, rev, incl, nsub, head0, mxu_transpose):
    L, N = q.shape
    W = x.shape[1]
    Hg = cs_tok.shape[1]
    nh = W // HEAD_W
    t = lax.broadcasted_iota(jnp.int32, (L, L), 0)
    l = lax.broadcasted_iota(jnp.int32, (L, L), 1)
    if rev:
        mask = (l >= t) if incl else (l > t)
    else:
        mask = (l <= t) if incl else (l < t)
    mask01 = mask.astype(f32)
    lane_a = lax.broadcasted_iota(jnp.int32, cs_tok.shape, 1)
    row_a = lax.broadcasted_iota(jnp.int32, (Hg, L), 0)
    last = lax.broadcasted_iota(jnp.int32, (1, L), 1) == (0 if rev else L - 1)
    vhead = lax.broadcasted_iota(jnp.int32, (1, W), 1) // HEAD_W
    qhead = lax.broadcasted_iota(jnp.int32, (1, N), 1) // (N // nsub)
    eye = t == l
    if mxu_transpose:
        cs_rows = lax.dot_general(cs_tok, eye.astype(f32), _DIMS["tn"], precision=lax.Precision.HIGHEST,
                                  preferred_element_type=f32)

    decay, lam_e, tau_e, gam_e, dt_e = [], 0.0, 0.0, 0.0, 0.0
    for i in range(nh):
        cs_col = jnp.sum(jnp.where(lane_a == head0 + i, cs_tok, 0.0), axis=1, keepdims=True)
        if mxu_transpose:
            cs_row = jnp.sum(jnp.where(row_a == head0 + i, cs_rows, 0.0), axis=0, keepdims=True)
        else:
            cs_row = jnp.sum(jnp.where(eye, cs_col, 0.0), axis=0, keepdims=True)
        tot = jnp.sum(jnp.where(last, cs_row, 0.0), axis=1, keepdims=True)
        decay.append(_masked_decay(cs_col, cs_row, mask01))
        sel = vhead == i
        lam_e = lam_e + jnp.where(sel, jnp.exp(cs_col), 0.0)
        tau_e = tau_e + jnp.where(sel, jnp.exp(tot - cs_col), 0.0)
        gam_e = gam_e + jnp.where(sel, jnp.exp(tot), 0.0)
        if dt_tok is not None:
            dt_col = jnp.sum(jnp.where(lane_a == head0 + i, dt_tok, 0.0), axis=1, keepdims=True)
            dt_e = dt_e + jnp.where(sel, dt_col, 0.0)
    v = x if dt_tok is None else x * dt_e
    s_shared = bdot(q, k, "nt") if nsub == 1 else None
    y = lam_e * bdot(q, h, "nn")
    for i in range(nh):
        s = s_shared if nsub == 1 else bdot(jnp.where(qhead == i, q, 0.0), k, "nt")
        y = y + jnp.where(vhead == i, bdot(s * decay[i], v, "nn"), 0.0)
    hn = gam_e * h + bdot(k, tau_e * v, "tn")
    if nsub > 1:
        nhead = lax.broadcasted_iota(jnp.int32, (N, W), 0) // (N // nsub)
        hn = jnp.where(nhead == lax.broadcasted_iota(jnp.int32, (N, W), 1) // HEAD_W, hn, 0.0)
    return y, hn


def chunk_cumsum(a_tok, rev):
    G, S, Hg = a_tok.shape
    L = CHUNK

    def call(a, rev):
        def body(a_ref, o_ref):
            t = lax.broadcasted_iota(jnp.int32, (L, L), 0)
            l = lax.broadcasted_iota(jnp.int32, (L, L), 1)
            tri = ((l >= t) if rev else (l <= t)).astype(f32)
            o_ref[0] = _exact_dot(tri, a_ref[0])

        spec = pl.BlockSpec((1, L, Hg), lambda g, c: (g, c, 0))
        return pl.pallas_call(
            body,
            out_shape=jax.ShapeDtypeStruct((G, S, Hg), f32),
            grid=(G, S // L),
            in_specs=[spec],
            out_specs=spec,
            compiler_params=_cparams(("parallel", "parallel")),
            name="chunk_cumsum",
        )(a)

    @jax.custom_vjp
    def op(a):
        return call(a, rev)

    def fwd(a):
        return call(a, rev), None

    def bwd(_, g):
        return (call(g, not rev),)

    op.defvjp(fwd, bwd)
    return op(a_tok)


def scan_op(q, k, x, a_tok, dt_tok, *, rev, incl, nsub):
    S = q.shape[0]
    G, _, Hg = a_tok.shape
    N = q.shape[1] // G
    Vw = x.shape[1] // G
    L = CHUNK
    nc = S // L
    use_dt = dt_tok is not None
    PW = min(Vw, LANES)
    chunk = functools.partial(_scan_chunk, rev=rev, incl=incl, nsub=nsub, mxu_transpose=Vw > PW)
    blocks = [(p * PW, (p * PW) // HEAD_W) for p in range(Vw // PW)]
    own_qk = nsub > 1
    NB = PW if own_qk else N

    def order(c, backward):
        return (nc - 1 - c) if (rev != backward) else c

    def specs(backward):
        qs = pl.BlockSpec((L, N), lambda g, c: (order(c, backward), g))
        xs = pl.BlockSpec((L, Vw), lambda g, c: (order(c, backward), g))
        as_ = pl.BlockSpec((1, L, Hg), lambda g, c: (g, order(c, backward), 0))
        hs = pl.BlockSpec((1, 1, NB, Vw), lambda g, c: (g, order(c, backward), 0, 0))
        return qs, xs, as_, hs

    def call_fwd(q, k, x, a_tok, dt_tok, y_prev=None):
        qs, xs, as_, hs = specs(False)
        n_in = 4 + use_dt + (y_prev is not None)

        def body(*refs):
            q_ref, k_ref, x_ref, a_ref = refs[:4]
            dt_ref = refs[4] if use_dt else None
            yp_ref = refs[n_in - 1] if y_prev is not None else None
            y_ref, hs_ref, h_scr = refs[n_in:]

            @pl.when(pl.program_id(1) == 0)
            def _():
                h_scr[...] = jnp.zeros_like(h_scr)

            hs_ref[0, 0] = h_scr[...]
            q, k, a, dt = q_ref[...], k_ref[...], a_ref[0], dt_ref[0] if use_dt else None
            for lane0, head0 in blocks:
                cols = slice(lane0, lane0 + PW)
                qb, kb = (q[:, cols], k[:, cols]) if own_qk else (q, k)
                y, hn = chunk(qb, kb, x_ref[:, cols], a, dt, h_scr[:, cols], head0=head0)
                y_ref[:, cols] = y if yp_ref is None else y + yp_ref[:, cols]
                h_scr[:, cols] = hn

        ins = [q, k, x, a_tok] + ([dt_tok] if use_dt else []) + ([y_prev] if y_prev is not None else [])
        return pl.pallas_call(
            body,
            out_shape=[jax.ShapeDtypeStruct((S, G * Vw), f32), jax.ShapeDtypeStruct((G, nc, NB, Vw), f32)],
            grid=(G, nc),
            in_specs=[qs, qs, xs, as_] + ([as_] if use_dt else []) + ([xs] if y_prev is not None else []),
            out_specs=[xs, hs],
            scratch_shapes=[pltpu.VMEM((NB, Vw), f32)],
            compiler_params=_cparams(("parallel", "arbitrary")),
            name="scan_fwd",
        )(*ins)

    def call_bwd(q, k, x, a_tok, dt_tok, hsave, dy, acc=None):
        qs, xs, as_, hs = specs(True)
        n_in = 6 + use_dt + (3 if acc is not None else 0)

        def body(*refs):
            q_ref, k_ref, x_ref, a_ref = refs[:4]
            dt_ref = refs[4] if use_dt else None
            hs_ref, dy_ref = refs[4 + use_dt], refs[5 + use_dt]
            acc_refs = refs[n_in - 3:n_in] if acc is not None else None
            dq_ref, dk_ref, dx_ref, da_ref = refs[n_in:n_in + 4]
            ddt_ref = refs[n_in + 4] if use_dt else None
            dh_scr = refs[-1]

            @pl.when(pl.program_id(1) == 0)
            def _():
                dh_scr[...] = jnp.zeros_like(dh_scr)

            q, k, a = q_ref[...].astype(f32), k_ref[...].astype(f32), a_ref[0]
            dq, dk, da, ddt = 0.0, 0.0, 0.0, 0.0
            if acc is not None:
                dq, dk = acc_refs[0][...].astype(f32), acc_refs[1][...].astype(f32)
            for lane0, head0 in blocks:
                cols = slice(lane0, lane0 + PW)
                qb, kb = (q[:, cols], k[:, cols]) if own_qk else (q, k)
                if use_dt:
                    f = lambda q, k, x, a, dt, h: chunk(q, k, x, a, dt, h, head0=head0)
                    prim = [qb, kb, x_ref[:, cols], a, dt_ref[0], hs_ref[0, 0, :, cols]]
                else:
                    f = lambda q, k, x, a, h: chunk(q, k, x, a, None, h, head0=head0)
                    prim = [qb, kb, x_ref[:, cols], a, hs_ref[0, 0, :, cols]]
                _, vjp = jax.vjp(f, *prim)
                cts = vjp((dy_ref[:, cols], dh_scr[:, cols]))
                da = da + cts[3]
                if own_qk:
                    dqb, dkb = (cts[0], cts[1]) if acc is None else (cts[0] + dq[:, cols], cts[1] + dk[:, cols])
                    dq_ref[:, cols] = dqb.astype(dq_ref.dtype)
                    dk_ref[:, cols] = dkb.astype(dk_ref.dtype)
                else:
                    dq, dk = dq + cts[0], dk + cts[1]
                if use_dt:
                    ddt = ddt + cts[4]
                dx_ref[:, cols] = cts[2] if acc is None else cts[2] + acc_refs[2][:, cols]
                dh_scr[:, cols] = cts[-1]
            if not own_qk:
                dq_ref[...] = dq.astype(dq_ref.dtype)
                dk_ref[...] = dk.astype(dk_ref.dtype)
            da_ref[0] = da
            if use_dt:
                ddt_ref[0] = ddt

        ins = [q, k, x, a_tok] + ([dt_tok] if use_dt else []) + [hsave, dy] + (list(acc) if acc is not None else [])
        a_shape = jax.ShapeDtypeStruct(a_tok.shape, f32)
        return pl.pallas_call(
            body,
            out_shape=[jax.ShapeDtypeStruct(q.shape, q.dtype), jax.ShapeDtypeStruct(k.shape, k.dtype),
                       jax.ShapeDtypeStruct(x.shape, f32), a_shape] + ([a_shape] if use_dt else []),
            grid=(G, nc),
            in_specs=[qs, qs, xs, as_] + ([as_] if use_dt else []) + [hs, xs] + ([qs, qs, xs] if acc is not None else []),
            out_specs=[qs, qs, xs, as_] + ([as_] if use_dt else []),
            scratch_shapes=[pltpu.VMEM((NB, Vw), f32)],
            compiler_params=_cparams(("parallel", "arbitrary")),
            name="scan_bwd",
        )(*ins)

    return call_fwd, call_bwd


def bidir_scan(q, k, x, a_f, a_b, dt_f, dt_b, *, nsub):
    use_dt = dt_f is not None
    a_f, a_b = chunk_cumsum(a_f, False), chunk_cumsum(a_b, True)
    fwd_f, bwd_f = scan_op(q, k, x, a_f, dt_f, rev=False, incl=True, nsub=nsub)
    fwd_b, bwd_b = scan_op(q, k, x, a_b, dt_b, rev=True, incl=False, nsub=nsub)

    def run(q, k, x, a_f, a_b, dt_f, dt_b):
        y_f, hs_f = fwd_f(q, k, x, a_f, dt_f)
        y, hs_b = fwd_b(q, k, x, a_b, dt_b, y_prev=y_f)
        return y, (hs_f, hs_b)

    def grads(q, k, x, a_f, a_b, dt_f, dt_b, hs, dy):
        first = bwd_f(q, k, x, a_f, dt_f, hs[0], dy)
        both = bwd_b(q, k, x, a_b, dt_b, hs[1], dy, acc=first[:3])
        return both[0], both[1], both[2], first[3], both[3], (first[4] if use_dt else None), (both[4] if use_dt else None)

    if use_dt:
        @jax.custom_vjp
        def op(q, k, x, a_f, a_b, dt_f, dt_b):
            return run(q, k, x, a_f, a_b, dt_f, dt_b)[0]

        def fwd(q, k, x, a_f, a_b, dt_f, dt_b):
            y, hs = run(q, k, x, a_f, a_b, dt_f, dt_b)
            return y, (q, k, x, a_f, a_b, dt_f, dt_b, hs)

        def bwd(res, dy):
            return grads(*res, dy)

        op.defvjp(fwd, bwd)
        return op(q, k, x, a_f, a_b, dt_f, dt_b)

    @jax.custom_vjp
    def op(q, k, x, a_f, a_b):
        return run(q, k, x, a_f, a_b, None, None)[0]

    def fwd(q, k, x, a_f, a_b):
        y, hs = run(q, k, x, a_f, a_b, None, None)
        return y, (q, k, x, a_f, a_b, hs)

    def bwd(res, dy):
        q, k, x, a_f, a_b, hs = res
        return grads(q, k, x, a_f, a_b, None, None, hs, dy)[:5]

    op.defvjp(fwd, bwd)
    return op(q, k, x, a_f, a_b)


def _swap_halves(x, dh):
    W = x.shape[1]
    lane = lax.broadcasted_iota(jnp.int32, (1, W), 1) % dh
    return jnp.where(lane < dh // 2, pltpu.roll(x, W - dh // 2, 1), pltpu.roll(x, dh // 2, 1))


def rotary(rq, rk, cos_t, sin_t):
    scale = RET_DH ** -0.5

    def fn(rq, rk, c, s):
        return rq * c + _swap_halves(rq, RET_DH) * s, (rk * c + _swap_halves(rk, RET_DH) * s) * scale

    def bwd_fn(rv, pv, dos):
        _, _, c, s = rv
        dq, dk = dos
        dk = dk * scale
        return (dq * c + _swap_halves(dq * s, RET_DH), dk * c + _swap_halves(dk * s, RET_DH)), ()

    return rowwise(fn, "rotary", [rq, rk, cos_t, sin_t], [], [MXU_DTYPE, MXU_DTYPE], n_diff_rows=2, bwd_fn=bwd_fn)


def _rope_tables(S, width):
    half = RET_DH // 2
    inv = 1.0 / (ROPE_BASE ** (jnp.arange(half, dtype=f32) / half))
    ang = jnp.arange(S, dtype=f32)[:, None] * inv[None, :]
    cos, sin = jnp.cos(ang), jnp.sin(ang)
    reps = width // RET_DH
    return jnp.tile(jnp.concatenate([cos, cos], axis=1), (1, reps)), jnp.tile(jnp.concatenate([-sin, sin], axis=1), (1, reps))


def _exact_dot(x, m):
    return jnp.dot(x, m, precision=lax.Precision.HIGHEST, preferred_element_type=f32)


def ret_post(y, rg, gn_g):
    W = y.shape[1]
    idx = np.arange(W) // RET_DH
    avg = jnp.asarray((idx[:, None] == idx[None, :]).astype(np.float32) / RET_DH)

    def fn(y, rg, g, avg):
        mu = _exact_dot(y, avg)
        d = y - mu
        var = _exact_dot(d * d, avg)
        return (_silu(rg) * (d * lax.rsqrt(var + EPS) * g),)

    return rowwise(fn, "ret_post", [y, rg], [gn_g.reshape(1, -1), avg], [MXU_DTYPE], n_diff_params=1)[0]


def _na_bias(rpb, win_r):
    H = rpb.shape[0]
    qc = np.arange(GRID_W)[:, None]
    kc = np.arange(GRID_W)[None, :]
    cstart = np.clip(qc - NA_WIN_C // 2, 0, GRID_W - NA_WIN_C)
    valid = (kc >= cstart) & (kc < cstart + NA_WIN_C)
    dc = np.clip(kc - qc, -(NA_WIN_C - 1), NA_WIN_C - 1) + (NA_WIN_C - 1)
    onehot = (dc[None] == np.arange(2 * NA_WIN_C - 1)[:, None, None]).astype(np.float32)
    t1 = jnp.einsum("hrd,dqk->hrqk", rpb.astype(f32), jnp.asarray(onehot), precision=lax.Precision.HIGHEST)
    per_delta = [t1[:, NA_WIN_R - 1 - d:NA_WIN_R - 1 - d + win_r] for d in range(win_r)]
    b = jnp.stack(per_delta, axis=1)
    b = jnp.where(jnp.asarray(valid)[None, None, None], b, NEG_INF)
    return jnp.transpose(b, (0, 1, 3, 2, 4)).reshape(H, win_r, GRID_W, win_r * GRID_W)


def _na_row(q, kw, vw, biases):
    lane = lax.broadcasted_iota(jnp.int32, (1, q.shape[1]), 1) // NA_DH
    o = 0.0
    for i, b in enumerate(biases):
        qi = jnp.where(lane == i, q, 0.0) * (NA_DH ** -0.5)
        s = bdot(qi, kw, "nt") + b
        e = jnp.exp(s - jnp.max(s, axis=1, keepdims=True))
        p = e / jnp.sum(e, axis=1, keepdims=True)
        o = o + jnp.where(lane == i, bdot(p, vw, "nn"), 0.0)
    return o


def _na_row_bwd(q, kw, vw, biases, do):
    lane = lax.broadcasted_iota(jnp.int32, (1, q.shape[1]), 1) // NA_DH
    scale = NA_DH ** -0.5
    dq, dk, dv, dbs = 0.0, 0.0, 0.0, []
    for i, b in enumerate(biases):
        sel = lane == i
        qi = jnp.where(sel, q, 0.0) * scale
        s = _bdot_raw(qi, kw, "nt") + b
        e = jnp.exp(s - jnp.max(s, axis=1, keepdims=True))
        p = e / jnp.sum(e, axis=1, keepdims=True)
        doi = jnp.where(sel, do, 0.0)
        dp = _bdot_raw(doi, vw, "nt")
        ds = p * (dp - jnp.sum(dp * p, axis=1, keepdims=True))
        dbs.append(ds)
        dq = dq + jnp.where(sel, _bdot_raw(ds, kw, "nn"), 0.0) * scale
        dk = dk + _bdot_raw(ds, qi, "tn")
        dv = dv + _bdot_raw(p, doi, "tn")
    return dq, dk, dv, dbs


def na_op(nq, nk, nv, bias):
    S, W = nq.shape
    rows = S // GRID_W
    win_r = bias.shape[1]
    nkeys = win_r * GRID_W
    hp = LANES // NA_DH
    npair = W // LANES
    RB = min(16, rows)
    nrb = rows // RB
    qspec = pl.BlockSpec((RB * GRID_W, LANES), lambda p, r: (r, p))
    kspec = pl.BlockSpec((S, LANES), lambda p, r: (0, p))
    bspec = pl.BlockSpec((hp, win_r, GRID_W, nkeys), lambda p, r: (p, 0, 0, 0))

    def window(r):
        r0 = jnp.clip(r - win_r // 2, 0, rows - win_r)
        return pl.multiple_of(r0 * GRID_W, GRID_W), r - r0

    def call_fwd(nq, nk, nv, bias):
        def body(q_ref, k_ref, v_ref, b_ref, o_ref):
            rb = pl.program_id(1)

            def row(i, c):
                k0, d = window(rb * RB + i)
                q0 = pl.multiple_of(i * GRID_W, GRID_W)
                o = _na_row(q_ref[pl.ds(q0, GRID_W), :].astype(f32), k_ref[pl.ds(k0, nkeys), :], v_ref[pl.ds(k0, nkeys), :],
                            [b_ref[h, pl.ds(d, 1)][0] for h in range(hp)])
                o_ref[pl.ds(q0, GRID_W), :] = o.astype(o_ref.dtype)
                return c

            lax.fori_loop(0, RB, row, 0, unroll=4)

        return pl.pallas_call(
            body,
            out_shape=jax.ShapeDtypeStruct((S, W), nq.dtype),
            grid=(npair, nrb),
            in_specs=[qspec, kspec, kspec, bspec],
            out_specs=qspec,
            compiler_params=_cparams(("parallel", "arbitrary")),
            name="na_fwd",
        )(nq, nk, nv, bias)

    def call_bwd(nq, nk, nv, bias, do):
        def body(q_ref, k_ref, v_ref, b_ref, do_ref, dq_ref, dk_ref, dv_ref, db_ref, dk_acc, dv_acc):
            rb = pl.program_id(1)

            @pl.when(rb == 0)
            def _():
                dk_acc[...] = jnp.zeros_like(dk_acc)
                dv_acc[...] = jnp.zeros_like(dv_acc)
                db_ref[...] = jnp.zeros_like(db_ref)

            def row(i, c):
                k0, d = window(rb * RB + i)
                q0 = pl.multiple_of(i * GRID_W, GRID_W)
                bs = [b_ref[h, pl.ds(d, 1)][0] for h in range(hp)]
                dq, dk, dv, dbs = _na_row_bwd(q_ref[pl.ds(q0, GRID_W), :].astype(f32), k_ref[pl.ds(k0, nkeys), :],
                                              v_ref[pl.ds(k0, nkeys), :], bs, do_ref[pl.ds(q0, GRID_W), :].astype(f32))
                dq_ref[pl.ds(q0, GRID_W), :] = dq.astype(dq_ref.dtype)
                dk_acc[pl.ds(k0, nkeys), :] += dk
                dv_acc[pl.ds(k0, nkeys), :] += dv
                for h in range(hp):
                    db_ref[h, pl.ds(d, 1)] += dbs[h][None]
                return c

            lax.fori_loop(0, RB, row, 0, unroll=2)

            @pl.when(rb == nrb - 1)
            def _():
                dk_ref[...] = dk_acc[...].astype(dk_ref.dtype)
                dv_ref[...] = dv_acc[...].astype(dv_ref.dtype)

        return pl.pallas_call(
            body,
            out_shape=[jax.ShapeDtypeStruct((S, W), nq.dtype), jax.ShapeDtypeStruct((S, W), nk.dtype),
                       jax.ShapeDtypeStruct((S, W), nv.dtype), jax.ShapeDtypeStruct(bias.shape, f32)],
            grid=(npair, nrb),
            in_specs=[qspec, kspec, kspec, bspec, qspec],
            out_specs=[qspec, kspec, kspec, bspec],
            scratch_shapes=[pltpu.VMEM((S, LANES), f32), pltpu.VMEM((S, LANES), f32)],
            compiler_params=_cparams(("parallel", "arbitrary")),
            name="na_bwd",
        )(nq, nk, nv, bias, do)

    @jax.custom_vjp
    def op(nq, nk, nv, bias):
        return call_fwd(nq, nk, nv, bias)

    def fwd(nq, nk, nv, bias):
        return call_fwd(nq, nk, nv, bias), (nq, nk, nv, bias)

    def bwd(res, do):
        return tuple(call_bwd(*res, do))

    op.defvjp(fwd, bwd)
    return op(nq, nk, nv, bias)


def ssd_dt(dt_raw, dt_bias, a_neg):
    def fn(r, b, a):
        dt = _softplus(r + b)
        return dt, dt * a

    return rowwise(fn, "ssd_dt", [dt_raw], [dt_bias, a_neg], [f32, f32])


def ssd_post(y, xs, z, d_skip_lanes, norm_g, groups):
    def fn(y, xs, z, dsk, g):
        y = (y + xs * dsk) * _silu(z)
        return (y * lax.rsqrt(jnp.mean(y * y, axis=-1, keepdims=True) + EPS) * g,)

    return rowwise(fn, "ssd_post", [y, xs, z], [d_skip_lanes.reshape(1, -1), norm_g.reshape(1, -1)], [MXU_DTYPE],
                   ncol=groups)[0]


def _heads_major(t, groups):
    S = t.shape[0]
    return jnp.transpose(t.reshape(S, groups, -1), (1, 0, 2))


def retention_na_mixer(hn, w_in, decay_logit, gn_g, rpb, w_out, tables):
    S = hn.shape[0]
    R = RET_HEADS * RET_DH
    NW = NA_HEADS * NA_DH
    cols = lambda a, b: w_in[:, a:b]
    rq, rk, rv, rg = (mm(hn, cols(j * R, (j + 1) * R)) for j in range(4))
    nq, nk, nv = (mm(hn, cols(4 * R + j * NW, 4 * R + (j + 1) * NW), out_dtype=MXU_DTYPE) for j in range(3))
    qr, kr = rotary(rq, rk, *tables)
    log_gamma = -_softplus(-decay_logit.astype(f32))
    pairs = R // LANES
    hp = LANES // RET_DH
    hpad = -(-RET_HEADS // 8) * 8
    pad8 = lambda a: jnp.pad(a.reshape(1, 1, RET_HEADS), ((0, 0), (0, 0), (0, hpad - RET_HEADS)))
    a_f = jnp.broadcast_to(pad8(log_gamma[0]), (1, S, hpad))
    a_b = jnp.broadcast_to(pad8(log_gamma[1]), (1, S, hpad))
    ret = ret_post(bidir_scan(qr, kr, rv, a_f, a_b, None, None, nsub=hp), rg, gn_g)
    rows = S // GRID_W
    nao = na_op(nq, nk, nv, _na_bias(rpb, min(NA_WIN_R, rows)))
    return mm(ret, w_out[:R]) + mm(nao, w_out[R:])


def ssd_mixer(hn, w_in, conv_w, conv_b, dt_bias, a_log, d_skip, norm_g, w_out):
    heads = d_skip.shape[0]
    inner = heads * SSD_HEADDIM
    gs = SSD_GROUPS * SSD_STATE
    o_x, o_b, o_c, o_dt = inner, 2 * inner, 2 * inner + gs, 2 * inner + 2 * gs
    z = mm(hn, w_in[:, :inner])
    dt_raw = mm(hn, w_in[:, o_dt:])
    xs, bm, cm = (mm_conv_act(hn, [w_in[:, a:b]], [conv_w[:, a - inner:b - inner]], [conv_b[a - inner:b - inner]],
                              _silu, f32, "conv_silu") for a, b in ((o_x, o_b), (o_b, o_c), (o_c, o_dt)))
    a_neg = -jnp.exp(a_log.astype(f32)).reshape(1, -1)
    dt, la = ssd_dt(dt_raw, dt_bias.astype(f32).reshape(1, -1), a_neg)
    dt_f, dt_b = _heads_major(dt[:, :heads], SSD_GROUPS), _heads_major(dt[:, heads:], SSD_GROUPS)
    la_f, la_b = _heads_major(la[:, :heads], SSD_GROUPS), _heads_major(la[:, heads:], SSD_GROUPS)
    y = bidir_scan(cm, bm, xs, la_f, la_b, dt_f, dt_b, nsub=1)
    y = ssd_post(y, xs, z, jnp.repeat(d_skip.astype(f32), SSD_HEADDIM), norm_g, SSD_GROUPS)
    return mm(y, w_out)


def conv_geglu_ffn(hf, w_up, conv_w, conv_b, w_down):
    F = w_down.shape[0]
    a = mm_conv_act(hf, [w_up[:, :F], w_up[:, F:]], [conv_w[:, :F], conv_w[:, F:]], [conv_b[:F], conv_b[F:]],
                    lambda g, v: _gelu_tanh(g) * v, MXU_DTYPE, "conv_geglu")
    return mm(a, w_down)


def model_loss(x, tgt, big, small, rep):
    S = x.shape[0]
    depth = rep["norm_mix_pre"].shape[0]
    tables = _rope_tables(S, RET_HEADS * RET_DH)
    hn = rms(x, rep["norm_mix_pre"][0], MXU_DTYPE)
    for layer in range(depth):
        i = layer // 2
        if layer % 2 == 0:
            m = retention_na_mixer(hn, big["ab_w_in"][i], rep["ab_ret_decay_logit"][i], rep["ab_ret_gn_g"][i],
                                   rep["ab_na_rpb"][i], big["ab_w_out"][i], tables)
        else:
            m = ssd_mixer(hn, big["c_w_in"][i], small["c_conv_w"][i], small["c_conv_b"][i], rep["c_dt_bias"][i],
                          rep["c_a_log"][i], rep["c_d_skip"][i], small["c_norm_g"][i], big["c_w_out"][i])
        x, hf = rms_residual_norm(m, rep["norm_mix_post"][layer], x, rep["norm_ffn_pre"][layer])
        f = conv_geglu_ffn(hf, big["ffn_w_up"][layer], small["ffn_conv_w"][layer], rep["ffn_conv_b"][layer],
                           big["ffn_w_down"][layer])
        if layer + 1 < depth:
            x, hn = rms_residual_norm(f, rep["norm_ffn_post"][layer], x, rep["norm_mix_pre"][layer + 1])
        else:
            x = rms_residual(f, rep["norm_ffn_post"][layer], x)
    return loss_op(x, tgt)


def _mesh_pos():
    return lax.axis_index("x"), lax.axis_index("y"), lax.axis_index("c")


def gather_chips(local):
    R, Wd = local.shape

    half = R // 2
    CH = COPY_CHUNKS
    q = half // CH

    def body(x_ref, out_ref, send_sems, recv_sems):
        x, y, c = _mesh_pos()
        my = 2 * x + y
        chips = [(1 - x, y), (x, 1 - y), (1 - x, 1 - y)]

        def piece(ref, h, j):
            return ref.at[pl.ds(pl.multiple_of(h * half + j * q, PACK_ALIGN), q), :]

        def copy(k, src, chip, h, j, to):
            return pltpu.make_async_remote_copy(src_ref=src, dst_ref=piece(out_ref.at[chip], h, j), send_sem=send_sems.at[k],
                                                recv_sem=recv_sems.at[k], device_id=to, device_id_type=pl.DeviceIdType.MESH)

        first = [[copy(k * CH + j, piece(x_ref, c, j), my, c, j, (cx, cy, c)) for j in range(CH)]
                 for k, (cx, cy) in enumerate(chips)]
        for j in range(CH):
            for k in range(3):
                first[k][j].start()
        passed = [[copy((3 + k) * CH + j, piece(out_ref.at[2 * cx + cy], c, j), 2 * cx + cy, c, j, (x, y, 1 - c))
                   for j in range(CH)] for k, (cx, cy) in enumerate(chips)]
        for j in range(CH):
            for k, (cx, cy) in enumerate(chips):
                copy(k * CH + j, piece(x_ref, c, j), 2 * cx + cy, c, j, (cx, cy, c)).wait_recv()
                passed[k][j].start()
        for j in range(CH):
            for k, (cx, cy) in enumerate(chips):
                copy((3 + k) * CH + j, piece(x_ref, c, j), 2 * cx + cy, 1 - c, j, (x, y, 1 - c)).wait_recv()
        for k in range(3):
            for cp in first[k] + passed[k]:
                cp.wait_send()

    return pl.pallas_call(
        body,
        out_shape=jax.ShapeDtypeStruct((N_CHIPS, R, Wd), local.dtype),
        in_specs=[pl.BlockSpec(memory_space=pl.ANY)],
        out_specs=pl.BlockSpec(memory_space=pl.ANY),
        scratch_shapes=[pltpu.SemaphoreType.DMA((6 * CH,)), pltpu.SemaphoreType.DMA((6 * CH,))],
        name="gather_chips",
    )(local)


def pair_swap(parts):
    n, R, Wd = parts.shape
    half = R // 2

    CH = COPY_CHUNKS
    q = half // CH

    def body(p_ref, got_ref, send_sems, recv_sems):
        x, y, c = _mesh_pos()

        def src(s, j):
            return p_ref.at[s, pl.ds(pl.multiple_of((1 - c) * half + j * q, PACK_ALIGN), q), :]

        swap = [pltpu.make_async_remote_copy(src_ref=src(s, j), dst_ref=got_ref.at[s, pl.ds(j * q, q), :],
                                             send_sem=send_sems.at[s * CH + j], recv_sem=recv_sems.at[s * CH + j],
                                             device_id=(x, y, 1 - c), device_id_type=pl.DeviceIdType.MESH)
                for s in range(n) for j in range(CH)]
        for cp in swap:
            cp.start()
        for cp in swap:
            cp.wait()

    return pl.pallas_call(
        body,
        out_shape=jax.ShapeDtypeStruct((n, half, Wd), parts.dtype),
        in_specs=[pl.BlockSpec(memory_space=pl.ANY)],
        out_specs=pl.BlockSpec(memory_space=pl.ANY),
        scratch_shapes=[pltpu.SemaphoreType.DMA((n * CH,)), pltpu.SemaphoreType.DMA((n * CH,))],
        name="pair_swap",
    )(parts)


def chip_exchange(parts):
    n, R, Wd = parts.shape

    def body(p_ref, out_ref, send_sems, recv_sems):
        x, y, c = _mesh_pos()
        my = 2 * x + y
        chips = [(1 - x, y), (x, 1 - y), (1 - x, 1 - y)]

        def copy(k, src_slot, dst_slot, to):
            return pltpu.make_async_remote_copy(src_ref=p_ref.at[src_slot], dst_ref=out_ref.at[dst_slot], send_sem=send_sems.at[k],
                                                recv_sem=recv_sems.at[k], device_id=to, device_id_type=pl.DeviceIdType.MESH)

        sends = [copy(k, 2 * cx + cy, my, (cx, cy, c)) for k, (cx, cy) in enumerate(chips)]
        for cp in sends:
            cp.start()
        for k, (cx, cy) in enumerate(chips):
            copy(k, my, 2 * cx + cy, (cx, cy, c)).wait_recv()
        for cp in sends:
            cp.wait_send()

    return pl.pallas_call(
        body,
        out_shape=jax.ShapeDtypeStruct((n, R, Wd), parts.dtype),
        in_specs=[pl.BlockSpec(memory_space=pl.ANY)],
        out_specs=pl.BlockSpec(memory_space=pl.ANY),
        scratch_shapes=[pltpu.SemaphoreType.DMA((3,)), pltpu.SemaphoreType.DMA((3,))],
        name="chip_exchange",
    )(parts)


def pair_share(mine):
    R, Wd = mine.shape

    CH = 2 * COPY_CHUNKS
    q = R // CH

    def body(m_ref, out_ref, send_sems, recv_sems):
        x, y, c = _mesh_pos()
        swap = [pltpu.make_async_remote_copy(src_ref=m_ref.at[pl.ds(j * q, q), :], dst_ref=out_ref.at[pl.ds(j * q, q), :],
                                             send_sem=send_sems.at[j], recv_sem=recv_sems.at[j], device_id=(x, y, 1 - c),
                                             device_id_type=pl.DeviceIdType.MESH) for j in range(CH)]
        for cp in swap:
            cp.start()
        for cp in swap:
            cp.wait()

    return pl.pallas_call(
        body,
        out_shape=jax.ShapeDtypeStruct((R, Wd), mine.dtype),
        in_specs=[pl.BlockSpec(memory_space=pl.ANY)],
        out_specs=pl.BlockSpec(memory_space=pl.ANY),
        scratch_shapes=[pltpu.SemaphoreType.DMA((CH,)), pltpu.SemaphoreType.DMA((CH,))],
        name="pair_share",
    )(mine)


def sum_chips(recv, own):
    n, R, Wd = recv.shape
    tr = _pick(R, (512, 256, 128, 64, 32, 16, 8))

    def body(r_ref, p_ref, o_ref):
        my = 2 * lax.axis_index("x") + lax.axis_index("y")
        acc = jnp.zeros((tr, Wd), f32)
        for s in range(n):
            acc = acc + jnp.where(my == s, p_ref[s], r_ref[s]).astype(f32)
        o_ref[...] = acc

    spec = pl.BlockSpec((n, tr, Wd), lambda i: (0, i, 0))
    return pl.pallas_call(
        body,
        out_shape=jax.ShapeDtypeStruct((R, Wd), f32),
        grid=(R // tr,),
        in_specs=[spec, spec],
        out_specs=pl.BlockSpec((tr, Wd), lambda i: (i, 0)),
        compiler_params=_cparams(("parallel",)),
        name="sum_chips",
    )(recv, own)


def add_pair(parts, got):
    n, R, Wd = parts.shape
    half = R // 2
    tr = _pick(half, (512, 256, 128, 64, 32, 16, 8))
    nb = half // tr

    def body(lo_ref, hi_ref, g_ref, o_ref):
        mine = jnp.where(lax.axis_index("c") == 0, lo_ref[...], hi_ref[...])
        o_ref[...] = (mine.astype(f32) + g_ref[...].astype(f32)).astype(o_ref.dtype)

    spec = pl.BlockSpec((1, tr, Wd), lambda s, i: (s, i, 0))
    return pl.pallas_call(
        body,
        out_shape=jax.ShapeDtypeStruct(got.shape, parts.dtype),
        grid=(n, nb),
        in_specs=[spec, pl.BlockSpec((1, tr, Wd), lambda s, i: (s, nb + i, 0)), spec],
        out_specs=spec,
        compiler_params=_cparams(("parallel", "parallel")),
        name="add_pair",
    )(parts, parts, got)


def reduce_scatter(parts):
    chip_sum = add_pair(parts, pair_swap(parts))
    mine = sum_chips(chip_exchange(chip_sum), chip_sum)
    theirs = pair_share(mine)
    first = lax.axis_index("c") == 0
    return jnp.concatenate([jnp.where(first, mine, theirs), jnp.where(first, theirs, mine)], axis=0)


def adamw(w, g, m, v):
    R, C = w.shape
    tr = R
    for cand in (512, 256, 128, 64, 32, 16, 8):
        if R * C * 4 > (1 << 20) and R % cand == 0 and cand * C * 4 <= (1 << 20):
            tr = cand
            break

    def body(w_ref, g_ref, m_ref, v_ref, d_ref, mo_ref, vo_ref):
        g = g_ref[...]
        m = ADAM_B1 * m_ref[...] + (1.0 - ADAM_B1) * g
        v = ADAM_B2 * v_ref[...] + (1.0 - ADAM_B2) * (g * g)
        m_hat = m / (1.0 - ADAM_B1 ** ADAM_STEP)
        v_hat = v / (1.0 - ADAM_B2 ** ADAM_STEP)
        d_ref[...] = -ADAM_LR * (m_hat / (jnp.sqrt(v_hat) + ADAM_EPS) + ADAM_WD * w_ref[...])
        mo_ref[...] = m
        vo_ref[...] = v

    spec = pl.BlockSpec((tr, C), lambda i: (i, 0))
    return pl.pallas_call(
        body,
        out_shape=[jax.ShapeDtypeStruct((R, C), f32)] * 3,
        grid=(R // tr,),
        in_specs=[spec] * 4,
        out_specs=[spec] * 3,
        compiler_params=_cparams(("parallel",)),
        name="adamw",
    )(w, g, m, v)


def _pack(arrs, dtype):
    flat = jnp.concatenate([a.astype(dtype).reshape(-1) for a in arrs])
    n = flat.shape[0]
    unit = PACK_W * PACK_ROWS
    padded = -(-n // unit) * unit
    return jnp.pad(flat, (0, padded - n)).reshape(-1, PACK_W)


def _unpack(buf, shapes):
    flat = buf.reshape(-1)
    out, off = [], 0
    for s in shapes:
        n = int(np.prod(s))
        out.append(flat[off:off + n].reshape(s))
        off += n
    return out


BIG = (("ab_w_in", 2), ("ab_w_out", 1), ("c_w_in", 2), ("c_w_out", 1), ("ffn_w_up", 2), ("ffn_w_down", 1))
SMALL = (("c_conv_w", 2), ("c_conv_b", 1), ("c_norm_g", 1), ("ffn_conv_w", 2))
REP = ("norm_mix_pre", "norm_mix_post", "norm_ffn_pre", "norm_ffn_post", "ab_ret_decay_logit", "ab_ret_gn_g", "ab_na_rpb",
       "c_dt_bias", "c_a_log", "c_d_skip", "ffn_conv_b")
WEIGHTS = ("norm_mix_pre", "norm_mix_post", "norm_ffn_pre", "norm_ffn_post", "ab_w_in", "ab_ret_decay_logit", "ab_ret_gn_g",
           "ab_na_rpb", "ab_w_out", "c_w_in", "c_conv_w", "c_conv_b", "c_dt_bias", "c_a_log", "c_d_skip", "c_norm_g", "c_w_out",
           "ffn_w_up", "ffn_conv_w", "ffn_conv_b", "ffn_w_down")


def _gather_set(local, spec, dtype):
    shapes = [local[n].shape for n, _ in spec]
    got = gather_chips(_pack([local[n] for n, _ in spec], dtype))
    my = 2 * lax.axis_index("x") + lax.axis_index("y")
    per_chip = [_unpack(got[s], shapes) for s in range(N_CHIPS)]
    return {n: jnp.concatenate([jnp.where(my == s, local[n].astype(dtype), per_chip[s][j]) for s in range(N_CHIPS)], axis=ax)
            for j, (n, ax) in enumerate(spec)}


def _scatter_parts(full, spec, extra, dtype):
    split = {n: jnp.split(full[n], N_CHIPS, axis=ax) for n, ax in spec}
    return jnp.stack([_pack([split[n][s] for n, _ in spec] + list(extra), dtype) for s in range(N_CHIPS)])


def kernel(x, norm_mix_pre, norm_mix_post, norm_ffn_pre, norm_ffn_post, ab_w_in, ab_ret_decay_logit, ab_ret_gn_g, ab_na_rpb, ab_w_out, c_w_in, c_conv_w, c_conv_b, c_dt_bias, c_a_log, c_d_skip, c_norm_g, c_w_out, ffn_w_up, ffn_conv_w, ffn_conv_b, ffn_w_down, loss_target, m_norm_mix_pre, m_norm_mix_post, m_norm_ffn_pre, m_norm_ffn_post, m_ab_w_in, m_ab_ret_decay_logit, m_ab_ret_gn_g, m_ab_na_rpb, m_ab_w_out, m_c_w_in, m_c_conv_w, m_c_conv_b, m_c_dt_bias, m_c_a_log, m_c_d_skip, m_c_norm_g, m_c_w_out, m_ffn_w_up, m_ffn_conv_w, m_ffn_conv_b, m_ffn_w_down, v_norm_mix_pre, v_norm_mix_post, v_norm_ffn_pre, v_norm_ffn_post, v_ab_w_in, v_ab_ret_decay_logit, v_ab_ret_gn_g, v_ab_na_rpb, v_ab_w_out, v_c_w_in, v_c_conv_w, v_c_conv_b, v_c_dt_bias, v_c_a_log, v_c_d_skip, v_c_norm_g, v_c_w_out, v_ffn_w_up, v_ffn_conv_w, v_ffn_conv_b, v_ffn_w_down):
    args = dict(locals())
    w = {n: args[n] for n in WEIGHTS}
    mom = {n: args["m_" + n] for n in WEIGHTS}
    var = {n: args["v_" + n] for n in WEIGHTS}

    big = _gather_set(w, BIG, MXU_DTYPE)
    small = _gather_set(w, SMALL, f32)
    rep = {n: w[n] for n in REP}

    def loss_fn(xs, big, small, rep):
        return model_loss(xs, loss_target[0], big, small, rep)

    loss, (gx, gbig, gsmall, grep) = jax.value_and_grad(loss_fn, argnums=(0, 1, 2, 3))(x[0], big, small, rep)
    loss = lax.psum(loss, ("x", "y", "c"))

    big_shapes = [w[n].shape for n, _ in BIG]
    small_shapes = [w[n].shape for n, _ in SMALL] + [w[n].shape for n in REP]
    g_big = _unpack(reduce_scatter(_scatter_parts(gbig, BIG, (), MXU_DTYPE)), big_shapes)
    g_small_buf = reduce_scatter(_scatter_parts(gsmall, SMALL, [grep[n] for n in REP], f32))
    grads = dict(zip([n for n, _ in BIG], g_big))
    small_names = [n for n, _ in SMALL] + list(REP)
    grads.update(zip(small_names, _unpack(g_small_buf, small_shapes)))

    delta, new_m, new_v = {}, {}, {}
    for n in WEIGHTS:
        shp = w[n].shape
        two_d = lambda a: a.reshape(-1, shp[-1])
        d, m2, v2 = adamw(two_d(w[n]), two_d(grads[n]), two_d(mom[n]), two_d(var[n]))
        delta[n], new_m[n], new_v[n] = d.reshape(shp), m2.reshape(shp), v2.reshape(shp)

    return (loss, gx[None], *[grads[n] for n in WEIGHTS], *[delta[n] for n in WEIGHTS],
            *[new_m[n] for n in WEIGHTS], *[new_v[n] for n in WEIGHTS])
```

```python
import functools
import math

import numpy as np
import jax
import jax.numpy as jnp
from jax import lax
from jax.experimental import pallas as pl
from jax.experimental.pallas import tpu as pltpu

f32 = jnp.float32
bf16 = jnp.bfloat16
MXU_DTYPE = bf16

GRID_W = 64
CHUNK = 128
EPS = 1e-6
RET_HEADS = 8
RET_DH = 64
ROPE_BASE = 10000.0
NA_HEADS = 8
NA_DH = 64
NA_WIN_R = 8
NA_WIN_C = 16
NA_ROWS_PER_STEP = 4
SSD_HEADDIM = 64
SSD_GROUPS = 4
SSD_STATE = 128
ADAM_LR = 0.001
ADAM_B1 = 0.9
ADAM_B2 = 0.999
ADAM_EPS = 1e-08
ADAM_WD = 0.01
ADAM_STEP = 10

LANES = 128
HEAD_W = 64
PACK_W = 512
PACK_ROWS = 1024
PACK_ALIGN = 16
COPY_CHUNKS = 4
VMEM_LIMIT = 56 * 1024 * 1024
MM_BLOCK_BYTES = 6 * 1024 * 1024
N_CHIPS = 4
N_DEV = 8
NEG_INF = -1e30

_DIMS = {"nn": (((1,), (0,)), ((), ())), "nt": (((1,), (1,)), ((), ())), "tn": (((0,), (0,)), ((), ()))}


def _cparams(sem=None):
    return pltpu.CompilerParams(dimension_semantics=sem, vmem_limit_bytes=VMEM_LIMIT)


def _pick(dim, cands):
    for c in cands:
        if dim % c == 0:
            return c
    return dim


def _divisor_tile(dim, fits, align):
    for d in range(1, dim + 1):
        t = dim // d
        if dim % d == 0 and t % align == 0 and fits(t):
            return t
    return dim


def _bdot_raw(a, b, mode):
    return lax.dot_general(a.astype(MXU_DTYPE), b.astype(MXU_DTYPE), _DIMS[mode], preferred_element_type=f32)


@functools.partial(jax.custom_vjp, nondiff_argnums=(2,))
def bdot(a, b, mode):
    return _bdot_raw(a, b, mode)


def _bdot_fwd(a, b, mode):
    return _bdot_raw(a, b, mode), (a, b)


def _bdot_bwd(mode, res, g):
    a, b = res
    if mode == "nn":
        da, db = _bdot_raw(g, b, "nt"), _bdot_raw(a, g, "tn")
    elif mode == "nt":
        da, db = _bdot_raw(g, b, "nn"), _bdot_raw(g, a, "tn")
    else:
        da, db = _bdot_raw(b, g, "nt"), _bdot_raw(a, g, "nn")
    return da.astype(a.dtype), db.astype(b.dtype)


bdot.defvjp(_bdot_fwd, _bdot_bwd)


def _mm_call(a, b, mode, out_dtype):
    if mode == "nn":
        (M, K), (K2, N) = a.shape, b.shape
    elif mode == "nt":
        (M, K), (N, K2) = a.shape, b.shape
    else:
        (K, M), (K2, N) = a.shape, b.shape
    assert K == K2, (a.shape, b.shape, mode)
    a_bytes, b_bytes, o_bytes = a.dtype.itemsize, b.dtype.itemsize, jnp.dtype(out_dtype).itemsize
    if mode == "tn":
        tk = _pick(K, (512, 256, 128))
        tn = _divisor_tile(N, lambda t: t <= 1536, LANES)
        tm = _divisor_tile(M, lambda t: t * tn * 4 <= MM_BLOCK_BYTES, 8)
    else:
        tk, tn = K, N
        tm = _divisor_tile(M, lambda t: t * K * a_bytes <= MM_BLOCK_BYTES and t * N * o_bytes <= MM_BLOCK_BYTES, 8)
    nk = K // tk
    if mode == "nn":
        a_spec = pl.BlockSpec((tm, tk), lambda i, j, k: (i, k))
        b_spec = pl.BlockSpec((tk, tn), lambda i, j, k: (k, j))
    elif mode == "nt":
        a_spec = pl.BlockSpec((tm, tk), lambda i, j, k: (i, k))
        b_spec = pl.BlockSpec((tn, tk), lambda i, j, k: (j, k))
    else:
        a_spec = pl.BlockSpec((tk, tm), lambda i, j, k: (k, i))
        b_spec = pl.BlockSpec((tk, tn), lambda i, j, k: (k, j))

    if nk == 1:
        def body(a_ref, b_ref, o_ref):
            o_ref[...] = _bdot_raw(a_ref[...], b_ref[...], mode).astype(o_ref.dtype)
    else:
        def body(a_ref, b_ref, o_ref, acc_ref):
            k = pl.program_id(2)

            @pl.when(k == 0)
            def _():
                acc_ref[...] = jnp.zeros_like(acc_ref)

            acc_ref[...] += _bdot_raw(a_ref[...], b_ref[...], mode)

            @pl.when(k == nk - 1)
            def _():
                o_ref[...] = acc_ref[...].astype(o_ref.dtype)

    return pl.pallas_call(
        body,
        out_shape=jax.ShapeDtypeStruct((M, N), out_dtype),
        grid=(M // tm, N // tn, nk),
        in_specs=[a_spec, b_spec],
        out_specs=pl.BlockSpec((tm, tn), lambda i, j, k: (i, j)),
        scratch_shapes=[pltpu.VMEM((tm, tn), f32)] if nk > 1 else [],
        compiler_params=_cparams(("parallel", "parallel", "arbitrary")),
        name="mm_" + mode,
    )(a, b)


def mm(a, b, mode="nn", out_dtype=f32):
    @jax.custom_vjp
    def op(a, b):
        return _mm_call(a, b, mode, out_dtype)

    def fwd(a, b):
        return _mm_call(a, b, mode, out_dtype), (a, b)

    def bwd(res, g):
        a, b = res
        if mode == "nn":
            return _mm_call(g, b, "nt", a.dtype), _mm_call(a, g, "tn", b.dtype)
        if mode == "nt":
            return _mm_call(g, b, "nn", a.dtype), _mm_call(g, a, "tn", b.dtype)
        return _mm_call(b, g, "nt", a.dtype), _mm_call(a, g, "nn", b.dtype)

    op.defvjp(fwd, bwd)
    return op(a, b)


def _row_tile(S, row_bytes):
    tm = 512
    while tm > 8 and (tm * row_bytes > (6 << 20) or S % tm):
        tm //= 2
    return tm


def rowwise(fn, name, rows, params, out_dtypes, n_diff_rows=None, n_diff_params=None, ncol=1, bwd_fn=None):
    rows, params = list(rows), list(params)
    nr, npar = len(rows), len(params)
    ndr = nr if n_diff_rows is None else n_diff_rows
    ndp = npar if n_diff_params is None else n_diff_params
    S = rows[0].shape[0]
    rw = [r.shape[1] // ncol for r in rows]
    pshape = [(p.shape[0], p.shape[1] // ncol) for p in params]

    def block_structs(tm):
        return ([jax.ShapeDtypeStruct((tm, w), f32) for w in rw] + [jax.ShapeDtypeStruct(s, f32) for s in pshape])

    outs_s = jax.eval_shape(fn, *block_structs(8))
    ow = [o.shape[1] for o in outs_s]
    nout = len(ow)
    row_bytes = 4 * (sum(rw) * 2 + sum(ow) * 2)
    tm = _row_tile(S, row_bytes)
    grid = (ncol, S // tm)

    def rspec(w):
        return pl.BlockSpec((tm, w), lambda g, i: (i, g))

    def pspec(s):
        return pl.BlockSpec(s, lambda g, i: (0, g))

    def call_fwd(*args):
        def body(*refs):
            vals = [r[...].astype(f32) for r in refs[:nr + npar]]
            res = fn(*vals)
            for o, r in zip(refs[nr + npar:], res):
                o[...] = r.astype(o.dtype)

        return pl.pallas_call(
            body,
            out_shape=[jax.ShapeDtypeStruct((S, w * ncol), dt) for w, dt in zip(ow, out_dtypes)],
            grid=grid,
            in_specs=[rspec(w) for w in rw] + [pspec(s) for s in pshape],
            out_specs=[rspec(w) for w in ow],
            compiler_params=_cparams(("parallel", "parallel")),
            name=name + "_fwd",
        )(*args)

    def call_bwd(args, douts):
        def body(*refs):
            in_refs = refs[:nr + npar]
            do_refs = refs[nr + npar:nr + npar + nout]
            dr_refs = refs[nr + npar + nout:nr + npar + nout + ndr]
            dp_refs = refs[nr + npar + nout + ndr:]
            rv = [r[...] for r in in_refs[:nr]]
            pv = [r[...] for r in in_refs[nr:]]
            dos = [d[...].astype(f32) for d in do_refs]
            if bwd_fn is not None:
                drs, dps = bwd_fn(rv, pv, dos)
            else:
                def f(*a):
                    return fn(*a[:ndr], *rv[ndr:], *a[ndr:], *pv[ndp:])

                _, vjp = jax.vjp(f, *[v.astype(f32) for v in rv[:ndr]], *pv[:ndp])
                cts = vjp(tuple(dos))
                drs, dps = cts[:ndr], cts[ndr:]
            for r, ct in zip(dr_refs, drs):
                r[...] = ct.astype(r.dtype)
            if ndp:
                @pl.when(pl.program_id(1) == 0)
                def _():
                    for r in dp_refs:
                        r[...] = jnp.zeros_like(r)

                for r, ct in zip(dp_refs, dps):
                    r[...] += ct

        return pl.pallas_call(
            body,
            out_shape=[jax.ShapeDtypeStruct(r.shape, r.dtype) for r in rows[:ndr]]
            + [jax.ShapeDtypeStruct(p.shape, f32) for p in params[:ndp]],
            grid=grid,
            in_specs=[rspec(w) for w in rw] + [pspec(s) for s in pshape] + [rspec(w) for w in ow],
            out_specs=[rspec(w) for w in rw[:ndr]] + [pspec(s) for s in pshape[:ndp]],
            compiler_params=_cparams(("parallel", "arbitrary")),
            name=name + "_bwd",
        )(*args, *douts)

    @jax.custom_vjp
    def op(*args):
        return tuple(call_fwd(*args))

    def fwd(*args):
        return tuple(call_fwd(*args)), args

    def bwd(args, douts):
        res = call_bwd(args, douts)
        drs, dps = res[:ndr], res[ndr:]
        out = list(drs) + [jnp.zeros_like(a) for a in args[ndr:nr]]
        out += [dp.astype(p.dtype) for dp, p in zip(dps, args[nr:nr + ndp])]
        out += [jnp.zeros_like(a) for a in args[nr + ndp:]]
        return tuple(out)

    op.defvjp(fwd, bwd)
    return op(*rows, *params)


def _silu(x):
    return x * (1.0 / (1.0 + jnp.exp(-x)))


def _softplus(x):
    return jnp.maximum(x, 0.0) + jnp.log(1.0 + jnp.exp(-jnp.abs(x)))


def _gelu_tanh(x):
    return 0.5 * x * (1.0 + jnp.tanh(math.sqrt(2.0 / math.pi) * (x + 0.044715 * (x * x * x))))


def _rms_fn(x, g):
    return x * lax.rsqrt(jnp.mean(x * x, axis=-1, keepdims=True) + EPS) * g


def _rms_bwd(x, g, dy):
    r = lax.rsqrt(jnp.mean(x * x, axis=-1, keepdims=True) + EPS)
    xh = x * r
    dxh = dy * g
    dx = r * (dxh - xh * jnp.mean(dxh * xh, axis=-1, keepdims=True))
    return dx, jnp.sum(dy * xh, axis=0, keepdims=True)


def rms(x, g, out_dtype):
    def bwd_fn(rv, pv, dos):
        dx, dg = _rms_bwd(rv[0], pv[0], dos[0])
        return (dx,), (dg,)

    return rowwise(lambda x, g: (_rms_fn(x, g),), "rms", [x], [g.reshape(1, -1)], [out_dtype], bwd_fn=bwd_fn)[0]


def rms_residual_norm(m, g, x, g_next):
    def fn(m, x, g, gn):
        xn = x + _rms_fn(m, g)
        return xn, _rms_fn(xn, gn)

    def bwd_fn(rv, pv, dos):
        (m, x), (g, gn), (dxn, dhn) = rv, pv, dos
        xn = x + _rms_fn(m, g)
        d_from_norm, dgn = _rms_bwd(xn, gn, dhn)
        dxn = dxn + d_from_norm
        dm, dg = _rms_bwd(m, g, dxn)
        return (dm, dxn), (dg, dgn)

    return rowwise(fn, "rms_res_norm", [m, x], [g.reshape(1, -1), g_next.reshape(1, -1)], [f32, MXU_DTYPE], bwd_fn=bwd_fn)


def rms_residual(m, g, x):
    def bwd_fn(rv, pv, dos):
        dm, dg = _rms_bwd(rv[0], pv[0], dos[0])
        return (dm, dos[0]), (dg,)

    return rowwise(lambda m, x, g: (x + _rms_fn(m, g),), "rms_res", [m, x], [g.reshape(1, -1)], [f32], bwd_fn=bwd_fn)[0]


def loss_op(y, tgt):
    S, D = y.shape
    tm = _row_tile(S, 4 * D * 4)

    def call_fwd(y, tgt):
        def body(y_ref, t_ref, o_ref):
            @pl.when(pl.program_id(0) == 0)
            def _():
                o_ref[...] = jnp.zeros_like(o_ref)

            e = y_ref[...] - t_ref[...]
            o_ref[...] += 0.5 * jnp.sum(jnp.mean(e * e, axis=-1, keepdims=True))

        out = pl.pallas_call(
            body,
            out_shape=jax.ShapeDtypeStruct((8, LANES), f32),
            grid=(S // tm,),
            in_specs=[pl.BlockSpec((tm, D), lambda i: (i, 0))] * 2,
            out_specs=pl.BlockSpec((8, LANES), lambda i: (0, 0)),
            compiler_params=_cparams(("arbitrary",)),
            name="loss_fwd",
        )(y, tgt)
        return out[0, 0]

    def call_bwd(y, tgt, g):
        def body(y_ref, t_ref, g_ref, o_ref):
            o_ref[...] = (y_ref[...] - t_ref[...]) * (g_ref[...] * (1.0 / D))

        return pl.pallas_call(
            body,
            out_shape=jax.ShapeDtypeStruct((S, D), f32),
            grid=(S // tm,),
            in_specs=[pl.BlockSpec((tm, D), lambda i: (i, 0))] * 2 + [pl.BlockSpec((1, 1), lambda i: (0, 0))],
            out_specs=pl.BlockSpec((tm, D), lambda i: (i, 0)),
            compiler_params=_cparams(("parallel",)),
            name="loss_bwd",
        )(y, tgt, g.reshape(1, 1).astype(f32))

    @jax.custom_vjp
    def op(y, tgt):
        return call_fwd(y, tgt)

    def fwd(y, tgt):
        return call_fwd(y, tgt), (y, tgt)

    def bwd(res, g):
        y, tgt = res
        return call_bwd(y, tgt, g), jnp.zeros_like(tgt)

    op.defvjp(fwd, bwd)
    return op(y, tgt)


HALO = 8


def _conv_tile(S, R):
    def ext(ref, r0):
        cur = ref[pl.ds(r0, R), :]
        prev = ref[pl.ds(pl.multiple_of(jnp.maximum(r0 - HALO, 0), HALO), HALO), :]
        nxt = ref[pl.ds(pl.multiple_of(jnp.minimum(r0 + R, S - HALO), HALO), HALO), :]
        prev = jnp.where(r0 > 0, prev, 0.0)
        nxt = jnp.where(r0 + R < S, nxt, 0.0)
        return jnp.concatenate([prev, cur, nxt], axis=0)

    return ext


def _shift_rows(e, k, R):
    n = e.shape[0]
    if k == 0:
        return e[HALO:HALO + R]
    return pltpu.roll(e, (-k) % n, 0)[HALO:HALO + R]


def _silu_bwd(us, dy):
    u, = us
    s = 1.0 / (1.0 + jnp.exp(-u))
    return (dy * (s * (1.0 + u * (1.0 - s))),)


def _geglu(g, v):
    return _gelu_tanh(g) * v


def _geglu_bwd(us, dy):
    g, v = us
    c = math.sqrt(2.0 / math.pi)
    t = jnp.tanh(c * (g + 0.044715 * (g * g * g)))
    half = 0.5 * (1.0 + t)
    dgelu = half + 0.5 * g * (1.0 - t * t) * (c * (1.0 + 3.0 * 0.044715 * (g * g)))
    return dy * v * dgelu, dy * (g * half)


def mm_conv_act(h, ws, cws, cbs, act, act_bwd, out_dtype, name):
    n = len(ws)
    S = h.shape[0]
    C = ws[0].shape[1]
    W = cws[0].shape[0]
    pad = W // 2
    bw = _pick(C, (LANES,))
    R = _pick(S, (256, 128, 64, 32, 16, 8))
    nt = S // R
    ext = _conv_tile(S, R)
    col = lambda rows: pl.BlockSpec((rows, bw), lambda j: (0, j))

    def conv(e, wv, bv):
        acc = bv + wv[pad] * e[HALO:HALO + R]
        for j in range(W):
            if j != pad:
                acc = acc + wv[j] * _shift_rows(e, j - pad, R)
        return acc

    def call_fwd(xs, cws, cbs):
        def body(*refs):
            x_refs, w_refs, b_refs, y_ref = refs[:n], refs[n:2 * n], refs[2 * n:3 * n], refs[3 * n]
            wvs = [[w[j:j + 1, :] for j in range(W)] for w in w_refs]
            bvs = [b[...] for b in b_refs]

            def tile(i, c):
                r0 = pl.multiple_of(i * R, R)
                us = [conv(ext(x, r0), wv, bv) for x, wv, bv in zip(x_refs, wvs, bvs)]
                y_ref[pl.ds(r0, R), :] = act(*us).astype(y_ref.dtype)
                return c

            lax.fori_loop(0, nt, tile, 0)

        return pl.pallas_call(
            body,
            out_shape=jax.ShapeDtypeStruct((S, C), out_dtype),
            grid=(C // bw,),
            in_specs=[col(S)] * n + [col(W)] * n + [col(1)] * n,
            out_specs=col(S),
            compiler_params=_cparams(("parallel",)),
            name=name + "_fwd",
        )(*xs, *cws, *cbs)

    def call_bwd(xs, cws, cbs, dy):
        def body(*refs):
            x_refs, w_refs, b_refs, dy_ref = refs[:n], refs[n:2 * n], refs[2 * n:3 * n], refs[3 * n]
            dx_refs, dw_refs, db_refs = refs[3 * n + 1:4 * n + 1], refs[4 * n + 1:5 * n + 1], refs[5 * n + 1:6 * n + 1]
            du_scr = refs[6 * n + 1:]
            wvs = [[w[j:j + 1, :] for j in range(W)] for w in w_refs]
            bvs = [b[...] for b in b_refs]
            zero = jnp.zeros((1, bw), f32)

            def first(i, dbs):
                r0 = pl.multiple_of(i * R, R)
                us = [conv(ext(x, r0), wv, bv) for x, wv, bv in zip(x_refs, wvs, bvs)]
                dus = act_bwd(us, dy_ref[pl.ds(r0, R), :].astype(f32))
                for scr, du in zip(du_scr, dus):
                    scr[pl.ds(r0, R), :] = du
                return tuple(db + jnp.sum(du, axis=0, keepdims=True) for db, du in zip(dbs, dus))

            dbs = lax.fori_loop(0, nt, first, tuple(zero for _ in range(n)))

            def second(i, dws):
                r0 = pl.multiple_of(i * R, R)
                new = []
                for x, scr, dx, wv, dw in zip(x_refs, du_scr, dx_refs, wvs, dws):
                    ex, ed = ext(x, r0), ext(scr, r0)
                    d0 = ed[HALO:HALO + R]
                    acc = jnp.zeros((R, bw), f32)
                    row = []
                    for j in range(W):
                        acc = acc + wv[j] * _shift_rows(ed, pad - j, R)
                        row.append(dw[j] + jnp.sum(d0 * _shift_rows(ex, j - pad, R), axis=0, keepdims=True))
                    dx[pl.ds(r0, R), :] = acc.astype(dx.dtype)
                    new.append(tuple(row))
                return tuple(new)

            dws = lax.fori_loop(0, nt, second, tuple(tuple(zero for _ in range(W)) for _ in range(n)))
            for dw_ref, db_ref, dw, db in zip(dw_refs, db_refs, dws, dbs):
                dw_ref[...] = jnp.zeros_like(dw_ref)
                for j in range(W):
                    dw_ref[j:j + 1, :] = dw[j]
                db_ref[...] = db

        return pl.pallas_call(
            body,
            out_shape=[jax.ShapeDtypeStruct((S, C), MXU_DTYPE)] * n + [jax.ShapeDtypeStruct((8, C), f32)] * n
            + [jax.ShapeDtypeStruct((1, C), f32)] * n,
            grid=(C // bw,),
            in_specs=[col(S)] * n + [col(W)] * n + [col(1)] * n + [col(S)],
            out_specs=[col(S)] * n + [col(8)] * n + [col(1)] * n,
            scratch_shapes=[pltpu.VMEM((S, bw), f32)] * n,
            compiler_params=_cparams(("parallel",)),
            name=name + "_bwd",
        )(*xs, *cws, *cbs, dy)

    @jax.custom_vjp
    def op(h, ws, cws, cbs):
        return call_fwd([_mm_call(h, w, "nn", f32) for w in ws], cws, cbs)

    def fwd(h, ws, cws, cbs):
        xs = [_mm_call(h, w, "nn", f32) for w in ws]
        return call_fwd(xs, cws, cbs), (h, ws, xs, cws, cbs)

    def bwd(res, dy):
        h, ws, xs, cws, cbs = res
        out = call_bwd(xs, cws, cbs, dy)
        dxs, dcws, dcbs = out[:n], out[n:2 * n], out[2 * n:]
        dh = _mm_call(dxs[0], ws[0], "nt", h.dtype)
        for dx, w in zip(dxs[1:], ws[1:]):
            dh = dh + _mm_call(dx, w, "nt", h.dtype)
        dws = tuple(_mm_call(h, dx, "tn", w.dtype) for dx, w in zip(dxs, ws))
        return dh, dws, tuple(d[:W] for d in dcws), tuple(dcbs)

    op.defvjp(fwd, bwd)
    return op(h, tuple(ws), tuple(cws), tuple(b.reshape(1, C) for b in cbs))


@jax.custom_vjp
def _masked_decay(cs_col, cs_row, mask01):
    return jnp.where(mask01 > 0, jnp.exp(cs_col - cs_row), 0.0)


def _masked_decay_fwd(cs_col, cs_row, mask01):
    d = jnp.where(mask01 > 0, jnp.exp(cs_col - cs_row), 0.0)
    return d, (d, mask01)


def _masked_decay_bwd(res, g):
    d, mask01 = res
    t = g * d
    return jnp.sum(t, axis=1, keepdims=True), -jnp.sum(t, axis=0, keepdims=True), jnp.zeros_like(mask01)


_masked_decay.defvjp(_masked_decay_fwd, _masked_decay_bwd)


def _scan_chunk(qs, ks, xs, cs_tok, dt_tok, hs, *, rev, incl, nsub):
    nb = len(xs)
    L, N = qs[0].shape
    W = xs[0].shape[1]
    Hg = cs_tok.shape[1]
    nh = W // HEAD_W
    shared = len(qs) == 1
    t = lax.broadcasted_iota(jnp.int32, (L, L), 0)
    l = lax.broadcasted_iota(jnp.int32, (L, L), 1)
    if rev:
        mask = (l >= t) if incl else (l > t)
    else:
        mask = (l <= t) if incl else (l < t)
    mask01 = mask.astype(f32)
    lane_a = lax.broadcasted_iota(jnp.int32, cs_tok.shape, 1)
    row_a = lax.broadcasted_iota(jnp.int32, (Hg, L), 0)
    last = lax.broadcasted_iota(jnp.int32, (1, L), 1) == (0 if rev else L - 1)
    vhead = lax.broadcasted_iota(jnp.int32, (1, W), 1) // HEAD_W
    qhead = lax.broadcasted_iota(jnp.int32, (1, N), 1) // (N // nsub)
    cs_rows = lax.dot_general(cs_tok, (t == l).astype(f32), _DIMS["tn"], precision=lax.Precision.HIGHEST,
                              preferred_element_type=f32)

    decay, lam_e, tau_e, gam_e, dt_e = [], [0.0] * nb, [0.0] * nb, [0.0] * nb, [0.0] * nb
    for b in range(nb):
        for i in range(nh):
            head = b * nh + i
            cs_col = jnp.sum(jnp.where(lane_a == head, cs_tok, 0.0), axis=1, keepdims=True)
            cs_row = jnp.sum(jnp.where(row_a == head, cs_rows, 0.0), axis=0, keepdims=True)
            tot = jnp.sum(jnp.where(last, cs_row, 0.0), axis=1, keepdims=True)
            decay.append(_masked_decay(cs_col, cs_row, mask01))
            sel = vhead == i
            lam_e[b] = lam_e[b] + jnp.where(sel, jnp.exp(cs_col), 0.0)
            tau_e[b] = tau_e[b] + jnp.where(sel, jnp.exp(tot - cs_col), 0.0)
            gam_e[b] = gam_e[b] + jnp.where(sel, jnp.exp(tot), 0.0)
            if dt_tok is not None:
                dt_col = jnp.sum(jnp.where(lane_a == head, dt_tok, 0.0), axis=1, keepdims=True)
                dt_e[b] = dt_e[b] + jnp.where(sel, dt_col, 0.0)
    vs = [x if dt_tok is None else x * dt_e[b] for b, x in enumerate(xs)]
    qk = lambda b: (qs[0], ks[0]) if shared else (qs[b], ks[b])
    if nsub == 1:
        scores = [bdot(qs[0], ks[0], "nt")] if shared else [bdot(*qk(b), "nt") for b in range(nb)]
        score = lambda b, i: scores[0 if shared else b]
    else:
        scores = [[bdot(jnp.where(qhead == i, qk(b)[0], 0.0), qk(b)[1], "nt") for i in range(nh)] for b in range(nb)]
        score = lambda b, i: scores[b][i]
    ys = [lam_e[b] * bdot(qk(b)[0], hs[b], "nn") for b in range(nb)]
    for b in range(nb):
        for i in range(nh):
            ys[b] = ys[b] + jnp.where(vhead == i, bdot(score(b, i) * decay[b * nh + i], vs[b], "nn"), 0.0)
    hns = [gam_e[b] * hs[b] + bdot(qk(b)[1], tau_e[b] * vs[b], "tn") for b in range(nb)]
    if nsub > 1:
        nhead = lax.broadcasted_iota(jnp.int32, (N, W), 0) // (N // nsub)
        keep = nhead == lax.broadcasted_iota(jnp.int32, (N, W), 1) // HEAD_W
        hns = [jnp.where(keep, hn, 0.0) for hn in hns]
    return ys, hns


def chunk_cumsum(a_tok, rev):
    G, S, Hg = a_tok.shape
    L = CHUNK
    CB = _pick(S // L, (16, 8, 4, 2))

    def call(a, rev):
        def body(a_ref, o_ref):
            t = lax.broadcasted_iota(jnp.int32, (L, L), 0)
            l = lax.broadcasted_iota(jnp.int32, (L, L), 1)
            tri = ((l >= t) if rev else (l <= t)).astype(f32)
            for j in range(CB):
                o_ref[0, j * L:(j + 1) * L, :] = _exact_dot(tri, a_ref[0, j * L:(j + 1) * L, :])

        spec = pl.BlockSpec((1, CB * L, Hg), lambda g, c: (g, c, 0))
        return pl.pallas_call(
            body,
            out_shape=jax.ShapeDtypeStruct((G, S, Hg), f32),
            grid=(G, S // (CB * L)),
            in_specs=[spec],
            out_specs=spec,
            compiler_params=_cparams(("parallel", "parallel")),
            name="chunk_cumsum",
        )(a)

    @jax.custom_vjp
    def op(a):
        return call(a, rev)

    def fwd(a):
        return call(a, rev), None

    def bwd(_, g):
        return (call(g, not rev),)

    op.defvjp(fwd, bwd)
    return op(a_tok)


def scan_op(q, k, x, a_tok, dt_tok, *, rev, incl, nsub):
    S = q.shape[0]
    G, _, Hg = a_tok.shape
    N = q.shape[1] // G
    Vw = x.shape[1] // G
    L = CHUNK
    nc = S // L
    use_dt = dt_tok is not None
    PW = min(Vw, LANES)
    chunk = functools.partial(_scan_chunk, rev=rev, incl=incl, nsub=nsub)
    cols = [slice(p * PW, (p + 1) * PW) for p in range(Vw // PW)]
    own_qk = nsub > 1
    NB = PW if own_qk else N

    def order(c, backward):
        return (nc - 1 - c) if (rev != backward) else c

    def specs(backward):
        qs = pl.BlockSpec((L, N), lambda g, c: (order(c, backward), g))
        xs = pl.BlockSpec((L, Vw), lambda g, c: (order(c, backward), g))
        as_ = pl.BlockSpec((1, L, Hg), lambda g, c: (g, order(c, backward), 0))
        hs = pl.BlockSpec((1, 1, NB, Vw), lambda g, c: (g, order(c, backward), 0, 0))
        return qs, xs, as_, hs

    def call_fwd(q, k, x, a_tok, dt_tok, y_prev=None):
        qs, xs, as_, hs = specs(False)
        n_in = 4 + use_dt + (y_prev is not None)

        def body(*refs):
            q_ref, k_ref, x_ref, a_ref = refs[:4]
            dt_ref = refs[4] if use_dt else None
            yp_ref = refs[n_in - 1] if y_prev is not None else None
            y_ref, hs_ref, h_scr = refs[n_in:]

            @pl.when(pl.program_id(1) == 0)
            def _():
                h_scr[...] = jnp.zeros_like(h_scr)

            hs_ref[0, 0] = h_scr[...]
            q, k, a, dt = q_ref[...], k_ref[...], a_ref[0], dt_ref[0] if use_dt else None
            qs, ks = ([q[:, c] for c in cols], [k[:, c] for c in cols]) if own_qk else ([q], [k])
            ys, hns = chunk(qs, ks, [x_ref[:, c] for c in cols], a, dt, [h_scr[:, c] for c in cols])
            for c, y, hn in zip(cols, ys, hns):
                y_ref[:, c] = y if yp_ref is None else y + yp_ref[:, c]
                h_scr[:, c] = hn

        ins = [q, k, x, a_tok] + ([dt_tok] if use_dt else []) + ([y_prev] if y_prev is not None else [])
        return pl.pallas_call(
            body,
            out_shape=[jax.ShapeDtypeStruct((S, G * Vw), f32), jax.ShapeDtypeStruct((G, nc, NB, Vw), f32)],
            grid=(G, nc),
            in_specs=[qs, qs, xs, as_] + ([as_] if use_dt else []) + ([xs] if y_prev is not None else []),
            out_specs=[xs, hs],
            scratch_shapes=[pltpu.VMEM((NB, Vw), f32)],
            compiler_params=_cparams(("parallel", "arbitrary")),
            name="scan_fwd",
        )(*ins)

    def call_bwd(q, k, x, a_tok, dt_tok, hsave, dy, acc=None):
        qs, xs, as_, hs = specs(True)
        n_in = 6 + use_dt + (3 if acc is not None else 0)

        def body(*refs):
            q_ref, k_ref, x_ref, a_ref = refs[:4]
            dt_ref = refs[4] if use_dt else None
            hs_ref, dy_ref = refs[4 + use_dt], refs[5 + use_dt]
            acc_refs = refs[n_in - 3:n_in] if acc is not None else None
            dq_ref, dk_ref, dx_ref, da_ref = refs[n_in:n_in + 4]
            ddt_ref = refs[n_in + 4] if use_dt else None
            dh_scr = refs[-1]

            @pl.when(pl.program_id(1) == 0)
            def _():
                dh_scr[...] = jnp.zeros_like(dh_scr)

            q, k, a = q_ref[...].astype(f32), k_ref[...].astype(f32), a_ref[0]
            qs, ks = ([q[:, c] for c in cols], [k[:, c] for c in cols]) if own_qk else ([q], [k])
            xs, hs_in = [x_ref[:, c] for c in cols], [hs_ref[0, 0, :, c] for c in cols]
            if use_dt:
                _, vjp = jax.vjp(chunk, qs, ks, xs, a, dt_ref[0], hs_in)
            else:
                _, vjp = jax.vjp(lambda qs, ks, xs, a, hs: chunk(qs, ks, xs, a, None, hs), qs, ks, xs, a, hs_in)
            cts = vjp(([dy_ref[:, c] for c in cols], [dh_scr[:, c] for c in cols]))
            dqs, dks, dxs, da, dhs = cts[0], cts[1], cts[2], cts[3], cts[-1]
            if acc is not None:
                dq_acc, dk_acc = acc_refs[0][...].astype(f32), acc_refs[1][...].astype(f32)
            if own_qk:
                for b, c in enumerate(cols):
                    dq_ref[:, c] = (dqs[b] if acc is None else dqs[b] + dq_acc[:, c]).astype(dq_ref.dtype)
                    dk_ref[:, c] = (dks[b] if acc is None else dks[b] + dk_acc[:, c]).astype(dk_ref.dtype)
            else:
                dq_ref[...] = (dqs[0] if acc is None else dqs[0] + dq_acc).astype(dq_ref.dtype)
                dk_ref[...] = (dks[0] if acc is None else dks[0] + dk_acc).astype(dk_ref.dtype)
            for b, c in enumerate(cols):
                dx_ref[:, c] = dxs[b] if acc is None else dxs[b] + acc_refs[2][:, c]
                dh_scr[:, c] = dhs[b]
            da_ref[0] = da
            if use_dt:
                ddt_ref[0] = cts[4]

        ins = [q, k, x, a_tok] + ([dt_tok] if use_dt else []) + [hsave, dy] + (list(acc) if acc is not None else [])
        a_shape = jax.ShapeDtypeStruct(a_tok.shape, f32)
        return pl.pallas_call(
            body,
            out_shape=[jax.ShapeDtypeStruct(q.shape, q.dtype), jax.ShapeDtypeStruct(k.shape, k.dtype),
                       jax.ShapeDtypeStruct(x.shape, f32), a_shape] + ([a_shape] if use_dt else []),
            grid=(G, nc),
            in_specs=[qs, qs, xs, as_] + ([as_] if use_dt else []) + [hs, xs] + ([qs, qs, xs] if acc is not None else []),
            out_specs=[qs, qs, xs, as_] + ([as_] if use_dt else []),
            scratch_shapes=[pltpu.VMEM((NB, Vw), f32)],
            compiler_params=_cparams(("parallel", "arbitrary")),
            name="scan_bwd",
        )(*ins)

    return call_fwd, call_bwd


def bidir_scan(q, k, x, a_f, a_b, dt_f, dt_b, *, nsub):
    use_dt = dt_f is not None
    a_f, a_b = chunk_cumsum(a_f, False), chunk_cumsum(a_b, True)
    fwd_f, bwd_f = scan_op(q, k, x, a_f, dt_f, rev=False, incl=True, nsub=nsub)
    fwd_b, bwd_b = scan_op(q, k, x, a_b, dt_b, rev=True, incl=False, nsub=nsub)

    def run(q, k, x, a_f, a_b, dt_f, dt_b):
        y_f, hs_f = fwd_f(q, k, x, a_f, dt_f)
        y, hs_b = fwd_b(q, k, x, a_b, dt_b, y_prev=y_f)
        return y, (hs_f, hs_b)

    def grads(q, k, x, a_f, a_b, dt_f, dt_b, hs, dy):
        first = bwd_f(q, k, x, a_f, dt_f, hs[0], dy)
        both = bwd_b(q, k, x, a_b, dt_b, hs[1], dy, acc=first[:3])
        return both[0], both[1], both[2], first[3], both[3], (first[4] if use_dt else None), (both[4] if use_dt else None)

    if use_dt:
        @jax.custom_vjp
        def op(q, k, x, a_f, a_b, dt_f, dt_b):
            return run(q, k, x, a_f, a_b, dt_f, dt_b)[0]

        def fwd(q, k, x, a_f, a_b, dt_f, dt_b):
            y, hs = run(q, k, x, a_f, a_b, dt_f, dt_b)
            return y, (q, k, x, a_f, a_b, dt_f, dt_b, hs)

        def bwd(res, dy):
            return grads(*res, dy)

        op.defvjp(fwd, bwd)
        return op(q, k, x, a_f, a_b, dt_f, dt_b)

    @jax.custom_vjp
    def op(q, k, x, a_f, a_b):
        return run(q, k, x, a_f, a_b, None, None)[0]

    def fwd(q, k, x, a_f, a_b):
        y, hs = run(q, k, x, a_f, a_b, None, None)
        return y, (q, k, x, a_f, a_b, hs)

    def bwd(res, dy):
        q, k, x, a_f, a_b, hs = res
        return grads(q, k, x, a_f, a_b, None, None, hs, dy)[:5]

    op.defvjp(fwd, bwd)
    return op(q, k, x, a_f, a_b)


def _swap_halves(x, dh):
    W = x.shape[1]
    lane = lax.broadcasted_iota(jnp.int32, (1, W), 1) % dh
    return jnp.where(lane < dh // 2, pltpu.roll(x, W - dh // 2, 1), pltpu.roll(x, dh // 2, 1))


def rotary(rq, rk, cos_t, sin_t):
    scale = RET_DH ** -0.5

    def fn(rq, rk, c, s):
        return rq * c + _swap_halves(rq, RET_DH) * s, (rk * c + _swap_halves(rk, RET_DH) * s) * scale

    def bwd_fn(rv, pv, dos):
        _, _, c, s = rv
        dq, dk = dos
        dk = dk * scale
        return (dq * c + _swap_halves(dq * s, RET_DH), dk * c + _swap_halves(dk * s, RET_DH)), ()

    return rowwise(fn, "rotary", [rq, rk, cos_t, sin_t], [], [MXU_DTYPE, MXU_DTYPE], n_diff_rows=2, bwd_fn=bwd_fn)


def _rope_tables(S, width):
    half = RET_DH // 2
    inv = 1.0 / (ROPE_BASE ** (jnp.arange(half, dtype=f32) / half))
    ang = jnp.arange(S, dtype=f32)[:, None] * inv[None, :]
    cos, sin = jnp.cos(ang), jnp.sin(ang)
    reps = width // RET_DH
    return jnp.tile(jnp.concatenate([cos, cos], axis=1), (1, reps)), jnp.tile(jnp.concatenate([-sin, sin], axis=1), (1, reps))


def _exact_dot(x, m):
    return jnp.dot(x, m, precision=lax.Precision.HIGHEST, preferred_element_type=f32)


def ret_post(y, rg, gn_g):
    W = y.shape[1]
    idx = np.arange(W) // RET_DH
    avg = jnp.asarray((idx[:, None] == idx[None, :]).astype(np.float32) / RET_DH)

    def fn(y, rg, g, avg):
        mu = _exact_dot(y, avg)
        d = y - mu
        var = _exact_dot(d * d, avg)
        return (_silu(rg) * (d * lax.rsqrt(var + EPS) * g),)

    return rowwise(fn, "ret_post", [y, rg], [gn_g.reshape(1, -1), avg], [MXU_DTYPE], n_diff_params=1)[0]


def _na_bias(rpb, win_r):
    H = rpb.shape[0]
    qc = np.arange(GRID_W)[:, None]
    kc = np.arange(GRID_W)[None, :]
    cstart = np.clip(qc - NA_WIN_C // 2, 0, GRID_W - NA_WIN_C)
    valid = (kc >= cstart) & (kc < cstart + NA_WIN_C)
    dc = np.clip(kc - qc, -(NA_WIN_C - 1), NA_WIN_C - 1) + (NA_WIN_C - 1)
    onehot = (dc[None] == np.arange(2 * NA_WIN_C - 1)[:, None, None]).astype(np.float32)
    t1 = jnp.einsum("hrd,dqk->hrqk", rpb.astype(f32), jnp.asarray(onehot), precision=lax.Precision.HIGHEST)
    per_delta = [t1[:, NA_WIN_R - 1 - d:NA_WIN_R - 1 - d + win_r] for d in range(win_r)]
    b = jnp.stack(per_delta, axis=1)
    b = jnp.where(jnp.asarray(valid)[None, None, None], b, NEG_INF)
    return jnp.transpose(b, (0, 1, 3, 2, 4)).reshape(H, win_r, GRID_W, win_r * GRID_W)


def _na_rows(rows):
    lane = lax.broadcasted_iota(jnp.int32, (1, rows[0][0].shape[1]), 1) // NA_DH
    scale = NA_DH ** -0.5
    ss = [[_bdot_raw(jnp.where(lane == i, q, 0.0) * scale, kw, "nt") + b for i, b in enumerate(bs)] for q, kw, _, bs in rows]
    es = [[jnp.exp(s - jnp.max(s, axis=1, keepdims=True)) for s in srow] for srow in ss]
    ps = [[e / jnp.sum(e, axis=1, keepdims=True) for e in erow] for erow in es]
    outs = []
    for prow, (_, _, vw, _) in zip(ps, rows):
        o = 0.0
        for i, p in enumerate(prow):
            o = o + jnp.where(lane == i, _bdot_raw(p, vw, "nn"), 0.0)
        outs.append(o)
    return outs


def _na_rows_bwd(rows):
    lane = lax.broadcasted_iota(jnp.int32, (1, rows[0][0].shape[1]), 1) // NA_DH
    scale = NA_DH ** -0.5
    heads = range(len(rows[0][3]))
    qis = [[jnp.where(lane == i, q, 0.0) * scale for i in heads] for q, _, _, _, _ in rows]
    dos = [[jnp.where(lane == i, do, 0.0) for i in heads] for _, _, _, _, do in rows]
    ss = [[_bdot_raw(qi, kw, "nt") + b for qi, b in zip(qrow, bs)] for qrow, (_, kw, _, bs, _) in zip(qis, rows)]
    dps = [[_bdot_raw(doi, vw, "nt") for doi in drow] for drow, (_, _, vw, _, _) in zip(dos, rows)]
    es = [[jnp.exp(s - jnp.max(s, axis=1, keepdims=True)) for s in srow] for srow in ss]
    ps = [[e / jnp.sum(e, axis=1, keepdims=True) for e in erow] for erow in es]
    dss = [[p * (dp - jnp.sum(dp * p, axis=1, keepdims=True)) for p, dp in zip(prow, dprow)] for prow, dprow in zip(ps, dps)]
    out = []
    for qrow, drow, prow, dsrow, (_, kw, _, _, _) in zip(qis, dos, ps, dss, rows):
        dq, dk, dv = 0.0, 0.0, 0.0
        for i in heads:
            dq = dq + jnp.where(lane == i, _bdot_raw(dsrow[i], kw, "nn"), 0.0) * scale
            dk = dk + _bdot_raw(dsrow[i], qrow[i], "tn")
            dv = dv + _bdot_raw(prow[i], drow[i], "tn")
        out.append((dq, dk, dv, dsrow))
    return out


def na_op(nq, nk, nv, bias):
    S, W = nq.shape
    rows = S // GRID_W
    win_r = bias.shape[1]
    nkeys = win_r * GRID_W
    hp = LANES // NA_DH
    npair = W // LANES
    RB = min(16, rows)
    nrb = rows // RB
    qspec = pl.BlockSpec((RB * GRID_W, LANES), lambda p, r: (r, p))
    kspec = pl.BlockSpec((S, LANES), lambda p, r: (0, p))
    bspec = pl.BlockSpec((hp, win_r, GRID_W, nkeys), lambda p, r: (p, 0, 0, 0))

    def window(r):
        r0 = jnp.clip(r - win_r // 2, 0, rows - win_r)
        return pl.multiple_of(r0 * GRID_W, GRID_W), r - r0

    def call_fwd(nq, nk, nv, bias):
        def body(q_ref, k_ref, v_ref, b_ref, o_ref):
            rb = pl.program_id(1)

            def step(j, c):
                args, q0s = [], []
                for u in range(NA_ROWS_PER_STEP):
                    i = j * NA_ROWS_PER_STEP + u
                    k0, d = window(rb * RB + i)
                    q0 = pl.multiple_of(i * GRID_W, GRID_W)
                    q0s.append(q0)
                    args.append((q_ref[pl.ds(q0, GRID_W), :].astype(f32), k_ref[pl.ds(k0, nkeys), :], v_ref[pl.ds(k0, nkeys), :],
                                 [b_ref[h, pl.ds(d, 1)][0] for h in range(hp)]))
                for q0, o in zip(q0s, _na_rows(args)):
                    o_ref[pl.ds(q0, GRID_W), :] = o.astype(o_ref.dtype)
                return c

            lax.fori_loop(0, RB // NA_ROWS_PER_STEP, step, 0)

        return pl.pallas_call(
            body,
            out_shape=jax.ShapeDtypeStruct((S, W), nq.dtype),
            grid=(npair, nrb),
            in_specs=[qspec, kspec, kspec, bspec],
            out_specs=qspec,
            compiler_params=_cparams(("parallel", "arbitrary")),
            name="na_fwd",
        )(nq, nk, nv, bias)

    def call_bwd(nq, nk, nv, bias, do):
        def body(q_ref, k_ref, v_ref, b_ref, do_ref, dq_ref, dk_ref, dv_ref, db_ref, dk_acc, dv_acc):
            rb = pl.program_id(1)

            @pl.when(rb == 0)
            def _():
                dk_acc[...] = jnp.zeros_like(dk_acc)
                dv_acc[...] = jnp.zeros_like(dv_acc)
                db_ref[...] = jnp.zeros_like(db_ref)

            def step(j, c):
                args, where = [], []
                for u in range(NA_ROWS_PER_STEP):
                    i = j * NA_ROWS_PER_STEP + u
                    k0, d = window(rb * RB + i)
                    q0 = pl.multiple_of(i * GRID_W, GRID_W)
                    where.append((q0, k0, d))
                    args.append((q_ref[pl.ds(q0, GRID_W), :].astype(f32), k_ref[pl.ds(k0, nkeys), :], v_ref[pl.ds(k0, nkeys), :],
                                 [b_ref[h, pl.ds(d, 1)][0] for h in range(hp)], do_ref[pl.ds(q0, GRID_W), :].astype(f32)))
                for (q0, k0, d), (dq, dk, dv, dbs) in zip(where, _na_rows_bwd(args)):
                    dq_ref[pl.ds(q0, GRID_W), :] = dq.astype(dq_ref.dtype)
                    dk_acc[pl.ds(k0, nkeys), :] += dk
                    dv_acc[pl.ds(k0, nkeys), :] += dv
                    for h in range(hp):
                        db_ref[h, pl.ds(d, 1)] += dbs[h][None]
                return c

            lax.fori_loop(0, RB // NA_ROWS_PER_STEP, step, 0)

            @pl.when(rb == nrb - 1)
            def _():
                dk_ref[...] = dk_acc[...].astype(dk_ref.dtype)
                dv_ref[...] = dv_acc[...].astype(dv_ref.dtype)

        return pl.pallas_call(
            body,
            out_shape=[jax.ShapeDtypeStruct((S, W), nq.dtype), jax.ShapeDtypeStruct((S, W), nk.dtype),
                       jax.ShapeDtypeStruct((S, W), nv.dtype), jax.ShapeDtypeStruct(bias.shape, f32)],
            grid=(npair, nrb),
            in_specs=[qspec, kspec, kspec, bspec, qspec],
            out_specs=[qspec, kspec, kspec, bspec],
            scratch_shapes=[pltpu.VMEM((S, LANES), f32), pltpu.VMEM((S, LANES), f32)],
            compiler_params=_cparams(("parallel", "arbitrary")),
            name="na_bwd",
        )(nq, nk, nv, bias, do)

    @jax.custom_vjp
    def op(nq, nk, nv, bias):
        return call_fwd(nq, nk, nv, bias)

    def fwd(nq, nk, nv, bias):
        return call_fwd(nq, nk, nv, bias), (nq, nk, nv, bias)

    def bwd(res, do):
        return tuple(call_bwd(*res, do))

    op.defvjp(fwd, bwd)
    return op(nq, nk, nv, bias)


def ssd_dt(dt_raw, dt_bias, a_neg):
    def fn(r, b, a):
        dt = _softplus(r + b)
        return dt, dt * a

    return rowwise(fn, "ssd_dt", [dt_raw], [dt_bias, a_neg], [f32, f32])


def ssd_post(y, xs, z, d_skip_lanes, norm_g, groups):
    def fn(y, xs, z, dsk, g):
        y = (y + xs * dsk) * _silu(z)
        return (y * lax.rsqrt(jnp.mean(y * y, axis=-1, keepdims=True) + EPS) * g,)

    return rowwise(fn, "ssd_post", [y, xs, z], [d_skip_lanes.reshape(1, -1), norm_g.reshape(1, -1)], [MXU_DTYPE],
                   ncol=groups)[0]


def _heads_major(t, groups):
    S = t.shape[0]
    return jnp.transpose(t.reshape(S, groups, -1), (1, 0, 2))


def retention_na_mixer(hn, w_in, decay_logit, gn_g, rpb, w_out, tables):
    S = hn.shape[0]
    R = RET_HEADS * RET_DH
    NW = NA_HEADS * NA_DH
    cols = lambda a, b: w_in[:, a:b]
    rq, rk, rv, rg = (mm(hn, cols(j * R, (j + 1) * R)) for j in range(4))
    nq, nk, nv = (mm(hn, cols(4 * R + j * NW, 4 * R + (j + 1) * NW), out_dtype=MXU_DTYPE) for j in range(3))
    qr, kr = rotary(rq, rk, *tables)
    log_gamma = -_softplus(-decay_logit.astype(f32))
    pairs = R // LANES
    hp = LANES // RET_DH
    hpad = -(-RET_HEADS // 8) * 8
    pad8 = lambda a: jnp.pad(a.reshape(1, 1, RET_HEADS), ((0, 0), (0, 0), (0, hpad - RET_HEADS)))
    a_f = jnp.broadcast_to(pad8(log_gamma[0]), (1, S, hpad))
    a_b = jnp.broadcast_to(pad8(log_gamma[1]), (1, S, hpad))
    ret = ret_post(bidir_scan(qr, kr, rv, a_f, a_b, None, None, nsub=hp), rg, gn_g)
    rows = S // GRID_W
    nao = na_op(nq, nk, nv, _na_bias(rpb, min(NA_WIN_R, rows)))
    return mm(ret, w_out[:R]) + mm(nao, w_out[R:])


def ssd_mixer(hn, w_in, conv_w, conv_b, dt_bias, a_log, d_skip, norm_g, w_out):
    heads = d_skip.shape[0]
    inner = heads * SSD_HEADDIM
    gs = SSD_GROUPS * SSD_STATE
    o_x, o_b, o_c, o_dt = inner, 2 * inner, 2 * inner + gs, 2 * inner + 2 * gs
    z = mm(hn, w_in[:, :inner])
    dt_raw = mm(hn, w_in[:, o_dt:])
    xs, bm, cm = (mm_conv_act(hn, [w_in[:, a:b]], [conv_w[:, a - inner:b - inner]], [conv_b[a - inner:b - inner]],
                              _silu, _silu_bwd, f32, "conv_silu") for a, b in ((o_x, o_b), (o_b, o_c), (o_c, o_dt)))
    a_neg = -jnp.exp(a_log.astype(f32)).reshape(1, -1)
    dt, la = ssd_dt(dt_raw, dt_bias.astype(f32).reshape(1, -1), a_neg)
    dt_f, dt_b = _heads_major(dt[:, :heads], SSD_GROUPS), _heads_major(dt[:, heads:], SSD_GROUPS)
    la_f, la_b = _heads_major(la[:, :heads], SSD_GROUPS), _heads_major(la[:, heads:], SSD_GROUPS)
    y = bidir_scan(cm, bm, xs, la_f, la_b, dt_f, dt_b, nsub=1)
    y = ssd_post(y, xs, z, jnp.repeat(d_skip.astype(f32), SSD_HEADDIM), norm_g, SSD_GROUPS)
    return mm(y, w_out)


def conv_geglu_ffn(hf, w_up, conv_w, conv_b, w_down):
    F = w_down.shape[0]
    a = mm_conv_act(hf, [w_up[:, :F], w_up[:, F:]], [conv_w[:, :F], conv_w[:, F:]], [conv_b[:F], conv_b[F:]],
                    _geglu, _geglu_bwd, MXU_DTYPE, "conv_geglu")
    return mm(a, w_down)


def model_loss(x, tgt, big, small, rep):
    S = x.shape[0]
    depth = rep["norm_mix_pre"].shape[0]
    tables = _rope_tables(S, RET_HEADS * RET_DH)
    hn = rms(x, rep["norm_mix_pre"][0], MXU_DTYPE)
    for layer in range(depth):
        i = layer // 2
        if layer % 2 == 0:
            m = retention_na_mixer(hn, big["ab_w_in"][i], rep["ab_ret_decay_logit"][i], rep["ab_ret_gn_g"][i],
                                   rep["ab_na_rpb"][i], big["ab_w_out"][i], tables)
        else:
            m = ssd_mixer(hn, big["c_w_in"][i], small["c_conv_w"][i], small["c_conv_b"][i], rep["c_dt_bias"][i],
                          rep["c_a_log"][i], rep["c_d_skip"][i], small["c_norm_g"][i], big["c_w_out"][i])
        x, hf = rms_residual_norm(m, rep["norm_mix_post"][layer], x, rep["norm_ffn_pre"][layer])
        f = conv_geglu_ffn(hf, big["ffn_w_up"][layer], small["ffn_conv_w"][layer], rep["ffn_conv_b"][layer],
                           big["ffn_w_down"][layer])
        if layer + 1 < depth:
            x, hn = rms_residual_norm(f, rep["norm_ffn_post"][layer], x, rep["norm_mix_pre"][layer + 1])
        else:
            x = rms_residual(f, rep["norm_ffn_post"][layer], x)
    return loss_op(x, tgt)


def _mesh_pos():
    return lax.axis_index("x"), lax.axis_index("y"), lax.axis_index("c")


def gather_chips(local):
    R, Wd = local.shape

    half = R // 2
    CH = COPY_CHUNKS
    q = half // CH

    def body(x_ref, out_ref, send_sems, recv_sems):
        x, y, c = _mesh_pos()
        my = 2 * x + y
        chips = [(1 - x, y), (x, 1 - y), (1 - x, 1 - y)]

        def piece(ref, h, j):
            return ref.at[pl.ds(pl.multiple_of(h * half + j * q, PACK_ALIGN), q), :]

        def copy(k, src, chip, h, j, to):
            return pltpu.make_async_remote_copy(src_ref=src, dst_ref=piece(out_ref.at[chip], h, j), send_sem=send_sems.at[k],
                                                recv_sem=recv_sems.at[k], device_id=to, device_id_type=pl.DeviceIdType.MESH)

        first = [[copy(k * CH + j, piece(x_ref, c, j), my, c, j, (cx, cy, c)) for j in range(CH)]
                 for k, (cx, cy) in enumerate(chips)]
        for j in range(CH):
            for k in range(3):
                first[k][j].start()
        passed = [[copy((3 + k) * CH + j, piece(out_ref.at[2 * cx + cy], c, j), 2 * cx + cy, c, j, (x, y, 1 - c))
                   for j in range(CH)] for k, (cx, cy) in enumerate(chips)]
        for j in range(CH):
            for k, (cx, cy) in enumerate(chips):
                copy(k * CH + j, piece(x_ref, c, j), 2 * cx + cy, c, j, (cx, cy, c)).wait_recv()
                passed[k][j].start()
        for j in range(CH):
            for k, (cx, cy) in enumerate(chips):
                copy((3 + k) * CH + j, piece(x_ref, c, j), 2 * cx + cy, 1 - c, j, (x, y, 1 - c)).wait_recv()
        for k in range(3):
            for cp in first[k] + passed[k]:
                cp.wait_send()

    return pl.pallas_call(
        body,
        out_shape=jax.ShapeDtypeStruct((N_CHIPS, R, Wd), local.dtype),
        in_specs=[pl.BlockSpec(memory_space=pl.ANY)],
        out_specs=pl.BlockSpec(memory_space=pl.ANY),
        scratch_shapes=[pltpu.SemaphoreType.DMA((6 * CH,)), pltpu.SemaphoreType.DMA((6 * CH,))],
        name="gather_chips",
    )(local)


def pair_swap(parts):
    n, R, Wd = parts.shape
    half = R // 2

    CH = COPY_CHUNKS
    q = half // CH

    def body(p_ref, got_ref, send_sems, recv_sems):
        x, y, c = _mesh_pos()

        def src(s, j):
            return p_ref.at[s, pl.ds(pl.multiple_of((1 - c) * half + j * q, PACK_ALIGN), q), :]

        swap = [pltpu.make_async_remote_copy(src_ref=src(s, j), dst_ref=got_ref.at[s, pl.ds(j * q, q), :],
                                             send_sem=send_sems.at[s * CH + j], recv_sem=recv_sems.at[s * CH + j],
                                             device_id=(x, y, 1 - c), device_id_type=pl.DeviceIdType.MESH)
                for s in range(n) for j in range(CH)]
        for cp in swap:
            cp.start()
        for cp in swap:
            cp.wait()

    return pl.pallas_call(
        body,
        out_shape=jax.ShapeDtypeStruct((n, half, Wd), parts.dtype),
        in_specs=[pl.BlockSpec(memory_space=pl.ANY)],
        out_specs=pl.BlockSpec(memory_space=pl.ANY),
        scratch_shapes=[pltpu.SemaphoreType.DMA((n * CH,)), pltpu.SemaphoreType.DMA((n * CH,))],
        name="pair_swap",
    )(parts)


def chip_exchange(parts):
    n, R, Wd = parts.shape

    def body(p_ref, out_ref, send_sems, recv_sems):
        x, y, c = _mesh_pos()
        my = 2 * x + y
        chips = [(1 - x, y), (x, 1 - y), (1 - x, 1 - y)]

        def copy(k, src_slot, dst_slot, to):
            return pltpu.make_async_remote_copy(src_ref=p_ref.at[src_slot], dst_ref=out_ref.at[dst_slot], send_sem=send_sems.at[k],
                                                recv_sem=recv_sems.at[k], device_id=to, device_id_type=pl.DeviceIdType.MESH)

        sends = [copy(k, 2 * cx + cy, my, (cx, cy, c)) for k, (cx, cy) in enumerate(chips)]
        for cp in sends:
            cp.start()
        for k, (cx, cy) in enumerate(chips):
            copy(k, my, 2 * cx + cy, (cx, cy, c)).wait_recv()
        for cp in sends:
            cp.wait_send()

    return pl.pallas_call(
        body,
        out_shape=jax.ShapeDtypeStruct((n, R, Wd), parts.dtype),
        in_specs=[pl.BlockSpec(memory_space=pl.ANY)],
        out_specs=pl.BlockSpec(memory_space=pl.ANY),
        scratch_shapes=[pltpu.SemaphoreType.DMA((3,)), pltpu.SemaphoreType.DMA((3,))],
        name="chip_exchange",
    )(parts)


def pair_share(mine):
    R, Wd = mine.shape

    CH = 2 * COPY_CHUNKS
    q = R // CH

    def body(m_ref, out_ref, send_sems, recv_sems):
        x, y, c = _mesh_pos()
        swap = [pltpu.make_async_remote_copy(src_ref=m_ref.at[pl.ds(j * q, q), :], dst_ref=out_ref.at[pl.ds(j * q, q), :],
                                             send_sem=send_sems.at[j], recv_sem=recv_sems.at[j], device_id=(x, y, 1 - c),
                                             device_id_type=pl.DeviceIdType.MESH) for j in range(CH)]
        for cp in swap:
            cp.start()
        for cp in swap:
            cp.wait()

    return pl.pallas_call(
        body,
        out_shape=jax.ShapeDtypeStruct((R, Wd), mine.dtype),
        in_specs=[pl.BlockSpec(memory_space=pl.ANY)],
        out_specs=pl.BlockSpec(memory_space=pl.ANY),
        scratch_shapes=[pltpu.SemaphoreType.DMA((CH,)), pltpu.SemaphoreType.DMA((CH,))],
        name="pair_share",
    )(mine)


def sum_chips(recv, own):
    n, R, Wd = recv.shape
    tr = _pick(R, (512, 256, 128, 64, 32, 16, 8))

    def body(r_ref, p_ref, o_ref):
        my = 2 * lax.axis_index("x") + lax.axis_index("y")
        acc = jnp.zeros((tr, Wd), f32)
        for s in range(n):
            acc = acc + jnp.where(my == s, p_ref[s], r_ref[s]).astype(f32)
        o_ref[...] = acc

    spec = pl.BlockSpec((n, tr, Wd), lambda i: (0, i, 0))
    return pl.pallas_call(
        body,
        out_shape=jax.ShapeDtypeStruct((R, Wd), f32),
        grid=(R // tr,),
        in_specs=[spec, spec],
        out_specs=pl.BlockSpec((tr, Wd), lambda i: (i, 0)),
        compiler_params=_cparams(("parallel",)),
        name="sum_chips",
    )(recv, own)


def add_pair(parts, got):
    n, R, Wd = parts.shape
    half = R // 2
    tr = _pick(half, (512, 256, 128, 64, 32, 16, 8))
    nb = half // tr

    def body(lo_ref, hi_ref, g_ref, o_ref):
        mine = jnp.where(lax.axis_index("c") == 0, lo_ref[...], hi_ref[...])
        o_ref[...] = (mine.astype(f32) + g_ref[...].astype(f32)).astype(o_ref.dtype)

    spec = pl.BlockSpec((1, tr, Wd), lambda s, i: (s, i, 0))
    return pl.pallas_call(
        body,
        out_shape=jax.ShapeDtypeStruct(got.shape, parts.dtype),
        grid=(n, nb),
        in_specs=[spec, pl.BlockSpec((1, tr, Wd), lambda s, i: (s, nb + i, 0)), spec],
        out_specs=spec,
        compiler_params=_cparams(("parallel", "parallel")),
        name="add_pair",
    )(parts, parts, got)


def reduce_scatter(parts):
    chip_sum = add_pair(parts, pair_swap(parts))
    mine = sum_chips(chip_exchange(chip_sum), chip_sum)
    theirs = pair_share(mine)
    first = lax.axis_index("c") == 0
    return jnp.concatenate([jnp.where(first, mine, theirs), jnp.where(first, theirs, mine)], axis=0)


def adamw(w, g, m, v):
    R, C = w.shape
    tr = R
    for cand in (512, 256, 128, 64, 32, 16, 8):
        if R * C * 4 > (1 << 20) and R % cand == 0 and cand * C * 4 <= (1 << 20):
            tr = cand
            break

    def body(w_ref, g_ref, m_ref, v_ref, d_ref, mo_ref, vo_ref):
        g = g_ref[...]
        m = ADAM_B1 * m_ref[...] + (1.0 - ADAM_B1) * g
        v = ADAM_B2 * v_ref[...] + (1.0 - ADAM_B2) * (g * g)
        m_hat = m / (1.0 - ADAM_B1 ** ADAM_STEP)
        v_hat = v / (1.0 - ADAM_B2 ** ADAM_STEP)
        d_ref[...] = -ADAM_LR * (m_hat / (jnp.sqrt(v_hat) + ADAM_EPS) + ADAM_WD * w_ref[...])
        mo_ref[...] = m
        vo_ref[...] = v

    spec = pl.BlockSpec((tr, C), lambda i: (i, 0))
    return pl.pallas_call(
        body,
        out_shape=[jax.ShapeDtypeStruct((R, C), f32)] * 3,
        grid=(R // tr,),
        in_specs=[spec] * 4,
        out_specs=[spec] * 3,
        compiler_params=_cparams(("parallel",)),
        name="adamw",
    )(w, g, m, v)


def _pack(arrs, dtype):
    flat = jnp.concatenate([a.astype(dtype).reshape(-1) for a in arrs])
    n = flat.shape[0]
    unit = PACK_W * PACK_ROWS
    padded = -(-n // unit) * unit
    return jnp.pad(flat, (0, padded - n)).reshape(-1, PACK_W)


def _unpack(buf, shapes):
    flat = buf.reshape(-1)
    out, off = [], 0
    for s in shapes:
        n = int(np.prod(s))
        out.append(flat[off:off + n].reshape(s))
        off += n
    return out


BIG = (("ab_w_in", 2), ("ab_w_out", 1), ("c_w_in", 2), ("c_w_out", 1), ("ffn_w_up", 2), ("ffn_w_down", 1))
SMALL = (("c_conv_w", 2), ("c_conv_b", 1), ("c_norm_g", 1), ("ffn_conv_w", 2))
REP = ("norm_mix_pre", "norm_mix_post", "norm_ffn_pre", "norm_ffn_post", "ab_ret_decay_logit", "ab_ret_gn_g", "ab_na_rpb",
       "c_dt_bias", "c_a_log", "c_d_skip", "ffn_conv_b")
WEIGHTS = ("norm_mix_pre", "norm_mix_post", "norm_ffn_pre", "norm_ffn_post", "ab_w_in", "ab_ret_decay_logit", "ab_ret_gn_g",
           "ab_na_rpb", "ab_w_out", "c_w_in", "c_conv_w", "c_conv_b", "c_dt_bias", "c_a_log", "c_d_skip", "c_norm_g", "c_w_out",
           "ffn_w_up", "ffn_conv_w", "ffn_conv_b", "ffn_w_down")


def _gather_set(local, spec, dtype):
    shapes = [local[n].shape for n, _ in spec]
    got = gather_chips(_pack([local[n] for n, _ in spec], dtype))
    my = 2 * lax.axis_index("x") + lax.axis_index("y")
    per_chip = [_unpack(got[s], shapes) for s in range(N_CHIPS)]
    return {n: jnp.concatenate([jnp.where(my == s, local[n].astype(dtype), per_chip[s][j]) for s in range(N_CHIPS)], axis=ax)
            for j, (n, ax) in enumerate(spec)}


def _scatter_parts(full, spec, extra, dtype):
    split = {n: jnp.split(full[n], N_CHIPS, axis=ax) for n, ax in spec}
    return jnp.stack([_pack([split[n][s] for n, _ in spec] + list(extra), dtype) for s in range(N_CHIPS)])


def kernel(x, norm_mix_pre, norm_mix_post, norm_ffn_pre, norm_ffn_post, ab_w_in, ab_ret_decay_logit, ab_ret_gn_g, ab_na_rpb, ab_w_out, c_w_in, c_conv_w, c_conv_b, c_dt_bias, c_a_log, c_d_skip, c_norm_g, c_w_out, ffn_w_up, ffn_conv_w, ffn_conv_b, ffn_w_down, loss_target, m_norm_mix_pre, m_norm_mix_post, m_norm_ffn_pre, m_norm_ffn_post, m_ab_w_in, m_ab_ret_decay_logit, m_ab_ret_gn_g, m_ab_na_rpb, m_ab_w_out, m_c_w_in, m_c_conv_w, m_c_conv_b, m_c_dt_bias, m_c_a_log, m_c_d_skip, m_c_norm_g, m_c_w_out, m_ffn_w_up, m_ffn_conv_w, m_ffn_conv_b, m_ffn_w_down, v_norm_mix_pre, v_norm_mix_post, v_norm_ffn_pre, v_norm_ffn_post, v_ab_w_in, v_ab_ret_decay_logit, v_ab_ret_gn_g, v_ab_na_rpb, v_ab_w_out, v_c_w_in, v_c_conv_w, v_c_conv_b, v_c_dt_bias, v_c_a_log, v_c_d_skip, v_c_norm_g, v_c_w_out, v_ffn_w_up, v_ffn_conv_w, v_ffn_conv_b, v_ffn_w_down):
    args = dict(locals())
    w = {n: args[n] for n in WEIGHTS}
    mom = {n: args["m_" + n] for n in WEIGHTS}
    var = {n: args["v_" + n] for n in WEIGHTS}

    big = _gather_set(w, BIG, MXU_DTYPE)
    small = _gather_set(w, SMALL, f32)
    rep = {n: w[n] for n in REP}

    def loss_fn(xs, big, small, rep):
        return model_loss(xs, loss_target[0], big, small, rep)

    loss, (gx, gbig, gsmall, grep) = jax.value_and_grad(loss_fn, argnums=(0, 1, 2, 3))(x[0], big, small, rep)
    loss = lax.psum(loss, ("x", "y", "c"))

    big_shapes = [w[n].shape for n, _ in BIG]
    small_shapes = [w[n].shape for n, _ in SMALL] + [w[n].shape for n in REP]
    g_big = _unpack(reduce_scatter(_scatter_parts(gbig, BIG, (), MXU_DTYPE)), big_shapes)
    g_small_buf = reduce_scatter(_scatter_parts(gsmall, SMALL, [grep[n] for n in REP], f32))
    grads = dict(zip([n for n, _ in BIG], g_big))
    small_names = [n for n, _ in SMALL] + list(REP)
    grads.update(zip(small_names, _unpack(g_small_buf, small_shapes)))

    delta, new_m, new_v = {}, {}, {}
    for n in WEIGHTS:
        shp = w[n].shape
        two_d = lambda a: a.reshape(-1, shp[-1])
        d, m2, v2 = adamw(two_d(w[n]), two_d(grads[n]), two_d(mom[n]), two_d(var[n]))
        delta[n], new_m[n], new_v[n] = d.reshape(shp), m2.reshape(shp), v2.reshape(shp)

    return (loss, gx[None], *[grads[n] for n in WEIGHTS], *[delta[n] for n in WEIGHTS],
            *[new_m[n] for n in WEIGHTS], *[new_v[n] for n in WEIGHTS])
```

```python
import functools
import math

import numpy as np
import jax
import jax.numpy as jnp
from jax import lax
from jax.experimental import pallas as pl
from jax.experimental.pallas import tpu as pltpu

f32 = jnp.float32
bf16 = jnp.bfloat16
MXU_DTYPE = bf16

GRID_W = 64
CHUNK = 128
EPS = 1e-6
RET_HEADS = 8
RET_DH = 64
ROPE_BASE = 10000.0
NA_HEADS = 8
NA_DH = 64
NA_WIN_R = 8
NA_WIN_C = 16
NA_ROWS_PER_STEP = 4
SSD_HEADDIM = 64
SSD_GROUPS = 4
SSD_STATE = 128
ADAM_LR = 0.001
ADAM_B1 = 0.9
ADAM_B2 = 0.999
ADAM_EPS = 1e-08
ADAM_WD = 0.01
ADAM_STEP = 10

LANES = 128
HEAD_W = 64
PACK_W = 512
PACK_ROWS = 1024
PACK_ALIGN = 16
COPY_CHUNKS = 4
VMEM_LIMIT = 56 * 1024 * 1024
MM_BLOCK_BYTES = 6 * 1024 * 1024
N_CHIPS = 4
N_DEV = 8
NEG_INF = -1e30

_DIMS = {"nn": (((1,), (0,)), ((), ())), "nt": (((1,), (1,)), ((), ())), "tn": (((0,), (0,)), ((), ()))}


def _cparams(sem=None):
    return pltpu.CompilerParams(dimension_semantics=sem, vmem_limit_bytes=VMEM_LIMIT)


def _pick(dim, cands):
    for c in cands:
        if dim % c == 0:
            return c
    return dim


def _divisor_tile(dim, fits, align):
    for d in range(1, dim + 1):
        t = dim // d
        if dim % d == 0 and t % align == 0 and fits(t):
            return t
    return dim


def _bdot_raw(a, b, mode):
    return lax.dot_general(a.astype(MXU_DTYPE), b.astype(MXU_DTYPE), _DIMS[mode], preferred_element_type=f32)


@functools.partial(jax.custom_vjp, nondiff_argnums=(2,))
def bdot(a, b, mode):
    return _bdot_raw(a, b, mode)


def _bdot_fwd(a, b, mode):
    return _bdot_raw(a, b, mode), (a, b)


def _bdot_bwd(mode, res, g):
    a, b = res
    if mode == "nn":
        da, db = _bdot_raw(g, b, "nt"), _bdot_raw(a, g, "tn")
    elif mode == "nt":
        da, db = _bdot_raw(g, b, "nn"), _bdot_raw(g, a, "tn")
    else:
        da, db = _bdot_raw(b, g, "nt"), _bdot_raw(a, g, "nn")
    return da.astype(a.dtype), db.astype(b.dtype)


bdot.defvjp(_bdot_fwd, _bdot_bwd)


def _mm_call(a, b, mode, out_dtype):
    if mode == "nn":
        (M, K), (K2, N) = a.shape, b.shape
    elif mode == "nt":
        (M, K), (N, K2) = a.shape, b.shape
    else:
        (K, M), (K2, N) = a.shape, b.shape
    assert K == K2, (a.shape, b.shape, mode)
    a_bytes, b_bytes, o_bytes = a.dtype.itemsize, b.dtype.itemsize, jnp.dtype(out_dtype).itemsize
    if mode == "tn":
        tn = _divisor_tile(N, lambda t: t <= 1536, LANES)
        tm = _divisor_tile(M, lambda t: t * tn * 4 <= MM_BLOCK_BYTES, 8)
        tk = _divisor_tile(K, lambda t: t * tm * a_bytes <= MM_BLOCK_BYTES and t * tn * b_bytes <= MM_BLOCK_BYTES, LANES)
    else:
        tk, tn = K, N
        tm = _divisor_tile(M, lambda t: t * K * a_bytes <= MM_BLOCK_BYTES and t * N * o_bytes <= MM_BLOCK_BYTES, 8)
    nk = K // tk
    if mode == "nn":
        a_spec = pl.BlockSpec((tm, tk), lambda i, j, k: (i, k))
        b_spec = pl.BlockSpec((tk, tn), lambda i, j, k: (k, j))
    elif mode == "nt":
        a_spec = pl.BlockSpec((tm, tk), lambda i, j, k: (i, k))
        b_spec = pl.BlockSpec((tn, tk), lambda i, j, k: (j, k))
    else:
        a_spec = pl.BlockSpec((tk, tm), lambda i, j, k: (k, i))
        b_spec = pl.BlockSpec((tk, tn), lambda i, j, k: (k, j))

    if nk == 1:
        def body(a_ref, b_ref, o_ref):
            o_ref[...] = _bdot_raw(a_ref[...], b_ref[...], mode).astype(o_ref.dtype)
    else:
        def body(a_ref, b_ref, o_ref, acc_ref):
            k = pl.program_id(2)

            @pl.when(k == 0)
            def _():
                acc_ref[...] = jnp.zeros_like(acc_ref)

            acc_ref[...] += _bdot_raw(a_ref[...], b_ref[...], mode)

            @pl.when(k == nk - 1)
            def _():
                o_ref[...] = acc_ref[...].astype(o_ref.dtype)

    return pl.pallas_call(
        body,
        out_shape=jax.ShapeDtypeStruct((M, N), out_dtype),
        grid=(M // tm, N // tn, nk),
        in_specs=[a_spec, b_spec],
        out_specs=pl.BlockSpec((tm, tn), lambda i, j, k: (i, j)),
        scratch_shapes=[pltpu.VMEM((tm, tn), f32)] if nk > 1 else [],
        compiler_params=_cparams(("parallel", "parallel", "arbitrary")),
        name="mm_" + mode,
    )(a, b)


def mm(a, b, mode="nn", out_dtype=f32):
    @jax.custom_vjp
    def op(a, b):
        return _mm_call(a, b, mode, out_dtype)

    def fwd(a, b):
        return _mm_call(a, b, mode, out_dtype), (a, b)

    def bwd(res, g):
        a, b = res
        if mode == "nn":
            return _mm_call(g, b, "nt", a.dtype), _mm_call(a, g, "tn", b.dtype)
        if mode == "nt":
            return _mm_call(g, b, "nn", a.dtype), _mm_call(g, a, "tn", b.dtype)
        return _mm_call(b, g, "nt", a.dtype), _mm_call(a, g, "nn", b.dtype)

    op.defvjp(fwd, bwd)
    return op(a, b)


def _row_tile(S, row_bytes):
    tm = 512
    while tm > 8 and (tm * row_bytes > (6 << 20) or S % tm):
        tm //= 2
    return tm


def rowwise(fn, name, rows, params, out_dtypes, n_diff_rows=None, n_diff_params=None, ncol=1, bwd_fn=None):
    rows, params = list(rows), list(params)
    nr, npar = len(rows), len(params)
    ndr = nr if n_diff_rows is None else n_diff_rows
    ndp = npar if n_diff_params is None else n_diff_params
    S = rows[0].shape[0]
    rw = [r.shape[1] // ncol for r in rows]
    pshape = [(p.shape[0], p.shape[1] // ncol) for p in params]

    def block_structs(tm):
        return ([jax.ShapeDtypeStruct((tm, w), f32) for w in rw] + [jax.ShapeDtypeStruct(s, f32) for s in pshape])

    outs_s = jax.eval_shape(fn, *block_structs(8))
    ow = [o.shape[1] for o in outs_s]
    nout = len(ow)
    row_bytes = 4 * (sum(rw) * 2 + sum(ow) * 2)
    tm = _row_tile(S, row_bytes)
    grid = (ncol, S // tm)

    def rspec(w):
        return pl.BlockSpec((tm, w), lambda g, i: (i, g))

    def pspec(s):
        return pl.BlockSpec(s, lambda g, i: (0, g))

    def call_fwd(*args):
        def body(*refs):
            vals = [r[...].astype(f32) for r in refs[:nr + npar]]
            res = fn(*vals)
            for o, r in zip(refs[nr + npar:], res):
                o[...] = r.astype(o.dtype)

        return pl.pallas_call(
            body,
            out_shape=[jax.ShapeDtypeStruct((S, w * ncol), dt) for w, dt in zip(ow, out_dtypes)],
            grid=grid,
            in_specs=[rspec(w) for w in rw] + [pspec(s) for s in pshape],
            out_specs=[rspec(w) for w in ow],
            compiler_params=_cparams(("parallel", "parallel")),
            name=name + "_fwd",
        )(*args)

    def call_bwd(args, douts):
        def body(*refs):
            in_refs = refs[:nr + npar]
            do_refs = refs[nr + npar:nr + npar + nout]
            dr_refs = refs[nr + npar + nout:nr + npar + nout + ndr]
            dp_refs = refs[nr + npar + nout + ndr:]
            rv = [r[...] for r in in_refs[:nr]]
            pv = [r[...] for r in in_refs[nr:]]
            dos = [d[...].astype(f32) for d in do_refs]
            if bwd_fn is not None:
                drs, dps = bwd_fn(rv, pv, dos)
            else:
                def f(*a):
                    return fn(*a[:ndr], *rv[ndr:], *a[ndr:], *pv[ndp:])

                _, vjp = jax.vjp(f, *[v.astype(f32) for v in rv[:ndr]], *pv[:ndp])
                cts = vjp(tuple(dos))
                drs, dps = cts[:ndr], cts[ndr:]
            for r, ct in zip(dr_refs, drs):
                r[...] = ct.astype(r.dtype)
            if ndp:
                @pl.when(pl.program_id(1) == 0)
                def _():
                    for r in dp_refs:
                        r[...] = jnp.zeros_like(r)

                for r, ct in zip(dp_refs, dps):
                    r[...] += ct

        return pl.pallas_call(
            body,
            out_shape=[jax.ShapeDtypeStruct(r.shape, r.dtype) for r in rows[:ndr]]
            + [jax.ShapeDtypeStruct(p.shape, f32) for p in params[:ndp]],
            grid=grid,
            in_specs=[rspec(w) for w in rw] + [pspec(s) for s in pshape] + [rspec(w) for w in ow],
            out_specs=[rspec(w) for w in rw[:ndr]] + [pspec(s) for s in pshape[:ndp]],
            compiler_params=_cparams(("parallel", "arbitrary")),
            name=name + "_bwd",
        )(*args, *douts)

    @jax.custom_vjp
    def op(*args):
        return tuple(call_fwd(*args))

    def fwd(*args):
        return tuple(call_fwd(*args)), args

    def bwd(args, douts):
        res = call_bwd(args, douts)
        drs, dps = res[:ndr], res[ndr:]
        out = list(drs) + [jnp.zeros_like(a) for a in args[ndr:nr]]
        out += [dp.astype(p.dtype) for dp, p in zip(dps, args[nr:nr + ndp])]
        out += [jnp.zeros_like(a) for a in args[nr + ndp:]]
        return tuple(out)

    op.defvjp(fwd, bwd)
    return op(*rows, *params)


def _silu(x):
    return x * (1.0 / (1.0 + jnp.exp(-x)))


def _softplus(x):
    return jnp.maximum(x, 0.0) + jnp.log(1.0 + jnp.exp(-jnp.abs(x)))


def _gelu_tanh(x):
    return 0.5 * x * (1.0 + jnp.tanh(math.sqrt(2.0 / math.pi) * (x + 0.044715 * (x * x * x))))


def _rms_fn(x, g):
    return x * lax.rsqrt(jnp.mean(x * x, axis=-1, keepdims=True) + EPS) * g


def _rms_bwd(x, g, dy):
    r = lax.rsqrt(jnp.mean(x * x, axis=-1, keepdims=True) + EPS)
    xh = x * r
    dxh = dy * g
    dx = r * (dxh - xh * jnp.mean(dxh * xh, axis=-1, keepdims=True))
    return dx, jnp.sum(dy * xh, axis=0, keepdims=True)


def rms(x, g, out_dtype):
    def bwd_fn(rv, pv, dos):
        dx, dg = _rms_bwd(rv[0], pv[0], dos[0])
        return (dx,), (dg,)

    return rowwise(lambda x, g: (_rms_fn(x, g),), "rms", [x], [g.reshape(1, -1)], [out_dtype], bwd_fn=bwd_fn)[0]


def rms_residual_norm(m, g, x, g_next):
    def fn(m, x, g, gn):
        xn = x + _rms_fn(m, g)
        return xn, _rms_fn(xn, gn)

    def bwd_fn(rv, pv, dos):
        (m, x), (g, gn), (dxn, dhn) = rv, pv, dos
        xn = x + _rms_fn(m, g)
        d_from_norm, dgn = _rms_bwd(xn, gn, dhn)
        dxn = dxn + d_from_norm
        dm, dg = _rms_bwd(m, g, dxn)
        return (dm, dxn), (dg, dgn)

    return rowwise(fn, "rms_res_norm", [m, x], [g.reshape(1, -1), g_next.reshape(1, -1)], [f32, MXU_DTYPE], bwd_fn=bwd_fn)


def rms_residual(m, g, x):
    def bwd_fn(rv, pv, dos):
        dm, dg = _rms_bwd(rv[0], pv[0], dos[0])
        return (dm, dos[0]), (dg,)

    return rowwise(lambda m, x, g: (x + _rms_fn(m, g),), "rms_res", [m, x], [g.reshape(1, -1)], [f32], bwd_fn=bwd_fn)[0]


def loss_op(y, tgt):
    S, D = y.shape
    tm = _row_tile(S, 4 * D * 4)

    def call_fwd(y, tgt):
        def body(y_ref, t_ref, o_ref):
            @pl.when(pl.program_id(0) == 0)
            def _():
                o_ref[...] = jnp.zeros_like(o_ref)

            e = y_ref[...] - t_ref[...]
            o_ref[...] += 0.5 * jnp.sum(jnp.mean(e * e, axis=-1, keepdims=True))

        out = pl.pallas_call(
            body,
            out_shape=jax.ShapeDtypeStruct((8, LANES), f32),
            grid=(S // tm,),
            in_specs=[pl.BlockSpec((tm, D), lambda i: (i, 0))] * 2,
            out_specs=pl.BlockSpec((8, LANES), lambda i: (0, 0)),
            compiler_params=_cparams(("arbitrary",)),
            name="loss_fwd",
        )(y, tgt)
        return out[0, 0]

    def call_bwd(y, tgt, g):
        def body(y_ref, t_ref, g_ref, o_ref):
            o_ref[...] = (y_ref[...] - t_ref[...]) * (g_ref[...] * (1.0 / D))

        return pl.pallas_call(
            body,
            out_shape=jax.ShapeDtypeStruct((S, D), f32),
            grid=(S // tm,),
            in_specs=[pl.BlockSpec((tm, D), lambda i: (i, 0))] * 2 + [pl.BlockSpec((1, 1), lambda i: (0, 0))],
            out_specs=pl.BlockSpec((tm, D), lambda i: (i, 0)),
            compiler_params=_cparams(("parallel",)),
            name="loss_bwd",
        )(y, tgt, g.reshape(1, 1).astype(f32))

    @jax.custom_vjp
    def op(y, tgt):
        return call_fwd(y, tgt)

    def fwd(y, tgt):
        return call_fwd(y, tgt), (y, tgt)

    def bwd(res, g):
        y, tgt = res
        return call_bwd(y, tgt, g), jnp.zeros_like(tgt)

    op.defvjp(fwd, bwd)
    return op(y, tgt)


HALO = 8


def _conv_tile(S, R):
    def ext(ref, r0):
        cur = ref[pl.ds(r0, R), :]
        prev = ref[pl.ds(pl.multiple_of(jnp.maximum(r0 - HALO, 0), HALO), HALO), :]
        nxt = ref[pl.ds(pl.multiple_of(jnp.minimum(r0 + R, S - HALO), HALO), HALO), :]
        prev = jnp.where(r0 > 0, prev, 0.0)
        nxt = jnp.where(r0 + R < S, nxt, 0.0)
        return jnp.concatenate([prev, cur, nxt], axis=0)

    return ext


def _shift_rows(e, k, R):
    n = e.shape[0]
    if k == 0:
        return e[HALO:HALO + R]
    return pltpu.roll(e, (-k) % n, 0)[HALO:HALO + R]


def _silu_bwd(us, dy):
    u, = us
    s = 1.0 / (1.0 + jnp.exp(-u))
    return (dy * (s * (1.0 + u * (1.0 - s))),)


def _geglu(g, v):
    return _gelu_tanh(g) * v


def _geglu_bwd(us, dy):
    g, v = us
    c = math.sqrt(2.0 / math.pi)
    t = jnp.tanh(c * (g + 0.044715 * (g * g * g)))
    half = 0.5 * (1.0 + t)
    dgelu = half + 0.5 * g * (1.0 - t * t) * (c * (1.0 + 3.0 * 0.044715 * (g * g)))
    return dy * v * dgelu, dy * (g * half)


def mm_conv_act(h, ws, cws, cbs, act, act_bwd, out_dtype, name):
    n = len(ws)
    S = h.shape[0]
    C = ws[0].shape[1]
    W = cws[0].shape[0]
    pad = W // 2
    bw = _pick(C, (LANES,))
    R = _pick(S, (256, 128, 64, 32, 16, 8))
    nt = S // R
    ext = _conv_tile(S, R)
    col = lambda rows: pl.BlockSpec((rows, bw), lambda j: (0, j))

    def conv(e, wv, bv):
        acc = bv + wv[pad] * e[HALO:HALO + R]
        for j in range(W):
            if j != pad:
                acc = acc + wv[j] * _shift_rows(e, j - pad, R)
        return acc

    def call_fwd(xs, cws, cbs):
        def body(*refs):
            x_refs, w_refs, b_refs, y_ref = refs[:n], refs[n:2 * n], refs[2 * n:3 * n], refs[3 * n]
            wvs = [[w[j:j + 1, :] for j in range(W)] for w in w_refs]
            bvs = [b[...] for b in b_refs]

            def tile(i, c):
                r0 = pl.multiple_of(i * R, R)
                us = [conv(ext(x, r0), wv, bv) for x, wv, bv in zip(x_refs, wvs, bvs)]
                y_ref[pl.ds(r0, R), :] = act(*us).astype(y_ref.dtype)
                return c

            lax.fori_loop(0, nt, tile, 0)

        return pl.pallas_call(
            body,
            out_shape=jax.ShapeDtypeStruct((S, C), out_dtype),
            grid=(C // bw,),
            in_specs=[col(S)] * n + [col(W)] * n + [col(1)] * n,
            out_specs=col(S),
            compiler_params=_cparams(("parallel",)),
            name=name + "_fwd",
        )(*xs, *cws, *cbs)

    def call_bwd(xs, cws, cbs, dy):
        def body(*refs):
            x_refs, w_refs, b_refs, dy_ref = refs[:n], refs[n:2 * n], refs[2 * n:3 * n], refs[3 * n]
            dx_refs, dw_refs, db_refs = refs[3 * n + 1:4 * n + 1], refs[4 * n + 1:5 * n + 1], refs[5 * n + 1:6 * n + 1]
            du_scr = refs[6 * n + 1:]
            wvs = [[w[j:j + 1, :] for j in range(W)] for w in w_refs]
            bvs = [b[...] for b in b_refs]
            zero = jnp.zeros((1, bw), f32)

            def first(i, dbs):
                r0 = pl.multiple_of(i * R, R)
                us = [conv(ext(x, r0), wv, bv) for x, wv, bv in zip(x_refs, wvs, bvs)]
                dus = act_bwd(us, dy_ref[pl.ds(r0, R), :].astype(f32))
                for scr, du in zip(du_scr, dus):
                    scr[pl.ds(r0, R), :] = du
                return tuple(db + jnp.sum(du, axis=0, keepdims=True) for db, du in zip(dbs, dus))

            dbs = lax.fori_loop(0, nt, first, tuple(zero for _ in range(n)))

            def second(i, dws):
                r0 = pl.multiple_of(i * R, R)
                new = []
                for x, scr, dx, wv, dw in zip(x_refs, du_scr, dx_refs, wvs, dws):
                    ex, ed = ext(x, r0), ext(scr, r0)
                    d0 = ed[HALO:HALO + R]
                    acc = jnp.zeros((R, bw), f32)
                    row = []
                    for j in range(W):
                        acc = acc + wv[j] * _shift_rows(ed, pad - j, R)
                        row.append(dw[j] + jnp.sum(d0 * _shift_rows(ex, j - pad, R), axis=0, keepdims=True))
                    dx[pl.ds(r0, R), :] = acc.astype(dx.dtype)
                    new.append(tuple(row))
                return tuple(new)

            dws = lax.fori_loop(0, nt, second, tuple(tuple(zero for _ in range(W)) for _ in range(n)))
            for dw_ref, db_ref, dw, db in zip(dw_refs, db_refs, dws, dbs):
                dw_ref[...] = jnp.zeros_like(dw_ref)
                for j in range(W):
                    dw_ref[j:j + 1, :] = dw[j]
                db_ref[...] = db

        return pl.pallas_call(
            body,
            out_shape=[jax.ShapeDtypeStruct((S, C), MXU_DTYPE)] * n + [jax.ShapeDtypeStruct((8, C), f32)] * n
            + [jax.ShapeDtypeStruct((1, C), f32)] * n,
            grid=(C // bw,),
            in_specs=[col(S)] * n + [col(W)] * n + [col(1)] * n + [col(S)],
            out_specs=[col(S)] * n + [col(8)] * n + [col(1)] * n,
            scratch_shapes=[pltpu.VMEM((S, bw), f32)] * n,
            compiler_params=_cparams(("parallel",)),
            name=name + "_bwd",
        )(*xs, *cws, *cbs, dy)

    @jax.custom_vjp
    def op(h, ws, cws, cbs):
        return call_fwd([_mm_call(h, w, "nn", f32) for w in ws], cws, cbs)

    def fwd(h, ws, cws, cbs):
        xs = [_mm_call(h, w, "nn", f32) for w in ws]
        return call_fwd(xs, cws, cbs), (h, ws, xs, cws, cbs)

    def bwd(res, dy):
        h, ws, xs, cws, cbs = res
        out = call_bwd(xs, cws, cbs, dy)
        dxs, dcws, dcbs = out[:n], out[n:2 * n], out[2 * n:]
        dh = _mm_call(dxs[0], ws[0], "nt", h.dtype)
        for dx, w in zip(dxs[1:], ws[1:]):
            dh = dh + _mm_call(dx, w, "nt", h.dtype)
        dws = tuple(_mm_call(h, dx, "tn", w.dtype) for dx, w in zip(dxs, ws))
        return dh, dws, tuple(d[:W] for d in dcws), tuple(dcbs)

    op.defvjp(fwd, bwd)
    return op(h, tuple(ws), tuple(cws), tuple(b.reshape(1, C) for b in cbs))


@jax.custom_vjp
def _masked_decay(cs_col, cs_row, mask01):
    return jnp.where(mask01 > 0, jnp.exp(cs_col - cs_row), 0.0)


def _masked_decay_fwd(cs_col, cs_row, mask01):
    d = jnp.where(mask01 > 0, jnp.exp(cs_col - cs_row), 0.0)
    return d, (d, mask01)


def _masked_decay_bwd(res, g):
    d, mask01 = res
    t = g * d
    return jnp.sum(t, axis=1, keepdims=True), -jnp.sum(t, axis=0, keepdims=True), jnp.zeros_like(mask01)


_masked_decay.defvjp(_masked_decay_fwd, _masked_decay_bwd)


def _scan_chunk(qs, ks, xs, cs_tok, dt_tok, hs, *, rev, incl, nsub):
    nb = len(xs)
    L, N = qs[0].shape
    W = xs[0].shape[1]
    Hg = cs_tok.shape[1]
    nh = W // HEAD_W
    shared = len(qs) == 1
    t = lax.broadcasted_iota(jnp.int32, (L, L), 0)
    l = lax.broadcasted_iota(jnp.int32, (L, L), 1)
    if rev:
        mask = (l >= t) if incl else (l > t)
    else:
        mask = (l <= t) if incl else (l < t)
    mask01 = mask.astype(f32)
    lane_a = lax.broadcasted_iota(jnp.int32, cs_tok.shape, 1)
    row_a = lax.broadcasted_iota(jnp.int32, (Hg, L), 0)
    last = lax.broadcasted_iota(jnp.int32, (1, L), 1) == (0 if rev else L - 1)
    vhead = lax.broadcasted_iota(jnp.int32, (1, W), 1) // HEAD_W
    qhead = lax.broadcasted_iota(jnp.int32, (1, N), 1) // (N // nsub)
    cs_rows = lax.dot_general(cs_tok, (t == l).astype(f32), _DIMS["tn"], precision=lax.Precision.HIGHEST,
                              preferred_element_type=f32)

    def by_head(vals):
        if len(vals) == 2:
            return jnp.where(vhead == 0, vals[0], vals[1])
        return sum(jnp.where(vhead == i, v, 0.0) for i, v in enumerate(vals))

    decay, lam_e, tau_e, gam_e, dt_e = [], [], [], [], []
    for b in range(nb):
        cs_cols, tots, dt_cols = [], [], []
        for i in range(nh):
            head = b * nh + i
            cs_col = jnp.sum(jnp.where(lane_a == head, cs_tok, 0.0), axis=1, keepdims=True)
            cs_row = jnp.sum(jnp.where(row_a == head, cs_rows, 0.0), axis=0, keepdims=True)
            tots.append(jnp.sum(jnp.where(last, cs_row, 0.0), axis=1, keepdims=True))
            decay.append(_masked_decay(cs_col, cs_row, mask01))
            cs_cols.append(cs_col)
            if dt_tok is not None:
                dt_cols.append(jnp.sum(jnp.where(lane_a == head, dt_tok, 0.0), axis=1, keepdims=True))
        cs_e, tot_e = by_head(cs_cols), by_head(tots)
        lam_e.append(jnp.exp(cs_e))
        tau_e.append(jnp.exp(tot_e - cs_e))
        gam_e.append(jnp.exp(tot_e))
        if dt_tok is not None:
            dt_e.append(by_head(dt_cols))
    vs = [x if dt_tok is None else x * dt_e[b] for b, x in enumerate(xs)]
    qk = lambda b: (qs[0], ks[0]) if shared else (qs[b], ks[b])
    if nsub == 1:
        scores = [bdot(qs[0], ks[0], "nt")] if shared else [bdot(*qk(b), "nt") for b in range(nb)]
        score = lambda b, i: scores[0 if shared else b]
    else:
        scores = [[bdot(jnp.where(qhead == i, qk(b)[0], 0.0), qk(b)[1], "nt") for i in range(nh)] for b in range(nb)]
        score = lambda b, i: scores[b][i]
    ys = [lam_e[b] * bdot(qk(b)[0], hs[b], "nn")
          + by_head([bdot(score(b, i) * decay[b * nh + i], vs[b], "nn") for i in range(nh)]) for b in range(nb)]
    hns = [gam_e[b] * hs[b] + bdot(qk(b)[1], tau_e[b] * vs[b], "tn") for b in range(nb)]
    if nsub > 1:
        nhead = lax.broadcasted_iota(jnp.int32, (N, W), 0) // (N // nsub)
        keep = nhead == lax.broadcasted_iota(jnp.int32, (N, W), 1) // HEAD_W
        hns = [jnp.where(keep, hn, 0.0) for hn in hns]
    return ys, hns


def chunk_cumsum(a_tok, rev):
    G, S, Hg = a_tok.shape
    L = CHUNK
    CB = _pick(S // L, (16, 8, 4, 2))

    def call(a, rev):
        def body(a_ref, o_ref):
            t = lax.broadcasted_iota(jnp.int32, (L, L), 0)
            l = lax.broadcasted_iota(jnp.int32, (L, L), 1)
            tri = ((l >= t) if rev else (l <= t)).astype(f32)
            for j in range(CB):
                o_ref[0, j * L:(j + 1) * L, :] = _exact_dot(tri, a_ref[0, j * L:(j + 1) * L, :])

        spec = pl.BlockSpec((1, CB * L, Hg), lambda g, c: (g, c, 0))
        return pl.pallas_call(
            body,
            out_shape=jax.ShapeDtypeStruct((G, S, Hg), f32),
            grid=(G, S // (CB * L)),
            in_specs=[spec],
            out_specs=spec,
            compiler_params=_cparams(("parallel", "parallel")),
            name="chunk_cumsum",
        )(a)

    @jax.custom_vjp
    def op(a):
        return call(a, rev)

    def fwd(a):
        return call(a, rev), None

    def bwd(_, g):
        return (call(g, not rev),)

    op.defvjp(fwd, bwd)
    return op(a_tok)


def scan_op(q, k, x, a_tok, dt_tok, *, rev, incl, nsub):
    S = q.shape[0]
    G, _, Hg = a_tok.shape
    N = q.shape[1] // G
    Vw = x.shape[1] // G
    L = CHUNK
    nc = S // L
    use_dt = dt_tok is not None
    PW = min(Vw, LANES)
    chunk = functools.partial(_scan_chunk, rev=rev, incl=incl, nsub=nsub)
    cols = [slice(p * PW, (p + 1) * PW) for p in range(Vw // PW)]
    own_qk = nsub > 1
    NB = PW if own_qk else N

    def order(c, backward):
        return (nc - 1 - c) if (rev != backward) else c

    def specs(backward):
        qs = pl.BlockSpec((L, N), lambda g, c: (order(c, backward), g))
        xs = pl.BlockSpec((L, Vw), lambda g, c: (order(c, backward), g))
        as_ = pl.BlockSpec((1, L, Hg), lambda g, c: (g, order(c, backward), 0))
        hs = pl.BlockSpec((1, 1, NB, Vw), lambda g, c: (g, order(c, backward), 0, 0))
        return qs, xs, as_, hs

    def call_fwd(q, k, x, a_tok, dt_tok, y_prev=None):
        qs, xs, as_, hs = specs(False)
        n_in = 4 + use_dt + (y_prev is not None)

        def body(*refs):
            q_ref, k_ref, x_ref, a_ref = refs[:4]
            dt_ref = refs[4] if use_dt else None
            yp_ref = refs[n_in - 1] if y_prev is not None else None
            y_ref, hs_ref, h_scr = refs[n_in:]

            @pl.when(pl.program_id(1) == 0)
            def _():
                h_scr[...] = jnp.zeros_like(h_scr)

            hs_ref[0, 0] = h_scr[...]
            q, k, a, dt = q_ref[...], k_ref[...], a_ref[0], dt_ref[0] if use_dt else None
            qs, ks = ([q[:, c] for c in cols], [k[:, c] for c in cols]) if own_qk else ([q], [k])
            ys, hns = chunk(qs, ks, [x_ref[:, c] for c in cols], a, dt, [h_scr[:, c] for c in cols])
            for c, y, hn in zip(cols, ys, hns):
                y_ref[:, c] = y if yp_ref is None else y + yp_ref[:, c]
                h_scr[:, c] = hn

        ins = [q, k, x, a_tok] + ([dt_tok] if use_dt else []) + ([y_prev] if y_prev is not None else [])
        return pl.pallas_call(
            body,
            out_shape=[jax.ShapeDtypeStruct((S, G * Vw), f32), jax.ShapeDtypeStruct((G, nc, NB, Vw), f32)],
            grid=(G, nc),
            in_specs=[qs, qs, xs, as_] + ([as_] if use_dt else []) + ([xs] if y_prev is not None else []),
            out_specs=[xs, hs],
            scratch_shapes=[pltpu.VMEM((NB, Vw), f32)],
            compiler_params=_cparams(("parallel", "arbitrary")),
            name="scan_fwd",
        )(*ins)

    def call_bwd(q, k, x, a_tok, dt_tok, hsave, dy, acc=None):
        qs, xs, as_, hs = specs(True)
        n_in = 6 + use_dt + (3 if acc is not None else 0)

        def body(*refs):
            q_ref, k_ref, x_ref, a_ref = refs[:4]
            dt_ref = refs[4] if use_dt else None
            hs_ref, dy_ref = refs[4 + use_dt], refs[5 + use_dt]
            acc_refs = refs[n_in - 3:n_in] if acc is not None else None
            dq_ref, dk_ref, dx_ref, da_ref = refs[n_in:n_in + 4]
            ddt_ref = refs[n_in + 4] if use_dt else None
            dh_scr = refs[-1]

            @pl.when(pl.program_id(1) == 0)
            def _():
                dh_scr[...] = jnp.zeros_like(dh_scr)

            q, k, a = q_ref[...].astype(f32), k_ref[...].astype(f32), a_ref[0]
            qs, ks = ([q[:, c] for c in cols], [k[:, c] for c in cols]) if own_qk else ([q], [k])
            xs, hs_in = [x_ref[:, c] for c in cols], [hs_ref[0, 0, :, c] for c in cols]
            if use_dt:
                _, vjp = jax.vjp(chunk, qs, ks, xs, a, dt_ref[0], hs_in)
            else:
                _, vjp = jax.vjp(lambda qs, ks, xs, a, hs: chunk(qs, ks, xs, a, None, hs), qs, ks, xs, a, hs_in)
            cts = vjp(([dy_ref[:, c] for c in cols], [dh_scr[:, c] for c in cols]))
            dqs, dks, dxs, da, dhs = cts[0], cts[1], cts[2], cts[3], cts[-1]
            if acc is not None:
                dq_acc, dk_acc = acc_refs[0][...].astype(f32), acc_refs[1][...].astype(f32)
            if own_qk:
                for b, c in enumerate(cols):
                    dq_ref[:, c] = (dqs[b] if acc is None else dqs[b] + dq_acc[:, c]).astype(dq_ref.dtype)
                    dk_ref[:, c] = (dks[b] if acc is None else dks[b] + dk_acc[:, c]).astype(dk_ref.dtype)
            else:
                dq_ref[...] = (dqs[0] if acc is None else dqs[0] + dq_acc).astype(dq_ref.dtype)
                dk_ref[...] = (dks[0] if acc is None else dks[0] + dk_acc).astype(dk_ref.dtype)
            for b, c in enumerate(cols):
                dx_ref[:, c] = dxs[b] if acc is None else dxs[b] + acc_refs[2][:, c]
                dh_scr[:, c] = dhs[b]
            da_ref[0] = da
            if use_dt:
                ddt_ref[0] = cts[4]

        ins = [q, k, x, a_tok] + ([dt_tok] if use_dt else []) + [hsave, dy] + (list(acc) if acc is not None else [])
        a_shape = jax.ShapeDtypeStruct(a_tok.shape, f32)
        return pl.pallas_call(
            body,
            out_shape=[jax.ShapeDtypeStruct(q.shape, q.dtype), jax.ShapeDtypeStruct(k.shape, k.dtype),
                       jax.ShapeDtypeStruct(x.shape, f32), a_shape] + ([a_shape] if use_dt else []),
            grid=(G, nc),
            in_specs=[qs, qs, xs, as_] + ([as_] if use_dt else []) + [hs, xs] + ([qs, qs, xs] if acc is not None else []),
            out_specs=[qs, qs, xs, as_] + ([as_] if use_dt else []),
            scratch_shapes=[pltpu.VMEM((NB, Vw), f32)],
            compiler_params=_cparams(("parallel", "arbitrary")),
            name="scan_bwd",
        )(*ins)

    return call_fwd, call_bwd


def bidir_scan(q, k, x, a_f, a_b, dt_f, dt_b, *, nsub):
    use_dt = dt_f is not None
    a_f, a_b = chunk_cumsum(a_f, False), chunk_cumsum(a_b, True)
    fwd_f, bwd_f = scan_op(q, k, x, a_f, dt_f, rev=False, incl=True, nsub=nsub)
    fwd_b, bwd_b = scan_op(q, k, x, a_b, dt_b, rev=True, incl=False, nsub=nsub)

    def run(q, k, x, a_f, a_b, dt_f, dt_b):
        y_f, hs_f = fwd_f(q, k, x, a_f, dt_f)
        y, hs_b = fwd_b(q, k, x, a_b, dt_b, y_prev=y_f)
        return y, (hs_f, hs_b)

    def grads(q, k, x, a_f, a_b, dt_f, dt_b, hs, dy):
        first = bwd_f(q, k, x, a_f, dt_f, hs[0], dy)
        both = bwd_b(q, k, x, a_b, dt_b, hs[1], dy, acc=first[:3])
        return both[0], both[1], both[2], first[3], both[3], (first[4] if use_dt else None), (both[4] if use_dt else None)

    if use_dt:
        @jax.custom_vjp
        def op(q, k, x, a_f, a_b, dt_f, dt_b):
            return run(q, k, x, a_f, a_b, dt_f, dt_b)[0]

        def fwd(q, k, x, a_f, a_b, dt_f, dt_b):
            y, hs = run(q, k, x, a_f, a_b, dt_f, dt_b)
            return y, (q, k, x, a_f, a_b, dt_f, dt_b, hs)

        def bwd(res, dy):
            return grads(*res, dy)

        op.defvjp(fwd, bwd)
        return op(q, k, x, a_f, a_b, dt_f, dt_b)

    @jax.custom_vjp
    def op(q, k, x, a_f, a_b):
        return run(q, k, x, a_f, a_b, None, None)[0]

    def fwd(q, k, x, a_f, a_b):
        y, hs = run(q, k, x, a_f, a_b, None, None)
        return y, (q, k, x, a_f, a_b, hs)

    def bwd(res, dy):
        q, k, x, a_f, a_b, hs = res
        return grads(q, k, x, a_f, a_b, None, None, hs, dy)[:5]

    op.defvjp(fwd, bwd)
    return op(q, k, x, a_f, a_b)


def _swap_halves(x, dh):
    W = x.shape[1]
    lane = lax.broadcasted_iota(jnp.int32, (1, W), 1) % dh
    return jnp.where(lane < dh // 2, pltpu.roll(x, W - dh // 2, 1), pltpu.roll(x, dh // 2, 1))


def rotary(rq, rk, cos_t, sin_t):
    scale = RET_DH ** -0.5

    def fn(rq, rk, c, s):
        return rq * c + _swap_halves(rq, RET_DH) * s, (rk * c + _swap_halves(rk, RET_DH) * s) * scale

    def bwd_fn(rv, pv, dos):
        _, _, c, s = rv
        dq, dk = dos
        dk = dk * scale
        return (dq * c + _swap_halves(dq * s, RET_DH), dk * c + _swap_halves(dk * s, RET_DH)), ()

    return rowwise(fn, "rotary", [rq, rk, cos_t, sin_t], [], [MXU_DTYPE, MXU_DTYPE], n_diff_rows=2, bwd_fn=bwd_fn)


def _rope_tables(S, width):
    half = RET_DH // 2
    inv = 1.0 / (ROPE_BASE ** (jnp.arange(half, dtype=f32) / half))
    ang = jnp.arange(S, dtype=f32)[:, None] * inv[None, :]
    cos, sin = jnp.cos(ang), jnp.sin(ang)
    reps = width // RET_DH
    return jnp.tile(jnp.concatenate([cos, cos], axis=1), (1, reps)), jnp.tile(jnp.concatenate([-sin, sin], axis=1), (1, reps))


def _exact_dot(x, m):
    return jnp.dot(x, m, precision=lax.Precision.HIGHEST, preferred_element_type=f32)


def ret_post(y, rg, gn_g):
    W = y.shape[1]
    idx = np.arange(W) // RET_DH
    avg = jnp.asarray((idx[:, None] == idx[None, :]).astype(np.float32) / RET_DH)

    def fn(y, rg, g, avg):
        mu = _exact_dot(y, avg)
        d = y - mu
        var = _exact_dot(d * d, avg)
        return (_silu(rg) * (d * lax.rsqrt(var + EPS) * g),)

    return rowwise(fn, "ret_post", [y, rg], [gn_g.reshape(1, -1), avg], [MXU_DTYPE], n_diff_params=1)[0]


def _na_bias(rpb, win_r):
    H = rpb.shape[0]
    qc = np.arange(GRID_W)[:, None]
    kc = np.arange(GRID_W)[None, :]
    cstart = np.clip(qc - NA_WIN_C // 2, 0, GRID_W - NA_WIN_C)
    valid = (kc >= cstart) & (kc < cstart + NA_WIN_C)
    dc = np.clip(kc - qc, -(NA_WIN_C - 1), NA_WIN_C - 1) + (NA_WIN_C - 1)
    onehot = (dc[None] == np.arange(2 * NA_WIN_C - 1)[:, None, None]).astype(np.float32)
    t1 = jnp.einsum("hrd,dqk->hrqk", rpb.astype(f32), jnp.asarray(onehot), precision=lax.Precision.HIGHEST)
    per_delta = [t1[:, NA_WIN_R - 1 - d:NA_WIN_R - 1 - d + win_r] for d in range(win_r)]
    b = jnp.stack(per_delta, axis=1)
    b = jnp.where(jnp.asarray(valid)[None, None, None], b, NEG_INF)
    return jnp.transpose(b, (0, 1, 3, 2, 4)).reshape(H, win_r, GRID_W, win_r * GRID_W)


def _na_rows(rows):
    lane = lax.broadcasted_iota(jnp.int32, (1, rows[0][0].shape[1]), 1) // NA_DH
    scale = NA_DH ** -0.5
    ss = [[_bdot_raw(jnp.where(lane == i, q, 0.0) * scale, kw, "nt") + b for i, b in enumerate(bs)] for q, kw, _, bs in rows]
    es = [[jnp.exp(s - jnp.max(s, axis=1, keepdims=True)) for s in srow] for srow in ss]
    ps = [[e / jnp.sum(e, axis=1, keepdims=True) for e in erow] for erow in es]
    outs = []
    for prow, (_, _, vw, _) in zip(ps, rows):
        o = 0.0
        for i, p in enumerate(prow):
            o = o + jnp.where(lane == i, _bdot_raw(p, vw, "nn"), 0.0)
        outs.append(o)
    return outs


def _na_rows_bwd(rows):
    lane = lax.broadcasted_iota(jnp.int32, (1, rows[0][0].shape[1]), 1) // NA_DH
    scale = NA_DH ** -0.5
    heads = range(len(rows[0][3]))
    qis = [[jnp.where(lane == i, q, 0.0) * scale for i in heads] for q, _, _, _, _ in rows]
    dos = [[jnp.where(lane == i, do, 0.0) for i in heads] for _, _, _, _, do in rows]
    ss = [[_bdot_raw(qi, kw, "nt") + b for qi, b in zip(qrow, bs)] for qrow, (_, kw, _, bs, _) in zip(qis, rows)]
    dps = [[_bdot_raw(doi, vw, "nt") for doi in drow] for drow, (_, _, vw, _, _) in zip(dos, rows)]
    es = [[jnp.exp(s - jnp.max(s, axis=1, keepdims=True)) for s in srow] for srow in ss]
    ps = [[e / jnp.sum(e, axis=1, keepdims=True) for e in erow] for erow in es]
    dss = [[p * (dp - jnp.sum(dp * p, axis=1, keepdims=True)) for p, dp in zip(prow, dprow)] for prow, dprow in zip(ps, dps)]
    out = []
    for qrow, drow, prow, dsrow, (_, kw, _, _, _) in zip(qis, dos, ps, dss, rows):
        dq, dk, dv = 0.0, 0.0, 0.0
        for i in heads:
            dq = dq + jnp.where(lane == i, _bdot_raw(dsrow[i], kw, "nn"), 0.0) * scale
            dk = dk + _bdot_raw(dsrow[i], qrow[i], "tn")
            dv = dv + _bdot_raw(prow[i], drow[i], "tn")
        out.append((dq, dk, dv, dsrow))
    return out


def na_op(nq, nk, nv, bias):
    S, W = nq.shape
    rows = S // GRID_W
    win_r = bias.shape[1]
    nkeys = win_r * GRID_W
    hp = LANES // NA_DH
    npair = W // LANES
    RB = min(16, rows)
    nrb = rows // RB
    qspec = pl.BlockSpec((RB * GRID_W, LANES), lambda p, r: (r, p))
    kspec = pl.BlockSpec((S, LANES), lambda p, r: (0, p))
    bspec = pl.BlockSpec((hp, win_r, GRID_W, nkeys), lambda p, r: (p, 0, 0, 0))

    def window(r):
        r0 = jnp.clip(r - win_r // 2, 0, rows - win_r)
        return pl.multiple_of(r0 * GRID_W, GRID_W), r - r0

    def call_fwd(nq, nk, nv, bias):
        def body(q_ref, k_ref, v_ref, b_ref, o_ref):
            rb = pl.program_id(1)

            def step(j, c):
                args, q0s = [], []
                for u in range(NA_ROWS_PER_STEP):
                    i = j * NA_ROWS_PER_STEP + u
                    k0, d = window(rb * RB + i)
                    q0 = pl.multiple_of(i * GRID_W, GRID_W)
                    q0s.append(q0)
                    args.append((q_ref[pl.ds(q0, GRID_W), :].astype(f32), k_ref[pl.ds(k0, nkeys), :], v_ref[pl.ds(k0, nkeys), :],
                                 [b_ref[h, pl.ds(d, 1)][0] for h in range(hp)]))
                for q0, o in zip(q0s, _na_rows(args)):
                    o_ref[pl.ds(q0, GRID_W), :] = o.astype(o_ref.dtype)
                return c

            lax.fori_loop(0, RB // NA_ROWS_PER_STEP, step, 0)

        return pl.pallas_call(
            body,
            out_shape=jax.ShapeDtypeStruct((S, W), nq.dtype),
            grid=(npair, nrb),
            in_specs=[qspec, kspec, kspec, bspec],
            out_specs=qspec,
            compiler_params=_cparams(("parallel", "arbitrary")),
            name="na_fwd",
        )(nq, nk, nv, bias)

    def call_bwd(nq, nk, nv, bias, do):
        def body(q_ref, k_ref, v_ref, b_ref, do_ref, dq_ref, dk_ref, dv_ref, db_ref, dk_acc, dv_acc):
            rb = pl.program_id(1)

            @pl.when(rb == 0)
            def _():
                dk_acc[...] = jnp.zeros_like(dk_acc)
                dv_acc[...] = jnp.zeros_like(dv_acc)
                db_ref[...] = jnp.zeros_like(db_ref)

            def step(j, c):
                args, where = [], []
                for u in range(NA_ROWS_PER_STEP):
                    i = j * NA_ROWS_PER_STEP + u
                    k0, d = window(rb * RB + i)
                    q0 = pl.multiple_of(i * GRID_W, GRID_W)
                    where.append((q0, k0, d))
                    args.append((q_ref[pl.ds(q0, GRID_W), :].astype(f32), k_ref[pl.ds(k0, nkeys), :], v_ref[pl.ds(k0, nkeys), :],
                                 [b_ref[h, pl.ds(d, 1)][0] for h in range(hp)], do_ref[pl.ds(q0, GRID_W), :].astype(f32)))
                for (q0, k0, d), (dq, dk, dv, dbs) in zip(where, _na_rows_bwd(args)):
                    dq_ref[pl.ds(q0, GRID_W), :] = dq.astype(dq_ref.dtype)
                    dk_acc[pl.ds(k0, nkeys), :] += dk
                    dv_acc[pl.ds(k0, nkeys), :] += dv
                    for h in range(hp):
                        db_ref[h, pl.ds(d, 1)] += dbs[h][None]
                return c

            lax.fori_loop(0, RB // NA_ROWS_PER_STEP, step, 0)

            @pl.when(rb == nrb - 1)
            def _():
                dk_ref[...] = dk_acc[...].astype(dk_ref.dtype)
                dv_ref[...] = dv_acc[...].astype(dv_ref.dtype)

        return pl.pallas_call(
            body,
            out_shape=[jax.ShapeDtypeStruct((S, W), nq.dtype), jax.ShapeDtypeStruct((S, W), nk.dtype),
                       jax.ShapeDtypeStruct((S, W), nv.dtype), jax.ShapeDtypeStruct(bias.shape, f32)],
            grid=(npair, nrb),
            in_specs=[qspec, kspec, kspec, bspec, qspec],
            out_specs=[qspec, kspec, kspec, bspec],
            scratch_shapes=[pltpu.VMEM((S, LANES), f32), pltpu.VMEM((S, LANES), f32)],
            compiler_params=_cparams(("parallel", "arbitrary")),
            name="na_bwd",
        )(nq, nk, nv, bias, do)

    @jax.custom_vjp
    def op(nq, nk, nv, bias):
        return call_fwd(nq, nk, nv, bias)

    def fwd(nq, nk, nv, bias):
        return call_fwd(nq, nk, nv, bias), (nq, nk, nv, bias)

    def bwd(res, do):
        return tuple(call_bwd(*res, do))

    op.defvjp(fwd, bwd)
    return op(nq, nk, nv, bias)


def ssd_dt(dt_raw, dt_bias, a_neg):
    def fn(r, b, a):
        dt = _softplus(r + b)
        return dt, dt * a

    return rowwise(fn, "ssd_dt", [dt_raw], [dt_bias, a_neg], [f32, f32])


def ssd_post(y, xs, z, d_skip_lanes, norm_g, groups):
    def fn(y, xs, z, dsk, g):
        y = (y + xs * dsk) * _silu(z)
        return (y * lax.rsqrt(jnp.mean(y * y, axis=-1, keepdims=True) + EPS) * g,)

    return rowwise(fn, "ssd_post", [y, xs, z], [d_skip_lanes.reshape(1, -1), norm_g.reshape(1, -1)], [MXU_DTYPE],
                   ncol=groups)[0]


def _heads_major(t, groups):
    S = t.shape[0]
    return jnp.transpose(t.reshape(S, groups, -1), (1, 0, 2))


def retention_na_mixer(hn, w_in, decay_logit, gn_g, rpb, w_out, tables):
    S = hn.shape[0]
    R = RET_HEADS * RET_DH
    NW = NA_HEADS * NA_DH
    cols = lambda a, b: w_in[:, a:b]
    rq, rk, rv, rg = (mm(hn, cols(j * R, (j + 1) * R)) for j in range(4))
    nq, nk, nv = (mm(hn, cols(4 * R + j * NW, 4 * R + (j + 1) * NW), out_dtype=MXU_DTYPE) for j in range(3))
    qr, kr = rotary(rq, rk, *tables)
    log_gamma = -_softplus(-decay_logit.astype(f32))
    pairs = R // LANES
    hp = LANES // RET_DH
    hpad = -(-RET_HEADS // 8) * 8
    pad8 = lambda a: jnp.pad(a.reshape(1, 1, RET_HEADS), ((0, 0), (0, 0), (0, hpad - RET_HEADS)))
    a_f = jnp.broadcast_to(pad8(log_gamma[0]), (1, S, hpad))
    a_b = jnp.broadcast_to(pad8(log_gamma[1]), (1, S, hpad))
    ret = ret_post(bidir_scan(qr, kr, rv, a_f, a_b, None, None, nsub=hp), rg, gn_g)
    rows = S // GRID_W
    nao = na_op(nq, nk, nv, _na_bias(rpb, min(NA_WIN_R, rows)))
    return mm(ret, w_out[:R]) + mm(nao, w_out[R:])


def ssd_mixer(hn, w_in, conv_w, conv_b, dt_bias, a_log, d_skip, norm_g, w_out):
    heads = d_skip.shape[0]
    inner = heads * SSD_HEADDIM
    gs = SSD_GROUPS * SSD_STATE
    o_x, o_b, o_c, o_dt = inner, 2 * inner, 2 * inner + gs, 2 * inner + 2 * gs
    z = mm(hn, w_in[:, :inner])
    dt_raw = mm(hn, w_in[:, o_dt:])
    xs, bm, cm = (mm_conv_act(hn, [w_in[:, a:b]], [conv_w[:, a - inner:b - inner]], [conv_b[a - inner:b - inner]],
                              _silu, _silu_bwd, f32, "conv_silu") for a, b in ((o_x, o_b), (o_b, o_c), (o_c, o_dt)))
    a_neg = -jnp.exp(a_log.astype(f32)).reshape(1, -1)
    dt, la = ssd_dt(dt_raw, dt_bias.astype(f32).reshape(1, -1), a_neg)
    dt_f, dt_b = _heads_major(dt[:, :heads], SSD_GROUPS), _heads_major(dt[:, heads:], SSD_GROUPS)
    la_f, la_b = _heads_major(la[:, :heads], SSD_GROUPS), _heads_major(la[:, heads:], SSD_GROUPS)
    y = bidir_scan(cm, bm, xs, la_f, la_b, dt_f, dt_b, nsub=1)
    y = ssd_post(y, xs, z, jnp.repeat(d_skip.astype(f32), SSD_HEADDIM), norm_g, SSD_GROUPS)
    return mm(y, w_out)


def conv_geglu_ffn(hf, w_up, conv_w, conv_b, w_down):
    F = w_down.shape[0]
    a = mm_conv_act(hf, [w_up[:, :F], w_up[:, F:]], [conv_w[:, :F], conv_w[:, F:]], [conv_b[:F], conv_b[F:]],
                    _geglu, _geglu_bwd, MXU_DTYPE, "conv_geglu")
    return mm(a, w_down)


def model_loss(x, tgt, big, small, rep):
    S = x.shape[0]
    depth = rep["norm_mix_pre"].shape[0]
    tables = _rope_tables(S, RET_HEADS * RET_DH)
    hn = rms(x, rep["norm_mix_pre"][0], MXU_DTYPE)
    for layer in range(depth):
        i = layer // 2
        if layer % 2 == 0:
            m = retention_na_mixer(hn, big["ab_w_in"][i], rep["ab_ret_decay_logit"][i], rep["ab_ret_gn_g"][i],
                                   rep["ab_na_rpb"][i], big["ab_w_out"][i], tables)
        else:
            m = ssd_mixer(hn, big["c_w_in"][i], small["c_conv_w"][i], small["c_conv_b"][i], rep["c_dt_bias"][i],
                          rep["c_a_log"][i], rep["c_d_skip"][i], small["c_norm_g"][i], big["c_w_out"][i])
        x, hf = rms_residual_norm(m, rep["norm_mix_post"][layer], x, rep["norm_ffn_pre"][layer])
        f = conv_geglu_ffn(hf, big["ffn_w_up"][layer], small["ffn_conv_w"][layer], rep["ffn_conv_b"][layer],
                           big["ffn_w_down"][layer])
        if layer + 1 < depth:
            x, hn = rms_residual_norm(f, rep["norm_ffn_post"][layer], x, rep["norm_mix_pre"][layer + 1])
        else:
            x = rms_residual(f, rep["norm_ffn_post"][layer], x)
    return loss_op(x, tgt)


def _mesh_pos():
    return lax.axis_index("x"), lax.axis_index("y"), lax.axis_index("c")


def gather_chips(local):
    R, Wd = local.shape

    half = R // 2
    CH = COPY_CHUNKS
    q = half // CH

    def body(x_ref, out_ref, send_sems, recv_sems):
        x, y, c = _mesh_pos()
        my = 2 * x + y
        chips = [(1 - x, y), (x, 1 - y), (1 - x, 1 - y)]

        def piece(ref, h, j):
            return ref.at[pl.ds(pl.multiple_of(h * half + j * q, PACK_ALIGN), q), :]

        def copy(k, src, chip, h, j, to):
            return pltpu.make_async_remote_copy(src_ref=src, dst_ref=piece(out_ref.at[chip], h, j), send_sem=send_sems.at[k],
                                                recv_sem=recv_sems.at[k], device_id=to, device_id_type=pl.DeviceIdType.MESH)

        first = [[copy(k * CH + j, piece(x_ref, c, j), my, c, j, (cx, cy, c)) for j in range(CH)]
                 for k, (cx, cy) in enumerate(chips)]
        for j in range(CH):
            for k in range(3):
                first[k][j].start()
        passed = [[copy((3 + k) * CH + j, piece(out_ref.at[2 * cx + cy], c, j), 2 * cx + cy, c, j, (x, y, 1 - c))
                   for j in range(CH)] for k, (cx, cy) in enumerate(chips)]
        for j in range(CH):
            for k, (cx, cy) in enumerate(chips):
                copy(k * CH + j, piece(x_ref, c, j), 2 * cx + cy, c, j, (cx, cy, c)).wait_recv()
                passed[k][j].start()
        for j in range(CH):
            for k, (cx, cy) in enumerate(chips):
                copy((3 + k) * CH + j, piece(x_ref, c, j), 2 * cx + cy, 1 - c, j, (x, y, 1 - c)).wait_recv()
        for k in range(3):
            for cp in first[k] + passed[k]:
                cp.wait_send()

    return pl.pallas_call(
        body,
        out_shape=jax.ShapeDtypeStruct((N_CHIPS, R, Wd), local.dtype),
        in_specs=[pl.BlockSpec(memory_space=pl.ANY)],
        out_specs=pl.BlockSpec(memory_space=pl.ANY),
        scratch_shapes=[pltpu.SemaphoreType.DMA((6 * CH,)), pltpu.SemaphoreType.DMA((6 * CH,))],
        name="gather_chips",
    )(local)


def pair_swap(parts):
    n, R, Wd = parts.shape
    half = R // 2

    CH = COPY_CHUNKS
    q = half // CH

    def body(p_ref, got_ref, send_sems, recv_sems):
        x, y, c = _mesh_pos()

        def src(s, j):
            return p_ref.at[s, pl.ds(pl.multiple_of((1 - c) * half + j * q, PACK_ALIGN), q), :]

        swap = [pltpu.make_async_remote_copy(src_ref=src(s, j), dst_ref=got_ref.at[s, pl.ds(j * q, q), :],
                                             send_sem=send_sems.at[s * CH + j], recv_sem=recv_sems.at[s * CH + j],
                                             device_id=(x, y, 1 - c), device_id_type=pl.DeviceIdType.MESH)
                for s in range(n) for j in range(CH)]
        for cp in swap:
            cp.start()
        for cp in swap:
            cp.wait()

    return pl.pallas_call(
        body,
        out_shape=jax.ShapeDtypeStruct((n, half, Wd), parts.dtype),
        in_specs=[pl.BlockSpec(memory_space=pl.ANY)],
        out_specs=pl.BlockSpec(memory_space=pl.ANY),
        scratch_shapes=[pltpu.SemaphoreType.DMA((n * CH,)), pltpu.SemaphoreType.DMA((n * CH,))],
        name="pair_swap",
    )(parts)


def chip_exchange(parts):
    n, R, Wd = parts.shape

    def body(p_ref, out_ref, send_sems, recv_sems):
        x, y, c = _mesh_pos()
        my = 2 * x + y
        chips = [(1 - x, y), (x, 1 - y), (1 - x, 1 - y)]

        def copy(k, src_slot, dst_slot, to):
            return pltpu.make_async_remote_copy(src_ref=p_ref.at[src_slot], dst_ref=out_ref.at[dst_slot], send_sem=send_sems.at[k],
                                                recv_sem=recv_sems.at[k], device_id=to, device_id_type=pl.DeviceIdType.MESH)

        sends = [copy(k, 2 * cx + cy, my, (cx, cy, c)) for k, (cx, cy) in enumerate(chips)]
        for cp in sends:
            cp.start()
        for k, (cx, cy) in enumerate(chips):
            copy(k, my, 2 * cx + cy, (cx, cy, c)).wait_recv()
        for cp in sends:
            cp.wait_send()

    return pl.pallas_call(
        body,
        out_shape=jax.ShapeDtypeStruct((n, R, Wd), parts.dtype),
        in_specs=[pl.BlockSpec(memory_space=pl.ANY)],
        out_specs=pl.BlockSpec(memory_space=pl.ANY),
        scratch_shapes=[pltpu.SemaphoreType.DMA((3,)), pltpu.SemaphoreType.DMA((3,))],
        name="chip_exchange",
    )(parts)


def pair_share(mine):
    R, Wd = mine.shape

    CH = 2 * COPY_CHUNKS
    q = R // CH

    def body(m_ref, out_ref, send_sems, recv_sems):
        x, y, c = _mesh_pos()
        swap = [pltpu.make_async_remote_copy(src_ref=m_ref.at[pl.ds(j * q, q), :], dst_ref=out_ref.at[pl.ds(j * q, q), :],
                                             send_sem=send_sems.at[j], recv_sem=recv_sems.at[j], device_id=(x, y, 1 - c),
                                             device_id_type=pl.DeviceIdType.MESH) for j in range(CH)]
        for cp in swap:
            cp.start()
        for cp in swap:
            cp.wait()

    return pl.pallas_call(
        body,
        out_shape=jax.ShapeDtypeStruct((R, Wd), mine.dtype),
        in_specs=[pl.BlockSpec(memory_space=pl.ANY)],
        out_specs=pl.BlockSpec(memory_space=pl.ANY),
        scratch_shapes=[pltpu.SemaphoreType.DMA((CH,)), pltpu.SemaphoreType.DMA((CH,))],
        name="pair_share",
    )(mine)


def sum_chips(recv, own):
    n, R, Wd = recv.shape
    tr = _pick(R, (512, 256, 128, 64, 32, 16, 8))

    def body(r_ref, p_ref, o_ref):
        my = 2 * lax.axis_index("x") + lax.axis_index("y")
        acc = jnp.zeros((tr, Wd), f32)
        for s in range(n):
            acc = acc + jnp.where(my == s, p_ref[s], r_ref[s]).astype(f32)
        o_ref[...] = acc

    spec = pl.BlockSpec((n, tr, Wd), lambda i: (0, i, 0))
    return pl.pallas_call(
        body,
        out_shape=jax.ShapeDtypeStruct((R, Wd), f32),
        grid=(R // tr,),
        in_specs=[spec, spec],
        out_specs=pl.BlockSpec((tr, Wd), lambda i: (i, 0)),
        compiler_params=_cparams(("parallel",)),
        name="sum_chips",
    )(recv, own)


def add_pair(parts, got):
    n, R, Wd = parts.shape
    half = R // 2
    tr = _pick(half, (512, 256, 128, 64, 32, 16, 8))
    nb = half // tr

    def body(lo_ref, hi_ref, g_ref, o_ref):
        mine = jnp.where(lax.axis_index("c") == 0, lo_ref[...], hi_ref[...])
        o_ref[...] = (mine.astype(f32) + g_ref[...].astype(f32)).astype(o_ref.dtype)

    spec = pl.BlockSpec((1, tr, Wd), lambda s, i: (s, i, 0))
    return pl.pallas_call(
        body,
        out_shape=jax.ShapeDtypeStruct(got.shape, parts.dtype),
        grid=(n, nb),
        in_specs=[spec, pl.BlockSpec((1, tr, Wd), lambda s, i: (s, nb + i, 0)), spec],
        out_specs=spec,
        compiler_params=_cparams(("parallel", "parallel")),
        name="add_pair",
    )(parts, parts, got)


def reduce_scatter(parts):
    chip_sum = add_pair(parts, pair_swap(parts))
    mine = sum_chips(chip_exchange(chip_sum), chip_sum)
    theirs = pair_share(mine)
    first = lax.axis_index("c") == 0
    return jnp.concatenate([jnp.where(first, mine, theirs), jnp.where(first, theirs, mine)], axis=0)


def adamw(w, g, m, v):
    R, C = w.shape
    tr = R
    for cand in (512, 256, 128, 64, 32, 16, 8):
        if R * C * 4 > (1 << 20) and R % cand == 0 and cand * C * 4 <= (1 << 20):
            tr = cand
            break

    def body(w_ref, g_ref, m_ref, v_ref, d_ref, mo_ref, vo_ref):
        g = g_ref[...]
        m = ADAM_B1 * m_ref[...] + (1.0 - ADAM_B1) * g
        v = ADAM_B2 * v_ref[...] + (1.0 - ADAM_B2) * (g * g)
        m_hat = m / (1.0 - ADAM_B1 ** ADAM_STEP)
        v_hat = v / (1.0 - ADAM_B2 ** ADAM_STEP)
        d_ref[...] = -ADAM_LR * (m_hat / (jnp.sqrt(v_hat) + ADAM_EPS) + ADAM_WD * w_ref[...])
        mo_ref[...] = m
        vo_ref[...] = v

    spec = pl.BlockSpec((tr, C), lambda i: (i, 0))
    return pl.pallas_call(
        body,
        out_shape=[jax.ShapeDtypeStruct((R, C), f32)] * 3,
        grid=(R // tr,),
        in_specs=[spec] * 4,
        out_specs=[spec] * 3,
        compiler_params=_cparams(("parallel",)),
        name="adamw",
    )(w, g, m, v)


def _pack(arrs, dtype):
    flat = jnp.concatenate([a.astype(dtype).reshape(-1) for a in arrs])
    n = flat.shape[0]
    unit = PACK_W * PACK_ROWS
    padded = -(-n // unit) * unit
    return jnp.pad(flat, (0, padded - n)).reshape(-1, PACK_W)


def _unpack(buf, shapes):
    flat = buf.reshape(-1)
    out, off = [], 0
    for s in shapes:
        n = int(np.prod(s))
        out.append(flat[off:off + n].reshape(s))
        off += n
    return out


BIG = (("ab_w_in", 2), ("ab_w_out", 1), ("c_w_in", 2), ("c_w_out", 1), ("ffn_w_up", 2), ("ffn_w_down", 1))
SMALL = (("c_conv_w", 2), ("c_conv_b", 1), ("c_norm_g", 1), ("ffn_conv_w", 2))
REP = ("norm_mix_pre", "norm_mix_post", "norm_ffn_pre", "norm_ffn_post", "ab_ret_decay_logit", "ab_ret_gn_g", "ab_na_rpb",
       "c_dt_bias", "c_a_log", "c_d_skip", "ffn_conv_b")
WEIGHTS = ("norm_mix_pre", "norm_mix_post", "norm_ffn_pre", "norm_ffn_post", "ab_w_in", "ab_ret_decay_logit", "ab_ret_gn_g",
           "ab_na_rpb", "ab_w_out", "c_w_in", "c_conv_w", "c_conv_b", "c_dt_bias", "c_a_log", "c_d_skip", "c_norm_g", "c_w_out",
           "ffn_w_up", "ffn_conv_w", "ffn_conv_b", "ffn_w_down")


def _gather_set(local, spec, dtype):
    shapes = [local[n].shape for n, _ in spec]
    got = gather_chips(_pack([local[n] for n, _ in spec], dtype))
    my = 2 * lax.axis_index("x") + lax.axis_index("y")
    per_chip = [_unpack(got[s], shapes) for s in range(N_CHIPS)]
    return {n: jnp.concatenate([jnp.where(my == s, local[n].astype(dtype), per_chip[s][j]) for s in range(N_CHIPS)], axis=ax)
            for j, (n, ax) in enumerate(spec)}


def _scatter_parts(full, spec, extra, dtype):
    split = {n: jnp.split(full[n], N_CHIPS, axis=ax) for n, ax in spec}
    return jnp.stack([_pack([split[n][s] for n, _ in spec] + list(extra), dtype) for s in range(N_CHIPS)])


def kernel(x, norm_mix_pre, norm_mix_post, norm_ffn_pre, norm_ffn_post, ab_w_in, ab_ret_decay_logit, ab_ret_gn_g, ab_na_rpb, ab_w_out, c_w_in, c_conv_w, c_conv_b, c_dt_bias, c_a_log, c_d_skip, c_norm_g, c_w_out, ffn_w_up, ffn_conv_w, ffn_conv_b, ffn_w_down, loss_target, m_norm_mix_pre, m_norm_mix_post, m_norm_ffn_pre, m_norm_ffn_post, m_ab_w_in, m_ab_ret_decay_logit, m_ab_ret_gn_g, m_ab_na_rpb, m_ab_w_out, m_c_w_in, m_c_conv_w, m_c_conv_b, m_c_dt_bias, m_c_a_log, m_c_d_skip, m_c_norm_g, m_c_w_out, m_ffn_w_up, m_ffn_conv_w, m_ffn_conv_b, m_ffn_w_down, v_norm_mix_pre, v_norm_mix_post, v_norm_ffn_pre, v_norm_ffn_post, v_ab_w_in, v_ab_ret_decay_logit, v_ab_ret_gn_g, v_ab_na_rpb, v_ab_w_out, v_c_w_in, v_c_conv_w, v_c_conv_b, v_c_dt_bias, v_c_a_log, v_c_d_skip, v_c_norm_g, v_c_w_out, v_ffn_w_up, v_ffn_conv_w, v_ffn_conv_b, v_ffn_w_down):
    args = dict(locals())
    w = {n: args[n] for n in WEIGHTS}
    mom = {n: args["m_" + n] for n in WEIGHTS}
    var = {n: args["v_" + n] for n in WEIGHTS}

    big = _gather_set(w, BIG, MXU_DTYPE)
    small = _gather_set(w, SMALL, f32)
    rep = {n: w[n] for n in REP}

    def loss_fn(xs, big, small, rep):
        return model_loss(xs, loss_target[0], big, small, rep)

    loss, (gx, gbig, gsmall, grep) = jax.value_and_grad(loss_fn, argnums=(0, 1, 2, 3))(x[0], big, small, rep)
    loss = lax.psum(loss, ("x", "y", "c"))

    big_shapes = [w[n].shape for n, _ in BIG]
    small_shapes = [w[n].shape for n, _ in SMALL] + [w[n].shape for n in REP]
    g_big = _unpack(reduce_scatter(_scatter_parts(gbig, BIG, (), MXU_DTYPE)), big_shapes)
    g_small_buf = reduce_scatter(_scatter_parts(gsmall, SMALL, [grep[n] for n in REP], f32))
    grads = dict(zip([n for n, _ in BIG], g_big))
    small_names = [n for n, _ in SMALL] + list(REP)
    grads.update(zip(small_names, _unpack(g_small_buf, small_shapes)))

    delta, new_m, new_v = {}, {}, {}
    for n in WEIGHTS:
        shp = w[n].shape
        two_d = lambda a: a.reshape(-1, shp[-1])
        d, m2, v2 = adamw(two_d(w[n]), two_d(grads[n]), two_d(mom[n]), two_d(var[n]))
        delta[n], new_m[n], new_v[n] = d.reshape(shp), m2.reshape(shp), v2.reshape(shp)

    return (loss, gx[None], *[grads[n] for n in WEIGHTS], *[delta[n] for n in WEIGHTS],
            *[new_m[n] for n in WEIGHTS], *[new_v[n] for n in WEIGHTS])
```

```python
import functools
import math

import numpy as np
import jax
import jax.numpy as jnp
from jax import lax
from jax.experimental import pallas as pl
from jax.experimental.pallas import tpu as pltpu

f32 = jnp.float32
bf16 = jnp.bfloat16
MXU_DTYPE = bf16

GRID_W = 64
CHUNK = 128
EPS = 1e-6
RET_HEADS = 8
RET_DH = 64
ROPE_BASE = 10000.0
NA_HEADS = 8
NA_DH = 64
NA_WIN_R = 8
NA_WIN_C = 16
NA_ROWS_PER_STEP = 4
SSD_HEADDIM = 64
SSD_GROUPS = 4
SSD_STATE = 128
ADAM_LR = 0.001
ADAM_B1 = 0.9
ADAM_B2 = 0.999
ADAM_EPS = 1e-08
ADAM_WD = 0.01
ADAM_STEP = 10

LANES = 128
HEAD_W = 64
PACK_W = 512
PACK_ROWS = 1024
PACK_ALIGN = 16
COPY_CHUNKS = 4
VMEM_LIMIT = 56 * 1024 * 1024
MM_BLOCK_BYTES = 6 * 1024 * 1024
N_CHIPS = 4
N_DEV = 8
NEG_INF = -1e30

_DIMS = {"nn": (((1,), (0,)), ((), ())), "nt": (((1,), (1,)), ((), ())), "tn": (((0,), (0,)), ((), ()))}


def _cparams(sem=None):
    return pltpu.CompilerParams(dimension_semantics=sem, vmem_limit_bytes=VMEM_LIMIT)


def _pick(dim, cands):
    for c in cands:
        if dim % c == 0:
            return c
    return dim


def _divisor_tile(dim, fits, align):
    for d in range(1, dim + 1):
        t = dim // d
        if dim % d == 0 and t % align == 0 and fits(t):
            return t
    return dim


def _bdot_raw(a, b, mode):
    return lax.dot_general(a.astype(MXU_DTYPE), b.astype(MXU_DTYPE), _DIMS[mode], preferred_element_type=f32)


@functools.partial(jax.custom_vjp, nondiff_argnums=(2,))
def bdot(a, b, mode):
    return _bdot_raw(a, b, mode)


def _bdot_fwd(a, b, mode):
    return _bdot_raw(a, b, mode), (a, b)


def _bdot_bwd(mode, res, g):
    a, b = res
    if mode == "nn":
        da, db = _bdot_raw(g, b, "nt"), _bdot_raw(a, g, "tn")
    elif mode == "nt":
        da, db = _bdot_raw(g, b, "nn"), _bdot_raw(g, a, "tn")
    else:
        da, db = _bdot_raw(b, g, "nt"), _bdot_raw(a, g, "nn")
    return da.astype(a.dtype), db.astype(b.dtype)


bdot.defvjp(_bdot_fwd, _bdot_bwd)


def _mm_call(a, b, mode, out_dtype):
    if mode == "nn":
        (M, K), (K2, N) = a.shape, b.shape
    elif mode == "nt":
        (M, K), (N, K2) = a.shape, b.shape
    else:
        (K, M), (K2, N) = a.shape, b.shape
    assert K == K2, (a.shape, b.shape, mode)
    a_bytes, b_bytes, o_bytes = a.dtype.itemsize, b.dtype.itemsize, jnp.dtype(out_dtype).itemsize
    if mode == "tn":
        tn = _divisor_tile(N, lambda t: t <= 1536, LANES)
        tm = _divisor_tile(M, lambda t: t * tn * 4 <= MM_BLOCK_BYTES, 8)
        tk = _divisor_tile(K, lambda t: t * tm * a_bytes <= MM_BLOCK_BYTES and t * tn * b_bytes <= MM_BLOCK_BYTES, LANES)
    else:
        tk, tn = K, N
        tm = _divisor_tile(M, lambda t: t * K * a_bytes <= MM_BLOCK_BYTES and t * N * o_bytes <= MM_BLOCK_BYTES, 8)
    nk = K // tk
    if mode == "nn":
        a_spec = pl.BlockSpec((tm, tk), lambda i, j, k: (i, k))
        b_spec = pl.BlockSpec((tk, tn), lambda i, j, k: (k, j))
    elif mode == "nt":
        a_spec = pl.BlockSpec((tm, tk), lambda i, j, k: (i, k))
        b_spec = pl.BlockSpec((tn, tk), lambda i, j, k: (j, k))
    else:
        a_spec = pl.BlockSpec((tk, tm), lambda i, j, k: (k, i))
        b_spec = pl.BlockSpec((tk, tn), lambda i, j, k: (k, j))

    if nk == 1:
        def body(a_ref, b_ref, o_ref):
            o_ref[...] = _bdot_raw(a_ref[...], b_ref[...], mode).astype(o_ref.dtype)
    else:
        def body(a_ref, b_ref, o_ref, acc_ref):
            k = pl.program_id(2)

            @pl.when(k == 0)
            def _():
                acc_ref[...] = jnp.zeros_like(acc_ref)

            acc_ref[...] += _bdot_raw(a_ref[...], b_ref[...], mode)

            @pl.when(k == nk - 1)
            def _():
                o_ref[...] = acc_ref[...].astype(o_ref.dtype)

    return pl.pallas_call(
        body,
        out_shape=jax.ShapeDtypeStruct((M, N), out_dtype),
        grid=(M // tm, N // tn, nk),
        in_specs=[a_spec, b_spec],
        out_specs=pl.BlockSpec((tm, tn), lambda i, j, k: (i, j)),
        scratch_shapes=[pltpu.VMEM((tm, tn), f32)] if nk > 1 else [],
        compiler_params=_cparams(("parallel", "parallel", "arbitrary")),
        name="mm_" + mode,
    )(a, b)


def mm(a, b, mode="nn", out_dtype=f32):
    @jax.custom_vjp
    def op(a, b):
        return _mm_call(a, b, mode, out_dtype)

    def fwd(a, b):
        return _mm_call(a, b, mode, out_dtype), (a, b)

    def bwd(res, g):
        a, b = res
        if mode == "nn":
            return _mm_call(g, b, "nt", a.dtype), _mm_call(a, g, "tn", b.dtype)
        if mode == "nt":
            return _mm_call(g, b, "nn", a.dtype), _mm_call(g, a, "tn", b.dtype)
        return _mm_call(b, g, "nt", a.dtype), _mm_call(a, g, "nn", b.dtype)

    op.defvjp(fwd, bwd)
    return op(a, b)


def _row_tile(S, row_bytes):
    tm = 512
    while tm > 8 and (tm * row_bytes > (6 << 20) or S % tm):
        tm //= 2
    return tm


def rowwise(fn, name, rows, params, out_dtypes, n_diff_rows=None, n_diff_params=None, ncol=1, bwd_fn=None):
    rows, params = list(rows), list(params)
    nr, npar = len(rows), len(params)
    ndr = nr if n_diff_rows is None else n_diff_rows
    ndp = npar if n_diff_params is None else n_diff_params
    S = rows[0].shape[0]
    rw = [r.shape[1] // ncol for r in rows]
    pshape = [(p.shape[0], p.shape[1] // ncol) for p in params]

    def block_structs(tm):
        return ([jax.ShapeDtypeStruct((tm, w), f32) for w in rw] + [jax.ShapeDtypeStruct(s, f32) for s in pshape])

    outs_s = jax.eval_shape(fn, *block_structs(8))
    ow = [o.shape[1] for o in outs_s]
    nout = len(ow)
    row_bytes = 4 * (sum(rw) * 2 + sum(ow) * 2)
    tm = _row_tile(S, row_bytes)
    grid = (ncol, S // tm)

    def rspec(w):
        return pl.BlockSpec((tm, w), lambda g, i: (i, g))

    def pspec(s):
        return pl.BlockSpec(s, lambda g, i: (0, g))

    def call_fwd(*args):
        def body(*refs):
            vals = [r[...].astype(f32) for r in refs[:nr + npar]]
            res = fn(*vals)
            for o, r in zip(refs[nr + npar:], res):
                o[...] = r.astype(o.dtype)

        return pl.pallas_call(
            body,
            out_shape=[jax.ShapeDtypeStruct((S, w * ncol), dt) for w, dt in zip(ow, out_dtypes)],
            grid=grid,
            in_specs=[rspec(w) for w in rw] + [pspec(s) for s in pshape],
            out_specs=[rspec(w) for w in ow],
            compiler_params=_cparams(("parallel", "parallel")),
            name=name + "_fwd",
        )(*args)

    def call_bwd(args, douts):
        def body(*refs):
            in_refs = refs[:nr + npar]
            do_refs = refs[nr + npar:nr + npar + nout]
            dr_refs = refs[nr + npar + nout:nr + npar + nout + ndr]
            dp_refs = refs[nr + npar + nout + ndr:]
            rv = [r[...] for r in in_refs[:nr]]
            pv = [r[...] for r in in_refs[nr:]]
            dos = [d[...].astype(f32) for d in do_refs]
            if bwd_fn is not None:
                drs, dps = bwd_fn(rv, pv, dos)
            else:
                def f(*a):
                    return fn(*a[:ndr], *rv[ndr:], *a[ndr:], *pv[ndp:])

                _, vjp = jax.vjp(f, *[v.astype(f32) for v in rv[:ndr]], *pv[:ndp])
                cts = vjp(tuple(dos))
                drs, dps = cts[:ndr], cts[ndr:]
            for r, ct in zip(dr_refs, drs):
                r[...] = ct.astype(r.dtype)
            if ndp:
                @pl.when(pl.program_id(1) == 0)
                def _():
                    for r in dp_refs:
                        r[...] = jnp.zeros_like(r)

                for r, ct in zip(dp_refs, dps):
                    r[...] += ct

        return pl.pallas_call(
            body,
            out_shape=[jax.ShapeDtypeStruct(r.shape, r.dtype) for r in rows[:ndr]]
            + [jax.ShapeDtypeStruct(p.shape, f32) for p in params[:ndp]],
            grid=grid,
            in_specs=[rspec(w) for w in rw] + [pspec(s) for s in pshape] + [rspec(w) for w in ow],
            out_specs=[rspec(w) for w in rw[:ndr]] + [pspec(s) for s in pshape[:ndp]],
            compiler_params=_cparams(("parallel", "arbitrary")),
            name=name + "_bwd",
        )(*args, *douts)

    @jax.custom_vjp
    def op(*args):
        return tuple(call_fwd(*args))

    def fwd(*args):
        return tuple(call_fwd(*args)), args

    def bwd(args, douts):
        res = call_bwd(args, douts)
        drs, dps = res[:ndr], res[ndr:]
        out = list(drs) + [jnp.zeros_like(a) for a in args[ndr:nr]]
        out += [dp.astype(p.dtype) for dp, p in zip(dps, args[nr:nr + ndp])]
        out += [jnp.zeros_like(a) for a in args[nr + ndp:]]
        return tuple(out)

    op.defvjp(fwd, bwd)
    return op(*rows, *params)


def _silu(x):
    return x * (1.0 / (1.0 + jnp.exp(-x)))


def _softplus(x):
    return jnp.maximum(x, 0.0) + jnp.log(1.0 + jnp.exp(-jnp.abs(x)))


def _gelu_tanh(x):
    return 0.5 * x * (1.0 + jnp.tanh(math.sqrt(2.0 / math.pi) * (x + 0.044715 * (x * x * x))))


def _rms_fn(x, g):
    return x * lax.rsqrt(jnp.mean(x * x, axis=-1, keepdims=True) + EPS) * g


def _rms_bwd(x, g, dy):
    r = lax.rsqrt(jnp.mean(x * x, axis=-1, keepdims=True) + EPS)
    xh = x * r
    dxh = dy * g
    dx = r * (dxh - xh * jnp.mean(dxh * xh, axis=-1, keepdims=True))
    return dx, jnp.sum(dy * xh, axis=0, keepdims=True)


def rms(x, g, out_dtype):
    def bwd_fn(rv, pv, dos):
        dx, dg = _rms_bwd(rv[0], pv[0], dos[0])
        return (dx,), (dg,)

    return rowwise(lambda x, g: (_rms_fn(x, g),), "rms", [x], [g.reshape(1, -1)], [out_dtype], bwd_fn=bwd_fn)[0]


def rms_residual_norm(m, g, x, g_next):
    def fn(m, x, g, gn):
        xn = x + _rms_fn(m, g)
        return xn, _rms_fn(xn, gn)

    def bwd_fn(rv, pv, dos):
        (m, x), (g, gn), (dxn, dhn) = rv, pv, dos
        xn = x + _rms_fn(m, g)
        d_from_norm, dgn = _rms_bwd(xn, gn, dhn)
        dxn = dxn + d_from_norm
        dm, dg = _rms_bwd(m, g, dxn)
        return (dm, dxn), (dg, dgn)

    return rowwise(fn, "rms_res_norm", [m, x], [g.reshape(1, -1), g_next.reshape(1, -1)], [f32, MXU_DTYPE], bwd_fn=bwd_fn)


def rms_residual(m, g, x):
    def bwd_fn(rv, pv, dos):
        dm, dg = _rms_bwd(rv[0], pv[0], dos[0])
        return (dm, dos[0]), (dg,)

    return rowwise(lambda m, x, g: (x + _rms_fn(m, g),), "rms_res", [m, x], [g.reshape(1, -1)], [f32], bwd_fn=bwd_fn)[0]


def loss_op(y, tgt):
    S, D = y.shape
    tm = _row_tile(S, 4 * D * 4)

    def call_fwd(y, tgt):
        def body(y_ref, t_ref, o_ref):
            @pl.when(pl.program_id(0) == 0)
            def _():
                o_ref[...] = jnp.zeros_like(o_ref)

            e = y_ref[...] - t_ref[...]
            o_ref[...] += 0.5 * jnp.sum(jnp.mean(e * e, axis=-1, keepdims=True))

        out = pl.pallas_call(
            body,
            out_shape=jax.ShapeDtypeStruct((8, LANES), f32),
            grid=(S // tm,),
            in_specs=[pl.BlockSpec((tm, D), lambda i: (i, 0))] * 2,
            out_specs=pl.BlockSpec((8, LANES), lambda i: (0, 0)),
            compiler_params=_cparams(("arbitrary",)),
            name="loss_fwd",
        )(y, tgt)
        return out[0, 0]

    def call_bwd(y, tgt, g):
        def body(y_ref, t_ref, g_ref, o_ref):
            o_ref[...] = (y_ref[...] - t_ref[...]) * (g_ref[...] * (1.0 / D))

        return pl.pallas_call(
            body,
            out_shape=jax.ShapeDtypeStruct((S, D), f32),
            grid=(S // tm,),
            in_specs=[pl.BlockSpec((tm, D), lambda i: (i, 0))] * 2 + [pl.BlockSpec((1, 1), lambda i: (0, 0))],
            out_specs=pl.BlockSpec((tm, D), lambda i: (i, 0)),
            compiler_params=_cparams(("parallel",)),
            name="loss_bwd",
        )(y, tgt, g.reshape(1, 1).astype(f32))

    @jax.custom_vjp
    def op(y, tgt):
        return call_fwd(y, tgt)

    def fwd(y, tgt):
        return call_fwd(y, tgt), (y, tgt)

    def bwd(res, g):
        y, tgt = res
        return call_bwd(y, tgt, g), jnp.zeros_like(tgt)

    op.defvjp(fwd, bwd)
    return op(y, tgt)


HALO = 8


def _conv_tile(S, R):
    def ext(ref, r0):
        cur = ref[pl.ds(r0, R), :]
        prev = ref[pl.ds(pl.multiple_of(jnp.maximum(r0 - HALO, 0), HALO), HALO), :]
        nxt = ref[pl.ds(pl.multiple_of(jnp.minimum(r0 + R, S - HALO), HALO), HALO), :]
        prev = jnp.where(r0 > 0, prev, 0.0)
        nxt = jnp.where(r0 + R < S, nxt, 0.0)
        return jnp.concatenate([prev, cur, nxt], axis=0)

    return ext


def _shift_rows(e, k, R):
    n = e.shape[0]
    if k == 0:
        return e[HALO:HALO + R]
    return pltpu.roll(e, (-k) % n, 0)[HALO:HALO + R]


def _silu_bwd(us, dy):
    u, = us
    s = 1.0 / (1.0 + jnp.exp(-u))
    return (dy * (s * (1.0 + u * (1.0 - s))),)


def _geglu(g, v):
    return _gelu_tanh(g) * v


def _geglu_bwd(us, dy):
    g, v = us
    c = math.sqrt(2.0 / math.pi)
    t = jnp.tanh(c * (g + 0.044715 * (g * g * g)))
    half = 0.5 * (1.0 + t)
    dgelu = half + 0.5 * g * (1.0 - t * t) * (c * (1.0 + 3.0 * 0.044715 * (g * g)))
    return dy * v * dgelu, dy * (g * half)


def mm_conv_act(h, ws, cws, cbs, act, act_bwd, out_dtype, name):
    n = len(ws)
    S = h.shape[0]
    C = ws[0].shape[1]
    W = cws[0].shape[0]
    pad = W // 2
    bw = _pick(C, (LANES,))
    R = _pick(S, (256, 128, 64, 32, 16, 8))
    nt = S // R
    ext = _conv_tile(S, R)
    col = lambda rows: pl.BlockSpec((rows, bw), lambda j: (0, j))

    def conv(e, wv, bv):
        acc = bv + wv[pad] * e[HALO:HALO + R]
        for j in range(W):
            if j != pad:
                acc = acc + wv[j] * _shift_rows(e, j - pad, R)
        return acc

    def call_fwd(xs, cws, cbs):
        def body(*refs):
            x_refs, w_refs, b_refs, y_ref = refs[:n], refs[n:2 * n], refs[2 * n:3 * n], refs[3 * n]
            wvs = [[w[j:j + 1, :] for j in range(W)] for w in w_refs]
            bvs = [b[...] for b in b_refs]

            def tile(i, c):
                r0 = pl.multiple_of(i * R, R)
                us = [conv(ext(x, r0), wv, bv) for x, wv, bv in zip(x_refs, wvs, bvs)]
                y_ref[pl.ds(r0, R), :] = act(*us).astype(y_ref.dtype)
                return c

            lax.fori_loop(0, nt, tile, 0)

        return pl.pallas_call(
            body,
            out_shape=jax.ShapeDtypeStruct((S, C), out_dtype),
            grid=(C // bw,),
            in_specs=[col(S)] * n + [col(W)] * n + [col(1)] * n,
            out_specs=col(S),
            compiler_params=_cparams(("parallel",)),
            name=name + "_fwd",
        )(*xs, *cws, *cbs)

    def call_bwd(xs, cws, cbs, dy):
        def body(*refs):
            x_refs, w_refs, b_refs, dy_ref = refs[:n], refs[n:2 * n], refs[2 * n:3 * n], refs[3 * n]
            dx_refs, dw_refs, db_refs = refs[3 * n + 1:4 * n + 1], refs[4 * n + 1:5 * n + 1], refs[5 * n + 1:6 * n + 1]
            du_scr = refs[6 * n + 1:]
            wvs = [[w[j:j + 1, :] for j in range(W)] for w in w_refs]
            bvs = [b[...] for b in b_refs]
            zero = jnp.zeros((1, bw), f32)

            def first(i, dbs):
                r0 = pl.multiple_of(i * R, R)
                us = [conv(ext(x, r0), wv, bv) for x, wv, bv in zip(x_refs, wvs, bvs)]
                dus = act_bwd(us, dy_ref[pl.ds(r0, R), :].astype(f32))
                for scr, du in zip(du_scr, dus):
                    scr[pl.ds(r0, R), :] = du
                return tuple(db + jnp.sum(du, axis=0, keepdims=True) for db, du in zip(dbs, dus))

            dbs = lax.fori_loop(0, nt, first, tuple(zero for _ in range(n)))

            def second(i, dws):
                r0 = pl.multiple_of(i * R, R)
                new = []
                for x, scr, dx, wv, dw in zip(x_refs, du_scr, dx_refs, wvs, dws):
                    ex, ed = ext(x, r0), ext(scr, r0)
                    d0 = ed[HALO:HALO + R]
                    acc = jnp.zeros((R, bw), f32)
                    row = []
                    for j in range(W):
                        acc = acc + wv[j] * _shift_rows(ed, pad - j, R)
                        row.append(dw[j] + jnp.sum(d0 * _shift_rows(ex, j - pad, R), axis=0, keepdims=True))
                    dx[pl.ds(r0, R), :] = acc.astype(dx.dtype)
                    new.append(tuple(row))
                return tuple(new)

            dws = lax.fori_loop(0, nt, second, tuple(tuple(zero for _ in range(W)) for _ in range(n)))
            for dw_ref, db_ref, dw, db in zip(dw_refs, db_refs, dws, dbs):
                dw_ref[...] = jnp.zeros_like(dw_ref)
                for j in range(W):
                    dw_ref[j:j + 1, :] = dw[j]
                db_ref[...] = db

        return pl.pallas_call(
            body,
            out_shape=[jax.ShapeDtypeStruct((S, C), MXU_DTYPE)] * n + [jax.ShapeDtypeStruct((8, C), f32)] * n
            + [jax.ShapeDtypeStruct((1, C), f32)] * n,
            grid=(C // bw,),
            in_specs=[col(S)] * n + [col(W)] * n + [col(1)] * n + [col(S)],
            out_specs=[col(S)] * n + [col(8)] * n + [col(1)] * n,
            scratch_shapes=[pltpu.VMEM((S, bw), f32)] * n,
            compiler_params=_cparams(("parallel",)),
            name=name + "_bwd",
        )(*xs, *cws, *cbs, dy)

    @jax.custom_vjp
    def op(h, ws, cws, cbs):
        return call_fwd([_mm_call(h, w, "nn", f32) for w in ws], cws, cbs)

    def fwd(h, ws, cws, cbs):
        xs = [_mm_call(h, w, "nn", f32) for w in ws]
        return call_fwd(xs, cws, cbs), (h, ws, xs, cws, cbs)

    def bwd(res, dy):
        h, ws, xs, cws, cbs = res
        out = call_bwd(xs, cws, cbs, dy)
        dxs, dcws, dcbs = out[:n], out[n:2 * n], out[2 * n:]
        dh = _mm_call(dxs[0], ws[0], "nt", h.dtype)
        for dx, w in zip(dxs[1:], ws[1:]):
            dh = dh + _mm_call(dx, w, "nt", h.dtype)
        dws = tuple(_mm_call(h, dx, "tn", w.dtype) for dx, w in zip(dxs, ws))
        return dh, dws, tuple(d[:W] for d in dcws), tuple(dcbs)

    op.defvjp(fwd, bwd)
    return op(h, tuple(ws), tuple(cws), tuple(b.reshape(1, C) for b in cbs))


@jax.custom_vjp
def _masked_decay(cs_col, cs_row, mask01):
    return jnp.where(mask01 > 0, jnp.exp(cs_col - cs_row), 0.0)


def _masked_decay_fwd(cs_col, cs_row, mask01):
    d = jnp.where(mask01 > 0, jnp.exp(cs_col - cs_row), 0.0)
    return d, (d, mask01)


def _masked_decay_bwd(res, g):
    d, mask01 = res
    t = g * d
    return jnp.sum(t, axis=1, keepdims=True), -jnp.sum(t, axis=0, keepdims=True), jnp.zeros_like(mask01)


_masked_decay.defvjp(_masked_decay_fwd, _masked_decay_bwd)


def _scan_chunk(qs, ks, xs, cs_tok, dt_tok, hs, *, rev, incl, nsub):
    nb = len(xs)
    L, N = qs[0].shape
    W = xs[0].shape[1]
    Hg = cs_tok.shape[1]
    nh = W // HEAD_W
    shared = len(qs) == 1
    t = lax.broadcasted_iota(jnp.int32, (L, L), 0)
    l = lax.broadcasted_iota(jnp.int32, (L, L), 1)
    if rev:
        mask = (l >= t) if incl else (l > t)
    else:
        mask = (l <= t) if incl else (l < t)
    mask01 = mask.astype(f32)
    lane_a = lax.broadcasted_iota(jnp.int32, cs_tok.shape, 1)
    row_a = lax.broadcasted_iota(jnp.int32, (Hg, L), 0)
    last = lax.broadcasted_iota(jnp.int32, (1, L), 1) == (0 if rev else L - 1)
    vhead = lax.broadcasted_iota(jnp.int32, (1, W), 1) // HEAD_W
    qhead = lax.broadcasted_iota(jnp.int32, (1, N), 1) // (N // nsub)
    cs_rows = lax.dot_general(cs_tok, (t == l).astype(f32), _DIMS["tn"], precision=lax.Precision.HIGHEST,
                              preferred_element_type=f32)

    def by_head(vals):
        if len(vals) == 2:
            return jnp.where(vhead == 0, vals[0], vals[1])
        return sum(jnp.where(vhead == i, v, 0.0) for i, v in enumerate(vals))

    decay, lam_e, tau_e, gam_e, dt_e = [], [], [], [], []
    for b in range(nb):
        cs_cols, tots, dt_cols = [], [], []
        for i in range(nh):
            head = b * nh + i
            cs_col = jnp.sum(jnp.where(lane_a == head, cs_tok, 0.0), axis=1, keepdims=True)
            cs_row = jnp.sum(jnp.where(row_a == head, cs_rows, 0.0), axis=0, keepdims=True)
            tots.append(jnp.sum(jnp.where(last, cs_row, 0.0), axis=1, keepdims=True))
            decay.append(_masked_decay(cs_col, cs_row, mask01))
            cs_cols.append(cs_col)
            if dt_tok is not None:
                dt_cols.append(jnp.sum(jnp.where(lane_a == head, dt_tok, 0.0), axis=1, keepdims=True))
        cs_e, tot_e = by_head(cs_cols), by_head(tots)
        lam_e.append(jnp.exp(cs_e))
        tau_e.append(jnp.exp(tot_e - cs_e))
        gam_e.append(jnp.exp(tot_e))
        if dt_tok is not None:
            dt_e.append(by_head(dt_cols))
    vs = [x if dt_tok is None else x * dt_e[b] for b, x in enumerate(xs)]
    qk = lambda b: (qs[0], ks[0]) if shared else (qs[b], ks[b])
    if nsub == 1:
        scores = [bdot(qs[0], ks[0], "nt")] if shared else [bdot(*qk(b), "nt") for b in range(nb)]
        score = lambda b, i: scores[0 if shared else b]
    else:
        scores = [[bdot(jnp.where(qhead == i, qk(b)[0], 0.0), qk(b)[1], "nt") for i in range(nh)] for b in range(nb)]
        score = lambda b, i: scores[b][i]
    ys = [lam_e[b] * bdot(qk(b)[0], hs[b], "nn")
          + by_head([bdot(score(b, i) * decay[b * nh + i], vs[b], "nn") for i in range(nh)]) for b in range(nb)]
    hns = [gam_e[b] * hs[b] + bdot(qk(b)[1], tau_e[b] * vs[b], "tn") for b in range(nb)]
    if nsub > 1:
        nhead = lax.broadcasted_iota(jnp.int32, (N, W), 0) // (N // nsub)
        keep = nhead == lax.broadcasted_iota(jnp.int32, (N, W), 1) // HEAD_W
        hns = [jnp.where(keep, hn, 0.0) for hn in hns]
    return ys, hns


def chunk_cumsum(a_tok, rev):
    G, S, Hg = a_tok.shape
    L = CHUNK
    CB = _pick(S // L, (16, 8, 4, 2))

    def call(a, rev):
        def body(a_ref, o_ref):
            t = lax.broadcasted_iota(jnp.int32, (L, L), 0)
            l = lax.broadcasted_iota(jnp.int32, (L, L), 1)
            tri = ((l >= t) if rev else (l <= t)).astype(f32)
            for j in range(CB):
                o_ref[0, j * L:(j + 1) * L, :] = _exact_dot(tri, a_ref[0, j * L:(j + 1) * L, :])

        spec = pl.BlockSpec((1, CB * L, Hg), lambda g, c: (g, c, 0))
        return pl.pallas_call(
            body,
            out_shape=jax.ShapeDtypeStruct((G, S, Hg), f32),
            grid=(G, S // (CB * L)),
            in_specs=[spec],
            out_specs=spec,
            compiler_params=_cparams(("parallel", "parallel")),
            name="chunk_cumsum",
        )(a)

    @jax.custom_vjp
    def op(a):
        return call(a, rev)

    def fwd(a):
        return call(a, rev), None

    def bwd(_, g):
        return (call(g, not rev),)

    op.defvjp(fwd, bwd)
    return op(a_tok)


def scan_op(q, k, x, a_tok, dt_tok, *, rev, incl, nsub):
    S = q.shape[0]
    G, _, Hg = a_tok.shape
    N = q.shape[1] // G
    Vw = x.shape[1] // G
    L = CHUNK
    nc = S // L
    use_dt = dt_tok is not None
    PW = min(Vw, LANES)
    chunk = functools.partial(_scan_chunk, rev=rev, incl=incl, nsub=nsub)
    cols = [slice(p * PW, (p + 1) * PW) for p in range(Vw // PW)]
    own_qk = nsub > 1
    NB = PW if own_qk else N

    def order(c, backward):
        return (nc - 1 - c) if (rev != backward) else c

    def specs(backward):
        qs = pl.BlockSpec((L, N), lambda g, c: (order(c, backward), g))
        xs = pl.BlockSpec((L, Vw), lambda g, c: (order(c, backward), g))
        as_ = pl.BlockSpec((1, L, Hg), lambda g, c: (g, order(c, backward), 0))
        hs = pl.BlockSpec((1, 1, NB, Vw), lambda g, c: (g, order(c, backward), 0, 0))
        return qs, xs, as_, hs

    def call_fwd(q, k, x, a_tok, dt_tok, y_prev=None):
        qs, xs, as_, hs = specs(False)
        n_in = 4 + use_dt + (y_prev is not None)

        def body(*refs):
            q_ref, k_ref, x_ref, a_ref = refs[:4]
            dt_ref = refs[4] if use_dt else None
            yp_ref = refs[n_in - 1] if y_prev is not None else None
            y_ref, hs_ref, h_scr = refs[n_in:]

            @pl.when(pl.program_id(1) == 0)
            def _():
                h_scr[...] = jnp.zeros_like(h_scr)

            hs_ref[0, 0] = h_scr[...]
            q, k, a, dt = q_ref[...], k_ref[...], a_ref[0], dt_ref[0] if use_dt else None
            qs, ks = ([q[:, c] for c in cols], [k[:, c] for c in cols]) if own_qk else ([q], [k])
            ys, hns = chunk(qs, ks, [x_ref[:, c] for c in cols], a, dt, [h_scr[:, c] for c in cols])
            for c, y, hn in zip(cols, ys, hns):
                y_ref[:, c] = y if yp_ref is None else y + yp_ref[:, c]
                h_scr[:, c] = hn

        ins = [q, k, x, a_tok] + ([dt_tok] if use_dt else []) + ([y_prev] if y_prev is not None else [])
        return pl.pallas_call(
            body,
            out_shape=[jax.ShapeDtypeStruct((S, G * Vw), f32), jax.ShapeDtypeStruct((G, nc, NB, Vw), f32)],
            grid=(G, nc),
            in_specs=[qs, qs, xs, as_] + ([as_] if use_dt else []) + ([xs] if y_prev is not None else []),
            out_specs=[xs, hs],
            scratch_shapes=[pltpu.VMEM((NB, Vw), f32)],
            compiler_params=_cparams(("parallel", "arbitrary")),
            name="scan_fwd",
        )(*ins)

    def call_bwd(q, k, x, a_tok, dt_tok, hsave, dy, acc=None):
        qs, xs, as_, hs = specs(True)
        n_in = 6 + use_dt + (3 if acc is not None else 0)

        def body(*refs):
            q_ref, k_ref, x_ref, a_ref = refs[:4]
            dt_ref = refs[4] if use_dt else None
            hs_ref, dy_ref = refs[4 + use_dt], refs[5 + use_dt]
            acc_refs = refs[n_in - 3:n_in] if acc is not None else None
            dq_ref, dk_ref, dx_ref, da_ref = refs[n_in:n_in + 4]
            ddt_ref = refs[n_in + 4] if use_dt else None
            dh_scr = refs[-1]

            @pl.when(pl.program_id(1) == 0)
            def _():
                dh_scr[...] = jnp.zeros_like(dh_scr)

            q, k, a = q_ref[...].astype(f32), k_ref[...].astype(f32), a_ref[0]
            qs, ks = ([q[:, c] for c in cols], [k[:, c] for c in cols]) if own_qk else ([q], [k])
            xs, hs_in = [x_ref[:, c] for c in cols], [hs_ref[0, 0, :, c] for c in cols]
            if use_dt:
                _, vjp = jax.vjp(chunk, qs, ks, xs, a, dt_ref[0], hs_in)
            else:
                _, vjp = jax.vjp(lambda qs, ks, xs, a, hs: chunk(qs, ks, xs, a, None, hs), qs, ks, xs, a, hs_in)
            cts = vjp(([dy_ref[:, c] for c in cols], [dh_scr[:, c] for c in cols]))
            dqs, dks, dxs, da, dhs = cts[0], cts[1], cts[2], cts[3], cts[-1]
            if acc is not None:
                dq_acc, dk_acc = acc_refs[0][...].astype(f32), acc_refs[1][...].astype(f32)
            if own_qk:
                for b, c in enumerate(cols):
                    dq_ref[:, c] = (dqs[b] if acc is None else dqs[b] + dq_acc[:, c]).astype(dq_ref.dtype)
                    dk_ref[:, c] = (dks[b] if acc is None else dks[b] + dk_acc[:, c]).astype(dk_ref.dtype)
            else:
                dq_ref[...] = (dqs[0] if acc is None else dqs[0] + dq_acc).astype(dq_ref.dtype)
                dk_ref[...] = (dks[0] if acc is None else dks[0] + dk_acc).astype(dk_ref.dtype)
            for b, c in enumerate(cols):
                dx_ref[:, c] = dxs[b] if acc is None else dxs[b] + acc_refs[2][:, c]
                dh_scr[:, c] = dhs[b]
            da_ref[0] = da
            if use_dt:
                ddt_ref[0] = cts[4]

        ins = [q, k, x, a_tok] + ([dt_tok] if use_dt else []) + [hsave, dy] + (list(acc) if acc is not None else [])
        a_shape = jax.ShapeDtypeStruct(a_tok.shape, f32)
        return pl.pallas_call(
            body,
            out_shape=[jax.ShapeDtypeStruct(q.shape, q.dtype), jax.ShapeDtypeStruct(k.shape, k.dtype),
                       jax.ShapeDtypeStruct(x.shape, f32), a_shape] + ([a_shape] if use_dt else []),
            grid=(G, nc),
            in_specs=[qs, qs, xs, as_] + ([as_] if use_dt else []) + [hs, xs] + ([qs, qs, xs] if acc is not None else []),
            out_specs=[qs, qs, xs, as_] + ([as_] if use_dt else []),
            scratch_shapes=[pltpu.VMEM((NB, Vw), f32)],
            compiler_params=_cparams(("parallel", "arbitrary")),
            name="scan_bwd",
        )(*ins)

    return call_fwd, call_bwd


def bidir_scan(q, k, x, a_f, a_b, dt_f, dt_b, *, nsub):
    use_dt = dt_f is not None
    a_f, a_b = chunk_cumsum(a_f, False), chunk_cumsum(a_b, True)
    fwd_f, bwd_f = scan_op(q, k, x, a_f, dt_f, rev=False, incl=True, nsub=nsub)
    fwd_b, bwd_b = scan_op(q, k, x, a_b, dt_b, rev=True, incl=False, nsub=nsub)

    def run(q, k, x, a_f, a_b, dt_f, dt_b):
        y_f, hs_f = fwd_f(q, k, x, a_f, dt_f)
        y, hs_b = fwd_b(q, k, x, a_b, dt_b, y_prev=y_f)
        return y, (hs_f, hs_b)

    def grads(q, k, x, a_f, a_b, dt_f, dt_b, hs, dy):
        first = bwd_f(q, k, x, a_f, dt_f, hs[0], dy)
        both = bwd_b(q, k, x, a_b, dt_b, hs[1], dy, acc=first[:3])
        return both[0], both[1], both[2], first[3], both[3], (first[4] if use_dt else None), (both[4] if use_dt else None)

    if use_dt:
        @jax.custom_vjp
        def op(q, k, x, a_f, a_b, dt_f, dt_b):
            return run(q, k, x, a_f, a_b, dt_f, dt_b)[0]

        def fwd(q, k, x, a_f, a_b, dt_f, dt_b):
            y, hs = run(q, k, x, a_f, a_b, dt_f, dt_b)
            return y, (q, k, x, a_f, a_b, dt_f, dt_b, hs)

        def bwd(res, dy):
            return grads(*res, dy)

        op.defvjp(fwd, bwd)
        return op(q, k, x, a_f, a_b, dt_f, dt_b)

    @jax.custom_vjp
    def op(q, k, x, a_f, a_b):
        return run(q, k, x, a_f, a_b, None, None)[0]

    def fwd(q, k, x, a_f, a_b):
        y, hs = run(q, k, x, a_f, a_b, None, None)
        return y, (q, k, x, a_f, a_b, hs)

    def bwd(res, dy):
        q, k, x, a_f, a_b, hs = res
        return grads(q, k, x, a_f, a_b, None, None, hs, dy)[:5]

    op.defvjp(fwd, bwd)
    return op(q, k, x, a_f, a_b)


def _swap_halves(x, dh):
    W = x.shape[1]
    lane = lax.broadcasted_iota(jnp.int32, (1, W), 1) % dh
    return jnp.where(lane < dh // 2, pltpu.roll(x, W - dh // 2, 1), pltpu.roll(x, dh // 2, 1))


def rotary(rq, rk, cos_t, sin_t):
    scale = RET_DH ** -0.5

    def fn(rq, rk, c, s):
        return rq * c + _swap_halves(rq, RET_DH) * s, (rk * c + _swap_halves(rk, RET_DH) * s) * scale

    def bwd_fn(rv, pv, dos):
        _, _, c, s = rv
        dq, dk = dos
        dk = dk * scale
        return (dq * c + _swap_halves(dq * s, RET_DH), dk * c + _swap_halves(dk * s, RET_DH)), ()

    return rowwise(fn, "rotary", [rq, rk, cos_t, sin_t], [], [MXU_DTYPE, MXU_DTYPE], n_diff_rows=2, bwd_fn=bwd_fn)


def _rope_tables(S, width):
    half = RET_DH // 2
    inv = 1.0 / (ROPE_BASE ** (jnp.arange(half, dtype=f32) / half))
    ang = jnp.arange(S, dtype=f32)[:, None] * inv[None, :]
    cos, sin = jnp.cos(ang), jnp.sin(ang)
    reps = width // RET_DH
    return jnp.tile(jnp.concatenate([cos, cos], axis=1), (1, reps)), jnp.tile(jnp.concatenate([-sin, sin], axis=1), (1, reps))


def _exact_dot(x, m):
    return jnp.dot(x, m, precision=lax.Precision.HIGHEST, preferred_element_type=f32)


def ret_post(y, rg, gn_g):
    W = y.shape[1]
    idx = np.arange(W) // RET_DH
    avg = jnp.asarray((idx[:, None] == idx[None, :]).astype(np.float32) / RET_DH)

    def fn(y, rg, g, avg):
        mu = _exact_dot(y, avg)
        d = y - mu
        var = _exact_dot(d * d, avg)
        return (_silu(rg) * (d * lax.rsqrt(var + EPS) * g),)

    return rowwise(fn, "ret_post", [y, rg], [gn_g.reshape(1, -1), avg], [MXU_DTYPE], n_diff_params=1)[0]


def _na_bias(rpb, win_r):
    H = rpb.shape[0]
    qc = np.arange(GRID_W)[:, None]
    kc = np.arange(GRID_W)[None, :]
    cstart = np.clip(qc - NA_WIN_C // 2, 0, GRID_W - NA_WIN_C)
    valid = (kc >= cstart) & (kc < cstart + NA_WIN_C)
    dc = np.clip(kc - qc, -(NA_WIN_C - 1), NA_WIN_C - 1) + (NA_WIN_C - 1)
    onehot = (dc[None] == np.arange(2 * NA_WIN_C - 1)[:, None, None]).astype(np.float32)
    t1 = jnp.einsum("hrd,dqk->hrqk", rpb.astype(f32), jnp.asarray(onehot), precision=lax.Precision.HIGHEST)
    per_delta = [t1[:, NA_WIN_R - 1 - d:NA_WIN_R - 1 - d + win_r] for d in range(win_r)]
    b = jnp.stack(per_delta, axis=1)
    b = jnp.where(jnp.asarray(valid)[None, None, None], b, NEG_INF)
    return jnp.transpose(b, (0, 1, 3, 2, 4)).reshape(H, win_r, GRID_W, win_r * GRID_W)


def _na_rows(rows):
    lane = lax.broadcasted_iota(jnp.int32, (1, rows[0][0].shape[1]), 1) // NA_DH
    scale = NA_DH ** -0.5
    ss = [[_bdot_raw(jnp.where(lane == i, q, 0.0) * scale, kw, "nt") + b for i, b in enumerate(bs)] for q, kw, _, bs in rows]
    es = [[jnp.exp(s - jnp.max(s, axis=1, keepdims=True)) for s in srow] for srow in ss]
    ps = [[e / jnp.sum(e, axis=1, keepdims=True) for e in erow] for erow in es]
    return [_lanes_by_head(lane, [_bdot_raw(p, vw, "nn") for p in prow]) for prow, (_, _, vw, _) in zip(ps, rows)]


def _lanes_by_head(lane, vals):
    if len(vals) == 2:
        return jnp.where(lane == 0, vals[0], vals[1])
    return sum(jnp.where(lane == i, v, 0.0) for i, v in enumerate(vals))


def _na_rows_bwd(rows):
    lane = lax.broadcasted_iota(jnp.int32, (1, rows[0][0].shape[1]), 1) // NA_DH
    scale = NA_DH ** -0.5
    heads = range(len(rows[0][3]))
    qis = [[jnp.where(lane == i, q, 0.0) * scale for i in heads] for q, _, _, _, _ in rows]
    dos = [[jnp.where(lane == i, do, 0.0) for i in heads] for _, _, _, _, do in rows]
    ss = [[_bdot_raw(qi, kw, "nt") + b for qi, b in zip(qrow, bs)] for qrow, (_, kw, _, bs, _) in zip(qis, rows)]
    dps = [[_bdot_raw(doi, vw, "nt") for doi in drow] for drow, (_, _, vw, _, _) in zip(dos, rows)]
    es = [[jnp.exp(s - jnp.max(s, axis=1, keepdims=True)) for s in srow] for srow in ss]
    ps = [[e / jnp.sum(e, axis=1, keepdims=True) for e in erow] for erow in es]
    dss = [[p * (dp - jnp.sum(dp * p, axis=1, keepdims=True)) for p, dp in zip(prow, dprow)] for prow, dprow in zip(ps, dps)]
    out = []
    for qrow, drow, prow, dsrow, (_, kw, _, _, _) in zip(qis, dos, ps, dss, rows):
        dq = _lanes_by_head(lane, [_bdot_raw(dsrow[i], kw, "nn") for i in heads]) * scale
        dk, dv = 0.0, 0.0
        for i in heads:
            dk = dk + _bdot_raw(dsrow[i], qrow[i], "tn")
            dv = dv + _bdot_raw(prow[i], drow[i], "tn")
        out.append((dq, dk, dv, dsrow))
    return out


def na_op(nq, nk, nv, bias):
    S, W = nq.shape
    rows = S // GRID_W
    win_r = bias.shape[1]
    nkeys = win_r * GRID_W
    hp = LANES // NA_DH
    npair = W // LANES
    RB = min(16, rows)
    nrb = rows // RB
    qspec = pl.BlockSpec((RB * GRID_W, LANES), lambda p, r: (r, p))
    kspec = pl.BlockSpec((S, LANES), lambda p, r: (0, p))
    bspec = pl.BlockSpec((hp, win_r, GRID_W, nkeys), lambda p, r: (p, 0, 0, 0))

    def window(r):
        r0 = jnp.clip(r - win_r // 2, 0, rows - win_r)
        return pl.multiple_of(r0 * GRID_W, GRID_W), r - r0

    def call_fwd(nq, nk, nv, bias):
        def body(q_ref, k_ref, v_ref, b_ref, o_ref):
            rb = pl.program_id(1)

            def step(j, c):
                args, q0s = [], []
                for u in range(NA_ROWS_PER_STEP):
                    i = j * NA_ROWS_PER_STEP + u
                    k0, d = window(rb * RB + i)
                    q0 = pl.multiple_of(i * GRID_W, GRID_W)
                    q0s.append(q0)
                    args.append((q_ref[pl.ds(q0, GRID_W), :].astype(f32), k_ref[pl.ds(k0, nkeys), :], v_ref[pl.ds(k0, nkeys), :],
                                 [b_ref[h, pl.ds(d, 1)][0] for h in range(hp)]))
                for q0, o in zip(q0s, _na_rows(args)):
                    o_ref[pl.ds(q0, GRID_W), :] = o.astype(o_ref.dtype)
                return c

            lax.fori_loop(0, RB // NA_ROWS_PER_STEP, step, 0)

        return pl.pallas_call(
            body,
            out_shape=jax.ShapeDtypeStruct((S, W), nq.dtype),
            grid=(npair, nrb),
            in_specs=[qspec, kspec, kspec, bspec],
            out_specs=qspec,
            compiler_params=_cparams(("parallel", "arbitrary")),
            name="na_fwd",
        )(nq, nk, nv, bias)

    def call_bwd(nq, nk, nv, bias, do):
        def body(q_ref, k_ref, v_ref, b_ref, do_ref, dq_ref, dk_ref, dv_ref, db_ref, dk_acc, dv_acc):
            rb = pl.program_id(1)

            @pl.when(rb == 0)
            def _():
                dk_acc[...] = jnp.zeros_like(dk_acc)
                dv_acc[...] = jnp.zeros_like(dv_acc)
                db_ref[...] = jnp.zeros_like(db_ref)

            def step(j, c):
                args, where = [], []
                for u in range(NA_ROWS_PER_STEP):
                    i = j * NA_ROWS_PER_STEP + u
                    k0, d = window(rb * RB + i)
                    q0 = pl.multiple_of(i * GRID_W, GRID_W)
                    where.append((q0, k0, d))
                    args.append((q_ref[pl.ds(q0, GRID_W), :].astype(f32), k_ref[pl.ds(k0, nkeys), :], v_ref[pl.ds(k0, nkeys), :],
                                 [b_ref[h, pl.ds(d, 1)][0] for h in range(hp)], do_ref[pl.ds(q0, GRID_W), :].astype(f32)))
                for (q0, k0, d), (dq, dk, dv, dbs) in zip(where, _na_rows_bwd(args)):
                    dq_ref[pl.ds(q0, GRID_W), :] = dq.astype(dq_ref.dtype)
                    dk_acc[pl.ds(k0, nkeys), :] += dk
                    dv_acc[pl.ds(k0, nkeys), :] += dv
                    for h in range(hp):
                        db_ref[h, pl.ds(d, 1)] += dbs[h][None]
                return c

            lax.fori_loop(0, RB // NA_ROWS_PER_STEP, step, 0)

            @pl.when(rb == nrb - 1)
            def _():
                dk_ref[...] = dk_acc[...].astype(dk_ref.dtype)
                dv_ref[...] = dv_acc[...].astype(dv_ref.dtype)

        return pl.pallas_call(
            body,
            out_shape=[jax.ShapeDtypeStruct((S, W), nq.dtype), jax.ShapeDtypeStruct((S, W), nk.dtype),
                       jax.ShapeDtypeStruct((S, W), nv.dtype), jax.ShapeDtypeStruct(bias.shape, f32)],
            grid=(npair, nrb),
            in_specs=[qspec, kspec, kspec, bspec, qspec],
            out_specs=[qspec, kspec, kspec, bspec],
            scratch_shapes=[pltpu.VMEM((S, LANES), f32), pltpu.VMEM((S, LANES), f32)],
            compiler_params=_cparams(("parallel", "arbitrary")),
            name="na_bwd",
        )(nq, nk, nv, bias, do)

    @jax.custom_vjp
    def op(nq, nk, nv, bias):
        return call_fwd(nq, nk, nv, bias)

    def fwd(nq, nk, nv, bias):
        return call_fwd(nq, nk, nv, bias), (nq, nk, nv, bias)

    def bwd(res, do):
        return tuple(call_bwd(*res, do))

    op.defvjp(fwd, bwd)
    return op(nq, nk, nv, bias)


def ssd_dt(dt_raw, dt_bias, a_neg):
    def fn(r, b, a):
        dt = _softplus(r + b)
        return dt, dt * a

    return rowwise(fn, "ssd_dt", [dt_raw], [dt_bias, a_neg], [f32, f32])


def ssd_post(y, xs, z, d_skip_lanes, norm_g, groups):
    def fn(y, xs, z, dsk, g):
        y = (y + xs * dsk) * _silu(z)
        return (y * lax.rsqrt(jnp.mean(y * y, axis=-1, keepdims=True) + EPS) * g,)

    return rowwise(fn, "ssd_post", [y, xs, z], [d_skip_lanes.reshape(1, -1), norm_g.reshape(1, -1)], [MXU_DTYPE],
                   ncol=groups)[0]


def _heads_major(t, groups):
    S = t.shape[0]
    return jnp.transpose(t.reshape(S, groups, -1), (1, 0, 2))


def retention_na_mixer(hn, w_in, decay_logit, gn_g, rpb, w_out, tables):
    S = hn.shape[0]
    R = RET_HEADS * RET_DH
    NW = NA_HEADS * NA_DH
    cols = lambda a, b: w_in[:, a:b]
    rq, rk, rv, rg = (mm(hn, cols(j * R, (j + 1) * R)) for j in range(4))
    nq, nk, nv = (mm(hn, cols(4 * R + j * NW, 4 * R + (j + 1) * NW), out_dtype=MXU_DTYPE) for j in range(3))
    qr, kr = rotary(rq, rk, *tables)
    log_gamma = -_softplus(-decay_logit.astype(f32))
    pairs = R // LANES
    hp = LANES // RET_DH
    hpad = -(-RET_HEADS // 8) * 8
    pad8 = lambda a: jnp.pad(a.reshape(1, 1, RET_HEADS), ((0, 0), (0, 0), (0, hpad - RET_HEADS)))
    a_f = jnp.broadcast_to(pad8(log_gamma[0]), (1, S, hpad))
    a_b = jnp.broadcast_to(pad8(log_gamma[1]), (1, S, hpad))
    ret = ret_post(bidir_scan(qr, kr, rv, a_f, a_b, None, None, nsub=hp), rg, gn_g)
    rows = S // GRID_W
    nao = na_op(nq, nk, nv, _na_bias(rpb, min(NA_WIN_R, rows)))
    return mm(ret, w_out[:R]) + mm(nao, w_out[R:])


def ssd_mixer(hn, w_in, conv_w, conv_b, dt_bias, a_log, d_skip, norm_g, w_out):
    heads = d_skip.shape[0]
    inner = heads * SSD_HEADDIM
    gs = SSD_GROUPS * SSD_STATE
    o_x, o_b, o_c, o_dt = inner, 2 * inner, 2 * inner + gs, 2 * inner + 2 * gs
    z = mm(hn, w_in[:, :inner])
    dt_raw = mm(hn, w_in[:, o_dt:])
    xs, bm, cm = (mm_conv_act(hn, [w_in[:, a:b]], [conv_w[:, a - inner:b - inner]], [conv_b[a - inner:b - inner]],
                              _silu, _silu_bwd, f32, "conv_silu") for a, b in ((o_x, o_b), (o_b, o_c), (o_c, o_dt)))
    a_neg = -jnp.exp(a_log.astype(f32)).reshape(1, -1)
    dt, la = ssd_dt(dt_raw, dt_bias.astype(f32).reshape(1, -1), a_neg)
    dt_f, dt_b = _heads_major(dt[:, :heads], SSD_GROUPS), _heads_major(dt[:, heads:], SSD_GROUPS)
    la_f, la_b = _heads_major(la[:, :heads], SSD_GROUPS), _heads_major(la[:, heads:], SSD_GROUPS)
    y = bidir_scan(cm, bm, xs, la_f, la_b, dt_f, dt_b, nsub=1)
    y = ssd_post(y, xs, z, jnp.repeat(d_skip.astype(f32), SSD_HEADDIM), norm_g, SSD_GROUPS)
    return mm(y, w_out)


def conv_geglu_ffn(hf, w_up, conv_w, conv_b, w_down):
    F = w_down.shape[0]
    a = mm_conv_act(hf, [w_up[:, :F], w_up[:, F:]], [conv_w[:, :F], conv_w[:, F:]], [conv_b[:F], conv_b[F:]],
                    _geglu, _geglu_bwd, MXU_DTYPE, "conv_geglu")
    return mm(a, w_down)


def model_loss(x, tgt, big, small, rep):
    S = x.shape[0]
    depth = rep["norm_mix_pre"].shape[0]
    tables = _rope_tables(S, RET_HEADS * RET_DH)
    hn = rms(x, rep["norm_mix_pre"][0], MXU_DTYPE)
    for layer in range(depth):
        i = layer // 2
        if layer % 2 == 0:
            m = retention_na_mixer(hn, big["ab_w_in"][i], rep["ab_ret_decay_logit"][i], rep["ab_ret_gn_g"][i],
                                   rep["ab_na_rpb"][i], big["ab_w_out"][i], tables)
        else:
            m = ssd_mixer(hn, big["c_w_in"][i], small["c_conv_w"][i], small["c_conv_b"][i], rep["c_dt_bias"][i],
                          rep["c_a_log"][i], rep["c_d_skip"][i], small["c_norm_g"][i], big["c_w_out"][i])
        x, hf = rms_residual_norm(m, rep["norm_mix_post"][layer], x, rep["norm_ffn_pre"][layer])
        f = conv_geglu_ffn(hf, big["ffn_w_up"][layer], small["ffn_conv_w"][layer], rep["ffn_conv_b"][layer],
                           big["ffn_w_down"][layer])
        if layer + 1 < depth:
            x, hn = rms_residual_norm(f, rep["norm_ffn_post"][layer], x, rep["norm_mix_pre"][layer + 1])
        else:
            x = rms_residual(f, rep["norm_ffn_post"][layer], x)
    return loss_op(x, tgt)


def _mesh_pos():
    return lax.axis_index("x"), lax.axis_index("y"), lax.axis_index("c")


def gather_chips(local):
    R, Wd = local.shape

    half = R // 2
    CH = COPY_CHUNKS
    q = half // CH

    def body(x_ref, out_ref, send_sems, recv_sems):
        x, y, c = _mesh_pos()
        my = 2 * x + y
        chips = [(1 - x, y), (x, 1 - y), (1 - x, 1 - y)]

        def piece(ref, h, j):
            return ref.at[pl.ds(pl.multiple_of(h * half + j * q, PACK_ALIGN), q), :]

        def copy(k, src, chip, h, j, to):
            return pltpu.make_async_remote_copy(src_ref=src, dst_ref=piece(out_ref.at[chip], h, j), send_sem=send_sems.at[k],
                                                recv_sem=recv_sems.at[k], device_id=to, device_id_type=pl.DeviceIdType.MESH)

        first = [[copy(k * CH + j, piece(x_ref, c, j), my, c, j, (cx, cy, c)) for j in range(CH)]
                 for k, (cx, cy) in enumerate(chips)]
        for j in range(CH):
            for k in range(3):
                first[k][j].start()
        passed = [[copy((3 + k) * CH + j, piece(out_ref.at[2 * cx + cy], c, j), 2 * cx + cy, c, j, (x, y, 1 - c))
                   for j in range(CH)] for k, (cx, cy) in enumerate(chips)]
        for j in range(CH):
            for k, (cx, cy) in enumerate(chips):
                copy(k * CH + j, piece(x_ref, c, j), 2 * cx + cy, c, j, (cx, cy, c)).wait_recv()
                passed[k][j].start()
        for j in range(CH):
            for k, (cx, cy) in enumerate(chips):
                copy((3 + k) * CH + j, piece(x_ref, c, j), 2 * cx + cy, 1 - c, j, (x, y, 1 - c)).wait_recv()
        for k in range(3):
            for cp in first[k] + passed[k]:
                cp.wait_send()

    return pl.pallas_call(
        body,
        out_shape=jax.ShapeDtypeStruct((N_CHIPS, R, Wd), local.dtype),
        in_specs=[pl.BlockSpec(memory_space=pl.ANY)],
        out_specs=pl.BlockSpec(memory_space=pl.ANY),
        scratch_shapes=[pltpu.SemaphoreType.DMA((6 * CH,)), pltpu.SemaphoreType.DMA((6 * CH,))],
        name="gather_chips",
    )(local)


def pair_swap(parts):
    n, R, Wd = parts.shape
    half = R // 2

    CH = COPY_CHUNKS
    q = half // CH

    def body(p_ref, got_ref, send_sems, recv_sems):
        x, y, c = _mesh_pos()

        def src(s, j):
            return p_ref.at[s, pl.ds(pl.multiple_of((1 - c) * half + j * q, PACK_ALIGN), q), :]

        swap = [pltpu.make_async_remote_copy(src_ref=src(s, j), dst_ref=got_ref.at[s, pl.ds(j * q, q), :],
                                             send_sem=send_sems.at[s * CH + j], recv_sem=recv_sems.at[s * CH + j],
                                             device_id=(x, y, 1 - c), device_id_type=pl.DeviceIdType.MESH)
                for s in range(n) for j in range(CH)]
        for cp in swap:
            cp.start()
        for cp in swap:
            cp.wait()

    return pl.pallas_call(
        body,
        out_shape=jax.ShapeDtypeStruct((n, half, Wd), parts.dtype),
        in_specs=[pl.BlockSpec(memory_space=pl.ANY)],
        out_specs=pl.BlockSpec(memory_space=pl.ANY),
        scratch_shapes=[pltpu.SemaphoreType.DMA((n * CH,)), pltpu.SemaphoreType.DMA((n * CH,))],
        name="pair_swap",
    )(parts)


def chip_exchange(parts):
    n, R, Wd = parts.shape

    def body(p_ref, out_ref, send_sems, recv_sems):
        x, y, c = _mesh_pos()
        my = 2 * x + y
        chips = [(1 - x, y), (x, 1 - y), (1 - x, 1 - y)]

        def copy(k, src_slot, dst_slot, to):
            return pltpu.make_async_remote_copy(src_ref=p_ref.at[src_slot], dst_ref=out_ref.at[dst_slot], send_sem=send_sems.at[k],
                                                recv_sem=recv_sems.at[k], device_id=to, device_id_type=pl.DeviceIdType.MESH)

        sends = [copy(k, 2 * cx + cy, my, (cx, cy, c)) for k, (cx, cy) in enumerate(chips)]
        for cp in sends:
            cp.start()
        for k, (cx, cy) in enumerate(chips):
            copy(k, my, 2 * cx + cy, (cx, cy, c)).wait_recv()
        for cp in sends:
            cp.wait_send()

    return pl.pallas_call(
        body,
        out_shape=jax.ShapeDtypeStruct((n, R, Wd), parts.dtype),
        in_specs=[pl.BlockSpec(memory_space=pl.ANY)],
        out_specs=pl.BlockSpec(memory_space=pl.ANY),
        scratch_shapes=[pltpu.SemaphoreType.DMA((3,)), pltpu.SemaphoreType.DMA((3,))],
        name="chip_exchange",
    )(parts)


def pair_share(mine):
    R, Wd = mine.shape

    CH = 2 * COPY_CHUNKS
    q = R // CH

    def body(m_ref, out_ref, send_sems, recv_sems):
        x, y, c = _mesh_pos()
        swap = [pltpu.make_async_remote_copy(src_ref=m_ref.at[pl.ds(j * q, q), :], dst_ref=out_ref.at[pl.ds(j * q, q), :],
                                             send_sem=send_sems.at[j], recv_sem=recv_sems.at[j], device_id=(x, y, 1 - c),
                                             device_id_type=pl.DeviceIdType.MESH) for j in range(CH)]
        for cp in swap:
            cp.start()
        for cp in swap:
            cp.wait()

    return pl.pallas_call(
        body,
        out_shape=jax.ShapeDtypeStruct((R, Wd), mine.dtype),
        in_specs=[pl.BlockSpec(memory_space=pl.ANY)],
        out_specs=pl.BlockSpec(memory_space=pl.ANY),
        scratch_shapes=[pltpu.SemaphoreType.DMA((CH,)), pltpu.SemaphoreType.DMA((CH,))],
        name="pair_share",
    )(mine)


def sum_chips(recv, own):
    n, R, Wd = recv.shape
    tr = _pick(R, (512, 256, 128, 64, 32, 16, 8))

    def body(r_ref, p_ref, o_ref):
        my = 2 * lax.axis_index("x") + lax.axis_index("y")
        acc = jnp.zeros((tr, Wd), f32)
        for s in range(n):
            acc = acc + jnp.where(my == s, p_ref[s], r_ref[s]).astype(f32)
        o_ref[...] = acc

    spec = pl.BlockSpec((n, tr, Wd), lambda i: (0, i, 0))
    return pl.pallas_call(
        body,
        out_shape=jax.ShapeDtypeStruct((R, Wd), f32),
        grid=(R // tr,),
        in_specs=[spec, spec],
        out_specs=pl.BlockSpec((tr, Wd), lambda i: (i, 0)),
        compiler_params=_cparams(("parallel",)),
        name="sum_chips",
    )(recv, own)


def add_pair(parts, got):
    n, R, Wd = parts.shape
    half = R // 2
    tr = _pick(half, (512, 256, 128, 64, 32, 16, 8))
    nb = half // tr

    def body(lo_ref, hi_ref, g_ref, o_ref):
        mine = jnp.where(lax.axis_index("c") == 0, lo_ref[...], hi_ref[...])
        o_ref[...] = (mine.astype(f32) + g_ref[...].astype(f32)).astype(o_ref.dtype)

    spec = pl.BlockSpec((1, tr, Wd), lambda s, i: (s, i, 0))
    return pl.pallas_call(
        body,
        out_shape=jax.ShapeDtypeStruct(got.shape, parts.dtype),
        grid=(n, nb),
        in_specs=[spec, pl.BlockSpec((1, tr, Wd), lambda s, i: (s, nb + i, 0)), spec],
        out_specs=spec,
        compiler_params=_cparams(("parallel", "parallel")),
        name="add_pair",
    )(parts, parts, got)


def reduce_scatter(parts):
    chip_sum = add_pair(parts, pair_swap(parts))
    mine = sum_chips(chip_exchange(chip_sum), chip_sum)
    theirs = pair_share(mine)
    first = lax.axis_index("c") == 0
    return jnp.concatenate([jnp.where(first, mine, theirs), jnp.where(first, theirs, mine)], axis=0)


def adamw(w, g, m, v):
    shp = w.shape
    if w.size * 4 <= (1 << 20):
        grid, block, imap = (1,), shp, lambda i: (0,) * len(shp)
    else:
        n0, R, C = shp
        tr = _divisor_tile(R, lambda t: t * C * 4 <= (1 << 20), 8)
        grid, block, imap = (n0, R // tr), (1, tr, C), lambda j, i: (j, i, 0)

    def body(w_ref, g_ref, m_ref, v_ref, d_ref, mo_ref, vo_ref):
        g = g_ref[...]
        m = ADAM_B1 * m_ref[...] + (1.0 - ADAM_B1) * g
        v = ADAM_B2 * v_ref[...] + (1.0 - ADAM_B2) * (g * g)
        m_hat = m / (1.0 - ADAM_B1 ** ADAM_STEP)
        v_hat = v / (1.0 - ADAM_B2 ** ADAM_STEP)
        d_ref[...] = -ADAM_LR * (m_hat / (jnp.sqrt(v_hat) + ADAM_EPS) + ADAM_WD * w_ref[...])
        mo_ref[...] = m
        vo_ref[...] = v

    spec = pl.BlockSpec(block, imap)
    return pl.pallas_call(
        body,
        out_shape=[jax.ShapeDtypeStruct(shp, f32)] * 3,
        grid=grid,
        in_specs=[spec] * 4,
        out_specs=[spec] * 3,
        compiler_params=_cparams(("parallel",) * len(grid)),
        name="adamw",
    )(w, g, m, v)


def _pack(arrs, dtype):
    flat = jnp.concatenate([a.astype(dtype).reshape(-1) for a in arrs])
    n = flat.shape[0]
    unit = PACK_W * PACK_ROWS
    padded = -(-n // unit) * unit
    return jnp.pad(flat, (0, padded - n)).reshape(-1, PACK_W)


def _unpack(buf, shapes):
    flat = buf.reshape(-1)
    out, off = [], 0
    for s in shapes:
        n = int(np.prod(s))
        out.append(flat[off:off + n].reshape(s))
        off += n
    return out


BIG = (("ab_w_in", 2), ("ab_w_out", 1), ("c_w_in", 2), ("c_w_out", 1), ("ffn_w_up", 2), ("ffn_w_down", 1))
SMALL = (("c_conv_w", 2), ("c_conv_b", 1), ("c_norm_g", 1), ("ffn_conv_w", 2))
REP = ("norm_mix_pre", "norm_mix_post", "norm_ffn_pre", "norm_ffn_post", "ab_ret_decay_logit", "ab_ret_gn_g", "ab_na_rpb",
       "c_dt_bias", "c_a_log", "c_d_skip", "ffn_conv_b")
WEIGHTS = ("norm_mix_pre", "norm_mix_post", "norm_ffn_pre", "norm_ffn_post", "ab_w_in", "ab_ret_decay_logit", "ab_ret_gn_g",
           "ab_na_rpb", "ab_w_out", "c_w_in", "c_conv_w", "c_conv_b", "c_dt_bias", "c_a_log", "c_d_skip", "c_norm_g", "c_w_out",
           "ffn_w_up", "ffn_conv_w", "ffn_conv_b", "ffn_w_down")


def _gather_set(local, spec, dtype):
    shapes = [local[n].shape for n, _ in spec]
    got = gather_chips(_pack([local[n] for n, _ in spec], dtype))
    my = 2 * lax.axis_index("x") + lax.axis_index("y")
    per_chip = [_unpack(got[s], shapes) for s in range(N_CHIPS)]
    return {n: jnp.concatenate([jnp.where(my == s, local[n].astype(dtype), per_chip[s][j]) for s in range(N_CHIPS)], axis=ax)
            for j, (n, ax) in enumerate(spec)}


def _scatter_parts(full, spec, extra, dtype):
    split = {n: jnp.split(full[n], N_CHIPS, axis=ax) for n, ax in spec}
    return jnp.stack([_pack([split[n][s] for n, _ in spec] + list(extra), dtype) for s in range(N_CHIPS)])


def kernel(x, norm_mix_pre, norm_mix_post, norm_ffn_pre, norm_ffn_post, ab_w_in, ab_ret_decay_logit, ab_ret_gn_g, ab_na_rpb, ab_w_out, c_w_in, c_conv_w, c_conv_b, c_dt_bias, c_a_log, c_d_skip, c_norm_g, c_w_out, ffn_w_up, ffn_conv_w, ffn_conv_b, ffn_w_down, loss_target, m_norm_mix_pre, m_norm_mix_post, m_norm_ffn_pre, m_norm_ffn_post, m_ab_w_in, m_ab_ret_decay_logit, m_ab_ret_gn_g, m_ab_na_rpb, m_ab_w_out, m_c_w_in, m_c_conv_w, m_c_conv_b, m_c_dt_bias, m_c_a_log, m_c_d_skip, m_c_norm_g, m_c_w_out, m_ffn_w_up, m_ffn_conv_w, m_ffn_conv_b, m_ffn_w_down, v_norm_mix_pre, v_norm_mix_post, v_norm_ffn_pre, v_norm_ffn_post, v_ab_w_in, v_ab_ret_decay_logit, v_ab_ret_gn_g, v_ab_na_rpb, v_ab_w_out, v_c_w_in, v_c_conv_w, v_c_conv_b, v_c_dt_bias, v_c_a_log, v_c_d_skip, v_c_norm_g, v_c_w_out, v_ffn_w_up, v_ffn_conv_w, v_ffn_conv_b, v_ffn_w_down):
    args = dict(locals())
    w = {n: args[n] for n in WEIGHTS}
    mom = {n: args["m_" + n] for n in WEIGHTS}
    var = {n: args["v_" + n] for n in WEIGHTS}

    big = _gather_set(w, BIG, MXU_DTYPE)
    small = _gather_set(w, SMALL, f32)
    rep = {n: w[n] for n in REP}

    def loss_fn(xs, big, small, rep):
        return model_loss(xs, loss_target[0], big, small, rep)

    loss, (gx, gbig, gsmall, grep) = jax.value_and_grad(loss_fn, argnums=(0, 1, 2, 3))(x[0], big, small, rep)
    loss = lax.psum(loss, ("x", "y", "c"))

    big_shapes = [w[n].shape for n, _ in BIG]
    small_shapes = [w[n].shape for n, _ in SMALL] + [w[n].shape for n in REP]
    g_big = _unpack(reduce_scatter(_scatter_parts(gbig, BIG, (), MXU_DTYPE)), big_shapes)
    g_small_buf = reduce_scatter(_scatter_parts(gsmall, SMALL, [grep[n] for n in REP], f32))
    grads = dict(zip([n for n, _ in BIG], g_big))
    small_names = [n for n, _ in SMALL] + list(REP)
    grads.update(zip(small_names, _unpack(g_small_buf, small_shapes)))

    delta, new_m, new_v = {}, {}, {}
    for n in WEIGHTS:
        delta[n], new_m[n], new_v[n] = adamw(w[n], grads[n], mom[n], var[n])

    return (loss, gx[None], *[grads[n] for n in WEIGHTS], *[delta[n] for n in WEIGHTS],
            *[new_m[n] for n in WEIGHTS], *[new_v[n] for n in WEIGHTS])
```

```python
import functools
import math

import numpy as np
import jax
import jax.numpy as jnp
from jax import lax
from jax.experimental import pallas as pl
from jax.experimental.pallas import tpu as pltpu

f32 = jnp.float32
bf16 = jnp.bfloat16
MXU_DTYPE = bf16

GRID_W = 64
CHUNK = 128
EPS = 1e-6
RET_HEADS = 8
RET_DH = 64
ROPE_BASE = 10000.0
NA_HEADS = 8
NA_DH = 64
NA_WIN_R = 8
NA_WIN_C = 16
NA_ROWS_PER_STEP = 4
SSD_HEADDIM = 64
SSD_GROUPS = 4
SSD_STATE = 128
ADAM_LR = 0.001
ADAM_B1 = 0.9
ADAM_B2 = 0.999
ADAM_EPS = 1e-08
ADAM_WD = 0.01
ADAM_STEP = 10

LANES = 128
HEAD_W = 64
PACK_W = 512
PACK_ROWS = 1024
PACK_ALIGN = 16
COPY_CHUNKS = 2
VMEM_LIMIT = 56 * 1024 * 1024
MM_BLOCK_BYTES = 6 * 1024 * 1024
N_CHIPS = 4
N_DEV = 8
NEG_INF = -1e30

_DIMS = {"nn": (((1,), (0,)), ((), ())), "nt": (((1,), (1,)), ((), ())), "tn": (((0,), (0,)), ((), ()))}


def _cparams(sem=None):
    return pltpu.CompilerParams(dimension_semantics=sem, vmem_limit_bytes=VMEM_LIMIT)


def _pick(dim, cands):
    for c in cands:
        if dim % c == 0:
            return c
    return dim


def _divisor_tile(dim, fits, align):
    for d in range(1, dim + 1):
        t = dim // d
        if dim % d == 0 and t % align == 0 and fits(t):
            return t
    return dim


def _bdot_raw(a, b, mode):
    return lax.dot_general(a.astype(MXU_DTYPE), b.astype(MXU_DTYPE), _DIMS[mode], preferred_element_type=f32)


@functools.partial(jax.custom_vjp, nondiff_argnums=(2,))
def bdot(a, b, mode):
    return _bdot_raw(a, b, mode)


def _bdot_fwd(a, b, mode):
    return _bdot_raw(a, b, mode), (a, b)


def _bdot_bwd(mode, res, g):
    a, b = res
    if mode == "nn":
        da, db = _bdot_raw(g, b, "nt"), _bdot_raw(a, g, "tn")
    elif mode == "nt":
        da, db = _bdot_raw(g, b, "nn"), _bdot_raw(g, a, "tn")
    else:
        da, db = _bdot_raw(b, g, "nt"), _bdot_raw(a, g, "nn")
    return da.astype(a.dtype), db.astype(b.dtype)


bdot.defvjp(_bdot_fwd, _bdot_bwd)


def _mm_call(a, b, mode, out_dtype):
    if mode == "nn":
        (M, K), (K2, N) = a.shape, b.shape
    elif mode == "nt":
        (M, K), (N, K2) = a.shape, b.shape
    else:
        (K, M), (K2, N) = a.shape, b.shape
    assert K == K2, (a.shape, b.shape, mode)
    a_bytes, b_bytes, o_bytes = a.dtype.itemsize, b.dtype.itemsize, jnp.dtype(out_dtype).itemsize
    if mode == "tn":
        tn = _divisor_tile(N, lambda t: t <= 1536, LANES)
        tm = _divisor_tile(M, lambda t: t * tn * 4 <= MM_BLOCK_BYTES, 8)
        tk = _divisor_tile(K, lambda t: t * tm * a_bytes <= MM_BLOCK_BYTES and t * tn * b_bytes <= MM_BLOCK_BYTES, LANES)
    else:
        tk, tn = K, N
        tm = _divisor_tile(M, lambda t: t * K * a_bytes <= MM_BLOCK_BYTES and t * N * o_bytes <= MM_BLOCK_BYTES, 8)
    nk = K // tk
    if mode == "nn":
        a_spec = pl.BlockSpec((tm, tk), lambda i, j, k: (i, k))
        b_spec = pl.BlockSpec((tk, tn), lambda i, j, k: (k, j))
    elif mode == "nt":
        a_spec = pl.BlockSpec((tm, tk), lambda i, j, k: (i, k))
        b_spec = pl.BlockSpec((tn, tk), lambda i, j, k: (j, k))
    else:
        a_spec = pl.BlockSpec((tk, tm), lambda i, j, k: (k, i))
        b_spec = pl.BlockSpec((tk, tn), lambda i, j, k: (k, j))

    if nk == 1:
        def body(a_ref, b_ref, o_ref):
            o_ref[...] = _bdot_raw(a_ref[...], b_ref[...], mode).astype(o_ref.dtype)
    else:
        def body(a_ref, b_ref, o_ref, acc_ref):
            k = pl.program_id(2)

            @pl.when(k == 0)
            def _():
                acc_ref[...] = jnp.zeros_like(acc_ref)

            acc_ref[...] += _bdot_raw(a_ref[...], b_ref[...], mode)

            @pl.when(k == nk - 1)
            def _():
                o_ref[...] = acc_ref[...].astype(o_ref.dtype)

    return pl.pallas_call(
        body,
        out_shape=jax.ShapeDtypeStruct((M, N), out_dtype),
        grid=(M // tm, N // tn, nk),
        in_specs=[a_spec, b_spec],
        out_specs=pl.BlockSpec((tm, tn), lambda i, j, k: (i, j)),
        scratch_shapes=[pltpu.VMEM((tm, tn), f32)] if nk > 1 else [],
        compiler_params=_cparams(("parallel", "parallel", "arbitrary")),
        name="mm_" + mode,
    )(a, b)


def mm(a, b, mode="nn", out_dtype=f32):
    @jax.custom_vjp
    def op(a, b):
        return _mm_call(a, b, mode, out_dtype)

    def fwd(a, b):
        return _mm_call(a, b, mode, out_dtype), (a, b)

    def bwd(res, g):
        a, b = res
        if mode == "nn":
            return _mm_call(g, b, "nt", a.dtype), _mm_call(a, g, "tn", b.dtype)
        if mode == "nt":
            return _mm_call(g, b, "nn", a.dtype), _mm_call(g, a, "tn", b.dtype)
        return _mm_call(b, g, "nt", a.dtype), _mm_call(a, g, "nn", b.dtype)

    op.defvjp(fwd, bwd)
    return op(a, b)


def _row_tile(S, row_bytes):
    tm = 512
    while tm > 8 and (tm * row_bytes > (6 << 20) or S % tm):
        tm //= 2
    return tm


def rowwise(fn, name, rows, params, out_dtypes, n_diff_rows=None, n_diff_params=None, ncol=1, bwd_fn=None):
    rows, params = list(rows), list(params)
    nr, npar = len(rows), len(params)
    ndr = nr if n_diff_rows is None else n_diff_rows
    ndp = npar if n_diff_params is None else n_diff_params
    S = rows[0].shape[0]
    rw = [r.shape[1] // ncol for r in rows]
    pshape = [(p.shape[0], p.shape[1] // ncol) for p in params]

    def block_structs(tm):
        return ([jax.ShapeDtypeStruct((tm, w), f32) for w in rw] + [jax.ShapeDtypeStruct(s, f32) for s in pshape])

    outs_s = jax.eval_shape(fn, *block_structs(8))
    ow = [o.shape[1] for o in outs_s]
    nout = len(ow)
    row_bytes = 4 * (sum(rw) * 2 + sum(ow) * 2)
    tm = _row_tile(S, row_bytes)
    grid = (ncol, S // tm)

    def rspec(w):
        return pl.BlockSpec((tm, w), lambda g, i: (i, g))

    def pspec(s):
        return pl.BlockSpec(s, lambda g, i: (0, g))

    def call_fwd(*args):
        def body(*refs):
            vals = [r[...].astype(f32) for r in refs[:nr + npar]]
            res = fn(*vals)
            for o, r in zip(refs[nr + npar:], res):
                o[...] = r.astype(o.dtype)

        return pl.pallas_call(
            body,
            out_shape=[jax.ShapeDtypeStruct((S, w * ncol), dt) for w, dt in zip(ow, out_dtypes)],
            grid=grid,
            in_specs=[rspec(w) for w in rw] + [pspec(s) for s in pshape],
            out_specs=[rspec(w) for w in ow],
            compiler_params=_cparams(("parallel", "parallel")),
            name=name + "_fwd",
        )(*args)

    def call_bwd(args, douts):
        def body(*refs):
            in_refs = refs[:nr + npar]
            do_refs = refs[nr + npar:nr + npar + nout]
            dr_refs = refs[nr + npar + nout:nr + npar + nout + ndr]
            dp_refs = refs[nr + npar + nout + ndr:]
            rv = [r[...] for r in in_refs[:nr]]
            pv = [r[...] for r in in_refs[nr:]]
            dos = [d[...].astype(f32) for d in do_refs]
            if bwd_fn is not None:
                drs, dps = bwd_fn(rv, pv, dos)
            else:
                def f(*a):
                    return fn(*a[:ndr], *rv[ndr:], *a[ndr:], *pv[ndp:])

                _, vjp = jax.vjp(f, *[v.astype(f32) for v in rv[:ndr]], *pv[:ndp])
                cts = vjp(tuple(dos))
                drs, dps = cts[:ndr], cts[ndr:]
            for r, ct in zip(dr_refs, drs):
                r[...] = ct.astype(r.dtype)
            if ndp:
                @pl.when(pl.program_id(1) == 0)
                def _():
                    for r in dp_refs:
                        r[...] = jnp.zeros_like(r)

                for r, ct in zip(dp_refs, dps):
                    r[...] += ct

        return pl.pallas_call(
            body,
            out_shape=[jax.ShapeDtypeStruct(r.shape, r.dtype) for r in rows[:ndr]]
            + [jax.ShapeDtypeStruct(p.shape, f32) for p in params[:ndp]],
            grid=grid,
            in_specs=[rspec(w) for w in rw] + [pspec(s) for s in pshape] + [rspec(w) for w in ow],
            out_specs=[rspec(w) for w in rw[:ndr]] + [pspec(s) for s in pshape[:ndp]],
            compiler_params=_cparams(("parallel", "arbitrary")),
            name=name + "_bwd",
        )(*args, *douts)

    @jax.custom_vjp
    def op(*args):
        return tuple(call_fwd(*args))

    def fwd(*args):
        return tuple(call_fwd(*args)), args

    def bwd(args, douts):
        res = call_bwd(args, douts)
        drs, dps = res[:ndr], res[ndr:]
        out = list(drs) + [jnp.zeros_like(a) for a in args[ndr:nr]]
        out += [dp.astype(p.dtype) for dp, p in zip(dps, args[nr:nr + ndp])]
        out += [jnp.zeros_like(a) for a in args[nr + ndp:]]
        return tuple(out)

    op.defvjp(fwd, bwd)
    return op(*rows, *params)


def _silu(x):
    return x * (1.0 / (1.0 + jnp.exp(-x)))


def _softplus(x):
    return jnp.maximum(x, 0.0) + jnp.log(1.0 + jnp.exp(-jnp.abs(x)))


def _gelu_tanh(x):
    return 0.5 * x * (1.0 + jnp.tanh(math.sqrt(2.0 / math.pi) * (x + 0.044715 * (x * x * x))))


def _rms_fn(x, g):
    return x * lax.rsqrt(jnp.mean(x * x, axis=-1, keepdims=True) + EPS) * g


def _rms_bwd(x, g, dy):
    r = lax.rsqrt(jnp.mean(x * x, axis=-1, keepdims=True) + EPS)
    xh = x * r
    dxh = dy * g
    dx = r * (dxh - xh * jnp.mean(dxh * xh, axis=-1, keepdims=True))
    return dx, jnp.sum(dy * xh, axis=0, keepdims=True)


def rms(x, g, out_dtype):
    def bwd_fn(rv, pv, dos):
        dx, dg = _rms_bwd(rv[0], pv[0], dos[0])
        return (dx,), (dg,)

    return rowwise(lambda x, g: (_rms_fn(x, g),), "rms", [x], [g.reshape(1, -1)], [out_dtype], bwd_fn=bwd_fn)[0]


def rms_residual_norm(m, g, x, g_next):
    def fn(m, x, g, gn):
        xn = x + _rms_fn(m, g)
        return xn, _rms_fn(xn, gn)

    def bwd_fn(rv, pv, dos):
        (m, x), (g, gn), (dxn, dhn) = rv, pv, dos
        xn = x + _rms_fn(m, g)
        d_from_norm, dgn = _rms_bwd(xn, gn, dhn)
        dxn = dxn + d_from_norm
        dm, dg = _rms_bwd(m, g, dxn)
        return (dm, dxn), (dg, dgn)

    return rowwise(fn, "rms_res_norm", [m, x], [g.reshape(1, -1), g_next.reshape(1, -1)], [f32, MXU_DTYPE], bwd_fn=bwd_fn)


def rms_residual(m, g, x):
    def bwd_fn(rv, pv, dos):
        dm, dg = _rms_bwd(rv[0], pv[0], dos[0])
        return (dm, dos[0]), (dg,)

    return rowwise(lambda m, x, g: (x + _rms_fn(m, g),), "rms_res", [m, x], [g.reshape(1, -1)], [f32], bwd_fn=bwd_fn)[0]


def loss_op(y, tgt):
    S, D = y.shape
    tm = _row_tile(S, 4 * D * 4)

    def call_fwd(y, tgt):
        def body(y_ref, t_ref, o_ref):
            @pl.when(pl.program_id(0) == 0)
            def _():
                o_ref[...] = jnp.zeros_like(o_ref)

            e = y_ref[...] - t_ref[...]
            o_ref[...] += 0.5 * jnp.sum(jnp.mean(e * e, axis=-1, keepdims=True))

        out = pl.pallas_call(
            body,
            out_shape=jax.ShapeDtypeStruct((8, LANES), f32),
            grid=(S // tm,),
            in_specs=[pl.BlockSpec((tm, D), lambda i: (i, 0))] * 2,
            out_specs=pl.BlockSpec((8, LANES), lambda i: (0, 0)),
            compiler_params=_cparams(("arbitrary",)),
            name="loss_fwd",
        )(y, tgt)
        return out[0, 0]

    def call_bwd(y, tgt, g):
        def body(y_ref, t_ref, g_ref, o_ref):
            o_ref[...] = (y_ref[...] - t_ref[...]) * (g_ref[...] * (1.0 / D))

        return pl.pallas_call(
            body,
            out_shape=jax.ShapeDtypeStruct((S, D), f32),
            grid=(S // tm,),
            in_specs=[pl.BlockSpec((tm, D), lambda i: (i, 0))] * 2 + [pl.BlockSpec((1, 1), lambda i: (0, 0))],
            out_specs=pl.BlockSpec((tm, D), lambda i: (i, 0)),
            compiler_params=_cparams(("parallel",)),
            name="loss_bwd",
        )(y, tgt, g.reshape(1, 1).astype(f32))

    @jax.custom_vjp
    def op(y, tgt):
        return call_fwd(y, tgt)

    def fwd(y, tgt):
        return call_fwd(y, tgt), (y, tgt)

    def bwd(res, g):
        y, tgt = res
        return call_bwd(y, tgt, g), jnp.zeros_like(tgt)

    op.defvjp(fwd, bwd)
    return op(y, tgt)


HALO = 8


def _conv_tile(S, R):
    def ext(ref, r0):
        cur = ref[pl.ds(r0, R), :]
        prev = ref[pl.ds(pl.multiple_of(jnp.maximum(r0 - HALO, 0), HALO), HALO), :]
        nxt = ref[pl.ds(pl.multiple_of(jnp.minimum(r0 + R, S - HALO), HALO), HALO), :]
        prev = jnp.where(r0 > 0, prev, 0.0)
        nxt = jnp.where(r0 + R < S, nxt, 0.0)
        return jnp.concatenate([prev, cur, nxt], axis=0)

    return ext


def _shift_rows(e, k, R):
    n = e.shape[0]
    if k == 0:
        return e[HALO:HALO + R]
    return pltpu.roll(e, (-k) % n, 0)[HALO:HALO + R]


def _silu_bwd(us, dy):
    u, = us
    s = 1.0 / (1.0 + jnp.exp(-u))
    return (dy * (s * (1.0 + u * (1.0 - s))),)


def _geglu(g, v):
    return _gelu_tanh(g) * v


def _geglu_bwd(us, dy):
    g, v = us
    c = math.sqrt(2.0 / math.pi)
    t = jnp.tanh(c * (g + 0.044715 * (g * g * g)))
    half = 0.5 * (1.0 + t)
    dgelu = half + 0.5 * g * (1.0 - t * t) * (c * (1.0 + 3.0 * 0.044715 * (g * g)))
    return dy * v * dgelu, dy * (g * half)


def mm_conv_act(h, ws, cws, cbs, act, act_bwd, out_dtype, name):
    n = len(ws)
    S = h.shape[0]
    C = ws[0].shape[1]
    W = cws[0].shape[0]
    pad = W // 2
    bw = _pick(C, (LANES,))
    R = _pick(S, (256, 128, 64, 32, 16, 8))
    nt = S // R
    ext = _conv_tile(S, R)
    col = lambda rows: pl.BlockSpec((rows, bw), lambda j: (0, j))

    def conv(e, wv, bv):
        acc = bv + wv[pad] * e[HALO:HALO + R]
        for j in range(W):
            if j != pad:
                acc = acc + wv[j] * _shift_rows(e, j - pad, R)
        return acc

    def call_fwd(xs, cws, cbs):
        def body(*refs):
            x_refs, w_refs, b_refs, y_ref = refs[:n], refs[n:2 * n], refs[2 * n:3 * n], refs[3 * n]
            wvs = [[w[j:j + 1, :] for j in range(W)] for w in w_refs]
            bvs = [b[...] for b in b_refs]

            def tile(i, c):
                r0 = pl.multiple_of(i * R, R)
                us = [conv(ext(x, r0), wv, bv) for x, wv, bv in zip(x_refs, wvs, bvs)]
                y_ref[pl.ds(r0, R), :] = act(*us).astype(y_ref.dtype)
                return c

            lax.fori_loop(0, nt, tile, 0)

        return pl.pallas_call(
            body,
            out_shape=jax.ShapeDtypeStruct((S, C), out_dtype),
            grid=(C // bw,),
            in_specs=[col(S)] * n + [col(W)] * n + [col(1)] * n,
            out_specs=col(S),
            compiler_params=_cparams(("parallel",)),
            name=name + "_fwd",
        )(*xs, *cws, *cbs)

    def call_bwd(xs, cws, cbs, dy):
        def body(*refs):
            x_refs, w_refs, b_refs, dy_ref = refs[:n], refs[n:2 * n], refs[2 * n:3 * n], refs[3 * n]
            dx_refs, dw_refs, db_refs = refs[3 * n + 1:4 * n + 1], refs[4 * n + 1:5 * n + 1], refs[5 * n + 1:6 * n + 1]
            du_scr = refs[6 * n + 1:]
            wvs = [[w[j:j + 1, :] for j in range(W)] for w in w_refs]
            bvs = [b[...] for b in b_refs]
            zero = jnp.zeros((1, bw), f32)

            def first(i, dbs):
                r0 = pl.multiple_of(i * R, R)
                us = [conv(ext(x, r0), wv, bv) for x, wv, bv in zip(x_refs, wvs, bvs)]
                dus = act_bwd(us, dy_ref[pl.ds(r0, R), :].astype(f32))
                for scr, du in zip(du_scr, dus):
                    scr[pl.ds(r0, R), :] = du
                return tuple(db + jnp.sum(du, axis=0, keepdims=True) for db, du in zip(dbs, dus))

            dbs = lax.fori_loop(0, nt, first, tuple(zero for _ in range(n)))

            def second(i, dws):
                r0 = pl.multiple_of(i * R, R)
                new = []
                for x, scr, dx, wv, dw in zip(x_refs, du_scr, dx_refs, wvs, dws):
                    ex, ed = ext(x, r0), ext(scr, r0)
                    d0 = ed[HALO:HALO + R]
                    acc = jnp.zeros((R, bw), f32)
                    row = []
                    for j in range(W):
                        acc = acc + wv[j] * _shift_rows(ed, pad - j, R)
                        row.append(dw[j] + jnp.sum(d0 * _shift_rows(ex, j - pad, R), axis=0, keepdims=True))
                    dx[pl.ds(r0, R), :] = acc.astype(dx.dtype)
                    new.append(tuple(row))
                return tuple(new)

            dws = lax.fori_loop(0, nt, second, tuple(tuple(zero for _ in range(W)) for _ in range(n)))
            for dw_ref, db_ref, dw, db in zip(dw_refs, db_refs, dws, dbs):
                dw_ref[...] = jnp.zeros_like(dw_ref)
                for j in range(W):
                    dw_ref[j:j + 1, :] = dw[j]
                db_ref[...] = db

        return pl.pallas_call(
            body,
            out_shape=[jax.ShapeDtypeStruct((S, C), MXU_DTYPE)] * n + [jax.ShapeDtypeStruct((8, C), f32)] * n
            + [jax.ShapeDtypeStruct((1, C), f32)] * n,
            grid=(C // bw,),
            in_specs=[col(S)] * n + [col(W)] * n + [col(1)] * n + [col(S)],
            out_specs=[col(S)] * n + [col(8)] * n + [col(1)] * n,
            scratch_shapes=[pltpu.VMEM((S, bw), f32)] * n,
            compiler_params=_cparams(("parallel",)),
            name=name + "_bwd",
        )(*xs, *cws, *cbs, dy)

    @jax.custom_vjp
    def op(h, ws, cws, cbs):
        return call_fwd([_mm_call(h, w, "nn", f32) for w in ws], cws, cbs)

    def fwd(h, ws, cws, cbs):
        xs = [_mm_call(h, w, "nn", f32) for w in ws]
        return call_fwd(xs, cws, cbs), (h, ws, xs, cws, cbs)

    def bwd(res, dy):
        h, ws, xs, cws, cbs = res
        out = call_bwd(xs, cws, cbs, dy)
        dxs, dcws, dcbs = out[:n], out[n:2 * n], out[2 * n:]
        dh = _mm_call(dxs[0], ws[0], "nt", h.dtype)
        for dx, w in zip(dxs[1:], ws[1:]):
            dh = dh + _mm_call(dx, w, "nt", h.dtype)
        dws = tuple(_mm_call(h, dx, "tn", w.dtype) for dx, w in zip(dxs, ws))
        return dh, dws, tuple(d[:W] for d in dcws), tuple(dcbs)

    op.defvjp(fwd, bwd)
    return op(h, tuple(ws), tuple(cws), tuple(b.reshape(1, C) for b in cbs))


@jax.custom_vjp
def _masked_decay(cs_col, cs_row, mask01):
    return jnp.where(mask01 > 0, jnp.exp(cs_col - cs_row), 0.0)


def _masked_decay_fwd(cs_col, cs_row, mask01):
    d = jnp.where(mask01 > 0, jnp.exp(cs_col - cs_row), 0.0)
    return d, (d, mask01)


def _masked_decay_bwd(res, g):
    d, mask01 = res
    t = g * d
    return jnp.sum(t, axis=1, keepdims=True), -jnp.sum(t, axis=0, keepdims=True), jnp.zeros_like(mask01)


_masked_decay.defvjp(_masked_decay_fwd, _masked_decay_bwd)


def _scan_chunk(qs, ks, xs, cs_tok, dt_tok, hs, *, rev, incl, nsub):
    nb = len(xs)
    L, N = qs[0].shape
    W = xs[0].shape[1]
    Hg = cs_tok.shape[1]
    nh = W // HEAD_W
    shared = len(qs) == 1
    t = lax.broadcasted_iota(jnp.int32, (L, L), 0)
    l = lax.broadcasted_iota(jnp.int32, (L, L), 1)
    if rev:
        mask = (l >= t) if incl else (l > t)
    else:
        mask = (l <= t) if incl else (l < t)
    mask01 = mask.astype(f32)
    lane_a = lax.broadcasted_iota(jnp.int32, cs_tok.shape, 1)
    row_a = lax.broadcasted_iota(jnp.int32, (Hg, L), 0)
    last = lax.broadcasted_iota(jnp.int32, (1, L), 1) == (0 if rev else L - 1)
    vhead = lax.broadcasted_iota(jnp.int32, (1, W), 1) // HEAD_W
    qhead = lax.broadcasted_iota(jnp.int32, (1, N), 1) // (N // nsub)
    cs_rows = lax.dot_general(cs_tok, (t == l).astype(f32), _DIMS["tn"], precision=lax.Precision.HIGHEST,
                              preferred_element_type=f32)

    def by_head(vals):
        if len(vals) == 2:
            return jnp.where(vhead == 0, vals[0], vals[1])
        return sum(jnp.where(vhead == i, v, 0.0) for i, v in enumerate(vals))

    decay, lam_e, tau_e, gam_e, dt_e = [], [], [], [], []
    for b in range(nb):
        cs_cols, tots, dt_cols = [], [], []
        for i in range(nh):
            head = b * nh + i
            cs_col = jnp.sum(jnp.where(lane_a == head, cs_tok, 0.0), axis=1, keepdims=True)
            cs_row = jnp.sum(jnp.where(row_a == head, cs_rows, 0.0), axis=0, keepdims=True)
            tots.append(jnp.sum(jnp.where(last, cs_row, 0.0), axis=1, keepdims=True))
            decay.append(_masked_decay(cs_col, cs_row, mask01))
            cs_cols.append(cs_col)
            if dt_tok is not None:
                dt_cols.append(jnp.sum(jnp.where(lane_a == head, dt_tok, 0.0), axis=1, keepdims=True))
        cs_e, tot_e = by_head(cs_cols), by_head(tots)
        lam_e.append(jnp.exp(cs_e))
        tau_e.append(jnp.exp(tot_e - cs_e))
        gam_e.append(jnp.exp(tot_e))
        if dt_tok is not None:
            dt_e.append(by_head(dt_cols))
    vs = [x if dt_tok is None else x * dt_e[b] for b, x in enumerate(xs)]
    qk = lambda b: (qs[0], ks[0]) if shared else (qs[b], ks[b])
    if nsub == 1:
        scores = [bdot(qs[0], ks[0], "nt")] if shared else [bdot(*qk(b), "nt") for b in range(nb)]
        score = lambda b, i: scores[0 if shared else b]
    else:
        scores = [[bdot(jnp.where(qhead == i, qk(b)[0], 0.0), qk(b)[1], "nt") for i in range(nh)] for b in range(nb)]
        score = lambda b, i: scores[b][i]
    ys = [lam_e[b] * bdot(qk(b)[0], hs[b], "nn")
          + by_head([bdot(score(b, i) * decay[b * nh + i], vs[b], "nn") for i in range(nh)]) for b in range(nb)]
    hns = [gam_e[b] * hs[b] + bdot(qk(b)[1], tau_e[b] * vs[b], "tn") for b in range(nb)]
    if nsub > 1:
        nhead = lax.broadcasted_iota(jnp.int32, (N, W), 0) // (N // nsub)
        keep = nhead == lax.broadcasted_iota(jnp.int32, (N, W), 1) // HEAD_W
        hns = [jnp.where(keep, hn, 0.0) for hn in hns]
    return ys, hns


def chunk_cumsum(a_tok, rev):
    G, S, Hg = a_tok.shape
    L = CHUNK
    CB = _pick(S // L, (16, 8, 4, 2))

    def call(a, rev):
        def body(a_ref, o_ref):
            t = lax.broadcasted_iota(jnp.int32, (L, L), 0)
            l = lax.broadcasted_iota(jnp.int32, (L, L), 1)
            tri = ((l >= t) if rev else (l <= t)).astype(f32)
            for j in range(CB):
                o_ref[0, j * L:(j + 1) * L, :] = _exact_dot(tri, a_ref[0, j * L:(j + 1) * L, :])

        spec = pl.BlockSpec((1, CB * L, Hg), lambda g, c: (g, c, 0))
        return pl.pallas_call(
            body,
            out_shape=jax.ShapeDtypeStruct((G, S, Hg), f32),
            grid=(G, S // (CB * L)),
            in_specs=[spec],
            out_specs=spec,
            compiler_params=_cparams(("parallel", "parallel")),
            name="chunk_cumsum",
        )(a)

    @jax.custom_vjp
    def op(a):
        return call(a, rev)

    def fwd(a):
        return call(a, rev), None

    def bwd(_, g):
        return (call(g, not rev),)

    op.defvjp(fwd, bwd)
    return op(a_tok)


def scan_op(q, k, x, a_tok, dt_tok, *, rev, incl, nsub):
    S = q.shape[0]
    G, _, Hg = a_tok.shape
    N = q.shape[1] // G
    Vw = x.shape[1] // G
    L = CHUNK
    nc = S // L
    use_dt = dt_tok is not None
    PW = min(Vw, LANES)
    chunk = functools.partial(_scan_chunk, rev=rev, incl=incl, nsub=nsub)
    cols = [slice(p * PW, (p + 1) * PW) for p in range(Vw // PW)]
    own_qk = nsub > 1
    NB = PW if own_qk else N

    def order(c, backward):
        return (nc - 1 - c) if (rev != backward) else c

    def specs(backward):
        qs = pl.BlockSpec((L, N), lambda g, c: (order(c, backward), g))
        xs = pl.BlockSpec((L, Vw), lambda g, c: (order(c, backward), g))
        as_ = pl.BlockSpec((1, L, Hg), lambda g, c: (g, order(c, backward), 0))
        hs = pl.BlockSpec((1, 1, NB, Vw), lambda g, c: (g, order(c, backward), 0, 0))
        return qs, xs, as_, hs

    def call_fwd(q, k, x, a_tok, dt_tok, y_prev=None):
        qs, xs, as_, hs = specs(False)
        n_in = 4 + use_dt + (y_prev is not None)

        def body(*refs):
            q_ref, k_ref, x_ref, a_ref = refs[:4]
            dt_ref = refs[4] if use_dt else None
            yp_ref = refs[n_in - 1] if y_prev is not None else None
            y_ref, hs_ref, h_scr = refs[n_in:]

            @pl.when(pl.program_id(1) == 0)
            def _():
                h_scr[...] = jnp.zeros_like(h_scr)

            hs_ref[0, 0] = h_scr[...]
            q, k, a, dt = q_ref[...], k_ref[...], a_ref[0], dt_ref[0] if use_dt else None
            qs, ks = ([q[:, c] for c in cols], [k[:, c] for c in cols]) if own_qk else ([q], [k])
            ys, hns = chunk(qs, ks, [x_ref[:, c] for c in cols], a, dt, [h_scr[:, c] for c in cols])
            for c, y, hn in zip(cols, ys, hns):
                y_ref[:, c] = y if yp_ref is None else y + yp_ref[:, c]
                h_scr[:, c] = hn

        ins = [q, k, x, a_tok] + ([dt_tok] if use_dt else []) + ([y_prev] if y_prev is not None else [])
        return pl.pallas_call(
            body,
            out_shape=[jax.ShapeDtypeStruct((S, G * Vw), f32), jax.ShapeDtypeStruct((G, nc, NB, Vw), f32)],
            grid=(G, nc),
            in_specs=[qs, qs, xs, as_] + ([as_] if use_dt else []) + ([xs] if y_prev is not None else []),
            out_specs=[xs, hs],
            scratch_shapes=[pltpu.VMEM((NB, Vw), f32)],
            compiler_params=_cparams(("parallel", "arbitrary")),
            name="scan_fwd",
        )(*ins)

    def call_bwd(q, k, x, a_tok, dt_tok, hsave, dy, acc=None):
        qs, xs, as_, hs = specs(True)
        n_in = 6 + use_dt + (3 if acc is not None else 0)

        def body(*refs):
            q_ref, k_ref, x_ref, a_ref = refs[:4]
            dt_ref = refs[4] if use_dt else None
            hs_ref, dy_ref = refs[4 + use_dt], refs[5 + use_dt]
            acc_refs = refs[n_in - 3:n_in] if acc is not None else None
            dq_ref, dk_ref, dx_ref, da_ref = refs[n_in:n_in + 4]
            ddt_ref = refs[n_in + 4] if use_dt else None
            dh_scr = refs[-1]

            @pl.when(pl.program_id(1) == 0)
            def _():
                dh_scr[...] = jnp.zeros_like(dh_scr)

            q, k, a = q_ref[...].astype(f32), k_ref[...].astype(f32), a_ref[0]
            qs, ks = ([q[:, c] for c in cols], [k[:, c] for c in cols]) if own_qk else ([q], [k])
            xs, hs_in = [x_ref[:, c] for c in cols], [hs_ref[0, 0, :, c] for c in cols]
            if use_dt:
                _, vjp = jax.vjp(chunk, qs, ks, xs, a, dt_ref[0], hs_in)
            else:
                _, vjp = jax.vjp(lambda qs, ks, xs, a, hs: chunk(qs, ks, xs, a, None, hs), qs, ks, xs, a, hs_in)
            cts = vjp(([dy_ref[:, c] for c in cols], [dh_scr[:, c] for c in cols]))
            dqs, dks, dxs, da, dhs = cts[0], cts[1], cts[2], cts[3], cts[-1]
            if acc is not None:
                dq_acc, dk_acc = acc_refs[0][...].astype(f32), acc_refs[1][...].astype(f32)
            if own_qk:
                for b, c in enumerate(cols):
                    dq_ref[:, c] = (dqs[b] if acc is None else dqs[b] + dq_acc[:, c]).astype(dq_ref.dtype)
                    dk_ref[:, c] = (dks[b] if acc is None else dks[b] + dk_acc[:, c]).astype(dk_ref.dtype)
            else:
                dq_ref[...] = (dqs[0] if acc is None else dqs[0] + dq_acc).astype(dq_ref.dtype)
                dk_ref[...] = (dks[0] if acc is None else dks[0] + dk_acc).astype(dk_ref.dtype)
            for b, c in enumerate(cols):
                dx_ref[:, c] = dxs[b] if acc is None else dxs[b] + acc_refs[2][:, c]
                dh_scr[:, c] = dhs[b]
            da_ref[0] = da
            if use_dt:
                ddt_ref[0] = cts[4]

        ins = [q, k, x, a_tok] + ([dt_tok] if use_dt else []) + [hsave, dy] + (list(acc) if acc is not None else [])
        a_shape = jax.ShapeDtypeStruct(a_tok.shape, f32)
        return pl.pallas_call(
            body,
            out_shape=[jax.ShapeDtypeStruct(q.shape, q.dtype), jax.ShapeDtypeStruct(k.shape, k.dtype),
                       jax.ShapeDtypeStruct(x.shape, f32), a_shape] + ([a_shape] if use_dt else []),
            grid=(G, nc),
            in_specs=[qs, qs, xs, as_] + ([as_] if use_dt else []) + [hs, xs] + ([qs, qs, xs] if acc is not None else []),
            out_specs=[qs, qs, xs, as_] + ([as_] if use_dt else []),
            scratch_shapes=[pltpu.VMEM((NB, Vw), f32)],
            compiler_params=_cparams(("parallel", "arbitrary")),
            name="scan_bwd",
        )(*ins)

    return call_fwd, call_bwd


def bidir_scan(q, k, x, a_f, a_b, dt_f, dt_b, *, nsub):
    use_dt = dt_f is not None
    a_f, a_b = chunk_cumsum(a_f, False), chunk_cumsum(a_b, True)
    fwd_f, bwd_f = scan_op(q, k, x, a_f, dt_f, rev=False, incl=True, nsub=nsub)
    fwd_b, bwd_b = scan_op(q, k, x, a_b, dt_b, rev=True, incl=False, nsub=nsub)

    def run(q, k, x, a_f, a_b, dt_f, dt_b):
        y_f, hs_f = fwd_f(q, k, x, a_f, dt_f)
        y, hs_b = fwd_b(q, k, x, a_b, dt_b, y_prev=y_f)
        return y, (hs_f, hs_b)

    def grads(q, k, x, a_f, a_b, dt_f, dt_b, hs, dy):
        first = bwd_f(q, k, x, a_f, dt_f, hs[0], dy)
        both = bwd_b(q, k, x, a_b, dt_b, hs[1], dy, acc=first[:3])
        return both[0], both[1], both[2], first[3], both[3], (first[4] if use_dt else None), (both[4] if use_dt else None)

    if use_dt:
        @jax.custom_vjp
        def op(q, k, x, a_f, a_b, dt_f, dt_b):
            return run(q, k, x, a_f, a_b, dt_f, dt_b)[0]

        def fwd(q, k, x, a_f, a_b, dt_f, dt_b):
            y, hs = run(q, k, x, a_f, a_b, dt_f, dt_b)
            return y, (q, k, x, a_f, a_b, dt_f, dt_b, hs)

        def bwd(res, dy):
            return grads(*res, dy)

        op.defvjp(fwd, bwd)
        return op(q, k, x, a_f, a_b, dt_f, dt_b)

    @jax.custom_vjp
    def op(q, k, x, a_f, a_b):
        return run(q, k, x, a_f, a_b, None, None)[0]

    def fwd(q, k, x, a_f, a_b):
        y, hs = run(q, k, x, a_f, a_b, None, None)
        return y, (q, k, x, a_f, a_b, hs)

    def bwd(res, dy):
        q, k, x, a_f, a_b, hs = res
        return grads(q, k, x, a_f, a_b, None, None, hs, dy)[:5]

    op.defvjp(fwd, bwd)
    return op(q, k, x, a_f, a_b)


def _swap_halves(x, dh):
    W = x.shape[1]
    lane = lax.broadcasted_iota(jnp.int32, (1, W), 1) % dh
    return jnp.where(lane < dh // 2, pltpu.roll(x, W - dh // 2, 1), pltpu.roll(x, dh // 2, 1))


def rotary(rq, rk, cos_t, sin_t):
    scale = RET_DH ** -0.5

    def fn(rq, rk, c, s):
        return rq * c + _swap_halves(rq, RET_DH) * s, (rk * c + _swap_halves(rk, RET_DH) * s) * scale

    def bwd_fn(rv, pv, dos):
        _, _, c, s = rv
        dq, dk = dos
        dk = dk * scale
        return (dq * c + _swap_halves(dq * s, RET_DH), dk * c + _swap_halves(dk * s, RET_DH)), ()

    return rowwise(fn, "rotary", [rq, rk, cos_t, sin_t], [], [MXU_DTYPE, MXU_DTYPE], n_diff_rows=2, bwd_fn=bwd_fn)


def _rope_tables(S, width):
    half = RET_DH // 2
    inv = 1.0 / (ROPE_BASE ** (jnp.arange(half, dtype=f32) / half))
    ang = jnp.arange(S, dtype=f32)[:, None] * inv[None, :]
    cos, sin = jnp.cos(ang), jnp.sin(ang)
    reps = width // RET_DH
    return jnp.tile(jnp.concatenate([cos, cos], axis=1), (1, reps)), jnp.tile(jnp.concatenate([-sin, sin], axis=1), (1, reps))


def _exact_dot(x, m):
    return jnp.dot(x, m, precision=lax.Precision.HIGHEST, preferred_element_type=f32)


def ret_post(y, rg, gn_g):
    W = y.shape[1]
    idx = np.arange(W) // RET_DH
    avg = jnp.asarray((idx[:, None] == idx[None, :]).astype(np.float32) / RET_DH)

    def fn(y, rg, g, avg):
        mu = _exact_dot(y, avg)
        d = y - mu
        var = _exact_dot(d * d, avg)
        return (_silu(rg) * (d * lax.rsqrt(var + EPS) * g),)

    return rowwise(fn, "ret_post", [y, rg], [gn_g.reshape(1, -1), avg], [MXU_DTYPE], n_diff_params=1)[0]


def _na_bias(rpb, win_r):
    H = rpb.shape[0]
    qc = np.arange(GRID_W)[:, None]
    kc = np.arange(GRID_W)[None, :]
    cstart = np.clip(qc - NA_WIN_C // 2, 0, GRID_W - NA_WIN_C)
    valid = (kc >= cstart) & (kc < cstart + NA_WIN_C)
    dc = np.clip(kc - qc, -(NA_WIN_C - 1), NA_WIN_C - 1) + (NA_WIN_C - 1)
    onehot = (dc[None] == np.arange(2 * NA_WIN_C - 1)[:, None, None]).astype(np.float32)
    t1 = jnp.einsum("hrd,dqk->hrqk", rpb.astype(f32), jnp.asarray(onehot), precision=lax.Precision.HIGHEST)
    per_delta = [t1[:, NA_WIN_R - 1 - d:NA_WIN_R - 1 - d + win_r] for d in range(win_r)]
    b = jnp.stack(per_delta, axis=1)
    b = jnp.where(jnp.asarray(valid)[None, None, None], b, NEG_INF)
    return jnp.transpose(b, (0, 1, 3, 2, 4)).reshape(H, win_r, GRID_W, win_r * GRID_W)


def _na_rows(rows):
    lane = lax.broadcasted_iota(jnp.int32, (1, rows[0][0].shape[1]), 1) // NA_DH
    scale = NA_DH ** -0.5
    ss = [[_bdot_raw(jnp.where(lane == i, q, 0.0) * scale, kw, "nt") + b for i, b in enumerate(bs)] for q, kw, _, bs in rows]
    es = [[jnp.exp(s - jnp.max(s, axis=1, keepdims=True)) for s in srow] for srow in ss]
    ps = [[e / jnp.sum(e, axis=1, keepdims=True) for e in erow] for erow in es]
    return [_lanes_by_head(lane, [_bdot_raw(p, vw, "nn") for p in prow]) for prow, (_, _, vw, _) in zip(ps, rows)]


def _lanes_by_head(lane, vals):
    if len(vals) == 2:
        return jnp.where(lane == 0, vals[0], vals[1])
    return sum(jnp.where(lane == i, v, 0.0) for i, v in enumerate(vals))


def _na_rows_bwd(rows):
    lane = lax.broadcasted_iota(jnp.int32, (1, rows[0][0].shape[1]), 1) // NA_DH
    scale = NA_DH ** -0.5
    heads = range(len(rows[0][3]))
    qis = [[jnp.where(lane == i, q, 0.0) * scale for i in heads] for q, _, _, _, _ in rows]
    dos = [[jnp.where(lane == i, do, 0.0) for i in heads] for _, _, _, _, do in rows]
    ss = [[_bdot_raw(qi, kw, "nt") + b for qi, b in zip(qrow, bs)] for qrow, (_, kw, _, bs, _) in zip(qis, rows)]
    dps = [[_bdot_raw(doi, vw, "nt") for doi in drow] for drow, (_, _, vw, _, _) in zip(dos, rows)]
    es = [[jnp.exp(s - jnp.max(s, axis=1, keepdims=True)) for s in srow] for srow in ss]
    ps = [[e / jnp.sum(e, axis=1, keepdims=True) for e in erow] for erow in es]
    dss = [[p * (dp - jnp.sum(dp * p, axis=1, keepdims=True)) for p, dp in zip(prow, dprow)] for prow, dprow in zip(ps, dps)]
    out = []
    for qrow, drow, prow, dsrow, (_, kw, _, _, _) in zip(qis, dos, ps, dss, rows):
        dq = _lanes_by_head(lane, [_bdot_raw(dsrow[i], kw, "nn") for i in heads]) * scale
        dk, dv = 0.0, 0.0
        for i in heads:
            dk = dk + _bdot_raw(dsrow[i], qrow[i], "tn")
            dv = dv + _bdot_raw(prow[i], drow[i], "tn")
        out.append((dq, dk, dv, dsrow))
    return out


def na_op(nq, nk, nv, bias):
    S, W = nq.shape
    rows = S // GRID_W
    win_r = bias.shape[1]
    nkeys = win_r * GRID_W
    hp = LANES // NA_DH
    npair = W // LANES
    RB = min(16, rows)
    nrb = rows // RB
    qspec = pl.BlockSpec((RB * GRID_W, LANES), lambda p, r: (r, p))
    kspec = pl.BlockSpec((S, LANES), lambda p, r: (0, p))
    bspec = pl.BlockSpec((hp, win_r, GRID_W, nkeys), lambda p, r: (p, 0, 0, 0))

    def window(r):
        r0 = jnp.clip(r - win_r // 2, 0, rows - win_r)
        return pl.multiple_of(r0 * GRID_W, GRID_W), r - r0

    def call_fwd(nq, nk, nv, bias):
        def body(q_ref, k_ref, v_ref, b_ref, o_ref):
            rb = pl.program_id(1)

            def step(j, c):
                args, q0s = [], []
                for u in range(NA_ROWS_PER_STEP):
                    i = j * NA_ROWS_PER_STEP + u
                    k0, d = window(rb * RB + i)
                    q0 = pl.multiple_of(i * GRID_W, GRID_W)
                    q0s.append(q0)
                    args.append((q_ref[pl.ds(q0, GRID_W), :].astype(f32), k_ref[pl.ds(k0, nkeys), :], v_ref[pl.ds(k0, nkeys), :],
                                 [b_ref[h, pl.ds(d, 1)][0] for h in range(hp)]))
                for q0, o in zip(q0s, _na_rows(args)):
                    o_ref[pl.ds(q0, GRID_W), :] = o.astype(o_ref.dtype)
                return c

            lax.fori_loop(0, RB // NA_ROWS_PER_STEP, step, 0)

        return pl.pallas_call(
            body,
            out_shape=jax.ShapeDtypeStruct((S, W), nq.dtype),
            grid=(npair, nrb),
            in_specs=[qspec, kspec, kspec, bspec],
            out_specs=qspec,
            compiler_params=_cparams(("parallel", "arbitrary")),
            name="na_fwd",
        )(nq, nk, nv, bias)

    def call_bwd(nq, nk, nv, bias, do):
        def body(q_ref, k_ref, v_ref, b_ref, do_ref, dq_ref, dk_ref, dv_ref, db_ref, dk_acc, dv_acc):
            rb = pl.program_id(1)

            @pl.when(rb == 0)
            def _():
                dk_acc[...] = jnp.zeros_like(dk_acc)
                dv_acc[...] = jnp.zeros_like(dv_acc)
                db_ref[...] = jnp.zeros_like(db_ref)

            def step(j, c):
                args, where = [], []
                for u in range(NA_ROWS_PER_STEP):
                    i = j * NA_ROWS_PER_STEP + u
                    k0, d = window(rb * RB + i)
                    q0 = pl.multiple_of(i * GRID_W, GRID_W)
                    where.append((q0, k0, d))
                    args.append((q_ref[pl.ds(q0, GRID_W), :].astype(f32), k_ref[pl.ds(k0, nkeys), :], v_ref[pl.ds(k0, nkeys), :],
                                 [b_ref[h, pl.ds(d, 1)][0] for h in range(hp)], do_ref[pl.ds(q0, GRID_W), :].astype(f32)))
                for (q0, k0, d), (dq, dk, dv, dbs) in zip(where, _na_rows_bwd(args)):
                    dq_ref[pl.ds(q0, GRID_W), :] = dq.astype(dq_ref.dtype)
                    dk_acc[pl.ds(k0, nkeys), :] += dk
                    dv_acc[pl.ds(k0, nkeys), :] += dv
                    for h in range(hp):
                        db_ref[h, pl.ds(d, 1)] += dbs[h][None]
                return c

            lax.fori_loop(0, RB // NA_ROWS_PER_STEP, step, 0)

            @pl.when(rb == nrb - 1)
            def _():
                dk_ref[...] = dk_acc[...].astype(dk_ref.dtype)
                dv_ref[...] = dv_acc[...].astype(dv_ref.dtype)

        return pl.pallas_call(
            body,
            out_shape=[jax.ShapeDtypeStruct((S, W), nq.dtype), jax.ShapeDtypeStruct((S, W), nk.dtype),
                       jax.ShapeDtypeStruct((S, W), nv.dtype), jax.ShapeDtypeStruct(bias.shape, f32)],
            grid=(npair, nrb),
            in_specs=[qspec, kspec, kspec, bspec, qspec],
            out_specs=[qspec, kspec, kspec, bspec],
            scratch_shapes=[pltpu.VMEM((S, LANES), f32), pltpu.VMEM((S, LANES), f32)],
            compiler_params=_cparams(("parallel", "arbitrary")),
            name="na_bwd",
        )(nq, nk, nv, bias, do)

    @jax.custom_vjp
    def op(nq, nk, nv, bias):
        return call_fwd(nq, nk, nv, bias)

    def fwd(nq, nk, nv, bias):
        return call_fwd(nq, nk, nv, bias), (nq, nk, nv, bias)

    def bwd(res, do):
        return tuple(call_bwd(*res, do))

    op.defvjp(fwd, bwd)
    return op(nq, nk, nv, bias)


def ssd_dt(dt_raw, dt_bias, a_neg):
    def fn(r, b, a):
        dt = _softplus(r + b)
        return dt, dt * a

    return rowwise(fn, "ssd_dt", [dt_raw], [dt_bias, a_neg], [f32, f32])


def ssd_post(y, xs, z, d_skip_lanes, norm_g, groups):
    def fn(y, xs, z, dsk, g):
        y = (y + xs * dsk) * _silu(z)
        return (y * lax.rsqrt(jnp.mean(y * y, axis=-1, keepdims=True) + EPS) * g,)

    return rowwise(fn, "ssd_post", [y, xs, z], [d_skip_lanes.reshape(1, -1), norm_g.reshape(1, -1)], [MXU_DTYPE],
                   ncol=groups)[0]


def _heads_major(t, groups):
    S = t.shape[0]
    return jnp.transpose(t.reshape(S, groups, -1), (1, 0, 2))


def retention_na_mixer(hn, w_in, decay_logit, gn_g, rpb, w_out, tables):
    S = hn.shape[0]
    R = RET_HEADS * RET_DH
    NW = NA_HEADS * NA_DH
    cols = lambda a, b: w_in[:, a:b]
    rq, rk, rv, rg = (mm(hn, cols(j * R, (j + 1) * R)) for j in range(4))
    nq, nk, nv = (mm(hn, cols(4 * R + j * NW, 4 * R + (j + 1) * NW), out_dtype=MXU_DTYPE) for j in range(3))
    qr, kr = rotary(rq, rk, *tables)
    log_gamma = -_softplus(-decay_logit.astype(f32))
    pairs = R // LANES
    hp = LANES // RET_DH
    hpad = -(-RET_HEADS // 8) * 8
    pad8 = lambda a: jnp.pad(a.reshape(1, 1, RET_HEADS), ((0, 0), (0, 0), (0, hpad - RET_HEADS)))
    a_f = jnp.broadcast_to(pad8(log_gamma[0]), (1, S, hpad))
    a_b = jnp.broadcast_to(pad8(log_gamma[1]), (1, S, hpad))
    ret = ret_post(bidir_scan(qr, kr, rv, a_f, a_b, None, None, nsub=hp), rg, gn_g)
    rows = S // GRID_W
    nao = na_op(nq, nk, nv, _na_bias(rpb, min(NA_WIN_R, rows)))
    return mm(ret, w_out[:R]) + mm(nao, w_out[R:])


def ssd_mixer(hn, w_in, conv_w, conv_b, dt_bias, a_log, d_skip, norm_g, w_out):
    heads = d_skip.shape[0]
    inner = heads * SSD_HEADDIM
    gs = SSD_GROUPS * SSD_STATE
    o_x, o_b, o_c, o_dt = inner, 2 * inner, 2 * inner + gs, 2 * inner + 2 * gs
    z = mm(hn, w_in[:, :inner])
    dt_raw = mm(hn, w_in[:, o_dt:])
    xs, bm, cm = (mm_conv_act(hn, [w_in[:, a:b]], [conv_w[:, a - inner:b - inner]], [conv_b[a - inner:b - inner]],
                              _silu, _silu_bwd, f32, "conv_silu") for a, b in ((o_x, o_b), (o_b, o_c), (o_c, o_dt)))
    a_neg = -jnp.exp(a_log.astype(f32)).reshape(1, -1)
    dt, la = ssd_dt(dt_raw, dt_bias.astype(f32).reshape(1, -1), a_neg)
    dt_f, dt_b = _heads_major(dt[:, :heads], SSD_GROUPS), _heads_major(dt[:, heads:], SSD_GROUPS)
    la_f, la_b = _heads_major(la[:, :heads], SSD_GROUPS), _heads_major(la[:, heads:], SSD_GROUPS)
    y = bidir_scan(cm, bm, xs, la_f, la_b, dt_f, dt_b, nsub=1)
    y = ssd_post(y, xs, z, jnp.repeat(d_skip.astype(f32), SSD_HEADDIM), norm_g, SSD_GROUPS)
    return mm(y, w_out)


def conv_geglu_ffn(hf, w_up, conv_w, conv_b, w_down):
    F = w_down.shape[0]
    a = mm_conv_act(hf, [w_up[:, :F], w_up[:, F:]], [conv_w[:, :F], conv_w[:, F:]], [conv_b[:F], conv_b[F:]],
                    _geglu, _geglu_bwd, MXU_DTYPE, "conv_geglu")
    return mm(a, w_down)


def model_loss(x, tgt, big, small, rep):
    S = x.shape[0]
    depth = rep["norm_mix_pre"].shape[0]
    tables = _rope_tables(S, RET_HEADS * RET_DH)
    hn = rms(x, rep["norm_mix_pre"][0], MXU_DTYPE)
    for layer in range(depth):
        i = layer // 2
        if layer % 2 == 0:
            m = retention_na_mixer(hn, big["ab_w_in"][i], rep["ab_ret_decay_logit"][i], rep["ab_ret_gn_g"][i],
                                   rep["ab_na_rpb"][i], big["ab_w_out"][i], tables)
        else:
            m = ssd_mixer(hn, big["c_w_in"][i], small["c_conv_w"][i], small["c_conv_b"][i], rep["c_dt_bias"][i],
                          rep["c_a_log"][i], rep["c_d_skip"][i], small["c_norm_g"][i], big["c_w_out"][i])
        x, hf = rms_residual_norm(m, rep["norm_mix_post"][layer], x, rep["norm_ffn_pre"][layer])
        f = conv_geglu_ffn(hf, big["ffn_w_up"][layer], small["ffn_conv_w"][layer], rep["ffn_conv_b"][layer],
                           big["ffn_w_down"][layer])
        if layer + 1 < depth:
            x, hn = rms_residual_norm(f, rep["norm_ffn_post"][layer], x, rep["norm_mix_pre"][layer + 1])
        else:
            x = rms_residual(f, rep["norm_ffn_post"][layer], x)
    return loss_op(x, tgt)


def _mesh_pos():
    return lax.axis_index("x"), lax.axis_index("y"), lax.axis_index("c")


def _any_specs(n):
    return [pl.BlockSpec(memory_space=pl.ANY)] * n


def gather_chips(locals_):
    nbuf = len(locals_)
    CH = COPY_CHUNKS

    def body(*refs):
        x_refs, out_refs, (send_sems, recv_sems) = refs[:nbuf], refs[nbuf:2 * nbuf], refs[2 * nbuf:]
        x, y, c = _mesh_pos()
        my = 2 * x + y
        chips = [(1 - x, y), (x, 1 - y), (1 - x, 1 - y)]
        plans = []
        for a, (x_ref, out_ref) in enumerate(zip(x_refs, out_refs)):
            half = x_ref.shape[0] // 2
            q = half // CH

            def piece(ref, h, j, half=half, q=q):
                return ref.at[pl.ds(pl.multiple_of(h * half + j * q, PACK_ALIGN), q), :]

            def copy(k, src, chip, h, j, to, out_ref=out_ref, piece=piece, base=a * 6 * CH):
                return pltpu.make_async_remote_copy(src_ref=src, dst_ref=piece(out_ref.at[chip], h, j),
                                                    send_sem=send_sems.at[base + k], recv_sem=recv_sems.at[base + k],
                                                    device_id=to, device_id_type=pl.DeviceIdType.MESH)

            plans.append((x_ref, out_ref, piece, copy))

        first, passed = [], []
        for x_ref, out_ref, piece, copy in plans:
            first.append([[copy(k * CH + j, piece(x_ref, c, j), my, c, j, (cx, cy, c)) for j in range(CH)]
                          for k, (cx, cy) in enumerate(chips)])
            passed.append([[copy((3 + k) * CH + j, piece(out_ref.at[2 * cx + cy], c, j), 2 * cx + cy, c, j, (x, y, 1 - c))
                            for j in range(CH)] for k, (cx, cy) in enumerate(chips)])
        for a in range(nbuf):
            for j in range(CH):
                for k in range(3):
                    first[a][k][j].start()
        for a, (x_ref, _, piece, copy) in enumerate(plans):
            for j in range(CH):
                for k, (cx, cy) in enumerate(chips):
                    copy(k * CH + j, piece(x_ref, c, j), 2 * cx + cy, c, j, (cx, cy, c)).wait_recv()
                    passed[a][k][j].start()
        for a, (x_ref, _, piece, copy) in enumerate(plans):
            for j in range(CH):
                for k, (cx, cy) in enumerate(chips):
                    copy((3 + k) * CH + j, piece(x_ref, c, j), 2 * cx + cy, 1 - c, j, (x, y, 1 - c)).wait_recv()
        for a in range(nbuf):
            for k in range(3):
                for cp in first[a][k] + passed[a][k]:
                    cp.wait_send()

    return pl.pallas_call(
        body,
        out_shape=[jax.ShapeDtypeStruct((N_CHIPS,) + l.shape, l.dtype) for l in locals_],
        in_specs=_any_specs(nbuf),
        out_specs=_any_specs(nbuf),
        scratch_shapes=[pltpu.SemaphoreType.DMA((nbuf * 6 * CH,)), pltpu.SemaphoreType.DMA((nbuf * 6 * CH,))],
        name="gather_chips",
    )(*locals_)


def pair_swap(parts):
    nbuf = len(parts)
    n = N_CHIPS
    CH = COPY_CHUNKS

    def body(*refs):
        p_refs, got_refs, (send_sems, recv_sems) = refs[:nbuf], refs[nbuf:2 * nbuf], refs[2 * nbuf:]
        x, y, c = _mesh_pos()
        swap = []
        for a, (p_ref, got_ref) in enumerate(zip(p_refs, got_refs)):
            half = p_ref.shape[1] // 2
            q = half // CH
            for s in range(n):
                for j in range(CH):
                    k = (a * n + s) * CH + j
                    src = p_ref.at[s, pl.ds(pl.multiple_of((1 - c) * half + j * q, PACK_ALIGN), q), :]
                    swap.append(pltpu.make_async_remote_copy(src_ref=src, dst_ref=got_ref.at[s, pl.ds(j * q, q), :],
                                                             send_sem=send_sems.at[k], recv_sem=recv_sems.at[k],
                                                             device_id=(x, y, 1 - c), device_id_type=pl.DeviceIdType.MESH))
        for cp in swap:
            cp.start()
        for cp in swap:
            cp.wait()

    return pl.pallas_call(
        body,
        out_shape=[jax.ShapeDtypeStruct((n, p.shape[1] // 2, p.shape[2]), p.dtype) for p in parts],
        in_specs=_any_specs(nbuf),
        out_specs=_any_specs(nbuf),
        scratch_shapes=[pltpu.SemaphoreType.DMA((nbuf * n * CH,)), pltpu.SemaphoreType.DMA((nbuf * n * CH,))],
        name="pair_swap",
    )(*parts)


def chip_exchange(parts):
    nbuf = len(parts)

    def body(*refs):
        p_refs, out_refs, (send_sems, recv_sems) = refs[:nbuf], refs[nbuf:2 * nbuf], refs[2 * nbuf:]
        x, y, c = _mesh_pos()
        my = 2 * x + y
        chips = [(1 - x, y), (x, 1 - y), (1 - x, 1 - y)]

        def copy(a, k, src_slot, dst_slot, to):
            return pltpu.make_async_remote_copy(src_ref=p_refs[a].at[src_slot], dst_ref=out_refs[a].at[dst_slot],
                                                send_sem=send_sems.at[3 * a + k], recv_sem=recv_sems.at[3 * a + k],
                                                device_id=to, device_id_type=pl.DeviceIdType.MESH)

        sends = [copy(a, k, 2 * cx + cy, my, (cx, cy, c)) for a in range(nbuf) for k, (cx, cy) in enumerate(chips)]
        for cp in sends:
            cp.start()
        for a in range(nbuf):
            for k, (cx, cy) in enumerate(chips):
                copy(a, k, my, 2 * cx + cy, (cx, cy, c)).wait_recv()
        for cp in sends:
            cp.wait_send()

    return pl.pallas_call(
        body,
        out_shape=[jax.ShapeDtypeStruct(p.shape, p.dtype) for p in parts],
        in_specs=_any_specs(nbuf),
        out_specs=_any_specs(nbuf),
        scratch_shapes=[pltpu.SemaphoreType.DMA((3 * nbuf,)), pltpu.SemaphoreType.DMA((3 * nbuf,))],
        name="chip_exchange",
    )(*parts)


def pair_share(mine):
    nbuf = len(mine)
    CH = COPY_CHUNKS

    def body(*refs):
        m_refs, out_refs, (send_sems, recv_sems) = refs[:nbuf], refs[nbuf:2 * nbuf], refs[2 * nbuf:]
        x, y, c = _mesh_pos()
        swap = []
        for a, (m_ref, out_ref) in enumerate(zip(m_refs, out_refs)):
            q = m_ref.shape[0] // CH
            for j in range(CH):
                swap.append(pltpu.make_async_remote_copy(src_ref=m_ref.at[pl.ds(j * q, q), :], dst_ref=out_ref.at[pl.ds(j * q, q), :],
                                                         send_sem=send_sems.at[a * CH + j], recv_sem=recv_sems.at[a * CH + j],
                                                         device_id=(x, y, 1 - c), device_id_type=pl.DeviceIdType.MESH))
        for cp in swap:
            cp.start()
        for cp in swap:
            cp.wait()

    return pl.pallas_call(
        body,
        out_shape=[jax.ShapeDtypeStruct(m.shape, m.dtype) for m in mine],
        in_specs=_any_specs(nbuf),
        out_specs=_any_specs(nbuf),
        scratch_shapes=[pltpu.SemaphoreType.DMA((nbuf * CH,)), pltpu.SemaphoreType.DMA((nbuf * CH,))],
        name="pair_share",
    )(*mine)


def sum_chips(recv, own):
    n, R, Wd = recv.shape
    tr = _pick(R, (512, 256, 128, 64, 32, 16, 8))

    def body(r_ref, p_ref, o_ref):
        my = 2 * lax.axis_index("x") + lax.axis_index("y")
        acc = jnp.zeros((tr, Wd), f32)
        for s in range(n):
            acc = acc + jnp.where(my == s, p_ref[s], r_ref[s]).astype(f32)
        o_ref[...] = acc

    spec = pl.BlockSpec((n, tr, Wd), lambda i: (0, i, 0))
    return pl.pallas_call(
        body,
        out_shape=jax.ShapeDtypeStruct((R, Wd), f32),
        grid=(R // tr,),
        in_specs=[spec, spec],
        out_specs=pl.BlockSpec((tr, Wd), lambda i: (i, 0)),
        compiler_params=_cparams(("parallel",)),
        name="sum_chips",
    )(recv, own)


def add_pair(parts, got):
    n, R, Wd = parts.shape
    half = R // 2
    tr = _pick(half, (512, 256, 128, 64, 32, 16, 8))
    nb = half // tr

    def body(lo_ref, hi_ref, g_ref, o_ref):
        mine = jnp.where(lax.axis_index("c") == 0, lo_ref[...], hi_ref[...])
        o_ref[...] = (mine.astype(f32) + g_ref[...].astype(f32)).astype(o_ref.dtype)

    spec = pl.BlockSpec((1, tr, Wd), lambda s, i: (s, i, 0))
    return pl.pallas_call(
        body,
        out_shape=jax.ShapeDtypeStruct(got.shape, parts.dtype),
        grid=(n, nb),
        in_specs=[spec, pl.BlockSpec((1, tr, Wd), lambda s, i: (s, nb + i, 0)), spec],
        out_specs=spec,
        compiler_params=_cparams(("parallel", "parallel")),
        name="add_pair",
    )(parts, parts, got)


def reduce_scatter(parts):
    chip_sum = [add_pair(p, g) for p, g in zip(parts, pair_swap(parts))]
    mine = [sum_chips(r, s) for r, s in zip(chip_exchange(chip_sum), chip_sum)]
    first = lax.axis_index("c") == 0
    return [jnp.concatenate([jnp.where(first, m, t), jnp.where(first, t, m)], axis=0) for m, t in zip(mine, pair_share(mine))]


def adamw(w, g, m, v):
    shp = w.shape
    if w.size * 4 <= (1 << 20):
        grid, block, imap = (1,), shp, lambda i: (0,) * len(shp)
    else:
        n0, R, C = shp
        tr = _divisor_tile(R, lambda t: t * C * 4 <= (1 << 20), 8)
        grid, block, imap = (n0, R // tr), (1, tr, C), lambda j, i: (j, i, 0)

    def body(w_ref, g_ref, m_ref, v_ref, d_ref, mo_ref, vo_ref):
        g = g_ref[...]
        m = ADAM_B1 * m_ref[...] + (1.0 - ADAM_B1) * g
        v = ADAM_B2 * v_ref[...] + (1.0 - ADAM_B2) * (g * g)
        m_hat = m / (1.0 - ADAM_B1 ** ADAM_STEP)
        v_hat = v / (1.0 - ADAM_B2 ** ADAM_STEP)
        d_ref[...] = -ADAM_LR * (m_hat / (jnp.sqrt(v_hat) + ADAM_EPS) + ADAM_WD * w_ref[...])
        mo_ref[...] = m
        vo_ref[...] = v

    spec = pl.BlockSpec(block, imap)
    return pl.pallas_call(
        body,
        out_shape=[jax.ShapeDtypeStruct(shp, f32)] * 3,
        grid=grid,
        in_specs=[spec] * 4,
        out_specs=[spec] * 3,
        compiler_params=_cparams(("parallel",) * len(grid)),
        name="adamw",
    )(w, g, m, v)


def _pack(arrs, dtype):
    flat = jnp.concatenate([a.astype(dtype).reshape(-1) for a in arrs])
    n = flat.shape[0]
    unit = PACK_W * PACK_ROWS
    padded = -(-n // unit) * unit
    return jnp.pad(flat, (0, padded - n)).reshape(-1, PACK_W)


def _unpack(buf, shapes):
    flat = buf.reshape(-1)
    out, off = [], 0
    for s in shapes:
        n = int(np.prod(s))
        out.append(flat[off:off + n].reshape(s))
        off += n
    return out


BIG = (("ab_w_in", 2), ("ab_w_out", 1), ("c_w_in", 2), ("c_w_out", 1), ("ffn_w_up", 2), ("ffn_w_down", 1))
SMALL = (("c_conv_w", 2), ("c_conv_b", 1), ("c_norm_g", 1), ("ffn_conv_w", 2))
REP = ("norm_mix_pre", "norm_mix_post", "norm_ffn_pre", "norm_ffn_post", "ab_ret_decay_logit", "ab_ret_gn_g", "ab_na_rpb",
       "c_dt_bias", "c_a_log", "c_d_skip", "ffn_conv_b")
WEIGHTS = ("norm_mix_pre", "norm_mix_post", "norm_ffn_pre", "norm_ffn_post", "ab_w_in", "ab_ret_decay_logit", "ab_ret_gn_g",
           "ab_na_rpb", "ab_w_out", "c_w_in", "c_conv_w", "c_conv_b", "c_dt_bias", "c_a_log", "c_d_skip", "c_norm_g", "c_w_out",
           "ffn_w_up", "ffn_conv_w", "ffn_conv_b", "ffn_w_down")


BIG_GROUPS = (("ab_w_in",), ("ab_w_out", "c_w_out", "ffn_w_down"), ("c_w_in",), ("ffn_w_up",))
BIG_AXIS = dict(BIG)
ROW_UNIT = 128


def _rows(arrs, dtype):
    C = arrs[0].shape[-1]
    buf = jnp.concatenate([a.astype(dtype).reshape(-1, C) for a in arrs], axis=0)
    pad = -buf.shape[0] % ROW_UNIT
    return jnp.pad(buf, ((0, pad), (0, 0))) if pad else buf


def _unrows(buf, shapes):
    out, off = [], 0
    for s in shapes:
        n = int(np.prod(s[:-1]))
        out.append(buf[off:off + n].reshape(s))
        off += n
    return out


def _gather_all(w):
    bufs = [_rows([w[n] for n in grp], MXU_DTYPE) for grp in BIG_GROUPS] + [_pack([w[n] for n, _ in SMALL], f32)]
    got = gather_chips(bufs)
    my = 2 * lax.axis_index("x") + lax.axis_index("y")

    def whole(n, ax, dtype, pieces):
        return jnp.concatenate([jnp.where(my == s, w[n].astype(dtype), pieces[s]) for s in range(N_CHIPS)], axis=ax)

    big = {}
    for grp, g in zip(BIG_GROUPS, got):
        per_chip = [_unrows(g[s], [w[n].shape for n in grp]) for s in range(N_CHIPS)]
        for j, n in enumerate(grp):
            big[n] = whole(n, BIG_AXIS[n], MXU_DTYPE, [per_chip[s][j] for s in range(N_CHIPS)])
    per_chip = [_unpack(got[-1][s], [w[n].shape for n, _ in SMALL]) for s in range(N_CHIPS)]
    small = {n: whole(n, ax, f32, [per_chip[s][j] for s in range(N_CHIPS)]) for j, (n, ax) in enumerate(SMALL)}
    return big, small


def _reduce_all(gbig, gsmall, grep, w):
    split = {n: jnp.split(g, N_CHIPS, axis=BIG_AXIS[n]) for n, g in gbig.items()}
    parts = [jnp.stack([_rows([split[n][s] for n in grp], MXU_DTYPE) for s in range(N_CHIPS)]) for grp in BIG_GROUPS]
    ssplit = {n: jnp.split(gsmall[n], N_CHIPS, axis=ax) for n, ax in SMALL}
    parts.append(jnp.stack([_pack([ssplit[n][s] for n, _ in SMALL] + [grep[n] for n in REP], f32) for s in range(N_CHIPS)]))
    res = reduce_scatter(parts)
    grads = {}
    for grp, r in zip(BIG_GROUPS, res):
        grads.update(zip(grp, _unrows(r, [w[n].shape for n in grp])))
    small_names = [n for n, _ in SMALL] + list(REP)
    grads.update(zip(small_names, _unpack(res[-1], [w[n].shape for n in small_names])))
    return grads


def kernel(x, norm_mix_pre, norm_mix_post, norm_ffn_pre, norm_ffn_post, ab_w_in, ab_ret_decay_logit, ab_ret_gn_g, ab_na_rpb, ab_w_out, c_w_in, c_conv_w, c_conv_b, c_dt_bias, c_a_log, c_d_skip, c_norm_g, c_w_out, ffn_w_up, ffn_conv_w, ffn_conv_b, ffn_w_down, loss_target, m_norm_mix_pre, m_norm_mix_post, m_norm_ffn_pre, m_norm_ffn_post, m_ab_w_in, m_ab_ret_decay_logit, m_ab_ret_gn_g, m_ab_na_rpb, m_ab_w_out, m_c_w_in, m_c_conv_w, m_c_conv_b, m_c_dt_bias, m_c_a_log, m_c_d_skip, m_c_norm_g, m_c_w_out, m_ffn_w_up, m_ffn_conv_w, m_ffn_conv_b, m_ffn_w_down, v_norm_mix_pre, v_norm_mix_post, v_norm_ffn_pre, v_norm_ffn_post, v_ab_w_in, v_ab_ret_decay_logit, v_ab_ret_gn_g, v_ab_na_rpb, v_ab_w_out, v_c_w_in, v_c_conv_w, v_c_conv_b, v_c_dt_bias, v_c_a_log, v_c_d_skip, v_c_norm_g, v_c_w_out, v_ffn_w_up, v_ffn_conv_w, v_ffn_conv_b, v_ffn_w_down):
    args = dict(locals())
    w = {n: args[n] for n in WEIGHTS}
    mom = {n: args["m_" + n] for n in WEIGHTS}
    var = {n: args["v_" + n] for n in WEIGHTS}

    big, small = _gather_all(w)
    rep = {n: w[n] for n in REP}

    def loss_fn(xs, big, small, rep):
        return model_loss(xs, loss_target[0], big, small, rep)

    loss, (gx, gbig, gsmall, grep) = jax.value_and_grad(loss_fn, argnums=(0, 1, 2, 3))(x[0], big, small, rep)
    loss = lax.psum(loss, ("x", "y", "c"))

    grads = _reduce_all(gbig, gsmall, grep, w)

    delta, new_m, new_v = {}, {}, {}
    for n in WEIGHTS:
        delta[n], new_m[n], new_v[n] = adamw(w[n], grads[n], mom[n], var[n])

    return (loss, gx[None], *[grads[n] for n in WEIGHTS], *[delta[n] for n in WEIGHTS],
            *[new_m[n] for n in WEIGHTS], *[new_v[n] for n in WEIGHTS])
```

```python
import functools
import math

import numpy as np
import jax
import jax.numpy as jnp
from jax import lax
from jax.experimental import pallas as pl
from jax.experimental.pallas import tpu as pltpu

f32 = jnp.float32
bf16 = jnp.bfloat16
MXU_DTYPE = bf16

GRID_W = 64
CHUNK = 128
EPS = 1e-6
RET_HEADS = 8
RET_DH = 64
ROPE_BASE = 10000.0
NA_HEADS = 8
NA_DH = 64
NA_WIN_R = 8
NA_WIN_C = 16
NA_ROWS_PER_STEP = 8
SSD_HEADDIM = 64
SSD_GROUPS = 4
SSD_STATE = 128
ADAM_LR = 0.001
ADAM_B1 = 0.9
ADAM_B2 = 0.999
ADAM_EPS = 1e-08
ADAM_WD = 0.01
ADAM_STEP = 10

LANES = 128
HEAD_W = 64
PACK_W = 512
PACK_ROWS = 1024
PACK_ALIGN = 16
COPY_CHUNKS = 2
VMEM_LIMIT = 56 * 1024 * 1024
MM_BLOCK_BYTES = 6 * 1024 * 1024
N_CHIPS = 4
N_DEV = 8
NEG_INF = -1e30

_DIMS = {"nn": (((1,), (0,)), ((), ())), "nt": (((1,), (1,)), ((), ())), "tn": (((0,), (0,)), ((), ()))}


def _cparams(sem=None):
    return pltpu.CompilerParams(dimension_semantics=sem, vmem_limit_bytes=VMEM_LIMIT)


def _pick(dim, cands):
    for c in cands:
        if dim % c == 0:
            return c
    return dim


def _divisor_tile(dim, fits, align):
    for d in range(1, dim + 1):
        t = dim // d
        if dim % d == 0 and t % align == 0 and fits(t):
            return t
    return dim


def _bdot_raw(a, b, mode):
    return lax.dot_general(a.astype(MXU_DTYPE), b.astype(MXU_DTYPE), _DIMS[mode], preferred_element_type=f32)


@functools.partial(jax.custom_vjp, nondiff_argnums=(2,))
def bdot(a, b, mode):
    return _bdot_raw(a, b, mode)


def _bdot_fwd(a, b, mode):
    return _bdot_raw(a, b, mode), (a, b)


def _bdot_bwd(mode, res, g):
    a, b = res
    if mode == "nn":
        da, db = _bdot_raw(g, b, "nt"), _bdot_raw(a, g, "tn")
    elif mode == "nt":
        da, db = _bdot_raw(g, b, "nn"), _bdot_raw(g, a, "tn")
    else:
        da, db = _bdot_raw(b, g, "nt"), _bdot_raw(a, g, "nn")
    return da.astype(a.dtype), db.astype(b.dtype)


bdot.defvjp(_bdot_fwd, _bdot_bwd)


def _mm_call(a, b, mode, out_dtype):
    if mode == "nn":
        (M, K), (K2, N) = a.shape, b.shape
    elif mode == "nt":
        (M, K), (N, K2) = a.shape, b.shape
    else:
        (K, M), (K2, N) = a.shape, b.shape
    assert K == K2, (a.shape, b.shape, mode)
    a_bytes, b_bytes, o_bytes = a.dtype.itemsize, b.dtype.itemsize, jnp.dtype(out_dtype).itemsize
    if mode == "tn":
        tn = _divisor_tile(N, lambda t: t <= 1536, LANES)
        tm = _divisor_tile(M, lambda t: t * tn * 4 <= MM_BLOCK_BYTES, 8)
        tk = _divisor_tile(K, lambda t: t * tm * a_bytes <= MM_BLOCK_BYTES and t * tn * b_bytes <= MM_BLOCK_BYTES, LANES)
    else:
        tk, tn = K, N
        tm = _divisor_tile(M, lambda t: t * K * a_bytes <= MM_BLOCK_BYTES and t * N * o_bytes <= MM_BLOCK_BYTES, 8)
    nk = K // tk
    if mode == "nn":
        a_spec = pl.BlockSpec((tm, tk), lambda i, j, k: (i, k))
        b_spec = pl.BlockSpec((tk, tn), lambda i, j, k: (k, j))
    elif mode == "nt":
        a_spec = pl.BlockSpec((tm, tk), lambda i, j, k: (i, k))
        b_spec = pl.BlockSpec((tn, tk), lambda i, j, k: (j, k))
    else:
        a_spec = pl.BlockSpec((tk, tm), lambda i, j, k: (k, i))
        b_spec = pl.BlockSpec((tk, tn), lambda i, j, k: (k, j))

    if nk == 1:
        def body(a_ref, b_ref, o_ref):
            o_ref[...] = _bdot_raw(a_ref[...], b_ref[...], mode).astype(o_ref.dtype)
    else:
        def body(a_ref, b_ref, o_ref, acc_ref):
            k = pl.program_id(2)

            @pl.when(k == 0)
            def _():
                acc_ref[...] = jnp.zeros_like(acc_ref)

            acc_ref[...] += _bdot_raw(a_ref[...], b_ref[...], mode)

            @pl.when(k == nk - 1)
            def _():
                o_ref[...] = acc_ref[...].astype(o_ref.dtype)

    return pl.pallas_call(
        body,
        out_shape=jax.ShapeDtypeStruct((M, N), out_dtype),
        grid=(M // tm, N // tn, nk),
        in_specs=[a_spec, b_spec],
        out_specs=pl.BlockSpec((tm, tn), lambda i, j, k: (i, j)),
        scratch_shapes=[pltpu.VMEM((tm, tn), f32)] if nk > 1 else [],
        compiler_params=_cparams(("parallel", "parallel", "arbitrary")),
        name="mm_" + mode,
    )(a, b)


def mm(a, b, mode="nn", out_dtype=f32):
    @jax.custom_vjp
    def op(a, b):
        return _mm_call(a, b, mode, out_dtype)

    def fwd(a, b):
        return _mm_call(a, b, mode, out_dtype), (a, b)

    def bwd(res, g):
        a, b = res
        if mode == "nn":
            return _mm_call(g, b, "nt", a.dtype), _mm_call(a, g, "tn", b.dtype)
        if mode == "nt":
            return _mm_call(g, b, "nn", a.dtype), _mm_call(g, a, "tn", b.dtype)
        return _mm_call(b, g, "nt", a.dtype), _mm_call(a, g, "nn", b.dtype)

    op.defvjp(fwd, bwd)
    return op(a, b)


def _row_tile(S, row_bytes):
    tm = 512
    while tm > 8 and (tm * row_bytes > (6 << 20) or S % tm):
        tm //= 2
    return tm


def rowwise(fn, name, rows, params, out_dtypes, n_diff_rows=None, n_diff_params=None, ncol=1, bwd_fn=None):
    rows, params = list(rows), list(params)
    nr, npar = len(rows), len(params)
    ndr = nr if n_diff_rows is None else n_diff_rows
    ndp = npar if n_diff_params is None else n_diff_params
    S = rows[0].shape[0]
    rw = [r.shape[1] // ncol for r in rows]
    pshape = [(p.shape[0], p.shape[1] // ncol) for p in params]

    def block_structs(tm):
        return ([jax.ShapeDtypeStruct((tm, w), f32) for w in rw] + [jax.ShapeDtypeStruct(s, f32) for s in pshape])

    outs_s = jax.eval_shape(fn, *block_structs(8))
    ow = [o.shape[1] for o in outs_s]
    nout = len(ow)
    row_bytes = 4 * (sum(rw) * 2 + sum(ow) * 2)
    tm = _row_tile(S, row_bytes)
    grid = (ncol, S // tm)

    def rspec(w):
        return pl.BlockSpec((tm, w), lambda g, i: (i, g))

    def pspec(s):
        return pl.BlockSpec(s, lambda g, i: (0, g))

    def call_fwd(*args):
        def body(*refs):
            vals = [r[...].astype(f32) for r in refs[:nr + npar]]
            res = fn(*vals)
            for o, r in zip(refs[nr + npar:], res):
                o[...] = r.astype(o.dtype)

        return pl.pallas_call(
            body,
            out_shape=[jax.ShapeDtypeStruct((S, w * ncol), dt) for w, dt in zip(ow, out_dtypes)],
            grid=grid,
            in_specs=[rspec(w) for w in rw] + [pspec(s) for s in pshape],
            out_specs=[rspec(w) for w in ow],
            compiler_params=_cparams(("parallel", "parallel")),
            name=name + "_fwd",
        )(*args)

    def call_bwd(args, douts):
        def body(*refs):
            in_refs = refs[:nr + npar]
            do_refs = refs[nr + npar:nr + npar + nout]
            dr_refs = refs[nr + npar + nout:nr + npar + nout + ndr]
            dp_refs = refs[nr + npar + nout + ndr:]
            rv = [r[...] for r in in_refs[:nr]]
            pv = [r[...] for r in in_refs[nr:]]
            dos = [d[...].astype(f32) for d in do_refs]
            if bwd_fn is not None:
                drs, dps = bwd_fn(rv, pv, dos)
            else:
                def f(*a):
                    return fn(*a[:ndr], *rv[ndr:], *a[ndr:], *pv[ndp:])

                _, vjp = jax.vjp(f, *[v.astype(f32) for v in rv[:ndr]], *pv[:ndp])
                cts = vjp(tuple(dos))
                drs, dps = cts[:ndr], cts[ndr:]
            for r, ct in zip(dr_refs, drs):
                r[...] = ct.astype(r.dtype)
            if ndp:
                @pl.when(pl.program_id(1) == 0)
                def _():
                    for r in dp_refs:
                        r[...] = jnp.zeros_like(r)

                for r, ct in zip(dp_refs, dps):
                    r[...] += ct

        return pl.pallas_call(
            body,
            out_shape=[jax.ShapeDtypeStruct(r.shape, r.dtype) for r in rows[:ndr]]
            + [jax.ShapeDtypeStruct(p.shape, f32) for p in params[:ndp]],
            grid=grid,
            in_specs=[rspec(w) for w in rw] + [pspec(s) for s in pshape] + [rspec(w) for w in ow],
            out_specs=[rspec(w) for w in rw[:ndr]] + [pspec(s) for s in pshape[:ndp]],
            compiler_params=_cparams(("parallel", "arbitrary")),
            name=name + "_bwd",
        )(*args, *douts)

    @jax.custom_vjp
    def op(*args):
        return tuple(call_fwd(*args))

    def fwd(*args):
        return tuple(call_fwd(*args)), args

    def bwd(args, douts):
        res = call_bwd(args, douts)
        drs, dps = res[:ndr], res[ndr:]
        out = list(drs) + [jnp.zeros_like(a) for a in args[ndr:nr]]
        out += [dp.astype(p.dtype) for dp, p in zip(dps, args[nr:nr + ndp])]
        out += [jnp.zeros_like(a) for a in args[nr + ndp:]]
        return tuple(out)

    op.defvjp(fwd, bwd)
    return op(*rows, *params)


def _silu(x):
    return x * (1.0 / (1.0 + jnp.exp(-x)))


def _softplus(x):
    return jnp.maximum(x, 0.0) + jnp.log(1.0 + jnp.exp(-jnp.abs(x)))


def _gelu_tanh(x):
    return 0.5 * x * (1.0 + jnp.tanh(math.sqrt(2.0 / math.pi) * (x + 0.044715 * (x * x * x))))


def _rms_fn(x, g):
    return x * lax.rsqrt(jnp.mean(x * x, axis=-1, keepdims=True) + EPS) * g


def _rms_bwd(x, g, dy):
    r = lax.rsqrt(jnp.mean(x * x, axis=-1, keepdims=True) + EPS)
    xh = x * r
    dxh = dy * g
    dx = r * (dxh - xh * jnp.mean(dxh * xh, axis=-1, keepdims=True))
    return dx, jnp.sum(dy * xh, axis=0, keepdims=True)


def rms(x, g, out_dtype):
    def bwd_fn(rv, pv, dos):
        dx, dg = _rms_bwd(rv[0], pv[0], dos[0])
        return (dx,), (dg,)

    return rowwise(lambda x, g: (_rms_fn(x, g),), "rms", [x], [g.reshape(1, -1)], [out_dtype], bwd_fn=bwd_fn)[0]


def rms_residual_norm(m, g, x, g_next):
    def fn(m, x, g, gn):
        xn = x + _rms_fn(m, g)
        return xn, _rms_fn(xn, gn)

    def bwd_fn(rv, pv, dos):
        (m, x), (g, gn), (dxn, dhn) = rv, pv, dos
        xn = x + _rms_fn(m, g)
        d_from_norm, dgn = _rms_bwd(xn, gn, dhn)
        dxn = dxn + d_from_norm
        dm, dg = _rms_bwd(m, g, dxn)
        return (dm, dxn), (dg, dgn)

    return rowwise(fn, "rms_res_norm", [m, x], [g.reshape(1, -1), g_next.reshape(1, -1)], [f32, MXU_DTYPE], bwd_fn=bwd_fn)


def rms_residual(m, g, x):
    def bwd_fn(rv, pv, dos):
        dm, dg = _rms_bwd(rv[0], pv[0], dos[0])
        return (dm, dos[0]), (dg,)

    return rowwise(lambda m, x, g: (x + _rms_fn(m, g),), "rms_res", [m, x], [g.reshape(1, -1)], [f32], bwd_fn=bwd_fn)[0]


def loss_op(y, tgt):
    S, D = y.shape
    tm = _row_tile(S, 4 * D * 4)

    def call_fwd(y, tgt):
        def body(y_ref, t_ref, o_ref):
            @pl.when(pl.program_id(0) == 0)
            def _():
                o_ref[...] = jnp.zeros_like(o_ref)

            e = y_ref[...] - t_ref[...]
            o_ref[...] += 0.5 * jnp.sum(jnp.mean(e * e, axis=-1, keepdims=True))

        out = pl.pallas_call(
            body,
            out_shape=jax.ShapeDtypeStruct((8, LANES), f32),
            grid=(S // tm,),
            in_specs=[pl.BlockSpec((tm, D), lambda i: (i, 0))] * 2,
            out_specs=pl.BlockSpec((8, LANES), lambda i: (0, 0)),
            compiler_params=_cparams(("arbitrary",)),
            name="loss_fwd",
        )(y, tgt)
        return out[0, 0]

    def call_bwd(y, tgt, g):
        def body(y_ref, t_ref, g_ref, o_ref):
            o_ref[...] = (y_ref[...] - t_ref[...]) * (g_ref[...] * (1.0 / D))

        return pl.pallas_call(
            body,
            out_shape=jax.ShapeDtypeStruct((S, D), f32),
            grid=(S // tm,),
            in_specs=[pl.BlockSpec((tm, D), lambda i: (i, 0))] * 2 + [pl.BlockSpec((1, 1), lambda i: (0, 0))],
            out_specs=pl.BlockSpec((tm, D), lambda i: (i, 0)),
            compiler_params=_cparams(("parallel",)),
            name="loss_bwd",
        )(y, tgt, g.reshape(1, 1).astype(f32))

    @jax.custom_vjp
    def op(y, tgt):
        return call_fwd(y, tgt)

    def fwd(y, tgt):
        return call_fwd(y, tgt), (y, tgt)

    def bwd(res, g):
        y, tgt = res
        return call_bwd(y, tgt, g), jnp.zeros_like(tgt)

    op.defvjp(fwd, bwd)
    return op(y, tgt)


HALO = 8


def _conv_tile(S, R):
    def ext(ref, r0):
        cur = ref[pl.ds(r0, R), :]
        prev = ref[pl.ds(pl.multiple_of(jnp.maximum(r0 - HALO, 0), HALO), HALO), :]
        nxt = ref[pl.ds(pl.multiple_of(jnp.minimum(r0 + R, S - HALO), HALO), HALO), :]
        prev = jnp.where(r0 > 0, prev, 0.0)
        nxt = jnp.where(r0 + R < S, nxt, 0.0)
        return jnp.concatenate([prev, cur, nxt], axis=0)

    return ext


def _shift_rows(e, k, R):
    n = e.shape[0]
    if k == 0:
        return e[HALO:HALO + R]
    return pltpu.roll(e, (-k) % n, 0)[HALO:HALO + R]


def _silu_bwd(us, dy):
    u, = us
    s = 1.0 / (1.0 + jnp.exp(-u))
    return (dy * (s * (1.0 + u * (1.0 - s))),)


def _geglu(g, v):
    return _gelu_tanh(g) * v


def _geglu_bwd(us, dy):
    g, v = us
    c = math.sqrt(2.0 / math.pi)
    t = jnp.tanh(c * (g + 0.044715 * (g * g * g)))
    half = 0.5 * (1.0 + t)
    dgelu = half + 0.5 * g * (1.0 - t * t) * (c * (1.0 + 3.0 * 0.044715 * (g * g)))
    return dy * v * dgelu, dy * (g * half)


def mm_conv_act(h, ws, cws, cbs, act, act_bwd, out_dtype, name):
    n = len(ws)
    S = h.shape[0]
    C = ws[0].shape[1]
    W = cws[0].shape[0]
    pad = W // 2
    bw = _pick(C, (LANES,))
    R = _pick(S, (256, 128, 64, 32, 16, 8))
    nt = S // R
    ext = _conv_tile(S, R)
    col = lambda rows: pl.BlockSpec((rows, bw), lambda j: (0, j))

    def conv(e, wv, bv):
        acc = bv + wv[pad] * e[HALO:HALO + R]
        for j in range(W):
            if j != pad:
                acc = acc + wv[j] * _shift_rows(e, j - pad, R)
        return acc

    def call_fwd(xs, cws, cbs):
        def body(*refs):
            x_refs, w_refs, b_refs, y_ref = refs[:n], refs[n:2 * n], refs[2 * n:3 * n], refs[3 * n]
            wvs = [[w[j:j + 1, :] for j in range(W)] for w in w_refs]
            bvs = [b[...] for b in b_refs]

            def tile(i, c):
                r0 = pl.multiple_of(i * R, R)
                us = [conv(ext(x, r0), wv, bv) for x, wv, bv in zip(x_refs, wvs, bvs)]
                y_ref[pl.ds(r0, R), :] = act(*us).astype(y_ref.dtype)
                return c

            lax.fori_loop(0, nt, tile, 0)

        return pl.pallas_call(
            body,
            out_shape=jax.ShapeDtypeStruct((S, C), out_dtype),
            grid=(C // bw,),
            in_specs=[col(S)] * n + [col(W)] * n + [col(1)] * n,
            out_specs=col(S),
            compiler_params=_cparams(("parallel",)),
            name=name + "_fwd",
        )(*xs, *cws, *cbs)

    def call_bwd(xs, cws, cbs, dy):
        def body(*refs):
            x_refs, w_refs, b_refs, dy_ref = refs[:n], refs[n:2 * n], refs[2 * n:3 * n], refs[3 * n]
            dx_refs, dw_refs, db_refs = refs[3 * n + 1:4 * n + 1], refs[4 * n + 1:5 * n + 1], refs[5 * n + 1:6 * n + 1]
            du_scr = refs[6 * n + 1:]
            wvs = [[w[j:j + 1, :] for j in range(W)] for w in w_refs]
            bvs = [b[...] for b in b_refs]
            zero = jnp.zeros((1, bw), f32)

            def first(i, dbs):
                r0 = pl.multiple_of(i * R, R)
                us = [conv(ext(x, r0), wv, bv) for x, wv, bv in zip(x_refs, wvs, bvs)]
                dus = act_bwd(us, dy_ref[pl.ds(r0, R), :].astype(f32))
                for scr, du in zip(du_scr, dus):
                    scr[pl.ds(r0, R), :] = du
                return tuple(db + jnp.sum(du, axis=0, keepdims=True) for db, du in zip(dbs, dus))

            dbs = lax.fori_loop(0, nt, first, tuple(zero for _ in range(n)))

            def second(i, dws):
                r0 = pl.multiple_of(i * R, R)
                new = []
                for x, scr, dx, wv, dw in zip(x_refs, du_scr, dx_refs, wvs, dws):
                    ex, ed = ext(x, r0), ext(scr, r0)
                    d0 = ed[HALO:HALO + R]
                    acc = jnp.zeros((R, bw), f32)
                    row = []
                    for j in range(W):
                        acc = acc + wv[j] * _shift_rows(ed, pad - j, R)
                        row.append(dw[j] + jnp.sum(d0 * _shift_rows(ex, j - pad, R), axis=0, keepdims=True))
                    dx[pl.ds(r0, R), :] = acc.astype(dx.dtype)
                    new.append(tuple(row))
                return tuple(new)

            dws = lax.fori_loop(0, nt, second, tuple(tuple(zero for _ in range(W)) for _ in range(n)))
            for dw_ref, db_ref, dw, db in zip(dw_refs, db_refs, dws, dbs):
                dw_ref[...] = jnp.zeros_like(dw_ref)
                for j in range(W):
                    dw_ref[j:j + 1, :] = dw[j]
                db_ref[...] = db

        return pl.pallas_call(
            body,
            out_shape=[jax.ShapeDtypeStruct((S, C), MXU_DTYPE)] * n + [jax.ShapeDtypeStruct((8, C), f32)] * n
            + [jax.ShapeDtypeStruct((1, C), f32)] * n,
            grid=(C // bw,),
            in_specs=[col(S)] * n + [col(W)] * n + [col(1)] * n + [col(S)],
            out_specs=[col(S)] * n + [col(8)] * n + [col(1)] * n,
            scratch_shapes=[pltpu.VMEM((S, bw), f32)] * n,
            compiler_params=_cparams(("parallel",)),
            name=name + "_bwd",
        )(*xs, *cws, *cbs, dy)

    @jax.custom_vjp
    def op(h, ws, cws, cbs):
        return call_fwd([_mm_call(h, w, "nn", f32) for w in ws], cws, cbs)

    def fwd(h, ws, cws, cbs):
        xs = [_mm_call(h, w, "nn", f32) for w in ws]
        return call_fwd(xs, cws, cbs), (h, ws, xs, cws, cbs)

    def bwd(res, dy):
        h, ws, xs, cws, cbs = res
        out = call_bwd(xs, cws, cbs, dy)
        dxs, dcws, dcbs = out[:n], out[n:2 * n], out[2 * n:]
        dh = _mm_call(dxs[0], ws[0], "nt", h.dtype)
        for dx, w in zip(dxs[1:], ws[1:]):
            dh = dh + _mm_call(dx, w, "nt", h.dtype)
        dws = tuple(_mm_call(h, dx, "tn", w.dtype) for dx, w in zip(dxs, ws))
        return dh, dws, tuple(d[:W] for d in dcws), tuple(dcbs)

    op.defvjp(fwd, bwd)
    return op(h, tuple(ws), tuple(cws), tuple(b.reshape(1, C) for b in cbs))


@jax.custom_vjp
def _masked_decay(cs_col, cs_row, mask01):
    return jnp.where(mask01 > 0, jnp.exp(cs_col - cs_row), 0.0)


def _masked_decay_fwd(cs_col, cs_row, mask01):
    d = jnp.where(mask01 > 0, jnp.exp(cs_col - cs_row), 0.0)
    return d, (d, mask01)


def _masked_decay_bwd(res, g):
    d, mask01 = res
    t = g * d
    return jnp.sum(t, axis=1, keepdims=True), -jnp.sum(t, axis=0, keepdims=True), jnp.zeros_like(mask01)


_masked_decay.defvjp(_masked_decay_fwd, _masked_decay_bwd)


def _scan_chunk(qs, ks, xs, cs_tok, dt_tok, hs, *, rev, incl, nsub):
    nb = len(xs)
    L, N = qs[0].shape
    W = xs[0].shape[1]
    Hg = cs_tok.shape[1]
    nh = W // HEAD_W
    shared = len(qs) == 1
    t = lax.broadcasted_iota(jnp.int32, (L, L), 0)
    l = lax.broadcasted_iota(jnp.int32, (L, L), 1)
    if rev:
        mask = (l >= t) if incl else (l > t)
    else:
        mask = (l <= t) if incl else (l < t)
    mask01 = mask.astype(f32)
    lane_a = lax.broadcasted_iota(jnp.int32, cs_tok.shape, 1)
    row_a = lax.broadcasted_iota(jnp.int32, (Hg, L), 0)
    last = lax.broadcasted_iota(jnp.int32, (1, L), 1) == (0 if rev else L - 1)
    vhead = lax.broadcasted_iota(jnp.int32, (1, W), 1) // HEAD_W
    qhead = lax.broadcasted_iota(jnp.int32, (1, N), 1) // (N // nsub)
    cs_rows = lax.dot_general(cs_tok, (t == l).astype(f32), _DIMS["tn"], precision=lax.Precision.HIGHEST,
                              preferred_element_type=f32)

    def by_head(vals):
        if len(vals) == 2:
            return jnp.where(vhead == 0, vals[0], vals[1])
        return sum(jnp.where(vhead == i, v, 0.0) for i, v in enumerate(vals))

    decay, lam_e, tau_e, gam_e, dt_e = [], [], [], [], []
    for b in range(nb):
        cs_cols, tots, dt_cols = [], [], []
        for i in range(nh):
            head = b * nh + i
            cs_col = jnp.sum(jnp.where(lane_a == head, cs_tok, 0.0), axis=1, keepdims=True)
            cs_row = jnp.sum(jnp.where(row_a == head, cs_rows, 0.0), axis=0, keepdims=True)
            tots.append(jnp.sum(jnp.where(last, cs_row, 0.0), axis=1, keepdims=True))
            decay.append(_masked_decay(cs_col, cs_row, mask01))
            cs_cols.append(cs_col)
            if dt_tok is not None:
                dt_cols.append(jnp.sum(jnp.where(lane_a == head, dt_tok, 0.0), axis=1, keepdims=True))
        cs_e, tot_e = by_head(cs_cols), by_head(tots)
        lam_e.append(jnp.exp(cs_e))
        tau_e.append(jnp.exp(tot_e - cs_e))
        gam_e.append(jnp.exp(tot_e))
        if dt_tok is not None:
            dt_e.append(by_head(dt_cols))
    vs = [x if dt_tok is None else x * dt_e[b] for b, x in enumerate(xs)]
    qk = lambda b: (qs[0], ks[0]) if shared else (qs[b], ks[b])
    if nsub == 1:
        scores = [bdot(qs[0], ks[0], "nt")] if shared else [bdot(*qk(b), "nt") for b in range(nb)]
        score = lambda b, i: scores[0 if shared else b]
    else:
        scores = [[bdot(jnp.where(qhead == i, qk(b)[0], 0.0), qk(b)[1], "nt") for i in range(nh)] for b in range(nb)]
        score = lambda b, i: scores[b][i]
    ys = [lam_e[b] * bdot(qk(b)[0], hs[b], "nn")
          + by_head([bdot(score(b, i) * decay[b * nh + i], vs[b], "nn") for i in range(nh)]) for b in range(nb)]
    hns = [gam_e[b] * hs[b] + bdot(qk(b)[1], tau_e[b] * vs[b], "tn") for b in range(nb)]
    if nsub > 1:
        nhead = lax.broadcasted_iota(jnp.int32, (N, W), 0) // (N // nsub)
        keep = nhead == lax.broadcasted_iota(jnp.int32, (N, W), 1) // HEAD_W
        hns = [jnp.where(keep, hn, 0.0) for hn in hns]
    return ys, hns


def chunk_cumsum(a_tok, rev):
    G, S, Hg = a_tok.shape
    L = CHUNK
    CB = _pick(S // L, (16, 8, 4, 2))

    def call(a, rev):
        def body(a_ref, o_ref):
            t = lax.broadcasted_iota(jnp.int32, (L, L), 0)
            l = lax.broadcasted_iota(jnp.int32, (L, L), 1)
            tri = ((l >= t) if rev else (l <= t)).astype(f32)
            for j in range(CB):
                o_ref[0, j * L:(j + 1) * L, :] = _exact_dot(tri, a_ref[0, j * L:(j + 1) * L, :])

        spec = pl.BlockSpec((1, CB * L, Hg), lambda g, c: (g, c, 0))
        return pl.pallas_call(
            body,
            out_shape=jax.ShapeDtypeStruct((G, S, Hg), f32),
            grid=(G, S // (CB * L)),
            in_specs=[spec],
            out_specs=spec,
            compiler_params=_cparams(("parallel", "parallel")),
            name="chunk_cumsum",
        )(a)

    @jax.custom_vjp
    def op(a):
        return call(a, rev)

    def fwd(a):
        return call(a, rev), None

    def bwd(_, g):
        return (call(g, not rev),)

    op.defvjp(fwd, bwd)
    return op(a_tok)


def scan_op(q, k, x, a_tok, dt_tok, *, rev, incl, nsub):
    S = q.shape[0]
    G, _, Hg = a_tok.shape
    N = q.shape[1] // G
    Vw = x.shape[1] // G
    L = CHUNK
    nc = S // L
    use_dt = dt_tok is not None
    PW = min(Vw, LANES)
    chunk = functools.partial(_scan_chunk, rev=rev, incl=incl, nsub=nsub)
    cols = [slice(p * PW, (p + 1) * PW) for p in range(Vw // PW)]
    own_qk = nsub > 1
    NB = PW if own_qk else N

    def order(c, backward):
        return (nc - 1 - c) if (rev != backward) else c

    def specs(backward):
        qs = pl.BlockSpec((L, N), lambda g, c: (order(c, backward), g))
        xs = pl.BlockSpec((L, Vw), lambda g, c: (order(c, backward), g))
        as_ = pl.BlockSpec((1, L, Hg), lambda g, c: (g, order(c, backward), 0))
        hs = pl.BlockSpec((1, 1, NB, Vw), lambda g, c: (g, order(c, backward), 0, 0))
        return qs, xs, as_, hs

    def call_fwd(q, k, x, a_tok, dt_tok, y_prev=None):
        qs, xs, as_, hs = specs(False)
        n_in = 4 + use_dt + (y_prev is not None)

        def body(*refs):
            q_ref, k_ref, x_ref, a_ref = refs[:4]
            dt_ref = refs[4] if use_dt else None
            yp_ref = refs[n_in - 1] if y_prev is not None else None
            y_ref, hs_ref, h_scr = refs[n_in:]

            @pl.when(pl.program_id(1) == 0)
            def _():
                h_scr[...] = jnp.zeros_like(h_scr)

            hs_ref[0, 0] = h_scr[...]
            q, k, a, dt = q_ref[...], k_ref[...], a_ref[0], dt_ref[0] if use_dt else None
            qs, ks = ([q[:, c] for c in cols], [k[:, c] for c in cols]) if own_qk else ([q], [k])
            ys, hns = chunk(qs, ks, [x_ref[:, c] for c in cols], a, dt, [h_scr[:, c] for c in cols])
            for c, y, hn in zip(cols, ys, hns):
                y_ref[:, c] = y if yp_ref is None else y + yp_ref[:, c]
                h_scr[:, c] = hn

        ins = [q, k, x, a_tok] + ([dt_tok] if use_dt else []) + ([y_prev] if y_prev is not None else [])
        return pl.pallas_call(
            body,
            out_shape=[jax.ShapeDtypeStruct((S, G * Vw), f32), jax.ShapeDtypeStruct((G, nc, NB, Vw), f32)],
            grid=(G, nc),
            in_specs=[qs, qs, xs, as_] + ([as_] if use_dt else []) + ([xs] if y_prev is not None else []),
            out_specs=[xs, hs],
            scratch_shapes=[pltpu.VMEM((NB, Vw), f32)],
            compiler_params=_cparams(("parallel", "arbitrary")),
            name="scan_fwd",
        )(*ins)

    def call_bwd(q, k, x, a_tok, dt_tok, hsave, dy, acc=None):
        qs, xs, as_, hs = specs(True)
        n_in = 6 + use_dt + (3 if acc is not None else 0)

        def body(*refs):
            q_ref, k_ref, x_ref, a_ref = refs[:4]
            dt_ref = refs[4] if use_dt else None
            hs_ref, dy_ref = refs[4 + use_dt], refs[5 + use_dt]
            acc_refs = refs[n_in - 3:n_in] if acc is not None else None
            dq_ref, dk_ref, dx_ref, da_ref = refs[n_in:n_in + 4]
            ddt_ref = refs[n_in + 4] if use_dt else None
            dh_scr = refs[-1]

            @pl.when(pl.program_id(1) == 0)
            def _():
                dh_scr[...] = jnp.zeros_like(dh_scr)

            q, k, a = q_ref[...].astype(f32), k_ref[...].astype(f32), a_ref[0]
            qs, ks = ([q[:, c] for c in cols], [k[:, c] for c in cols]) if own_qk else ([q], [k])
            xs, hs_in = [x_ref[:, c] for c in cols], [hs_ref[0, 0, :, c] for c in cols]
            if use_dt:
                _, vjp = jax.vjp(chunk, qs, ks, xs, a, dt_ref[0], hs_in)
            else:
                _, vjp = jax.vjp(lambda qs, ks, xs, a, hs: chunk(qs, ks, xs, a, None, hs), qs, ks, xs, a, hs_in)
            cts = vjp(([dy_ref[:, c] for c in cols], [dh_scr[:, c] for c in cols]))
            dqs, dks, dxs, da, dhs = cts[0], cts[1], cts[2], cts[3], cts[-1]
            if acc is not None:
                dq_acc, dk_acc = acc_refs[0][...].astype(f32), acc_refs[1][...].astype(f32)
            if own_qk:
                for b, c in enumerate(cols):
                    dq_ref[:, c] = (dqs[b] if acc is None else dqs[b] + dq_acc[:, c]).astype(dq_ref.dtype)
                    dk_ref[:, c] = (dks[b] if acc is None else dks[b] + dk_acc[:, c]).astype(dk_ref.dtype)
            else:
                dq_ref[...] = (dqs[0] if acc is None else dqs[0] + dq_acc).astype(dq_ref.dtype)
                dk_ref[...] = (dks[0] if acc is None else dks[0] + dk_acc).astype(dk_ref.dtype)
            for b, c in enumerate(cols):
                dx_ref[:, c] = dxs[b] if acc is None else dxs[b] + acc_refs[2][:, c]
                dh_scr[:, c] = dhs[b]
            da_ref[0] = da
            if use_dt:
                ddt_ref[0] = cts[4]

        ins = [q, k, x, a_tok] + ([dt_tok] if use_dt else []) + [hsave, dy] + (list(acc) if acc is not None else [])
        a_shape = jax.ShapeDtypeStruct(a_tok.shape, f32)
        return pl.pallas_call(
            body,
            out_shape=[jax.ShapeDtypeStruct(q.shape, q.dtype), jax.ShapeDtypeStruct(k.shape, k.dtype),
                       jax.ShapeDtypeStruct(x.shape, f32), a_shape] + ([a_shape] if use_dt else []),
            grid=(G, nc),
            in_specs=[qs, qs, xs, as_] + ([as_] if use_dt else []) + [hs, xs] + ([qs, qs, xs] if acc is not None else []),
            out_specs=[qs, qs, xs, as_] + ([as_] if use_dt else []),
            scratch_shapes=[pltpu.VMEM((NB, Vw), f32)],
            compiler_params=_cparams(("parallel", "arbitrary")),
            name="scan_bwd",
        )(*ins)

    return call_fwd, call_bwd


def bidir_scan(q, k, x, a_f, a_b, dt_f, dt_b, *, nsub):
    use_dt = dt_f is not None
    a_f, a_b = chunk_cumsum(a_f, False), chunk_cumsum(a_b, True)
    fwd_f, bwd_f = scan_op(q, k, x, a_f, dt_f, rev=False, incl=True, nsub=nsub)
    fwd_b, bwd_b = scan_op(q, k, x, a_b, dt_b, rev=True, incl=False, nsub=nsub)

    def run(q, k, x, a_f, a_b, dt_f, dt_b):
        y_f, hs_f = fwd_f(q, k, x, a_f, dt_f)
        y, hs_b = fwd_b(q, k, x, a_b, dt_b, y_prev=y_f)
        return y, (hs_f, hs_b)

    def grads(q, k, x, a_f, a_b, dt_f, dt_b, hs, dy):
        first = bwd_f(q, k, x, a_f, dt_f, hs[0], dy)
        both = bwd_b(q, k, x, a_b, dt_b, hs[1], dy, acc=first[:3])
        return both[0], both[1], both[2], first[3], both[3], (first[4] if use_dt else None), (both[4] if use_dt else None)

    if use_dt:
        @jax.custom_vjp
        def op(q, k, x, a_f, a_b, dt_f, dt_b):
            return run(q, k, x, a_f, a_b, dt_f, dt_b)[0]

        def fwd(q, k, x, a_f, a_b, dt_f, dt_b):
            y, hs = run(q, k, x, a_f, a_b, dt_f, dt_b)
            return y, (q, k, x, a_f, a_b, dt_f, dt_b, hs)

        def bwd(res, dy):
            return grads(*res, dy)

        op.defvjp(fwd, bwd)
        return op(q, k, x, a_f, a_b, dt_f, dt_b)

    @jax.custom_vjp
    def op(q, k, x, a_f, a_b):
        return run(q, k, x, a_f, a_b, None, None)[0]

    def fwd(q, k, x, a_f, a_b):
        y, hs = run(q, k, x, a_f, a_b, None, None)
        return y, (q, k, x, a_f, a_b, hs)

    def bwd(res, dy):
        q, k, x, a_f, a_b, hs = res
        return grads(q, k, x, a_f, a_b, None, None, hs, dy)[:5]

    op.defvjp(fwd, bwd)
    return op(q, k, x, a_f, a_b)


def _swap_halves(x, dh):
    W = x.shape[1]
    lane = lax.broadcasted_iota(jnp.int32, (1, W), 1) % dh
    return jnp.where(lane < dh // 2, pltpu.roll(x, W - dh // 2, 1), pltpu.roll(x, dh // 2, 1))


def rotary(rq, rk, cos_t, sin_t):
    scale = RET_DH ** -0.5

    def fn(rq, rk, c, s):
        return rq * c + _swap_halves(rq, RET_DH) * s, (rk * c + _swap_halves(rk, RET_DH) * s) * scale

    def bwd_fn(rv, pv, dos):
        _, _, c, s = rv
        dq, dk = dos
        dk = dk * scale
        return (dq * c + _swap_halves(dq * s, RET_DH), dk * c + _swap_halves(dk * s, RET_DH)), ()

    return rowwise(fn, "rotary", [rq, rk, cos_t, sin_t], [], [MXU_DTYPE, MXU_DTYPE], n_diff_rows=2, bwd_fn=bwd_fn)


def _rope_tables(S, width):
    half = RET_DH // 2
    inv = 1.0 / (ROPE_BASE ** (jnp.arange(half, dtype=f32) / half))
    ang = jnp.arange(S, dtype=f32)[:, None] * inv[None, :]
    cos, sin = jnp.cos(ang), jnp.sin(ang)
    reps = width // RET_DH
    return jnp.tile(jnp.concatenate([cos, cos], axis=1), (1, reps)), jnp.tile(jnp.concatenate([-sin, sin], axis=1), (1, reps))


def _exact_dot(x, m):
    return jnp.dot(x, m, precision=lax.Precision.HIGHEST, preferred_element_type=f32)


def ret_post(y, rg, gn_g):
    W = y.shape[1]
    idx = np.arange(W) // RET_DH
    avg = jnp.asarray((idx[:, None] == idx[None, :]).astype(np.float32) / RET_DH)

    def fn(y, rg, g, avg):
        mu = _exact_dot(y, avg)
        d = y - mu
        var = _exact_dot(d * d, avg)
        return (_silu(rg) * (d * lax.rsqrt(var + EPS) * g),)

    return rowwise(fn, "ret_post", [y, rg], [gn_g.reshape(1, -1), avg], [MXU_DTYPE], n_diff_params=1)[0]


def _na_bias(rpb, win_r):
    H = rpb.shape[0]
    qc = np.arange(GRID_W)[:, None]
    kc = np.arange(GRID_W)[None, :]
    cstart = np.clip(qc - NA_WIN_C // 2, 0, GRID_W - NA_WIN_C)
    valid = (kc >= cstart) & (kc < cstart + NA_WIN_C)
    dc = np.clip(kc - qc, -(NA_WIN_C - 1), NA_WIN_C - 1) + (NA_WIN_C - 1)
    onehot = (dc[None] == np.arange(2 * NA_WIN_C - 1)[:, None, None]).astype(np.float32)
    t1 = jnp.einsum("hrd,dqk->hrqk", rpb.astype(f32), jnp.asarray(onehot), precision=lax.Precision.HIGHEST)
    per_delta = [t1[:, NA_WIN_R - 1 - d:NA_WIN_R - 1 - d + win_r] for d in range(win_r)]
    b = jnp.stack(per_delta, axis=1)
    b = jnp.where(jnp.asarray(valid)[None, None, None], b, NEG_INF)
    return jnp.transpose(b, (0, 1, 3, 2, 4)).reshape(H, win_r, GRID_W, win_r * GRID_W)


def _na_rows(rows):
    lane = lax.broadcasted_iota(jnp.int32, (1, rows[0][0].shape[1]), 1) // NA_DH
    scale = NA_DH ** -0.5
    ss = [[_bdot_raw(jnp.where(lane == i, q, 0.0) * scale, kw, "nt") + b for i, b in enumerate(bs)] for q, kw, _, bs in rows]
    es = [[jnp.exp(s - jnp.max(s, axis=1, keepdims=True)) for s in srow] for srow in ss]
    ps = [[e / jnp.sum(e, axis=1, keepdims=True) for e in erow] for erow in es]
    return [_lanes_by_head(lane, [_bdot_raw(p, vw, "nn") for p in prow]) for prow, (_, _, vw, _) in zip(ps, rows)]


def _lanes_by_head(lane, vals):
    if len(vals) == 2:
        return jnp.where(lane == 0, vals[0], vals[1])
    return sum(jnp.where(lane == i, v, 0.0) for i, v in enumerate(vals))


def _na_rows_bwd(rows):
    lane = lax.broadcasted_iota(jnp.int32, (1, rows[0][0].shape[1]), 1) // NA_DH
    scale = NA_DH ** -0.5
    heads = range(len(rows[0][3]))
    qis = [[jnp.where(lane == i, q, 0.0) * scale for i in heads] for q, _, _, _, _ in rows]
    dos = [[jnp.where(lane == i, do, 0.0) for i in heads] for _, _, _, _, do in rows]
    ss = [[_bdot_raw(qi, kw, "nt") + b for qi, b in zip(qrow, bs)] for qrow, (_, kw, _, bs, _) in zip(qis, rows)]
    dps = [[_bdot_raw(doi, vw, "nt") for doi in drow] for drow, (_, _, vw, _, _) in zip(dos, rows)]
    es = [[jnp.exp(s - jnp.max(s, axis=1, keepdims=True)) for s in srow] for srow in ss]
    ps = [[e / jnp.sum(e, axis=1, keepdims=True) for e in erow] for erow in es]
    dss = [[p * (dp - jnp.sum(dp * p, axis=1, keepdims=True)) for p, dp in zip(prow, dprow)] for prow, dprow in zip(ps, dps)]
    out = []
    for qrow, drow, prow, dsrow, (_, kw, _, _, _) in zip(qis, dos, ps, dss, rows):
        dq = _lanes_by_head(lane, [_bdot_raw(dsrow[i], kw, "nn") for i in heads]) * scale
        dk, dv = 0.0, 0.0
        for i in heads:
            dk = dk + _bdot_raw(dsrow[i], qrow[i], "tn")
            dv = dv + _bdot_raw(prow[i], drow[i], "tn")
        out.append((dq, dk, dv, dsrow))
    return out


def na_op(nq, nk, nv, bias):
    S, W = nq.shape
    rows = S // GRID_W
    win_r = bias.shape[1]
    nkeys = win_r * GRID_W
    hp = LANES // NA_DH
    npair = W // LANES
    RB = min(16, rows)
    nrb = rows // RB
    qspec = pl.BlockSpec((RB * GRID_W, LANES), lambda p, r: (r, p))
    kspec = pl.BlockSpec((S, LANES), lambda p, r: (0, p))
    bspec = pl.BlockSpec((hp, win_r, GRID_W, nkeys), lambda p, r: (p, 0, 0, 0))

    def window(r):
        r0 = jnp.clip(r - win_r // 2, 0, rows - win_r)
        return pl.multiple_of(r0 * GRID_W, GRID_W), r - r0

    def call_fwd(nq, nk, nv, bias):
        def body(q_ref, k_ref, v_ref, b_ref, o_ref):
            rb = pl.program_id(1)

            def step(j, c):
                args, q0s = [], []
                for u in range(NA_ROWS_PER_STEP):
                    i = j * NA_ROWS_PER_STEP + u
                    k0, d = window(rb * RB + i)
                    q0 = pl.multiple_of(i * GRID_W, GRID_W)
                    q0s.append(q0)
                    args.append((q_ref[pl.ds(q0, GRID_W), :].astype(f32), k_ref[pl.ds(k0, nkeys), :], v_ref[pl.ds(k0, nkeys), :],
                                 [b_ref[h, pl.ds(d, 1)][0] for h in range(hp)]))
                for q0, o in zip(q0s, _na_rows(args)):
                    o_ref[pl.ds(q0, GRID_W), :] = o.astype(o_ref.dtype)
                return c

            lax.fori_loop(0, RB // NA_ROWS_PER_STEP, step, 0)

        return pl.pallas_call(
            body,
            out_shape=jax.ShapeDtypeStruct((S, W), nq.dtype),
            grid=(npair, nrb),
            in_specs=[qspec, kspec, kspec, bspec],
            out_specs=qspec,
            compiler_params=_cparams(("parallel", "arbitrary")),
            name="na_fwd",
        )(nq, nk, nv, bias)

    def call_bwd(nq, nk, nv, bias, do):
        def body(q_ref, k_ref, v_ref, b_ref, do_ref, dq_ref, dk_ref, dv_ref, db_ref, dk_acc, dv_acc):
            rb = pl.program_id(1)

            @pl.when(rb == 0)
            def _():
                dk_acc[...] = jnp.zeros_like(dk_acc)
                dv_acc[...] = jnp.zeros_like(dv_acc)
                db_ref[...] = jnp.zeros_like(db_ref)

            def step(j, c):
                args, spots = [], []
                for u in range(NA_ROWS_PER_STEP):
                    i = j * NA_ROWS_PER_STEP + u
                    k0, d = window(rb * RB + i)
                    q0 = pl.multiple_of(i * GRID_W, GRID_W)
                    spots.append((q0, k0, d))
                    args.append((q_ref[pl.ds(q0, GRID_W), :].astype(f32), k_ref[pl.ds(k0, nkeys), :], v_ref[pl.ds(k0, nkeys), :],
                                 [b_ref[h, pl.ds(d, 1)][0] for h in range(hp)], do_ref[pl.ds(q0, GRID_W), :].astype(f32)))
                for (q0, k0, d), (dq, dk, dv, dbs) in zip(spots, _na_rows_bwd(args)):
                    dq_ref[pl.ds(q0, GRID_W), :] = dq.astype(dq_ref.dtype)
                    dk_acc[pl.ds(k0, nkeys), :] += dk
                    dv_acc[pl.ds(k0, nkeys), :] += dv
                    for h in range(hp):
                        db_ref[h, pl.ds(d, 1)] += dbs[h][None]
                return c

            lax.fori_loop(0, RB // NA_ROWS_PER_STEP, step, 0)

            @pl.when(rb == nrb - 1)
            def _():
                dk_ref[...] = dk_acc[...].astype(dk_ref.dtype)
                dv_ref[...] = dv_acc[...].astype(dv_ref.dtype)

        return pl.pallas_call(
            body,
            out_shape=[jax.ShapeDtypeStruct((S, W), nq.dtype), jax.ShapeDtypeStruct((S, W), nk.dtype),
                       jax.ShapeDtypeStruct((S, W), nv.dtype), jax.ShapeDtypeStruct(bias.shape, f32)],
            grid=(npair, nrb),
            in_specs=[qspec, kspec, kspec, bspec, qspec],
            out_specs=[qspec, kspec, kspec, bspec],
            scratch_shapes=[pltpu.VMEM((S, LANES), f32), pltpu.VMEM((S, LANES), f32)],
            compiler_params=_cparams(("parallel", "arbitrary")),
            name="na_bwd",
        )(nq, nk, nv, bias, do)

    @jax.custom_vjp
    def op(nq, nk, nv, bias):
        return call_fwd(nq, nk, nv, bias)

    def fwd(nq, nk, nv, bias):
        return call_fwd(nq, nk, nv, bias), (nq, nk, nv, bias)

    def bwd(res, do):
        return tuple(call_bwd(*res, do))

    op.defvjp(fwd, bwd)
    return op(nq, nk, nv, bias)


def ssd_dt(dt_raw, dt_bias, a_neg):
    def fn(r, b, a):
        dt = _softplus(r + b)
        return dt, dt * a

    return rowwise(fn, "ssd_dt", [dt_raw], [dt_bias, a_neg], [f32, f32])


def ssd_post(y, xs, z, d_skip_lanes, norm_g, groups):
    def fn(y, xs, z, dsk, g):
        y = (y + xs * dsk) * _silu(z)
        return (y * lax.rsqrt(jnp.mean(y * y, axis=-1, keepdims=True) + EPS) * g,)

    return rowwise(fn, "ssd_post", [y, xs, z], [d_skip_lanes.reshape(1, -1), norm_g.reshape(1, -1)], [MXU_DTYPE],
                   ncol=groups)[0]


def _heads_major(t, groups):
    S = t.shape[0]
    return jnp.transpose(t.reshape(S, groups, -1), (1, 0, 2))


def retention_na_mixer(hn, w_in, decay_logit, gn_g, rpb, w_out, tables):
    S = hn.shape[0]
    R = RET_HEADS * RET_DH
    NW = NA_HEADS * NA_DH
    cols = lambda a, b: w_in[:, a:b]
    rq, rk, rv, rg = (mm(hn, cols(j * R, (j + 1) * R)) for j in range(4))
    nq, nk, nv = (mm(hn, cols(4 * R + j * NW, 4 * R + (j + 1) * NW), out_dtype=MXU_DTYPE) for j in range(3))
    qr, kr = rotary(rq, rk, *tables)
    log_gamma = -_softplus(-decay_logit.astype(f32))
    hp = LANES // RET_DH
    hpad = -(-RET_HEADS // 8) * 8
    pad8 = lambda a: jnp.pad(a.reshape(1, 1, RET_HEADS), ((0, 0), (0, 0), (0, hpad - RET_HEADS)))
    a_f = jnp.broadcast_to(pad8(log_gamma[0]), (1, S, hpad))
    a_b = jnp.broadcast_to(pad8(log_gamma[1]), (1, S, hpad))
    ret = ret_post(bidir_scan(qr, kr, rv, a_f, a_b, None, None, nsub=hp), rg, gn_g)
    rows = S // GRID_W
    nao = na_op(nq, nk, nv, _na_bias(rpb, min(NA_WIN_R, rows)))
    return mm(ret, w_out[:R]) + mm(nao, w_out[R:])


def ssd_mixer(hn, w_in, conv_w, conv_b, dt_bias, a_log, d_skip, norm_g, w_out):
    heads = d_skip.shape[0]
    inner = heads * SSD_HEADDIM
    gs = SSD_GROUPS * SSD_STATE
    o_x, o_b, o_c, o_dt = inner, 2 * inner, 2 * inner + gs, 2 * inner + 2 * gs
    z = mm(hn, w_in[:, :inner])
    dt_raw = mm(hn, w_in[:, o_dt:])
    xs, bm, cm = (mm_conv_act(hn, [w_in[:, a:b]], [conv_w[:, a - inner:b - inner]], [conv_b[a - inner:b - inner]],
                              _silu, _silu_bwd, f32, "conv_silu") for a, b in ((o_x, o_b), (o_b, o_c), (o_c, o_dt)))
    a_neg = -jnp.exp(a_log.astype(f32)).reshape(1, -1)
    dt, la = ssd_dt(dt_raw, dt_bias.astype(f32).reshape(1, -1), a_neg)
    dt_f, dt_b = _heads_major(dt[:, :heads], SSD_GROUPS), _heads_major(dt[:, heads:], SSD_GROUPS)
    la_f, la_b = _heads_major(la[:, :heads], SSD_GROUPS), _heads_major(la[:, heads:], SSD_GROUPS)
    y = bidir_scan(cm, bm, xs, la_f, la_b, dt_f, dt_b, nsub=1)
    y = ssd_post(y, xs, z, jnp.repeat(d_skip.astype(f32), SSD_HEADDIM), norm_g, SSD_GROUPS)
    return mm(y, w_out)


def conv_geglu_ffn(hf, w_up, conv_w, conv_b, w_down):
    F = w_down.shape[0]
    a = mm_conv_act(hf, [w_up[:, :F], w_up[:, F:]], [conv_w[:, :F], conv_w[:, F:]], [conv_b[:F], conv_b[F:]],
                    _geglu, _geglu_bwd, MXU_DTYPE, "conv_geglu")
    return mm(a, w_down)


def model_loss(x, tgt, big, small, rep):
    S = x.shape[0]
    depth = rep["norm_mix_pre"].shape[0]
    tables = _rope_tables(S, RET_HEADS * RET_DH)
    hn = rms(x, rep["norm_mix_pre"][0], MXU_DTYPE)
    for layer in range(depth):
        i = layer // 2
        if layer % 2 == 0:
            m = retention_na_mixer(hn, big["ab_w_in"][i], rep["ab_ret_decay_logit"][i], rep["ab_ret_gn_g"][i],
                                   rep["ab_na_rpb"][i], big["ab_w_out"][i], tables)
        else:
            m = ssd_mixer(hn, big["c_w_in"][i], small["c_conv_w"][i], small["c_conv_b"][i], rep["c_dt_bias"][i],
                          rep["c_a_log"][i], rep["c_d_skip"][i], small["c_norm_g"][i], big["c_w_out"][i])
        x, hf = rms_residual_norm(m, rep["norm_mix_post"][layer], x, rep["norm_ffn_pre"][layer])
        f = conv_geglu_ffn(hf, big["ffn_w_up"][layer], small["ffn_conv_w"][layer], rep["ffn_conv_b"][layer],
                           big["ffn_w_down"][layer])
        if layer + 1 < depth:
            x, hn = rms_residual_norm(f, rep["norm_ffn_post"][layer], x, rep["norm_mix_pre"][layer + 1])
        else:
            x = rms_residual(f, rep["norm_ffn_post"][layer], x)
    return loss_op(x, tgt)


def _mesh_pos():
    return lax.axis_index("x"), lax.axis_index("y"), lax.axis_index("c")


def _any_specs(n):
    return [pl.BlockSpec(memory_space=pl.ANY)] * n


def gather_chips(locals_):
    nbuf = len(locals_)
    CH = COPY_CHUNKS

    def body(*refs):
        x_refs, out_refs, (send_sems, recv_sems) = refs[:nbuf], refs[nbuf:2 * nbuf], refs[2 * nbuf:]
        x, y, c = _mesh_pos()
        my = 2 * x + y
        chips = [(1 - x, y), (x, 1 - y), (1 - x, 1 - y)]
        plans = []
        for a, (x_ref, out_ref) in enumerate(zip(x_refs, out_refs)):
            half = x_ref.shape[0] // 2
            q = half // CH

            def piece(ref, h, j, half=half, q=q):
                return ref.at[pl.ds(pl.multiple_of(h * half + j * q, PACK_ALIGN), q), :]

            def copy(k, src, chip, h, j, to, out_ref=out_ref, piece=piece, base=a * 6 * CH):
                return pltpu.make_async_remote_copy(src_ref=src, dst_ref=piece(out_ref.at[chip], h, j),
                                                    send_sem=send_sems.at[base + k], recv_sem=recv_sems.at[base + k],
                                                    device_id=to, device_id_type=pl.DeviceIdType.MESH)

            plans.append((x_ref, out_ref, piece, copy))

        first, passed = [], []
        for x_ref, out_ref, piece, copy in plans:
            first.append([[copy(k * CH + j, piece(x_ref, c, j), my, c, j, (cx, cy, c)) for j in range(CH)]
                          for k, (cx, cy) in enumerate(chips)])
            passed.append([[copy((3 + k) * CH + j, piece(out_ref.at[2 * cx + cy], c, j), 2 * cx + cy, c, j, (x, y, 1 - c))
                            for j in range(CH)] for k, (cx, cy) in enumerate(chips)])
        for a in range(nbuf):
            for j in range(CH):
                for k in range(3):
                    first[a][k][j].start()
        for a, (x_ref, _, piece, copy) in enumerate(plans):
            for j in range(CH):
                for k, (cx, cy) in enumerate(chips):
                    copy(k * CH + j, piece(x_ref, c, j), 2 * cx + cy, c, j, (cx, cy, c)).wait_recv()
                    passed[a][k][j].start()
        for a, (x_ref, _, piece, copy) in enumerate(plans):
            for j in range(CH):
                for k, (cx, cy) in enumerate(chips):
                    copy((3 + k) * CH + j, piece(x_ref, c, j), 2 * cx + cy, 1 - c, j, (x, y, 1 - c)).wait_recv()
        for a in range(nbuf):
            for k in range(3):
                for cp in first[a][k] + passed[a][k]:
                    cp.wait_send()

    return pl.pallas_call(
        body,
        out_shape=[jax.ShapeDtypeStruct((N_CHIPS,) + l.shape, l.dtype) for l in locals_],
        in_specs=_any_specs(nbuf),
        out_specs=_any_specs(nbuf),
        scratch_shapes=[pltpu.SemaphoreType.DMA((nbuf * 6 * CH,)), pltpu.SemaphoreType.DMA((nbuf * 6 * CH,))],
        name="gather_chips",
    )(*locals_)


def pair_swap(parts):
    nbuf = len(parts)
    n = N_CHIPS
    CH = COPY_CHUNKS

    def body(*refs):
        p_refs, got_refs, (send_sems, recv_sems) = refs[:nbuf], refs[nbuf:2 * nbuf], refs[2 * nbuf:]
        x, y, c = _mesh_pos()
        swap = []
        for a, (p_ref, got_ref) in enumerate(zip(p_refs, got_refs)):
            half = p_ref.shape[1] // 2
            q = half // CH
            for s in range(n):
                for j in range(CH):
                    k = (a * n + s) * CH + j
                    src = p_ref.at[s, pl.ds(pl.multiple_of((1 - c) * half + j * q, PACK_ALIGN), q), :]
                    swap.append(pltpu.make_async_remote_copy(src_ref=src, dst_ref=got_ref.at[s, pl.ds(j * q, q), :],
                                                             send_sem=send_sems.at[k], recv_sem=recv_sems.at[k],
                                                             device_id=(x, y, 1 - c), device_id_type=pl.DeviceIdType.MESH))
        for cp in swap:
            cp.start()
        for cp in swap:
            cp.wait()

    return pl.pallas_call(
        body,
        out_shape=[jax.ShapeDtypeStruct((n, p.shape[1] // 2, p.shape[2]), p.dtype) for p in parts],
        in_specs=_any_specs(nbuf),
        out_specs=_any_specs(nbuf),
        scratch_shapes=[pltpu.SemaphoreType.DMA((nbuf * n * CH,)), pltpu.SemaphoreType.DMA((nbuf * n * CH,))],
        name="pair_swap",
    )(*parts)


def chip_exchange(parts):
    nbuf = len(parts)

    def body(*refs):
        p_refs, out_refs, (send_sems, recv_sems) = refs[:nbuf], refs[nbuf:2 * nbuf], refs[2 * nbuf:]
        x, y, c = _mesh_pos()
        my = 2 * x + y
        chips = [(1 - x, y), (x, 1 - y), (1 - x, 1 - y)]

        def copy(a, k, src_slot, dst_slot, to):
            return pltpu.make_async_remote_copy(src_ref=p_refs[a].at[src_slot], dst_ref=out_refs[a].at[dst_slot],
                                                send_sem=send_sems.at[3 * a + k], recv_sem=recv_sems.at[3 * a + k],
                                                device_id=to, device_id_type=pl.DeviceIdType.MESH)

        sends = [copy(a, k, 2 * cx + cy, my, (cx, cy, c)) for a in range(nbuf) for k, (cx, cy) in enumerate(chips)]
        for cp in sends:
            cp.start()
        for a in range(nbuf):
            for k, (cx, cy) in enumerate(chips):
                copy(a, k, my, 2 * cx + cy, (cx, cy, c)).wait_recv()
        for cp in sends:
            cp.wait_send()

    return pl.pallas_call(
        body,
        out_shape=[jax.ShapeDtypeStruct(p.shape, p.dtype) for p in parts],
        in_specs=_any_specs(nbuf),
        out_specs=_any_specs(nbuf),
        scratch_shapes=[pltpu.SemaphoreType.DMA((3 * nbuf,)), pltpu.SemaphoreType.DMA((3 * nbuf,))],
        name="chip_exchange",
    )(*parts)


def pair_share(mine):
    nbuf = len(mine)
    CH = COPY_CHUNKS

    def body(*refs):
        m_refs, out_refs, (send_sems, recv_sems) = refs[:nbuf], refs[nbuf:2 * nbuf], refs[2 * nbuf:]
        x, y, c = _mesh_pos()
        swap = []
        for a, (m_ref, out_ref) in enumerate(zip(m_refs, out_refs)):
            q = m_ref.shape[0] // CH
            for j in range(CH):
                swap.append(pltpu.make_async_remote_copy(src_ref=m_ref.at[pl.ds(j * q, q), :], dst_ref=out_ref.at[pl.ds(j * q, q), :],
                                                         send_sem=send_sems.at[a * CH + j], recv_sem=recv_sems.at[a * CH + j],
                                                         device_id=(x, y, 1 - c), device_id_type=pl.DeviceIdType.MESH))
        for cp in swap:
            cp.start()
        for cp in swap:
            cp.wait()

    return pl.pallas_call(
        body,
        out_shape=[jax.ShapeDtypeStruct(m.shape, m.dtype) for m in mine],
        in_specs=_any_specs(nbuf),
        out_specs=_any_specs(nbuf),
        scratch_shapes=[pltpu.SemaphoreType.DMA((nbuf * CH,)), pltpu.SemaphoreType.DMA((nbuf * CH,))],
        name="pair_share",
    )(*mine)


def sum_chips(recv, own):
    n, R, Wd = recv.shape
    tr = _pick(R, (512, 256, 128, 64, 32, 16, 8))

    def body(r_ref, p_ref, o_ref):
        my = 2 * lax.axis_index("x") + lax.axis_index("y")
        acc = jnp.zeros((tr, Wd), f32)
        for s in range(n):
            acc = acc + jnp.where(my == s, p_ref[s], r_ref[s]).astype(f32)
        o_ref[...] = acc

    spec = pl.BlockSpec((n, tr, Wd), lambda i: (0, i, 0))
    return pl.pallas_call(
        body,
        out_shape=jax.ShapeDtypeStruct((R, Wd), f32),
        grid=(R // tr,),
        in_specs=[spec, spec],
        out_specs=pl.BlockSpec((tr, Wd), lambda i: (i, 0)),
        compiler_params=_cparams(("parallel",)),
        name="sum_chips",
    )(recv, own)


def add_pair(parts, got):
    n, R, Wd = parts.shape
    half = R // 2
    tr = _pick(half, (512, 256, 128, 64, 32, 16, 8))
    nb = half // tr

    def body(lo_ref, hi_ref, g_ref, o_ref):
        mine = jnp.where(lax.axis_index("c") == 0, lo_ref[...], hi_ref[...])
        o_ref[...] = (mine.astype(f32) + g_ref[...].astype(f32)).astype(o_ref.dtype)

    spec = pl.BlockSpec((1, tr, Wd), lambda s, i: (s, i, 0))
    return pl.pallas_call(
        body,
        out_shape=jax.ShapeDtypeStruct(got.shape, parts.dtype),
        grid=(n, nb),
        in_specs=[spec, pl.BlockSpec((1, tr, Wd), lambda s, i: (s, nb + i, 0)), spec],
        out_specs=spec,
        compiler_params=_cparams(("parallel", "parallel")),
        name="add_pair",
    )(parts, parts, got)


def reduce_scatter(parts):
    chip_sum = [add_pair(p, g) for p, g in zip(parts, pair_swap(parts))]
    mine = [sum_chips(r, s) for r, s in zip(chip_exchange(chip_sum), chip_sum)]
    first = lax.axis_index("c") == 0
    return [jnp.concatenate([jnp.where(first, m, t), jnp.where(first, t, m)], axis=0) for m, t in zip(mine, pair_share(mine))]


def adamw(w, g, m, v):
    shp = w.shape
    if w.size * 4 <= (1 << 20):
        grid, block, imap = (1,), shp, lambda i: (0,) * len(shp)
    else:
        n0, R, C = shp
        tr = _divisor_tile(R, lambda t: t * C * 4 <= (1 << 20), 8)
        grid, block, imap = (n0, R // tr), (1, tr, C), lambda j, i: (j, i, 0)

    def body(w_ref, g_ref, m_ref, v_ref, d_ref, mo_ref, vo_ref):
        g = g_ref[...]
        m = ADAM_B1 * m_ref[...] + (1.0 - ADAM_B1) * g
        v = ADAM_B2 * v_ref[...] + (1.0 - ADAM_B2) * (g * g)
        m_hat = m / (1.0 - ADAM_B1 ** ADAM_STEP)
        v_hat = v / (1.0 - ADAM_B2 ** ADAM_STEP)
        d_ref[...] = -ADAM_LR * (m_hat / (jnp.sqrt(v_hat) + ADAM_EPS) + ADAM_WD * w_ref[...])
        mo_ref[...] = m
        vo_ref[...] = v

    spec = pl.BlockSpec(block, imap)
    return pl.pallas_call(
        body,
        out_shape=[jax.ShapeDtypeStruct(shp, f32)] * 3,
        grid=grid,
        in_specs=[spec] * 4,
        out_specs=[spec] * 3,
        compiler_params=_cparams(("parallel",) * len(grid)),
        name="adamw",
    )(w, g, m, v)


def _pack(arrs, dtype):
    flat = jnp.concatenate([a.astype(dtype).reshape(-1) for a in arrs])
    n = flat.shape[0]
    unit = PACK_W * PACK_ROWS
    padded = -(-n // unit) * unit
    return jnp.pad(flat, (0, padded - n)).reshape(-1, PACK_W)


def _unpack(buf, shapes):
    flat = buf.reshape(-1)
    out, off = [], 0
    for s in shapes:
        n = int(np.prod(s))
        out.append(flat[off:off + n].reshape(s))
        off += n
    return out


BIG = (("ab_w_in", 2), ("ab_w_out", 1), ("c_w_in", 2), ("c_w_out", 1), ("ffn_w_up", 2), ("ffn_w_down", 1))
SMALL = (("c_conv_w", 2), ("c_conv_b", 1), ("c_norm_g", 1), ("ffn_conv_w", 2))
REP = ("norm_mix_pre", "norm_mix_post", "norm_ffn_pre", "norm_ffn_post", "ab_ret_decay_logit", "ab_ret_gn_g", "ab_na_rpb",
       "c_dt_bias", "c_a_log", "c_d_skip", "ffn_conv_b")
WEIGHTS = ("norm_mix_pre", "norm_mix_post", "norm_ffn_pre", "norm_ffn_post", "ab_w_in", "ab_ret_decay_logit", "ab_ret_gn_g",
           "ab_na_rpb", "ab_w_out", "c_w_in", "c_conv_w", "c_conv_b", "c_dt_bias", "c_a_log", "c_d_skip", "c_norm_g", "c_w_out",
           "ffn_w_up", "ffn_conv_w", "ffn_conv_b", "ffn_w_down")


BIG_GROUPS = (("ab_w_in",), ("ab_w_out", "c_w_out", "ffn_w_down"), ("c_w_in",), ("ffn_w_up",))
BIG_AXIS = dict(BIG)
ROW_UNIT = 128


def _rows(arrs, dtype):
    C = arrs[0].shape[-1]
    buf = jnp.concatenate([a.astype(dtype).reshape(-1, C) for a in arrs], axis=0)
    pad = -buf.shape[0] % ROW_UNIT
    return jnp.pad(buf, ((0, pad), (0, 0))) if pad else buf


def _unrows(buf, shapes):
    out, off = [], 0
    for s in shapes:
        n = int(np.prod(s[:-1]))
        out.append(buf[off:off + n].reshape(s))
        off += n
    return out


def _gather_all(w):
    bufs = [_rows([w[n] for n in grp], MXU_DTYPE) for grp in BIG_GROUPS] + [_pack([w[n] for n, _ in SMALL], f32)]
    got = gather_chips(bufs)
    my = 2 * lax.axis_index("x") + lax.axis_index("y")

    def whole(n, ax, dtype, pieces):
        return jnp.concatenate([jnp.where(my == s, w[n].astype(dtype), pieces[s]) for s in range(N_CHIPS)], axis=ax)

    big = {}
    for grp, g in zip(BIG_GROUPS, got):
        per_chip = [_unrows(g[s], [w[n].shape for n in grp]) for s in range(N_CHIPS)]
        for j, n in enumerate(grp):
            big[n] = whole(n, BIG_AXIS[n], MXU_DTYPE, [per_chip[s][j] for s in range(N_CHIPS)])
    per_chip = [_unpack(got[-1][s], [w[n].shape for n, _ in SMALL]) for s in range(N_CHIPS)]
    small = {n: whole(n, ax, f32, [per_chip[s][j] for s in range(N_CHIPS)]) for j, (n, ax) in enumerate(SMALL)}
    return big, small


def _reduce_all(gbig, gsmall, grep, w):
    split = {n: jnp.split(g, N_CHIPS, axis=BIG_AXIS[n]) for n, g in gbig.items()}
    parts = [jnp.stack([_rows([split[n][s] for n in grp], MXU_DTYPE) for s in range(N_CHIPS)]) for grp in BIG_GROUPS]
    ssplit = {n: jnp.split(gsmall[n], N_CHIPS, axis=ax) for n, ax in SMALL}
    parts.append(jnp.stack([_pack([ssplit[n][s] for n, _ in SMALL] + [grep[n] for n in REP], f32) for s in range(N_CHIPS)]))
    res = reduce_scatter(parts)
    grads = {}
    for grp, r in zip(BIG_GROUPS, res):
        grads.update(zip(grp, _unrows(r, [w[n].shape for n in grp])))
    small_names = [n for n, _ in SMALL] + list(REP)
    grads.update(zip(small_names, _unpack(res[-1], [w[n].shape for n in small_names])))
    return grads


def kernel(x, norm_mix_pre, norm_mix_post, norm_ffn_pre, norm_ffn_post, ab_w_in, ab_ret_decay_logit, ab_ret_gn_g, ab_na_rpb, ab_w_out, c_w_in, c_conv_w, c_conv_b, c_dt_bias, c_a_log, c_d_skip, c_norm_g, c_w_out, ffn_w_up, ffn_conv_w, ffn_conv_b, ffn_w_down, loss_target, m_norm_mix_pre, m_norm_mix_post, m_norm_ffn_pre, m_norm_ffn_post, m_ab_w_in, m_ab_ret_decay_logit, m_ab_ret_gn_g, m_ab_na_rpb, m_ab_w_out, m_c_w_in, m_c_conv_w, m_c_conv_b, m_c_dt_bias, m_c_a_log, m_c_d_skip, m_c_norm_g, m_c_w_out, m_ffn_w_up, m_ffn_conv_w, m_ffn_conv_b, m_ffn_w_down, v_norm_mix_pre, v_norm_mix_post, v_norm_ffn_pre, v_norm_ffn_post, v_ab_w_in, v_ab_ret_decay_logit, v_ab_ret_gn_g, v_ab_na_rpb, v_ab_w_out, v_c_w_in, v_c_conv_w, v_c_conv_b, v_c_dt_bias, v_c_a_log, v_c_d_skip, v_c_norm_g, v_c_w_out, v_ffn_w_up, v_ffn_conv_w, v_ffn_conv_b, v_ffn_w_down):
    args = dict(locals())
    w = {n: args[n] for n in WEIGHTS}
    mom = {n: args["m_" + n] for n in WEIGHTS}
    var = {n: args["v_" + n] for n in WEIGHTS}

    big, small = _gather_all(w)
    rep = {n: w[n] for n in REP}

    def loss_fn(xs, big, small, rep):
        return model_loss(xs, loss_target[0], big, small, rep)

    loss, (gx, gbig, gsmall, grep) = jax.value_and_grad(loss_fn, argnums=(0, 1, 2, 3))(x[0], big, small, rep)
    loss = lax.psum(loss, ("x", "y", "c"))

    grads = _reduce_all(gbig, gsmall, grep, w)

    delta, new_m, new_v = {}, {}, {}
    for n in WEIGHTS:
        delta[n], new_m[n], new_v[n] = adamw(w[n], grads[n], mom[n], var[n])

    return (loss, gx[None], *[grads[n] for n in WEIGHTS], *[delta[n] for n in WEIGHTS],
            *[new_m[n] for n in WEIGHTS], *[new_v[n] for n in WEIGHTS])
```

```python
import functools
import math

import numpy as np
import jax
import jax.numpy as jnp
from jax import lax
from jax.experimental import pallas as pl
from jax.experimental.pallas import tpu as pltpu

f32 = jnp.float32
bf16 = jnp.bfloat16
MXU_DTYPE = bf16

GRID_W = 64
CHUNK = 128
EPS = 1e-6
RET_HEADS = 8
RET_DH = 64
ROPE_BASE = 10000.0
NA_HEADS = 8
NA_DH = 64
NA_WIN_R = 8
NA_WIN_C = 16
NA_ROWS_PER_STEP = 8
SSD_HEADDIM = 64
SSD_GROUPS = 4
SSD_STATE = 128
ADAM_LR = 0.001
ADAM_B1 = 0.9
ADAM_B2 = 0.999
ADAM_EPS = 1e-08
ADAM_WD = 0.01
ADAM_STEP = 10

LANES = 128
HEAD_W = 64
PACK_W = 512
PACK_ROWS = 1024
PACK_ALIGN = 16
COPY_CHUNKS = 2
VMEM_LIMIT = 56 * 1024 * 1024
MM_BLOCK_BYTES = 6 * 1024 * 1024
ROW_BLOCK_BYTES = 16 * 1024 * 1024
N_CHIPS = 4
N_DEV = 8
NEG_INF = -1e30

_DIMS = {"nn": (((1,), (0,)), ((), ())), "nt": (((1,), (1,)), ((), ())), "tn": (((0,), (0,)), ((), ()))}


def _cparams(sem=None):
    return pltpu.CompilerParams(dimension_semantics=sem, vmem_limit_bytes=VMEM_LIMIT)


def _pick(dim, cands):
    for c in cands:
        if dim % c == 0:
            return c
    return dim


def _divisor_tile(dim, fits, align):
    for d in range(1, dim + 1):
        t = dim // d
        if dim % d == 0 and t % align == 0 and fits(t):
            return t
    return dim


def _bdot_raw(a, b, mode):
    return lax.dot_general(a.astype(MXU_DTYPE), b.astype(MXU_DTYPE), _DIMS[mode], preferred_element_type=f32)


@functools.partial(jax.custom_vjp, nondiff_argnums=(2,))
def bdot(a, b, mode):
    return _bdot_raw(a, b, mode)


def _bdot_fwd(a, b, mode):
    return _bdot_raw(a, b, mode), (a, b)


def _bdot_bwd(mode, res, g):
    a, b = res
    if mode == "nn":
        da, db = _bdot_raw(g, b, "nt"), _bdot_raw(a, g, "tn")
    elif mode == "nt":
        da, db = _bdot_raw(g, b, "nn"), _bdot_raw(g, a, "tn")
    else:
        da, db = _bdot_raw(b, g, "nt"), _bdot_raw(a, g, "nn")
    return da.astype(a.dtype), db.astype(b.dtype)


bdot.defvjp(_bdot_fwd, _bdot_bwd)


def _mm_call(a, b, mode, out_dtype):
    if mode == "nn":
        (M, K), (K2, N) = a.shape, b.shape
    elif mode == "nt":
        (M, K), (N, K2) = a.shape, b.shape
    else:
        (K, M), (K2, N) = a.shape, b.shape
    assert K == K2, (a.shape, b.shape, mode)
    a_bytes, b_bytes, o_bytes = a.dtype.itemsize, b.dtype.itemsize, jnp.dtype(out_dtype).itemsize
    if mode == "tn":
        tn = _divisor_tile(N, lambda t: t <= 1536, LANES)
        tm = _divisor_tile(M, lambda t: t * tn * 4 <= MM_BLOCK_BYTES, 8)
        tk = _divisor_tile(K, lambda t: t * tm * a_bytes <= MM_BLOCK_BYTES and t * tn * b_bytes <= MM_BLOCK_BYTES, LANES)
    else:
        tk, tn = K, N
        tm = _divisor_tile(M, lambda t: t * K * a_bytes <= MM_BLOCK_BYTES and t * N * o_bytes <= MM_BLOCK_BYTES, 8)
    nk = K // tk
    if mode == "nn":
        a_spec = pl.BlockSpec((tm, tk), lambda i, j, k: (i, k))
        b_spec = pl.BlockSpec((tk, tn), lambda i, j, k: (k, j))
    elif mode == "nt":
        a_spec = pl.BlockSpec((tm, tk), lambda i, j, k: (i, k))
        b_spec = pl.BlockSpec((tn, tk), lambda i, j, k: (j, k))
    else:
        a_spec = pl.BlockSpec((tk, tm), lambda i, j, k: (k, i))
        b_spec = pl.BlockSpec((tk, tn), lambda i, j, k: (k, j))

    if nk == 1:
        def body(a_ref, b_ref, o_ref):
            o_ref[...] = _bdot_raw(a_ref[...], b_ref[...], mode).astype(o_ref.dtype)
    else:
        def body(a_ref, b_ref, o_ref, acc_ref):
            k = pl.program_id(2)

            @pl.when(k == 0)
            def _():
                acc_ref[...] = jnp.zeros_like(acc_ref)

            acc_ref[...] += _bdot_raw(a_ref[...], b_ref[...], mode)

            @pl.when(k == nk - 1)
            def _():
                o_ref[...] = acc_ref[...].astype(o_ref.dtype)

    return pl.pallas_call(
        body,
        out_shape=jax.ShapeDtypeStruct((M, N), out_dtype),
        grid=(M // tm, N // tn, nk),
        in_specs=[a_spec, b_spec],
        out_specs=pl.BlockSpec((tm, tn), lambda i, j, k: (i, j)),
        scratch_shapes=[pltpu.VMEM((tm, tn), f32)] if nk > 1 else [],
        compiler_params=_cparams(("parallel", "parallel", "arbitrary")),
        name="mm_" + mode,
    )(a, b)


def mm(a, b, mode="nn", out_dtype=f32):
    @jax.custom_vjp
    def op(a, b):
        return _mm_call(a, b, mode, out_dtype)

    def fwd(a, b):
        return _mm_call(a, b, mode, out_dtype), (a, b)

    def bwd(res, g):
        a, b = res
        if mode == "nn":
            return _mm_call(g, b, "nt", a.dtype), _mm_call(a, g, "tn", b.dtype)
        if mode == "nt":
            return _mm_call(g, b, "nn", a.dtype), _mm_call(g, a, "tn", b.dtype)
        return _mm_call(b, g, "nt", a.dtype), _mm_call(a, g, "nn", b.dtype)

    op.defvjp(fwd, bwd)
    return op(a, b)


def _row_tile(S, row_bytes):
    tm = 1024
    while tm > 8 and (tm * row_bytes > ROW_BLOCK_BYTES or S % tm):
        tm //= 2
    return tm


def rowwise(fn, name, rows, params, out_dtypes, n_diff_rows=None, n_diff_params=None, ncol=1, bwd_fn=None):
    rows, params = list(rows), list(params)
    nr, npar = len(rows), len(params)
    ndr = nr if n_diff_rows is None else n_diff_rows
    ndp = npar if n_diff_params is None else n_diff_params
    S = rows[0].shape[0]
    rw = [r.shape[1] // ncol for r in rows]
    pshape = [(p.shape[0], p.shape[1] // ncol) for p in params]

    def block_structs(tm):
        return ([jax.ShapeDtypeStruct((tm, w), f32) for w in rw] + [jax.ShapeDtypeStruct(s, f32) for s in pshape])

    outs_s = jax.eval_shape(fn, *block_structs(8))
    ow = [o.shape[1] for o in outs_s]
    nout = len(ow)
    row_bytes = 4 * (sum(rw) * 2 + sum(ow) * 2)
    tm = _row_tile(S, row_bytes)
    grid = (ncol, S // tm)

    def rspec(w):
        return pl.BlockSpec((tm, w), lambda g, i: (i, g))

    def pspec(s):
        return pl.BlockSpec(s, lambda g, i: (0, g))

    def call_fwd(*args):
        def body(*refs):
            vals = [r[...].astype(f32) for r in refs[:nr + npar]]
            res = fn(*vals)
            for o, r in zip(refs[nr + npar:], res):
                o[...] = r.astype(o.dtype)

        return pl.pallas_call(
            body,
            out_shape=[jax.ShapeDtypeStruct((S, w * ncol), dt) for w, dt in zip(ow, out_dtypes)],
            grid=grid,
            in_specs=[rspec(w) for w in rw] + [pspec(s) for s in pshape],
            out_specs=[rspec(w) for w in ow],
            compiler_params=_cparams(("parallel", "parallel")),
            name=name + "_fwd",
        )(*args)

    def call_bwd(args, douts):
        def body(*refs):
            in_refs = refs[:nr + npar]
            do_refs = refs[nr + npar:nr + npar + nout]
            dr_refs = refs[nr + npar + nout:nr + npar + nout + ndr]
            dp_refs = refs[nr + npar + nout + ndr:]
            rv = [r[...] for r in in_refs[:nr]]
            pv = [r[...] for r in in_refs[nr:]]
            dos = [d[...].astype(f32) for d in do_refs]
            if bwd_fn is not None:
                drs, dps = bwd_fn(rv, pv, dos)
            else:
                def f(*a):
                    return fn(*a[:ndr], *rv[ndr:], *a[ndr:], *pv[ndp:])

                _, vjp = jax.vjp(f, *[v.astype(f32) for v in rv[:ndr]], *pv[:ndp])
                cts = vjp(tuple(dos))
                drs, dps = cts[:ndr], cts[ndr:]
            for r, ct in zip(dr_refs, drs):
                r[...] = ct.astype(r.dtype)
            if ndp:
                @pl.when(pl.program_id(1) == 0)
                def _():
                    for r in dp_refs:
                        r[...] = jnp.zeros_like(r)

                for r, ct in zip(dp_refs, dps):
                    r[...] += ct

        return pl.pallas_call(
            body,
            out_shape=[jax.ShapeDtypeStruct(r.shape, r.dtype) for r in rows[:ndr]]
            + [jax.ShapeDtypeStruct(p.shape, f32) for p in params[:ndp]],
            grid=grid,
            in_specs=[rspec(w) for w in rw] + [pspec(s) for s in pshape] + [rspec(w) for w in ow],
            out_specs=[rspec(w) for w in rw[:ndr]] + [pspec(s) for s in pshape[:ndp]],
            compiler_params=_cparams(("parallel", "arbitrary")),
            name=name + "_bwd",
        )(*args, *douts)

    @jax.custom_vjp
    def op(*args):
        return tuple(call_fwd(*args))

    def fwd(*args):
        return tuple(call_fwd(*args)), args

    def bwd(args, douts):
        res = call_bwd(args, douts)
        drs, dps = res[:ndr], res[ndr:]
        out = list(drs) + [jnp.zeros_like(a) for a in args[ndr:nr]]
        out += [dp.astype(p.dtype) for dp, p in zip(dps, args[nr:nr + ndp])]
        out += [jnp.zeros_like(a) for a in args[nr + ndp:]]
        return tuple(out)

    op.defvjp(fwd, bwd)
    return op(*rows, *params)


def _silu(x):
    return x * (1.0 / (1.0 + jnp.exp(-x)))


def _softplus(x):
    return jnp.maximum(x, 0.0) + jnp.log(1.0 + jnp.exp(-jnp.abs(x)))


def _gelu_tanh(x):
    return 0.5 * x * (1.0 + jnp.tanh(math.sqrt(2.0 / math.pi) * (x + 0.044715 * (x * x * x))))


def _rms_fn(x, g):
    return x * lax.rsqrt(jnp.mean(x * x, axis=-1, keepdims=True) + EPS) * g


def _rms_bwd(x, g, dy):
    r = lax.rsqrt(jnp.mean(x * x, axis=-1, keepdims=True) + EPS)
    xh = x * r
    dxh = dy * g
    dx = r * (dxh - xh * jnp.mean(dxh * xh, axis=-1, keepdims=True))
    return dx, jnp.sum(dy * xh, axis=0, keepdims=True)


def rms(x, g, out_dtype):
    def bwd_fn(rv, pv, dos):
        dx, dg = _rms_bwd(rv[0], pv[0], dos[0])
        return (dx,), (dg,)

    return rowwise(lambda x, g: (_rms_fn(x, g),), "rms", [x], [g.reshape(1, -1)], [out_dtype], bwd_fn=bwd_fn)[0]


def rms_residual_norm(m, g, x, g_next):
    def fn(m, x, g, gn):
        xn = x + _rms_fn(m, g)
        return xn, _rms_fn(xn, gn)

    def bwd_fn(rv, pv, dos):
        (m, x), (g, gn), (dxn, dhn) = rv, pv, dos
        xn = x + _rms_fn(m, g)
        d_from_norm, dgn = _rms_bwd(xn, gn, dhn)
        dxn = dxn + d_from_norm
        dm, dg = _rms_bwd(m, g, dxn)
        return (dm, dxn), (dg, dgn)

    return rowwise(fn, "rms_res_norm", [m, x], [g.reshape(1, -1), g_next.reshape(1, -1)], [f32, MXU_DTYPE], bwd_fn=bwd_fn)


def rms_residual(m, g, x):
    def bwd_fn(rv, pv, dos):
        dm, dg = _rms_bwd(rv[0], pv[0], dos[0])
        return (dm, dos[0]), (dg,)

    return rowwise(lambda m, x, g: (x + _rms_fn(m, g),), "rms_res", [m, x], [g.reshape(1, -1)], [f32], bwd_fn=bwd_fn)[0]


def loss_op(y, tgt):
    S, D = y.shape
    tm = _row_tile(S, 4 * D * 4)

    def call_fwd(y, tgt):
        def body(y_ref, t_ref, o_ref):
            @pl.when(pl.program_id(0) == 0)
            def _():
                o_ref[...] = jnp.zeros_like(o_ref)

            e = y_ref[...] - t_ref[...]
            o_ref[...] += 0.5 * jnp.sum(jnp.mean(e * e, axis=-1, keepdims=True))

        out = pl.pallas_call(
            body,
            out_shape=jax.ShapeDtypeStruct((8, LANES), f32),
            grid=(S // tm,),
            in_specs=[pl.BlockSpec((tm, D), lambda i: (i, 0))] * 2,
            out_specs=pl.BlockSpec((8, LANES), lambda i: (0, 0)),
            compiler_params=_cparams(("arbitrary",)),
            name="loss_fwd",
        )(y, tgt)
        return out[0, 0]

    def call_bwd(y, tgt, g):
        def body(y_ref, t_ref, g_ref, o_ref):
            o_ref[...] = (y_ref[...] - t_ref[...]) * (g_ref[...] * (1.0 / D))

        return pl.pallas_call(
            body,
            out_shape=jax.ShapeDtypeStruct((S, D), f32),
            grid=(S // tm,),
            in_specs=[pl.BlockSpec((tm, D), lambda i: (i, 0))] * 2 + [pl.BlockSpec((1, 1), lambda i: (0, 0))],
            out_specs=pl.BlockSpec((tm, D), lambda i: (i, 0)),
            compiler_params=_cparams(("parallel",)),
            name="loss_bwd",
        )(y, tgt, g.reshape(1, 1).astype(f32))

    @jax.custom_vjp
    def op(y, tgt):
        return call_fwd(y, tgt)

    def fwd(y, tgt):
        return call_fwd(y, tgt), (y, tgt)

    def bwd(res, g):
        y, tgt = res
        return call_bwd(y, tgt, g), jnp.zeros_like(tgt)

    op.defvjp(fwd, bwd)
    return op(y, tgt)


HALO = 8


def _conv_tile(S, R):
    def ext(ref, r0):
        cur = ref[pl.ds(r0, R), :]
        prev = ref[pl.ds(pl.multiple_of(jnp.maximum(r0 - HALO, 0), HALO), HALO), :]
        nxt = ref[pl.ds(pl.multiple_of(jnp.minimum(r0 + R, S - HALO), HALO), HALO), :]
        prev = jnp.where(r0 > 0, prev, 0.0)
        nxt = jnp.where(r0 + R < S, nxt, 0.0)
        return jnp.concatenate([prev, cur, nxt], axis=0)

    return ext


def _shift_rows(e, k, R):
    n = e.shape[0]
    if k == 0:
        return e[HALO:HALO + R]
    return pltpu.roll(e, (-k) % n, 0)[HALO:HALO + R]


def _silu_bwd(us, dy):
    u, = us
    s = 1.0 / (1.0 + jnp.exp(-u))
    return (dy * (s * (1.0 + u * (1.0 - s))),)


def _geglu(g, v):
    return _gelu_tanh(g) * v


def _geglu_bwd(us, dy):
    g, v = us
    c = math.sqrt(2.0 / math.pi)
    t = jnp.tanh(c * (g + 0.044715 * (g * g * g)))
    half = 0.5 * (1.0 + t)
    dgelu = half + 0.5 * g * (1.0 - t * t) * (c * (1.0 + 3.0 * 0.044715 * (g * g)))
    return dy * v * dgelu, dy * (g * half)


def mm_conv_act(h, ws, cws, cbs, act, act_bwd, out_dtype, name):
    n = len(ws)
    S = h.shape[0]
    C = ws[0].shape[1]
    W = cws[0].shape[0]
    pad = W // 2
    bw = _pick(C, (LANES,))
    R = _pick(S, (256, 128, 64, 32, 16, 8))
    nt = S // R
    ext = _conv_tile(S, R)
    col = lambda rows: pl.BlockSpec((rows, bw), lambda j: (0, j))

    def conv(e, wv, bv):
        acc = bv + wv[pad] * e[HALO:HALO + R]
        for j in range(W):
            if j != pad:
                acc = acc + wv[j] * _shift_rows(e, j - pad, R)
        return acc

    def call_fwd(xs, cws, cbs):
        def body(*refs):
            x_refs, w_refs, b_refs, y_ref = refs[:n], refs[n:2 * n], refs[2 * n:3 * n], refs[3 * n]
            wvs = [[w[j:j + 1, :] for j in range(W)] for w in w_refs]
            bvs = [b[...] for b in b_refs]

            def tile(i, c):
                r0 = pl.multiple_of(i * R, R)
                us = [conv(ext(x, r0), wv, bv) for x, wv, bv in zip(x_refs, wvs, bvs)]
                y_ref[pl.ds(r0, R), :] = act(*us).astype(y_ref.dtype)
                return c

            lax.fori_loop(0, nt, tile, 0)

        return pl.pallas_call(
            body,
            out_shape=jax.ShapeDtypeStruct((S, C), out_dtype),
            grid=(C // bw,),
            in_specs=[col(S)] * n + [col(W)] * n + [col(1)] * n,
            out_specs=col(S),
            compiler_params=_cparams(("parallel",)),
            name=name + "_fwd",
        )(*xs, *cws, *cbs)

    def call_bwd(xs, cws, cbs, dy):
        def body(*refs):
            x_refs, w_refs, b_refs, dy_ref = refs[:n], refs[n:2 * n], refs[2 * n:3 * n], refs[3 * n]
            dx_refs, dw_refs, db_refs = refs[3 * n + 1:4 * n + 1], refs[4 * n + 1:5 * n + 1], refs[5 * n + 1:6 * n + 1]
            du_scr = refs[6 * n + 1:]
            wvs = [[w[j:j + 1, :] for j in range(W)] for w in w_refs]
            bvs = [b[...] for b in b_refs]
            zero = jnp.zeros((1, bw), f32)

            def first(i, dbs):
                r0 = pl.multiple_of(i * R, R)
                us = [conv(ext(x, r0), wv, bv) for x, wv, bv in zip(x_refs, wvs, bvs)]
                dus = act_bwd(us, dy_ref[pl.ds(r0, R), :].astype(f32))
                for scr, du in zip(du_scr, dus):
                    scr[pl.ds(r0, R), :] = du
                return tuple(db + jnp.sum(du, axis=0, keepdims=True) for db, du in zip(dbs, dus))

            dbs = lax.fori_loop(0, nt, first, tuple(zero for _ in range(n)))

            def second(i, dws):
                r0 = pl.multiple_of(i * R, R)
                new = []
                for x, scr, dx, wv, dw in zip(x_refs, du_scr, dx_refs, wvs, dws):
                    ex, ed = ext(x, r0), ext(scr, r0)
                    d0 = ed[HALO:HALO + R]
                    acc = jnp.zeros((R, bw), f32)
                    row = []
                    for j in range(W):
                        acc = acc + wv[j] * _shift_rows(ed, pad - j, R)
                        row.append(dw[j] + jnp.sum(d0 * _shift_rows(ex, j - pad, R), axis=0, keepdims=True))
                    dx[pl.ds(r0, R), :] = acc.astype(dx.dtype)
                    new.append(tuple(row))
                return tuple(new)

            dws = lax.fori_loop(0, nt, second, tuple(tuple(zero for _ in range(W)) for _ in range(n)))
            for dw_ref, db_ref, dw, db in zip(dw_refs, db_refs, dws, dbs):
                dw_ref[...] = jnp.zeros_like(dw_ref)
                for j in range(W):
                    dw_ref[j:j + 1, :] = dw[j]
                db_ref[...] = db

        return pl.pallas_call(
            body,
            out_shape=[jax.ShapeDtypeStruct((S, C), MXU_DTYPE)] * n + [jax.ShapeDtypeStruct((8, C), f32)] * n
            + [jax.ShapeDtypeStruct((1, C), f32)] * n,
            grid=(C // bw,),
            in_specs=[col(S)] * n + [col(W)] * n + [col(1)] * n + [col(S)],
            out_specs=[col(S)] * n + [col(8)] * n + [col(1)] * n,
            scratch_shapes=[pltpu.VMEM((S, bw), f32)] * n,
            compiler_params=_cparams(("parallel",)),
            name=name + "_bwd",
        )(*xs, *cws, *cbs, dy)

    @jax.custom_vjp
    def op(h, ws, cws, cbs):
        return call_fwd([_mm_call(h, w, "nn", f32) for w in ws], cws, cbs)

    def fwd(h, ws, cws, cbs):
        xs = [_mm_call(h, w, "nn", f32) for w in ws]
        return call_fwd(xs, cws, cbs), (h, ws, xs, cws, cbs)

    def bwd(res, dy):
        h, ws, xs, cws, cbs = res
        out = call_bwd(xs, cws, cbs, dy)
        dxs, dcws, dcbs = out[:n], out[n:2 * n], out[2 * n:]
        dh = _mm_call(dxs[0], ws[0], "nt", h.dtype)
        for dx, w in zip(dxs[1:], ws[1:]):
            dh = dh + _mm_call(dx, w, "nt", h.dtype)
        dws = tuple(_mm_call(h, dx, "tn", w.dtype) for dx, w in zip(dxs, ws))
        return dh, dws, tuple(d[:W] for d in dcws), tuple(dcbs)

    op.defvjp(fwd, bwd)
    return op(h, tuple(ws), tuple(cws), tuple(b.reshape(1, C) for b in cbs))


@jax.custom_vjp
def _masked_decay(cs_col, cs_row, mask01):
    return jnp.where(mask01 > 0, jnp.exp(cs_col - cs_row), 0.0)


def _masked_decay_fwd(cs_col, cs_row, mask01):
    d = jnp.where(mask01 > 0, jnp.exp(cs_col - cs_row), 0.0)
    return d, (d, mask01)


def _masked_decay_bwd(res, g):
    d, mask01 = res
    t = g * d
    return jnp.sum(t, axis=1, keepdims=True), -jnp.sum(t, axis=0, keepdims=True), jnp.zeros_like(mask01)


_masked_decay.defvjp(_masked_decay_fwd, _masked_decay_bwd)


def _scan_chunk(qs, ks, xs, cs_tok, dt_tok, hs, *, rev, incl, nsub):
    nb = len(xs)
    L, N = qs[0].shape
    W = xs[0].shape[1]
    Hg = cs_tok.shape[1]
    nh = W // HEAD_W
    shared = len(qs) == 1
    t = lax.broadcasted_iota(jnp.int32, (L, L), 0)
    l = lax.broadcasted_iota(jnp.int32, (L, L), 1)
    if rev:
        mask = (l >= t) if incl else (l > t)
    else:
        mask = (l <= t) if incl else (l < t)
    mask01 = mask.astype(f32)
    lane_a = lax.broadcasted_iota(jnp.int32, cs_tok.shape, 1)
    row_a = lax.broadcasted_iota(jnp.int32, (Hg, L), 0)
    last = lax.broadcasted_iota(jnp.int32, (1, L), 1) == (0 if rev else L - 1)
    vhead = lax.broadcasted_iota(jnp.int32, (1, W), 1) // HEAD_W
    qhead = lax.broadcasted_iota(jnp.int32, (1, N), 1) // (N // nsub)
    cs_rows = lax.dot_general(cs_tok, (t == l).astype(f32), _DIMS["tn"], precision=lax.Precision.HIGHEST,
                              preferred_element_type=f32)

    def by_head(vals):
        if len(vals) == 2:
            return jnp.where(vhead == 0, vals[0], vals[1])
        return sum(jnp.where(vhead == i, v, 0.0) for i, v in enumerate(vals))

    decay, lam_e, tau_e, gam_e, dt_e = [], [], [], [], []
    for b in range(nb):
        cs_cols, tots, dt_cols = [], [], []
        for i in range(nh):
            head = b * nh + i
            cs_col = jnp.sum(jnp.where(lane_a == head, cs_tok, 0.0), axis=1, keepdims=True)
            cs_row = jnp.sum(jnp.where(row_a == head, cs_rows, 0.0), axis=0, keepdims=True)
            tots.append(jnp.sum(jnp.where(last, cs_row, 0.0), axis=1, keepdims=True))
            decay.append(_masked_decay(cs_col, cs_row, mask01))
            cs_cols.append(cs_col)
            if dt_tok is not None:
                dt_cols.append(jnp.sum(jnp.where(lane_a == head, dt_tok, 0.0), axis=1, keepdims=True))
        cs_e, tot_e = by_head(cs_cols), by_head(tots)
        lam_e.append(jnp.exp(cs_e))
        tau_e.append(jnp.exp(tot_e - cs_e))
        gam_e.append(jnp.exp(tot_e))
        if dt_tok is not None:
            dt_e.append(by_head(dt_cols))
    vs = [x if dt_tok is None else x * dt_e[b] for b, x in enumerate(xs)]
    qk = lambda b: (qs[0], ks[0]) if shared else (qs[b], ks[b])
    if nsub == 1:
        scores = [bdot(qs[0], ks[0], "nt")] if shared else [bdot(*qk(b), "nt") for b in range(nb)]
        score = lambda b, i: scores[0 if shared else b]
    else:
        scores = [[bdot(jnp.where(qhead == i, qk(b)[0], 0.0), qk(b)[1], "nt") for i in range(nh)] for b in range(nb)]
        score = lambda b, i: scores[b][i]
    ys = [lam_e[b] * bdot(qk(b)[0], hs[b], "nn")
          + by_head([bdot(score(b, i) * decay[b * nh + i], vs[b], "nn") for i in range(nh)]) for b in range(nb)]
    hns = [gam_e[b] * hs[b] + bdot(qk(b)[1], tau_e[b] * vs[b], "tn") for b in range(nb)]
    if nsub > 1:
        nhead = lax.broadcasted_iota(jnp.int32, (N, W), 0) // (N // nsub)
        keep = nhead == lax.broadcasted_iota(jnp.int32, (N, W), 1) // HEAD_W
        hns = [jnp.where(keep, hn, 0.0) for hn in hns]
    return ys, hns


def chunk_cumsum(a_tok, rev):
    G, S, Hg = a_tok.shape
    L = CHUNK
    CB = _pick(S // L, (16, 8, 4, 2))

    def call(a, rev):
        def body(a_ref, o_ref):
            t = lax.broadcasted_iota(jnp.int32, (L, L), 0)
            l = lax.broadcasted_iota(jnp.int32, (L, L), 1)
            tri = ((l >= t) if rev else (l <= t)).astype(f32)
            for j in range(CB):
                o_ref[0, j * L:(j + 1) * L, :] = _exact_dot(tri, a_ref[0, j * L:(j + 1) * L, :])

        spec = pl.BlockSpec((1, CB * L, Hg), lambda g, c: (g, c, 0))
        return pl.pallas_call(
            body,
            out_shape=jax.ShapeDtypeStruct((G, S, Hg), f32),
            grid=(G, S // (CB * L)),
            in_specs=[spec],
            out_specs=spec,
            compiler_params=_cparams(("parallel", "parallel")),
            name="chunk_cumsum",
        )(a)

    @jax.custom_vjp
    def op(a):
        return call(a, rev)

    def fwd(a):
        return call(a, rev), None

    def bwd(_, g):
        return (call(g, not rev),)

    op.defvjp(fwd, bwd)
    return op(a_tok)


def scan_op(q, k, x, a_tok, dt_tok, *, rev, incl, nsub):
    S = q.shape[0]
    G, _, Hg = a_tok.shape
    N = q.shape[1] // G
    Vw = x.shape[1] // G
    L = CHUNK
    nc = S // L
    use_dt = dt_tok is not None
    PW = min(Vw, LANES)
    chunk = functools.partial(_scan_chunk, rev=rev, incl=incl, nsub=nsub)
    cols = [slice(p * PW, (p + 1) * PW) for p in range(Vw // PW)]
    own_qk = nsub > 1
    NB = PW if own_qk else N

    def order(c, backward):
        return (nc - 1 - c) if (rev != backward) else c

    def specs(backward):
        qs = pl.BlockSpec((L, N), lambda g, c: (order(c, backward), g))
        xs = pl.BlockSpec((L, Vw), lambda g, c: (order(c, backward), g))
        as_ = pl.BlockSpec((1, L, Hg), lambda g, c: (g, order(c, backward), 0))
        hs = pl.BlockSpec((1, 1, NB, Vw), lambda g, c: (g, order(c, backward), 0, 0))
        return qs, xs, as_, hs

    def call_fwd(q, k, x, a_tok, dt_tok, y_prev=None):
        qs, xs, as_, hs = specs(False)
        n_in = 4 + use_dt + (y_prev is not None)

        def body(*refs):
            q_ref, k_ref, x_ref, a_ref = refs[:4]
            dt_ref = refs[4] if use_dt else None
            yp_ref = refs[n_in - 1] if y_prev is not None else None
            y_ref, hs_ref, h_scr = refs[n_in:]

            @pl.when(pl.program_id(1) == 0)
            def _():
                h_scr[...] = jnp.zeros_like(h_scr)

            hs_ref[0, 0] = h_scr[...]
            q, k, a, dt = q_ref[...], k_ref[...], a_ref[0], dt_ref[0] if use_dt else None
            qs, ks = ([q[:, c] for c in cols], [k[:, c] for c in cols]) if own_qk else ([q], [k])
            ys, hns = chunk(qs, ks, [x_ref[:, c] for c in cols], a, dt, [h_scr[:, c] for c in cols])
            for c, y, hn in zip(cols, ys, hns):
                y_ref[:, c] = y if yp_ref is None else y + yp_ref[:, c]
                h_scr[:, c] = hn

        ins = [q, k, x, a_tok] + ([dt_tok] if use_dt else []) + ([y_prev] if y_prev is not None else [])
        return pl.pallas_call(
            body,
            out_shape=[jax.ShapeDtypeStruct((S, G * Vw), f32), jax.ShapeDtypeStruct((G, nc, NB, Vw), f32)],
            grid=(G, nc),
            in_specs=[qs, qs, xs, as_] + ([as_] if use_dt else []) + ([xs] if y_prev is not None else []),
            out_specs=[xs, hs],
            scratch_shapes=[pltpu.VMEM((NB, Vw), f32)],
            compiler_params=_cparams(("parallel", "arbitrary")),
            name="scan_fwd",
        )(*ins)

    def call_bwd(q, k, x, a_tok, dt_tok, hsave, dy, acc=None):
        qs, xs, as_, hs = specs(True)
        n_in = 6 + use_dt + (3 if acc is not None else 0)

        def body(*refs):
            q_ref, k_ref, x_ref, a_ref = refs[:4]
            dt_ref = refs[4] if use_dt else None
            hs_ref, dy_ref = refs[4 + use_dt], refs[5 + use_dt]
            acc_refs = refs[n_in - 3:n_in] if acc is not None else None
            dq_ref, dk_ref, dx_ref, da_ref = refs[n_in:n_in + 4]
            ddt_ref = refs[n_in + 4] if use_dt else None
            dh_scr = refs[-1]

            @pl.when(pl.program_id(1) == 0)
            def _():
                dh_scr[...] = jnp.zeros_like(dh_scr)

            q, k, a = q_ref[...].astype(f32), k_ref[...].astype(f32), a_ref[0]
            qs, ks = ([q[:, c] for c in cols], [k[:, c] for c in cols]) if own_qk else ([q], [k])
            xs, hs_in = [x_ref[:, c] for c in cols], [hs_ref[0, 0, :, c] for c in cols]
            if use_dt:
                _, vjp = jax.vjp(chunk, qs, ks, xs, a, dt_ref[0], hs_in)
            else:
                _, vjp = jax.vjp(lambda qs, ks, xs, a, hs: chunk(qs, ks, xs, a, None, hs), qs, ks, xs, a, hs_in)
            cts = vjp(([dy_ref[:, c] for c in cols], [dh_scr[:, c] for c in cols]))
            dqs, dks, dxs, da, dhs = cts[0], cts[1], cts[2], cts[3], cts[-1]
            if acc is not None:
                dq_acc, dk_acc = acc_refs[0][...].astype(f32), acc_refs[1][...].astype(f32)
            if own_qk:
                for b, c in enumerate(cols):
                    dq_ref[:, c] = (dqs[b] if acc is None else dqs[b] + dq_acc[:, c]).astype(dq_ref.dtype)
                    dk_ref[:, c] = (dks[b] if acc is None else dks[b] + dk_acc[:, c]).astype(dk_ref.dtype)
            else:
                dq_ref[...] = (dqs[0] if acc is None else dqs[0] + dq_acc).astype(dq_ref.dtype)
                dk_ref[...] = (dks[0] if acc is None else dks[0] + dk_acc).astype(dk_ref.dtype)
            for b, c in enumerate(cols):
                dx_ref[:, c] = dxs[b] if acc is None else dxs[b] + acc_refs[2][:, c]
                dh_scr[:, c] = dhs[b]
            da_ref[0] = da
            if use_dt:
                ddt_ref[0] = cts[4]

        ins = [q, k, x, a_tok] + ([dt_tok] if use_dt else []) + [hsave, dy] + (list(acc) if acc is not None else [])
        a_shape = jax.ShapeDtypeStruct(a_tok.shape, f32)
        return pl.pallas_call(
            body,
            out_shape=[jax.ShapeDtypeStruct(q.shape, q.dtype), jax.ShapeDtypeStruct(k.shape, k.dtype),
                       jax.ShapeDtypeStruct(x.shape, f32), a_shape] + ([a_shape] if use_dt else []),
            grid=(G, nc),
            in_specs=[qs, qs, xs, as_] + ([as_] if use_dt else []) + [hs, xs] + ([qs, qs, xs] if acc is not None else []),
            out_specs=[qs, qs, xs, as_] + ([as_] if use_dt else []),
            scratch_shapes=[pltpu.VMEM((NB, Vw), f32)],
            compiler_params=_cparams(("parallel", "arbitrary")),
            name="scan_bwd",
        )(*ins)

    return call_fwd, call_bwd


def bidir_scan(q, k, x, a_f, a_b, dt_f, dt_b, *, nsub):
    use_dt = dt_f is not None
    a_f, a_b = chunk_cumsum(a_f, False), chunk_cumsum(a_b, True)
    fwd_f, bwd_f = scan_op(q, k, x, a_f, dt_f, rev=False, incl=True, nsub=nsub)
    fwd_b, bwd_b = scan_op(q, k, x, a_b, dt_b, rev=True, incl=False, nsub=nsub)

    def run(q, k, x, a_f, a_b, dt_f, dt_b):
        y_f, hs_f = fwd_f(q, k, x, a_f, dt_f)
        y, hs_b = fwd_b(q, k, x, a_b, dt_b, y_prev=y_f)
        return y, (hs_f, hs_b)

    def grads(q, k, x, a_f, a_b, dt_f, dt_b, hs, dy):
        first = bwd_f(q, k, x, a_f, dt_f, hs[0], dy)
        both = bwd_b(q, k, x, a_b, dt_b, hs[1], dy, acc=first[:3])
        return both[0], both[1], both[2], first[3], both[3], (first[4] if use_dt else None), (both[4] if use_dt else None)

    if use_dt:
        @jax.custom_vjp
        def op(q, k, x, a_f, a_b, dt_f, dt_b):
            return run(q, k, x, a_f, a_b, dt_f, dt_b)[0]

        def fwd(q, k, x, a_f, a_b, dt_f, dt_b):
            y, hs = run(q, k, x, a_f, a_b, dt_f, dt_b)
            return y, (q, k, x, a_f, a_b, dt_f, dt_b, hs)

        def bwd(res, dy):
            return grads(*res, dy)

        op.defvjp(fwd, bwd)
        return op(q, k, x, a_f, a_b, dt_f, dt_b)

    @jax.custom_vjp
    def op(q, k, x, a_f, a_b):
        return run(q, k, x, a_f, a_b, None, None)[0]

    def fwd(q, k, x, a_f, a_b):
        y, hs = run(q, k, x, a_f, a_b, None, None)
        return y, (q, k, x, a_f, a_b, hs)

    def bwd(res, dy):
        q, k, x, a_f, a_b, hs = res
        return grads(q, k, x, a_f, a_b, None, None, hs, dy)[:5]

    op.defvjp(fwd, bwd)
    return op(q, k, x, a_f, a_b)


def _swap_halves(x, dh):
    W = x.shape[1]
    lane = lax.broadcasted_iota(jnp.int32, (1, W), 1) % dh
    return jnp.where(lane < dh // 2, pltpu.roll(x, W - dh // 2, 1), pltpu.roll(x, dh // 2, 1))


def rotary(rq, rk, cos_t, sin_t):
    scale = RET_DH ** -0.5

    def fn(rq, rk, c, s):
        return rq * c + _swap_halves(rq, RET_DH) * s, (rk * c + _swap_halves(rk, RET_DH) * s) * scale

    def bwd_fn(rv, pv, dos):
        _, _, c, s = rv
        dq, dk = dos
        dk = dk * scale
        return (dq * c + _swap_halves(dq * s, RET_DH), dk * c + _swap_halves(dk * s, RET_DH)), ()

    return rowwise(fn, "rotary", [rq, rk, cos_t, sin_t], [], [MXU_DTYPE, MXU_DTYPE], n_diff_rows=2, bwd_fn=bwd_fn)


def _rope_tables(S, width):
    half = RET_DH // 2
    inv = 1.0 / (ROPE_BASE ** (jnp.arange(half, dtype=f32) / half))
    ang = jnp.arange(S, dtype=f32)[:, None] * inv[None, :]
    cos, sin = jnp.cos(ang), jnp.sin(ang)
    reps = width // RET_DH
    return jnp.tile(jnp.concatenate([cos, cos], axis=1), (1, reps)), jnp.tile(jnp.concatenate([-sin, sin], axis=1), (1, reps))


def _exact_dot(x, m):
    return jnp.dot(x, m, precision=lax.Precision.HIGHEST, preferred_element_type=f32)


def ret_post(y, rg, gn_g):
    W = y.shape[1]
    idx = np.arange(W) // RET_DH
    avg = jnp.asarray((idx[:, None] == idx[None, :]).astype(np.float32) / RET_DH)

    def fn(y, rg, g, avg):
        mu = _exact_dot(y, avg)
        d = y - mu
        var = _exact_dot(d * d, avg)
        return (_silu(rg) * (d * lax.rsqrt(var + EPS) * g),)

    return rowwise(fn, "ret_post", [y, rg], [gn_g.reshape(1, -1), avg], [MXU_DTYPE], n_diff_params=1)[0]


def _na_bias(rpb, win_r):
    H = rpb.shape[0]
    qc = np.arange(GRID_W)[:, None]
    kc = np.arange(GRID_W)[None, :]
    cstart = np.clip(qc - NA_WIN_C // 2, 0, GRID_W - NA_WIN_C)
    valid = (kc >= cstart) & (kc < cstart + NA_WIN_C)
    dc = np.clip(kc - qc, -(NA_WIN_C - 1), NA_WIN_C - 1) + (NA_WIN_C - 1)
    onehot = (dc[None] == np.arange(2 * NA_WIN_C - 1)[:, None, None]).astype(np.float32)
    t1 = jnp.einsum("hrd,dqk->hrqk", rpb.astype(f32), jnp.asarray(onehot), precision=lax.Precision.HIGHEST)
    per_delta = [t1[:, NA_WIN_R - 1 - d:NA_WIN_R - 1 - d + win_r] for d in range(win_r)]
    b = jnp.stack(per_delta, axis=1)
    b = jnp.where(jnp.asarray(valid)[None, None, None], b, NEG_INF)
    return jnp.transpose(b, (0, 1, 3, 2, 4)).reshape(H, win_r, GRID_W, win_r * GRID_W)


def _na_rows(rows):
    lane = lax.broadcasted_iota(jnp.int32, (1, rows[0][0].shape[1]), 1) // NA_DH
    scale = NA_DH ** -0.5
    ss = [[_bdot_raw(jnp.where(lane == i, q, 0.0) * scale, kw, "nt") + b for i, b in enumerate(bs)] for q, kw, _, bs in rows]
    es = [[jnp.exp(s - jnp.max(s, axis=1, keepdims=True)) for s in srow] for srow in ss]
    ps = [[e / jnp.sum(e, axis=1, keepdims=True) for e in erow] for erow in es]
    return [_lanes_by_head(lane, [_bdot_raw(p, vw, "nn") for p in prow]) for prow, (_, _, vw, _) in zip(ps, rows)]


def _lanes_by_head(lane, vals):
    if len(vals) == 2:
        return jnp.where(lane == 0, vals[0], vals[1])
    return sum(jnp.where(lane == i, v, 0.0) for i, v in enumerate(vals))


def _na_rows_bwd(rows):
    lane = lax.broadcasted_iota(jnp.int32, (1, rows[0][0].shape[1]), 1) // NA_DH
    scale = NA_DH ** -0.5
    heads = range(len(rows[0][3]))
    qis = [[jnp.where(lane == i, q, 0.0) * scale for i in heads] for q, _, _, _, _ in rows]
    dos = [[jnp.where(lane == i, do, 0.0) for i in heads] for _, _, _, _, do in rows]
    ss = [[_bdot_raw(qi, kw, "nt") + b for qi, b in zip(qrow, bs)] for qrow, (_, kw, _, bs, _) in zip(qis, rows)]
    dps = [[_bdot_raw(doi, vw, "nt") for doi in drow] for drow, (_, _, vw, _, _) in zip(dos, rows)]
    es = [[jnp.exp(s - jnp.max(s, axis=1, keepdims=True)) for s in srow] for srow in ss]
    ps = [[e / jnp.sum(e, axis=1, keepdims=True) for e in erow] for erow in es]
    dss = [[p * (dp - jnp.sum(dp * p, axis=1, keepdims=True)) for p, dp in zip(prow, dprow)] for prow, dprow in zip(ps, dps)]
    out = []
    for qrow, drow, prow, dsrow, (_, kw, _, _, _) in zip(qis, dos, ps, dss, rows):
        dq = _lanes_by_head(lane, [_bdot_raw(dsrow[i], kw, "nn") for i in heads]) * scale
        dk, dv = 0.0, 0.0
        for i in heads:
            dk = dk + _bdot_raw(dsrow[i], qrow[i], "tn")
            dv = dv + _bdot_raw(prow[i], drow[i], "tn")
        out.append((dq, dk, dv, dsrow))
    return out


def na_op(nq, nk, nv, bias):
    S, W = nq.shape
    rows = S // GRID_W
    win_r = bias.shape[1]
    nkeys = win_r * GRID_W
    hp = LANES // NA_DH
    npair = W // LANES
    RB = min(16, rows)
    nrb = rows // RB
    qspec = pl.BlockSpec((RB * GRID_W, LANES), lambda p, r: (r, p))
    kspec = pl.BlockSpec((S, LANES), lambda p, r: (0, p))
    bspec = pl.BlockSpec((hp, win_r, GRID_W, nkeys), lambda p, r: (p, 0, 0, 0))

    def window(r):
        r0 = jnp.clip(r - win_r // 2, 0, rows - win_r)
        return pl.multiple_of(r0 * GRID_W, GRID_W), r - r0

    def call_fwd(nq, nk, nv, bias):
        def body(q_ref, k_ref, v_ref, b_ref, o_ref):
            rb = pl.program_id(1)

            def step(j, c):
                args, q0s = [], []
                for u in range(NA_ROWS_PER_STEP):
                    i = j * NA_ROWS_PER_STEP + u
                    k0, d = window(rb * RB + i)
                    q0 = pl.multiple_of(i * GRID_W, GRID_W)
                    q0s.append(q0)
                    args.append((q_ref[pl.ds(q0, GRID_W), :].astype(f32), k_ref[pl.ds(k0, nkeys), :], v_ref[pl.ds(k0, nkeys), :],
                                 [b_ref[h, pl.ds(d, 1)][0] for h in range(hp)]))
                for q0, o in zip(q0s, _na_rows(args)):
                    o_ref[pl.ds(q0, GRID_W), :] = o.astype(o_ref.dtype)
                return c

            lax.fori_loop(0, RB // NA_ROWS_PER_STEP, step, 0)

        return pl.pallas_call(
            body,
            out_shape=jax.ShapeDtypeStruct((S, W), nq.dtype),
            grid=(npair, nrb),
            in_specs=[qspec, kspec, kspec, bspec],
            out_specs=qspec,
            compiler_params=_cparams(("parallel", "arbitrary")),
            name="na_fwd",
        )(nq, nk, nv, bias)

    def call_bwd(nq, nk, nv, bias, do):
        def body(q_ref, k_ref, v_ref, b_ref, do_ref, dq_ref, dk_ref, dv_ref, db_ref, dk_acc, dv_acc):
            rb = pl.program_id(1)

            @pl.when(rb == 0)
            def _():
                dk_acc[...] = jnp.zeros_like(dk_acc)
                dv_acc[...] = jnp.zeros_like(dv_acc)
                db_ref[...] = jnp.zeros_like(db_ref)

            def step(j, c):
                args, spots = [], []
                for u in range(NA_ROWS_PER_STEP):
                    i = j * NA_ROWS_PER_STEP + u
                    k0, d = window(rb * RB + i)
                    q0 = pl.multiple_of(i * GRID_W, GRID_W)
                    spots.append((q0, k0, d))
                    args.append((q_ref[pl.ds(q0, GRID_W), :].astype(f32), k_ref[pl.ds(k0, nkeys), :], v_ref[pl.ds(k0, nkeys), :],
                                 [b_ref[h, pl.ds(d, 1)][0] for h in range(hp)], do_ref[pl.ds(q0, GRID_W), :].astype(f32)))
                for (q0, k0, d), (dq, dk, dv, dbs) in zip(spots, _na_rows_bwd(args)):
                    dq_ref[pl.ds(q0, GRID_W), :] = dq.astype(dq_ref.dtype)
                    dk_acc[pl.ds(k0, nkeys), :] += dk
                    dv_acc[pl.ds(k0, nkeys), :] += dv
                    for h in range(hp):
                        db_ref[h, pl.ds(d, 1)] += dbs[h][None]
                return c

            lax.fori_loop(0, RB // NA_ROWS_PER_STEP, step, 0)

            @pl.when(rb == nrb - 1)
            def _():
                dk_ref[...] = dk_acc[...].astype(dk_ref.dtype)
                dv_ref[...] = dv_acc[...].astype(dv_ref.dtype)

        return pl.pallas_call(
            body,
            out_shape=[jax.ShapeDtypeStruct((S, W), nq.dtype), jax.ShapeDtypeStruct((S, W), nk.dtype),
                       jax.ShapeDtypeStruct((S, W), nv.dtype), jax.ShapeDtypeStruct(bias.shape, f32)],
            grid=(npair, nrb),
            in_specs=[qspec, kspec, kspec, bspec, qspec],
            out_specs=[qspec, kspec, kspec, bspec],
            scratch_shapes=[pltpu.VMEM((S, LANES), f32), pltpu.VMEM((S, LANES), f32)],
            compiler_params=_cparams(("parallel", "arbitrary")),
            name="na_bwd",
        )(nq, nk, nv, bias, do)

    @jax.custom_vjp
    def op(nq, nk, nv, bias):
        return call_fwd(nq, nk, nv, bias)

    def fwd(nq, nk, nv, bias):
        return call_fwd(nq, nk, nv, bias), (nq, nk, nv, bias)

    def bwd(res, do):
        return tuple(call_bwd(*res, do))

    op.defvjp(fwd, bwd)
    return op(nq, nk, nv, bias)


def ssd_dt(dt_raw, dt_bias, a_neg):
    def fn(r, b, a):
        dt = _softplus(r + b)
        return dt, dt * a

    return rowwise(fn, "ssd_dt", [dt_raw], [dt_bias, a_neg], [f32, f32])


def ssd_post(y, xs, z, d_skip_lanes, norm_g, groups):
    def fn(y, xs, z, dsk, g):
        y = (y + xs * dsk) * _silu(z)
        return (y * lax.rsqrt(jnp.mean(y * y, axis=-1, keepdims=True) + EPS) * g,)

    def bwd_fn(rv, pv, dos):
        (y, xs, z), (dsk, g), (do,) = rv, pv, dos
        sg = 1.0 / (1.0 + jnp.exp(-z))
        s, u = z * sg, y + xs * dsk
        dw, dg = _rms_bwd(u * s, g, do)
        du = dw * s
        dz = dw * u * (sg * (1.0 + z * (1.0 - sg)))
        return (du, du * dsk, dz), (jnp.sum(du * xs, axis=0, keepdims=True), dg)

    return rowwise(fn, "ssd_post", [y, xs, z], [d_skip_lanes.reshape(1, -1), norm_g.reshape(1, -1)], [MXU_DTYPE],
                   ncol=groups, bwd_fn=bwd_fn)[0]


def _heads_major(t, groups):
    S = t.shape[0]
    return jnp.transpose(t.reshape(S, groups, -1), (1, 0, 2))


def retention_na_mixer(hn, w_in, decay_logit, gn_g, rpb, w_out, tables):
    S = hn.shape[0]
    R = RET_HEADS * RET_DH
    NW = NA_HEADS * NA_DH
    cols = lambda a, b: w_in[:, a:b]
    rq, rk, rv, rg = (mm(hn, cols(j * R, (j + 1) * R)) for j in range(4))
    nq, nk, nv = (mm(hn, cols(4 * R + j * NW, 4 * R + (j + 1) * NW), out_dtype=MXU_DTYPE) for j in range(3))
    qr, kr = rotary(rq, rk, *tables)
    log_gamma = -_softplus(-decay_logit.astype(f32))
    hp = LANES // RET_DH
    hpad = -(-RET_HEADS // 8) * 8
    pad8 = lambda a: jnp.pad(a.reshape(1, 1, RET_HEADS), ((0, 0), (0, 0), (0, hpad - RET_HEADS)))
    a_f = jnp.broadcast_to(pad8(log_gamma[0]), (1, S, hpad))
    a_b = jnp.broadcast_to(pad8(log_gamma[1]), (1, S, hpad))
    ret = ret_post(bidir_scan(qr, kr, rv, a_f, a_b, None, None, nsub=hp), rg, gn_g)
    rows = S // GRID_W
    nao = na_op(nq, nk, nv, _na_bias(rpb, min(NA_WIN_R, rows)))
    return mm(ret, w_out[:R]) + mm(nao, w_out[R:])


def ssd_mixer(hn, w_in, conv_w, conv_b, dt_bias, a_log, d_skip, norm_g, w_out):
    heads = d_skip.shape[0]
    inner = heads * SSD_HEADDIM
    gs = SSD_GROUPS * SSD_STATE
    o_x, o_b, o_c, o_dt = inner, 2 * inner, 2 * inner + gs, 2 * inner + 2 * gs
    z = mm(hn, w_in[:, :inner])
    dt_raw = mm(hn, w_in[:, o_dt:])
    xs, bm, cm = (mm_conv_act(hn, [w_in[:, a:b]], [conv_w[:, a - inner:b - inner]], [conv_b[a - inner:b - inner]],
                              _silu, _silu_bwd, f32, "conv_silu") for a, b in ((o_x, o_b), (o_b, o_c), (o_c, o_dt)))
    a_neg = -jnp.exp(a_log.astype(f32)).reshape(1, -1)
    dt, la = ssd_dt(dt_raw, dt_bias.astype(f32).reshape(1, -1), a_neg)
    dt_f, dt_b = _heads_major(dt[:, :heads], SSD_GROUPS), _heads_major(dt[:, heads:], SSD_GROUPS)
    la_f, la_b = _heads_major(la[:, :heads], SSD_GROUPS), _heads_major(la[:, heads:], SSD_GROUPS)
    y = bidir_scan(cm, bm, xs, la_f, la_b, dt_f, dt_b, nsub=1)
    y = ssd_post(y, xs, z, jnp.repeat(d_skip.astype(f32), SSD_HEADDIM), norm_g, SSD_GROUPS)
    return mm(y, w_out)


def conv_geglu_ffn(hf, w_up, conv_w, conv_b, w_down):
    F = w_down.shape[0]
    a = mm_conv_act(hf, [w_up[:, :F], w_up[:, F:]], [conv_w[:, :F], conv_w[:, F:]], [conv_b[:F], conv_b[F:]],
                    _geglu, _geglu_bwd, MXU_DTYPE, "conv_geglu")
    return mm(a, w_down)


def model_loss(x, tgt, big, small, rep):
    S = x.shape[0]
    depth = rep["norm_mix_pre"].shape[0]
    tables = _rope_tables(S, RET_HEADS * RET_DH)
    hn = rms(x, rep["norm_mix_pre"][0], MXU_DTYPE)
    for layer in range(depth):
        i = layer // 2
        if layer % 2 == 0:
            m = retention_na_mixer(hn, big["ab_w_in"][i], rep["ab_ret_decay_logit"][i], rep["ab_ret_gn_g"][i],
                                   rep["ab_na_rpb"][i], big["ab_w_out"][i], tables)
        else:
            m = ssd_mixer(hn, big["c_w_in"][i], small["c_conv_w"][i], small["c_conv_b"][i], rep["c_dt_bias"][i],
                          rep["c_a_log"][i], rep["c_d_skip"][i], small["c_norm_g"][i], big["c_w_out"][i])
        x, hf = rms_residual_norm(m, rep["norm_mix_post"][layer], x, rep["norm_ffn_pre"][layer])
        f = conv_geglu_ffn(hf, big["ffn_w_up"][layer], small["ffn_conv_w"][layer], rep["ffn_conv_b"][layer],
                           big["ffn_w_down"][layer])
        if layer + 1 < depth:
            x, hn = rms_residual_norm(f, rep["norm_ffn_post"][layer], x, rep["norm_mix_pre"][layer + 1])
        else:
            x = rms_residual(f, rep["norm_ffn_post"][layer], x)
    return loss_op(x, tgt)


def _mesh_pos():
    return lax.axis_index("x"), lax.axis_index("y"), lax.axis_index("c")


def _any_specs(n):
    return [pl.BlockSpec(memory_space=pl.ANY)] * n


def gather_chips(locals_):
    nbuf = len(locals_)
    CH = COPY_CHUNKS

    def body(*refs):
        x_refs, out_refs, (send_sems, recv_sems) = refs[:nbuf], refs[nbuf:2 * nbuf], refs[2 * nbuf:]
        x, y, c = _mesh_pos()
        my = 2 * x + y
        chips = [(1 - x, y), (x, 1 - y), (1 - x, 1 - y)]
        plans = []
        for a, (x_ref, out_ref) in enumerate(zip(x_refs, out_refs)):
            half = x_ref.shape[0] // 2
            q = half // CH

            def piece(ref, h, j, half=half, q=q):
                return ref.at[pl.ds(pl.multiple_of(h * half + j * q, PACK_ALIGN), q), :]

            def copy(k, src, chip, h, j, to, out_ref=out_ref, piece=piece, base=a * 6 * CH):
                return pltpu.make_async_remote_copy(src_ref=src, dst_ref=piece(out_ref.at[chip], h, j),
                                                    send_sem=send_sems.at[base + k], recv_sem=recv_sems.at[base + k],
                                                    device_id=to, device_id_type=pl.DeviceIdType.MESH)

            plans.append((x_ref, out_ref, piece, copy))

        first, passed = [], []
        for x_ref, out_ref, piece, copy in plans:
            first.append([[copy(k * CH + j, piece(x_ref, c, j), my, c, j, (cx, cy, c)) for j in range(CH)]
                          for k, (cx, cy) in enumerate(chips)])
            passed.append([[copy((3 + k) * CH + j, piece(out_ref.at[2 * cx + cy], c, j), 2 * cx + cy, c, j, (x, y, 1 - c))
                            for j in range(CH)] for k, (cx, cy) in enumerate(chips)])
        for a in range(nbuf):
            for j in range(CH):
                for k in range(3):
                    first[a][k][j].start()
        for a, (x_ref, _, piece, copy) in enumerate(plans):
            for j in range(CH):
                for k, (cx, cy) in enumerate(chips):
                    copy(k * CH + j, piece(x_ref, c, j), 2 * cx + cy, c, j, (cx, cy, c)).wait_recv()
                    passed[a][k][j].start()
        for a, (x_ref, _, piece, copy) in enumerate(plans):
            for j in range(CH):
                for k, (cx, cy) in enumerate(chips):
                    copy((3 + k) * CH + j, piece(x_ref, c, j), 2 * cx + cy, 1 - c, j, (x, y, 1 - c)).wait_recv()
        for a in range(nbuf):
            for k in range(3):
                for cp in first[a][k] + passed[a][k]:
                    cp.wait_send()

    return pl.pallas_call(
        body,
        out_shape=[jax.ShapeDtypeStruct((N_CHIPS,) + l.shape, l.dtype) for l in locals_],
        in_specs=_any_specs(nbuf),
        out_specs=_any_specs(nbuf),
        scratch_shapes=[pltpu.SemaphoreType.DMA((nbuf * 6 * CH,)), pltpu.SemaphoreType.DMA((nbuf * 6 * CH,))],
        name="gather_chips",
    )(*locals_)


def pair_swap(parts):
    nbuf = len(parts)
    n = N_CHIPS
    CH = COPY_CHUNKS

    def body(*refs):
        p_refs, got_refs, (send_sems, recv_sems) = refs[:nbuf], refs[nbuf:2 * nbuf], refs[2 * nbuf:]
        x, y, c = _mesh_pos()
        swap = []
        for a, (p_ref, got_ref) in enumerate(zip(p_refs, got_refs)):
            half = p_ref.shape[1] // 2
            q = half // CH
            for s in range(n):
                for j in range(CH):
                    k = (a * n + s) * CH + j
                    src = p_ref.at[s, pl.ds(pl.multiple_of((1 - c) * half + j * q, PACK_ALIGN), q), :]
                    swap.append(pltpu.make_async_remote_copy(src_ref=src, dst_ref=got_ref.at[s, pl.ds(j * q, q), :],
                                                             send_sem=send_sems.at[k], recv_sem=recv_sems.at[k],
                                                             device_id=(x, y, 1 - c), device_id_type=pl.DeviceIdType.MESH))
        for cp in swap:
            cp.start()
        for cp in swap:
            cp.wait()

    return pl.pallas_call(
        body,
        out_shape=[jax.ShapeDtypeStruct((n, p.shape[1] // 2, p.shape[2]), p.dtype) for p in parts],
        in_specs=_any_specs(nbuf),
        out_specs=_any_specs(nbuf),
        scratch_shapes=[pltpu.SemaphoreType.DMA((nbuf * n * CH,)), pltpu.SemaphoreType.DMA((nbuf * n * CH,))],
        name="pair_swap",
    )(*parts)


def chip_exchange(parts):
    nbuf = len(parts)

    def body(*refs):
        p_refs, out_refs, (send_sems, recv_sems) = refs[:nbuf], refs[nbuf:2 * nbuf], refs[2 * nbuf:]
        x, y, c = _mesh_pos()
        my = 2 * x + y
        chips = [(1 - x, y), (x, 1 - y), (1 - x, 1 - y)]

        def copy(a, k, src_slot, dst_slot, to):
            return pltpu.make_async_remote_copy(src_ref=p_refs[a].at[src_slot], dst_ref=out_refs[a].at[dst_slot],
                                                send_sem=send_sems.at[3 * a + k], recv_sem=recv_sems.at[3 * a + k],
                                                device_id=to, device_id_type=pl.DeviceIdType.MESH)

        sends = [copy(a, k, 2 * cx + cy, my, (cx, cy, c)) for a in range(nbuf) for k, (cx, cy) in enumerate(chips)]
        for cp in sends:
            cp.start()
        for a in range(nbuf):
            for k, (cx, cy) in enumerate(chips):
                copy(a, k, my, 2 * cx + cy, (cx, cy, c)).wait_recv()
        for cp in sends:
            cp.wait_send()

    return pl.pallas_call(
        body,
        out_shape=[jax.ShapeDtypeStruct(p.shape, p.dtype) for p in parts],
        in_specs=_any_specs(nbuf),
        out_specs=_any_specs(nbuf),
        scratch_shapes=[pltpu.SemaphoreType.DMA((3 * nbuf,)), pltpu.SemaphoreType.DMA((3 * nbuf,))],
        name="chip_exchange",
    )(*parts)


def pair_share(mine):
    nbuf = len(mine)
    CH = COPY_CHUNKS

    def body(*refs):
        m_refs, out_refs, (send_sems, recv_sems) = refs[:nbuf], refs[nbuf:2 * nbuf], refs[2 * nbuf:]
        x, y, c = _mesh_pos()
        swap = []
        for a, (m_ref, out_ref) in enumerate(zip(m_refs, out_refs)):
            q = m_ref.shape[0] // CH
            for j in range(CH):
                swap.append(pltpu.make_async_remote_copy(src_ref=m_ref.at[pl.ds(j * q, q), :], dst_ref=out_ref.at[pl.ds(j * q, q), :],
                                                         send_sem=send_sems.at[a * CH + j], recv_sem=recv_sems.at[a * CH + j],
                                                         device_id=(x, y, 1 - c), device_id_type=pl.DeviceIdType.MESH))
        for cp in swap:
            cp.start()
        for cp in swap:
            cp.wait()

    return pl.pallas_call(
        body,
        out_shape=[jax.ShapeDtypeStruct(m.shape, m.dtype) for m in mine],
        in_specs=_any_specs(nbuf),
        out_specs=_any_specs(nbuf),
        scratch_shapes=[pltpu.SemaphoreType.DMA((nbuf * CH,)), pltpu.SemaphoreType.DMA((nbuf * CH,))],
        name="pair_share",
    )(*mine)


def sum_chips(recv, own):
    n, R, Wd = recv.shape
    tr = _pick(R, (512, 256, 128, 64, 32, 16, 8))

    def body(r_ref, p_ref, o_ref):
        my = 2 * lax.axis_index("x") + lax.axis_index("y")
        acc = jnp.zeros((tr, Wd), f32)
        for s in range(n):
            acc = acc + jnp.where(my == s, p_ref[s], r_ref[s]).astype(f32)
        o_ref[...] = acc

    spec = pl.BlockSpec((n, tr, Wd), lambda i: (0, i, 0))
    return pl.pallas_call(
        body,
        out_shape=jax.ShapeDtypeStruct((R, Wd), f32),
        grid=(R // tr,),
        in_specs=[spec, spec],
        out_specs=pl.BlockSpec((tr, Wd), lambda i: (i, 0)),
        compiler_params=_cparams(("parallel",)),
        name="sum_chips",
    )(recv, own)


def add_pair(parts, got):
    n, R, Wd = parts.shape
    half = R // 2
    tr = _pick(half, (512, 256, 128, 64, 32, 16, 8))
    nb = half // tr

    def body(lo_ref, hi_ref, g_ref, o_ref):
        mine = jnp.where(lax.axis_index("c") == 0, lo_ref[...], hi_ref[...])
        o_ref[...] = (mine.astype(f32) + g_ref[...].astype(f32)).astype(o_ref.dtype)

    spec = pl.BlockSpec((1, tr, Wd), lambda s, i: (s, i, 0))
    return pl.pallas_call(
        body,
        out_shape=jax.ShapeDtypeStruct(got.shape, parts.dtype),
        grid=(n, nb),
        in_specs=[spec, pl.BlockSpec((1, tr, Wd), lambda s, i: (s, nb + i, 0)), spec],
        out_specs=spec,
        compiler_params=_cparams(("parallel", "parallel")),
        name="add_pair",
    )(parts, parts, got)


def reduce_scatter(parts):
    chip_sum = [add_pair(p, g) for p, g in zip(parts, pair_swap(parts))]
    mine = [sum_chips(r, s) for r, s in zip(chip_exchange(chip_sum), chip_sum)]
    first = lax.axis_index("c") == 0
    return [jnp.concatenate([jnp.where(first, m, t), jnp.where(first, t, m)], axis=0) for m, t in zip(mine, pair_share(mine))]


def adamw(w, g, m, v):
    shp = w.shape
    if w.size * 4 <= (1 << 20):
        grid, block, imap = (1,), shp, lambda i: (0,) * len(shp)
    else:
        n0, R, C = shp
        tr = _divisor_tile(R, lambda t: t * C * 4 <= (1 << 20), 8)
        grid, block, imap = (n0, R // tr), (1, tr, C), lambda j, i: (j, i, 0)

    def body(w_ref, g_ref, m_ref, v_ref, d_ref, mo_ref, vo_ref):
        g = g_ref[...]
        m = ADAM_B1 * m_ref[...] + (1.0 - ADAM_B1) * g
        v = ADAM_B2 * v_ref[...] + (1.0 - ADAM_B2) * (g * g)
        m_hat = m / (1.0 - ADAM_B1 ** ADAM_STEP)
        v_hat = v / (1.0 - ADAM_B2 ** ADAM_STEP)
        d_ref[...] = -ADAM_LR * (m_hat / (jnp.sqrt(v_hat) + ADAM_EPS) + ADAM_WD * w_ref[...])
        mo_ref[...] = m
        vo_ref[...] = v

    spec = pl.BlockSpec(block, imap)
    return pl.pallas_call(
        body,
        out_shape=[jax.ShapeDtypeStruct(shp, f32)] * 3,
        grid=grid,
        in_specs=[spec] * 4,
        out_specs=[spec] * 3,
        compiler_params=_cparams(("parallel",) * len(grid)),
        name="adamw",
    )(w, g, m, v)


def _pack(arrs, dtype):
    flat = jnp.concatenate([a.astype(dtype).reshape(-1) for a in arrs])
    n = flat.shape[0]
    unit = PACK_W * PACK_ROWS
    padded = -(-n // unit) * unit
    return jnp.pad(flat, (0, padded - n)).reshape(-1, PACK_W)


def _unpack(buf, shapes):
    flat = buf.reshape(-1)
    out, off = [], 0
    for s in shapes:
        n = int(np.prod(s))
        out.append(flat[off:off + n].reshape(s))
        off += n
    return out


BIG = (("ab_w_in", 2), ("ab_w_out", 1), ("c_w_in", 2), ("c_w_out", 1), ("ffn_w_up", 2), ("ffn_w_down", 1))
SMALL = (("c_conv_w", 2), ("c_conv_b", 1), ("c_norm_g", 1), ("ffn_conv_w", 2))
REP = ("norm_mix_pre", "norm_mix_post", "norm_ffn_pre", "norm_ffn_post", "ab_ret_decay_logit", "ab_ret_gn_g", "ab_na_rpb",
       "c_dt_bias", "c_a_log", "c_d_skip", "ffn_conv_b")
WEIGHTS = ("norm_mix_pre", "norm_mix_post", "norm_ffn_pre", "norm_ffn_post", "ab_w_in", "ab_ret_decay_logit", "ab_ret_gn_g",
           "ab_na_rpb", "ab_w_out", "c_w_in", "c_conv_w", "c_conv_b", "c_dt_bias", "c_a_log", "c_d_skip", "c_norm_g", "c_w_out",
           "ffn_w_up", "ffn_conv_w", "ffn_conv_b", "ffn_w_down")


BIG_GROUPS = (("ab_w_in",), ("ab_w_out", "c_w_out", "ffn_w_down"), ("c_w_in",), ("ffn_w_up",))
BIG_AXIS = dict(BIG)
ROW_UNIT = 128


def _rows(arrs, dtype):
    C = arrs[0].shape[-1]
    buf = jnp.concatenate([a.astype(dtype).reshape(-1, C) for a in arrs], axis=0)
    pad = -buf.shape[0] % ROW_UNIT
    return jnp.pad(buf, ((0, pad), (0, 0))) if pad else buf


def _unrows(buf, shapes):
    out, off = [], 0
    for s in shapes:
        n = int(np.prod(s[:-1]))
        out.append(buf[off:off + n].reshape(s))
        off += n
    return out


def _gather_all(w):
    bufs = [_rows([w[n] for n in grp], MXU_DTYPE) for grp in BIG_GROUPS] + [_pack([w[n] for n, _ in SMALL], f32)]
    got = gather_chips(bufs)
    my = 2 * lax.axis_index("x") + lax.axis_index("y")

    def whole(n, ax, dtype, pieces):
        return jnp.concatenate([jnp.where(my == s, w[n].astype(dtype), pieces[s]) for s in range(N_CHIPS)], axis=ax)

    big = {}
    for grp, g in zip(BIG_GROUPS, got):
        per_chip = [_unrows(g[s], [w[n].shape for n in grp]) for s in range(N_CHIPS)]
        for j, n in enumerate(grp):
            big[n] = whole(n, BIG_AXIS[n], MXU_DTYPE, [per_chip[s][j] for s in range(N_CHIPS)])
    per_chip = [_unpack(got[-1][s], [w[n].shape for n, _ in SMALL]) for s in range(N_CHIPS)]
    small = {n: whole(n, ax, f32, [per_chip[s][j] for s in range(N_CHIPS)]) for j, (n, ax) in enumerate(SMALL)}
    return big, small


def _reduce_all(gbig, gsmall, grep, w):
    split = {n: jnp.split(g, N_CHIPS, axis=BIG_AXIS[n]) for n, g in gbig.items()}
    parts = [jnp.stack([_rows([split[n][s] for n in grp], MXU_DTYPE) for s in range(N_CHIPS)]) for grp in BIG_GROUPS]
    ssplit = {n: jnp.split(gsmall[n], N_CHIPS, axis=ax) for n, ax in SMALL}
    parts.append(jnp.stack([_pack([ssplit[n][s] for n, _ in SMALL] + [grep[n] for n in REP], f32) for s in range(N_CHIPS)]))
    res = reduce_scatter(parts)
    grads = {}
    for grp, r in zip(BIG_GROUPS, res):
        grads.update(zip(grp, _unrows(r, [w[n].shape for n in grp])))
    small_names = [n for n, _ in SMALL] + list(REP)
    grads.update(zip(small_names, _unpack(res[-1], [w[n].shape for n in small_names])))
    return grads


def kernel(x, norm_mix_pre, norm_mix_post, norm_ffn_pre, norm_ffn_post, ab_w_in, ab_ret_decay_logit, ab_ret_gn_g, ab_na_rpb, ab_w_out, c_w_in, c_conv_w, c_conv_b, c_dt_bias, c_a_log, c_d_skip, c_norm_g, c_w_out, ffn_w_up, ffn_conv_w, ffn_conv_b, ffn_w_down, loss_target, m_norm_mix_pre, m_norm_mix_post, m_norm_ffn_pre, m_norm_ffn_post, m_ab_w_in, m_ab_ret_decay_logit, m_ab_ret_gn_g, m_ab_na_rpb, m_ab_w_out, m_c_w_in, m_c_conv_w, m_c_conv_b, m_c_dt_bias, m_c_a_log, m_c_d_skip, m_c_norm_g, m_c_w_out, m_ffn_w_up, m_ffn_conv_w, m_ffn_conv_b, m_ffn_w_down, v_norm_mix_pre, v_norm_mix_post, v_norm_ffn_pre, v_norm_ffn_post, v_ab_w_in, v_ab_ret_decay_logit, v_ab_ret_gn_g, v_ab_na_rpb, v_ab_w_out, v_c_w_in, v_c_conv_w, v_c_conv_b, v_c_dt_bias, v_c_a_log, v_c_d_skip, v_c_norm_g, v_c_w_out, v_ffn_w_up, v_ffn_conv_w, v_ffn_conv_b, v_ffn_w_down):
    args = dict(locals())
    w = {n: args[n] for n in WEIGHTS}
    mom = {n: args["m_" + n] for n in WEIGHTS}
    var = {n: args["v_" + n] for n in WEIGHTS}

    big, small = _gather_all(w)
    rep = {n: w[n] for n in REP}

    def loss_fn(xs, big, small, rep):
        return model_loss(xs, loss_target[0], big, small, rep)

    loss, (gx, gbig, gsmall, grep) = jax.value_and_grad(loss_fn, argnums=(0, 1, 2, 3))(x[0], big, small, rep)
    loss = lax.psum(loss, ("x", "y", "c"))

    grads = _reduce_all(gbig, gsmall, grep, w)

    delta, new_m, new_v = {}, {}, {}
    for n in WEIGHTS:
        delta[n], new_m[n], new_v[n] = adamw(w[n], grads[n], mom[n], var[n])

    return (loss, gx[None], *[grads[n] for n in WEIGHTS], *[delta[n] for n in WEIGHTS],
            *[new_m[n] for n in WEIGHTS], *[new_v[n] for n in WEIGHTS])
```

```python
import functools
import math

import numpy as np
import jax
import jax.numpy as jnp
from jax import lax
from jax.experimental import pallas as pl
from jax.experimental.pallas import tpu as pltpu

f32 = jnp.float32
bf16 = jnp.bfloat16
MXU_DTYPE = bf16

GRID_W = 64
CHUNK = 128
EPS = 1e-6
RET_HEADS = 8
RET_DH = 64
ROPE_BASE = 10000.0
NA_HEADS = 8
NA_DH = 64
NA_WIN_R = 8
NA_WIN_C = 16
NA_ROWS_PER_STEP = 8
SCAN_CHUNKS_PER_STEP = 2
SSD_HEADDIM = 64
SSD_GROUPS = 4
SSD_STATE = 128
ADAM_LR = 0.001
ADAM_B1 = 0.9
ADAM_B2 = 0.999
ADAM_EPS = 1e-08
ADAM_WD = 0.01
ADAM_STEP = 10

LANES = 128
HEAD_W = 64
PACK_W = 512
PACK_ROWS = 1024
PACK_ALIGN = 16
COPY_CHUNKS = 2
VMEM_LIMIT = 56 * 1024 * 1024
MM_BLOCK_BYTES = 6 * 1024 * 1024
ROW_BLOCK_BYTES = 16 * 1024 * 1024
N_CHIPS = 4
N_DEV = 8
NEG_INF = -1e30

_DIMS = {"nn": (((1,), (0,)), ((), ())), "nt": (((1,), (1,)), ((), ())), "tn": (((0,), (0,)), ((), ()))}


def _cparams(sem=None):
    return pltpu.CompilerParams(dimension_semantics=sem, vmem_limit_bytes=VMEM_LIMIT)


def _pick(dim, cands):
    for c in cands:
        if dim % c == 0:
            return c
    return dim


def _divisor_tile(dim, fits, align):
    for d in range(1, dim + 1):
        t = dim // d
        if dim % d == 0 and t % align == 0 and fits(t):
            return t
    return dim


def _bdot_raw(a, b, mode):
    return lax.dot_general(a.astype(MXU_DTYPE), b.astype(MXU_DTYPE), _DIMS[mode], preferred_element_type=f32)


@functools.partial(jax.custom_vjp, nondiff_argnums=(2,))
def bdot(a, b, mode):
    return _bdot_raw(a, b, mode)


def _bdot_fwd(a, b, mode):
    return _bdot_raw(a, b, mode), (a, b)


def _bdot_bwd(mode, res, g):
    a, b = res
    if mode == "nn":
        da, db = _bdot_raw(g, b, "nt"), _bdot_raw(a, g, "tn")
    elif mode == "nt":
        da, db = _bdot_raw(g, b, "nn"), _bdot_raw(g, a, "tn")
    else:
        da, db = _bdot_raw(b, g, "nt"), _bdot_raw(a, g, "nn")
    return da.astype(a.dtype), db.astype(b.dtype)


bdot.defvjp(_bdot_fwd, _bdot_bwd)


def _mm_call(a, b, mode, out_dtype):
    if mode == "nn":
        (M, K), (K2, N) = a.shape, b.shape
    elif mode == "nt":
        (M, K), (N, K2) = a.shape, b.shape
    else:
        (K, M), (K2, N) = a.shape, b.shape
    assert K == K2, (a.shape, b.shape, mode)
    a_bytes, b_bytes, o_bytes = a.dtype.itemsize, b.dtype.itemsize, jnp.dtype(out_dtype).itemsize
    if mode == "tn":
        tn = _divisor_tile(N, lambda t: t <= 1536, LANES)
        tm = _divisor_tile(M, lambda t: t * tn * 4 <= MM_BLOCK_BYTES, 8)
        tk = _divisor_tile(K, lambda t: t * tm * a_bytes <= MM_BLOCK_BYTES and t * tn * b_bytes <= MM_BLOCK_BYTES, LANES)
    else:
        tk, tn = K, N
        tm = _divisor_tile(M, lambda t: t * K * a_bytes <= MM_BLOCK_BYTES and t * N * o_bytes <= MM_BLOCK_BYTES, 8)
    nk = K // tk
    if mode == "nn":
        a_spec = pl.BlockSpec((tm, tk), lambda i, j, k: (i, k))
        b_spec = pl.BlockSpec((tk, tn), lambda i, j, k: (k, j))
    elif mode == "nt":
        a_spec = pl.BlockSpec((tm, tk), lambda i, j, k: (i, k))
        b_spec = pl.BlockSpec((tn, tk), lambda i, j, k: (j, k))
    else:
        a_spec = pl.BlockSpec((tk, tm), lambda i, j, k: (k, i))
        b_spec = pl.BlockSpec((tk, tn), lambda i, j, k: (k, j))

    if nk == 1:
        def body(a_ref, b_ref, o_ref):
            o_ref[...] = _bdot_raw(a_ref[...], b_ref[...], mode).astype(o_ref.dtype)
    else:
        def body(a_ref, b_ref, o_ref, acc_ref):
            k = pl.program_id(2)

            @pl.when(k == 0)
            def _():
                acc_ref[...] = jnp.zeros_like(acc_ref)

            acc_ref[...] += _bdot_raw(a_ref[...], b_ref[...], mode)

            @pl.when(k == nk - 1)
            def _():
                o_ref[...] = acc_ref[...].astype(o_ref.dtype)

    return pl.pallas_call(
        body,
        out_shape=jax.ShapeDtypeStruct((M, N), out_dtype),
        grid=(M // tm, N // tn, nk),
        in_specs=[a_spec, b_spec],
        out_specs=pl.BlockSpec((tm, tn), lambda i, j, k: (i, j)),
        scratch_shapes=[pltpu.VMEM((tm, tn), f32)] if nk > 1 else [],
        compiler_params=_cparams(("parallel", "parallel", "arbitrary")),
        name="mm_" + mode,
    )(a, b)


def mm(a, b, mode="nn", out_dtype=f32):
    @jax.custom_vjp
    def op(a, b):
        return _mm_call(a, b, mode, out_dtype)

    def fwd(a, b):
        return _mm_call(a, b, mode, out_dtype), (a, b)

    def bwd(res, g):
        a, b = res
        if mode == "nn":
            return _mm_call(g, b, "nt", a.dtype), _mm_call(a, g, "tn", b.dtype)
        if mode == "nt":
            return _mm_call(g, b, "nn", a.dtype), _mm_call(g, a, "tn", b.dtype)
        return _mm_call(b, g, "nt", a.dtype), _mm_call(a, g, "nn", b.dtype)

    op.defvjp(fwd, bwd)
    return op(a, b)


def _row_tile(S, row_bytes):
    tm = 1024
    while tm > 8 and (tm * row_bytes > ROW_BLOCK_BYTES or S % tm):
        tm //= 2
    return tm


def rowwise(fn, name, rows, params, out_dtypes, n_diff_rows=None, n_diff_params=None, ncol=1, bwd_fn=None):
    rows, params = list(rows), list(params)
    nr, npar = len(rows), len(params)
    ndr = nr if n_diff_rows is None else n_diff_rows
    ndp = npar if n_diff_params is None else n_diff_params
    S = rows[0].shape[0]
    rw = [r.shape[1] // ncol for r in rows]
    pshape = [(p.shape[0], p.shape[1] // ncol) for p in params]

    def block_structs(tm):
        return ([jax.ShapeDtypeStruct((tm, w), f32) for w in rw] + [jax.ShapeDtypeStruct(s, f32) for s in pshape])

    outs_s = jax.eval_shape(fn, *block_structs(8))
    ow = [o.shape[1] for o in outs_s]
    nout = len(ow)
    row_bytes = 4 * (sum(rw) * 2 + sum(ow) * 2)
    tm = _row_tile(S, row_bytes)
    grid = (ncol, S // tm)

    def rspec(w):
        return pl.BlockSpec((tm, w), lambda g, i: (i, g))

    def pspec(s):
        return pl.BlockSpec(s, lambda g, i: (0, g))

    def call_fwd(*args):
        def body(*refs):
            vals = [r[...].astype(f32) for r in refs[:nr + npar]]
            res = fn(*vals)
            for o, r in zip(refs[nr + npar:], res):
                o[...] = r.astype(o.dtype)

        return pl.pallas_call(
            body,
            out_shape=[jax.ShapeDtypeStruct((S, w * ncol), dt) for w, dt in zip(ow, out_dtypes)],
            grid=grid,
            in_specs=[rspec(w) for w in rw] + [pspec(s) for s in pshape],
            out_specs=[rspec(w) for w in ow],
            compiler_params=_cparams(("parallel", "parallel")),
            name=name + "_fwd",
        )(*args)

    def call_bwd(args, douts):
        def body(*refs):
            in_refs = refs[:nr + npar]
            do_refs = refs[nr + npar:nr + npar + nout]
            dr_refs = refs[nr + npar + nout:nr + npar + nout + ndr]
            dp_refs = refs[nr + npar + nout + ndr:]
            rv = [r[...] for r in in_refs[:nr]]
            pv = [r[...] for r in in_refs[nr:]]
            dos = [d[...].astype(f32) for d in do_refs]
            if bwd_fn is not None:
                drs, dps = bwd_fn(rv, pv, dos)
            else:
                def f(*a):
                    return fn(*a[:ndr], *rv[ndr:], *a[ndr:], *pv[ndp:])

                _, vjp = jax.vjp(f, *[v.astype(f32) for v in rv[:ndr]], *pv[:ndp])
                cts = vjp(tuple(dos))
                drs, dps = cts[:ndr], cts[ndr:]
            for r, ct in zip(dr_refs, drs):
                r[...] = ct.astype(r.dtype)
            if ndp:
                @pl.when(pl.program_id(1) == 0)
                def _():
                    for r in dp_refs:
                        r[...] = jnp.zeros_like(r)

                for r, ct in zip(dp_refs, dps):
                    r[...] += ct

        return pl.pallas_call(
            body,
            out_shape=[jax.ShapeDtypeStruct(r.shape, r.dtype) for r in rows[:ndr]]
            + [jax.ShapeDtypeStruct(p.shape, f32) for p in params[:ndp]],
            grid=grid,
            in_specs=[rspec(w) for w in rw] + [pspec(s) for s in pshape] + [rspec(w) for w in ow],
            out_specs=[rspec(w) for w in rw[:ndr]] + [pspec(s) for s in pshape[:ndp]],
            compiler_params=_cparams(("parallel", "arbitrary")),
            name=name + "_bwd",
        )(*args, *douts)

    @jax.custom_vjp
    def op(*args):
        return tuple(call_fwd(*args))

    def fwd(*args):
        return tuple(call_fwd(*args)), args

    def bwd(args, douts):
        res = call_bwd(args, douts)
        drs, dps = res[:ndr], res[ndr:]
        out = list(drs) + [jnp.zeros_like(a) for a in args[ndr:nr]]
        out += [dp.astype(p.dtype) for dp, p in zip(dps, args[nr:nr + ndp])]
        out += [jnp.zeros_like(a) for a in args[nr + ndp:]]
        return tuple(out)

    op.defvjp(fwd, bwd)
    return op(*rows, *params)


def _silu(x):
    return x * (1.0 / (1.0 + jnp.exp(-x)))


def _softplus(x):
    return jnp.maximum(x, 0.0) + jnp.log(1.0 + jnp.exp(-jnp.abs(x)))


def _gelu_tanh(x):
    return 0.5 * x * (1.0 + jnp.tanh(math.sqrt(2.0 / math.pi) * (x + 0.044715 * (x * x * x))))


def _rms_fn(x, g):
    return x * lax.rsqrt(jnp.mean(x * x, axis=-1, keepdims=True) + EPS) * g


def _rms_bwd(x, g, dy):
    r = lax.rsqrt(jnp.mean(x * x, axis=-1, keepdims=True) + EPS)
    xh = x * r
    dxh = dy * g
    dx = r * (dxh - xh * jnp.mean(dxh * xh, axis=-1, keepdims=True))
    return dx, jnp.sum(dy * xh, axis=0, keepdims=True)


def rms(x, g, out_dtype):
    def bwd_fn(rv, pv, dos):
        dx, dg = _rms_bwd(rv[0], pv[0], dos[0])
        return (dx,), (dg,)

    return rowwise(lambda x, g: (_rms_fn(x, g),), "rms", [x], [g.reshape(1, -1)], [out_dtype], bwd_fn=bwd_fn)[0]


def rms_residual_norm(m, g, x, g_next):
    def fn(m, x, g, gn):
        xn = x + _rms_fn(m, g)
        return xn, _rms_fn(xn, gn)

    def bwd_fn(rv, pv, dos):
        (m, x), (g, gn), (dxn, dhn) = rv, pv, dos
        xn = x + _rms_fn(m, g)
        d_from_norm, dgn = _rms_bwd(xn, gn, dhn)
        dxn = dxn + d_from_norm
        dm, dg = _rms_bwd(m, g, dxn)
        return (dm, dxn), (dg, dgn)

    return rowwise(fn, "rms_res_norm", [m, x], [g.reshape(1, -1), g_next.reshape(1, -1)], [f32, MXU_DTYPE], bwd_fn=bwd_fn)


def rms_residual(m, g, x):
    def bwd_fn(rv, pv, dos):
        dm, dg = _rms_bwd(rv[0], pv[0], dos[0])
        return (dm, dos[0]), (dg,)

    return rowwise(lambda m, x, g: (x + _rms_fn(m, g),), "rms_res", [m, x], [g.reshape(1, -1)], [f32], bwd_fn=bwd_fn)[0]


def loss_op(y, tgt):
    S, D = y.shape
    tm = _row_tile(S, 4 * D * 4)

    def call_fwd(y, tgt):
        def body(y_ref, t_ref, o_ref):
            @pl.when(pl.program_id(0) == 0)
            def _():
                o_ref[...] = jnp.zeros_like(o_ref)

            e = y_ref[...] - t_ref[...]
            o_ref[...] += 0.5 * jnp.sum(jnp.mean(e * e, axis=-1, keepdims=True))

        out = pl.pallas_call(
            body,
            out_shape=jax.ShapeDtypeStruct((8, LANES), f32),
            grid=(S // tm,),
            in_specs=[pl.BlockSpec((tm, D), lambda i: (i, 0))] * 2,
            out_specs=pl.BlockSpec((8, LANES), lambda i: (0, 0)),
            compiler_params=_cparams(("arbitrary",)),
            name="loss_fwd",
        )(y, tgt)
        return out[0, 0]

    def call_bwd(y, tgt, g):
        def body(y_ref, t_ref, g_ref, o_ref):
            o_ref[...] = (y_ref[...] - t_ref[...]) * (g_ref[...] * (1.0 / D))

        return pl.pallas_call(
            body,
            out_shape=jax.ShapeDtypeStruct((S, D), f32),
            grid=(S // tm,),
            in_specs=[pl.BlockSpec((tm, D), lambda i: (i, 0))] * 2 + [pl.BlockSpec((1, 1), lambda i: (0, 0))],
            out_specs=pl.BlockSpec((tm, D), lambda i: (i, 0)),
            compiler_params=_cparams(("parallel",)),
            name="loss_bwd",
        )(y, tgt, g.reshape(1, 1).astype(f32))

    @jax.custom_vjp
    def op(y, tgt):
        return call_fwd(y, tgt)

    def fwd(y, tgt):
        return call_fwd(y, tgt), (y, tgt)

    def bwd(res, g):
        y, tgt = res
        return call_bwd(y, tgt, g), jnp.zeros_like(tgt)

    op.defvjp(fwd, bwd)
    return op(y, tgt)


HALO = 8


def _conv_tile(S, R):
    def ext(ref, r0):
        cur = ref[pl.ds(r0, R), :]
        prev = ref[pl.ds(pl.multiple_of(jnp.maximum(r0 - HALO, 0), HALO), HALO), :]
        nxt = ref[pl.ds(pl.multiple_of(jnp.minimum(r0 + R, S - HALO), HALO), HALO), :]
        prev = jnp.where(r0 > 0, prev, 0.0)
        nxt = jnp.where(r0 + R < S, nxt, 0.0)
        return jnp.concatenate([prev, cur, nxt], axis=0)

    return ext


def _shift_rows(e, k, R):
    n = e.shape[0]
    if k == 0:
        return e[HALO:HALO + R]
    return pltpu.roll(e, (-k) % n, 0)[HALO:HALO + R]


def _silu_bwd(us, dy):
    u, = us
    s = 1.0 / (1.0 + jnp.exp(-u))
    return (dy * (s * (1.0 + u * (1.0 - s))),)


def _geglu(g, v):
    return _gelu_tanh(g) * v


def _geglu_bwd(us, dy):
    g, v = us
    c = math.sqrt(2.0 / math.pi)
    t = jnp.tanh(c * (g + 0.044715 * (g * g * g)))
    half = 0.5 * (1.0 + t)
    dgelu = half + 0.5 * g * (1.0 - t * t) * (c * (1.0 + 3.0 * 0.044715 * (g * g)))
    return dy * v * dgelu, dy * (g * half)


def mm_conv_act(h, ws, cws, cbs, act, act_bwd, out_dtype, name):
    n = len(ws)
    S = h.shape[0]
    C = ws[0].shape[1]
    W = cws[0].shape[0]
    pad = W // 2
    bw = _pick(C, (LANES,))
    R = _pick(S, (256, 128, 64, 32, 16, 8))
    nt = S // R
    ext = _conv_tile(S, R)
    col = lambda rows: pl.BlockSpec((rows, bw), lambda j: (0, j))

    def conv(e, wv, bv):
        acc = bv + wv[pad] * e[HALO:HALO + R]
        for j in range(W):
            if j != pad:
                acc = acc + wv[j] * _shift_rows(e, j - pad, R)
        return acc

    def call_fwd(xs, cws, cbs):
        def body(*refs):
            x_refs, w_refs, b_refs, y_ref = refs[:n], refs[n:2 * n], refs[2 * n:3 * n], refs[3 * n]
            wvs = [[w[j:j + 1, :] for j in range(W)] for w in w_refs]
            bvs = [b[...] for b in b_refs]

            def tile(i, c):
                r0 = pl.multiple_of(i * R, R)
                us = [conv(ext(x, r0), wv, bv) for x, wv, bv in zip(x_refs, wvs, bvs)]
                y_ref[pl.ds(r0, R), :] = act(*us).astype(y_ref.dtype)
                return c

            lax.fori_loop(0, nt, tile, 0)

        return pl.pallas_call(
            body,
            out_shape=jax.ShapeDtypeStruct((S, C), out_dtype),
            grid=(C // bw,),
            in_specs=[col(S)] * n + [col(W)] * n + [col(1)] * n,
            out_specs=col(S),
            compiler_params=_cparams(("parallel",)),
            name=name + "_fwd",
        )(*xs, *cws, *cbs)

    def call_bwd(xs, cws, cbs, dy):
        def body(*refs):
            x_refs, w_refs, b_refs, dy_ref = refs[:n], refs[n:2 * n], refs[2 * n:3 * n], refs[3 * n]
            dx_refs, dw_refs, db_refs = refs[3 * n + 1:4 * n + 1], refs[4 * n + 1:5 * n + 1], refs[5 * n + 1:6 * n + 1]
            du_scr = refs[6 * n + 1:]
            wvs = [[w[j:j + 1, :] for j in range(W)] for w in w_refs]
            bvs = [b[...] for b in b_refs]
            zero = jnp.zeros((1, bw), f32)

            def first(i, dbs):
                r0 = pl.multiple_of(i * R, R)
                us = [conv(ext(x, r0), wv, bv) for x, wv, bv in zip(x_refs, wvs, bvs)]
                dus = act_bwd(us, dy_ref[pl.ds(r0, R), :].astype(f32))
                for scr, du in zip(du_scr, dus):
                    scr[pl.ds(r0, R), :] = du
                return tuple(db + jnp.sum(du, axis=0, keepdims=True) for db, du in zip(dbs, dus))

            dbs = lax.fori_loop(0, nt, first, tuple(zero for _ in range(n)))

            def second(i, dws):
                r0 = pl.multiple_of(i * R, R)
                new = []
                for x, scr, dx, wv, dw in zip(x_refs, du_scr, dx_refs, wvs, dws):
                    ex, ed = ext(x, r0), ext(scr, r0)
                    d0 = ed[HALO:HALO + R]
                    acc = jnp.zeros((R, bw), f32)
                    row = []
                    for j in range(W):
                        acc = acc + wv[j] * _shift_rows(ed, pad - j, R)
                        row.append(dw[j] + jnp.sum(d0 * _shift_rows(ex, j - pad, R), axis=0, keepdims=True))
                    dx[pl.ds(r0, R), :] = acc.astype(dx.dtype)
                    new.append(tuple(row))
                return tuple(new)

            dws = lax.fori_loop(0, nt, second, tuple(tuple(zero for _ in range(W)) for _ in range(n)))
            for dw_ref, db_ref, dw, db in zip(dw_refs, db_refs, dws, dbs):
                dw_ref[...] = jnp.zeros_like(dw_ref)
                for j in range(W):
                    dw_ref[j:j + 1, :] = dw[j]
                db_ref[...] = db

        return pl.pallas_call(
            body,
            out_shape=[jax.ShapeDtypeStruct((S, C), MXU_DTYPE)] * n + [jax.ShapeDtypeStruct((8, C), f32)] * n
            + [jax.ShapeDtypeStruct((1, C), f32)] * n,
            grid=(C // bw,),
            in_specs=[col(S)] * n + [col(W)] * n + [col(1)] * n + [col(S)],
            out_specs=[col(S)] * n + [col(8)] * n + [col(1)] * n,
            scratch_shapes=[pltpu.VMEM((S, bw), f32)] * n,
            compiler_params=_cparams(("parallel",)),
            name=name + "_bwd",
        )(*xs, *cws, *cbs, dy)

    @jax.custom_vjp
    def op(h, ws, cws, cbs):
        return call_fwd([_mm_call(h, w, "nn", f32) for w in ws], cws, cbs)

    def fwd(h, ws, cws, cbs):
        xs = [_mm_call(h, w, "nn", f32) for w in ws]
        return call_fwd(xs, cws, cbs), (h, ws, xs, cws, cbs)

    def bwd(res, dy):
        h, ws, xs, cws, cbs = res
        out = call_bwd(xs, cws, cbs, dy)
        dxs, dcws, dcbs = out[:n], out[n:2 * n], out[2 * n:]
        dh = _mm_call(dxs[0], ws[0], "nt", h.dtype)
        for dx, w in zip(dxs[1:], ws[1:]):
            dh = dh + _mm_call(dx, w, "nt", h.dtype)
        dws = tuple(_mm_call(h, dx, "tn", w.dtype) for dx, w in zip(dxs, ws))
        return dh, dws, tuple(d[:W] for d in dcws), tuple(dcbs)

    op.defvjp(fwd, bwd)
    return op(h, tuple(ws), tuple(cws), tuple(b.reshape(1, C) for b in cbs))


@jax.custom_vjp
def _masked_decay(cs_col, cs_row, mask01):
    return jnp.where(mask01 > 0, jnp.exp(cs_col - cs_row), 0.0)


def _masked_decay_fwd(cs_col, cs_row, mask01):
    d = jnp.where(mask01 > 0, jnp.exp(cs_col - cs_row), 0.0)
    return d, (d, mask01)


def _masked_decay_bwd(res, g):
    d, mask01 = res
    t = g * d
    return jnp.sum(t, axis=1, keepdims=True), -jnp.sum(t, axis=0, keepdims=True), jnp.zeros_like(mask01)


_masked_decay.defvjp(_masked_decay_fwd, _masked_decay_bwd)


def _scan_chunk(qs, ks, xs, cs_tok, dt_tok, hs, *, rev, incl, nsub):
    nb = len(xs)
    L, N = qs[0].shape
    W = xs[0].shape[1]
    Hg = cs_tok.shape[1]
    nh = W // HEAD_W
    shared = len(qs) == 1
    t = lax.broadcasted_iota(jnp.int32, (L, L), 0)
    l = lax.broadcasted_iota(jnp.int32, (L, L), 1)
    if rev:
        mask = (l >= t) if incl else (l > t)
    else:
        mask = (l <= t) if incl else (l < t)
    mask01 = mask.astype(f32)
    lane_a = lax.broadcasted_iota(jnp.int32, cs_tok.shape, 1)
    row_a = lax.broadcasted_iota(jnp.int32, (Hg, L), 0)
    last = lax.broadcasted_iota(jnp.int32, (1, L), 1) == (0 if rev else L - 1)
    vhead = lax.broadcasted_iota(jnp.int32, (1, W), 1) // HEAD_W
    qhead = lax.broadcasted_iota(jnp.int32, (1, N), 1) // (N // nsub)
    cs_rows = lax.dot_general(cs_tok, (t == l).astype(f32), _DIMS["tn"], precision=lax.Precision.HIGHEST,
                              preferred_element_type=f32)

    def by_head(vals):
        if len(vals) == 2:
            return jnp.where(vhead == 0, vals[0], vals[1])
        return sum(jnp.where(vhead == i, v, 0.0) for i, v in enumerate(vals))

    decay, lam_e, tau_e, gam_e, dt_e = [], [], [], [], []
    for b in range(nb):
        cs_cols, tots, dt_cols = [], [], []
        for i in range(nh):
            head = b * nh + i
            cs_col = jnp.sum(jnp.where(lane_a == head, cs_tok, 0.0), axis=1, keepdims=True)
            cs_row = jnp.sum(jnp.where(row_a == head, cs_rows, 0.0), axis=0, keepdims=True)
            tots.append(jnp.sum(jnp.where(last, cs_row, 0.0), axis=1, keepdims=True))
            decay.append(_masked_decay(cs_col, cs_row, mask01))
            cs_cols.append(cs_col)
            if dt_tok is not None:
                dt_cols.append(jnp.sum(jnp.where(lane_a == head, dt_tok, 0.0), axis=1, keepdims=True))
        cs_e, tot_e = by_head(cs_cols), by_head(tots)
        lam_e.append(jnp.exp(cs_e))
        tau_e.append(jnp.exp(tot_e - cs_e))
        gam_e.append(jnp.exp(tot_e))
        if dt_tok is not None:
            dt_e.append(by_head(dt_cols))
    vs = [x if dt_tok is None else x * dt_e[b] for b, x in enumerate(xs)]
    qk = lambda b: (qs[0], ks[0]) if shared else (qs[b], ks[b])
    if nsub == 1:
        scores = [bdot(qs[0], ks[0], "nt")] if shared else [bdot(*qk(b), "nt") for b in range(nb)]
        score = lambda b, i: scores[0 if shared else b]
    else:
        scores = [[bdot(jnp.where(qhead == i, qk(b)[0], 0.0), qk(b)[1], "nt") for i in range(nh)] for b in range(nb)]
        score = lambda b, i: scores[b][i]
    ys = [lam_e[b] * bdot(qk(b)[0], hs[b], "nn")
          + by_head([bdot(score(b, i) * decay[b * nh + i], vs[b], "nn") for i in range(nh)]) for b in range(nb)]
    hns = [gam_e[b] * hs[b] + bdot(qk(b)[1], tau_e[b] * vs[b], "tn") for b in range(nb)]
    if nsub > 1:
        nhead = lax.broadcasted_iota(jnp.int32, (N, W), 0) // (N // nsub)
        keep = nhead == lax.broadcasted_iota(jnp.int32, (N, W), 1) // HEAD_W
        hns = [jnp.where(keep, hn, 0.0) for hn in hns]
    return ys, hns


def chunk_cumsum(a_tok, rev):
    G, S, Hg = a_tok.shape
    L = CHUNK
    CB = _pick(S // L, (16, 8, 4, 2))

    def call(a, rev):
        def body(a_ref, o_ref):
            t = lax.broadcasted_iota(jnp.int32, (L, L), 0)
            l = lax.broadcasted_iota(jnp.int32, (L, L), 1)
            tri = ((l >= t) if rev else (l <= t)).astype(f32)
            for j in range(CB):
                o_ref[0, j * L:(j + 1) * L, :] = _exact_dot(tri, a_ref[0, j * L:(j + 1) * L, :])

        spec = pl.BlockSpec((1, CB * L, Hg), lambda g, c: (g, c, 0))
        return pl.pallas_call(
            body,
            out_shape=jax.ShapeDtypeStruct((G, S, Hg), f32),
            grid=(G, S // (CB * L)),
            in_specs=[spec],
            out_specs=spec,
            compiler_params=_cparams(("parallel", "parallel")),
            name="chunk_cumsum",
        )(a)

    @jax.custom_vjp
    def op(a):
        return call(a, rev)

    def fwd(a):
        return call(a, rev), None

    def bwd(_, g):
        return (call(g, not rev),)

    op.defvjp(fwd, bwd)
    return op(a_tok)


def scan_op(q, k, x, a_tok, dt_tok, *, rev, incl, nsub):
    S = q.shape[0]
    G, _, Hg = a_tok.shape
    N = q.shape[1] // G
    Vw = x.shape[1] // G
    L = CHUNK
    nc = S // L
    use_dt = dt_tok is not None
    PW = min(Vw, LANES)
    chunk = functools.partial(_scan_chunk, rev=rev, incl=incl, nsub=nsub)
    cols = [slice(p * PW, (p + 1) * PW) for p in range(Vw // PW)]
    own_qk = nsub > 1
    NB = PW if own_qk else N

    CPS = SCAN_CHUNKS_PER_STEP if nc % SCAN_CHUNKS_PER_STEP == 0 else 1
    ns = nc // CPS

    def order(c, backward):
        return (ns - 1 - c) if (rev != backward) else c

    def visit(backward):
        js = range(CPS) if rev == backward else range(CPS - 1, -1, -1)
        return [(j, slice(j * L, (j + 1) * L)) for j in js]

    def specs(backward):
        qs = pl.BlockSpec((CPS * L, N), lambda g, c: (order(c, backward), g))
        xs = pl.BlockSpec((CPS * L, Vw), lambda g, c: (order(c, backward), g))
        as_ = pl.BlockSpec((1, CPS * L, Hg), lambda g, c: (g, order(c, backward), 0))
        hs = pl.BlockSpec((1, CPS, NB, Vw), lambda g, c: (g, order(c, backward), 0, 0))
        return qs, xs, as_, hs

    def call_fwd(q, k, x, a_tok, dt_tok, y_prev=None):
        qs, xs, as_, hs = specs(False)
        n_in = 4 + use_dt + (y_prev is not None)

        def body(*refs):
            q_ref, k_ref, x_ref, a_ref = refs[:4]
            dt_ref = refs[4] if use_dt else None
            yp_ref = refs[n_in - 1] if y_prev is not None else None
            y_ref, hs_ref, h_scr = refs[n_in:]

            @pl.when(pl.program_id(1) == 0)
            def _():
                h_scr[...] = jnp.zeros_like(h_scr)

            for j, rs in visit(False):
                hs_ref[0, j] = h_scr[...]
                q, k, a, dt = q_ref[rs, :], k_ref[rs, :], a_ref[0, rs, :], dt_ref[0, rs, :] if use_dt else None
                qs, ks = ([q[:, c] for c in cols], [k[:, c] for c in cols]) if own_qk else ([q], [k])
                ys, hns = chunk(qs, ks, [x_ref[rs, c] for c in cols], a, dt, [h_scr[:, c] for c in cols])
                for c, y, hn in zip(cols, ys, hns):
                    y_ref[rs, c] = y if yp_ref is None else y + yp_ref[rs, c]
                    h_scr[:, c] = hn

        ins = [q, k, x, a_tok] + ([dt_tok] if use_dt else []) + ([y_prev] if y_prev is not None else [])
        return pl.pallas_call(
            body,
            out_shape=[jax.ShapeDtypeStruct((S, G * Vw), f32), jax.ShapeDtypeStruct((G, nc, NB, Vw), f32)],
            grid=(G, ns),
            in_specs=[qs, qs, xs, as_] + ([as_] if use_dt else []) + ([xs] if y_prev is not None else []),
            out_specs=[xs, hs],
            scratch_shapes=[pltpu.VMEM((NB, Vw), f32)],
            compiler_params=_cparams(("parallel", "arbitrary")),
            name="scan_fwd",
        )(*ins)

    def call_bwd(q, k, x, a_tok, dt_tok, hsave, dy, acc=None):
        qs, xs, as_, hs = specs(True)
        n_in = 6 + use_dt + (3 if acc is not None else 0)

        def body(*refs):
            q_ref, k_ref, x_ref, a_ref = refs[:4]
            dt_ref = refs[4] if use_dt else None
            hs_ref, dy_ref = refs[4 + use_dt], refs[5 + use_dt]
            acc_refs = refs[n_in - 3:n_in] if acc is not None else None
            dq_ref, dk_ref, dx_ref, da_ref = refs[n_in:n_in + 4]
            ddt_ref = refs[n_in + 4] if use_dt else None
            dh_scr = refs[-1]

            @pl.when(pl.program_id(1) == 0)
            def _():
                dh_scr[...] = jnp.zeros_like(dh_scr)

            for j, rs in visit(True):
                q, k, a = q_ref[rs, :].astype(f32), k_ref[rs, :].astype(f32), a_ref[0, rs, :]
                qs, ks = ([q[:, c] for c in cols], [k[:, c] for c in cols]) if own_qk else ([q], [k])
                xs, hs_in = [x_ref[rs, c] for c in cols], [hs_ref[0, j, :, c] for c in cols]
                if use_dt:
                    _, vjp = jax.vjp(chunk, qs, ks, xs, a, dt_ref[0, rs, :], hs_in)
                else:
                    _, vjp = jax.vjp(lambda qs, ks, xs, a, hs: chunk(qs, ks, xs, a, None, hs), qs, ks, xs, a, hs_in)
                cts = vjp(([dy_ref[rs, c] for c in cols], [dh_scr[:, c] for c in cols]))
                dqs, dks, dxs, da, dhs = cts[0], cts[1], cts[2], cts[3], cts[-1]
                if acc is not None:
                    dq_acc, dk_acc = acc_refs[0][rs, :].astype(f32), acc_refs[1][rs, :].astype(f32)
                if own_qk:
                    for b, c in enumerate(cols):
                        dq_ref[rs, c] = (dqs[b] if acc is None else dqs[b] + dq_acc[:, c]).astype(dq_ref.dtype)
                        dk_ref[rs, c] = (dks[b] if acc is None else dks[b] + dk_acc[:, c]).astype(dk_ref.dtype)
                else:
                    dq_ref[rs, :] = (dqs[0] if acc is None else dqs[0] + dq_acc).astype(dq_ref.dtype)
                    dk_ref[rs, :] = (dks[0] if acc is None else dks[0] + dk_acc).astype(dk_ref.dtype)
                for b, c in enumerate(cols):
                    dx_ref[rs, c] = dxs[b] if acc is None else dxs[b] + acc_refs[2][rs, c]
                    dh_scr[:, c] = dhs[b]
                da_ref[0, rs, :] = da
                if use_dt:
                    ddt_ref[0, rs, :] = cts[4]

        ins = [q, k, x, a_tok] + ([dt_tok] if use_dt else []) + [hsave, dy] + (list(acc) if acc is not None else [])
        a_shape = jax.ShapeDtypeStruct(a_tok.shape, f32)
        return pl.pallas_call(
            body,
            out_shape=[jax.ShapeDtypeStruct(q.shape, q.dtype), jax.ShapeDtypeStruct(k.shape, k.dtype),
                       jax.ShapeDtypeStruct(x.shape, f32), a_shape] + ([a_shape] if use_dt else []),
            grid=(G, ns),
            in_specs=[qs, qs, xs, as_] + ([as_] if use_dt else []) + [hs, xs] + ([qs, qs, xs] if acc is not None else []),
            out_specs=[qs, qs, xs, as_] + ([as_] if use_dt else []),
            scratch_shapes=[pltpu.VMEM((NB, Vw), f32)],
            compiler_params=_cparams(("parallel", "arbitrary")),
            name="scan_bwd",
        )(*ins)

    return call_fwd, call_bwd


def bidir_scan(q, k, x, a_f, a_b, dt_f, dt_b, *, nsub):
    use_dt = dt_f is not None
    a_f, a_b = chunk_cumsum(a_f, False), chunk_cumsum(a_b, True)
    fwd_f, bwd_f = scan_op(q, k, x, a_f, dt_f, rev=False, incl=True, nsub=nsub)
    fwd_b, bwd_b = scan_op(q, k, x, a_b, dt_b, rev=True, incl=False, nsub=nsub)

    def run(q, k, x, a_f, a_b, dt_f, dt_b):
        y_f, hs_f = fwd_f(q, k, x, a_f, dt_f)
        y, hs_b = fwd_b(q, k, x, a_b, dt_b, y_prev=y_f)
        return y, (hs_f, hs_b)

    def grads(q, k, x, a_f, a_b, dt_f, dt_b, hs, dy):
        first = bwd_f(q, k, x, a_f, dt_f, hs[0], dy)
        both = bwd_b(q, k, x, a_b, dt_b, hs[1], dy, acc=first[:3])
        return both[0], both[1], both[2], first[3], both[3], (first[4] if use_dt else None), (both[4] if use_dt else None)

    if use_dt:
        @jax.custom_vjp
        def op(q, k, x, a_f, a_b, dt_f, dt_b):
            return run(q, k, x, a_f, a_b, dt_f, dt_b)[0]

        def fwd(q, k, x, a_f, a_b, dt_f, dt_b):
            y, hs = run(q, k, x, a_f, a_b, dt_f, dt_b)
            return y, (q, k, x, a_f, a_b, dt_f, dt_b, hs)

        def bwd(res, dy):
            return grads(*res, dy)

        op.defvjp(fwd, bwd)
        return op(q, k, x, a_f, a_b, dt_f, dt_b)

    @jax.custom_vjp
    def op(q, k, x, a_f, a_b):
        return run(q, k, x, a_f, a_b, None, None)[0]

    def fwd(q, k, x, a_f, a_b):
        y, hs = run(q, k, x, a_f, a_b, None, None)
        return y, (q, k, x, a_f, a_b, hs)

    def bwd(res, dy):
        q, k, x, a_f, a_b, hs = res
        return grads(q, k, x, a_f, a_b, None, None, hs, dy)[:5]

    op.defvjp(fwd, bwd)
    return op(q, k, x, a_f, a_b)


def _swap_halves(x, dh):
    W = x.shape[1]
    lane = lax.broadcasted_iota(jnp.int32, (1, W), 1) % dh
    return jnp.where(lane < dh // 2, pltpu.roll(x, W - dh // 2, 1), pltpu.roll(x, dh // 2, 1))


def rotary(rq, rk, cos_t, sin_t):
    scale = RET_DH ** -0.5

    def fn(rq, rk, c, s):
        return rq * c + _swap_halves(rq, RET_DH) * s, (rk * c + _swap_halves(rk, RET_DH) * s) * scale

    def bwd_fn(rv, pv, dos):
        _, _, c, s = rv
        dq, dk = dos
        dk = dk * scale
        return (dq * c + _swap_halves(dq * s, RET_DH), dk * c + _swap_halves(dk * s, RET_DH)), ()

    return rowwise(fn, "rotary", [rq, rk, cos_t, sin_t], [], [MXU_DTYPE, MXU_DTYPE], n_diff_rows=2, bwd_fn=bwd_fn)


def _rope_tables(S, width):
    half = RET_DH // 2
    inv = 1.0 / (ROPE_BASE ** (jnp.arange(half, dtype=f32) / half))
    ang = jnp.arange(S, dtype=f32)[:, None] * inv[None, :]
    cos, sin = jnp.cos(ang), jnp.sin(ang)
    reps = width // RET_DH
    return jnp.tile(jnp.concatenate([cos, cos], axis=1), (1, reps)), jnp.tile(jnp.concatenate([-sin, sin], axis=1), (1, reps))


def _exact_dot(x, m):
    return jnp.dot(x, m, precision=lax.Precision.HIGHEST, preferred_element_type=f32)


def ret_post(y, rg, gn_g):
    W = y.shape[1]
    idx = np.arange(W) // RET_DH
    avg = jnp.asarray((idx[:, None] == idx[None, :]).astype(np.float32) / RET_DH)

    def fn(y, rg, g, avg):
        mu = _exact_dot(y, avg)
        d = y - mu
        var = _exact_dot(d * d, avg)
        return (_silu(rg) * (d * lax.rsqrt(var + EPS) * g),)

    return rowwise(fn, "ret_post", [y, rg], [gn_g.reshape(1, -1), avg], [MXU_DTYPE], n_diff_params=1)[0]


def _na_bias(rpb, win_r):
    H = rpb.shape[0]
    qc = np.arange(GRID_W)[:, None]
    kc = np.arange(GRID_W)[None, :]
    cstart = np.clip(qc - NA_WIN_C // 2, 0, GRID_W - NA_WIN_C)
    valid = (kc >= cstart) & (kc < cstart + NA_WIN_C)
    dc = np.clip(kc - qc, -(NA_WIN_C - 1), NA_WIN_C - 1) + (NA_WIN_C - 1)
    onehot = (dc[None] == np.arange(2 * NA_WIN_C - 1)[:, None, None]).astype(np.float32)
    t1 = jnp.einsum("hrd,dqk->hrqk", rpb.astype(f32), jnp.asarray(onehot), precision=lax.Precision.HIGHEST)
    per_delta = [t1[:, NA_WIN_R - 1 - d:NA_WIN_R - 1 - d + win_r] for d in range(win_r)]
    b = jnp.stack(per_delta, axis=1)
    b = jnp.where(jnp.asarray(valid)[None, None, None], b, NEG_INF)
    return jnp.transpose(b, (0, 1, 3, 2, 4)).reshape(H, win_r, GRID_W, win_r * GRID_W)


def _na_rows(rows):
    lane = lax.broadcasted_iota(jnp.int32, (1, rows[0][0].shape[1]), 1) // NA_DH
    scale = NA_DH ** -0.5
    ss = [[_bdot_raw(jnp.where(lane == i, q, 0.0) * scale, kw, "nt") + b for i, b in enumerate(bs)] for q, kw, _, bs in rows]
    es = [[jnp.exp(s - jnp.max(s, axis=1, keepdims=True)) for s in srow] for srow in ss]
    ps = [[e / jnp.sum(e, axis=1, keepdims=True) for e in erow] for erow in es]
    return [_lanes_by_head(lane, [_bdot_raw(p, vw, "nn") for p in prow]) for prow, (_, _, vw, _) in zip(ps, rows)]


def _lanes_by_head(lane, vals):
    if len(vals) == 2:
        return jnp.where(lane == 0, vals[0], vals[1])
    return sum(jnp.where(lane == i, v, 0.0) for i, v in enumerate(vals))


def _na_rows_bwd(rows):
    lane = lax.broadcasted_iota(jnp.int32, (1, rows[0][0].shape[1]), 1) // NA_DH
    scale = NA_DH ** -0.5
    heads = range(len(rows[0][3]))
    qis = [[jnp.where(lane == i, q, 0.0) * scale for i in heads] for q, _, _, _, _ in rows]
    dos = [[jnp.where(lane == i, do, 0.0) for i in heads] for _, _, _, _, do in rows]
    ss = [[_bdot_raw(qi, kw, "nt") + b for qi, b in zip(qrow, bs)] for qrow, (_, kw, _, bs, _) in zip(qis, rows)]
    dps = [[_bdot_raw(doi, vw, "nt") for doi in drow] for drow, (_, _, vw, _, _) in zip(dos, rows)]
    es = [[jnp.exp(s - jnp.max(s, axis=1, keepdims=True)) for s in srow] for srow in ss]
    ps = [[e / jnp.sum(e, axis=1, keepdims=True) for e in erow] for erow in es]
    dss = [[p * (dp - jnp.sum(dp * p, axis=1, keepdims=True)) for p, dp in zip(prow, dprow)] for prow, dprow in zip(ps, dps)]
    out = []
    for qrow, drow, prow, dsrow, (_, kw, _, _, _) in zip(qis, dos, ps, dss, rows):
        dq = _lanes_by_head(lane, [_bdot_raw(dsrow[i], kw, "nn") for i in heads]) * scale
        dk, dv = 0.0, 0.0
        for i in heads:
            dk = dk + _bdot_raw(dsrow[i], qrow[i], "tn")
            dv = dv + _bdot_raw(prow[i], drow[i], "tn")
        out.append((dq, dk, dv, dsrow))
    return out


def na_op(nq, nk, nv, bias):
    S, W = nq.shape
    rows = S // GRID_W
    win_r = bias.shape[1]
    nkeys = win_r * GRID_W
    hp = LANES // NA_DH
    npair = W // LANES
    RB = min(16, rows)
    nrb = rows // RB
    qspec = pl.BlockSpec((RB * GRID_W, LANES), lambda p, r: (r, p))
    kspec = pl.BlockSpec((S, LANES), lambda p, r: (0, p))
    bspec = pl.BlockSpec((hp, win_r, GRID_W, nkeys), lambda p, r: (p, 0, 0, 0))

    def window(r):
        r0 = jnp.clip(r - win_r // 2, 0, rows - win_r)
        return pl.multiple_of(r0 * GRID_W, GRID_W), r - r0

    def call_fwd(nq, nk, nv, bias):
        def body(q_ref, k_ref, v_ref, b_ref, o_ref):
            rb = pl.program_id(1)

            def step(j, c):
                args, q0s = [], []
                for u in range(NA_ROWS_PER_STEP):
                    i = j * NA_ROWS_PER_STEP + u
                    k0, d = window(rb * RB + i)
                    q0 = pl.multiple_of(i * GRID_W, GRID_W)
                    q0s.append(q0)
                    args.append((q_ref[pl.ds(q0, GRID_W), :].astype(f32), k_ref[pl.ds(k0, nkeys), :], v_ref[pl.ds(k0, nkeys), :],
                                 [b_ref[h, pl.ds(d, 1)][0] for h in range(hp)]))
                for q0, o in zip(q0s, _na_rows(args)):
                    o_ref[pl.ds(q0, GRID_W), :] = o.astype(o_ref.dtype)
                return c

            lax.fori_loop(0, RB // NA_ROWS_PER_STEP, step, 0)

        return pl.pallas_call(
            body,
            out_shape=jax.ShapeDtypeStruct((S, W), nq.dtype),
            grid=(npair, nrb),
            in_specs=[qspec, kspec, kspec, bspec],
            out_specs=qspec,
            compiler_params=_cparams(("parallel", "arbitrary")),
            name="na_fwd",
        )(nq, nk, nv, bias)

    def call_bwd(nq, nk, nv, bias, do):
        def body(q_ref, k_ref, v_ref, b_ref, do_ref, dq_ref, dk_ref, dv_ref, db_ref, dk_acc, dv_acc):
            rb = pl.program_id(1)

            @pl.when(rb == 0)
            def _():
                dk_acc[...] = jnp.zeros_like(dk_acc)
                dv_acc[...] = jnp.zeros_like(dv_acc)
                db_ref[...] = jnp.zeros_like(db_ref)

            def step(j, c):
                args, spots = [], []
                for u in range(NA_ROWS_PER_STEP):
                    i = j * NA_ROWS_PER_STEP + u
                    k0, d = window(rb * RB + i)
                    q0 = pl.multiple_of(i * GRID_W, GRID_W)
                    spots.append((q0, k0, d))
                    args.append((q_ref[pl.ds(q0, GRID_W), :].astype(f32), k_ref[pl.ds(k0, nkeys), :], v_ref[pl.ds(k0, nkeys), :],
                                 [b_ref[h, pl.ds(d, 1)][0] for h in range(hp)], do_ref[pl.ds(q0, GRID_W), :].astype(f32)))
                for (q0, k0, d), (dq, dk, dv, dbs) in zip(spots, _na_rows_bwd(args)):
                    dq_ref[pl.ds(q0, GRID_W), :] = dq.astype(dq_ref.dtype)
                    dk_acc[pl.ds(k0, nkeys), :] += dk
                    dv_acc[pl.ds(k0, nkeys), :] += dv
                    for h in range(hp):
                        db_ref[h, pl.ds(d, 1)] += dbs[h][None]
                return c

            lax.fori_loop(0, RB // NA_ROWS_PER_STEP, step, 0)

            @pl.when(rb == nrb - 1)
            def _():
                dk_ref[...] = dk_acc[...].astype(dk_ref.dtype)
                dv_ref[...] = dv_acc[...].astype(dv_ref.dtype)

        return pl.pallas_call(
            body,
            out_shape=[jax.ShapeDtypeStruct((S, W), nq.dtype), jax.ShapeDtypeStruct((S, W), nk.dtype),
                       jax.ShapeDtypeStruct((S, W), nv.dtype), jax.ShapeDtypeStruct(bias.shape, f32)],
            grid=(npair, nrb),
            in_specs=[qspec, kspec, kspec, bspec, qspec],
            out_specs=[qspec, kspec, kspec, bspec],
            scratch_shapes=[pltpu.VMEM((S, LANES), f32), pltpu.VMEM((S, LANES), f32)],
            compiler_params=_cparams(("parallel", "arbitrary")),
            name="na_bwd",
        )(nq, nk, nv, bias, do)

    @jax.custom_vjp
    def op(nq, nk, nv, bias):
        return call_fwd(nq, nk, nv, bias)

    def fwd(nq, nk, nv, bias):
        return call_fwd(nq, nk, nv, bias), (nq, nk, nv, bias)

    def bwd(res, do):
        return tuple(call_bwd(*res, do))

    op.defvjp(fwd, bwd)
    return op(nq, nk, nv, bias)


def ssd_dt(dt_raw, dt_bias, a_neg):
    def fn(r, b, a):
        dt = _softplus(r + b)
        return dt, dt * a

    return rowwise(fn, "ssd_dt", [dt_raw], [dt_bias, a_neg], [f32, f32])


def ssd_post(y, xs, z, d_skip_lanes, norm_g, groups):
    def fn(y, xs, z, dsk, g):
        y = (y + xs * dsk) * _silu(z)
        return (y * lax.rsqrt(jnp.mean(y * y, axis=-1, keepdims=True) + EPS) * g,)

    def bwd_fn(rv, pv, dos):
        (y, xs, z), (dsk, g), (do,) = rv, pv, dos
        sg = 1.0 / (1.0 + jnp.exp(-z))
        s, u = z * sg, y + xs * dsk
        dw, dg = _rms_bwd(u * s, g, do)
        du = dw * s
        dz = dw * u * (sg * (1.0 + z * (1.0 - sg)))
        return (du, du * dsk, dz), (jnp.sum(du * xs, axis=0, keepdims=True), dg)

    return rowwise(fn, "ssd_post", [y, xs, z], [d_skip_lanes.reshape(1, -1), norm_g.reshape(1, -1)], [MXU_DTYPE],
                   ncol=groups, bwd_fn=bwd_fn)[0]


def _heads_major(t, groups):
    S = t.shape[0]
    return jnp.transpose(t.reshape(S, groups, -1), (1, 0, 2))


def retention_na_mixer(hn, w_in, decay_logit, gn_g, rpb, w_out, tables):
    S = hn.shape[0]
    R = RET_HEADS * RET_DH
    NW = NA_HEADS * NA_DH
    cols = lambda a, b: w_in[:, a:b]
    rq, rk, rv, rg = (mm(hn, cols(j * R, (j + 1) * R)) for j in range(4))
    nq, nk, nv = (mm(hn, cols(4 * R + j * NW, 4 * R + (j + 1) * NW), out_dtype=MXU_DTYPE) for j in range(3))
    qr, kr = rotary(rq, rk, *tables)
    log_gamma = -_softplus(-decay_logit.astype(f32))
    hp = LANES // RET_DH
    hpad = -(-RET_HEADS // 8) * 8
    pad8 = lambda a: jnp.pad(a.reshape(1, 1, RET_HEADS), ((0, 0), (0, 0), (0, hpad - RET_HEADS)))
    a_f = jnp.broadcast_to(pad8(log_gamma[0]), (1, S, hpad))
    a_b = jnp.broadcast_to(pad8(log_gamma[1]), (1, S, hpad))
    ret = ret_post(bidir_scan(qr, kr, rv, a_f, a_b, None, None, nsub=hp), rg, gn_g)
    rows = S // GRID_W
    nao = na_op(nq, nk, nv, _na_bias(rpb, min(NA_WIN_R, rows)))
    return mm(ret, w_out[:R]) + mm(nao, w_out[R:])


def ssd_mixer(hn, w_in, conv_w, conv_b, dt_bias, a_log, d_skip, norm_g, w_out):
    heads = d_skip.shape[0]
    inner = heads * SSD_HEADDIM
    gs = SSD_GROUPS * SSD_STATE
    o_x, o_b, o_c, o_dt = inner, 2 * inner, 2 * inner + gs, 2 * inner + 2 * gs
    z = mm(hn, w_in[:, :inner])
    dt_raw = mm(hn, w_in[:, o_dt:])
    xs, bm, cm = (mm_conv_act(hn, [w_in[:, a:b]], [conv_w[:, a - inner:b - inner]], [conv_b[a - inner:b - inner]],
                              _silu, _silu_bwd, f32, "conv_silu") for a, b in ((o_x, o_b), (o_b, o_c), (o_c, o_dt)))
    a_neg = -jnp.exp(a_log.astype(f32)).reshape(1, -1)
    dt, la = ssd_dt(dt_raw, dt_bias.astype(f32).reshape(1, -1), a_neg)
    dt_f, dt_b = _heads_major(dt[:, :heads], SSD_GROUPS), _heads_major(dt[:, heads:], SSD_GROUPS)
    la_f, la_b = _heads_major(la[:, :heads], SSD_GROUPS), _heads_major(la[:, heads:], SSD_GROUPS)
    y = bidir_scan(cm, bm, xs, la_f, la_b, dt_f, dt_b, nsub=1)
    y = ssd_post(y, xs, z, jnp.repeat(d_skip.astype(f32), SSD_HEADDIM), norm_g, SSD_GROUPS)
    return mm(y, w_out)


def conv_geglu_ffn(hf, w_up, conv_w, conv_b, w_down):
    F = w_down.shape[0]
    a = mm_conv_act(hf, [w_up[:, :F], w_up[:, F:]], [conv_w[:, :F], conv_w[:, F:]], [conv_b[:F], conv_b[F:]],
                    _geglu, _geglu_bwd, MXU_DTYPE, "conv_geglu")
    return mm(a, w_down)


def model_loss(x, tgt, big, small, rep):
    S = x.shape[0]
    depth = rep["norm_mix_pre"].shape[0]
    tables = _rope_tables(S, RET_HEADS * RET_DH)
    hn = rms(x, rep["norm_mix_pre"][0], MXU_DTYPE)
    for layer in range(depth):
        i = layer // 2
        if layer % 2 == 0:
            m = retention_na_mixer(hn, big["ab_w_in"][i], rep["ab_ret_decay_logit"][i], rep["ab_ret_gn_g"][i],
                                   rep["ab_na_rpb"][i], big["ab_w_out"][i], tables)
        else:
            m = ssd_mixer(hn, big["c_w_in"][i], small["c_conv_w"][i], small["c_conv_b"][i], rep["c_dt_bias"][i],
                          rep["c_a_log"][i], rep["c_d_skip"][i], small["c_norm_g"][i], big["c_w_out"][i])
        x, hf = rms_residual_norm(m, rep["norm_mix_post"][layer], x, rep["norm_ffn_pre"][layer])
        f = conv_geglu_ffn(hf, big["ffn_w_up"][layer], small["ffn_conv_w"][layer], rep["ffn_conv_b"][layer],
                           big["ffn_w_down"][layer])
        if layer + 1 < depth:
            x, hn = rms_residual_norm(f, rep["norm_ffn_post"][layer], x, rep["norm_mix_pre"][layer + 1])
        else:
            x = rms_residual(f, rep["norm_ffn_post"][layer], x)
    return loss_op(x, tgt)


def _mesh_pos():
    return lax.axis_index("x"), lax.axis_index("y"), lax.axis_index("c")


def _any_specs(n):
    return [pl.BlockSpec(memory_space=pl.ANY)] * n


def gather_chips(locals_):
    nbuf = len(locals_)
    CH = COPY_CHUNKS

    def body(*refs):
        x_refs, out_refs, (send_sems, recv_sems) = refs[:nbuf], refs[nbuf:2 * nbuf], refs[2 * nbuf:]
        x, y, c = _mesh_pos()
        my = 2 * x + y
        chips = [(1 - x, y), (x, 1 - y), (1 - x, 1 - y)]
        plans = []
        for a, (x_ref, out_ref) in enumerate(zip(x_refs, out_refs)):
            half = x_ref.shape[0] // 2
            q = half // CH

            def piece(ref, h, j, half=half, q=q):
                return ref.at[pl.ds(pl.multiple_of(h * half + j * q, PACK_ALIGN), q), :]

            def copy(k, src, chip, h, j, to, out_ref=out_ref, piece=piece, base=a * 6 * CH):
                return pltpu.make_async_remote_copy(src_ref=src, dst_ref=piece(out_ref.at[chip], h, j),
                                                    send_sem=send_sems.at[base + k], recv_sem=recv_sems.at[base + k],
                                                    device_id=to, device_id_type=pl.DeviceIdType.MESH)

            plans.append((x_ref, out_ref, piece, copy))

        first, passed = [], []
        for x_ref, out_ref, piece, copy in plans:
            first.append([[copy(k * CH + j, piece(x_ref, c, j), my, c, j, (cx, cy, c)) for j in range(CH)]
                          for k, (cx, cy) in enumerate(chips)])
            passed.append([[copy((3 + k) * CH + j, piece(out_ref.at[2 * cx + cy], c, j), 2 * cx + cy, c, j, (x, y, 1 - c))
                            for j in range(CH)] for k, (cx, cy) in enumerate(chips)])
        for a in range(nbuf):
            for j in range(CH):
                for k in range(3):
                    first[a][k][j].start()
        for a, (x_ref, _, piece, copy) in enumerate(plans):
            for j in range(CH):
                for k, (cx, cy) in enumerate(chips):
                    copy(k * CH + j, piece(x_ref, c, j), 2 * cx + cy, c, j, (cx, cy, c)).wait_recv()
                    passed[a][k][j].start()
        for a, (x_ref, _, piece, copy) in enumerate(plans):
            for j in range(CH):
                for k, (cx, cy) in enumerate(chips):
                    copy((3 + k) * CH + j, piece(x_ref, c, j), 2 * cx + cy, 1 - c, j, (x, y, 1 - c)).wait_recv()
        for a in range(nbuf):
            for k in range(3):
                for cp in first[a][k] + passed[a][k]:
                    cp.wait_send()

    return pl.pallas_call(
        body,
        out_shape=[jax.ShapeDtypeStruct((N_CHIPS,) + l.shape, l.dtype) for l in locals_],
        in_specs=_any_specs(nbuf),
        out_specs=_any_specs(nbuf),
        scratch_shapes=[pltpu.SemaphoreType.DMA((nbuf * 6 * CH,)), pltpu.SemaphoreType.DMA((nbuf * 6 * CH,))],
        name="gather_chips",
    )(*locals_)


def pair_swap(parts):
    nbuf = len(parts)
    n = N_CHIPS
    CH = COPY_CHUNKS

    def body(*refs):
        p_refs, got_refs, (send_sems, recv_sems) = refs[:nbuf], refs[nbuf:2 * nbuf], refs[2 * nbuf:]
        x, y, c = _mesh_pos()
        swap = []
        for a, (p_ref, got_ref) in enumerate(zip(p_refs, got_refs)):
            half = p_ref.shape[1] // 2
            q = half // CH
            for s in range(n):
                for j in range(CH):
                    k = (a * n + s) * CH + j
                    src = p_ref.at[s, pl.ds(pl.multiple_of((1 - c) * half + j * q, PACK_ALIGN), q), :]
                    swap.append(pltpu.make_async_remote_copy(src_ref=src, dst_ref=got_ref.at[s, pl.ds(j * q, q), :],
                                                             send_sem=send_sems.at[k], recv_sem=recv_sems.at[k],
                                                             device_id=(x, y, 1 - c), device_id_type=pl.DeviceIdType.MESH))
        for cp in swap:
            cp.start()
        for cp in swap:
            cp.wait()

    return pl.pallas_call(
        body,
        out_shape=[jax.ShapeDtypeStruct((n, p.shape[1] // 2, p.shape[2]), p.dtype) for p in parts],
        in_specs=_any_specs(nbuf),
        out_specs=_any_specs(nbuf),
        scratch_shapes=[pltpu.SemaphoreType.DMA((nbuf * n * CH,)), pltpu.SemaphoreType.DMA((nbuf * n * CH,))],
        name="pair_swap",
    )(*parts)


def chip_exchange(parts):
    nbuf = len(parts)

    def body(*refs):
        p_refs, out_refs, (send_sems, recv_sems) = refs[:nbuf], refs[nbuf:2 * nbuf], refs[2 * nbuf:]
        x, y, c = _mesh_pos()
        my = 2 * x + y
        chips = [(1 - x, y), (x, 1 - y), (1 - x, 1 - y)]

        def copy(a, k, src_slot, dst_slot, to):
            return pltpu.make_async_remote_copy(src_ref=p_refs[a].at[src_slot], dst_ref=out_refs[a].at[dst_slot],
                                                send_sem=send_sems.at[3 * a + k], recv_sem=recv_sems.at[3 * a + k],
                                                device_id=to, device_id_type=pl.DeviceIdType.MESH)

        sends = [copy(a, k, 2 * cx + cy, my, (cx, cy, c)) for a in range(nbuf) for k, (cx, cy) in enumerate(chips)]
        for cp in sends:
            cp.start()
        for a in range(nbuf):
            for k, (cx, cy) in enumerate(chips):
                copy(a, k, my, 2 * cx + cy, (cx, cy, c)).wait_recv()
        for cp in sends:
            cp.wait_send()

    return pl.pallas_call(
        body,
        out_shape=[jax.ShapeDtypeStruct(p.shape, p.dtype) for p in parts],
        in_specs=_any_specs(nbuf),
        out_specs=_any_specs(nbuf),
        scratch_shapes=[pltpu.SemaphoreType.DMA((3 * nbuf,)), pltpu.SemaphoreType.DMA((3 * nbuf,))],
        name="chip_exchange",
    )(*parts)


def pair_share(mine):
    nbuf = len(mine)
    CH = COPY_CHUNKS

    def body(*refs):
        m_refs, out_refs, (send_sems, recv_sems) = refs[:nbuf], refs[nbuf:2 * nbuf], refs[2 * nbuf:]
        x, y, c = _mesh_pos()
        swap = []
        for a, (m_ref, out_ref) in enumerate(zip(m_refs, out_refs)):
            q = m_ref.shape[0] // CH
            for j in range(CH):
                swap.append(pltpu.make_async_remote_copy(src_ref=m_ref.at[pl.ds(j * q, q), :], dst_ref=out_ref.at[pl.ds(j * q, q), :],
                                                         send_sem=send_sems.at[a * CH + j], recv_sem=recv_sems.at[a * CH + j],
                                                         device_id=(x, y, 1 - c), device_id_type=pl.DeviceIdType.MESH))
        for cp in swap:
            cp.start()
        for cp in swap:
            cp.wait()

    return pl.pallas_call(
        body,
        out_shape=[jax.ShapeDtypeStruct(m.shape, m.dtype) for m in mine],
        in_specs=_any_specs(nbuf),
        out_specs=_any_specs(nbuf),
        scratch_shapes=[pltpu.SemaphoreType.DMA((nbuf * CH,)), pltpu.SemaphoreType.DMA((nbuf * CH,))],
        name="pair_share",
    )(*mine)


def sum_chips(recv, own):
    n, R, Wd = recv.shape
    tr = _pick(R, (512, 256, 128, 64, 32, 16, 8))

    def body(r_ref, p_ref, o_ref):
        my = 2 * lax.axis_index("x") + lax.axis_index("y")
        acc = jnp.zeros((tr, Wd), f32)
        for s in range(n):
            acc = acc + jnp.where(my == s, p_ref[s], r_ref[s]).astype(f32)
        o_ref[...] = acc

    spec = pl.BlockSpec((n, tr, Wd), lambda i: (0, i, 0))
    return pl.pallas_call(
        body,
        out_shape=jax.ShapeDtypeStruct((R, Wd), f32),
        grid=(R // tr,),
        in_specs=[spec, spec],
        out_specs=pl.BlockSpec((tr, Wd), lambda i: (i, 0)),
        compiler_params=_cparams(("parallel",)),
        name="sum_chips",
    )(recv, own)


def add_pair(parts, got):
    n, R, Wd = parts.shape
    half = R // 2
    tr = _pick(half, (512, 256, 128, 64, 32, 16, 8))
    nb = half // tr

    def body(lo_ref, hi_ref, g_ref, o_ref):
        mine = jnp.where(lax.axis_index("c") == 0, lo_ref[...], hi_ref[...])
        o_ref[...] = (mine.astype(f32) + g_ref[...].astype(f32)).astype(o_ref.dtype)

    spec = pl.BlockSpec((1, tr, Wd), lambda s, i: (s, i, 0))
    return pl.pallas_call(
        body,
        out_shape=jax.ShapeDtypeStruct(got.shape, parts.dtype),
        grid=(n, nb),
        in_specs=[spec, pl.BlockSpec((1, tr, Wd), lambda s, i: (s, nb + i, 0)), spec],
        out_specs=spec,
        compiler_params=_cparams(("parallel", "parallel")),
        name="add_pair",
    )(parts, parts, got)


def reduce_scatter(parts):
    chip_sum = [add_pair(p, g) for p, g in zip(parts, pair_swap(parts))]
    mine = [sum_chips(r, s) for r, s in zip(chip_exchange(chip_sum), chip_sum)]
    first = lax.axis_index("c") == 0
    return [jnp.concatenate([jnp.where(first, m, t), jnp.where(first, t, m)], axis=0) for m, t in zip(mine, pair_share(mine))]


def adamw(w, g, m, v):
    shp = w.shape
    if w.size * 4 <= (1 << 20):
        grid, block, imap = (1,), shp, lambda i: (0,) * len(shp)
    else:
        n0, R, C = shp
        tr = _divisor_tile(R, lambda t: t * C * 4 <= (1 << 20), 8)
        grid, block, imap = (n0, R // tr), (1, tr, C), lambda j, i: (j, i, 0)

    def body(w_ref, g_ref, m_ref, v_ref, d_ref, mo_ref, vo_ref):
        g = g_ref[...]
        m = ADAM_B1 * m_ref[...] + (1.0 - ADAM_B1) * g
        v = ADAM_B2 * v_ref[...] + (1.0 - ADAM_B2) * (g * g)
        m_hat = m / (1.0 - ADAM_B1 ** ADAM_STEP)
        v_hat = v / (1.0 - ADAM_B2 ** ADAM_STEP)
        d_ref[...] = -ADAM_LR * (m_hat / (jnp.sqrt(v_hat) + ADAM_EPS) + ADAM_WD * w_ref[...])
        mo_ref[...] = m
        vo_ref[...] = v

    spec = pl.BlockSpec(block, imap)
    return pl.pallas_call(
        body,
        out_shape=[jax.ShapeDtypeStruct(shp, f32)] * 3,
        grid=grid,
        in_specs=[spec] * 4,
        out_specs=[spec] * 3,
        compiler_params=_cparams(("parallel",) * len(grid)),
        name="adamw",
    )(w, g, m, v)


def _pack(arrs, dtype):
    flat = jnp.concatenate([a.astype(dtype).reshape(-1) for a in arrs])
    n = flat.shape[0]
    unit = PACK_W * PACK_ROWS
    padded = -(-n // unit) * unit
    return jnp.pad(flat, (0, padded - n)).reshape(-1, PACK_W)


def _unpack(buf, shapes):
    flat = buf.reshape(-1)
    out, off = [], 0
    for s in shapes:
        n = int(np.prod(s))
        out.append(flat[off:off + n].reshape(s))
        off += n
    return out


BIG = (("ab_w_in", 2), ("ab_w_out", 1), ("c_w_in", 2), ("c_w_out", 1), ("ffn_w_up", 2), ("ffn_w_down", 1))
SMALL = (("c_conv_w", 2), ("c_conv_b", 1), ("c_norm_g", 1), ("ffn_conv_w", 2))
REP = ("norm_mix_pre", "norm_mix_post", "norm_ffn_pre", "norm_ffn_post", "ab_ret_decay_logit", "ab_ret_gn_g", "ab_na_rpb",
       "c_dt_bias", "c_a_log", "c_d_skip", "ffn_conv_b")
WEIGHTS = ("norm_mix_pre", "norm_mix_post", "norm_ffn_pre", "norm_ffn_post", "ab_w_in", "ab_ret_decay_logit", "ab_ret_gn_g",
           "ab_na_rpb", "ab_w_out", "c_w_in", "c_conv_w", "c_conv_b", "c_dt_bias", "c_a_log", "c_d_skip", "c_norm_g", "c_w_out",
           "ffn_w_up", "ffn_conv_w", "ffn_conv_b", "ffn_w_down")


BIG_GROUPS = (("ab_w_in",), ("ab_w_out", "c_w_out", "ffn_w_down"), ("c_w_in",), ("ffn_w_up",))
BIG_AXIS = dict(BIG)
ROW_UNIT = 128


def _rows(arrs, dtype):
    C = arrs[0].shape[-1]
    buf = jnp.concatenate([a.astype(dtype).reshape(-1, C) for a in arrs], axis=0)
    pad = -buf.shape[0] % ROW_UNIT
    return jnp.pad(buf, ((0, pad), (0, 0))) if pad else buf


def _unrows(buf, shapes):
    out, off = [], 0
    for s in shapes:
        n = int(np.prod(s[:-1]))
        out.append(buf[off:off + n].reshape(s))
        off += n
    return out


def _gather_all(w):
    bufs = [_rows([w[n] for n in grp], MXU_DTYPE) for grp in BIG_GROUPS] + [_pack([w[n] for n, _ in SMALL], f32)]
    got = gather_chips(bufs)
    my = 2 * lax.axis_index("x") + lax.axis_index("y")

    def whole(n, ax, dtype, pieces):
        return jnp.concatenate([jnp.where(my == s, w[n].astype(dtype), pieces[s]) for s in range(N_CHIPS)], axis=ax)

    big = {}
    for grp, g in zip(BIG_GROUPS, got):
        per_chip = [_unrows(g[s], [w[n].shape for n in grp]) for s in range(N_CHIPS)]
        for j, n in enumerate(grp):
            big[n] = whole(n, BIG_AXIS[n], MXU_DTYPE, [per_chip[s][j] for s in range(N_CHIPS)])
    per_chip = [_unpack(got[-1][s], [w[n].shape for n, _ in SMALL]) for s in range(N_CHIPS)]
    small = {n: whole(n, ax, f32, [per_chip[s][j] for s in range(N_CHIPS)]) for j, (n, ax) in enumerate(SMALL)}
    return big, small


def _reduce_all(gbig, gsmall, grep, w):
    split = {n: jnp.split(g, N_CHIPS, axis=BIG_AXIS[n]) for n, g in gbig.items()}
    parts = [jnp.stack([_rows([split[n][s] for n in grp], MXU_DTYPE) for s in range(N_CHIPS)]) for grp in BIG_GROUPS]
    ssplit = {n: jnp.split(gsmall[n], N_CHIPS, axis=ax) for n, ax in SMALL}
    parts.append(jnp.stack([_pack([ssplit[n][s] for n, _ in SMALL] + [grep[n] for n in REP], f32) for s in range(N_CHIPS)]))
    res = reduce_scatter(parts)
    grads = {}
    for grp, r in zip(BIG_GROUPS, res):
        grads.update(zip(grp, _unrows(r, [w[n].shape for n in grp])))
    small_names = [n for n, _ in SMALL] + list(REP)
    grads.update(zip(small_names, _unpack(res[-1], [w[n].shape for n in small_names])))
    return grads


def kernel(x, norm_mix_pre, norm_mix_post, norm_ffn_pre, norm_ffn_post, ab_w_in, ab_ret_decay_logit, ab_ret_gn_g, ab_na_rpb, ab_w_out, c_w_in, c_conv_w, c_conv_b, c_dt_bias, c_a_log, c_d_skip, c_norm_g, c_w_out, ffn_w_up, ffn_conv_w, ffn_conv_b, ffn_w_down, loss_target, m_norm_mix_pre, m_norm_mix_post, m_norm_ffn_pre, m_norm_ffn_post, m_ab_w_in, m_ab_ret_decay_logit, m_ab_ret_gn_g, m_ab_na_rpb, m_ab_w_out, m_c_w_in, m_c_conv_w, m_c_conv_b, m_c_dt_bias, m_c_a_log, m_c_d_skip, m_c_norm_g, m_c_w_out, m_ffn_w_up, m_ffn_conv_w, m_ffn_conv_b, m_ffn_w_down, v_norm_mix_pre, v_norm_mix_post, v_norm_ffn_pre, v_norm_ffn_post, v_ab_w_in, v_ab_ret_decay_logit, v_ab_ret_gn_g, v_ab_na_rpb, v_ab_w_out, v_c_w_in, v_c_conv_w, v_c_conv_b, v_c_dt_bias, v_c_a_log, v_c_d_skip, v_c_norm_g, v_c_w_out, v_ffn_w_up, v_ffn_conv_w, v_ffn_conv_b, v_ffn_w_down):
    args = dict(locals())
    w = {n: args[n] for n in WEIGHTS}
    mom = {n: args["m_" + n] for n in WEIGHTS}
    var = {n: args["v_" + n] for n in WEIGHTS}

    big, small = _gather_all(w)
    rep = {n: w[n] for n in REP}

    def loss_fn(xs, big, small, rep):
        return model_loss(xs, loss_target[0], big, small, rep)

    loss, (gx, gbig, gsmall, grep) = jax.value_and_grad(loss_fn, argnums=(0, 1, 2, 3))(x[0], big, small, rep)
    loss = lax.psum(loss, ("x", "y", "c"))

    grads = _reduce_all(gbig, gsmall, grep, w)

    delta, new_m, new_v = {}, {}, {}
    for n in WEIGHTS:
        delta[n], new_m[n], new_v[n] = adamw(w[n], grads[n], mom[n], var[n])

    return (loss, gx[None], *[grads[n] for n in WEIGHTS], *[delta[n] for n in WEIGHTS],
            *[new_m[n] for n in WEIGHTS], *[new_v[n] for n in WEIGHTS])
```

```python
import functools
import math

import numpy as np
import jax
import jax.numpy as jnp
from jax import lax
from jax.experimental import pallas as pl
from jax.experimental.pallas import tpu as pltpu

f32 = jnp.float32
bf16 = jnp.bfloat16
MXU_DTYPE = bf16

GRID_W = 64
CHUNK = 128
EPS = 1e-6
RET_HEADS = 8
RET_DH = 64
ROPE_BASE = 10000.0
NA_HEADS = 8
NA_DH = 64
NA_WIN_R = 8
NA_WIN_C = 16
NA_ROWS_PER_STEP = 8
SCAN_CHUNKS_PER_STEP = 4
SSD_HEADDIM = 64
SSD_GROUPS = 4
SSD_STATE = 128
ADAM_LR = 0.001
ADAM_B1 = 0.9
ADAM_B2 = 0.999
ADAM_EPS = 1e-08
ADAM_WD = 0.01
ADAM_STEP = 10

LANES = 128
HEAD_W = 64
PACK_W = 512
PACK_ROWS = 1024
PACK_ALIGN = 16
COPY_CHUNKS = 2
VMEM_LIMIT = 56 * 1024 * 1024
MM_BLOCK_BYTES = 6 * 1024 * 1024
ROW_BLOCK_BYTES = 16 * 1024 * 1024
N_CHIPS = 4
N_DEV = 8
NEG_INF = -1e30

_DIMS = {"nn": (((1,), (0,)), ((), ())), "nt": (((1,), (1,)), ((), ())), "tn": (((0,), (0,)), ((), ()))}


def _cparams(sem=None):
    return pltpu.CompilerParams(dimension_semantics=sem, vmem_limit_bytes=VMEM_LIMIT)


def _pick(dim, cands):
    for c in cands:
        if dim % c == 0:
            return c
    return dim


def _divisor_tile(dim, fits, align):
    for d in range(1, dim + 1):
        t = dim // d
        if dim % d == 0 and t % align == 0 and fits(t):
            return t
    return dim


def _bdot_raw(a, b, mode):
    return lax.dot_general(a.astype(MXU_DTYPE), b.astype(MXU_DTYPE), _DIMS[mode], preferred_element_type=f32)


@functools.partial(jax.custom_vjp, nondiff_argnums=(2,))
def bdot(a, b, mode):
    return _bdot_raw(a, b, mode)


def _bdot_fwd(a, b, mode):
    return _bdot_raw(a, b, mode), (a, b)


def _bdot_bwd(mode, res, g):
    a, b = res
    if mode == "nn":
        da, db = _bdot_raw(g, b, "nt"), _bdot_raw(a, g, "tn")
    elif mode == "nt":
        da, db = _bdot_raw(g, b, "nn"), _bdot_raw(g, a, "tn")
    else:
        da, db = _bdot_raw(b, g, "nt"), _bdot_raw(a, g, "nn")
    return da.astype(a.dtype), db.astype(b.dtype)


bdot.defvjp(_bdot_fwd, _bdot_bwd)


def _mm_call(a, b, mode, out_dtype):
    if mode == "nn":
        (M, K), (K2, N) = a.shape, b.shape
    elif mode == "nt":
        (M, K), (N, K2) = a.shape, b.shape
    else:
        (K, M), (K2, N) = a.shape, b.shape
    assert K == K2, (a.shape, b.shape, mode)
    a_bytes, b_bytes, o_bytes = a.dtype.itemsize, b.dtype.itemsize, jnp.dtype(out_dtype).itemsize
    if mode == "tn":
        tn = _divisor_tile(N, lambda t: t <= 1536, LANES)
        tm = _divisor_tile(M, lambda t: t * tn * 4 <= MM_BLOCK_BYTES, 8)
        tk = _divisor_tile(K, lambda t: t * tm * a_bytes <= MM_BLOCK_BYTES and t * tn * b_bytes <= MM_BLOCK_BYTES, LANES)
    else:
        tk, tn = K, N
        tm = _divisor_tile(M, lambda t: t * K * a_bytes <= MM_BLOCK_BYTES and t * N * o_bytes <= MM_BLOCK_BYTES, 8)
    nk = K // tk
    if mode == "nn":
        a_spec = pl.BlockSpec((tm, tk), lambda i, j, k: (i, k))
        b_spec = pl.BlockSpec((tk, tn), lambda i, j, k: (k, j))
    elif mode == "nt":
        a_spec = pl.BlockSpec((tm, tk), lambda i, j, k: (i, k))
        b_spec = pl.BlockSpec((tn, tk), lambda i, j, k: (j, k))
    else:
        a_spec = pl.BlockSpec((tk, tm), lambda i, j, k: (k, i))
        b_spec = pl.BlockSpec((tk, tn), lambda i, j, k: (k, j))

    if nk == 1:
        def body(a_ref, b_ref, o_ref):
            o_ref[...] = _bdot_raw(a_ref[...], b_ref[...], mode).astype(o_ref.dtype)
    else:
        def body(a_ref, b_ref, o_ref, acc_ref):
            k = pl.program_id(2)

            @pl.when(k == 0)
            def _():
                acc_ref[...] = jnp.zeros_like(acc_ref)

            acc_ref[...] += _bdot_raw(a_ref[...], b_ref[...], mode)

            @pl.when(k == nk - 1)
            def _():
                o_ref[...] = acc_ref[...].astype(o_ref.dtype)

    return pl.pallas_call(
        body,
        out_shape=jax.ShapeDtypeStruct((M, N), out_dtype),
        grid=(M // tm, N // tn, nk),
        in_specs=[a_spec, b_spec],
        out_specs=pl.BlockSpec((tm, tn), lambda i, j, k: (i, j)),
        scratch_shapes=[pltpu.VMEM((tm, tn), f32)] if nk > 1 else [],
        compiler_params=_cparams(("parallel", "parallel", "arbitrary")),
        name="mm_" + mode,
    )(a, b)


def mm(a, b, mode="nn", out_dtype=f32):
    @jax.custom_vjp
    def op(a, b):
        return _mm_call(a, b, mode, out_dtype)

    def fwd(a, b):
        return _mm_call(a, b, mode, out_dtype), (a, b)

    def bwd(res, g):
        a, b = res
        if mode == "nn":
            return _mm_call(g, b, "nt", a.dtype), _mm_call(a, g, "tn", b.dtype)
        if mode == "nt":
            return _mm_call(g, b, "nn", a.dtype), _mm_call(g, a, "tn", b.dtype)
        return _mm_call(b, g, "nt", a.dtype), _mm_call(a, g, "nn", b.dtype)

    op.defvjp(fwd, bwd)
    return op(a, b)


def _row_tile(S, row_bytes):
    tm = 1024
    while tm > 8 and (tm * row_bytes > ROW_BLOCK_BYTES or S % tm):
        tm //= 2
    return tm


def rowwise(fn, name, rows, params, out_dtypes, n_diff_rows=None, n_diff_params=None, ncol=1, bwd_fn=None):
    rows, params = list(rows), list(params)
    nr, npar = len(rows), len(params)
    ndr = nr if n_diff_rows is None else n_diff_rows
    ndp = npar if n_diff_params is None else n_diff_params
    S = rows[0].shape[0]
    rw = [r.shape[1] // ncol for r in rows]
    pshape = [(p.shape[0], p.shape[1] // ncol) for p in params]

    def block_structs(tm):
        return ([jax.ShapeDtypeStruct((tm, w), f32) for w in rw] + [jax.ShapeDtypeStruct(s, f32) for s in pshape])

    outs_s = jax.eval_shape(fn, *block_structs(8))
    ow = [o.shape[1] for o in outs_s]
    nout = len(ow)
    row_bytes = 4 * (sum(rw) * 2 + sum(ow) * 2)
    tm = _row_tile(S, row_bytes)
    grid = (ncol, S // tm)

    def rspec(w):
        return pl.BlockSpec((tm, w), lambda g, i: (i, g))

    def pspec(s):
        return pl.BlockSpec(s, lambda g, i: (0, g))

    def call_fwd(*args):
        def body(*refs):
            vals = [r[...].astype(f32) for r in refs[:nr + npar]]
            res = fn(*vals)
            for o, r in zip(refs[nr + npar:], res):
                o[...] = r.astype(o.dtype)

        return pl.pallas_call(
            body,
            out_shape=[jax.ShapeDtypeStruct((S, w * ncol), dt) for w, dt in zip(ow, out_dtypes)],
            grid=grid,
            in_specs=[rspec(w) for w in rw] + [pspec(s) for s in pshape],
            out_specs=[rspec(w) for w in ow],
            compiler_params=_cparams(("parallel", "parallel")),
            name=name + "_fwd",
        )(*args)

    def call_bwd(args, douts):
        def body(*refs):
            in_refs = refs[:nr + npar]
            do_refs = refs[nr + npar:nr + npar + nout]
            dr_refs = refs[nr + npar + nout:nr + npar + nout + ndr]
            dp_refs = refs[nr + npar + nout + ndr:]
            rv = [r[...] for r in in_refs[:nr]]
            pv = [r[...] for r in in_refs[nr:]]
            dos = [d[...].astype(f32) for d in do_refs]
            if bwd_fn is not None:
                drs, dps = bwd_fn(rv, pv, dos)
            else:
                def f(*a):
                    return fn(*a[:ndr], *rv[ndr:], *a[ndr:], *pv[ndp:])

                _, vjp = jax.vjp(f, *[v.astype(f32) for v in rv[:ndr]], *pv[:ndp])
                cts = vjp(tuple(dos))
                drs, dps = cts[:ndr], cts[ndr:]
            for r, ct in zip(dr_refs, drs):
                r[...] = ct.astype(r.dtype)
            if ndp:
                @pl.when(pl.program_id(1) == 0)
                def _():
                    for r in dp_refs:
                        r[...] = jnp.zeros_like(r)

                for r, ct in zip(dp_refs, dps):
                    r[...] += ct

        return pl.pallas_call(
            body,
            out_shape=[jax.ShapeDtypeStruct(r.shape, r.dtype) for r in rows[:ndr]]
            + [jax.ShapeDtypeStruct(p.shape, f32) for p in params[:ndp]],
            grid=grid,
            in_specs=[rspec(w) for w in rw] + [pspec(s) for s in pshape] + [rspec(w) for w in ow],
            out_specs=[rspec(w) for w in rw[:ndr]] + [pspec(s) for s in pshape[:ndp]],
            compiler_params=_cparams(("parallel", "arbitrary")),
            name=name + "_bwd",
        )(*args, *douts)

    @jax.custom_vjp
    def op(*args):
        return tuple(call_fwd(*args))

    def fwd(*args):
        return tuple(call_fwd(*args)), args

    def bwd(args, douts):
        res = call_bwd(args, douts)
        drs, dps = res[:ndr], res[ndr:]
        out = list(drs) + [jnp.zeros_like(a) for a in args[ndr:nr]]
        out += [dp.astype(p.dtype) for dp, p in zip(dps, args[nr:nr + ndp])]
        out += [jnp.zeros_like(a) for a in args[nr + ndp:]]
        return tuple(out)

    op.defvjp(fwd, bwd)
    return op(*rows, *params)


def _silu(x):
    return x * (1.0 / (1.0 + jnp.exp(-x)))


def _softplus(x):
    return jnp.maximum(x, 0.0) + jnp.log(1.0 + jnp.exp(-jnp.abs(x)))


def _gelu_tanh(x):
    return 0.5 * x * (1.0 + jnp.tanh(math.sqrt(2.0 / math.pi) * (x + 0.044715 * (x * x * x))))


def _rms_fn(x, g):
    return x * lax.rsqrt(jnp.mean(x * x, axis=-1, keepdims=True) + EPS) * g


def _rms_bwd(x, g, dy):
    r = lax.rsqrt(jnp.mean(x * x, axis=-1, keepdims=True) + EPS)
    xh = x * r
    dxh = dy * g
    dx = r * (dxh - xh * jnp.mean(dxh * xh, axis=-1, keepdims=True))
    return dx, jnp.sum(dy * xh, axis=0, keepdims=True)


def rms(x, g, out_dtype):
    def bwd_fn(rv, pv, dos):
        dx, dg = _rms_bwd(rv[0], pv[0], dos[0])
        return (dx,), (dg,)

    return rowwise(lambda x, g: (_rms_fn(x, g),), "rms", [x], [g.reshape(1, -1)], [out_dtype], bwd_fn=bwd_fn)[0]


def rms_residual_norm(m, g, x, g_next):
    def fn(m, x, g, gn):
        xn = x + _rms_fn(m, g)
        return xn, _rms_fn(xn, gn)

    def bwd_fn(rv, pv, dos):
        (m, x), (g, gn), (dxn, dhn) = rv, pv, dos
        xn = x + _rms_fn(m, g)
        d_from_norm, dgn = _rms_bwd(xn, gn, dhn)
        dxn = dxn + d_from_norm
        dm, dg = _rms_bwd(m, g, dxn)
        return (dm, dxn), (dg, dgn)

    return rowwise(fn, "rms_res_norm", [m, x], [g.reshape(1, -1), g_next.reshape(1, -1)], [f32, MXU_DTYPE], bwd_fn=bwd_fn)


def rms_residual(m, g, x):
    def bwd_fn(rv, pv, dos):
        dm, dg = _rms_bwd(rv[0], pv[0], dos[0])
        return (dm, dos[0]), (dg,)

    return rowwise(lambda m, x, g: (x + _rms_fn(m, g),), "rms_res", [m, x], [g.reshape(1, -1)], [f32], bwd_fn=bwd_fn)[0]


def loss_op(y, tgt):
    S, D = y.shape
    tm = _row_tile(S, 4 * D * 4)

    def call_fwd(y, tgt):
        def body(y_ref, t_ref, o_ref):
            @pl.when(pl.program_id(0) == 0)
            def _():
                o_ref[...] = jnp.zeros_like(o_ref)

            e = y_ref[...] - t_ref[...]
            o_ref[...] += 0.5 * jnp.sum(jnp.mean(e * e, axis=-1, keepdims=True))

        out = pl.pallas_call(
            body,
            out_shape=jax.ShapeDtypeStruct((8, LANES), f32),
            grid=(S // tm,),
            in_specs=[pl.BlockSpec((tm, D), lambda i: (i, 0))] * 2,
            out_specs=pl.BlockSpec((8, LANES), lambda i: (0, 0)),
            compiler_params=_cparams(("arbitrary",)),
            name="loss_fwd",
        )(y, tgt)
        return out[0, 0]

    def call_bwd(y, tgt, g):
        def body(y_ref, t_ref, g_ref, o_ref):
            o_ref[...] = (y_ref[...] - t_ref[...]) * (g_ref[...] * (1.0 / D))

        return pl.pallas_call(
            body,
            out_shape=jax.ShapeDtypeStruct((S, D), f32),
            grid=(S // tm,),
            in_specs=[pl.BlockSpec((tm, D), lambda i: (i, 0))] * 2 + [pl.BlockSpec((1, 1), lambda i: (0, 0))],
            out_specs=pl.BlockSpec((tm, D), lambda i: (i, 0)),
            compiler_params=_cparams(("parallel",)),
            name="loss_bwd",
        )(y, tgt, g.reshape(1, 1).astype(f32))

    @jax.custom_vjp
    def op(y, tgt):
        return call_fwd(y, tgt)

    def fwd(y, tgt):
        return call_fwd(y, tgt), (y, tgt)

    def bwd(res, g):
        y, tgt = res
        return call_bwd(y, tgt, g), jnp.zeros_like(tgt)

    op.defvjp(fwd, bwd)
    return op(y, tgt)


HALO = 8


def _conv_tile(S, R):
    def ext(ref, r0):
        cur = ref[pl.ds(r0, R), :]
        prev = ref[pl.ds(pl.multiple_of(jnp.maximum(r0 - HALO, 0), HALO), HALO), :]
        nxt = ref[pl.ds(pl.multiple_of(jnp.minimum(r0 + R, S - HALO), HALO), HALO), :]
        prev = jnp.where(r0 > 0, prev, 0.0)
        nxt = jnp.where(r0 + R < S, nxt, 0.0)
        return jnp.concatenate([prev, cur, nxt], axis=0)

    return ext


def _shift_rows(e, k, R):
    n = e.shape[0]
    if k == 0:
        return e[HALO:HALO + R]
    return pltpu.roll(e, (-k) % n, 0)[HALO:HALO + R]


def _silu_bwd(us, dy):
    u, = us
    s = 1.0 / (1.0 + jnp.exp(-u))
    return (dy * (s * (1.0 + u * (1.0 - s))),)


def _geglu(g, v):
    return _gelu_tanh(g) * v


def _geglu_bwd(us, dy):
    g, v = us
    c = math.sqrt(2.0 / math.pi)
    t = jnp.tanh(c * (g + 0.044715 * (g * g * g)))
    half = 0.5 * (1.0 + t)
    dgelu = half + 0.5 * g * (1.0 - t * t) * (c * (1.0 + 3.0 * 0.044715 * (g * g)))
    return dy * v * dgelu, dy * (g * half)


def mm_conv_act(h, ws, cws, cbs, act, act_bwd, out_dtype, name):
    n = len(ws)
    S = h.shape[0]
    C = ws[0].shape[1]
    W = cws[0].shape[0]
    pad = W // 2
    bw = _pick(C, (LANES,))
    R = _pick(S, (256, 128, 64, 32, 16, 8))
    nt = S // R
    ext = _conv_tile(S, R)
    col = lambda rows: pl.BlockSpec((rows, bw), lambda j: (0, j))

    def conv(e, wv, bv):
        acc = bv + wv[pad] * e[HALO:HALO + R]
        for j in range(W):
            if j != pad:
                acc = acc + wv[j] * _shift_rows(e, j - pad, R)
        return acc

    def call_fwd(xs, cws, cbs):
        def body(*refs):
            x_refs, w_refs, b_refs, y_ref = refs[:n], refs[n:2 * n], refs[2 * n:3 * n], refs[3 * n]
            wvs = [[w[j:j + 1, :] for j in range(W)] for w in w_refs]
            bvs = [b[...] for b in b_refs]

            def tile(i, c):
                r0 = pl.multiple_of(i * R, R)
                us = [conv(ext(x, r0), wv, bv) for x, wv, bv in zip(x_refs, wvs, bvs)]
                y_ref[pl.ds(r0, R), :] = act(*us).astype(y_ref.dtype)
                return c

            lax.fori_loop(0, nt, tile, 0)

        return pl.pallas_call(
            body,
            out_shape=jax.ShapeDtypeStruct((S, C), out_dtype),
            grid=(C // bw,),
            in_specs=[col(S)] * n + [col(W)] * n + [col(1)] * n,
            out_specs=col(S),
            compiler_params=_cparams(("parallel",)),
            name=name + "_fwd",
        )(*xs, *cws, *cbs)

    def call_bwd(xs, cws, cbs, dy):
        def body(*refs):
            x_refs, w_refs, b_refs, dy_ref = refs[:n], refs[n:2 * n], refs[2 * n:3 * n], refs[3 * n]
            dx_refs, dw_refs, db_refs = refs[3 * n + 1:4 * n + 1], refs[4 * n + 1:5 * n + 1], refs[5 * n + 1:6 * n + 1]
            du_scr = refs[6 * n + 1:]
            wvs = [[w[j:j + 1, :] for j in range(W)] for w in w_refs]
            bvs = [b[...] for b in b_refs]
            zero = jnp.zeros((1, bw), f32)

            def first(i, dbs):
                r0 = pl.multiple_of(i * R, R)
                us = [conv(ext(x, r0), wv, bv) for x, wv, bv in zip(x_refs, wvs, bvs)]
                dus = act_bwd(us, dy_ref[pl.ds(r0, R), :].astype(f32))
                for scr, du in zip(du_scr, dus):
                    scr[pl.ds(r0, R), :] = du
                return tuple(db + jnp.sum(du, axis=0, keepdims=True) for db, du in zip(dbs, dus))

            dbs = lax.fori_loop(0, nt, first, tuple(zero for _ in range(n)))

            def second(i, dws):
                r0 = pl.multiple_of(i * R, R)
                new = []
                for x, scr, dx, wv, dw in zip(x_refs, du_scr, dx_refs, wvs, dws):
                    ex, ed = ext(x, r0), ext(scr, r0)
                    d0 = ed[HALO:HALO + R]
                    acc = jnp.zeros((R, bw), f32)
                    row = []
                    for j in range(W):
                        acc = acc + wv[j] * _shift_rows(ed, pad - j, R)
                        row.append(dw[j] + jnp.sum(d0 * _shift_rows(ex, j - pad, R), axis=0, keepdims=True))
                    dx[pl.ds(r0, R), :] = acc.astype(dx.dtype)
                    new.append(tuple(row))
                return tuple(new)

            dws = lax.fori_loop(0, nt, second, tuple(tuple(zero for _ in range(W)) for _ in range(n)))
            for dw_ref, db_ref, dw, db in zip(dw_refs, db_refs, dws, dbs):
                dw_ref[...] = jnp.zeros_like(dw_ref)
                for j in range(W):
                    dw_ref[j:j + 1, :] = dw[j]
                db_ref[...] = db

        return pl.pallas_call(
            body,
            out_shape=[jax.ShapeDtypeStruct((S, C), MXU_DTYPE)] * n + [jax.ShapeDtypeStruct((8, C), f32)] * n
            + [jax.ShapeDtypeStruct((1, C), f32)] * n,
            grid=(C // bw,),
            in_specs=[col(S)] * n + [col(W)] * n + [col(1)] * n + [col(S)],
            out_specs=[col(S)] * n + [col(8)] * n + [col(1)] * n,
            scratch_shapes=[pltpu.VMEM((S, bw), f32)] * n,
            compiler_params=_cparams(("parallel",)),
            name=name + "_bwd",
        )(*xs, *cws, *cbs, dy)

    @jax.custom_vjp
    def op(h, ws, cws, cbs):
        return call_fwd([_mm_call(h, w, "nn", f32) for w in ws], cws, cbs)

    def fwd(h, ws, cws, cbs):
        xs = [_mm_call(h, w, "nn", f32) for w in ws]
        return call_fwd(xs, cws, cbs), (h, ws, xs, cws, cbs)

    def bwd(res, dy):
        h, ws, xs, cws, cbs = res
        out = call_bwd(xs, cws, cbs, dy)
        dxs, dcws, dcbs = out[:n], out[n:2 * n], out[2 * n:]
        dh = _mm_call(dxs[0], ws[0], "nt", h.dtype)
        for dx, w in zip(dxs[1:], ws[1:]):
            dh = dh + _mm_call(dx, w, "nt", h.dtype)
        dws = tuple(_mm_call(h, dx, "tn", w.dtype) for dx, w in zip(dxs, ws))
        return dh, dws, tuple(d[:W] for d in dcws), tuple(dcbs)

    op.defvjp(fwd, bwd)
    return op(h, tuple(ws), tuple(cws), tuple(b.reshape(1, C) for b in cbs))


@jax.custom_vjp
def _masked_decay(cs_col, cs_row, mask01):
    return jnp.where(mask01 > 0, jnp.exp(cs_col - cs_row), 0.0)


def _masked_decay_fwd(cs_col, cs_row, mask01):
    d = jnp.where(mask01 > 0, jnp.exp(cs_col - cs_row), 0.0)
    return d, (d, mask01)


def _masked_decay_bwd(res, g):
    d, mask01 = res
    t = g * d
    return jnp.sum(t, axis=1, keepdims=True), -jnp.sum(t, axis=0, keepdims=True), jnp.zeros_like(mask01)


_masked_decay.defvjp(_masked_decay_fwd, _masked_decay_bwd)


def _scan_chunk(qs, ks, xs, cs_tok, dt_tok, hs, *, rev, incl, nsub):
    nb = len(xs)
    L, N = qs[0].shape
    W = xs[0].shape[1]
    Hg = cs_tok.shape[1]
    nh = W // HEAD_W
    shared = len(qs) == 1
    t = lax.broadcasted_iota(jnp.int32, (L, L), 0)
    l = lax.broadcasted_iota(jnp.int32, (L, L), 1)
    if rev:
        mask = (l >= t) if incl else (l > t)
    else:
        mask = (l <= t) if incl else (l < t)
    mask01 = mask.astype(f32)
    lane_a = lax.broadcasted_iota(jnp.int32, cs_tok.shape, 1)
    row_a = lax.broadcasted_iota(jnp.int32, (Hg, L), 0)
    last = lax.broadcasted_iota(jnp.int32, (1, L), 1) == (0 if rev else L - 1)
    vhead = lax.broadcasted_iota(jnp.int32, (1, W), 1) // HEAD_W
    qhead = lax.broadcasted_iota(jnp.int32, (1, N), 1) // (N // nsub)
    cs_rows = lax.dot_general(cs_tok, (t == l).astype(f32), _DIMS["tn"], precision=lax.Precision.HIGHEST,
                              preferred_element_type=f32)

    def by_head(vals):
        if len(vals) == 2:
            return jnp.where(vhead == 0, vals[0], vals[1])
        return sum(jnp.where(vhead == i, v, 0.0) for i, v in enumerate(vals))

    decay, lam_e, tau_e, gam_e, dt_e = [], [], [], [], []
    for b in range(nb):
        cs_cols, tots, dt_cols = [], [], []
        for i in range(nh):
            head = b * nh + i
            cs_col = jnp.sum(jnp.where(lane_a == head, cs_tok, 0.0), axis=1, keepdims=True)
            cs_row = jnp.sum(jnp.where(row_a == head, cs_rows, 0.0), axis=0, keepdims=True)
            tots.append(jnp.sum(jnp.where(last, cs_row, 0.0), axis=1, keepdims=True))
            decay.append(_masked_decay(cs_col, cs_row, mask01))
            cs_cols.append(cs_col)
            if dt_tok is not None:
                dt_cols.append(jnp.sum(jnp.where(lane_a == head, dt_tok, 0.0), axis=1, keepdims=True))
        cs_e, tot_e = by_head(cs_cols), by_head(tots)
        lam_e.append(jnp.exp(cs_e))
        tau_e.append(jnp.exp(tot_e - cs_e))
        gam_e.append(jnp.exp(tot_e))
        if dt_tok is not None:
            dt_e.append(by_head(dt_cols))
    vs = [x if dt_tok is None else x * dt_e[b] for b, x in enumerate(xs)]
    qk = lambda b: (qs[0], ks[0]) if shared else (qs[b], ks[b])
    if nsub == 1:
        scores = [bdot(qs[0], ks[0], "nt")] if shared else [bdot(*qk(b), "nt") for b in range(nb)]
        score = lambda b, i: scores[0 if shared else b]
    else:
        scores = [[bdot(jnp.where(qhead == i, qk(b)[0], 0.0), qk(b)[1], "nt") for i in range(nh)] for b in range(nb)]
        score = lambda b, i: scores[b][i]
    ys = [lam_e[b] * bdot(qk(b)[0], hs[b], "nn")
          + by_head([bdot(score(b, i) * decay[b * nh + i], vs[b], "nn") for i in range(nh)]) for b in range(nb)]
    hns = [gam_e[b] * hs[b] + bdot(qk(b)[1], tau_e[b] * vs[b], "tn") for b in range(nb)]
    if nsub > 1:
        nhead = lax.broadcasted_iota(jnp.int32, (N, W), 0) // (N // nsub)
        keep = nhead == lax.broadcasted_iota(jnp.int32, (N, W), 1) // HEAD_W
        hns = [jnp.where(keep, hn, 0.0) for hn in hns]
    return ys, hns


def chunk_cumsum(a_tok, rev):
    G, S, Hg = a_tok.shape
    L = CHUNK
    CB = _pick(S // L, (16, 8, 4, 2))

    def call(a, rev):
        def body(a_ref, o_ref):
            t = lax.broadcasted_iota(jnp.int32, (L, L), 0)
            l = lax.broadcasted_iota(jnp.int32, (L, L), 1)
            tri = ((l >= t) if rev else (l <= t)).astype(f32)
            for j in range(CB):
                o_ref[0, j * L:(j + 1) * L, :] = _exact_dot(tri, a_ref[0, j * L:(j + 1) * L, :])

        spec = pl.BlockSpec((1, CB * L, Hg), lambda g, c: (g, c, 0))
        return pl.pallas_call(
            body,
            out_shape=jax.ShapeDtypeStruct((G, S, Hg), f32),
            grid=(G, S // (CB * L)),
            in_specs=[spec],
            out_specs=spec,
            compiler_params=_cparams(("parallel", "parallel")),
            name="chunk_cumsum",
        )(a)

    @jax.custom_vjp
    def op(a):
        return call(a, rev)

    def fwd(a):
        return call(a, rev), None

    def bwd(_, g):
        return (call(g, not rev),)

    op.defvjp(fwd, bwd)
    return op(a_tok)


def scan_op(q, k, x, a_tok, dt_tok, *, rev, incl, nsub):
    S = q.shape[0]
    G, _, Hg = a_tok.shape
    N = q.shape[1] // G
    Vw = x.shape[1] // G
    L = CHUNK
    nc = S // L
    use_dt = dt_tok is not None
    PW = min(Vw, LANES)
    chunk = functools.partial(_scan_chunk, rev=rev, incl=incl, nsub=nsub)
    cols = [slice(p * PW, (p + 1) * PW) for p in range(Vw // PW)]
    own_qk = nsub > 1
    NB = PW if own_qk else N

    CPS = SCAN_CHUNKS_PER_STEP if nc % SCAN_CHUNKS_PER_STEP == 0 else 1
    ns = nc // CPS

    def order(c, backward):
        return (ns - 1 - c) if (rev != backward) else c

    def visit(backward):
        js = range(CPS) if rev == backward else range(CPS - 1, -1, -1)
        return [(j, slice(j * L, (j + 1) * L)) for j in js]

    def specs(backward):
        qs = pl.BlockSpec((CPS * L, N), lambda g, c: (order(c, backward), g))
        xs = pl.BlockSpec((CPS * L, Vw), lambda g, c: (order(c, backward), g))
        as_ = pl.BlockSpec((1, CPS * L, Hg), lambda g, c: (g, order(c, backward), 0))
        hs = pl.BlockSpec((1, CPS, NB, Vw), lambda g, c: (g, order(c, backward), 0, 0))
        return qs, xs, as_, hs

    def call_fwd(q, k, x, a_tok, dt_tok, y_prev=None):
        qs, xs, as_, hs = specs(False)
        n_in = 4 + use_dt + (y_prev is not None)

        def body(*refs):
            q_ref, k_ref, x_ref, a_ref = refs[:4]
            dt_ref = refs[4] if use_dt else None
            yp_ref = refs[n_in - 1] if y_prev is not None else None
            y_ref, hs_ref, h_scr = refs[n_in:]

            @pl.when(pl.program_id(1) == 0)
            def _():
                h_scr[...] = jnp.zeros_like(h_scr)

            for j, rs in visit(False):
                hs_ref[0, j] = h_scr[...]
                q, k, a, dt = q_ref[rs, :], k_ref[rs, :], a_ref[0, rs, :], dt_ref[0, rs, :] if use_dt else None
                qs, ks = ([q[:, c] for c in cols], [k[:, c] for c in cols]) if own_qk else ([q], [k])
                ys, hns = chunk(qs, ks, [x_ref[rs, c] for c in cols], a, dt, [h_scr[:, c] for c in cols])
                for c, y, hn in zip(cols, ys, hns):
                    y_ref[rs, c] = y if yp_ref is None else y + yp_ref[rs, c]
                    h_scr[:, c] = hn

        ins = [q, k, x, a_tok] + ([dt_tok] if use_dt else []) + ([y_prev] if y_prev is not None else [])
        return pl.pallas_call(
            body,
            out_shape=[jax.ShapeDtypeStruct((S, G * Vw), f32), jax.ShapeDtypeStruct((G, nc, NB, Vw), f32)],
            grid=(G, ns),
            in_specs=[qs, qs, xs, as_] + ([as_] if use_dt else []) + ([xs] if y_prev is not None else []),
            out_specs=[xs, hs],
            scratch_shapes=[pltpu.VMEM((NB, Vw), f32)],
            compiler_params=_cparams(("parallel", "arbitrary")),
            name="scan_fwd",
        )(*ins)

    def call_bwd(q, k, x, a_tok, dt_tok, hsave, dy, acc=None):
        qs, xs, as_, hs = specs(True)
        n_in = 6 + use_dt + (3 if acc is not None else 0)

        def body(*refs):
            q_ref, k_ref, x_ref, a_ref = refs[:4]
            dt_ref = refs[4] if use_dt else None
            hs_ref, dy_ref = refs[4 + use_dt], refs[5 + use_dt]
            acc_refs = refs[n_in - 3:n_in] if acc is not None else None
            dq_ref, dk_ref, dx_ref, da_ref = refs[n_in:n_in + 4]
            ddt_ref = refs[n_in + 4] if use_dt else None
            dh_scr = refs[-1]

            @pl.when(pl.program_id(1) == 0)
            def _():
                dh_scr[...] = jnp.zeros_like(dh_scr)

            for j, rs in visit(True):
                q, k, a = q_ref[rs, :].astype(f32), k_ref[rs, :].astype(f32), a_ref[0, rs, :]
                qs, ks = ([q[:, c] for c in cols], [k[:, c] for c in cols]) if own_qk else ([q], [k])
                xs, hs_in = [x_ref[rs, c] for c in cols], [hs_ref[0, j, :, c] for c in cols]
                if use_dt:
                    _, vjp = jax.vjp(chunk, qs, ks, xs, a, dt_ref[0, rs, :], hs_in)
                else:
                    _, vjp = jax.vjp(lambda qs, ks, xs, a, hs: chunk(qs, ks, xs, a, None, hs), qs, ks, xs, a, hs_in)
                cts = vjp(([dy_ref[rs, c] for c in cols], [dh_scr[:, c] for c in cols]))
                dqs, dks, dxs, da, dhs = cts[0], cts[1], cts[2], cts[3], cts[-1]
                if acc is not None:
                    dq_acc, dk_acc = acc_refs[0][rs, :].astype(f32), acc_refs[1][rs, :].astype(f32)
                if own_qk:
                    for b, c in enumerate(cols):
                        dq_ref[rs, c] = (dqs[b] if acc is None else dqs[b] + dq_acc[:, c]).astype(dq_ref.dtype)
                        dk_ref[rs, c] = (dks[b] if acc is None else dks[b] + dk_acc[:, c]).astype(dk_ref.dtype)
                else:
                    dq_ref[rs, :] = (dqs[0] if acc is None else dqs[0] + dq_acc).astype(dq_ref.dtype)
                    dk_ref[rs, :] = (dks[0] if acc is None else dks[0] + dk_acc).astype(dk_ref.dtype)
                for b, c in enumerate(cols):
                    dx_ref[rs, c] = dxs[b] if acc is None else dxs[b] + acc_refs[2][rs, c]
                    dh_scr[:, c] = dhs[b]
                da_ref[0, rs, :] = da
                if use_dt:
                    ddt_ref[0, rs, :] = cts[4]

        ins = [q, k, x, a_tok] + ([dt_tok] if use_dt else []) + [hsave, dy] + (list(acc) if acc is not None else [])
        a_shape = jax.ShapeDtypeStruct(a_tok.shape, f32)
        return pl.pallas_call(
            body,
            out_shape=[jax.ShapeDtypeStruct(q.shape, q.dtype), jax.ShapeDtypeStruct(k.shape, k.dtype),
                       jax.ShapeDtypeStruct(x.shape, f32), a_shape] + ([a_shape] if use_dt else []),
            grid=(G, ns),
            in_specs=[qs, qs, xs, as_] + ([as_] if use_dt else []) + [hs, xs] + ([qs, qs, xs] if acc is not None else []),
            out_specs=[qs, qs, xs, as_] + ([as_] if use_dt else []),
            scratch_shapes=[pltpu.VMEM((NB, Vw), f32)],
            compiler_params=_cparams(("parallel", "arbitrary")),
            name="scan_bwd",
        )(*ins)

    return call_fwd, call_bwd


def bidir_scan(q, k, x, a_f, a_b, dt_f, dt_b, *, nsub):
    use_dt = dt_f is not None
    a_f, a_b = chunk_cumsum(a_f, False), chunk_cumsum(a_b, True)
    fwd_f, bwd_f = scan_op(q, k, x, a_f, dt_f, rev=False, incl=True, nsub=nsub)
    fwd_b, bwd_b = scan_op(q, k, x, a_b, dt_b, rev=True, incl=False, nsub=nsub)

    def run(q, k, x, a_f, a_b, dt_f, dt_b):
        y_f, hs_f = fwd_f(q, k, x, a_f, dt_f)
        y, hs_b = fwd_b(q, k, x, a_b, dt_b, y_prev=y_f)
        return y, (hs_f, hs_b)

    def grads(q, k, x, a_f, a_b, dt_f, dt_b, hs, dy):
        first = bwd_f(q, k, x, a_f, dt_f, hs[0], dy)
        both = bwd_b(q, k, x, a_b, dt_b, hs[1], dy, acc=first[:3])
        return both[0], both[1], both[2], first[3], both[3], (first[4] if use_dt else None), (both[4] if use_dt else None)

    if use_dt:
        @jax.custom_vjp
        def op(q, k, x, a_f, a_b, dt_f, dt_b):
            return run(q, k, x, a_f, a_b, dt_f, dt_b)[0]

        def fwd(q, k, x, a_f, a_b, dt_f, dt_b):
            y, hs = run(q, k, x, a_f, a_b, dt_f, dt_b)
            return y, (q, k, x, a_f, a_b, dt_f, dt_b, hs)

        def bwd(res, dy):
            return grads(*res, dy)

        op.defvjp(fwd, bwd)
        return op(q, k, x, a_f, a_b, dt_f, dt_b)

    @jax.custom_vjp
    def op(q, k, x, a_f, a_b):
        return run(q, k, x, a_f, a_b, None, None)[0]

    def fwd(q, k, x, a_f, a_b):
        y, hs = run(q, k, x, a_f, a_b, None, None)
        return y, (q, k, x, a_f, a_b, hs)

    def bwd(res, dy):
        q, k, x, a_f, a_b, hs = res
        return grads(q, k, x, a_f, a_b, None, None, hs, dy)[:5]

    op.defvjp(fwd, bwd)
    return op(q, k, x, a_f, a_b)


def _swap_halves(x, dh):
    W = x.shape[1]
    lane = lax.broadcasted_iota(jnp.int32, (1, W), 1) % dh
    return jnp.where(lane < dh // 2, pltpu.roll(x, W - dh // 2, 1), pltpu.roll(x, dh // 2, 1))


def rotary(rq, rk, cos_t, sin_t):
    scale = RET_DH ** -0.5

    def fn(rq, rk, c, s):
        return rq * c + _swap_halves(rq, RET_DH) * s, (rk * c + _swap_halves(rk, RET_DH) * s) * scale

    def bwd_fn(rv, pv, dos):
        _, _, c, s = rv
        dq, dk = dos
        dk = dk * scale
        return (dq * c + _swap_halves(dq * s, RET_DH), dk * c + _swap_halves(dk * s, RET_DH)), ()

    return rowwise(fn, "rotary", [rq, rk, cos_t, sin_t], [], [MXU_DTYPE, MXU_DTYPE], n_diff_rows=2, bwd_fn=bwd_fn)


def _rope_tables(S, width):
    half = RET_DH // 2
    inv = 1.0 / (ROPE_BASE ** (jnp.arange(half, dtype=f32) / half))
    ang = jnp.arange(S, dtype=f32)[:, None] * inv[None, :]
    cos, sin = jnp.cos(ang), jnp.sin(ang)
    reps = width // RET_DH
    return jnp.tile(jnp.concatenate([cos, cos], axis=1), (1, reps)), jnp.tile(jnp.concatenate([-sin, sin], axis=1), (1, reps))


def _exact_dot(x, m):
    return jnp.dot(x, m, precision=lax.Precision.HIGHEST, preferred_element_type=f32)


def ret_post(y, rg, gn_g):
    W = y.shape[1]
    idx = np.arange(W) // RET_DH
    avg = jnp.asarray((idx[:, None] == idx[None, :]).astype(np.float32) / RET_DH)

    def fn(y, rg, g, avg):
        mu = _exact_dot(y, avg)
        d = y - mu
        var = _exact_dot(d * d, avg)
        return (_silu(rg) * (d * lax.rsqrt(var + EPS) * g),)

    return rowwise(fn, "ret_post", [y, rg], [gn_g.reshape(1, -1), avg], [MXU_DTYPE], n_diff_params=1)[0]


def _na_bias(rpb, win_r):
    H = rpb.shape[0]
    qc = np.arange(GRID_W)[:, None]
    kc = np.arange(GRID_W)[None, :]
    cstart = np.clip(qc - NA_WIN_C // 2, 0, GRID_W - NA_WIN_C)
    valid = (kc >= cstart) & (kc < cstart + NA_WIN_C)
    dc = np.clip(kc - qc, -(NA_WIN_C - 1), NA_WIN_C - 1) + (NA_WIN_C - 1)
    onehot = (dc[None] == np.arange(2 * NA_WIN_C - 1)[:, None, None]).astype(np.float32)
    t1 = jnp.einsum("hrd,dqk->hrqk", rpb.astype(f32), jnp.asarray(onehot), precision=lax.Precision.HIGHEST)
    per_delta = [t1[:, NA_WIN_R - 1 - d:NA_WIN_R - 1 - d + win_r] for d in range(win_r)]
    b = jnp.stack(per_delta, axis=1)
    b = jnp.where(jnp.asarray(valid)[None, None, None], b, NEG_INF)
    return jnp.transpose(b, (0, 1, 3, 2, 4)).reshape(H, win_r, GRID_W, win_r * GRID_W)


def _na_rows(rows):
    lane = lax.broadcasted_iota(jnp.int32, (1, rows[0][0].shape[1]), 1) // NA_DH
    scale = NA_DH ** -0.5
    ss = [[_bdot_raw(jnp.where(lane == i, q, 0.0) * scale, kw, "nt") + b for i, b in enumerate(bs)] for q, kw, _, bs in rows]
    es = [[jnp.exp(s - jnp.max(s, axis=1, keepdims=True)) for s in srow] for srow in ss]
    ps = [[e / jnp.sum(e, axis=1, keepdims=True) for e in erow] for erow in es]
    return [_lanes_by_head(lane, [_bdot_raw(p, vw, "nn") for p in prow]) for prow, (_, _, vw, _) in zip(ps, rows)]


def _lanes_by_head(lane, vals):
    if len(vals) == 2:
        return jnp.where(lane == 0, vals[0], vals[1])
    return sum(jnp.where(lane == i, v, 0.0) for i, v in enumerate(vals))


def _na_rows_bwd(rows):
    lane = lax.broadcasted_iota(jnp.int32, (1, rows[0][0].shape[1]), 1) // NA_DH
    scale = NA_DH ** -0.5
    heads = range(len(rows[0][3]))
    qis = [[jnp.where(lane == i, q, 0.0) * scale for i in heads] for q, _, _, _, _ in rows]
    dos = [[jnp.where(lane == i, do, 0.0) for i in heads] for _, _, _, _, do in rows]
    ss = [[_bdot_raw(qi, kw, "nt") + b for qi, b in zip(qrow, bs)] for qrow, (_, kw, _, bs, _) in zip(qis, rows)]
    dps = [[_bdot_raw(doi, vw, "nt") for doi in drow] for drow, (_, _, vw, _, _) in zip(dos, rows)]
    es = [[jnp.exp(s - jnp.max(s, axis=1, keepdims=True)) for s in srow] for srow in ss]
    ps = [[e / jnp.sum(e, axis=1, keepdims=True) for e in erow] for erow in es]
    dss = [[p * (dp - jnp.sum(dp * p, axis=1, keepdims=True)) for p, dp in zip(prow, dprow)] for prow, dprow in zip(ps, dps)]
    out = []
    for qrow, drow, prow, dsrow, (_, kw, _, _, _) in zip(qis, dos, ps, dss, rows):
        dq = _lanes_by_head(lane, [_bdot_raw(dsrow[i], kw, "nn") for i in heads]) * scale
        dk, dv = 0.0, 0.0
        for i in heads:
            dk = dk + _bdot_raw(dsrow[i], qrow[i], "tn")
            dv = dv + _bdot_raw(prow[i], drow[i], "tn")
        out.append((dq, dk, dv, dsrow))
    return out


def na_op(nq, nk, nv, bias):
    S, W = nq.shape
    rows = S // GRID_W
    win_r = bias.shape[1]
    nkeys = win_r * GRID_W
    hp = LANES // NA_DH
    npair = W // LANES
    RB = min(16, rows)
    nrb = rows // RB
    qspec = pl.BlockSpec((RB * GRID_W, LANES), lambda p, r: (r, p))
    kspec = pl.BlockSpec((S, LANES), lambda p, r: (0, p))
    bspec = pl.BlockSpec((hp, win_r, GRID_W, nkeys), lambda p, r: (p, 0, 0, 0))

    def window(r):
        r0 = jnp.clip(r - win_r // 2, 0, rows - win_r)
        return pl.multiple_of(r0 * GRID_W, GRID_W), r - r0

    def call_fwd(nq, nk, nv, bias):
        def body(q_ref, k_ref, v_ref, b_ref, o_ref):
            rb = pl.program_id(1)

            def step(j, c):
                args, q0s = [], []
                for u in range(NA_ROWS_PER_STEP):
                    i = j * NA_ROWS_PER_STEP + u
                    k0, d = window(rb * RB + i)
                    q0 = pl.multiple_of(i * GRID_W, GRID_W)
                    q0s.append(q0)
                    args.append((q_ref[pl.ds(q0, GRID_W), :].astype(f32), k_ref[pl.ds(k0, nkeys), :], v_ref[pl.ds(k0, nkeys), :],
                                 [b_ref[h, pl.ds(d, 1)][0] for h in range(hp)]))
                for q0, o in zip(q0s, _na_rows(args)):
                    o_ref[pl.ds(q0, GRID_W), :] = o.astype(o_ref.dtype)
                return c

            lax.fori_loop(0, RB // NA_ROWS_PER_STEP, step, 0)

        return pl.pallas_call(
            body,
            out_shape=jax.ShapeDtypeStruct((S, W), nq.dtype),
            grid=(npair, nrb),
            in_specs=[qspec, kspec, kspec, bspec],
            out_specs=qspec,
            compiler_params=_cparams(("parallel", "arbitrary")),
            name="na_fwd",
        )(nq, nk, nv, bias)

    def call_bwd(nq, nk, nv, bias, do):
        def body(q_ref, k_ref, v_ref, b_ref, do_ref, dq_ref, dk_ref, dv_ref, db_ref, dk_acc, dv_acc):
            rb = pl.program_id(1)

            @pl.when(rb == 0)
            def _():
                dk_acc[...] = jnp.zeros_like(dk_acc)
                dv_acc[...] = jnp.zeros_like(dv_acc)
                db_ref[...] = jnp.zeros_like(db_ref)

            def step(j, c):
                args, spots = [], []
                for u in range(NA_ROWS_PER_STEP):
                    i = j * NA_ROWS_PER_STEP + u
                    k0, d = window(rb * RB + i)
                    q0 = pl.multiple_of(i * GRID_W, GRID_W)
                    spots.append((q0, k0, d))
                    args.append((q_ref[pl.ds(q0, GRID_W), :].astype(f32), k_ref[pl.ds(k0, nkeys), :], v_ref[pl.ds(k0, nkeys), :],
                                 [b_ref[h, pl.ds(d, 1)][0] for h in range(hp)], do_ref[pl.ds(q0, GRID_W), :].astype(f32)))
                for (q0, k0, d), (dq, dk, dv, dbs) in zip(spots, _na_rows_bwd(args)):
                    dq_ref[pl.ds(q0, GRID_W), :] = dq.astype(dq_ref.dtype)
                    dk_acc[pl.ds(k0, nkeys), :] += dk
                    dv_acc[pl.ds(k0, nkeys), :] += dv
                    for h in range(hp):
                        db_ref[h, pl.ds(d, 1)] += dbs[h][None]
                return c

            lax.fori_loop(0, RB // NA_ROWS_PER_STEP, step, 0)

            @pl.when(rb == nrb - 1)
            def _():
                dk_ref[...] = dk_acc[...].astype(dk_ref.dtype)
                dv_ref[...] = dv_acc[...].astype(dv_ref.dtype)

        return pl.pallas_call(
            body,
            out_shape=[jax.ShapeDtypeStruct((S, W), nq.dtype), jax.ShapeDtypeStruct((S, W), nk.dtype),
                       jax.ShapeDtypeStruct((S, W), nv.dtype), jax.ShapeDtypeStruct(bias.shape, f32)],
            grid=(npair, nrb),
            in_specs=[qspec, kspec, kspec, bspec, qspec],
            out_specs=[qspec, kspec, kspec, bspec],
            scratch_shapes=[pltpu.VMEM((S, LANES), f32), pltpu.VMEM((S, LANES), f32)],
            compiler_params=_cparams(("parallel", "arbitrary")),
            name="na_bwd",
        )(nq, nk, nv, bias, do)

    @jax.custom_vjp
    def op(nq, nk, nv, bias):
        return call_fwd(nq, nk, nv, bias)

    def fwd(nq, nk, nv, bias):
        return call_fwd(nq, nk, nv, bias), (nq, nk, nv, bias)

    def bwd(res, do):
        return tuple(call_bwd(*res, do))

    op.defvjp(fwd, bwd)
    return op(nq, nk, nv, bias)


def ssd_dt(dt_raw, dt_bias, a_neg):
    def fn(r, b, a):
        dt = _softplus(r + b)
        return dt, dt * a

    return rowwise(fn, "ssd_dt", [dt_raw], [dt_bias, a_neg], [f32, f32])


def ssd_post(y, xs, z, d_skip_lanes, norm_g, groups):
    def fn(y, xs, z, dsk, g):
        y = (y + xs * dsk) * _silu(z)
        return (y * lax.rsqrt(jnp.mean(y * y, axis=-1, keepdims=True) + EPS) * g,)

    def bwd_fn(rv, pv, dos):
        (y, xs, z), (dsk, g), (do,) = rv, pv, dos
        sg = 1.0 / (1.0 + jnp.exp(-z))
        s, u = z * sg, y + xs * dsk
        dw, dg = _rms_bwd(u * s, g, do)
        du = dw * s
        dz = dw * u * (sg * (1.0 + z * (1.0 - sg)))
        return (du, du * dsk, dz), (jnp.sum(du * xs, axis=0, keepdims=True), dg)

    return rowwise(fn, "ssd_post", [y, xs, z], [d_skip_lanes.reshape(1, -1), norm_g.reshape(1, -1)], [MXU_DTYPE],
                   ncol=groups, bwd_fn=bwd_fn)[0]


def _heads_major(t, groups):
    S = t.shape[0]
    return jnp.transpose(t.reshape(S, groups, -1), (1, 0, 2))


def retention_na_mixer(hn, w_in, decay_logit, gn_g, rpb, w_out, tables):
    S = hn.shape[0]
    R = RET_HEADS * RET_DH
    NW = NA_HEADS * NA_DH
    cols = lambda a, b: w_in[:, a:b]
    rq, rk, rv, rg = (mm(hn, cols(j * R, (j + 1) * R)) for j in range(4))
    nq, nk, nv = (mm(hn, cols(4 * R + j * NW, 4 * R + (j + 1) * NW), out_dtype=MXU_DTYPE) for j in range(3))
    qr, kr = rotary(rq, rk, *tables)
    log_gamma = -_softplus(-decay_logit.astype(f32))
    hp = LANES // RET_DH
    hpad = -(-RET_HEADS // 8) * 8
    pad8 = lambda a: jnp.pad(a.reshape(1, 1, RET_HEADS), ((0, 0), (0, 0), (0, hpad - RET_HEADS)))
    a_f = jnp.broadcast_to(pad8(log_gamma[0]), (1, S, hpad))
    a_b = jnp.broadcast_to(pad8(log_gamma[1]), (1, S, hpad))
    ret = ret_post(bidir_scan(qr, kr, rv, a_f, a_b, None, None, nsub=hp), rg, gn_g)
    rows = S // GRID_W
    nao = na_op(nq, nk, nv, _na_bias(rpb, min(NA_WIN_R, rows)))
    return mm(ret, w_out[:R]) + mm(nao, w_out[R:])


def ssd_mixer(hn, w_in, conv_w, conv_b, dt_bias, a_log, d_skip, norm_g, w_out):
    heads = d_skip.shape[0]
    inner = heads * SSD_HEADDIM
    gs = SSD_GROUPS * SSD_STATE
    o_x, o_b, o_c, o_dt = inner, 2 * inner, 2 * inner + gs, 2 * inner + 2 * gs
    z = mm(hn, w_in[:, :inner])
    dt_raw = mm(hn, w_in[:, o_dt:])
    xs, bm, cm = (mm_conv_act(hn, [w_in[:, a:b]], [conv_w[:, a - inner:b - inner]], [conv_b[a - inner:b - inner]],
                              _silu, _silu_bwd, f32, "conv_silu") for a, b in ((o_x, o_b), (o_b, o_c), (o_c, o_dt)))
    a_neg = -jnp.exp(a_log.astype(f32)).reshape(1, -1)
    dt, la = ssd_dt(dt_raw, dt_bias.astype(f32).reshape(1, -1), a_neg)
    dt_f, dt_b = _heads_major(dt[:, :heads], SSD_GROUPS), _heads_major(dt[:, heads:], SSD_GROUPS)
    la_f, la_b = _heads_major(la[:, :heads], SSD_GROUPS), _heads_major(la[:, heads:], SSD_GROUPS)
    y = bidir_scan(cm, bm, xs, la_f, la_b, dt_f, dt_b, nsub=1)
    y = ssd_post(y, xs, z, jnp.repeat(d_skip.astype(f32), SSD_HEADDIM), norm_g, SSD_GROUPS)
    return mm(y, w_out)


def conv_geglu_ffn(hf, w_up, conv_w, conv_b, w_down):
    F = w_down.shape[0]
    a = mm_conv_act(hf, [w_up[:, :F], w_up[:, F:]], [conv_w[:, :F], conv_w[:, F:]], [conv_b[:F], conv_b[F:]],
                    _geglu, _geglu_bwd, MXU_DTYPE, "conv_geglu")
    return mm(a, w_down)


def model_loss(x, tgt, big, small, rep):
    S = x.shape[0]
    depth = rep["norm_mix_pre"].shape[0]
    tables = _rope_tables(S, RET_HEADS * RET_DH)
    hn = rms(x, rep["norm_mix_pre"][0], MXU_DTYPE)
    for layer in range(depth):
        i = layer // 2
        if layer % 2 == 0:
            m = retention_na_mixer(hn, big["ab_w_in"][i], rep["ab_ret_decay_logit"][i], rep["ab_ret_gn_g"][i],
                                   rep["ab_na_rpb"][i], big["ab_w_out"][i], tables)
        else:
            m = ssd_mixer(hn, big["c_w_in"][i], small["c_conv_w"][i], small["c_conv_b"][i], rep["c_dt_bias"][i],
                          rep["c_a_log"][i], rep["c_d_skip"][i], small["c_norm_g"][i], big["c_w_out"][i])
        x, hf = rms_residual_norm(m, rep["norm_mix_post"][layer], x, rep["norm_ffn_pre"][layer])
        f = conv_geglu_ffn(hf, big["ffn_w_up"][layer], small["ffn_conv_w"][layer], rep["ffn_conv_b"][layer],
                           big["ffn_w_down"][layer])
        if layer + 1 < depth:
            x, hn = rms_residual_norm(f, rep["norm_ffn_post"][layer], x, rep["norm_mix_pre"][layer + 1])
        else:
            x = rms_residual(f, rep["norm_ffn_post"][layer], x)
    return loss_op(x, tgt)


def _mesh_pos():
    return lax.axis_index("x"), lax.axis_index("y"), lax.axis_index("c")


def _any_specs(n):
    return [pl.BlockSpec(memory_space=pl.ANY)] * n


def gather_chips(locals_):
    nbuf = len(locals_)
    CH = COPY_CHUNKS

    def body(*refs):
        x_refs, out_refs, (send_sems, recv_sems) = refs[:nbuf], refs[nbuf:2 * nbuf], refs[2 * nbuf:]
        x, y, c = _mesh_pos()
        my = 2 * x + y
        chips = [(1 - x, y), (x, 1 - y), (1 - x, 1 - y)]
        plans = []
        for a, (x_ref, out_ref) in enumerate(zip(x_refs, out_refs)):
            half = x_ref.shape[0] // 2
            q = half // CH

            def piece(ref, h, j, half=half, q=q):
                return ref.at[pl.ds(pl.multiple_of(h * half + j * q, PACK_ALIGN), q), :]

            def copy(k, src, chip, h, j, to, out_ref=out_ref, piece=piece, base=a * 6 * CH):
                return pltpu.make_async_remote_copy(src_ref=src, dst_ref=piece(out_ref.at[chip], h, j),
                                                    send_sem=send_sems.at[base + k], recv_sem=recv_sems.at[base + k],
                                                    device_id=to, device_id_type=pl.DeviceIdType.MESH)

            plans.append((x_ref, out_ref, piece, copy))

        first, passed = [], []
        for x_ref, out_ref, piece, copy in plans:
            first.append([[copy(k * CH + j, piece(x_ref, c, j), my, c, j, (cx, cy, c)) for j in range(CH)]
                          for k, (cx, cy) in enumerate(chips)])
            passed.append([[copy((3 + k) * CH + j, piece(out_ref.at[2 * cx + cy], c, j), 2 * cx + cy, c, j, (x, y, 1 - c))
                            for j in range(CH)] for k, (cx, cy) in enumerate(chips)])
        for a in range(nbuf):
            for j in range(CH):
                for k in range(3):
                    first[a][k][j].start()
        for a, (x_ref, _, piece, copy) in enumerate(plans):
            for j in range(CH):
                for k, (cx, cy) in enumerate(chips):
                    copy(k * CH + j, piece(x_ref, c, j), 2 * cx + cy, c, j, (cx, cy, c)).wait_recv()
                    passed[a][k][j].start()
        for a, (x_ref, _, piece, copy) in enumerate(plans):
            for j in range(CH):
                for k, (cx, cy) in enumerate(chips):
                    copy((3 + k) * CH + j, piece(x_ref, c, j), 2 * cx + cy, 1 - c, j, (x, y, 1 - c)).wait_recv()
        for a in range(nbuf):
            for k in range(3):
                for cp in first[a][k] + passed[a][k]:
                    cp.wait_send()

    return pl.pallas_call(
        body,
        out_shape=[jax.ShapeDtypeStruct((N_CHIPS,) + l.shape, l.dtype) for l in locals_],
        in_specs=_any_specs(nbuf),
        out_specs=_any_specs(nbuf),
        scratch_shapes=[pltpu.SemaphoreType.DMA((nbuf * 6 * CH,)), pltpu.SemaphoreType.DMA((nbuf * 6 * CH,))],
        name="gather_chips",
    )(*locals_)


def pair_swap(parts):
    nbuf = len(parts)
    n = N_CHIPS
    CH = COPY_CHUNKS

    def body(*refs):
        p_refs, got_refs, (send_sems, recv_sems) = refs[:nbuf], refs[nbuf:2 * nbuf], refs[2 * nbuf:]
        x, y, c = _mesh_pos()
        swap = []
        for a, (p_ref, got_ref) in enumerate(zip(p_refs, got_refs)):
            half = p_ref.shape[1] // 2
            q = half // CH
            for s in range(n):
                for j in range(CH):
                    k = (a * n + s) * CH + j
                    src = p_ref.at[s, pl.ds(pl.multiple_of((1 - c) * half + j * q, PACK_ALIGN), q), :]
                    swap.append(pltpu.make_async_remote_copy(src_ref=src, dst_ref=got_ref.at[s, pl.ds(j * q, q), :],
                                                             send_sem=send_sems.at[k], recv_sem=recv_sems.at[k],
                                                             device_id=(x, y, 1 - c), device_id_type=pl.DeviceIdType.MESH))
        for cp in swap:
            cp.start()
        for cp in swap:
            cp.wait()

    return pl.pallas_call(
        body,
        out_shape=[jax.ShapeDtypeStruct((n, p.shape[1] // 2, p.shape[2]), p.dtype) for p in parts],
        in_specs=_any_specs(nbuf),
        out_specs=_any_specs(nbuf),
        scratch_shapes=[pltpu.SemaphoreType.DMA((nbuf * n * CH,)), pltpu.SemaphoreType.DMA((nbuf * n * CH,))],
        name="pair_swap",
    )(*parts)


def chip_exchange(parts):
    nbuf = len(parts)

    def body(*refs):
        p_refs, out_refs, (send_sems, recv_sems) = refs[:nbuf], refs[nbuf:2 * nbuf], refs[2 * nbuf:]
        x, y, c = _mesh_pos()
        my = 2 * x + y
        chips = [(1 - x, y), (x, 1 - y), (1 - x, 1 - y)]

        def copy(a, k, src_slot, dst_slot, to):
            return pltpu.make_async_remote_copy(src_ref=p_refs[a].at[src_slot], dst_ref=out_refs[a].at[dst_slot],
                                                send_sem=send_sems.at[3 * a + k], recv_sem=recv_sems.at[3 * a + k],
                                                device_id=to, device_id_type=pl.DeviceIdType.MESH)

        sends = [copy(a, k, 2 * cx + cy, my, (cx, cy, c)) for a in range(nbuf) for k, (cx, cy) in enumerate(chips)]
        for cp in sends:
            cp.start()
        for a in range(nbuf):
            for k, (cx, cy) in enumerate(chips):
                copy(a, k, my, 2 * cx + cy, (cx, cy, c)).wait_recv()
        for cp in sends:
            cp.wait_send()

    return pl.pallas_call(
        body,
        out_shape=[jax.ShapeDtypeStruct(p.shape, p.dtype) for p in parts],
        in_specs=_any_specs(nbuf),
        out_specs=_any_specs(nbuf),
        scratch_shapes=[pltpu.SemaphoreType.DMA((3 * nbuf,)), pltpu.SemaphoreType.DMA((3 * nbuf,))],
        name="chip_exchange",
    )(*parts)


def pair_share(mine):
    nbuf = len(mine)
    CH = COPY_CHUNKS

    def body(*refs):
        m_refs, out_refs, (send_sems, recv_sems) = refs[:nbuf], refs[nbuf:2 * nbuf], refs[2 * nbuf:]
        x, y, c = _mesh_pos()
        swap = []
        for a, (m_ref, out_ref) in enumerate(zip(m_refs, out_refs)):
            q = m_ref.shape[0] // CH
            for j in range(CH):
                swap.append(pltpu.make_async_remote_copy(src_ref=m_ref.at[pl.ds(j * q, q), :], dst_ref=out_ref.at[pl.ds(j * q, q), :],
                                                         send_sem=send_sems.at[a * CH + j], recv_sem=recv_sems.at[a * CH + j],
                                                         device_id=(x, y, 1 - c), device_id_type=pl.DeviceIdType.MESH))
        for cp in swap:
            cp.start()
        for cp in swap:
            cp.wait()

    return pl.pallas_call(
        body,
        out_shape=[jax.ShapeDtypeStruct(m.shape, m.dtype) for m in mine],
        in_specs=_any_specs(nbuf),
        out_specs=_any_specs(nbuf),
        scratch_shapes=[pltpu.SemaphoreType.DMA((nbuf * CH,)), pltpu.SemaphoreType.DMA((nbuf * CH,))],
        name="pair_share",
    )(*mine)


def sum_chips(recv, own):
    n, R, Wd = recv.shape
    tr = _pick(R, (512, 256, 128, 64, 32, 16, 8))

    def body(r_ref, p_ref, o_ref):
        my = 2 * lax.axis_index("x") + lax.axis_index("y")
        acc = jnp.zeros((tr, Wd), f32)
        for s in range(n):
            acc = acc + jnp.where(my == s, p_ref[s], r_ref[s]).astype(f32)
        o_ref[...] = acc

    spec = pl.BlockSpec((n, tr, Wd), lambda i: (0, i, 0))
    return pl.pallas_call(
        body,
        out_shape=jax.ShapeDtypeStruct((R, Wd), f32),
        grid=(R // tr,),
        in_specs=[spec, spec],
        out_specs=pl.BlockSpec((tr, Wd), lambda i: (i, 0)),
        compiler_params=_cparams(("parallel",)),
        name="sum_chips",
    )(recv, own)


def add_pair(parts, got):
    n, R, Wd = parts.shape
    half = R // 2
    tr = _pick(half, (512, 256, 128, 64, 32, 16, 8))
    nb = half // tr

    def body(lo_ref, hi_ref, g_ref, o_ref):
        mine = jnp.where(lax.axis_index("c") == 0, lo_ref[...], hi_ref[...])
        o_ref[...] = (mine.astype(f32) + g_ref[...].astype(f32)).astype(o_ref.dtype)

    spec = pl.BlockSpec((1, tr, Wd), lambda s, i: (s, i, 0))
    return pl.pallas_call(
        body,
        out_shape=jax.ShapeDtypeStruct(got.shape, parts.dtype),
        grid=(n, nb),
        in_specs=[spec, pl.BlockSpec((1, tr, Wd), lambda s, i: (s, nb + i, 0)), spec],
        out_specs=spec,
        compiler_params=_cparams(("parallel", "parallel")),
        name="add_pair",
    )(parts, parts, got)


def reduce_scatter(parts):
    chip_sum = [add_pair(p, g) for p, g in zip(parts, pair_swap(parts))]
    mine = [sum_chips(r, s) for r, s in zip(chip_exchange(chip_sum), chip_sum)]
    first = lax.axis_index("c") == 0
    return [jnp.concatenate([jnp.where(first, m, t), jnp.where(first, t, m)], axis=0) for m, t in zip(mine, pair_share(mine))]


def adamw(w, g, m, v):
    shp = w.shape
    if w.size * 4 <= (1 << 20):
        grid, block, imap = (1,), shp, lambda i: (0,) * len(shp)
    else:
        n0, R, C = shp
        tr = _divisor_tile(R, lambda t: t * C * 4 <= (1 << 20), 8)
        grid, block, imap = (n0, R // tr), (1, tr, C), lambda j, i: (j, i, 0)

    def body(w_ref, g_ref, m_ref, v_ref, d_ref, mo_ref, vo_ref):
        g = g_ref[...]
        m = ADAM_B1 * m_ref[...] + (1.0 - ADAM_B1) * g
        v = ADAM_B2 * v_ref[...] + (1.0 - ADAM_B2) * (g * g)
        m_hat = m / (1.0 - ADAM_B1 ** ADAM_STEP)
        v_hat = v / (1.0 - ADAM_B2 ** ADAM_STEP)
        d_ref[...] = -ADAM_LR * (m_hat / (jnp.sqrt(v_hat) + ADAM_EPS) + ADAM_WD * w_ref[...])
        mo_ref[...] = m
        vo_ref[...] = v

    spec = pl.BlockSpec(block, imap)
    return pl.pallas_call(
        body,
        out_shape=[jax.ShapeDtypeStruct(shp, f32)] * 3,
        grid=grid,
        in_specs=[spec] * 4,
        out_specs=[spec] * 3,
        compiler_params=_cparams(("parallel",) * len(grid)),
        name="adamw",
    )(w, g, m, v)


def _pack(arrs, dtype):
    flat = jnp.concatenate([a.astype(dtype).reshape(-1) for a in arrs])
    n = flat.shape[0]
    unit = PACK_W * PACK_ROWS
    padded = -(-n // unit) * unit
    return jnp.pad(flat, (0, padded - n)).reshape(-1, PACK_W)


def _unpack(buf, shapes):
    flat = buf.reshape(-1)
    out, off = [], 0
    for s in shapes:
        n = int(np.prod(s))
        out.append(flat[off:off + n].reshape(s))
        off += n
    return out


BIG = (("ab_w_in", 2), ("ab_w_out", 1), ("c_w_in", 2), ("c_w_out", 1), ("ffn_w_up", 2), ("ffn_w_down", 1))
SMALL = (("c_conv_w", 2), ("c_conv_b", 1), ("c_norm_g", 1), ("ffn_conv_w", 2))
REP = ("norm_mix_pre", "norm_mix_post", "norm_ffn_pre", "norm_ffn_post", "ab_ret_decay_logit", "ab_ret_gn_g", "ab_na_rpb",
       "c_dt_bias", "c_a_log", "c_d_skip", "ffn_conv_b")
WEIGHTS = ("norm_mix_pre", "norm_mix_post", "norm_ffn_pre", "norm_ffn_post", "ab_w_in", "ab_ret_decay_logit", "ab_ret_gn_g",
           "ab_na_rpb", "ab_w_out", "c_w_in", "c_conv_w", "c_conv_b", "c_dt_bias", "c_a_log", "c_d_skip", "c_norm_g", "c_w_out",
           "ffn_w_up", "ffn_conv_w", "ffn_conv_b", "ffn_w_down")


BIG_GROUPS = (("ab_w_in",), ("ab_w_out", "c_w_out", "ffn_w_down"), ("c_w_in",), ("ffn_w_up",))
BIG_AXIS = dict(BIG)
ROW_UNIT = 128


def _rows(arrs, dtype):
    C = arrs[0].shape[-1]
    buf = jnp.concatenate([a.astype(dtype).reshape(-1, C) for a in arrs], axis=0)
    pad = -buf.shape[0] % ROW_UNIT
    return jnp.pad(buf, ((0, pad), (0, 0))) if pad else buf


def _unrows(buf, shapes):
    out, off = [], 0
    for s in shapes:
        n = int(np.prod(s[:-1]))
        out.append(buf[off:off + n].reshape(s))
        off += n
    return out


def _gather_all(w):
    bufs = [_rows([w[n] for n in grp], MXU_DTYPE) for grp in BIG_GROUPS] + [_pack([w[n] for n, _ in SMALL], f32)]
    got = gather_chips(bufs)
    my = 2 * lax.axis_index("x") + lax.axis_index("y")

    def whole(n, ax, dtype, pieces):
        return jnp.concatenate([jnp.where(my == s, w[n].astype(dtype), pieces[s]) for s in range(N_CHIPS)], axis=ax)

    big = {}
    for grp, g in zip(BIG_GROUPS, got):
        per_chip = [_unrows(g[s], [w[n].shape for n in grp]) for s in range(N_CHIPS)]
        for j, n in enumerate(grp):
            big[n] = whole(n, BIG_AXIS[n], MXU_DTYPE, [per_chip[s][j] for s in range(N_CHIPS)])
    per_chip = [_unpack(got[-1][s], [w[n].shape for n, _ in SMALL]) for s in range(N_CHIPS)]
    small = {n: whole(n, ax, f32, [per_chip[s][j] for s in range(N_CHIPS)]) for j, (n, ax) in enumerate(SMALL)}
    return big, small


def _reduce_all(gbig, gsmall, grep, w):
    split = {n: jnp.split(g, N_CHIPS, axis=BIG_AXIS[n]) for n, g in gbig.items()}
    parts = [jnp.stack([_rows([split[n][s] for n in grp], MXU_DTYPE) for s in range(N_CHIPS)]) for grp in BIG_GROUPS]
    ssplit = {n: jnp.split(gsmall[n], N_CHIPS, axis=ax) for n, ax in SMALL}
    parts.append(jnp.stack([_pack([ssplit[n][s] for n, _ in SMALL] + [grep[n] for n in REP], f32) for s in range(N_CHIPS)]))
    res = reduce_scatter(parts)
    grads = {}
    for grp, r in zip(BIG_GROUPS, res):
        grads.update(zip(grp, _unrows(r, [w[n].shape for n in grp])))
    small_names = [n for n, _ in SMALL] + list(REP)
    grads.update(zip(small_names, _unpack(res[-1], [w[n].shape for n in small_names])))
    return grads


def kernel(x, norm_mix_pre, norm_mix_post, norm_ffn_pre, norm_ffn_post, ab_w_in, ab_ret_decay_logit, ab_ret_gn_g, ab_na_rpb, ab_w_out, c_w_in, c_conv_w, c_conv_b, c_dt_bias, c_a_log, c_d_skip, c_norm_g, c_w_out, ffn_w_up, ffn_conv_w, ffn_conv_b, ffn_w_down, loss_target, m_norm_mix_pre, m_norm_mix_post, m_norm_ffn_pre, m_norm_ffn_post, m_ab_w_in, m_ab_ret_decay_logit, m_ab_ret_gn_g, m_ab_na_rpb, m_ab_w_out, m_c_w_in, m_c_conv_w, m_c_conv_b, m_c_dt_bias, m_c_a_log, m_c_d_skip, m_c_norm_g, m_c_w_out, m_ffn_w_up, m_ffn_conv_w, m_ffn_conv_b, m_ffn_w_down, v_norm_mix_pre, v_norm_mix_post, v_norm_ffn_pre, v_norm_ffn_post, v_ab_w_in, v_ab_ret_decay_logit, v_ab_ret_gn_g, v_ab_na_rpb, v_ab_w_out, v_c_w_in, v_c_conv_w, v_c_conv_b, v_c_dt_bias, v_c_a_log, v_c_d_skip, v_c_norm_g, v_c_w_out, v_ffn_w_up, v_ffn_conv_w, v_ffn_conv_b, v_ffn_w_down):
    args = dict(locals())
    w = {n: args[n] for n in WEIGHTS}
    mom = {n: args["m_" + n] for n in WEIGHTS}
    var = {n: args["v_" + n] for n in WEIGHTS}

    big, small = _gather_all(w)
    rep = {n: w[n] for n in REP}

    def loss_fn(xs, big, small, rep):
        return model_loss(xs, loss_target[0], big, small, rep)

    loss, (gx, gbig, gsmall, grep) = jax.value_and_grad(loss_fn, argnums=(0, 1, 2, 3))(x[0], big, small, rep)
    loss = lax.psum(loss, ("x", "y", "c"))

    grads = _reduce_all(gbig, gsmall, grep, w)

    delta, new_m, new_v = {}, {}, {}
    for n in WEIGHTS:
        delta[n], new_m[n], new_v[n] = adamw(w[n], grads[n], mom[n], var[n])

    return (loss, gx[None], *[grads[n] for n in WEIGHTS], *[delta[n] for n in WEIGHTS],
            *[new_m[n] for n in WEIGHTS], *[new_v[n] for n in WEIGHTS])
```

```python
import functools
import math

import numpy as np
import jax
import jax.numpy as jnp
from jax import lax
from jax.experimental import pallas as pl
from jax.experimental.pallas import tpu as pltpu

f32 = jnp.float32
bf16 = jnp.bfloat16
MXU_DTYPE = bf16

GRID_W = 64
CHUNK = 128
EPS = 1e-6
RET_HEADS = 8
RET_DH = 64
ROPE_BASE = 10000.0
NA_HEADS = 8
NA_DH = 64
NA_WIN_R = 8
NA_WIN_C = 16
NA_ROWS_PER_STEP = 8
SCAN_CHUNKS_PER_STEP = 4
SSD_HEADDIM = 64
SSD_GROUPS = 4
SSD_STATE = 128
ADAM_LR = 0.001
ADAM_B1 = 0.9
ADAM_B2 = 0.999
ADAM_EPS = 1e-08
ADAM_WD = 0.01
ADAM_STEP = 10

LANES = 128
HEAD_W = 64
PACK_W = 512
PACK_ROWS = 1024
PACK_ALIGN = 16
COPY_CHUNKS = 2
VMEM_LIMIT = 56 * 1024 * 1024
MM_BLOCK_BYTES = 6 * 1024 * 1024
ROW_BLOCK_BYTES = 16 * 1024 * 1024
N_CHIPS = 4
N_DEV = 8
NEG_INF = -1e30

_DIMS = {"nn": (((1,), (0,)), ((), ())), "nt": (((1,), (1,)), ((), ())), "tn": (((0,), (0,)), ((), ()))}


def _cparams(sem=None):
    return pltpu.CompilerParams(dimension_semantics=sem, vmem_limit_bytes=VMEM_LIMIT)


def _pick(dim, cands):
    for c in cands:
        if dim % c == 0:
            return c
    return dim


def _divisor_tile(dim, fits, align):
    for d in range(1, dim + 1):
        t = dim // d
        if dim % d == 0 and t % align == 0 and fits(t):
            return t
    return dim


def _bdot_raw(a, b, mode):
    return lax.dot_general(a.astype(MXU_DTYPE), b.astype(MXU_DTYPE), _DIMS[mode], preferred_element_type=f32)


@functools.partial(jax.custom_vjp, nondiff_argnums=(2,))
def bdot(a, b, mode):
    return _bdot_raw(a, b, mode)


def _bdot_fwd(a, b, mode):
    return _bdot_raw(a, b, mode), (a, b)


def _bdot_bwd(mode, res, g):
    a, b = res
    if mode == "nn":
        da, db = _bdot_raw(g, b, "nt"), _bdot_raw(a, g, "tn")
    elif mode == "nt":
        da, db = _bdot_raw(g, b, "nn"), _bdot_raw(g, a, "tn")
    else:
        da, db = _bdot_raw(b, g, "nt"), _bdot_raw(a, g, "nn")
    return da.astype(a.dtype), db.astype(b.dtype)


bdot.defvjp(_bdot_fwd, _bdot_bwd)


def _mm_call(a, b, mode, out_dtype):
    if mode == "nn":
        (M, K), (K2, N) = a.shape, b.shape
    elif mode == "nt":
        (M, K), (N, K2) = a.shape, b.shape
    else:
        (K, M), (K2, N) = a.shape, b.shape
    assert K == K2, (a.shape, b.shape, mode)
    a_bytes, b_bytes, o_bytes = a.dtype.itemsize, b.dtype.itemsize, jnp.dtype(out_dtype).itemsize
    if mode == "tn":
        tn = _divisor_tile(N, lambda t: t <= 1536, LANES)
        tm = _divisor_tile(M, lambda t: t * tn * 4 <= MM_BLOCK_BYTES, 8)
        tk = _divisor_tile(K, lambda t: t * tm * a_bytes <= MM_BLOCK_BYTES and t * tn * b_bytes <= MM_BLOCK_BYTES, LANES)
    else:
        tk, tn = K, N
        tm = _divisor_tile(M, lambda t: t * K * a_bytes <= MM_BLOCK_BYTES and t * N * o_bytes <= MM_BLOCK_BYTES, 8)
    nk = K // tk
    if mode == "nn":
        a_spec = pl.BlockSpec((tm, tk), lambda i, j, k: (i, k))
        b_spec = pl.BlockSpec((tk, tn), lambda i, j, k: (k, j))
    elif mode == "nt":
        a_spec = pl.BlockSpec((tm, tk), lambda i, j, k: (i, k))
        b_spec = pl.BlockSpec((tn, tk), lambda i, j, k: (j, k))
    else:
        a_spec = pl.BlockSpec((tk, tm), lambda i, j, k: (k, i))
        b_spec = pl.BlockSpec((tk, tn), lambda i, j, k: (k, j))

    if nk == 1:
        def body(a_ref, b_ref, o_ref):
            o_ref[...] = _bdot_raw(a_ref[...], b_ref[...], mode).astype(o_ref.dtype)
    else:
        def body(a_ref, b_ref, o_ref, acc_ref):
            k = pl.program_id(2)

            @pl.when(k == 0)
            def _():
                acc_ref[...] = jnp.zeros_like(acc_ref)

            acc_ref[...] += _bdot_raw(a_ref[...], b_ref[...], mode)

            @pl.when(k == nk - 1)
            def _():
                o_ref[...] = acc_ref[...].astype(o_ref.dtype)

    return pl.pallas_call(
        body,
        out_shape=jax.ShapeDtypeStruct((M, N), out_dtype),
        grid=(M // tm, N // tn, nk),
        in_specs=[a_spec, b_spec],
        out_specs=pl.BlockSpec((tm, tn), lambda i, j, k: (i, j)),
        scratch_shapes=[pltpu.VMEM((tm, tn), f32)] if nk > 1 else [],
        compiler_params=_cparams(("parallel", "parallel", "arbitrary")),
        name="mm_" + mode,
    )(a, b)


def mm_pair(a1, b1, a2, b2):
    M, N = a1.shape[0], b1.shape[1]
    K1, K2 = a1.shape[1], a2.shape[1]
    tm = _divisor_tile(M, lambda t: t * (K1 + K2) * a1.dtype.itemsize <= MM_BLOCK_BYTES and t * N * 4 <= MM_BLOCK_BYTES, 8)

    def call(a1, b1, a2, b2):
        def body(a1_ref, b1_ref, a2_ref, b2_ref, o_ref):
            o_ref[...] = _bdot_raw(a1_ref[...], b1_ref[...], "nn") + _bdot_raw(a2_ref[...], b2_ref[...], "nn")

        return pl.pallas_call(
            body,
            out_shape=jax.ShapeDtypeStruct((M, N), f32),
            grid=(M // tm,),
            in_specs=[pl.BlockSpec((tm, K1), lambda i: (i, 0)), pl.BlockSpec((K1, N), lambda i: (0, 0)),
                      pl.BlockSpec((tm, K2), lambda i: (i, 0)), pl.BlockSpec((K2, N), lambda i: (0, 0))],
            out_specs=pl.BlockSpec((tm, N), lambda i: (i, 0)),
            compiler_params=_cparams(("parallel",)),
            name="mm_pair",
        )(a1, b1, a2, b2)

    @jax.custom_vjp
    def op(a1, b1, a2, b2):
        return call(a1, b1, a2, b2)

    def fwd(a1, b1, a2, b2):
        return call(a1, b1, a2, b2), (a1, b1, a2, b2)

    def bwd(res, g):
        a1, b1, a2, b2 = res
        return (_mm_call(g, b1, "nt", a1.dtype), _mm_call(a1, g, "tn", b1.dtype),
                _mm_call(g, b2, "nt", a2.dtype), _mm_call(a2, g, "tn", b2.dtype))

    op.defvjp(fwd, bwd)
    return op(a1, b1, a2, b2)


def mm(a, b, mode="nn", out_dtype=f32):
    @jax.custom_vjp
    def op(a, b):
        return _mm_call(a, b, mode, out_dtype)

    def fwd(a, b):
        return _mm_call(a, b, mode, out_dtype), (a, b)

    def bwd(res, g):
        a, b = res
        if mode == "nn":
            return _mm_call(g, b, "nt", a.dtype), _mm_call(a, g, "tn", b.dtype)
        if mode == "nt":
            return _mm_call(g, b, "nn", a.dtype), _mm_call(g, a, "tn", b.dtype)
        return _mm_call(b, g, "nt", a.dtype), _mm_call(a, g, "nn", b.dtype)

    op.defvjp(fwd, bwd)
    return op(a, b)


def _row_tile(S, row_bytes):
    tm = 1024
    while tm > 8 and (tm * row_bytes > ROW_BLOCK_BYTES or S % tm):
        tm //= 2
    return tm


def rowwise(fn, name, rows, params, out_dtypes, n_diff_rows=None, n_diff_params=None, ncol=1, bwd_fn=None):
    rows, params = list(rows), list(params)
    nr, npar = len(rows), len(params)
    ndr = nr if n_diff_rows is None else n_diff_rows
    ndp = npar if n_diff_params is None else n_diff_params
    S = rows[0].shape[0]
    rw = [r.shape[1] // ncol for r in rows]
    pshape = [(p.shape[0], p.shape[1] // ncol) for p in params]

    def block_structs(tm):
        return ([jax.ShapeDtypeStruct((tm, w), f32) for w in rw] + [jax.ShapeDtypeStruct(s, f32) for s in pshape])

    outs_s = jax.eval_shape(fn, *block_structs(8))
    ow = [o.shape[1] for o in outs_s]
    nout = len(ow)
    row_bytes = 4 * (sum(rw) * 2 + sum(ow) * 2)
    tm = _row_tile(S, row_bytes)
    grid = (ncol, S // tm)

    def rspec(w):
        return pl.BlockSpec((tm, w), lambda g, i: (i, g))

    def pspec(s):
        return pl.BlockSpec(s, lambda g, i: (0, g))

    def call_fwd(*args):
        def body(*refs):
            vals = [r[...].astype(f32) for r in refs[:nr + npar]]
            res = fn(*vals)
            for o, r in zip(refs[nr + npar:], res):
                o[...] = r.astype(o.dtype)

        return pl.pallas_call(
            body,
            out_shape=[jax.ShapeDtypeStruct((S, w * ncol), dt) for w, dt in zip(ow, out_dtypes)],
            grid=grid,
            in_specs=[rspec(w) for w in rw] + [pspec(s) for s in pshape],
            out_specs=[rspec(w) for w in ow],
            compiler_params=_cparams(("parallel", "parallel")),
            name=name + "_fwd",
        )(*args)

    def call_bwd(args, douts):
        def body(*refs):
            in_refs = refs[:nr + npar]
            do_refs = refs[nr + npar:nr + npar + nout]
            dr_refs = refs[nr + npar + nout:nr + npar + nout + ndr]
            dp_refs = refs[nr + npar + nout + ndr:]
            rv = [r[...] for r in in_refs[:nr]]
            pv = [r[...] for r in in_refs[nr:]]
            dos = [d[...].astype(f32) for d in do_refs]
            if bwd_fn is not None:
                drs, dps = bwd_fn(rv, pv, dos)
            else:
                def f(*a):
                    return fn(*a[:ndr], *rv[ndr:], *a[ndr:], *pv[ndp:])

                _, vjp = jax.vjp(f, *[v.astype(f32) for v in rv[:ndr]], *pv[:ndp])
                cts = vjp(tuple(dos))
                drs, dps = cts[:ndr], cts[ndr:]
            for r, ct in zip(dr_refs, drs):
                r[...] = ct.astype(r.dtype)
            if ndp:
                @pl.when(pl.program_id(1) == 0)
                def _():
                    for r in dp_refs:
                        r[...] = jnp.zeros_like(r)

                for r, ct in zip(dp_refs, dps):
                    r[...] += ct

        return pl.pallas_call(
            body,
            out_shape=[jax.ShapeDtypeStruct(r.shape, r.dtype) for r in rows[:ndr]]
            + [jax.ShapeDtypeStruct(p.shape, f32) for p in params[:ndp]],
            grid=grid,
            in_specs=[rspec(w) for w in rw] + [pspec(s) for s in pshape] + [rspec(w) for w in ow],
            out_specs=[rspec(w) for w in rw[:ndr]] + [pspec(s) for s in pshape[:ndp]],
            compiler_params=_cparams(("parallel", "arbitrary")),
            name=name + "_bwd",
        )(*args, *douts)

    @jax.custom_vjp
    def op(*args):
        return tuple(call_fwd(*args))

    def fwd(*args):
        return tuple(call_fwd(*args)), args

    def bwd(args, douts):
        res = call_bwd(args, douts)
        drs, dps = res[:ndr], res[ndr:]
        out = list(drs) + [jnp.zeros_like(a) for a in args[ndr:nr]]
        out += [dp.astype(p.dtype) for dp, p in zip(dps, args[nr:nr + ndp])]
        out += [jnp.zeros_like(a) for a in args[nr + ndp:]]
        return tuple(out)

    op.defvjp(fwd, bwd)
    return op(*rows, *params)


def _silu(x):
    return x * (1.0 / (1.0 + jnp.exp(-x)))


def _softplus(x):
    return jnp.maximum(x, 0.0) + jnp.log(1.0 + jnp.exp(-jnp.abs(x)))


def _gelu_tanh(x):
    return 0.5 * x * (1.0 + jnp.tanh(math.sqrt(2.0 / math.pi) * (x + 0.044715 * (x * x * x))))


def _rms_fn(x, g):
    return x * lax.rsqrt(jnp.mean(x * x, axis=-1, keepdims=True) + EPS) * g


def _rms_bwd(x, g, dy):
    r = lax.rsqrt(jnp.mean(x * x, axis=-1, keepdims=True) + EPS)
    xh = x * r
    dxh = dy * g
    dx = r * (dxh - xh * jnp.mean(dxh * xh, axis=-1, keepdims=True))
    return dx, jnp.sum(dy * xh, axis=0, keepdims=True)


def rms(x, g, out_dtype):
    def bwd_fn(rv, pv, dos):
        dx, dg = _rms_bwd(rv[0], pv[0], dos[0])
        return (dx,), (dg,)

    return rowwise(lambda x, g: (_rms_fn(x, g),), "rms", [x], [g.reshape(1, -1)], [out_dtype], bwd_fn=bwd_fn)[0]


def rms_residual_norm(m, g, x, g_next):
    def fn(m, x, g, gn):
        xn = x + _rms_fn(m, g)
        return xn, _rms_fn(xn, gn)

    def bwd_fn(rv, pv, dos):
        (m, x), (g, gn), (dxn, dhn) = rv, pv, dos
        xn = x + _rms_fn(m, g)
        d_from_norm, dgn = _rms_bwd(xn, gn, dhn)
        dxn = dxn + d_from_norm
        dm, dg = _rms_bwd(m, g, dxn)
        return (dm, dxn), (dg, dgn)

    return rowwise(fn, "rms_res_norm", [m, x], [g.reshape(1, -1), g_next.reshape(1, -1)], [f32, MXU_DTYPE], bwd_fn=bwd_fn)


def rms_residual(m, g, x):
    def bwd_fn(rv, pv, dos):
        dm, dg = _rms_bwd(rv[0], pv[0], dos[0])
        return (dm, dos[0]), (dg,)

    return rowwise(lambda m, x, g: (x + _rms_fn(m, g),), "rms_res", [m, x], [g.reshape(1, -1)], [f32], bwd_fn=bwd_fn)[0]


def loss_op(y, tgt):
    S, D = y.shape
    tm = _row_tile(S, 4 * D * 4)

    def call_fwd(y, tgt):
        def body(y_ref, t_ref, o_ref):
            @pl.when(pl.program_id(0) == 0)
            def _():
                o_ref[...] = jnp.zeros_like(o_ref)

            e = y_ref[...] - t_ref[...]
            o_ref[...] += 0.5 * jnp.sum(jnp.mean(e * e, axis=-1, keepdims=True))

        out = pl.pallas_call(
            body,
            out_shape=jax.ShapeDtypeStruct((8, LANES), f32),
            grid=(S // tm,),
            in_specs=[pl.BlockSpec((tm, D), lambda i: (i, 0))] * 2,
            out_specs=pl.BlockSpec((8, LANES), lambda i: (0, 0)),
            compiler_params=_cparams(("arbitrary",)),
            name="loss_fwd",
        )(y, tgt)
        return out[0, 0]

    def call_bwd(y, tgt, g):
        def body(y_ref, t_ref, g_ref, o_ref):
            o_ref[...] = (y_ref[...] - t_ref[...]) * (g_ref[...] * (1.0 / D))

        return pl.pallas_call(
            body,
            out_shape=jax.ShapeDtypeStruct((S, D), f32),
            grid=(S // tm,),
            in_specs=[pl.BlockSpec((tm, D), lambda i: (i, 0))] * 2 + [pl.BlockSpec((1, 1), lambda i: (0, 0))],
            out_specs=pl.BlockSpec((tm, D), lambda i: (i, 0)),
            compiler_params=_cparams(("parallel",)),
            name="loss_bwd",
        )(y, tgt, g.reshape(1, 1).astype(f32))

    @jax.custom_vjp
    def op(y, tgt):
        return call_fwd(y, tgt)

    def fwd(y, tgt):
        return call_fwd(y, tgt), (y, tgt)

    def bwd(res, g):
        y, tgt = res
        return call_bwd(y, tgt, g), jnp.zeros_like(tgt)

    op.defvjp(fwd, bwd)
    return op(y, tgt)


HALO = 8


def _conv_tile(S, R):
    def ext(ref, r0):
        cur = ref[pl.ds(r0, R), :]
        prev = ref[pl.ds(pl.multiple_of(jnp.maximum(r0 - HALO, 0), HALO), HALO), :]
        nxt = ref[pl.ds(pl.multiple_of(jnp.minimum(r0 + R, S - HALO), HALO), HALO), :]
        prev = jnp.where(r0 > 0, prev, 0.0)
        nxt = jnp.where(r0 + R < S, nxt, 0.0)
        return jnp.concatenate([prev, cur, nxt], axis=0)

    return ext


def _shift_rows(e, k, R):
    n = e.shape[0]
    if k == 0:
        return e[HALO:HALO + R]
    return pltpu.roll(e, (-k) % n, 0)[HALO:HALO + R]


def _silu_bwd(us, dy):
    u, = us
    s = 1.0 / (1.0 + jnp.exp(-u))
    return (dy * (s * (1.0 + u * (1.0 - s))),)


def _geglu(g, v):
    return _gelu_tanh(g) * v


def _geglu_bwd(us, dy):
    g, v = us
    c = math.sqrt(2.0 / math.pi)
    t = jnp.tanh(c * (g + 0.044715 * (g * g * g)))
    half = 0.5 * (1.0 + t)
    dgelu = half + 0.5 * g * (1.0 - t * t) * (c * (1.0 + 3.0 * 0.044715 * (g * g)))
    return dy * v * dgelu, dy * (g * half)


def mm_conv_act(h, ws, cws, cbs, act, act_bwd, out_dtype, name):
    n = len(ws)
    S = h.shape[0]
    C = ws[0].shape[1]
    W = cws[0].shape[0]
    pad = W // 2
    bw = _pick(C, (LANES,))
    R = _pick(S, (256, 128, 64, 32, 16, 8))
    nt = S // R
    ext = _conv_tile(S, R)
    col = lambda rows: pl.BlockSpec((rows, bw), lambda j: (0, j))

    def conv(e, wv, bv):
        acc = bv + wv[pad] * e[HALO:HALO + R]
        for j in range(W):
            if j != pad:
                acc = acc + wv[j] * _shift_rows(e, j - pad, R)
        return acc

    def call_fwd(xs, cws, cbs):
        def body(*refs):
            x_refs, w_refs, b_refs, y_ref = refs[:n], refs[n:2 * n], refs[2 * n:3 * n], refs[3 * n]
            wvs = [[w[j:j + 1, :] for j in range(W)] for w in w_refs]
            bvs = [b[...] for b in b_refs]

            def tile(i, c):
                r0 = pl.multiple_of(i * R, R)
                us = [conv(ext(x, r0), wv, bv) for x, wv, bv in zip(x_refs, wvs, bvs)]
                y_ref[pl.ds(r0, R), :] = act(*us).astype(y_ref.dtype)
                return c

            lax.fori_loop(0, nt, tile, 0)

        return pl.pallas_call(
            body,
            out_shape=jax.ShapeDtypeStruct((S, C), out_dtype),
            grid=(C // bw,),
            in_specs=[col(S)] * n + [col(W)] * n + [col(1)] * n,
            out_specs=col(S),
            compiler_params=_cparams(("parallel",)),
            name=name + "_fwd",
        )(*xs, *cws, *cbs)

    def call_bwd(xs, cws, cbs, dy):
        def body(*refs):
            x_refs, w_refs, b_refs, dy_ref = refs[:n], refs[n:2 * n], refs[2 * n:3 * n], refs[3 * n]
            dx_refs, dw_refs, db_refs = refs[3 * n + 1:4 * n + 1], refs[4 * n + 1:5 * n + 1], refs[5 * n + 1:6 * n + 1]
            du_scr = refs[6 * n + 1:]
            wvs = [[w[j:j + 1, :] for j in range(W)] for w in w_refs]
            bvs = [b[...] for b in b_refs]
            zero = jnp.zeros((1, bw), f32)

            def first(i, dbs):
                r0 = pl.multiple_of(i * R, R)
                us = [conv(ext(x, r0), wv, bv) for x, wv, bv in zip(x_refs, wvs, bvs)]
                dus = act_bwd(us, dy_ref[pl.ds(r0, R), :].astype(f32))
                for scr, du in zip(du_scr, dus):
                    scr[pl.ds(r0, R), :] = du
                return tuple(db + jnp.sum(du, axis=0, keepdims=True) for db, du in zip(dbs, dus))

            dbs = lax.fori_loop(0, nt, first, tuple(zero for _ in range(n)))

            def second(i, dws):
                r0 = pl.multiple_of(i * R, R)
                new = []
                for x, scr, dx, wv, dw in zip(x_refs, du_scr, dx_refs, wvs, dws):
                    ex, ed = ext(x, r0), ext(scr, r0)
                    d0 = ed[HALO:HALO + R]
                    acc = jnp.zeros((R, bw), f32)
                    row = []
                    for j in range(W):
                        acc = acc + wv[j] * _shift_rows(ed, pad - j, R)
                        row.append(dw[j] + jnp.sum(d0 * _shift_rows(ex, j - pad, R), axis=0, keepdims=True))
                    dx[pl.ds(r0, R), :] = acc.astype(dx.dtype)
                    new.append(tuple(row))
                return tuple(new)

            dws = lax.fori_loop(0, nt, second, tuple(tuple(zero for _ in range(W)) for _ in range(n)))
            for dw_ref, db_ref, dw, db in zip(dw_refs, db_refs, dws, dbs):
                dw_ref[...] = jnp.zeros_like(dw_ref)
                for j in range(W):
                    dw_ref[j:j + 1, :] = dw[j]
                db_ref[...] = db

        return pl.pallas_call(
            body,
            out_shape=[jax.ShapeDtypeStruct((S, C), MXU_DTYPE)] * n + [jax.ShapeDtypeStruct((8, C), f32)] * n
            + [jax.ShapeDtypeStruct((1, C), f32)] * n,
            grid=(C // bw,),
            in_specs=[col(S)] * n + [col(W)] * n + [col(1)] * n + [col(S)],
            out_specs=[col(S)] * n + [col(8)] * n + [col(1)] * n,
            scratch_shapes=[pltpu.VMEM((S, bw), f32)] * n,
            compiler_params=_cparams(("parallel",)),
            name=name + "_bwd",
        )(*xs, *cws, *cbs, dy)

    @jax.custom_vjp
    def op(h, ws, cws, cbs):
        return call_fwd([_mm_call(h, w, "nn", f32) for w in ws], cws, cbs)

    def fwd(h, ws, cws, cbs):
        xs = [_mm_call(h, w, "nn", f32) for w in ws]
        return call_fwd(xs, cws, cbs), (h, ws, xs, cws, cbs)

    def bwd(res, dy):
        h, ws, xs, cws, cbs = res
        out = call_bwd(xs, cws, cbs, dy)
        dxs, dcws, dcbs = out[:n], out[n:2 * n], out[2 * n:]
        dh = _mm_call(dxs[0], ws[0], "nt", h.dtype)
        for dx, w in zip(dxs[1:], ws[1:]):
            dh = dh + _mm_call(dx, w, "nt", h.dtype)
        dws = tuple(_mm_call(h, dx, "tn", w.dtype) for dx, w in zip(dxs, ws))
        return dh, dws, tuple(d[:W] for d in dcws), tuple(dcbs)

    op.defvjp(fwd, bwd)
    return op(h, tuple(ws), tuple(cws), tuple(b.reshape(1, C) for b in cbs))


@jax.custom_vjp
def _masked_decay(cs_col, cs_row, mask01):
    return jnp.where(mask01 > 0, jnp.exp(cs_col - cs_row), 0.0)


def _masked_decay_fwd(cs_col, cs_row, mask01):
    d = jnp.where(mask01 > 0, jnp.exp(cs_col - cs_row), 0.0)
    return d, (d, mask01)


def _masked_decay_bwd(res, g):
    d, mask01 = res
    t = g * d
    return jnp.sum(t, axis=1, keepdims=True), -jnp.sum(t, axis=0, keepdims=True), jnp.zeros_like(mask01)


_masked_decay.defvjp(_masked_decay_fwd, _masked_decay_bwd)


def _scan_chunk(qs, ks, xs, cs_tok, dt_tok, hs, *, rev, incl, nsub):
    nb = len(xs)
    L, N = qs[0].shape
    W = xs[0].shape[1]
    Hg = cs_tok.shape[1]
    nh = W // HEAD_W
    shared = len(qs) == 1
    t = lax.broadcasted_iota(jnp.int32, (L, L), 0)
    l = lax.broadcasted_iota(jnp.int32, (L, L), 1)
    if rev:
        mask = (l >= t) if incl else (l > t)
    else:
        mask = (l <= t) if incl else (l < t)
    mask01 = mask.astype(f32)
    lane_a = lax.broadcasted_iota(jnp.int32, cs_tok.shape, 1)
    row_a = lax.broadcasted_iota(jnp.int32, (Hg, L), 0)
    last = lax.broadcasted_iota(jnp.int32, (1, L), 1) == (0 if rev else L - 1)
    vhead = lax.broadcasted_iota(jnp.int32, (1, W), 1) // HEAD_W
    qhead = lax.broadcasted_iota(jnp.int32, (1, N), 1) // (N // nsub)
    cs_rows = lax.dot_general(cs_tok, (t == l).astype(f32), _DIMS["tn"], precision=lax.Precision.HIGHEST,
                              preferred_element_type=f32)

    def by_head(vals):
        if len(vals) == 2:
            return jnp.where(vhead == 0, vals[0], vals[1])
        return sum(jnp.where(vhead == i, v, 0.0) for i, v in enumerate(vals))

    decay, lam_e, tau_e, gam_e, dt_e = [], [], [], [], []
    for b in range(nb):
        cs_cols, tots, dt_cols = [], [], []
        for i in range(nh):
            head = b * nh + i
            cs_col = jnp.sum(jnp.where(lane_a == head, cs_tok, 0.0), axis=1, keepdims=True)
            cs_row = jnp.sum(jnp.where(row_a == head, cs_rows, 0.0), axis=0, keepdims=True)
            tots.append(jnp.sum(jnp.where(last, cs_row, 0.0), axis=1, keepdims=True))
            decay.append(_masked_decay(cs_col, cs_row, mask01))
            cs_cols.append(cs_col)
            if dt_tok is not None:
                dt_cols.append(jnp.sum(jnp.where(lane_a == head, dt_tok, 0.0), axis=1, keepdims=True))
        cs_e, tot_e = by_head(cs_cols), by_head(tots)
        lam_e.append(jnp.exp(cs_e))
        tau_e.append(jnp.exp(tot_e - cs_e))
        gam_e.append(jnp.exp(tot_e))
        if dt_tok is not None:
            dt_e.append(by_head(dt_cols))
    vs = [x if dt_tok is None else x * dt_e[b] for b, x in enumerate(xs)]
    qk = lambda b: (qs[0], ks[0]) if shared else (qs[b], ks[b])
    if nsub == 1:
        scores = [bdot(qs[0], ks[0], "nt")] if shared else [bdot(*qk(b), "nt") for b in range(nb)]
        score = lambda b, i: scores[0 if shared else b]
    else:
        scores = [[bdot(jnp.where(qhead == i, qk(b)[0], 0.0), qk(b)[1], "nt") for i in range(nh)] for b in range(nb)]
        score = lambda b, i: scores[b][i]
    ys = [lam_e[b] * bdot(qk(b)[0], hs[b], "nn")
          + by_head([bdot(score(b, i) * decay[b * nh + i], vs[b], "nn") for i in range(nh)]) for b in range(nb)]
    hns = [gam_e[b] * hs[b] + bdot(qk(b)[1], tau_e[b] * vs[b], "tn") for b in range(nb)]
    if nsub > 1:
        nhead = lax.broadcasted_iota(jnp.int32, (N, W), 0) // (N // nsub)
        keep = nhead == lax.broadcasted_iota(jnp.int32, (N, W), 1) // HEAD_W
        hns = [jnp.where(keep, hn, 0.0) for hn in hns]
    return ys, hns


def chunk_cumsum(a_tok, rev):
    G, S, Hg = a_tok.shape
    L = CHUNK
    CB = _pick(S // L, (16, 8, 4, 2))

    def call(a, rev):
        def body(a_ref, o_ref):
            t = lax.broadcasted_iota(jnp.int32, (L, L), 0)
            l = lax.broadcasted_iota(jnp.int32, (L, L), 1)
            tri = ((l >= t) if rev else (l <= t)).astype(f32)
            for j in range(CB):
                o_ref[0, j * L:(j + 1) * L, :] = _exact_dot(tri, a_ref[0, j * L:(j + 1) * L, :])

        spec = pl.BlockSpec((1, CB * L, Hg), lambda g, c: (g, c, 0))
        return pl.pallas_call(
            body,
            out_shape=jax.ShapeDtypeStruct((G, S, Hg), f32),
            grid=(G, S // (CB * L)),
            in_specs=[spec],
            out_specs=spec,
            compiler_params=_cparams(("parallel", "parallel")),
            name="chunk_cumsum",
        )(a)

    @jax.custom_vjp
    def op(a):
        return call(a, rev)

    def fwd(a):
        return call(a, rev), None

    def bwd(_, g):
        return (call(g, not rev),)

    op.defvjp(fwd, bwd)
    return op(a_tok)


def scan_op(q, k, x, a_tok, dt_tok, *, rev, incl, nsub):
    S = q.shape[0]
    G, _, Hg = a_tok.shape
    N = q.shape[1] // G
    Vw = x.shape[1] // G
    L = CHUNK
    nc = S // L
    use_dt = dt_tok is not None
    PW = min(Vw, LANES)
    chunk = functools.partial(_scan_chunk, rev=rev, incl=incl, nsub=nsub)
    cols = [slice(p * PW, (p + 1) * PW) for p in range(Vw // PW)]
    own_qk = nsub > 1
    NB = PW if own_qk else N

    CPS = SCAN_CHUNKS_PER_STEP if nc % SCAN_CHUNKS_PER_STEP == 0 else 1
    ns = nc // CPS

    def order(c, backward):
        return (ns - 1 - c) if (rev != backward) else c

    def visit(backward):
        js = range(CPS) if rev == backward else range(CPS - 1, -1, -1)
        return [(j, slice(j * L, (j + 1) * L)) for j in js]

    def specs(backward):
        qs = pl.BlockSpec((CPS * L, N), lambda g, c: (order(c, backward), g))
        xs = pl.BlockSpec((CPS * L, Vw), lambda g, c: (order(c, backward), g))
        as_ = pl.BlockSpec((1, CPS * L, Hg), lambda g, c: (g, order(c, backward), 0))
        hs = pl.BlockSpec((1, CPS, NB, Vw), lambda g, c: (g, order(c, backward), 0, 0))
        return qs, xs, as_, hs

    def call_fwd(q, k, x, a_tok, dt_tok, y_prev=None):
        qs, xs, as_, hs = specs(False)
        n_in = 4 + use_dt + (y_prev is not None)

        def body(*refs):
            q_ref, k_ref, x_ref, a_ref = refs[:4]
            dt_ref = refs[4] if use_dt else None
            yp_ref = refs[n_in - 1] if y_prev is not None else None
            y_ref, hs_ref, h_scr = refs[n_in:]

            @pl.when(pl.program_id(1) == 0)
            def _():
                h_scr[...] = jnp.zeros_like(h_scr)

            for j, rs in visit(False):
                hs_ref[0, j] = h_scr[...]
                q, k, a, dt = q_ref[rs, :], k_ref[rs, :], a_ref[0, rs, :], dt_ref[0, rs, :] if use_dt else None
                qs, ks = ([q[:, c] for c in cols], [k[:, c] for c in cols]) if own_qk else ([q], [k])
                ys, hns = chunk(qs, ks, [x_ref[rs, c] for c in cols], a, dt, [h_scr[:, c] for c in cols])
                for c, y, hn in zip(cols, ys, hns):
                    y_ref[rs, c] = y if yp_ref is None else y + yp_ref[rs, c]
                    h_scr[:, c] = hn

        ins = [q, k, x, a_tok] + ([dt_tok] if use_dt else []) + ([y_prev] if y_prev is not None else [])
        return pl.pallas_call(
            body,
            out_shape=[jax.ShapeDtypeStruct((S, G * Vw), f32), jax.ShapeDtypeStruct((G, nc, NB, Vw), f32)],
            grid=(G, ns),
            in_specs=[qs, qs, xs, as_] + ([as_] if use_dt else []) + ([xs] if y_prev is not None else []),
            out_specs=[xs, hs],
            scratch_shapes=[pltpu.VMEM((NB, Vw), f32)],
            compiler_params=_cparams(("parallel", "arbitrary")),
            name="scan_fwd",
        )(*ins)

    def call_bwd(q, k, x, a_tok, dt_tok, hsave, dy, acc=None):
        qs, xs, as_, hs = specs(True)
        n_in = 6 + use_dt + (3 if acc is not None else 0)

        def body(*refs):
            q_ref, k_ref, x_ref, a_ref = refs[:4]
            dt_ref = refs[4] if use_dt else None
            hs_ref, dy_ref = refs[4 + use_dt], refs[5 + use_dt]
            acc_refs = refs[n_in - 3:n_in] if acc is not None else None
            dq_ref, dk_ref, dx_ref, da_ref = refs[n_in:n_in + 4]
            ddt_ref = refs[n_in + 4] if use_dt else None
            dh_scr = refs[-1]

            @pl.when(pl.program_id(1) == 0)
            def _():
                dh_scr[...] = jnp.zeros_like(dh_scr)

            for j, rs in visit(True):
                q, k, a = q_ref[rs, :].astype(f32), k_ref[rs, :].astype(f32), a_ref[0, rs, :]
                qs, ks = ([q[:, c] for c in cols], [k[:, c] for c in cols]) if own_qk else ([q], [k])
                xs, hs_in = [x_ref[rs, c] for c in cols], [hs_ref[0, j, :, c] for c in cols]
                if use_dt:
                    _, vjp = jax.vjp(chunk, qs, ks, xs, a, dt_ref[0, rs, :], hs_in)
                else:
                    _, vjp = jax.vjp(lambda qs, ks, xs, a, hs: chunk(qs, ks, xs, a, None, hs), qs, ks, xs, a, hs_in)
                cts = vjp(([dy_ref[rs, c] for c in cols], [dh_scr[:, c] for c in cols]))
                dqs, dks, dxs, da, dhs = cts[0], cts[1], cts[2], cts[3], cts[-1]
                if acc is not None:
                    dq_acc, dk_acc = acc_refs[0][rs, :].astype(f32), acc_refs[1][rs, :].astype(f32)
                if own_qk:
                    for b, c in enumerate(cols):
                        dq_ref[rs, c] = (dqs[b] if acc is None else dqs[b] + dq_acc[:, c]).astype(dq_ref.dtype)
                        dk_ref[rs, c] = (dks[b] if acc is None else dks[b] + dk_acc[:, c]).astype(dk_ref.dtype)
                else:
                    dq_ref[rs, :] = (dqs[0] if acc is None else dqs[0] + dq_acc).astype(dq_ref.dtype)
                    dk_ref[rs, :] = (dks[0] if acc is None else dks[0] + dk_acc).astype(dk_ref.dtype)
                for b, c in enumerate(cols):
                    dx_ref[rs, c] = dxs[b] if acc is None else dxs[b] + acc_refs[2][rs, c]
                    dh_scr[:, c] = dhs[b]
                da_ref[0, rs, :] = da
                if use_dt:
                    ddt_ref[0, rs, :] = cts[4]

        ins = [q, k, x, a_tok] + ([dt_tok] if use_dt else []) + [hsave, dy] + (list(acc) if acc is not None else [])
        a_shape = jax.ShapeDtypeStruct(a_tok.shape, f32)
        return pl.pallas_call(
            body,
            out_shape=[jax.ShapeDtypeStruct(q.shape, q.dtype), jax.ShapeDtypeStruct(k.shape, k.dtype),
                       jax.ShapeDtypeStruct(x.shape, f32), a_shape] + ([a_shape] if use_dt else []),
            grid=(G, ns),
            in_specs=[qs, qs, xs, as_] + ([as_] if use_dt else []) + [hs, xs] + ([qs, qs, xs] if acc is not None else []),
            out_specs=[qs, qs, xs, as_] + ([as_] if use_dt else []),
            scratch_shapes=[pltpu.VMEM((NB, Vw), f32)],
            compiler_params=_cparams(("parallel", "arbitrary")),
            name="scan_bwd",
        )(*ins)

    return call_fwd, call_bwd


def bidir_scan(q, k, x, a_f, a_b, dt_f, dt_b, *, nsub):
    use_dt = dt_f is not None
    a_f, a_b = chunk_cumsum(a_f, False), chunk_cumsum(a_b, True)
    fwd_f, bwd_f = scan_op(q, k, x, a_f, dt_f, rev=False, incl=True, nsub=nsub)
    fwd_b, bwd_b = scan_op(q, k, x, a_b, dt_b, rev=True, incl=False, nsub=nsub)

    def run(q, k, x, a_f, a_b, dt_f, dt_b):
        y_f, hs_f = fwd_f(q, k, x, a_f, dt_f)
        y, hs_b = fwd_b(q, k, x, a_b, dt_b, y_prev=y_f)
        return y, (hs_f, hs_b)

    def grads(q, k, x, a_f, a_b, dt_f, dt_b, hs, dy):
        first = bwd_f(q, k, x, a_f, dt_f, hs[0], dy)
        both = bwd_b(q, k, x, a_b, dt_b, hs[1], dy, acc=first[:3])
        return both[0], both[1], both[2], first[3], both[3], (first[4] if use_dt else None), (both[4] if use_dt else None)

    if use_dt:
        @jax.custom_vjp
        def op(q, k, x, a_f, a_b, dt_f, dt_b):
            return run(q, k, x, a_f, a_b, dt_f, dt_b)[0]

        def fwd(q, k, x, a_f, a_b, dt_f, dt_b):
            y, hs = run(q, k, x, a_f, a_b, dt_f, dt_b)
            return y, (q, k, x, a_f, a_b, dt_f, dt_b, hs)

        def bwd(res, dy):
            return grads(*res, dy)

        op.defvjp(fwd, bwd)
        return op(q, k, x, a_f, a_b, dt_f, dt_b)

    @jax.custom_vjp
    def op(q, k, x, a_f, a_b):
        return run(q, k, x, a_f, a_b, None, None)[0]

    def fwd(q, k, x, a_f, a_b):
        y, hs = run(q, k, x, a_f, a_b, None, None)
        return y, (q, k, x, a_f, a_b, hs)

    def bwd(res, dy):
        q, k, x, a_f, a_b, hs = res
        return grads(q, k, x, a_f, a_b, None, None, hs, dy)[:5]

    op.defvjp(fwd, bwd)
    return op(q, k, x, a_f, a_b)


def _swap_halves(x, dh):
    W = x.shape[1]
    lane = lax.broadcasted_iota(jnp.int32, (1, W), 1) % dh
    return jnp.where(lane < dh // 2, pltpu.roll(x, W - dh // 2, 1), pltpu.roll(x, dh // 2, 1))


def rotary(rq, rk, cos_t, sin_t):
    scale = RET_DH ** -0.5

    def fn(rq, rk, c, s):
        return rq * c + _swap_halves(rq, RET_DH) * s, (rk * c + _swap_halves(rk, RET_DH) * s) * scale

    def bwd_fn(rv, pv, dos):
        _, _, c, s = rv
        dq, dk = dos
        dk = dk * scale
        return (dq * c + _swap_halves(dq * s, RET_DH), dk * c + _swap_halves(dk * s, RET_DH)), ()

    return rowwise(fn, "rotary", [rq, rk, cos_t, sin_t], [], [MXU_DTYPE, MXU_DTYPE], n_diff_rows=2, bwd_fn=bwd_fn)


def _rope_tables(S, width):
    half = RET_DH // 2
    inv = 1.0 / (ROPE_BASE ** (jnp.arange(half, dtype=f32) / half))
    ang = jnp.arange(S, dtype=f32)[:, None] * inv[None, :]
    cos, sin = jnp.cos(ang), jnp.sin(ang)
    reps = width // RET_DH
    return jnp.tile(jnp.concatenate([cos, cos], axis=1), (1, reps)), jnp.tile(jnp.concatenate([-sin, sin], axis=1), (1, reps))


def _exact_dot(x, m):
    return jnp.dot(x, m, precision=lax.Precision.HIGHEST, preferred_element_type=f32)


def ret_post(y, rg, gn_g):
    W = y.shape[1]
    idx = np.arange(W) // RET_DH
    avg = jnp.asarray((idx[:, None] == idx[None, :]).astype(np.float32) / RET_DH)

    def fn(y, rg, g, avg):
        mu = _exact_dot(y, avg)
        d = y - mu
        var = _exact_dot(d * d, avg)
        return (_silu(rg) * (d * lax.rsqrt(var + EPS) * g),)

    return rowwise(fn, "ret_post", [y, rg], [gn_g.reshape(1, -1), avg], [MXU_DTYPE], n_diff_params=1)[0]


def _na_bias(rpb, win_r):
    H = rpb.shape[0]
    qc = np.arange(GRID_W)[:, None]
    kc = np.arange(GRID_W)[None, :]
    cstart = np.clip(qc - NA_WIN_C // 2, 0, GRID_W - NA_WIN_C)
    valid = (kc >= cstart) & (kc < cstart + NA_WIN_C)
    dc = np.clip(kc - qc, -(NA_WIN_C - 1), NA_WIN_C - 1) + (NA_WIN_C - 1)
    onehot = (dc[None] == np.arange(2 * NA_WIN_C - 1)[:, None, None]).astype(np.float32)
    t1 = jnp.einsum("hrd,dqk->hrqk", rpb.astype(f32), jnp.asarray(onehot), precision=lax.Precision.HIGHEST)
    per_delta = [t1[:, NA_WIN_R - 1 - d:NA_WIN_R - 1 - d + win_r] for d in range(win_r)]
    b = jnp.stack(per_delta, axis=1)
    b = jnp.where(jnp.asarray(valid)[None, None, None], b, NEG_INF)
    return jnp.transpose(b, (0, 1, 3, 2, 4)).reshape(H, win_r, GRID_W, win_r * GRID_W)


def _na_rows(rows):
    lane = lax.broadcasted_iota(jnp.int32, (1, rows[0][0].shape[1]), 1) // NA_DH
    scale = NA_DH ** -0.5
    ss = [[_bdot_raw(jnp.where(lane == i, q, 0.0) * scale, kw, "nt") + b for i, b in enumerate(bs)] for q, kw, _, bs in rows]
    es = [[jnp.exp(s - jnp.max(s, axis=1, keepdims=True)) for s in srow] for srow in ss]
    ps = [[e / jnp.sum(e, axis=1, keepdims=True) for e in erow] for erow in es]
    return [_lanes_by_head(lane, [_bdot_raw(p, vw, "nn") for p in prow]) for prow, (_, _, vw, _) in zip(ps, rows)]


def _lanes_by_head(lane, vals):
    if len(vals) == 2:
        return jnp.where(lane == 0, vals[0], vals[1])
    return sum(jnp.where(lane == i, v, 0.0) for i, v in enumerate(vals))


def _na_rows_bwd(rows):
    lane = lax.broadcasted_iota(jnp.int32, (1, rows[0][0].shape[1]), 1) // NA_DH
    scale = NA_DH ** -0.5
    heads = range(len(rows[0][3]))
    qis = [[jnp.where(lane == i, q, 0.0) * scale for i in heads] for q, _, _, _, _ in rows]
    dos = [[jnp.where(lane == i, do, 0.0) for i in heads] for _, _, _, _, do in rows]
    ss = [[_bdot_raw(qi, kw, "nt") + b for qi, b in zip(qrow, bs)] for qrow, (_, kw, _, bs, _) in zip(qis, rows)]
    dps = [[_bdot_raw(doi, vw, "nt") for doi in drow] for drow, (_, _, vw, _, _) in zip(dos, rows)]
    es = [[jnp.exp(s - jnp.max(s, axis=1, keepdims=True)) for s in srow] for srow in ss]
    ps = [[e / jnp.sum(e, axis=1, keepdims=True) for e in erow] for erow in es]
    dss = [[p * (dp - jnp.sum(dp * p, axis=1, keepdims=True)) for p, dp in zip(prow, dprow)] for prow, dprow in zip(ps, dps)]
    out = []
    for qrow, drow, prow, dsrow, (_, kw, _, _, _) in zip(qis, dos, ps, dss, rows):
        dq = _lanes_by_head(lane, [_bdot_raw(dsrow[i], kw, "nn") for i in heads]) * scale
        dk, dv = 0.0, 0.0
        for i in heads:
            dk = dk + _bdot_raw(dsrow[i], qrow[i], "tn")
            dv = dv + _bdot_raw(prow[i], drow[i], "tn")
        out.append((dq, dk, dv, dsrow))
    return out


def na_op(nq, nk, nv, bias):
    S, W = nq.shape
    rows = S // GRID_W
    win_r = bias.shape[1]
    nkeys = win_r * GRID_W
    hp = LANES // NA_DH
    npair = W // LANES
    RB = min(16, rows)
    nrb = rows // RB
    qspec = pl.BlockSpec((RB * GRID_W, LANES), lambda p, r: (r, p))
    kspec = pl.BlockSpec((S, LANES), lambda p, r: (0, p))
    bspec = pl.BlockSpec((hp, win_r, GRID_W, nkeys), lambda p, r: (p, 0, 0, 0))

    def window(r):
        r0 = jnp.clip(r - win_r // 2, 0, rows - win_r)
        return pl.multiple_of(r0 * GRID_W, GRID_W), r - r0

    def call_fwd(nq, nk, nv, bias):
        def body(q_ref, k_ref, v_ref, b_ref, o_ref):
            rb = pl.program_id(1)

            def step(j, c):
                args, q0s = [], []
                for u in range(NA_ROWS_PER_STEP):
                    i = j * NA_ROWS_PER_STEP + u
                    k0, d = window(rb * RB + i)
                    q0 = pl.multiple_of(i * GRID_W, GRID_W)
                    q0s.append(q0)
                    args.append((q_ref[pl.ds(q0, GRID_W), :].astype(f32), k_ref[pl.ds(k0, nkeys), :], v_ref[pl.ds(k0, nkeys), :],
                                 [b_ref[h, pl.ds(d, 1)][0] for h in range(hp)]))
                for q0, o in zip(q0s, _na_rows(args)):
                    o_ref[pl.ds(q0, GRID_W), :] = o.astype(o_ref.dtype)
                return c

            lax.fori_loop(0, RB // NA_ROWS_PER_STEP, step, 0)

        return pl.pallas_call(
            body,
            out_shape=jax.ShapeDtypeStruct((S, W), nq.dtype),
            grid=(npair, nrb),
            in_specs=[qspec, kspec, kspec, bspec],
            out_specs=qspec,
            compiler_params=_cparams(("parallel", "arbitrary")),
            name="na_fwd",
        )(nq, nk, nv, bias)

    def call_bwd(nq, nk, nv, bias, do):
        def body(q_ref, k_ref, v_ref, b_ref, do_ref, dq_ref, dk_ref, dv_ref, db_ref, dk_acc, dv_acc):
            rb = pl.program_id(1)

            @pl.when(rb == 0)
            def _():
                dk_acc[...] = jnp.zeros_like(dk_acc)
                dv_acc[...] = jnp.zeros_like(dv_acc)
                db_ref[...] = jnp.zeros_like(db_ref)

            def step(j, c):
                args, spots = [], []
                for u in range(NA_ROWS_PER_STEP):
                    i = j * NA_ROWS_PER_STEP + u
                    k0, d = window(rb * RB + i)
                    q0 = pl.multiple_of(i * GRID_W, GRID_W)
                    spots.append((q0, k0, d))
                    args.append((q_ref[pl.ds(q0, GRID_W), :].astype(f32), k_ref[pl.ds(k0, nkeys), :], v_ref[pl.ds(k0, nkeys), :],
                                 [b_ref[h, pl.ds(d, 1)][0] for h in range(hp)], do_ref[pl.ds(q0, GRID_W), :].astype(f32)))
                for (q0, k0, d), (dq, dk, dv, dbs) in zip(spots, _na_rows_bwd(args)):
                    dq_ref[pl.ds(q0, GRID_W), :] = dq.astype(dq_ref.dtype)
                    dk_acc[pl.ds(k0, nkeys), :] += dk
                    dv_acc[pl.ds(k0, nkeys), :] += dv
                    for h in range(hp):
                        db_ref[h, pl.ds(d, 1)] += dbs[h][None]
                return c

            lax.fori_loop(0, RB // NA_ROWS_PER_STEP, step, 0)

            @pl.when(rb == nrb - 1)
            def _():
                dk_ref[...] = dk_acc[...].astype(dk_ref.dtype)
                dv_ref[...] = dv_acc[...].astype(dv_ref.dtype)

        return pl.pallas_call(
            body,
            out_shape=[jax.ShapeDtypeStruct((S, W), nq.dtype), jax.ShapeDtypeStruct((S, W), nk.dtype),
                       jax.ShapeDtypeStruct((S, W), nv.dtype), jax.ShapeDtypeStruct(bias.shape, f32)],
            grid=(npair, nrb),
            in_specs=[qspec, kspec, kspec, bspec, qspec],
            out_specs=[qspec, kspec, kspec, bspec],
            scratch_shapes=[pltpu.VMEM((S, LANES), f32), pltpu.VMEM((S, LANES), f32)],
            compiler_params=_cparams(("parallel", "arbitrary")),
            name="na_bwd",
        )(nq, nk, nv, bias, do)

    @jax.custom_vjp
    def op(nq, nk, nv, bias):
        return call_fwd(nq, nk, nv, bias)

    def fwd(nq, nk, nv, bias):
        return call_fwd(nq, nk, nv, bias), (nq, nk, nv, bias)

    def bwd(res, do):
        return tuple(call_bwd(*res, do))

    op.defvjp(fwd, bwd)
    return op(nq, nk, nv, bias)


def ssd_dt(dt_raw, dt_bias, a_neg):
    def fn(r, b, a):
        dt = _softplus(r + b)
        return dt, dt * a

    return rowwise(fn, "ssd_dt", [dt_raw], [dt_bias, a_neg], [f32, f32])


def ssd_post(y, xs, z, d_skip_lanes, norm_g, groups):
    def fn(y, xs, z, dsk, g):
        y = (y + xs * dsk) * _silu(z)
        return (y * lax.rsqrt(jnp.mean(y * y, axis=-1, keepdims=True) + EPS) * g,)

    def bwd_fn(rv, pv, dos):
        (y, xs, z), (dsk, g), (do,) = rv, pv, dos
        sg = 1.0 / (1.0 + jnp.exp(-z))
        s, u = z * sg, y + xs * dsk
        dw, dg = _rms_bwd(u * s, g, do)
        du = dw * s
        dz = dw * u * (sg * (1.0 + z * (1.0 - sg)))
        return (du, du * dsk, dz), (jnp.sum(du * xs, axis=0, keepdims=True), dg)

    return rowwise(fn, "ssd_post", [y, xs, z], [d_skip_lanes.reshape(1, -1), norm_g.reshape(1, -1)], [MXU_DTYPE],
                   ncol=groups, bwd_fn=bwd_fn)[0]


def _heads_major(t, groups):
    S = t.shape[0]
    return jnp.transpose(t.reshape(S, groups, -1), (1, 0, 2))


def retention_na_mixer(hn, w_in, decay_logit, gn_g, rpb, w_out, tables):
    S = hn.shape[0]
    R = RET_HEADS * RET_DH
    NW = NA_HEADS * NA_DH
    cols = lambda a, b: w_in[:, a:b]
    rq, rk, rv, rg = (mm(hn, cols(j * R, (j + 1) * R)) for j in range(4))
    nq, nk, nv = (mm(hn, cols(4 * R + j * NW, 4 * R + (j + 1) * NW), out_dtype=MXU_DTYPE) for j in range(3))
    qr, kr = rotary(rq, rk, *tables)
    log_gamma = -_softplus(-decay_logit.astype(f32))
    hp = LANES // RET_DH
    hpad = -(-RET_HEADS // 8) * 8
    pad8 = lambda a: jnp.pad(a.reshape(1, 1, RET_HEADS), ((0, 0), (0, 0), (0, hpad - RET_HEADS)))
    a_f = jnp.broadcast_to(pad8(log_gamma[0]), (1, S, hpad))
    a_b = jnp.broadcast_to(pad8(log_gamma[1]), (1, S, hpad))
    ret = ret_post(bidir_scan(qr, kr, rv, a_f, a_b, None, None, nsub=hp), rg, gn_g)
    rows = S // GRID_W
    nao = na_op(nq, nk, nv, _na_bias(rpb, min(NA_WIN_R, rows)))
    return mm_pair(ret, w_out[:R], nao, w_out[R:])


def ssd_mixer(hn, w_in, conv_w, conv_b, dt_bias, a_log, d_skip, norm_g, w_out):
    heads = d_skip.shape[0]
    inner = heads * SSD_HEADDIM
    gs = SSD_GROUPS * SSD_STATE
    o_x, o_b, o_c, o_dt = inner, 2 * inner, 2 * inner + gs, 2 * inner + 2 * gs
    z = mm(hn, w_in[:, :inner])
    dt_raw = mm(hn, w_in[:, o_dt:])
    xs, bm, cm = (mm_conv_act(hn, [w_in[:, a:b]], [conv_w[:, a - inner:b - inner]], [conv_b[a - inner:b - inner]],
                              _silu, _silu_bwd, f32, "conv_silu") for a, b in ((o_x, o_b), (o_b, o_c), (o_c, o_dt)))
    a_neg = -jnp.exp(a_log.astype(f32)).reshape(1, -1)
    dt, la = ssd_dt(dt_raw, dt_bias.astype(f32).reshape(1, -1), a_neg)
    dt_f, dt_b = _heads_major(dt[:, :heads], SSD_GROUPS), _heads_major(dt[:, heads:], SSD_GROUPS)
    la_f, la_b = _heads_major(la[:, :heads], SSD_GROUPS), _heads_major(la[:, heads:], SSD_GROUPS)
    y = bidir_scan(cm, bm, xs, la_f, la_b, dt_f, dt_b, nsub=1)
    y = ssd_post(y, xs, z, jnp.repeat(d_skip.astype(f32), SSD_HEADDIM), norm_g, SSD_GROUPS)
    return mm(y, w_out)


def conv_geglu_ffn(hf, w_up, conv_w, conv_b, w_down):
    F = w_down.shape[0]
    a = mm_conv_act(hf, [w_up[:, :F], w_up[:, F:]], [conv_w[:, :F], conv_w[:, F:]], [conv_b[:F], conv_b[F:]],
                    _geglu, _geglu_bwd, MXU_DTYPE, "conv_geglu")
    return mm(a, w_down)


def model_loss(x, tgt, big, small, rep):
    S = x.shape[0]
    depth = rep["norm_mix_pre"].shape[0]
    tables = _rope_tables(S, RET_HEADS * RET_DH)
    hn = rms(x, rep["norm_mix_pre"][0], MXU_DTYPE)
    for layer in range(depth):
        i = layer // 2
        if layer % 2 == 0:
            m = retention_na_mixer(hn, big["ab_w_in"][i], rep["ab_ret_decay_logit"][i], rep["ab_ret_gn_g"][i],
                                   rep["ab_na_rpb"][i], big["ab_w_out"][i], tables)
        else:
            m = ssd_mixer(hn, big["c_w_in"][i], small["c_conv_w"][i], small["c_conv_b"][i], rep["c_dt_bias"][i],
                          rep["c_a_log"][i], rep["c_d_skip"][i], small["c_norm_g"][i], big["c_w_out"][i])
        x, hf = rms_residual_norm(m, rep["norm_mix_post"][layer], x, rep["norm_ffn_pre"][layer])
        f = conv_geglu_ffn(hf, big["ffn_w_up"][layer], small["ffn_conv_w"][layer], rep["ffn_conv_b"][layer],
                           big["ffn_w_down"][layer])
        if layer + 1 < depth:
            x, hn = rms_residual_norm(f, rep["norm_ffn_post"][layer], x, rep["norm_mix_pre"][layer + 1])
        else:
            x = rms_residual(f, rep["norm_ffn_post"][layer], x)
    return loss_op(x, tgt)


def _mesh_pos():
    return lax.axis_index("x"), lax.axis_index("y"), lax.axis_index("c")


def _any_specs(n):
    return [pl.BlockSpec(memory_space=pl.ANY)] * n


def gather_chips(locals_):
    nbuf = len(locals_)
    CH = COPY_CHUNKS

    def body(*refs):
        x_refs, out_refs, (send_sems, recv_sems) = refs[:nbuf], refs[nbuf:2 * nbuf], refs[2 * nbuf:]
        x, y, c = _mesh_pos()
        my = 2 * x + y
        chips = [(1 - x, y), (x, 1 - y), (1 - x, 1 - y)]
        plans = []
        for a, (x_ref, out_ref) in enumerate(zip(x_refs, out_refs)):
            half = x_ref.shape[0] // 2
            q = half // CH

            def piece(ref, h, j, half=half, q=q):
                return ref.at[pl.ds(pl.multiple_of(h * half + j * q, PACK_ALIGN), q), :]

            def copy(k, src, chip, h, j, to, out_ref=out_ref, piece=piece, base=a * 6 * CH):
                return pltpu.make_async_remote_copy(src_ref=src, dst_ref=piece(out_ref.at[chip], h, j),
                                                    send_sem=send_sems.at[base + k], recv_sem=recv_sems.at[base + k],
                                                    device_id=to, device_id_type=pl.DeviceIdType.MESH)

            plans.append((x_ref, out_ref, piece, copy))

        first, passed = [], []
        for x_ref, out_ref, piece, copy in plans:
            first.append([[copy(k * CH + j, piece(x_ref, c, j), my, c, j, (cx, cy, c)) for j in range(CH)]
                          for k, (cx, cy) in enumerate(chips)])
            passed.append([[copy((3 + k) * CH + j, piece(out_ref.at[2 * cx + cy], c, j), 2 * cx + cy, c, j, (x, y, 1 - c))
                            for j in range(CH)] for k, (cx, cy) in enumerate(chips)])
        for a in range(nbuf):
            for j in range(CH):
                for k in range(3):
                    first[a][k][j].start()
        for a, (x_ref, _, piece, copy) in enumerate(plans):
            for j in range(CH):
                for k, (cx, cy) in enumerate(chips):
                    copy(k * CH + j, piece(x_ref, c, j), 2 * cx + cy, c, j, (cx, cy, c)).wait_recv()
                    passed[a][k][j].start()
        for a, (x_ref, _, piece, copy) in enumerate(plans):
            for j in range(CH):
                for k, (cx, cy) in enumerate(chips):
                    copy((3 + k) * CH + j, piece(x_ref, c, j), 2 * cx + cy, 1 - c, j, (x, y, 1 - c)).wait_recv()
        for a in range(nbuf):
            for k in range(3):
                for cp in first[a][k] + passed[a][k]:
                    cp.wait_send()

    return pl.pallas_call(
        body,
        out_shape=[jax.ShapeDtypeStruct((N_CHIPS,) + l.shape, l.dtype) for l in locals_],
        in_specs=_any_specs(nbuf),
        out_specs=_any_specs(nbuf),
        scratch_shapes=[pltpu.SemaphoreType.DMA((nbuf * 6 * CH,)), pltpu.SemaphoreType.DMA((nbuf * 6 * CH,))],
        name="gather_chips",
    )(*locals_)


def pair_swap(parts):
    nbuf = len(parts)
    n = N_CHIPS
    CH = COPY_CHUNKS

    def body(*refs):
        p_refs, got_refs, (send_sems, recv_sems) = refs[:nbuf], refs[nbuf:2 * nbuf], refs[2 * nbuf:]
        x, y, c = _mesh_pos()
        swap = []
        for a, (p_ref, got_ref) in enumerate(zip(p_refs, got_refs)):
            half = p_ref.shape[1] // 2
            q = half // CH
            for s in range(n):
                for j in range(CH):
                    k = (a * n + s) * CH + j
                    src = p_ref.at[s, pl.ds(pl.multiple_of((1 - c) * half + j * q, PACK_ALIGN), q), :]
                    swap.append(pltpu.make_async_remote_copy(src_ref=src, dst_ref=got_ref.at[s, pl.ds(j * q, q), :],
                                                             send_sem=send_sems.at[k], recv_sem=recv_sems.at[k],
                                                             device_id=(x, y, 1 - c), device_id_type=pl.DeviceIdType.MESH))
        for cp in swap:
            cp.start()
        for cp in swap:
            cp.wait()

    return pl.pallas_call(
        body,
        out_shape=[jax.ShapeDtypeStruct((n, p.shape[1] // 2, p.shape[2]), p.dtype) for p in parts],
        in_specs=_any_specs(nbuf),
        out_specs=_any_specs(nbuf),
        scratch_shapes=[pltpu.SemaphoreType.DMA((nbuf * n * CH,)), pltpu.SemaphoreType.DMA((nbuf * n * CH,))],
        name="pair_swap",
    )(*parts)


def chip_exchange(parts):
    nbuf = len(parts)

    def body(*refs):
        p_refs, out_refs, (send_sems, recv_sems) = refs[:nbuf], refs[nbuf:2 * nbuf], refs[2 * nbuf:]
        x, y, c = _mesh_pos()
        my = 2 * x + y
        chips = [(1 - x, y), (x, 1 - y), (1 - x, 1 - y)]

        def copy(a, k, src_slot, dst_slot, to):
            return pltpu.make_async_remote_copy(src_ref=p_refs[a].at[src_slot], dst_ref=out_refs[a].at[dst_slot],
                                                send_sem=send_sems.at[3 * a + k], recv_sem=recv_sems.at[3 * a + k],
                                                device_id=to, device_id_type=pl.DeviceIdType.MESH)

        sends = [copy(a, k, 2 * cx + cy, my, (cx, cy, c)) for a in range(nbuf) for k, (cx, cy) in enumerate(chips)]
        for cp in sends:
            cp.start()
        for a in range(nbuf):
            for k, (cx, cy) in enumerate(chips):
                copy(a, k, my, 2 * cx + cy, (cx, cy, c)).wait_recv()
        for cp in sends:
            cp.wait_send()

    return pl.pallas_call(
        body,
        out_shape=[jax.ShapeDtypeStruct(p.shape, p.dtype) for p in parts],
        in_specs=_any_specs(nbuf),
        out_specs=_any_specs(nbuf),
        scratch_shapes=[pltpu.SemaphoreType.DMA((3 * nbuf,)), pltpu.SemaphoreType.DMA((3 * nbuf,))],
        name="chip_exchange",
    )(*parts)


def pair_share(mine):
    nbuf = len(mine)
    CH = COPY_CHUNKS

    def body(*refs):
        m_refs, out_refs, (send_sems, recv_sems) = refs[:nbuf], refs[nbuf:2 * nbuf], refs[2 * nbuf:]
        x, y, c = _mesh_pos()
        swap = []
        for a, (m_ref, out_ref) in enumerate(zip(m_refs, out_refs)):
            q = m_ref.shape[0] // CH
            for j in range(CH):
                swap.append(pltpu.make_async_remote_copy(src_ref=m_ref.at[pl.ds(j * q, q), :], dst_ref=out_ref.at[pl.ds(j * q, q), :],
                                                         send_sem=send_sems.at[a * CH + j], recv_sem=recv_sems.at[a * CH + j],
                                                         device_id=(x, y, 1 - c), device_id_type=pl.DeviceIdType.MESH))
        for cp in swap:
            cp.start()
        for cp in swap:
            cp.wait()

    return pl.pallas_call(
        body,
        out_shape=[jax.ShapeDtypeStruct(m.shape, m.dtype) for m in mine],
        in_specs=_any_specs(nbuf),
        out_specs=_any_specs(nbuf),
        scratch_shapes=[pltpu.SemaphoreType.DMA((nbuf * CH,)), pltpu.SemaphoreType.DMA((nbuf * CH,))],
        name="pair_share",
    )(*mine)


def sum_chips(recv, own):
    n, R, Wd = recv.shape
    tr = _pick(R, (512, 256, 128, 64, 32, 16, 8))

    def body(r_ref, p_ref, o_ref):
        my = 2 * lax.axis_index("x") + lax.axis_index("y")
        acc = jnp.zeros((tr, Wd), f32)
        for s in range(n):
            acc = acc + jnp.where(my == s, p_ref[s], r_ref[s]).astype(f32)
        o_ref[...] = acc

    spec = pl.BlockSpec((n, tr, Wd), lambda i: (0, i, 0))
    return pl.pallas_call(
        body,
        out_shape=jax.ShapeDtypeStruct((R, Wd), f32),
        grid=(R // tr,),
        in_specs=[spec, spec],
        out_specs=pl.BlockSpec((tr, Wd), lambda i: (i, 0)),
        compiler_params=_cparams(("parallel",)),
        name="sum_chips",
    )(recv, own)


def add_pair(parts, got):
    n, R, Wd = parts.shape
    half = R // 2
    tr = _pick(half, (512, 256, 128, 64, 32, 16, 8))
    nb = half // tr

    def body(lo_ref, hi_ref, g_ref, o_ref):
        mine = jnp.where(lax.axis_index("c") == 0, lo_ref[...], hi_ref[...])
        o_ref[...] = (mine.astype(f32) + g_ref[...].astype(f32)).astype(o_ref.dtype)

    spec = pl.BlockSpec((1, tr, Wd), lambda s, i: (s, i, 0))
    return pl.pallas_call(
        body,
        out_shape=jax.ShapeDtypeStruct(got.shape, parts.dtype),
        grid=(n, nb),
        in_specs=[spec, pl.BlockSpec((1, tr, Wd), lambda s, i: (s, nb + i, 0)), spec],
        out_specs=spec,
        compiler_params=_cparams(("parallel", "parallel")),
        name="add_pair",
    )(parts, parts, got)


def reduce_scatter(parts):
    chip_sum = [add_pair(p, g) for p, g in zip(parts, pair_swap(parts))]
    mine = [sum_chips(r, s) for r, s in zip(chip_exchange(chip_sum), chip_sum)]
    first = lax.axis_index("c") == 0
    return [jnp.concatenate([jnp.where(first, m, t), jnp.where(first, t, m)], axis=0) for m, t in zip(mine, pair_share(mine))]


def adamw(w, g, m, v):
    shp = w.shape
    if w.size * 4 <= (1 << 20):
        grid, block, imap = (1,), shp, lambda i: (0,) * len(shp)
    else:
        n0, R, C = shp
        tr = _divisor_tile(R, lambda t: t * C * 4 <= (1 << 20), 8)
        grid, block, imap = (n0, R // tr), (1, tr, C), lambda j, i: (j, i, 0)

    def body(w_ref, g_ref, m_ref, v_ref, d_ref, mo_ref, vo_ref):
        g = g_ref[...]
        m = ADAM_B1 * m_ref[...] + (1.0 - ADAM_B1) * g
        v = ADAM_B2 * v_ref[...] + (1.0 - ADAM_B2) * (g * g)
        m_hat = m / (1.0 - ADAM_B1 ** ADAM_STEP)
        v_hat = v / (1.0 - ADAM_B2 ** ADAM_STEP)
        d_ref[...] = -ADAM_LR * (m_hat / (jnp.sqrt(v_hat) + ADAM_EPS) + ADAM_WD * w_ref[...])
        mo_ref[...] = m
        vo_ref[...] = v

    spec = pl.BlockSpec(block, imap)
    return pl.pallas_call(
        body,
        out_shape=[jax.ShapeDtypeStruct(shp, f32)] * 3,
        grid=grid,
        in_specs=[spec] * 4,
        out_specs=[spec] * 3,
        compiler_params=_cparams(("parallel",) * len(grid)),
        name="adamw",
    )(w, g, m, v)


def _pack(arrs, dtype):
    flat = jnp.concatenate([a.astype(dtype).reshape(-1) for a in arrs])
    n = flat.shape[0]
    unit = PACK_W * PACK_ROWS
    padded = -(-n // unit) * unit
    return jnp.pad(flat, (0, padded - n)).reshape(-1, PACK_W)


def _unpack(buf, shapes):
    flat = buf.reshape(-1)
    out, off = [], 0
    for s in shapes:
        n = int(np.prod(s))
        out.append(flat[off:off + n].reshape(s))
        off += n
    return out


BIG = (("ab_w_in", 2), ("ab_w_out", 1), ("c_w_in", 2), ("c_w_out", 1), ("ffn_w_up", 2), ("ffn_w_down", 1))
SMALL = (("c_conv_w", 2), ("c_conv_b", 1), ("c_norm_g", 1), ("ffn_conv_w", 2))
REP = ("norm_mix_pre", "norm_mix_post", "norm_ffn_pre", "norm_ffn_post", "ab_ret_decay_logit", "ab_ret_gn_g", "ab_na_rpb",
       "c_dt_bias", "c_a_log", "c_d_skip", "ffn_conv_b")
WEIGHTS = ("norm_mix_pre", "norm_mix_post", "norm_ffn_pre", "norm_ffn_post", "ab_w_in", "ab_ret_decay_logit", "ab_ret_gn_g",
           "ab_na_rpb", "ab_w_out", "c_w_in", "c_conv_w", "c_conv_b", "c_dt_bias", "c_a_log", "c_d_skip", "c_norm_g", "c_w_out",
           "ffn_w_up", "ffn_conv_w", "ffn_conv_b", "ffn_w_down")


BIG_GROUPS = (("ab_w_in",), ("ab_w_out", "c_w_out", "ffn_w_down"), ("c_w_in",), ("ffn_w_up",))
BIG_AXIS = dict(BIG)
ROW_UNIT = 128


def _rows(arrs, dtype):
    C = arrs[0].shape[-1]
    buf = jnp.concatenate([a.astype(dtype).reshape(-1, C) for a in arrs], axis=0)
    pad = -buf.shape[0] % ROW_UNIT
    return jnp.pad(buf, ((0, pad), (0, 0))) if pad else buf


def _unrows(buf, shapes):
    out, off = [], 0
    for s in shapes:
        n = int(np.prod(s[:-1]))
        out.append(buf[off:off + n].reshape(s))
        off += n
    return out


def _gather_all(w):
    bufs = [_rows([w[n] for n in grp], MXU_DTYPE) for grp in BIG_GROUPS] + [_pack([w[n] for n, _ in SMALL], f32)]
    got = gather_chips(bufs)
    my = 2 * lax.axis_index("x") + lax.axis_index("y")

    def whole(n, ax, dtype, pieces):
        return jnp.concatenate([jnp.where(my == s, w[n].astype(dtype), pieces[s]) for s in range(N_CHIPS)], axis=ax)

    big = {}
    for grp, g in zip(BIG_GROUPS, got):
        per_chip = [_unrows(g[s], [w[n].shape for n in grp]) for s in range(N_CHIPS)]
        for j, n in enumerate(grp):
            big[n] = whole(n, BIG_AXIS[n], MXU_DTYPE, [per_chip[s][j] for s in range(N_CHIPS)])
    per_chip = [_unpack(got[-1][s], [w[n].shape for n, _ in SMALL]) for s in range(N_CHIPS)]
    small = {n: whole(n, ax, f32, [per_chip[s][j] for s in range(N_CHIPS)]) for j, (n, ax) in enumerate(SMALL)}
    return big, small


def _reduce_all(gbig, gsmall, grep, w):
    split = {n: jnp.split(g, N_CHIPS, axis=BIG_AXIS[n]) for n, g in gbig.items()}
    parts = [jnp.stack([_rows([split[n][s] for n in grp], MXU_DTYPE) for s in range(N_CHIPS)]) for grp in BIG_GROUPS]
    ssplit = {n: jnp.split(gsmall[n], N_CHIPS, axis=ax) for n, ax in SMALL}
    parts.append(jnp.stack([_pack([ssplit[n][s] for n, _ in SMALL] + [grep[n] for n in REP], f32) for s in range(N_CHIPS)]))
    res = reduce_scatter(parts)
    grads = {}
    for grp, r in zip(BIG_GROUPS, res):
        grads.update(zip(grp, _unrows(r, [w[n].shape for n in grp])))
    small_names = [n for n, _ in SMALL] + list(REP)
    grads.update(zip(small_names, _unpack(res[-1], [w[n].shape for n in small_names])))
    return grads


def kernel(x, norm_mix_pre, norm_mix_post, norm_ffn_pre, norm_ffn_post, ab_w_in, ab_ret_decay_logit, ab_ret_gn_g, ab_na_rpb, ab_w_out, c_w_in, c_conv_w, c_conv_b, c_dt_bias, c_a_log, c_d_skip, c_norm_g, c_w_out, ffn_w_up, ffn_conv_w, ffn_conv_b, ffn_w_down, loss_target, m_norm_mix_pre, m_norm_mix_post, m_norm_ffn_pre, m_norm_ffn_post, m_ab_w_in, m_ab_ret_decay_logit, m_ab_ret_gn_g, m_ab_na_rpb, m_ab_w_out, m_c_w_in, m_c_conv_w, m_c_conv_b, m_c_dt_bias, m_c_a_log, m_c_d_skip, m_c_norm_g, m_c_w_out, m_ffn_w_up, m_ffn_conv_w, m_ffn_conv_b, m_ffn_w_down, v_norm_mix_pre, v_norm_mix_post, v_norm_ffn_pre, v_norm_ffn_post, v_ab_w_in, v_ab_ret_decay_logit, v_ab_ret_gn_g, v_ab_na_rpb, v_ab_w_out, v_c_w_in, v_c_conv_w, v_c_conv_b, v_c_dt_bias, v_c_a_log, v_c_d_skip, v_c_norm_g, v_c_w_out, v_ffn_w_up, v_ffn_conv_w, v_ffn_conv_b, v_ffn_w_down):
    args = dict(locals())
    w = {n: args[n] for n in WEIGHTS}
    mom = {n: args["m_" + n] for n in WEIGHTS}
    var = {n: args["v_" + n] for n in WEIGHTS}

    big, small = _gather_all(w)
    rep = {n: w[n] for n in REP}

    def loss_fn(xs, big, small, rep):
        return model_loss(xs, loss_target[0], big, small, rep)

    loss, (gx, gbig, gsmall, grep) = jax.value_and_grad(loss_fn, argnums=(0, 1, 2, 3))(x[0], big, small, rep)
    loss = lax.psum(loss, ("x", "y", "c"))

    grads = _reduce_all(gbig, gsmall, grep, w)

    delta, new_m, new_v = {}, {}, {}
    for n in WEIGHTS:
        delta[n], new_m[n], new_v[n] = adamw(w[n], grads[n], mom[n], var[n])

    return (loss, gx[None], *[grads[n] for n in WEIGHTS], *[delta[n] for n in WEIGHTS],
            *[new_m[n] for n in WEIGHTS], *[new_v[n] for n in WEIGHTS])
```
